```python
import jax
import jax.numpy as jnp
from jax import lax
import numpy as np

D_MODEL = 1024
BATCH = 16
SEQ = 256
DEPTH = 1
DEC_BATCH = 2
DEC_SEQ = 4096
PAST_LEN = 512

GRID_W = 64
D_MIX = D_MODEL
D_GMLP = D_MIX // 2
D_MLSTM = D_MIX - D_GMLP
GMLP_GROUPS = 4
GMLP_GW = D_GMLP // GMLP_GROUPS
GMLP_CHUNK = 128
MLSTM_HEADS = 4
MLSTM_HD = D_MLSTM // MLSTM_HEADS
MLSTM_CHUNK = 128
CONV_W = 3
N_DIR = 2
N_GATE_COLS = N_DIR * 2 * MLSTM_HEADS
D_FF = -(-(8 * D_MODEL) // (3 * 256)) * 256

OFF_U = 0
OFF_V = OFF_U + D_GMLP
OFF_Q = OFF_V + D_GMLP
OFF_K = OFF_Q + D_MLSTM
OFF_VV = OFF_K + D_MLSTM
OFF_O = OFF_VV + D_MLSTM
OFF_G = OFF_O + D_MLSTM
D_IN = OFF_G + N_GATE_COLS

EPS = 1e-6
NEG = -1e30

kernel_name = 'hymba_gmlp_mlstm_diffusion_step'


def rmsnorm(x, g):
    xf = x.astype(jnp.float32)
    y = xf * lax.rsqrt(jnp.mean(xf * xf, axis=-1, keepdims=True) + EPS)
    return (y * g.astype(jnp.float32)).astype(x.dtype)


def short_conv(x, w, b, rows):
    B, T, C = x.shape
    xs = x if rows is None else x.reshape(B * rows, GRID_W, C)
    n = xs.shape[1]
    pad = CONV_W // 2
    xp = jnp.pad(xs, ((0, 0), (pad, pad), (0, 0)))
    y = b + sum(xp[:, j:j + n] * w[j] for j in range(CONV_W))
    return y.reshape(B, T, C)


def chunk_gmlp(u, v, w_s, b_s, g_v):
    B, T, _ = u.shape
    nc = T // GMLP_CHUNK
    u = jax.nn.gelu(u)
    vg = rmsnorm(jax.nn.gelu(v).reshape(B, T, GMLP_GROUPS, GMLP_GW), g_v)
    vg = vg.reshape(B, nc, GMLP_CHUNK, GMLP_GROUPS, GMLP_GW)
    mixed = jnp.einsum('gts,bcsgd->bctgd', w_s, vg) + jnp.swapaxes(b_s, 0, 1)[:, :, None]
    return u * mixed.reshape(B, T, D_GMLP)


def mlstm_scan(q, k, v, ig, lf, C0, n0, m0):
    B, T, H, HD = q.shape
    L = MLSTM_CHUNK
    nc = T // L

    def to_chunks(a):
        return jnp.swapaxes(a.reshape((B, nc, L) + a.shape[2:]), 0, 1)

    tril = jnp.tril(jnp.ones((L, L), dtype=bool))

    def step(carry, xs):
        C, n, m = carry
        qc, kc, vc, igc, lfc = xs
        b = jnp.swapaxes(jnp.cumsum(lfc, axis=1), 1, 2)
        i_ = jnp.swapaxes(igc, 1, 2)
        logw = jnp.where(tril, b[..., :, None] - b[..., None, :] + i_[..., None, :], NEG)
        a = b + m[..., None]
        mj = jnp.maximum(a, jnp.max(logw, axis=-1))
        w = jnp.exp(logw - mj[..., None])
        inter = jnp.exp(a - mj)
        s = jnp.einsum('bjhd,bshd->bhjs', qc, kc) * w
        num = (jnp.einsum('bhjs,bshd->bhjd', s, vc)
               + inter[..., None] * jnp.einsum('bjhk,bhkv->bhjv', qc, C))
        den = jnp.sum(s, axis=-1) + inter * jnp.einsum('bjhk,bhk->bhj', qc, n)
        h = num / jnp.maximum(jnp.abs(den), jnp.exp(-mj))[..., None]
        bL = b[..., -1]
        g = bL[..., None] - b + i_
        m_new = jnp.maximum(bL + m, jnp.max(g, axis=-1))
        wc = jnp.exp(g - m_new[..., None])
        decay = jnp.exp(bL + m - m_new)
        C_new = decay[..., None, None] * C + jnp.einsum('bhs,bshk,bshv->bhkv', wc, kc, vc)
        n_new = decay[..., None] * n + jnp.einsum('bhs,bshk->bhk', wc, kc)
        return (C_new, n_new, m_new), jnp.swapaxes(h, 1, 2)

    xs = (to_chunks(q), to_chunks(k), to_chunks(v), to_chunks(ig), to_chunks(lf))
    (C, n, m), hs = lax.scan(step, (C0, n0, m0), xs)
    h = jnp.swapaxes(hs, 0, 1).reshape(B, T, H, HD)
    return h, (C, n, m)


def mlstm_mixer(qk, v, o, gates, conv_w, conv_b, g_h, C0, n0, m0, rows):
    B, T, _ = v.shape
    f32 = jnp.float32
    qk = jax.nn.silu(short_conv(qk, conv_w, conv_b, rows))

    def heads(a):
        return a.astype(f32).reshape(B, T, MLSTM_HEADS, MLSTM_HD)

    qh = heads(qk[..., :D_MLSTM])
    kh = heads(qk[..., D_MLSTM:]) * (MLSTM_HD ** -0.5)
    vh = heads(v)
    g = gates.astype(f32).reshape(B, T, N_DIR, 2, MLSTM_HEADS)
    ig = g[:, :, :, 0]
    lf = jax.nn.log_sigmoid(g[:, :, :, 1])
    C0 = C0.astype(f32)
    n0 = n0.astype(f32)
    m0 = m0.astype(f32)
    h_f, st_f = mlstm_scan(qh, kh, vh, ig[:, :, 0], lf[:, :, 0], C0[:, 0], n0[:, 0], m0[:, 0])

    def rev(a):
        return jnp.flip(a, axis=1)

    h_b, st_b = mlstm_scan(rev(qh), rev(kh), rev(vh), rev(ig[:, :, 1]), rev(lf[:, :, 1]),
                           C0[:, 1], n0[:, 1], m0[:, 1])
    h = rmsnorm(h_f + rev(h_b), g_h).reshape(B, T, D_MLSTM).astype(v.dtype)
    out = h * jax.nn.sigmoid(o)
    state = tuple(jnp.stack([sf, sb], axis=1) for sf, sb in zip(st_f, st_b))
    return out, state


def trunk_layer(x, mod, C0, n0, m0, rows, p):
    sh1, sc1, ga1, sh2, sc2, ga2 = jnp.split(mod[:, None, :], 6, axis=-1)
    h = rmsnorm(x, p['g_norm1']) * (1 + sc1) + sh1
    z = h @ p['w_in']
    a_out = chunk_gmlp(z[..., OFF_U:OFF_V], z[..., OFF_V:OFF_Q], p['w_s'], p['b_s'], p['g_v'])
    b_out, state = mlstm_mixer(z[..., OFF_Q:OFF_VV], z[..., OFF_VV:OFF_O], z[..., OFF_O:OFF_G],
                               z[..., OFF_G:] + p['b_gate'], p['conv_w'], p['conv_b'], p['g_h'],
                               C0, n0, m0, rows)
    x = x + ga1 * (jnp.concatenate([a_out, b_out], axis=-1) @ p['w_out'])
    h = rmsnorm(x, p['g_norm2']) * (1 + sc2) + sh2
    x = x + ga2 * ((jax.nn.silu(h @ p['w1']) * (h @ p['w3'])) @ p['w2'])
    return x, state


def setup_inputs(seed: int = 0) -> dict:
    key = jax.random.key(seed)
    ks = jax.random.split(key, 26)
    f32 = jnp.float32
    L = DEPTH
    H = MLSTM_HEADS
    HD = MLSTM_HD

    def nrm(k, shape, s):
        return jax.random.normal(k, shape, f32) * s

    gate_offset = jnp.tile(jnp.repeat(jnp.array([0.0, 3.0], f32), H), N_DIR)
    return {
        'x_prompt': nrm(ks[0], (BATCH, SEQ, D_MODEL), 1.0),
        'x_sample': nrm(ks[1], (DEC_BATCH, DEC_SEQ, D_MODEL), 1.0),
        'state_C': nrm(ks[2], (DEC_BATCH, L, N_DIR, H, HD, HD), 0.1),
        'state_n': nrm(ks[3], (DEC_BATCH, L, N_DIR, H, HD), 0.5),
        'state_m': nrm(ks[4], (DEC_BATCH, L, N_DIR, H), 0.5),
        'c': nrm(ks[5], (DEC_BATCH, D_MODEL), 1.0),
        'c_ctx': nrm(ks[6], (D_MODEL,), 1.0),
        'w_ada': nrm(ks[7], (L, D_MODEL, 6 * D_MODEL), 0.5 * D_MODEL ** -0.5),
        'b_ada': nrm(ks[8], (L, 6 * D_MODEL), 0.02),
        'g_norm1': 1.0 + nrm(ks[9], (L, D_MODEL), 0.02),
        'w_in': nrm(ks[10], (L, D_MODEL, D_IN), D_MODEL ** -0.5),
        'b_gate': gate_offset + nrm(ks[11], (L, N_GATE_COLS), 0.1),
        'w_s': nrm(ks[12], (L, GMLP_GROUPS, GMLP_CHUNK, GMLP_CHUNK), GMLP_CHUNK ** -0.5),
        'b_s': 1.0 + nrm(ks[13], (L, GMLP_GROUPS, GMLP_CHUNK), 0.1),
        'g_v': 1.0 + nrm(ks[14], (L, GMLP_GROUPS, GMLP_GW), 0.02),
        'conv_w': nrm(ks[15], (L, CONV_W, 2 * D_MLSTM), CONV_W ** -0.5),
        'conv_b': nrm(ks[16], (L, 2 * D_MLSTM), 0.02),
        'g_h': 1.0 + nrm(ks[17], (L, H, HD), 0.02),
        'w_out': nrm(ks[18], (L, D_MIX, D_MODEL), D_MIX ** -0.5),
        'g_norm2': 1.0 + nrm(ks[19], (L, D_MODEL), 0.02),
        'w1': nrm(ks[20], (L, D_MODEL, D_FF), D_MODEL ** -0.5),
        'w3': nrm(ks[21], (L, D_MODEL, D_FF), D_MODEL ** -0.5),
        'w2': nrm(ks[22], (L, D_FF, D_MODEL), D_FF ** -0.5),
        'g_final': 1.0 + nrm(ks[23], (D_MODEL,), 0.02),
    }


def reference(x_prompt, x_sample, state_C, state_n, state_m, c, c_ctx, w_ada, b_ada, g_norm1,
              w_in, b_gate, w_s, b_s, g_v, conv_w, conv_b, g_h, w_out, g_norm2, w1, w3, w2,
              g_final):
    f32 = jnp.float32
    B = x_prompt.shape[0]
    rows = x_sample.shape[1] // GRID_W
    zC = jnp.zeros((B, N_DIR, MLSTM_HEADS, MLSTM_HD, MLSTM_HD), f32)
    zn = jnp.zeros((B, N_DIR, MLSTM_HEADS, MLSTM_HD), f32)
    zm = jnp.zeros((B, N_DIR, MLSTM_HEADS), f32)
    xp = x_prompt
    xs = x_sample
    new_C, new_n, new_m = [], [], []
    for l in range(DEPTH):
        p = {'g_norm1': g_norm1[l], 'w_in': w_in[l], 'b_gate': b_gate[l], 'w_s': w_s[l],
             'b_s': b_s[l], 'g_v': g_v[l], 'conv_w': conv_w[l], 'conv_b': conv_b[l],
             'g_h': g_h[l], 'w_out': w_out[l], 'g_norm2': g_norm2[l], 'w1': w1[l],
             'w3': w3[l], 'w2': w2[l]}
        mod_ctx = (jax.nn.silu(c_ctx) @ w_ada[l] + b_ada[l])[None]
        xp, (Cp, npr, mp) = trunk_layer(xp, mod_ctx, zC, zn, zm, None, p)
        new_C.append(Cp)
        new_n.append(npr)
        new_m.append(mp)
        mod_lat = jax.nn.silu(c) @ w_ada[l] + b_ada[l]
        xs, _ = trunk_layer(xs, mod_lat, state_C[:, l], state_n[:, l], state_m[:, l], rows, p)
    y_prompt = rmsnorm(xp, g_final)
    y_sample = rmsnorm(xs, g_final)
    new_C = jnp.stack(new_C, axis=1)
    new_n = jnp.stack(new_n, axis=1)
    new_m = jnp.stack(new_m, axis=1)
    return (y_prompt, y_sample, new_C, new_n, new_m)
```

```python
import functools

import jax
import jax.numpy as jnp
from jax import lax
from jax.experimental import pallas as pl
from jax.experimental.pallas import tpu as pltpu

D_MODEL = 1024
D_GMLP = 512
D_MLSTM = 512
GROUPS = 4
HEADS = 4
HD = 128
CHUNK = 128
N_DIR = 2
N_GATE = 16
GRID_W = 64
EPS = 1e-6
NEG = -1e30

TOKEN_BLOCK = 256
VMEM_LIMIT_BYTES = 56 * 1024 * 1024

F32 = jnp.float32
BF16 = jnp.bfloat16


def _rms(x, g):
    return x * lax.rsqrt(jnp.mean(x * x, axis=-1, keepdims=True) + EPS) * g


def _sigmoid(x):
    return 1.0 / (1.0 + jnp.exp(-x))


def _gelu_tanh(x):
    return 0.5 * x * (1.0 + jnp.tanh(0.7978845608028654 * (x + 0.044715 * (x * x * x))))


def _log_sigmoid(x):
    return jnp.minimum(x, 0.0) - jnp.log(1.0 + jnp.exp(-jnp.abs(x)))


def _dot(a, b):
    return jnp.dot(a, b, preferred_element_type=F32)


def _dot_exact(a, b):
    return jnp.dot(a, b, preferred_element_type=F32, precision=lax.Precision.HIGHEST)


def _const_spec(shape):
    zeros = (0,) * len(shape)
    return pl.BlockSpec(shape, lambda *_: zeros, pipeline_mode=pl.Buffered(1))


def _ada_kernel(c_ref, w_ref, b_ref, o_ref):
    c = c_ref[...]
    s = (c * _sigmoid(c)).astype(BF16)
    o_ref[...] = _dot(s, w_ref[...].astype(BF16)) + b_ref[...]


def _ada(cs, w_ada, b_ada):
    rows, d = cs.shape
    n = w_ada.shape[1]
    tn = 1024
    return pl.pallas_call(
        _ada_kernel,
        out_shape=jax.ShapeDtypeStruct((rows, n), F32),
        grid=(n // tn,),
        in_specs=[
            pl.BlockSpec((rows, d), lambda j: (0, 0)),
            pl.BlockSpec((d, tn), lambda j: (0, j)),
            pl.BlockSpec((1, tn), lambda j: (0, j)),
        ],
        out_specs=pl.BlockSpec((rows, tn), lambda j: (0, j)),
        compiler_params=pltpu.CompilerParams(
            dimension_semantics=("arbitrary",), vmem_limit_bytes=VMEM_LIMIT_BYTES),
        name="ada",
    )(cs, w_ada, b_ada)


def _inproj_kernel(x_ref, mod_ref, g1_ref, wu_ref, wv_ref, wqk_ref, wvv_ref, wo_ref, wg_ref, wgt_ref,
                   bg_ref, bgt_ref, cw_ref, cb_ref, ws_ref, bs_ref, gv_ref,
                   a_ref, q_ref, kt_ref, v_ref, o_ref, gc_ref, gr_ref, *, seg):
    tb = x_ref.shape[0]
    x = x_ref[...]
    mod = mod_ref[0]
    sh1 = mod[:, 0:D_MODEL]
    sc1 = mod[:, D_MODEL:2 * D_MODEL]
    hb = (_rms(x, g1_ref[...]) * (1.0 + sc1) + sh1).astype(BF16)

    u = _gelu_tanh(_dot(hb, wu_ref[...]))
    vv = _gelu_tanh(_dot(hb, wv_ref[...]))
    for g in range(GROUPS):
        gs = slice(g * HD, (g + 1) * HD)
        vg = _rms(vv[:, gs], gv_ref[:, gs]).astype(BF16)
        for c in range(tb // CHUNK):
            cs = slice(c * CHUNK, (c + 1) * CHUNK)
            mixed = _dot(ws_ref[g], vg[cs]) + bs_ref[:, gs]
            a_ref[cs, gs] = (u[cs, gs] * mixed).astype(BF16)

    zqk = _dot(hb, wqk_ref[...])
    pos = lax.broadcasted_iota(jnp.int32, (tb, 1), 0) % seg
    prev = jnp.where(pos != 0, pltpu.roll(zqk, 1, 0), 0.0)
    nxt = jnp.where(pos != seg - 1, pltpu.roll(zqk, tb - 1, 0), 0.0)
    y = cb_ref[...] + prev * cw_ref[0:1, :] + zqk * cw_ref[1:2, :] + nxt * cw_ref[2:3, :]
    y = y * _sigmoid(y)
    q_ref[...] = y[:, :D_MLSTM].astype(BF16)
    kt_ref[...] = jnp.transpose(y[:, D_MLSTM:] * (HD ** -0.5)).astype(BF16)
    v_ref[...] = _dot(hb, wvv_ref[...]).astype(BF16)
    o_ref[...] = _dot(hb, wo_ref[...])

    gates = _dot(hb, wg_ref[...]) + bg_ref[...]
    gates_t = lax.dot_general(wgt_ref[...], hb, (((1,), (1,)), ((), ())),
                              preferred_element_type=F32) + bgt_ref[...]
    ls = _log_sigmoid(gates)
    ls_t = _log_sigmoid(gates_t)
    ri = lax.broadcasted_iota(jnp.int32, (CHUNK, CHUNK), 0)
    ci = lax.broadcasted_iota(jnp.int32, (CHUNK, CHUNK), 1)
    lower = (ci <= ri).astype(F32)
    upper = (ci >= ri).astype(F32)
    col = lax.broadcasted_iota(jnp.int32, (CHUNK, N_GATE), 1)
    row = lax.broadcasted_iota(jnp.int32, (N_GATE, CHUNK), 0)
    for c in range(tb // CHUNK):
        cs = slice(c * CHUNK, (c + 1) * CHUNK)
        fwd = _dot_exact(lower, ls[cs])
        bwd = _dot_exact(upper, ls[cs])
        cum = jnp.where(col >= 8, bwd, fwd)
        gc_ref[cs, :] = jnp.where((col % 8) >= 4, cum, gates[cs])
        fwd_t = _dot_exact(ls_t[:, cs], upper)
        bwd_t = _dot_exact(ls_t[:, cs], lower)
        cum_t = jnp.where(row >= 8, bwd_t, fwd_t)
        gr_ref[:, cs] = jnp.where((row % 8) >= 4, cum_t, gates_t[:, cs])


def _inproj(x, mod, blocks_per_mod, seg, p):
    n = x.shape[0]
    tb = TOKEN_BLOCK
    nb = n // tb
    tok = lambda w: pl.BlockSpec((tb, w), lambda i: (i, 0))
    in_specs = [
        tok(D_MODEL),
        pl.BlockSpec((1, 1, 6 * D_MODEL), lambda i: (i // blocks_per_mod, 0, 0)),
        _const_spec((1, D_MODEL)),
        _const_spec((D_MODEL, D_GMLP)),
        _const_spec((D_MODEL, D_GMLP)),
        _const_spec((D_MODEL, 2 * D_MLSTM)),
        _const_spec((D_MODEL, D_MLSTM)),
        _const_spec((D_MODEL, D_MLSTM)),
        _const_spec((D_MODEL, N_GATE)),
        _const_spec((N_GATE, D_MODEL)),
        _const_spec((1, N_GATE)),
        _const_spec((N_GATE, 1)),
        _const_spec((3, 2 * D_MLSTM)),
        _const_spec((1, 2 * D_MLSTM)),
        _const_spec((GROUPS, CHUNK, CHUNK)),
        _const_spec((CHUNK, D_GMLP)),
        _const_spec((1, D_GMLP)),
    ]
    out_shape = [
        jax.ShapeDtypeStruct((n, D_GMLP), BF16),
        jax.ShapeDtypeStruct((n, D_MLSTM), BF16),
        jax.ShapeDtypeStruct((D_MLSTM, n), BF16),
        jax.ShapeDtypeStruct((n, D_MLSTM), BF16),
        jax.ShapeDtypeStruct((n, D_MLSTM), F32),
        jax.ShapeDtypeStruct((n, N_GATE), F32),
        jax.ShapeDtypeStruct((N_GATE, n), F32),
    ]
    out_specs = [
        tok(D_GMLP), tok(D_MLSTM),
        pl.BlockSpec((D_MLSTM, tb), lambda i: (0, i)),
        tok(D_MLSTM), tok(D_MLSTM), tok(N_GATE),
        pl.BlockSpec((N_GATE, tb), lambda i: (0, i)),
    ]
    return pl.pallas_call(
        functools.partial(_inproj_kernel, seg=seg),
        out_shape=out_shape,
        grid=(nb,),
        in_specs=in_specs,
        out_specs=out_specs,
        compiler_params=pltpu.CompilerParams(
            dimension_semantics=("arbitrary",), vmem_limit_bytes=VMEM_LIMIT_BYTES),
        name="inproj",
    )(x, mod, p["g1"], p["wu"], p["wv"], p["wqk"], p["wvv"], p["wo"], p["wg"], p["wgt"],
      p["bg"], p["bgt"], p["cw"], p["cb"], p["ws"], p["bs"], p["gv"])


def _mlstm_unit(q, kt, v, bc, br, ir, ic, s_aug, m, mask, b_last):
    logw = jnp.where(mask, bc - br + ir, NEG)
    a = bc + m
    mj = jnp.maximum(a, jnp.max(logw, axis=-1, keepdims=True))
    w = jnp.exp(logw - mj)
    inter = jnp.exp(a - mj)
    s = _dot(q, kt) * w
    qs = _dot(q, s_aug.astype(BF16))
    num = _dot(s.astype(BF16), v) + inter * qs[:, :HD]
    den = jnp.sum(s, axis=-1, keepdims=True) + inter * qs[:, HD:HD + 1]
    h = num / jnp.maximum(jnp.abs(den), jnp.exp(-mj))
    g = b_last - bc + ic
    m_new = jnp.maximum(b_last + m, jnp.max(g, axis=0, keepdims=True))
    wc = jnp.exp(g - m_new)
    decay = jnp.exp(b_last + m - m_new)
    lane = lax.broadcasted_iota(jnp.int32, (CHUNK, HD), 1)
    v_aug = jnp.concatenate(
        [v.astype(F32) * wc, jnp.where(lane == 0, wc, 0.0)], axis=-1).astype(BF16)
    s_new = decay * s_aug + _dot(kt, v_aug)
    return h, s_new, m_new


def _mlstm_kernel(*refs, has_init):
    if has_init:
        s0_ref, m0_ref = refs[:2]
        refs = refs[2:]
    (qf_ref, ktf_ref, vf_ref, gcf_ref, grf_ref,
     qb_ref, ktb_ref, vb_ref, gcb_ref, grb_ref,
     hf_ref, hb_ref, so_ref, mo_ref, s_ref, m_ref) = refs
    c = pl.program_id(1)
    nc = pl.num_programs(1)

    @pl.when(c == 0)
    def _():
        if has_init:
            s_ref[...] = s0_ref[0]
            m_ref[...] = m0_ref[0]
        else:
            s_ref[...] = jnp.zeros_like(s_ref)
            m_ref[...] = jnp.zeros_like(m_ref)

    ri = lax.broadcasted_iota(jnp.int32, (CHUNK, CHUNK), 0)
    ci = lax.broadcasted_iota(jnp.int32, (CHUNK, CHUNK), 1)
    dirs = (
        (qf_ref, ktf_ref, vf_ref, gcf_ref, grf_ref, hf_ref, ri >= ci, CHUNK - 1),
        (qb_ref, ktb_ref, vb_ref, gcb_ref, grb_ref, hb_ref, ri <= ci, 0),
    )
    for d, (q_ref, kt_ref, v_ref, gc_ref, gr_ref, h_ref, mask, last) in enumerate(dirs):
        for hd in range(HEADS):
            idx = d * HEADS + hd
            hs = slice(hd * HD, (hd + 1) * HD)
            icol = d * 8 + hd
            bcol = d * 8 + 4 + hd
            bc = gc_ref[:, bcol:bcol + 1]
            ic = gc_ref[:, icol:icol + 1]
            br = gr_ref[bcol:bcol + 1, :]
            ir = gr_ref[icol:icol + 1, :]
            b_last = gc_ref[last:last + 1, bcol:bcol + 1]
            m = m_ref[idx:idx + 1, 0:1]
            h, s_new, m_new = _mlstm_unit(
                q_ref[:, hs], kt_ref[hs, :], v_ref[:, hs], bc, br, ir, ic,
                s_ref[idx], m, mask, b_last)
            h_ref[:, hs] = h
            s_ref[idx] = s_new
            m_ref[idx:idx + 1, :] = jnp.broadcast_to(m_new, (1, HD))

    @pl.when(c == nc - 1)
    def _():
        so_ref[0] = s_ref[...]
        mo_ref[0] = m_ref[...]


def _mlstm(q, kt, v, gc, gr, batch, s0=None, m0=None):
    n = q.shape[0]
    nc = n // batch // CHUNK
    fwd = lambda b, c: b * nc + c
    bwd = lambda b, c: b * nc + nc - 1 - c

    def specs(ix):
        return [
            pl.BlockSpec((CHUNK, D_MLSTM), lambda b, c: (ix(b, c), 0)),
            pl.BlockSpec((D_MLSTM, CHUNK), lambda b, c: (0, ix(b, c))),
            pl.BlockSpec((CHUNK, D_MLSTM), lambda b, c: (ix(b, c), 0)),
            pl.BlockSpec((CHUNK, N_GATE), lambda b, c: (ix(b, c), 0)),
            pl.BlockSpec((N_GATE, CHUNK), lambda b, c: (0, ix(b, c))),
        ]

    nu = N_DIR * HEADS
    in_specs = specs(fwd) + specs(bwd)
    args = [q, kt, v, gc, gr, q, kt, v, gc, gr]
    has_init = s0 is not None
    if has_init:
        in_specs = [
            pl.BlockSpec((1, nu, HD, 2 * HD), lambda b, c: (b, 0, 0, 0)),
            pl.BlockSpec((1, nu, HD), lambda b, c: (b, 0, 0)),
        ] + in_specs
        args = [s0, m0] + args
    out_shape = [
        jax.ShapeDtypeStruct((n, D_MLSTM), F32),
        jax.ShapeDtypeStruct((n, D_MLSTM), F32),
        jax.ShapeDtypeStruct((batch, nu, HD, 2 * HD), F32),
        jax.ShapeDtypeStruct((batch, nu, HD), F32),
    ]
    out_specs = [
        pl.BlockSpec((CHUNK, D_MLSTM), lambda b, c: (fwd(b, c), 0)),
        pl.BlockSpec((CHUNK, D_MLSTM), lambda b, c: (bwd(b, c), 0)),
        pl.BlockSpec((1, nu, HD, 2 * HD), lambda b, c: (b, 0, 0, 0)),
        pl.BlockSpec((1, nu, HD), lambda b, c: (b, 0, 0)),
    ]
    return pl.pallas_call(
        functools.partial(_mlstm_kernel, has_init=has_init),
        out_shape=out_shape,
        grid=(batch, nc),
        in_specs=in_specs,
        out_specs=out_specs,
        scratch_shapes=[pltpu.VMEM((nu, HD, 2 * HD), F32), pltpu.VMEM((nu, HD), F32)],
        compiler_params=pltpu.CompilerParams(
            dimension_semantics=("arbitrary", "arbitrary"), vmem_limit_bytes=VMEM_LIMIT_BYTES),
        name="mlstm",
    )(*args)


def _outffn_kernel(x_ref, a_ref, hf_ref, hb_ref, o_ref, mod_ref, gh_ref, woa_ref, wob_ref, g2_ref,
                   w1_ref, w3_ref, w2_ref, gf_ref, y_ref):
    x = x_ref[...]
    mod = mod_ref[0]
    ga1 = mod[:, 2 * D_MODEL:3 * D_MODEL]
    sh2 = mod[:, 3 * D_MODEL:4 * D_MODEL]
    sc2 = mod[:, 4 * D_MODEL:5 * D_MODEL]
    ga2 = mod[:, 5 * D_MODEL:6 * D_MODEL]

    hs = hf_ref[...] + hb_ref[...]
    sig = _sigmoid(o_ref[...])
    parts = []
    for hd in range(HEADS):
        sl = slice(hd * HD, (hd + 1) * HD)
        parts.append(_rms(hs[:, sl], gh_ref[:, sl]) * sig[:, sl])
    b_out = jnp.concatenate(parts, axis=-1).astype(BF16)
    mix = _dot(a_ref[...], woa_ref[...]) + _dot(b_out, wob_ref[...])
    x1 = x + ga1 * mix

    h2 = (_rms(x1, g2_ref[...]) * (1.0 + sc2) + sh2).astype(BF16)
    u = _dot(h2, w1_ref[...])
    g = _dot(h2, w3_ref[...])
    f = (u * _sigmoid(u) * g).astype(BF16)
    x2 = x1 + ga2 * _dot(f, w2_ref[...])
    y_ref[...] = _rms(x2, gf_ref[...])


def _outffn(x, a, hf, hb, o, mod, blocks_per_mod, p):
    n = x.shape[0]
    tb = TOKEN_BLOCK
    d_ff = p["w1"].shape[1]
    tok = lambda w: pl.BlockSpec((tb, w), lambda i: (i, 0))
    in_specs = [
        tok(D_MODEL), tok(D_GMLP), tok(D_MLSTM), tok(D_MLSTM), tok(D_MLSTM),
        pl.BlockSpec((1, 1, 6 * D_MODEL), lambda i: (i // blocks_per_mod, 0, 0)),
        _const_spec((1, D_MLSTM)),
        _const_spec((D_GMLP, D_MODEL)),
        _const_spec((D_MLSTM, D_MODEL)),
        _const_spec((1, D_MODEL)),
        _const_spec((D_MODEL, d_ff)),
        _const_spec((D_MODEL, d_ff)),
        _const_spec((d_ff, D_MODEL)),
        _const_spec((1, D_MODEL)),
    ]
    return pl.pallas_call(
        _outffn_kernel,
        out_shape=jax.ShapeDtypeStruct((n, D_MODEL), F32),
        grid=(n // tb,),
        in_specs=in_specs,
        out_specs=tok(D_MODEL),
        compiler_params=pltpu.CompilerParams(
            dimension_semantics=("arbitrary",), vmem_limit_bytes=VMEM_LIMIT_BYTES),
        name="outffn",
    )(x, a, hf, hb, o, mod, p["gh"], p["woa"], p["wob"], p["g2"], p["w1"], p["w3"], p["w2"], p["gf"])


def _layer_params(l, g_norm1, w_in, b_gate, w_s, b_s, g_v, conv_w, conv_b, g_h, w_out, g_norm2,
                  w1, w3, w2, g_final):
    wi = w_in[l].astype(BF16)
    off_v, off_q, off_vv, off_o, off_g = 512, 1024, 2048, 2560, 3072
    return {
        "g1": g_norm1[l][None, :],
        "wu": wi[:, :off_v], "wv": wi[:, off_v:off_q], "wqk": wi[:, off_q:off_vv],
        "wvv": wi[:, off_vv:off_o], "wo": wi[:, off_o:off_g],
        "wg": wi[:, off_g:], "wgt": wi[:, off_g:].T,
        "bg": b_gate[l][None, :], "bgt": b_gate[l][:, None],
        "cw": conv_w[l], "cb": conv_b[l][None, :],
        "ws": w_s[l].astype(BF16),
        "bs": jnp.repeat(b_s[l].T, HD, axis=1),
        "gv": g_v[l].reshape(1, D_GMLP),
        "gh": g_h[l].reshape(1, D_MLSTM),
        "woa": w_out[l][:D_GMLP].astype(BF16), "wob": w_out[l][D_GMLP:].astype(BF16),
        "g2": g_norm2[l][None, :],
        "w1": w1[l].astype(BF16), "w3": w3[l].astype(BF16), "w2": w2[l].astype(BF16),
        "gf": g_final[None, :],
    }


def _trunk(x, mod, blocks_per_mod, seg, batch, p, s0=None, m0=None):
    a, q, kt, v, o, gc, gr = _inproj(x, mod, blocks_per_mod, seg, p)
    hf, hb, s_out, m_out = _mlstm(q, kt, v, gc, gr, batch, s0, m0)
    y = _outffn(x, a, hf, hb, o, mod, blocks_per_mod, p)
    return y, s_out, m_out


def kernel(x_prompt, x_sample, state_C, state_n, state_m, c, c_ctx, w_ada, b_ada, g_norm1, w_in,
           b_gate, w_s, b_s, g_v, conv_w, conv_b, g_h, w_out, g_norm2, w1, w3, w2, g_final):
    bp, tp, d = x_prompt.shape
    bs_, ts, _ = x_sample.shape
    depth = w_in.shape[0]
    assert depth == 1, "final norm is fused into the layer's last kernel"
    xp = x_prompt.reshape(bp * tp, d)
    xs = x_sample.reshape(bs_ * ts, d)
    nu = N_DIR * HEADS

    cs = jnp.zeros((8, d), F32).at[0].set(c_ctx).at[1:1 + bs_].set(c)
    new_c, new_n, new_m = [], [], []
    for l in range(depth):
        p = _layer_params(l, g_norm1, w_in, b_gate, w_s, b_s, g_v, conv_w, conv_b, g_h, w_out,
                          g_norm2, w1, w3, w2, g_final)
        mod = _ada(cs, w_ada[l], b_ada[l][None, :])
        mod_ctx = mod[0:1].reshape(1, 1, 6 * d)
        mod_lat = mod[1:1 + bs_].reshape(bs_, 1, 6 * d)

        xp, s_ctx, m_ctx = _trunk(xp, mod_ctx, (bp * tp) // TOKEN_BLOCK, tp, bp, p)
        new_c.append(s_ctx[..., :HD].reshape(bp, N_DIR, HEADS, HD, HD))
        new_n.append(s_ctx[..., HD].reshape(bp, N_DIR, HEADS, HD))
        new_m.append(m_ctx[..., 0].reshape(bp, N_DIR, HEADS))

        s0 = jnp.concatenate(
            [state_C[:, l], state_n[:, l][..., None], jnp.zeros((bs_, N_DIR, HEADS, HD, HD - 1), F32)],
            axis=-1).reshape(bs_, nu, HD, 2 * HD)
        m0 = jnp.broadcast_to(state_m[:, l].reshape(bs_, nu, 1), (bs_, nu, HD))
        xs, _, _ = _trunk(xs, mod_lat, ts // TOKEN_BLOCK, GRID_W, bs_, p, s0, m0)

    return (xp.reshape(bp, tp, d), xs.reshape(bs_, ts, d),
            jnp.stack(new_c, axis=1), jnp.stack(new_n, axis=1), jnp.stack(new_m, axis=1))
```

```python
import functools

import jax
import jax.numpy as jnp
from jax import lax
from jax.experimental import pallas as pl
from jax.experimental.pallas import tpu as pltpu

D_MODEL = 1024
D_GMLP = 512
D_MLSTM = 512
GROUPS = 4
HEADS = 4
HD = 128
CHUNK = 128
N_DIR = 2
N_UNIT = N_DIR * HEADS
GRID_W = 64
EPS = 1e-6
NEG = -1e30
ST_ROWS = HD + 16
GR_ROWS = 4 * N_UNIT

TOKEN_BLOCK = 256
VMEM_LIMIT_BYTES = 56 * 1024 * 1024

F32 = jnp.float32
BF16 = jnp.bfloat16


def _rms(x, g):
    return x * lax.rsqrt(jnp.mean(x * x, axis=-1, keepdims=True) + EPS) * g


def _sigmoid(x):
    return 1.0 / (1.0 + jnp.exp(-x))


def _gelu_tanh(x):
    return 0.5 * x * (1.0 + jnp.tanh(0.7978845608028654 * (x + 0.044715 * (x * x * x))))


def _log_sigmoid(x):
    return jnp.minimum(x, 0.0) - jnp.log(1.0 + jnp.exp(-jnp.abs(x)))


def _dot(a, b):
    return jnp.dot(a, b, preferred_element_type=F32)


def _dot_nt(a, b):
    return lax.dot_general(a, b, (((1,), (1,)), ((), ())), preferred_element_type=F32)


def _dot_exact(a, b):
    return jnp.dot(a, b, preferred_element_type=F32, precision=lax.Precision.HIGHEST)


def _const_spec(shape):
    zeros = (0,) * len(shape)
    return pl.BlockSpec(shape, lambda *_: zeros, pipeline_mode=pl.Buffered(1))


def _ada_kernel(c_ref, w_ref, b_ref, o_ref):
    c = c_ref[...]
    s = (c * _sigmoid(c)).astype(BF16)
    o_ref[...] = _dot(s, w_ref[...].astype(BF16)) + b_ref[...]


def _ada(cs, w_ada, b_ada):
    rows, d = cs.shape
    n = w_ada.shape[1]
    tn = 1024
    return pl.pallas_call(
        _ada_kernel,
        out_shape=jax.ShapeDtypeStruct((rows, n), F32),
        grid=(n // tn,),
        in_specs=[
            pl.BlockSpec((rows, d), lambda j: (0, 0)),
            pl.BlockSpec((d, tn), lambda j: (0, j)),
            pl.BlockSpec((1, tn), lambda j: (0, j)),
        ],
        out_specs=pl.BlockSpec((rows, tn), lambda j: (0, j)),
        compiler_params=pltpu.CompilerParams(
            dimension_semantics=("arbitrary",), vmem_limit_bytes=VMEM_LIMIT_BYTES),
        name="ada",
    )(cs, w_ada, b_ada)


def _inproj_kernel(x_ref, mod_ref, g1_ref, wu_ref, wv_ref, wqk_ref, wvvt_ref, wo_ref, wg_ref,
                   wgt_ref, bg_ref, bgt_ref, cw_ref, cb_ref, ws_ref, bs_ref, gv_ref,
                   a_ref, qt_ref, k_ref, vt_ref, o_ref, gc_ref, gr_ref, *, seg):
    tb = x_ref.shape[0]
    x = x_ref[...]
    mod = mod_ref[0]
    sh1 = mod[:, 0:D_MODEL]
    sc1 = mod[:, D_MODEL:2 * D_MODEL]
    hb = (_rms(x, g1_ref[...]) * (1.0 + sc1) + sh1).astype(BF16)

    u = _gelu_tanh(_dot(hb, wu_ref[...]))
    vv = _gelu_tanh(_dot(hb, wv_ref[...]))
    for g in range(GROUPS):
        gs = slice(g * HD, (g + 1) * HD)
        vg = _rms(vv[:, gs], gv_ref[:, gs]).astype(BF16)
        for c in range(tb // CHUNK):
            cs = slice(c * CHUNK, (c + 1) * CHUNK)
            mixed = _dot(ws_ref[g], vg[cs]) + bs_ref[:, gs]
            a_ref[cs, gs] = (u[cs, gs] * mixed).astype(BF16)

    zqk = _dot(hb, wqk_ref[...])
    pos = lax.broadcasted_iota(jnp.int32, (tb, 1), 0) % seg
    prev = jnp.where(pos != 0, pltpu.roll(zqk, 1, 0), 0.0)
    nxt = jnp.where(pos != seg - 1, pltpu.roll(zqk, tb - 1, 0), 0.0)
    y = cb_ref[...] + prev * cw_ref[0:1, :] + zqk * cw_ref[1:2, :] + nxt * cw_ref[2:3, :]
    y = y * _sigmoid(y)
    qt_ref[...] = jnp.transpose(y[:, :D_MLSTM]).astype(BF16)
    k_ref[...] = (y[:, D_MLSTM:] * (HD ** -0.5)).astype(BF16)
    vt_ref[...] = _dot_nt(wvvt_ref[...], hb).astype(BF16)
    o_ref[...] = _dot(hb, wo_ref[...])

    gates = _dot(hb, wg_ref[...]) + bg_ref[...]
    gi = gates[:, 0:N_UNIT]
    lf = _log_sigmoid(gates[:, N_UNIT:2 * N_UNIT])
    gt = _dot_nt(wgt_ref[...], hb) + bgt_ref[...]
    gi_t = gt[0:N_UNIT]
    lf_t = _log_sigmoid(gt[N_UNIT:2 * N_UNIT])
    ri = lax.broadcasted_iota(jnp.int32, (CHUNK, CHUNK), 0)
    ci = lax.broadcasted_iota(jnp.int32, (CHUNK, CHUNK), 1)
    lower = (ci <= ri).astype(F32)
    upper = (ci >= ri).astype(F32)
    ones = jnp.ones((CHUNK, CHUNK), F32)
    col_bwd = lax.broadcasted_iota(jnp.int32, (CHUNK, N_UNIT), 1) >= HEADS
    row_bwd = lax.broadcasted_iota(jnp.int32, (N_UNIT, CHUNK), 0) >= HEADS
    for c in range(tb // CHUNK):
        cs = slice(c * CHUNK, (c + 1) * CHUNK)
        b_col = jnp.where(col_bwd, _dot_exact(upper, lf[cs]), _dot_exact(lower, lf[cs]))
        gc_ref[cs, :] = gi[cs] - b_col
        b_row = jnp.where(row_bwd, _dot_exact(lf_t[:, cs], lower), _dot_exact(lf_t[:, cs], upper))
        b_last = _dot_exact(lf_t[:, cs], ones)
        g_row = b_last - b_row + gi_t[:, cs]
        g_max = jnp.broadcast_to(jnp.max(g_row, axis=1, keepdims=True), (N_UNIT, CHUNK))
        gr_ref[0 * N_UNIT:1 * N_UNIT, cs] = b_row
        gr_ref[1 * N_UNIT:2 * N_UNIT, cs] = g_row
        gr_ref[2 * N_UNIT:3 * N_UNIT, cs] = b_last
        gr_ref[3 * N_UNIT:4 * N_UNIT, cs] = g_max


def _inproj(x, mod, blocks_per_mod, seg, p):
    n = x.shape[0]
    tb = TOKEN_BLOCK
    assert tb % seg == 0 and n % tb == 0
    nb = n // tb
    tok = lambda w: pl.BlockSpec((tb, w), lambda i: (i, 0))
    tok_t = lambda h: pl.BlockSpec((h, tb), lambda i: (0, i))
    in_specs = [
        tok(D_MODEL),
        pl.BlockSpec((1, 1, 6 * D_MODEL), lambda i: (i // blocks_per_mod, 0, 0)),
        _const_spec((1, D_MODEL)),
        _const_spec((D_MODEL, D_GMLP)),
        _const_spec((D_MODEL, D_GMLP)),
        _const_spec((D_MODEL, 2 * D_MLSTM)),
        _const_spec((D_MLSTM, D_MODEL)),
        _const_spec((D_MODEL, D_MLSTM)),
        _const_spec((D_MODEL, 2 * N_UNIT)),
        _const_spec((2 * N_UNIT, D_MODEL)),
        _const_spec((1, 2 * N_UNIT)),
        _const_spec((2 * N_UNIT, 1)),
        _const_spec((3, 2 * D_MLSTM)),
        _const_spec((1, 2 * D_MLSTM)),
        _const_spec((GROUPS, CHUNK, CHUNK)),
        _const_spec((CHUNK, D_GMLP)),
        _const_spec((1, D_GMLP)),
    ]
    out_shape = [
        jax.ShapeDtypeStruct((n, D_GMLP), BF16),
        jax.ShapeDtypeStruct((D_MLSTM, n), BF16),
        jax.ShapeDtypeStruct((n, D_MLSTM), BF16),
        jax.ShapeDtypeStruct((D_MLSTM, n), BF16),
        jax.ShapeDtypeStruct((n, D_MLSTM), F32),
        jax.ShapeDtypeStruct((n, N_UNIT), F32),
        jax.ShapeDtypeStruct((GR_ROWS, n), F32),
    ]
    out_specs = [tok(D_GMLP), tok_t(D_MLSTM), tok(D_MLSTM), tok_t(D_MLSTM), tok(D_MLSTM),
                 tok(N_UNIT), tok_t(GR_ROWS)]
    return pl.pallas_call(
        functools.partial(_inproj_kernel, seg=seg),
        out_shape=out_shape,
        grid=(nb,),
        in_specs=in_specs,
        out_specs=out_specs,
        compiler_params=pltpu.CompilerParams(
            dimension_semantics=("arbitrary",), vmem_limit_bytes=VMEM_LIMIT_BYTES),
        name="inproj",
    )(x, mod, p["g1"], p["wu"], p["wv"], p["wqk"], p["wvvt"], p["wo"], p["wg"], p["wgt"],
      p["bg"], p["bgt"], p["cw"], p["cb"], p["ws"], p["bs"], p["gv"])


def _mlstm_unit(k, qt, vt, dcol, b_row, g_row, bl_row, gmax_row, st, m_row, mask):
    logw = jnp.where(mask, dcol + b_row, NEG)
    a = b_row + m_row
    mj = jnp.maximum(a, jnp.max(logw, axis=0, keepdims=True))
    w = jnp.exp(logw - mj)
    inter = jnp.exp(a - mj)
    s = _dot(k, qt) * w
    qs = _dot(st.astype(BF16), qt)
    num = _dot(vt, s.astype(BF16)) + inter * qs[:HD]
    den = jnp.sum(s, axis=0, keepdims=True) + inter * qs[HD:HD + 1]
    h = num * (1.0 / jnp.maximum(jnp.abs(den), jnp.exp(-mj)))
    m_new = jnp.maximum(bl_row + m_row, gmax_row)
    wc = jnp.exp(g_row - m_new)
    decay = jnp.exp(bl_row + m_row - m_new)
    pad_row = lax.broadcasted_iota(jnp.int32, (ST_ROWS - HD, CHUNK), 0)
    v_aug = jnp.concatenate(
        [vt.astype(F32) * wc, jnp.where(pad_row == 0, wc, 0.0)], axis=0).astype(BF16)
    st_new = decay * st + _dot(v_aug, k)
    return h, st_new, m_new


def _mlstm_kernel(*refs, has_init):
    if has_init:
        s0_ref, m0_ref = refs[:2]
        refs = refs[2:]
    (kf_ref, qtf_ref, vtf_ref, gcf_ref, grf_ref,
     kb_ref, qtb_ref, vtb_ref, gcb_ref, grb_ref,
     hf_ref, hb_ref, so_ref, mo_ref, s_ref, m_ref) = refs
    c = pl.program_id(1)
    nc = pl.num_programs(1)

    @pl.when(c == 0)
    def _():
        if has_init:
            s_ref[...] = s0_ref[0]
            m_ref[...] = m0_ref[0]
        else:
            s_ref[...] = jnp.zeros_like(s_ref)
            m_ref[...] = jnp.zeros_like(m_ref)

    si = lax.broadcasted_iota(jnp.int32, (CHUNK, CHUNK), 0)
    ji = lax.broadcasted_iota(jnp.int32, (CHUNK, CHUNK), 1)
    dirs = (
        (kf_ref, qtf_ref, vtf_ref, gcf_ref, grf_ref, hf_ref, si <= ji),
        (kb_ref, qtb_ref, vtb_ref, gcb_ref, grb_ref, hb_ref, si >= ji),
    )
    m_all = m_ref[...]
    m_news = []
    for d, (k_ref, qt_ref, vt_ref, gc_ref, gr_ref, h_ref, mask) in enumerate(dirs):
        for hd in range(HEADS):
            u = d * HEADS + hd
            hs = slice(hd * HD, (hd + 1) * HD)
            h, st_new, m_new = _mlstm_unit(
                k_ref[:, hs], qt_ref[hs, :], vt_ref[hs, :],
                gc_ref[:, u:u + 1],
                gr_ref[u:u + 1, :],
                gr_ref[N_UNIT + u:N_UNIT + u + 1, :],
                gr_ref[2 * N_UNIT + u:2 * N_UNIT + u + 1, :],
                gr_ref[3 * N_UNIT + u:3 * N_UNIT + u + 1, :],
                s_ref[u], m_all[u:u + 1], mask)
            h_ref[hs, :] = h
            s_ref[u] = st_new
            m_news.append(m_new)
    m_next = jnp.concatenate(m_news, axis=0)
    m_ref[...] = m_next

    @pl.when(c == nc - 1)
    def _():
        so_ref[0] = s_ref[...]
        mo_ref[0] = m_next


def _mlstm(k, qt, vt, gc, gr, batch, s0=None, m0=None):
    n = k.shape[0]
    nc = n // batch // CHUNK
    fwd = lambda b, c: b * nc + c
    bwd = lambda b, c: b * nc + nc - 1 - c

    def specs(ix):
        return [
            pl.BlockSpec((CHUNK, D_MLSTM), lambda b, c: (ix(b, c), 0)),
            pl.BlockSpec((D_MLSTM, CHUNK), lambda b, c: (0, ix(b, c))),
            pl.BlockSpec((D_MLSTM, CHUNK), lambda b, c: (0, ix(b, c))),
            pl.BlockSpec((CHUNK, N_UNIT), lambda b, c: (ix(b, c), 0)),
            pl.BlockSpec((GR_ROWS, CHUNK), lambda b, c: (0, ix(b, c))),
        ]

    in_specs = specs(fwd) + specs(bwd)
    args = [k, qt, vt, gc, gr, k, qt, vt, gc, gr]
    has_init = s0 is not None
    state_specs = [
        pl.BlockSpec((1, N_UNIT, ST_ROWS, HD), lambda b, c: (b, 0, 0, 0)),
        pl.BlockSpec((1, N_UNIT, CHUNK), lambda b, c: (b, 0, 0)),
    ]
    if has_init:
        in_specs = state_specs + in_specs
        args = [s0, m0] + args
    out_shape = [
        jax.ShapeDtypeStruct((D_MLSTM, n), F32),
        jax.ShapeDtypeStruct((D_MLSTM, n), F32),
        jax.ShapeDtypeStruct((batch, N_UNIT, ST_ROWS, HD), F32),
        jax.ShapeDtypeStruct((batch, N_UNIT, CHUNK), F32),
    ]
    out_specs = [
        pl.BlockSpec((D_MLSTM, CHUNK), lambda b, c: (0, fwd(b, c))),
        pl.BlockSpec((D_MLSTM, CHUNK), lambda b, c: (0, bwd(b, c))),
    ] + state_specs
    return pl.pallas_call(
        functools.partial(_mlstm_kernel, has_init=has_init),
        out_shape=out_shape,
        grid=(batch, nc),
        in_specs=in_specs,
        out_specs=out_specs,
        scratch_shapes=[pltpu.VMEM((N_UNIT, ST_ROWS, HD), F32), pltpu.VMEM((N_UNIT, CHUNK), F32)],
        compiler_params=pltpu.CompilerParams(
            dimension_semantics=("arbitrary", "arbitrary"), vmem_limit_bytes=VMEM_LIMIT_BYTES),
        name="mlstm",
    )(*args)


def _outffn_kernel(x_ref, a_ref, hft_ref, hbt_ref, o_ref, mod_ref, gh_ref, woa_ref, wob_ref, g2_ref,
                   w1_ref, w3_ref, w2_ref, gf_ref, y_ref):
    x = x_ref[...]
    mod = mod_ref[0]
    ga1 = mod[:, 2 * D_MODEL:3 * D_MODEL]
    sh2 = mod[:, 3 * D_MODEL:4 * D_MODEL]
    sc2 = mod[:, 4 * D_MODEL:5 * D_MODEL]
    ga2 = mod[:, 5 * D_MODEL:6 * D_MODEL]

    hs = jnp.transpose(hft_ref[...] + hbt_ref[...])
    sig = _sigmoid(o_ref[...])
    parts = []
    for hd in range(HEADS):
        sl = slice(hd * HD, (hd + 1) * HD)
        parts.append(_rms(hs[:, sl], gh_ref[:, sl]) * sig[:, sl])
    b_out = jnp.concatenate(parts, axis=-1).astype(BF16)
    mix = _dot(a_ref[...], woa_ref[...]) + _dot(b_out, wob_ref[...])
    x1 = x + ga1 * mix

    h2 = (_rms(x1, g2_ref[...]) * (1.0 + sc2) + sh2).astype(BF16)
    u = _dot(h2, w1_ref[...])
    g = _dot(h2, w3_ref[...])
    f = (u * _sigmoid(u) * g).astype(BF16)
    x2 = x1 + ga2 * _dot(f, w2_ref[...])
    y_ref[...] = _rms(x2, gf_ref[...])


def _outffn(x, a, hft, hbt, o, mod, blocks_per_mod, p):
    n = x.shape[0]
    tb = TOKEN_BLOCK
    d_ff = p["w1"].shape[1]
    tok = lambda w: pl.BlockSpec((tb, w), lambda i: (i, 0))
    tok_t = lambda h: pl.BlockSpec((h, tb), lambda i: (0, i))
    in_specs = [
        tok(D_MODEL), tok(D_GMLP), tok_t(D_MLSTM), tok_t(D_MLSTM), tok(D_MLSTM),
        pl.BlockSpec((1, 1, 6 * D_MODEL), lambda i: (i // blocks_per_mod, 0, 0)),
        _const_spec((1, D_MLSTM)),
        _const_spec((D_GMLP, D_MODEL)),
        _const_spec((D_MLSTM, D_MODEL)),
        _const_spec((1, D_MODEL)),
        _const_spec((D_MODEL, d_ff)),
        _const_spec((D_MODEL, d_ff)),
        _const_spec((d_ff, D_MODEL)),
        _const_spec((1, D_MODEL)),
    ]
    return pl.pallas_call(
        _outffn_kernel,
        out_shape=jax.ShapeDtypeStruct((n, D_MODEL), F32),
        grid=(n // tb,),
        in_specs=in_specs,
        out_specs=tok(D_MODEL),
        compiler_params=pltpu.CompilerParams(
            dimension_semantics=("arbitrary",), vmem_limit_bytes=VMEM_LIMIT_BYTES),
        name="outffn",
    )(x, a, hft, hbt, o, mod, p["gh"], p["woa"], p["wob"], p["g2"], p["w1"], p["w3"], p["w2"], p["gf"])


def _regroup_gates(g):
    h = HEADS
    return jnp.concatenate([g[:, 0:h], g[:, 2 * h:3 * h], g[:, h:2 * h], g[:, 3 * h:4 * h]], axis=1)


def _layer_params(l, g_norm1, w_in, b_gate, w_s, b_s, g_v, conv_w, conv_b, g_h, w_out, g_norm2,
                  w1, w3, w2, g_final):
    wi = w_in[l].astype(BF16)
    off_v, off_q, off_vv, off_o, off_g = 512, 1024, 2048, 2560, 3072
    wg = _regroup_gates(wi[:, off_g:])
    bg = _regroup_gates(b_gate[l][None, :])
    return {
        "g1": g_norm1[l][None, :],
        "wu": wi[:, :off_v], "wv": wi[:, off_v:off_q], "wqk": wi[:, off_q:off_vv],
        "wvvt": wi[:, off_vv:off_o].T, "wo": wi[:, off_o:off_g],
        "wg": wg, "wgt": wg.T, "bg": bg, "bgt": bg.T,
        "cw": conv_w[l], "cb": conv_b[l][None, :],
        "ws": w_s[l].astype(BF16),
        "bs": jnp.repeat(b_s[l].T, HD, axis=1),
        "gv": g_v[l].reshape(1, D_GMLP),
        "gh": g_h[l].reshape(1, D_MLSTM),
        "woa": w_out[l][:D_GMLP].astype(BF16), "wob": w_out[l][D_GMLP:].astype(BF16),
        "g2": g_norm2[l][None, :],
        "w1": w1[l].astype(BF16), "w3": w3[l].astype(BF16), "w2": w2[l].astype(BF16),
        "gf": g_final[None, :],
    }


def _trunk(x, mod, blocks_per_mod, seg, batch, p, s0=None, m0=None):
    a, qt, k, vt, o, gc, gr = _inproj(x, mod, blocks_per_mod, seg, p)
    hft, hbt, s_out, m_out = _mlstm(k, qt, vt, gc, gr, batch, s0, m0)
    y = _outffn(x, a, hft, hbt, o, mod, blocks_per_mod, p)
    return y, s_out, m_out


def kernel(x_prompt, x_sample, state_C, state_n, state_m, c, c_ctx, w_ada, b_ada, g_norm1, w_in,
           b_gate, w_s, b_s, g_v, conv_w, conv_b, g_h, w_out, g_norm2, w1, w3, w2, g_final):
    bp, tp, d = x_prompt.shape
    bs_, ts, _ = x_sample.shape
    depth = w_in.shape[0]
    assert depth == 1, "final norm is fused into the layer's last kernel"
    xp = x_prompt.reshape(bp * tp, d)
    xs = x_sample.reshape(bs_ * ts, d)

    cs = jnp.zeros((8, d), F32).at[0].set(c_ctx).at[1:1 + bs_].set(c)
    new_c, new_n, new_m = [], [], []
    for l in range(depth):
        p = _layer_params(l, g_norm1, w_in, b_gate, w_s, b_s, g_v, conv_w, conv_b, g_h, w_out,
                          g_norm2, w1, w3, w2, g_final)
        mod = _ada(cs, w_ada[l], b_ada[l][None, :])
        mod_ctx = mod[0:1].reshape(1, 1, 6 * d)
        mod_lat = mod[1:1 + bs_].reshape(bs_, 1, 6 * d)

        xp, s_ctx, m_ctx = _trunk(xp, mod_ctx, (bp * tp) // TOKEN_BLOCK, tp, bp, p)
        new_c.append(jnp.swapaxes(s_ctx[:, :, :HD, :], -1, -2).reshape(bp, N_DIR, HEADS, HD, HD))
        new_n.append(s_ctx[:, :, HD, :].reshape(bp, N_DIR, HEADS, HD))
        new_m.append(m_ctx[..., 0].reshape(bp, N_DIR, HEADS))

        s0 = jnp.concatenate(
            [jnp.swapaxes(state_C[:, l], -1, -2), state_n[:, l][..., None, :],
             jnp.zeros((bs_, N_DIR, HEADS, ST_ROWS - HD - 1, HD), F32)],
            axis=-2).reshape(bs_, N_UNIT, ST_ROWS, HD)
        m0 = jnp.broadcast_to(state_m[:, l].reshape(bs_, N_UNIT, 1), (bs_, N_UNIT, CHUNK))
        xs, _, _ = _trunk(xs, mod_lat, ts // TOKEN_BLOCK, GRID_W, bs_, p, s0, m0)

    return (xp.reshape(bp, tp, d), xs.reshape(bs_, ts, d),
            jnp.stack(new_c, axis=1), jnp.stack(new_n, axis=1), jnp.stack(new_m, axis=1))
```

```python
import functools

import jax
import jax.numpy as jnp
from jax import lax
from jax.experimental import pallas as pl
from jax.experimental.pallas import tpu as pltpu

D_MODEL = 1024
D_GMLP = 512
D_MLSTM = 512
GROUPS = 4
HEADS = 4
HD = 128
CHUNK = 128
N_DIR = 2
N_UNIT = N_DIR * HEADS
GRID_W = 64
EPS = 1e-6
NEG = -1e30
ST_ROWS = HD + 16
GR_ROWS = 4 * N_UNIT

TOKEN_BLOCK = 256
VMEM_LIMIT_BYTES = 56 * 1024 * 1024

F32 = jnp.float32
BF16 = jnp.bfloat16


def _rms(x, g):
    return x * lax.rsqrt(jnp.mean(x * x, axis=-1, keepdims=True) + EPS) * g


def _sigmoid(x):
    return 1.0 / (1.0 + jnp.exp(-x))


def _gelu_tanh(x):
    return 0.5 * x * (1.0 + jnp.tanh(0.7978845608028654 * (x + 0.044715 * (x * x * x))))


def _log_sigmoid(x):
    return jnp.minimum(x, 0.0) - jnp.log(1.0 + jnp.exp(-jnp.abs(x)))


def _dot(a, b):
    return jnp.dot(a, b, preferred_element_type=F32)


def _dot_nt(a, b):
    return lax.dot_general(a, b, (((1,), (1,)), ((), ())), preferred_element_type=F32)


def _dot_exact(a, b):
    return jnp.dot(a, b, preferred_element_type=F32, precision=lax.Precision.HIGHEST)


def _const_spec(shape):
    zeros = (0,) * len(shape)
    return pl.BlockSpec(shape, lambda *_: zeros, pipeline_mode=pl.Buffered(1))


def _ada_kernel(c_ref, w_ref, b_ref, o_ref):
    c = c_ref[...]
    s = (c * _sigmoid(c)).astype(BF16)
    o_ref[...] = _dot(s, w_ref[...].astype(BF16)) + b_ref[...]


def _ada(cs, w_ada, b_ada):
    rows, d = cs.shape
    n = w_ada.shape[1]
    tn = 1024
    return pl.pallas_call(
        _ada_kernel,
        out_shape=jax.ShapeDtypeStruct((rows, n), F32),
        grid=(n // tn,),
        in_specs=[
            pl.BlockSpec((rows, d), lambda j: (0, 0)),
            pl.BlockSpec((d, tn), lambda j: (0, j)),
            pl.BlockSpec((1, tn), lambda j: (0, j)),
        ],
        out_specs=pl.BlockSpec((rows, tn), lambda j: (0, j)),
        compiler_params=pltpu.CompilerParams(
            dimension_semantics=("arbitrary",), vmem_limit_bytes=VMEM_LIMIT_BYTES),
        name="ada",
    )(cs, w_ada, b_ada)


def _inproj_norm(x, mod, g1, hb_ref):
    sh1 = mod[:, 0:D_MODEL]
    sc1 = mod[:, D_MODEL:2 * D_MODEL]
    hb_ref[...] = (_rms(x, g1) * (1.0 + sc1) + sh1).astype(BF16)


def _inproj_project(hb_ref, w, z):
    hb = hb_ref[...]
    z["u"][...] = _dot(hb, w["u"][...])
    z["v"][...] = _dot(hb, w["v"][...])
    z["qk"][...] = _dot(hb, w["qk"][...])
    z["o"][...] = _dot(hb, w["o"][...])
    z["vt"][...] = _dot_nt(w["vvt"][...], hb).astype(BF16)
    z["gt"][...] = _dot_nt(w["gt"][...], hb)


def _inproj_finish(z, c, out, seg):
    tb = z["u"].shape[0]
    for g in range(GROUPS):
        gs = slice(g * HD, (g + 1) * HD)
        vg = _rms(_gelu_tanh(z["v"][:, gs]), c["gv"][:, gs]).astype(BF16)
        for ch in range(tb // CHUNK):
            cs = slice(ch * CHUNK, (ch + 1) * CHUNK)
            mixed = _dot(c["ws"][g], vg[cs]) + c["bs"][:, gs]
            out["a"][cs, gs] = (_gelu_tanh(z["u"][cs, gs]) * mixed).astype(BF16)

    zqk = z["qk"][...]
    pos = lax.broadcasted_iota(jnp.int32, (tb, 1), 0) % seg
    prev = jnp.where(pos != 0, pltpu.roll(zqk, 1, 0), 0.0)
    nxt = jnp.where(pos != seg - 1, pltpu.roll(zqk, tb - 1, 0), 0.0)
    cw = c["cw"]
    y = c["cb"][...] + prev * cw[0:1, :] + zqk * cw[1:2, :] + nxt * cw[2:3, :]
    y = y * _sigmoid(y)
    out["qt"][...] = jnp.transpose(y[:, :D_MLSTM]).astype(BF16)
    out["k"][...] = (y[:, D_MLSTM:] * (HD ** -0.5)).astype(BF16)
    out["vt"][...] = z["vt"][...]
    out["o"][...] = z["o"][...]

    gt = z["gt"][...] + c["bgt"][...]
    gi_t = gt[0:N_UNIT]
    lf_t = _log_sigmoid(gt[N_UNIT:2 * N_UNIT])
    lane = lax.broadcasted_iota(jnp.int32, (N_UNIT, CHUNK), 1)
    row_bwd = lax.broadcasted_iota(jnp.int32, (N_UNIT, CHUNK), 0) >= HEADS
    pad = jnp.zeros((CHUNK - N_UNIT, CHUNK), F32)
    for ch in range(tb // CHUNK):
        cs = slice(ch * CHUNK, (ch + 1) * CHUNK)
        lf_c = lf_t[:, cs]
        prefix = lf_c
        shift = 1
        while shift < CHUNK:
            prefix = prefix + jnp.where(lane >= shift, pltpu.roll(prefix, shift, 1), 0.0)
            shift *= 2
        b_last = jnp.broadcast_to(prefix[:, CHUNK - 1:CHUNK], (N_UNIT, CHUNK))
        b_row = jnp.where(row_bwd, b_last - prefix + lf_c, prefix)
        g_row = b_last - b_row + gi_t[:, cs]
        g_max = jnp.broadcast_to(jnp.max(g_row, axis=1, keepdims=True), (N_UNIT, CHUNK))
        out["gr"][0 * N_UNIT:1 * N_UNIT, cs] = b_row
        out["gr"][1 * N_UNIT:2 * N_UNIT, cs] = g_row
        out["gr"][2 * N_UNIT:3 * N_UNIT, cs] = b_last
        out["gr"][3 * N_UNIT:4 * N_UNIT, cs] = g_max
        d_row = jnp.concatenate([gi_t[:, cs] - b_row, pad], axis=0)
        out["gc"][cs, :] = jnp.transpose(d_row)[:, 0:N_UNIT]


_INPROJ_Z = ("u", "v", "qk", "o", "vt", "gt")


def _inproj_kernel(x_ref, mod_ref, g1_ref, wu_ref, wv_ref, wqk_ref, wvvt_ref, wo_ref,
                   wgt_ref, bgt_ref, cw_ref, cb_ref, ws_ref, bs_ref, gv_ref,
                   a_ref, qt_ref, k_ref, vt_ref, o_ref, gc_ref, gr_ref,
                   hb_s, *z_s, seg):
    tb = x_ref.shape[0] // 2
    w = {"u": wu_ref, "v": wv_ref, "qk": wqk_ref, "o": wo_ref, "vvt": wvvt_ref, "gt": wgt_ref}
    c = {"gv": gv_ref, "ws": ws_ref, "bs": bs_ref, "cw": cw_ref, "cb": cb_ref, "bgt": bgt_ref}

    @pl.when(pl.program_id(0) == 0)
    def _():
        hb_s[...] = jnp.zeros_like(hb_s)
        for ref in z_s:
            ref[...] = jnp.zeros_like(ref)

    mod = mod_ref[0]
    for half in range(2):
        rows = slice(half * tb, (half + 1) * tb)
        cur, oth = half, 1 - half
        z_cur = {name: ref.at[cur] for name, ref in zip(_INPROJ_Z, z_s)}
        z_oth = {name: ref.at[oth] for name, ref in zip(_INPROJ_Z, z_s)}
        out = {"a": a_ref.at[rows, :], "qt": qt_ref.at[:, rows], "k": k_ref.at[rows, :],
               "vt": vt_ref.at[:, rows], "o": o_ref.at[rows, :], "gc": gc_ref.at[rows, :],
               "gr": gr_ref.at[:, rows]}
        _inproj_project(hb_s.at[oth], w, z_oth)
        _inproj_norm(x_ref[rows, :], mod, g1_ref[...], hb_s.at[cur])
        _inproj_finish(z_cur, c, out, seg)


def _inproj(x, mod, rows_per_mod, seg, p):
    n = x.shape[0]
    tb = TOKEN_BLOCK
    tb2 = 2 * tb
    assert tb % seg == 0 and n % tb2 == 0 and rows_per_mod % tb2 == 0
    nb = n // tb2
    blk_in = lambda i: jnp.minimum(i, nb - 1)
    blk_out = lambda i: jnp.maximum(i - 1, 0)
    tok = lambda width: pl.BlockSpec((tb2, width), lambda i: (blk_out(i), 0))
    tok_t = lambda height: pl.BlockSpec((height, tb2), lambda i: (0, blk_out(i)))
    in_specs = [
        pl.BlockSpec((tb2, D_MODEL), lambda i: (blk_in(i), 0)),
        pl.BlockSpec((1, 1, 6 * D_MODEL), lambda i: (blk_in(i) * tb2 // rows_per_mod, 0, 0)),
        _const_spec((1, D_MODEL)),
        _const_spec((D_MODEL, D_GMLP)),
        _const_spec((D_MODEL, D_GMLP)),
        _const_spec((D_MODEL, 2 * D_MLSTM)),
        _const_spec((D_MLSTM, D_MODEL)),
        _const_spec((D_MODEL, D_MLSTM)),
        _const_spec((2 * N_UNIT, D_MODEL)),
        _const_spec((2 * N_UNIT, 1)),
        _const_spec((3, 2 * D_MLSTM)),
        _const_spec((1, 2 * D_MLSTM)),
        _const_spec((GROUPS, CHUNK, CHUNK)),
        _const_spec((CHUNK, D_GMLP)),
        _const_spec((1, D_GMLP)),
    ]
    out_shape = [
        jax.ShapeDtypeStruct((n, D_GMLP), BF16),
        jax.ShapeDtypeStruct((D_MLSTM, n), BF16),
        jax.ShapeDtypeStruct((n, D_MLSTM), BF16),
        jax.ShapeDtypeStruct((D_MLSTM, n), BF16),
        jax.ShapeDtypeStruct((n, D_MLSTM), F32),
        jax.ShapeDtypeStruct((n, N_UNIT), F32),
        jax.ShapeDtypeStruct((GR_ROWS, n), F32),
    ]
    out_specs = [tok(D_GMLP), tok_t(D_MLSTM), tok(D_MLSTM), tok_t(D_MLSTM), tok(D_MLSTM),
                 tok(N_UNIT), tok_t(GR_ROWS)]
    scratch_shapes = [
        pltpu.VMEM((2, tb, D_MODEL), BF16),
        pltpu.VMEM((2, tb, D_GMLP), F32),
        pltpu.VMEM((2, tb, D_GMLP), F32),
        pltpu.VMEM((2, tb, 2 * D_MLSTM), F32),
        pltpu.VMEM((2, tb, D_MLSTM), F32),
        pltpu.VMEM((2, D_MLSTM, tb), BF16),
        pltpu.VMEM((2, 2 * N_UNIT, tb), F32),
    ]
    return pl.pallas_call(
        functools.partial(_inproj_kernel, seg=seg),
        out_shape=out_shape,
        grid=(nb + 1,),
        in_specs=in_specs,
        out_specs=out_specs,
        scratch_shapes=scratch_shapes,
        compiler_params=pltpu.CompilerParams(
            dimension_semantics=("arbitrary",), vmem_limit_bytes=VMEM_LIMIT_BYTES),
        name="inproj",
    )(x, mod, p["g1"], p["wu"], p["wv"], p["wqk"], p["wvvt"], p["wo"], p["wgt"],
      p["bgt"], p["cw"], p["cb"], p["ws"], p["bs"], p["gv"])


def _mlstm_unit(k, qt, vt, dcol, b_row, g_row, bl_row, gmax_row, st, m_row, mask):
    logw = jnp.where(mask, dcol + b_row, NEG)
    a = b_row + m_row
    mj = jnp.maximum(a, jnp.max(logw, axis=0, keepdims=True))
    w = jnp.exp(logw - mj)
    inter = jnp.exp(a - mj)
    s = _dot(k, qt) * w
    qs = _dot(st.astype(BF16), qt)
    num = _dot(vt, s.astype(BF16)) + inter * qs[:HD]
    den = jnp.sum(s, axis=0, keepdims=True) + inter * qs[HD:HD + 1]
    h = num * (1.0 / jnp.maximum(jnp.abs(den), jnp.exp(-mj)))
    m_new = jnp.maximum(bl_row + m_row, gmax_row)
    wc = jnp.exp(g_row - m_new)
    decay = jnp.exp(bl_row + m_row - m_new)
    pad_row = lax.broadcasted_iota(jnp.int32, (ST_ROWS - HD, CHUNK), 0)
    v_aug = jnp.concatenate(
        [vt.astype(F32) * wc, jnp.where(pad_row == 0, wc, 0.0)], axis=0).astype(BF16)
    st_new = decay * st + _dot(v_aug, k)
    return h, st_new, m_new


def _mlstm_kernel(*refs, has_init):
    if has_init:
        s0_ref, m0_ref = refs[:2]
        refs = refs[2:]
    (kf_ref, qtf_ref, vtf_ref, gcf_ref, grf_ref,
     kb_ref, qtb_ref, vtb_ref, gcb_ref, grb_ref,
     hf_ref, hb_ref, so_ref, mo_ref, s_ref, m_ref) = refs
    c = pl.program_id(1)
    nc = pl.num_programs(1)

    @pl.when(c == 0)
    def _():
        if has_init:
            s_ref[...] = s0_ref[0]
            m_ref[...] = m0_ref[0]
        else:
            s_ref[...] = jnp.zeros_like(s_ref)
            m_ref[...] = jnp.zeros_like(m_ref)

    si = lax.broadcasted_iota(jnp.int32, (CHUNK, CHUNK), 0)
    ji = lax.broadcasted_iota(jnp.int32, (CHUNK, CHUNK), 1)
    dirs = (
        (kf_ref, qtf_ref, vtf_ref, gcf_ref, grf_ref, hf_ref, si <= ji),
        (kb_ref, qtb_ref, vtb_ref, gcb_ref, grb_ref, hb_ref, si >= ji),
    )
    m_all = m_ref[...]
    m_news = []
    for d, (k_ref, qt_ref, vt_ref, gc_ref, gr_ref, h_ref, mask) in enumerate(dirs):
        for hd in range(HEADS):
            u = d * HEADS + hd
            hs = slice(hd * HD, (hd + 1) * HD)
            h, st_new, m_new = _mlstm_unit(
                k_ref[:, hs], qt_ref[hs, :], vt_ref[hs, :],
                gc_ref[:, u:u + 1],
                gr_ref[u:u + 1, :],
                gr_ref[N_UNIT + u:N_UNIT + u + 1, :],
                gr_ref[2 * N_UNIT + u:2 * N_UNIT + u + 1, :],
                gr_ref[3 * N_UNIT + u:3 * N_UNIT + u + 1, :],
                s_ref[u], m_all[u:u + 1], mask)
            h_ref[hs, :] = h
            s_ref[u] = st_new
            m_news.append(m_new)
    m_next = jnp.concatenate(m_news, axis=0)
    m_ref[...] = m_next

    @pl.when(c == nc - 1)
    def _():
        so_ref[0] = s_ref[...]
        mo_ref[0] = m_next


def _mlstm(k, qt, vt, gc, gr, batch, s0=None, m0=None):
    n = k.shape[0]
    nc = n // batch // CHUNK
    fwd = lambda b, c: b * nc + c
    bwd = lambda b, c: b * nc + nc - 1 - c

    def specs(ix):
        return [
            pl.BlockSpec((CHUNK, D_MLSTM), lambda b, c: (ix(b, c), 0)),
            pl.BlockSpec((D_MLSTM, CHUNK), lambda b, c: (0, ix(b, c))),
            pl.BlockSpec((D_MLSTM, CHUNK), lambda b, c: (0, ix(b, c))),
            pl.BlockSpec((CHUNK, N_UNIT), lambda b, c: (ix(b, c), 0)),
            pl.BlockSpec((GR_ROWS, CHUNK), lambda b, c: (0, ix(b, c))),
        ]

    in_specs = specs(fwd) + specs(bwd)
    args = [k, qt, vt, gc, gr, k, qt, vt, gc, gr]
    has_init = s0 is not None
    state_specs = [
        pl.BlockSpec((1, N_UNIT, ST_ROWS, HD), lambda b, c: (b, 0, 0, 0)),
        pl.BlockSpec((1, N_UNIT, CHUNK), lambda b, c: (b, 0, 0)),
    ]
    if has_init:
        in_specs = state_specs + in_specs
        args = [s0, m0] + args
    out_shape = [
        jax.ShapeDtypeStruct((D_MLSTM, n), F32),
        jax.ShapeDtypeStruct((D_MLSTM, n), F32),
        jax.ShapeDtypeStruct((batch, N_UNIT, ST_ROWS, HD), F32),
        jax.ShapeDtypeStruct((batch, N_UNIT, CHUNK), F32),
    ]
    out_specs = [
        pl.BlockSpec((D_MLSTM, CHUNK), lambda b, c: (0, fwd(b, c))),
        pl.BlockSpec((D_MLSTM, CHUNK), lambda b, c: (0, bwd(b, c))),
    ] + state_specs
    return pl.pallas_call(
        functools.partial(_mlstm_kernel, has_init=has_init),
        out_shape=out_shape,
        grid=(batch, nc),
        in_specs=in_specs,
        out_specs=out_specs,
        scratch_shapes=[pltpu.VMEM((N_UNIT, ST_ROWS, HD), F32), pltpu.VMEM((N_UNIT, CHUNK), F32)],
        compiler_params=pltpu.CompilerParams(
            dimension_semantics=("arbitrary", "arbitrary"), vmem_limit_bytes=VMEM_LIMIT_BYTES),
        name="mlstm",
    )(*args)


def _outffn_kernel(x_ref, a_ref, hft_ref, hbt_ref, o_ref, mod_ref, gh_ref, woa_ref, wob_ref, g2_ref,
                   w1_ref, w3_ref, w2_ref, gf_ref, y_ref):
    x = x_ref[...]
    mod = mod_ref[0]
    ga1 = mod[:, 2 * D_MODEL:3 * D_MODEL]
    sh2 = mod[:, 3 * D_MODEL:4 * D_MODEL]
    sc2 = mod[:, 4 * D_MODEL:5 * D_MODEL]
    ga2 = mod[:, 5 * D_MODEL:6 * D_MODEL]

    hs = jnp.transpose(hft_ref[...] + hbt_ref[...])
    sig = _sigmoid(o_ref[...])
    parts = []
    for hd in range(HEADS):
        sl = slice(hd * HD, (hd + 1) * HD)
        parts.append(_rms(hs[:, sl], gh_ref[:, sl]) * sig[:, sl])
    b_out = jnp.concatenate(parts, axis=-1).astype(BF16)
    mix = _dot(a_ref[...], woa_ref[...]) + _dot(b_out, wob_ref[...])
    x1 = x + ga1 * mix

    h2 = (_rms(x1, g2_ref[...]) * (1.0 + sc2) + sh2).astype(BF16)
    u = _dot(h2, w1_ref[...])
    g = _dot(h2, w3_ref[...])
    f = (u * _sigmoid(u) * g).astype(BF16)
    x2 = x1 + ga2 * _dot(f, w2_ref[...])
    y_ref[...] = _rms(x2, gf_ref[...])


def _outffn(x, a, hft, hbt, o, mod, blocks_per_mod, p):
    n = x.shape[0]
    tb = TOKEN_BLOCK
    d_ff = p["w1"].shape[1]
    tok = lambda w: pl.BlockSpec((tb, w), lambda i: (i, 0))
    tok_t = lambda h: pl.BlockSpec((h, tb), lambda i: (0, i))
    in_specs = [
        tok(D_MODEL), tok(D_GMLP), tok_t(D_MLSTM), tok_t(D_MLSTM), tok(D_MLSTM),
        pl.BlockSpec((1, 1, 6 * D_MODEL), lambda i: (i // blocks_per_mod, 0, 0)),
        _const_spec((1, D_MLSTM)),
        _const_spec((D_GMLP, D_MODEL)),
        _const_spec((D_MLSTM, D_MODEL)),
        _const_spec((1, D_MODEL)),
        _const_spec((D_MODEL, d_ff)),
        _const_spec((D_MODEL, d_ff)),
        _const_spec((d_ff, D_MODEL)),
        _const_spec((1, D_MODEL)),
    ]
    return pl.pallas_call(
        _outffn_kernel,
        out_shape=jax.ShapeDtypeStruct((n, D_MODEL), F32),
        grid=(n // tb,),
        in_specs=in_specs,
        out_specs=tok(D_MODEL),
        compiler_params=pltpu.CompilerParams(
            dimension_semantics=("arbitrary",), vmem_limit_bytes=VMEM_LIMIT_BYTES),
        name="outffn",
    )(x, a, hft, hbt, o, mod, p["gh"], p["woa"], p["wob"], p["g2"], p["w1"], p["w3"], p["w2"], p["gf"])


def _regroup_gates(g):
    h = HEADS
    return jnp.concatenate([g[:, 0:h], g[:, 2 * h:3 * h], g[:, h:2 * h], g[:, 3 * h:4 * h]], axis=1)


def _layer_params(l, g_norm1, w_in, b_gate, w_s, b_s, g_v, conv_w, conv_b, g_h, w_out, g_norm2,
                  w1, w3, w2, g_final):
    wi = w_in[l].astype(BF16)
    off_v, off_q, off_vv, off_o, off_g = 512, 1024, 2048, 2560, 3072
    wg = _regroup_gates(wi[:, off_g:])
    bg = _regroup_gates(b_gate[l][None, :])
    return {
        "g1": g_norm1[l][None, :],
        "wu": wi[:, :off_v], "wv": wi[:, off_v:off_q], "wqk": wi[:, off_q:off_vv],
        "wvvt": wi[:, off_vv:off_o].T, "wo": wi[:, off_o:off_g],
        "wgt": wg.T, "bgt": bg.T,
        "cw": conv_w[l], "cb": conv_b[l][None, :],
        "ws": w_s[l].astype(BF16),
        "bs": jnp.repeat(b_s[l].T, HD, axis=1),
        "gv": g_v[l].reshape(1, D_GMLP),
        "gh": g_h[l].reshape(1, D_MLSTM),
        "woa": w_out[l][:D_GMLP].astype(BF16), "wob": w_out[l][D_GMLP:].astype(BF16),
        "g2": g_norm2[l][None, :],
        "w1": w1[l].astype(BF16), "w3": w3[l].astype(BF16), "w2": w2[l].astype(BF16),
        "gf": g_final[None, :],
    }


def _trunk(x, mod, rows_per_mod, seg, batch, p, s0=None, m0=None):
    a, qt, k, vt, o, gc, gr = _inproj(x, mod, rows_per_mod, seg, p)
    hft, hbt, s_out, m_out = _mlstm(k, qt, vt, gc, gr, batch, s0, m0)
    y = _outffn(x, a, hft, hbt, o, mod, rows_per_mod // TOKEN_BLOCK, p)
    return y, s_out, m_out


def kernel(x_prompt, x_sample, state_C, state_n, state_m, c, c_ctx, w_ada, b_ada, g_norm1, w_in,
           b_gate, w_s, b_s, g_v, conv_w, conv_b, g_h, w_out, g_norm2, w1, w3, w2, g_final):
    bp, tp, d = x_prompt.shape
    bs_, ts, _ = x_sample.shape
    depth = w_in.shape[0]
    assert depth == 1, "final norm is fused into the layer's last kernel"
    xp = x_prompt.reshape(bp * tp, d)
    xs = x_sample.reshape(bs_ * ts, d)

    cs = jnp.zeros((8, d), F32).at[0].set(c_ctx).at[1:1 + bs_].set(c)
    new_c, new_n, new_m = [], [], []
    for l in range(depth):
        p = _layer_params(l, g_norm1, w_in, b_gate, w_s, b_s, g_v, conv_w, conv_b, g_h, w_out,
                          g_norm2, w1, w3, w2, g_final)
        mod = _ada(cs, w_ada[l], b_ada[l][None, :])
        mod_ctx = mod[0:1].reshape(1, 1, 6 * d)
        mod_lat = mod[1:1 + bs_].reshape(bs_, 1, 6 * d)

        xp, s_ctx, m_ctx = _trunk(xp, mod_ctx, bp * tp, tp, bp, p)
        new_c.append(jnp.swapaxes(s_ctx[:, :, :HD, :], -1, -2).reshape(bp, N_DIR, HEADS, HD, HD))
        new_n.append(s_ctx[:, :, HD, :].reshape(bp, N_DIR, HEADS, HD))
        new_m.append(m_ctx[..., 0].reshape(bp, N_DIR, HEADS))

        s0 = jnp.concatenate(
            [jnp.swapaxes(state_C[:, l], -1, -2), state_n[:, l][..., None, :],
             jnp.zeros((bs_, N_DIR, HEADS, ST_ROWS - HD - 1, HD), F32)],
            axis=-2).reshape(bs_, N_UNIT, ST_ROWS, HD)
        m0 = jnp.broadcast_to(state_m[:, l].reshape(bs_, N_UNIT, 1), (bs_, N_UNIT, CHUNK))
        xs, _, _ = _trunk(xs, mod_lat, ts, GRID_W, bs_, p, s0, m0)

    return (xp.reshape(bp, tp, d), xs.reshape(bs_, ts, d),
            jnp.stack(new_c, axis=1), jnp.stack(new_n, axis=1), jnp.stack(new_m, axis=1))
```

```python
import functools

import jax
import jax.numpy as jnp
from jax import lax
from jax.experimental import pallas as pl
from jax.experimental.pallas import tpu as pltpu

D_MODEL = 1024
D_GMLP = 512
D_MLSTM = 512
GROUPS = 4
HEADS = 4
HD = 128
CHUNK = 128
SCAN = 256
N_DIR = 2
N_UNIT = N_DIR * HEADS
GRID_W = 64
EPS = 1e-6
NEG = -1e30
ST_ROWS = HD + 16
GR_ROWS = 4 * N_UNIT

TOKEN_BLOCK = 256
VMEM_LIMIT_BYTES = 56 * 1024 * 1024

F32 = jnp.float32
BF16 = jnp.bfloat16


def _rms(x, g):
    return x * lax.rsqrt(jnp.mean(x * x, axis=-1, keepdims=True) + EPS) * g


def _sigmoid(x):
    return 1.0 / (1.0 + jnp.exp(-x))


def _gelu_tanh(x):
    return 0.5 * x * (1.0 + jnp.tanh(0.7978845608028654 * (x + 0.044715 * (x * x * x))))


def _log_sigmoid(x):
    return jnp.minimum(x, 0.0) - jnp.log(1.0 + jnp.exp(-jnp.abs(x)))


def _dot(a, b):
    return jnp.dot(a, b, preferred_element_type=F32)


def _dot_nt(a, b):
    return lax.dot_general(a, b, (((1,), (1,)), ((), ())), preferred_element_type=F32)


def _dot_exact(a, b):
    return jnp.dot(a, b, preferred_element_type=F32, precision=lax.Precision.HIGHEST)


def _const_spec(shape):
    zeros = (0,) * len(shape)
    return pl.BlockSpec(shape, lambda *_: zeros, pipeline_mode=pl.Buffered(1))


def _ada_kernel(c_ref, w_ref, b_ref, o_ref):
    c = c_ref[...]
    s = (c * _sigmoid(c)).astype(BF16)
    o_ref[...] = _dot(s, w_ref[...].astype(BF16)) + b_ref[...]


def _ada(cs, w_ada, b_ada):
    rows, d = cs.shape
    n = w_ada.shape[1]
    tn = 1024
    return pl.pallas_call(
        _ada_kernel,
        out_shape=jax.ShapeDtypeStruct((rows, n), F32),
        grid=(n // tn,),
        in_specs=[
            pl.BlockSpec((rows, d), lambda j: (0, 0)),
            pl.BlockSpec((d, tn), lambda j: (0, j)),
            pl.BlockSpec((1, tn), lambda j: (0, j)),
        ],
        out_specs=pl.BlockSpec((rows, tn), lambda j: (0, j)),
        compiler_params=pltpu.CompilerParams(
            dimension_semantics=("arbitrary",), vmem_limit_bytes=VMEM_LIMIT_BYTES),
        name="ada",
    )(cs, w_ada, b_ada)


def _inproj_norm(x, mod, g1, hb_ref):
    sh1 = mod[:, 0:D_MODEL]
    sc1 = mod[:, D_MODEL:2 * D_MODEL]
    hb_ref[...] = (_rms(x, g1) * (1.0 + sc1) + sh1).astype(BF16)


def _inproj_project(hb_ref, w, z):
    hb = hb_ref[...]
    z["u"][...] = _dot(hb, w["u"][...])
    z["v"][...] = _dot(hb, w["v"][...])
    z["qk"][...] = _dot(hb, w["qk"][...])
    z["o"][...] = _dot(hb, w["o"][...])
    z["vt"][...] = _dot_nt(w["vvt"][...], hb).astype(BF16)
    z["gt"][...] = _dot_nt(w["gt"][...], hb)


def _inproj_finish(z, c, out, seg):
    tb = z["u"].shape[0]
    for g in range(GROUPS):
        gs = slice(g * HD, (g + 1) * HD)
        vg = _rms(_gelu_tanh(z["v"][:, gs]), c["gv"][:, gs]).astype(BF16)
        for ch in range(tb // CHUNK):
            cs = slice(ch * CHUNK, (ch + 1) * CHUNK)
            mixed = _dot(c["ws"][g], vg[cs]) + c["bs"][:, gs]
            out["a"][cs, gs] = (_gelu_tanh(z["u"][cs, gs]) * mixed).astype(BF16)

    zqk = z["qk"][...]
    pos = lax.broadcasted_iota(jnp.int32, (tb, 1), 0) % seg
    prev = jnp.where(pos != 0, pltpu.roll(zqk, 1, 0), 0.0)
    nxt = jnp.where(pos != seg - 1, pltpu.roll(zqk, tb - 1, 0), 0.0)
    cw = c["cw"]
    y = c["cb"][...] + prev * cw[0:1, :] + zqk * cw[1:2, :] + nxt * cw[2:3, :]
    y = y * _sigmoid(y)
    out["qt"][...] = jnp.transpose(y[:, :D_MLSTM]).astype(BF16)
    out["k"][...] = (y[:, D_MLSTM:] * (HD ** -0.5)).astype(BF16)
    out["vt"][...] = z["vt"][...]
    out["o"][...] = z["o"][...]

    assert tb == SCAN
    gt = z["gt"][...] + c["bgt"][...]
    gi_t = gt[0:N_UNIT]
    lf_t = _log_sigmoid(gt[N_UNIT:2 * N_UNIT])
    lane = lax.broadcasted_iota(jnp.int32, (N_UNIT, HD), 1)
    tiles = []
    carry = jnp.zeros((N_UNIT, HD), F32)
    for t in range(SCAN // HD):
        prefix = lf_t[:, t * HD:(t + 1) * HD]
        shift = 1
        while shift < HD:
            prefix = prefix + jnp.where(lane >= shift, pltpu.roll(prefix, shift, 1), 0.0)
            shift *= 2
        tiles.append(prefix + carry)
        carry = carry + jnp.broadcast_to(prefix[:, HD - 1:HD], (N_UNIT, HD))
    prefix = jnp.concatenate(tiles, axis=1)
    b_last = jnp.concatenate([carry] * (SCAN // HD), axis=1)
    row_bwd = lax.broadcasted_iota(jnp.int32, (N_UNIT, SCAN), 0) >= HEADS
    b_row = jnp.where(row_bwd, b_last - prefix + lf_t, prefix)
    g_row = b_last - b_row + gi_t
    g_max = jnp.broadcast_to(jnp.max(g_row, axis=1, keepdims=True), (N_UNIT, SCAN))
    out["gr"][0 * N_UNIT:1 * N_UNIT, :] = b_row
    out["gr"][1 * N_UNIT:2 * N_UNIT, :] = g_row
    out["gr"][2 * N_UNIT:3 * N_UNIT, :] = b_last
    out["gr"][3 * N_UNIT:4 * N_UNIT, :] = g_max
    d_row = gi_t - b_row
    pad = jnp.zeros((HD - N_UNIT, HD), F32)
    for t in range(SCAN // HD):
        ts = slice(t * HD, (t + 1) * HD)
        d_tile = jnp.concatenate([d_row[:, ts], pad], axis=0)
        out["gc"][ts, :] = jnp.transpose(d_tile)[:, 0:N_UNIT]


_INPROJ_Z = ("u", "v", "qk", "o", "vt", "gt")


def _inproj_kernel(x_ref, mod_ref, g1_ref, wu_ref, wv_ref, wqk_ref, wvvt_ref, wo_ref,
                   wgt_ref, bgt_ref, cw_ref, cb_ref, ws_ref, bs_ref, gv_ref,
                   a_ref, qt_ref, k_ref, vt_ref, o_ref, gc_ref, gr_ref,
                   hb_s, *z_s, seg):
    tb = x_ref.shape[0] // 2
    w = {"u": wu_ref, "v": wv_ref, "qk": wqk_ref, "o": wo_ref, "vvt": wvvt_ref, "gt": wgt_ref}
    c = {"gv": gv_ref, "ws": ws_ref, "bs": bs_ref, "cw": cw_ref, "cb": cb_ref, "bgt": bgt_ref}

    @pl.when(pl.program_id(0) == 0)
    def _():
        hb_s[...] = jnp.zeros_like(hb_s)
        for ref in z_s:
            ref[...] = jnp.zeros_like(ref)

    mod = mod_ref[0]
    for half in range(2):
        rows = slice(half * tb, (half + 1) * tb)
        cur, oth = half, 1 - half
        z_cur = {name: ref.at[cur] for name, ref in zip(_INPROJ_Z, z_s)}
        z_oth = {name: ref.at[oth] for name, ref in zip(_INPROJ_Z, z_s)}
        out = {"a": a_ref.at[rows, :], "qt": qt_ref.at[:, rows], "k": k_ref.at[rows, :],
               "vt": vt_ref.at[:, rows], "o": o_ref.at[rows, :], "gc": gc_ref.at[rows, :],
               "gr": gr_ref.at[:, rows]}
        _inproj_project(hb_s.at[oth], w, z_oth)
        _inproj_norm(x_ref[rows, :], mod, g1_ref[...], hb_s.at[cur])
        _inproj_finish(z_cur, c, out, seg)


def _inproj(x, mod, rows_per_mod, seg, p):
    n = x.shape[0]
    tb = TOKEN_BLOCK
    tb2 = 2 * tb
    assert tb % seg == 0 and n % tb2 == 0 and rows_per_mod % tb2 == 0
    nb = n // tb2
    blk_in = lambda i: jnp.minimum(i, nb - 1)
    blk_out = lambda i: jnp.maximum(i - 1, 0)
    tok = lambda width: pl.BlockSpec((tb2, width), lambda i: (blk_out(i), 0))
    tok_t = lambda height: pl.BlockSpec((height, tb2), lambda i: (0, blk_out(i)))
    in_specs = [
        pl.BlockSpec((tb2, D_MODEL), lambda i: (blk_in(i), 0)),
        pl.BlockSpec((1, 1, 6 * D_MODEL), lambda i: (blk_in(i) * tb2 // rows_per_mod, 0, 0)),
        _const_spec((1, D_MODEL)),
        _const_spec((D_MODEL, D_GMLP)),
        _const_spec((D_MODEL, D_GMLP)),
        _const_spec((D_MODEL, 2 * D_MLSTM)),
        _const_spec((D_MLSTM, D_MODEL)),
        _const_spec((D_MODEL, D_MLSTM)),
        _const_spec((2 * N_UNIT, D_MODEL)),
        _const_spec((2 * N_UNIT, 1)),
        _const_spec((3, 2 * D_MLSTM)),
        _const_spec((1, 2 * D_MLSTM)),
        _const_spec((GROUPS, CHUNK, CHUNK)),
        _const_spec((CHUNK, D_GMLP)),
        _const_spec((1, D_GMLP)),
    ]
    out_shape = [
        jax.ShapeDtypeStruct((n, D_GMLP), BF16),
        jax.ShapeDtypeStruct((D_MLSTM, n), BF16),
        jax.ShapeDtypeStruct((n, D_MLSTM), BF16),
        jax.ShapeDtypeStruct((D_MLSTM, n), BF16),
        jax.ShapeDtypeStruct((n, D_MLSTM), F32),
        jax.ShapeDtypeStruct((n, N_UNIT), F32),
        jax.ShapeDtypeStruct((GR_ROWS, n), F32),
    ]
    out_specs = [tok(D_GMLP), tok_t(D_MLSTM), tok(D_MLSTM), tok_t(D_MLSTM), tok(D_MLSTM),
                 tok(N_UNIT), tok_t(GR_ROWS)]
    scratch_shapes = [
        pltpu.VMEM((2, tb, D_MODEL), BF16),
        pltpu.VMEM((2, tb, D_GMLP), F32),
        pltpu.VMEM((2, tb, D_GMLP), F32),
        pltpu.VMEM((2, tb, 2 * D_MLSTM), F32),
        pltpu.VMEM((2, tb, D_MLSTM), F32),
        pltpu.VMEM((2, D_MLSTM, tb), BF16),
        pltpu.VMEM((2, 2 * N_UNIT, tb), F32),
    ]
    return pl.pallas_call(
        functools.partial(_inproj_kernel, seg=seg),
        out_shape=out_shape,
        grid=(nb + 1,),
        in_specs=in_specs,
        out_specs=out_specs,
        scratch_shapes=scratch_shapes,
        compiler_params=pltpu.CompilerParams(
            dimension_semantics=("arbitrary",), vmem_limit_bytes=VMEM_LIMIT_BYTES),
        name="inproj",
    )(x, mod, p["g1"], p["wu"], p["wv"], p["wqk"], p["wvvt"], p["wo"], p["wgt"],
      p["bgt"], p["cw"], p["cb"], p["ws"], p["bs"], p["gv"])


def _mlstm_unit(k, vt, kq, qs, dcol, b_row, g_row, bl_row, gmax_row, st, m_row, mask, carried):
    logw = jnp.where(mask, dcol + b_row, NEG)
    a = b_row + m_row
    mj = jnp.maximum(a, jnp.max(logw, axis=0, keepdims=True))
    w = jnp.exp(logw - mj)
    s = kq * w
    num = _dot(vt, s.astype(BF16))
    den = jnp.sum(s, axis=0, keepdims=True)
    if carried:
        inter = jnp.exp(a - mj)
        num = num + inter * qs[:HD]
        den = den + inter * qs[HD:HD + 1]
    h = num * (1.0 / jnp.maximum(jnp.abs(den), jnp.exp(-mj)))
    m_new = jnp.maximum(bl_row + m_row, gmax_row)
    wc = jnp.exp(g_row - m_new)
    pad_row = lax.broadcasted_iota(jnp.int32, (ST_ROWS - HD, wc.shape[1]), 0)
    v_aug = jnp.concatenate(
        [vt.astype(F32) * wc, jnp.where(pad_row == 0, wc, 0.0)], axis=0).astype(BF16)
    st_new = _dot(v_aug, k)
    if carried:
        decay = jnp.exp(bl_row + m_row - m_new)
        st_new = decay[:, :HD] * st + st_new
    return h, st_new, m_new


def _mlstm_kernel(*refs, has_init, has_out, carried):
    if has_init:
        s0_ref, m0_ref = refs[:2]
        refs = refs[2:]
    (kf_ref, qtf_ref, vtf_ref, gcf_ref, grf_ref,
     kb_ref, qtb_ref, vtb_ref, gcb_ref, grb_ref, hf_ref, hb_ref) = refs[:12]
    refs = refs[12:]
    if has_out:
        co_ref, no_ref, mo_ref = refs[:3]
        refs = refs[3:]
    s_ref, m_ref = refs
    c = pl.program_id(1)
    nc = pl.num_programs(1)

    @pl.when(c == 0)
    def _():
        if has_init:
            s_ref[...] = s0_ref[0]
            m_ref[...] = m0_ref[0]
        else:
            s_ref[...] = jnp.zeros_like(s_ref)
            m_ref[...] = jnp.zeros_like(m_ref)

    si = lax.broadcasted_iota(jnp.int32, (SCAN, SCAN), 0)
    ji = lax.broadcasted_iota(jnp.int32, (SCAN, SCAN), 1)
    dirs = (
        (kf_ref, qtf_ref, vtf_ref, gcf_ref, grf_ref, hf_ref, si <= ji),
        (kb_ref, qtb_ref, vtb_ref, gcb_ref, grb_ref, hb_ref, si >= ji),
    )
    m_all = m_ref[...]
    first = {}
    for d, (k_ref, qt_ref, _, _, _, _, _) in enumerate(dirs):
        for hd in range(HEADS):
            u = d * HEADS + hd
            hs = slice(hd * HD, (hd + 1) * HD)
            kq = _dot(k_ref[:, hs], qt_ref[hs, :])
            qs = _dot(s_ref[u].astype(BF16), qt_ref[hs, :]) if carried else None
            first[u] = (kq, qs)
    m_news = []
    for d, (k_ref, qt_ref, vt_ref, gc_ref, gr_ref, h_ref, mask) in enumerate(dirs):
        for hd in range(HEADS):
            u = d * HEADS + hd
            hs = slice(hd * HD, (hd + 1) * HD)
            h, st_new, m_new = _mlstm_unit(
                k_ref[:, hs], vt_ref[hs, :], *first[u],
                gc_ref[:, u:u + 1],
                gr_ref[u:u + 1, :],
                gr_ref[N_UNIT + u:N_UNIT + u + 1, :],
                gr_ref[2 * N_UNIT + u:2 * N_UNIT + u + 1, :],
                gr_ref[3 * N_UNIT + u:3 * N_UNIT + u + 1, :],
                s_ref[u], m_all[u:u + 1], mask, carried)
            h_ref[hs, :] = h
            s_ref[u] = st_new
            m_news.append(m_new)
    m_next = jnp.concatenate(m_news, axis=0)
    m_ref[...] = m_next

    if has_out:
        @pl.when(c == nc - 1)
        def _():
            for u in range(N_UNIT):
                co_ref[0, u] = jnp.transpose(s_ref[u, 0:HD, :])
                no_ref[0, u:u + 1, :] = s_ref[u, HD:HD + 1, :]
            mo_ref[0] = m_next


def _mlstm(k, qt, vt, gc, gr, batch, s0=None, m0=None, want_state=False):
    n = k.shape[0]
    nc = n // batch // SCAN
    fwd = lambda b, c: b * nc + c
    bwd = lambda b, c: b * nc + nc - 1 - c

    def specs(ix):
        return [
            pl.BlockSpec((SCAN, D_MLSTM), lambda b, c: (ix(b, c), 0)),
            pl.BlockSpec((D_MLSTM, SCAN), lambda b, c: (0, ix(b, c))),
            pl.BlockSpec((D_MLSTM, SCAN), lambda b, c: (0, ix(b, c))),
            pl.BlockSpec((SCAN, N_UNIT), lambda b, c: (ix(b, c), 0)),
            pl.BlockSpec((GR_ROWS, SCAN), lambda b, c: (0, ix(b, c))),
        ]

    in_specs = specs(fwd) + specs(bwd)
    args = [k, qt, vt, gc, gr, k, qt, vt, gc, gr]
    has_init = s0 is not None
    if has_init:
        in_specs = [
            pl.BlockSpec((1, N_UNIT, ST_ROWS, HD), lambda b, c: (b, 0, 0, 0)),
            pl.BlockSpec((1, N_UNIT, SCAN), lambda b, c: (b, 0, 0)),
        ] + in_specs
        args = [s0, m0] + args
    out_shape = [
        jax.ShapeDtypeStruct((D_MLSTM, n), F32),
        jax.ShapeDtypeStruct((D_MLSTM, n), F32),
    ]
    out_specs = [
        pl.BlockSpec((D_MLSTM, SCAN), lambda b, c: (0, fwd(b, c))),
        pl.BlockSpec((D_MLSTM, SCAN), lambda b, c: (0, bwd(b, c))),
    ]
    if want_state:
        out_shape += [
            jax.ShapeDtypeStruct((batch, N_UNIT, HD, HD), F32),
            jax.ShapeDtypeStruct((batch, N_UNIT, HD), F32),
            jax.ShapeDtypeStruct((batch, N_UNIT, SCAN), F32),
        ]
        out_specs += [
            pl.BlockSpec((1, N_UNIT, HD, HD), lambda b, c: (b, 0, 0, 0)),
            pl.BlockSpec((1, N_UNIT, HD), lambda b, c: (b, 0, 0)),
            pl.BlockSpec((1, N_UNIT, SCAN), lambda b, c: (b, 0, 0)),
        ]
    return pl.pallas_call(
        functools.partial(_mlstm_kernel, has_init=has_init, has_out=want_state,
                          carried=has_init or nc > 1),
        out_shape=out_shape,
        grid=(batch, nc),
        in_specs=in_specs,
        out_specs=out_specs,
        scratch_shapes=[pltpu.VMEM((N_UNIT, ST_ROWS, HD), F32), pltpu.VMEM((N_UNIT, SCAN), F32)],
        compiler_params=pltpu.CompilerParams(
            dimension_semantics=("arbitrary", "arbitrary"), vmem_limit_bytes=VMEM_LIMIT_BYTES),
        name="mlstm",
    )(*args)


def _outffn_kernel(x_ref, a_ref, hft_ref, hbt_ref, o_ref, mod_ref, gh_ref, woa_ref, wob_ref, g2_ref,
                   w1_ref, w3_ref, w2_ref, gf_ref, y_ref):
    x = x_ref[...]
    mod = mod_ref[0]
    ga1 = mod[:, 2 * D_MODEL:3 * D_MODEL]
    sh2 = mod[:, 3 * D_MODEL:4 * D_MODEL]
    sc2 = mod[:, 4 * D_MODEL:5 * D_MODEL]
    ga2 = mod[:, 5 * D_MODEL:6 * D_MODEL]

    hs = jnp.transpose(hft_ref[...] + hbt_ref[...])
    sig = _sigmoid(o_ref[...])
    parts = []
    for hd in range(HEADS):
        sl = slice(hd * HD, (hd + 1) * HD)
        parts.append(_rms(hs[:, sl], gh_ref[:, sl]) * sig[:, sl])
    b_out = jnp.concatenate(parts, axis=-1).astype(BF16)
    mix = _dot(a_ref[...], woa_ref[...]) + _dot(b_out, wob_ref[...])
    x1 = x + ga1 * mix

    h2 = (_rms(x1, g2_ref[...]) * (1.0 + sc2) + sh2).astype(BF16)
    u = _dot(h2, w1_ref[...])
    g = _dot(h2, w3_ref[...])
    f = (u * _sigmoid(u) * g).astype(BF16)
    x2 = x1 + ga2 * _dot(f, w2_ref[...])
    y_ref[...] = _rms(x2, gf_ref[...])


def _outffn(x, a, hft, hbt, o, mod, blocks_per_mod, p):
    n = x.shape[0]
    tb = TOKEN_BLOCK
    d_ff = p["w1"].shape[1]
    tok = lambda w: pl.BlockSpec((tb, w), lambda i: (i, 0))
    tok_t = lambda h: pl.BlockSpec((h, tb), lambda i: (0, i))
    in_specs = [
        tok(D_MODEL), tok(D_GMLP), tok_t(D_MLSTM), tok_t(D_MLSTM), tok(D_MLSTM),
        pl.BlockSpec((1, 1, 6 * D_MODEL), lambda i: (i // blocks_per_mod, 0, 0)),
        _const_spec((1, D_MLSTM)),
        _const_spec((D_GMLP, D_MODEL)),
        _const_spec((D_MLSTM, D_MODEL)),
        _const_spec((1, D_MODEL)),
        _const_spec((D_MODEL, d_ff)),
        _const_spec((D_MODEL, d_ff)),
        _const_spec((d_ff, D_MODEL)),
        _const_spec((1, D_MODEL)),
    ]
    return pl.pallas_call(
        _outffn_kernel,
        out_shape=jax.ShapeDtypeStruct((n, D_MODEL), F32),
        grid=(n // tb,),
        in_specs=in_specs,
        out_specs=tok(D_MODEL),
        compiler_params=pltpu.CompilerParams(
            dimension_semantics=("arbitrary",), vmem_limit_bytes=VMEM_LIMIT_BYTES),
        name="outffn",
    )(x, a, hft, hbt, o, mod, p["gh"], p["woa"], p["wob"], p["g2"], p["w1"], p["w3"], p["w2"], p["gf"])


def _regroup_gates(g):
    h = HEADS
    return jnp.concatenate([g[:, 0:h], g[:, 2 * h:3 * h], g[:, h:2 * h], g[:, 3 * h:4 * h]], axis=1)


def _layer_params(l, g_norm1, w_in, b_gate, w_s, b_s, g_v, conv_w, conv_b, g_h, w_out, g_norm2,
                  w1, w3, w2, g_final):
    wi = w_in[l].astype(BF16)
    off_v, off_q, off_vv, off_o, off_g = 512, 1024, 2048, 2560, 3072
    wg = _regroup_gates(wi[:, off_g:])
    bg = _regroup_gates(b_gate[l][None, :])
    return {
        "g1": g_norm1[l][None, :],
        "wu": wi[:, :off_v], "wv": wi[:, off_v:off_q], "wqk": wi[:, off_q:off_vv],
        "wvvt": wi[:, off_vv:off_o].T, "wo": wi[:, off_o:off_g],
        "wgt": wg.T, "bgt": bg.T,
        "cw": conv_w[l], "cb": conv_b[l][None, :],
        "ws": w_s[l].astype(BF16),
        "bs": jnp.repeat(b_s[l].T, HD, axis=1),
        "gv": g_v[l].reshape(1, D_GMLP),
        "gh": g_h[l].reshape(1, D_MLSTM),
        "woa": w_out[l][:D_GMLP].astype(BF16), "wob": w_out[l][D_GMLP:].astype(BF16),
        "g2": g_norm2[l][None, :],
        "w1": w1[l].astype(BF16), "w3": w3[l].astype(BF16), "w2": w2[l].astype(BF16),
        "gf": g_final[None, :],
    }


def _trunk(x, mod, rows_per_mod, seg, batch, p, s0=None, m0=None, want_state=False):
    a, qt, k, vt, o, gc, gr = _inproj(x, mod, rows_per_mod, seg, p)
    hft, hbt, *state = _mlstm(k, qt, vt, gc, gr, batch, s0, m0, want_state)
    y = _outffn(x, a, hft, hbt, o, mod, rows_per_mod // TOKEN_BLOCK, p)
    return y, state


def kernel(x_prompt, x_sample, state_C, state_n, state_m, c, c_ctx, w_ada, b_ada, g_norm1, w_in,
           b_gate, w_s, b_s, g_v, conv_w, conv_b, g_h, w_out, g_norm2, w1, w3, w2, g_final):
    bp, tp, d = x_prompt.shape
    bs_, ts, _ = x_sample.shape
    depth = w_in.shape[0]
    assert depth == 1, "final norm is fused into the layer's last kernel"
    xp = x_prompt.reshape(bp * tp, d)
    xs = x_sample.reshape(bs_ * ts, d)

    cs = jnp.zeros((8, d), F32).at[0].set(c_ctx).at[1:1 + bs_].set(c)
    new_c, new_n, new_m = [], [], []
    for l in range(depth):
        p = _layer_params(l, g_norm1, w_in, b_gate, w_s, b_s, g_v, conv_w, conv_b, g_h, w_out,
                          g_norm2, w1, w3, w2, g_final)
        mod = _ada(cs, w_ada[l], b_ada[l][None, :])
        mod_ctx = mod[0:1].reshape(1, 1, 6 * d)
        mod_lat = mod[1:1 + bs_].reshape(bs_, 1, 6 * d)

        xp, (c_ctx_out, n_ctx_out, m_ctx_out) = _trunk(xp, mod_ctx, bp * tp, tp, bp, p,
                                                       want_state=True)
        new_c.append(c_ctx_out.reshape(bp, N_DIR, HEADS, HD, HD))
        new_n.append(n_ctx_out.reshape(bp, N_DIR, HEADS, HD))
        new_m.append(m_ctx_out[..., 0].reshape(bp, N_DIR, HEADS))

        s0 = jnp.concatenate(
            [jnp.swapaxes(state_C[:, l], -1, -2), state_n[:, l][..., None, :],
             jnp.zeros((bs_, N_DIR, HEADS, ST_ROWS - HD - 1, HD), F32)],
            axis=-2).reshape(bs_, N_UNIT, ST_ROWS, HD)
        m0 = jnp.broadcast_to(state_m[:, l].reshape(bs_, N_UNIT, 1), (bs_, N_UNIT, SCAN))
        xs, _ = _trunk(xs, mod_lat, ts, GRID_W, bs_, p, s0, m0)

    return (xp.reshape(bp, tp, d), xs.reshape(bs_, ts, d),
            jnp.stack(new_c, axis=1), jnp.stack(new_n, axis=1), jnp.stack(new_m, axis=1))
```

```python
import functools

import jax
import jax.numpy as jnp
from jax import lax
from jax.experimental import pallas as pl
from jax.experimental.pallas import tpu as pltpu

D_MODEL = 1024
D_GMLP = 512
D_MLSTM = 512
GROUPS = 4
HEADS = 4
HD = 128
CHUNK = 128
SCAN = 256
N_DIR = 2
N_UNIT = N_DIR * HEADS
GRID_W = 64
EPS = 1e-6
NEG = -1e30
ST_ROWS = HD + 16
GR_ROWS = 4 * N_UNIT

TOKEN_BLOCK = 256
VMEM_LIMIT_BYTES = 56 * 1024 * 1024

F32 = jnp.float32
BF16 = jnp.bfloat16


def _rms(x, g):
    return x * lax.rsqrt(jnp.mean(x * x, axis=-1, keepdims=True) + EPS) * g


def _sigmoid(x):
    return 1.0 / (1.0 + jnp.exp(-x))


def _gelu_tanh(x):
    return 0.5 * x * (1.0 + jnp.tanh(0.7978845608028654 * (x + 0.044715 * (x * x * x))))


def _log_sigmoid(x):
    return jnp.minimum(x, 0.0) - jnp.log(1.0 + jnp.exp(-jnp.abs(x)))


def _dot(a, b):
    return jnp.dot(a, b, preferred_element_type=F32)


def _dot_nt(a, b):
    return lax.dot_general(a, b, (((1,), (1,)), ((), ())), preferred_element_type=F32)


def _dot_exact(a, b):
    return jnp.dot(a, b, preferred_element_type=F32, precision=lax.Precision.HIGHEST)


def _const_spec(shape):
    zeros = (0,) * len(shape)
    return pl.BlockSpec(shape, lambda *_: zeros, pipeline_mode=pl.Buffered(1))


def _ada_kernel(c_ref, w_ref, b_ref, o_ref):
    c = c_ref[...]
    s = (c * _sigmoid(c)).astype(BF16)
    o_ref[...] = _dot(s, w_ref[...].astype(BF16)) + b_ref[...]


def _ada(cs, w_ada, b_ada):
    rows, d = cs.shape
    n = w_ada.shape[1]
    tn = 1024
    return pl.pallas_call(
        _ada_kernel,
        out_shape=jax.ShapeDtypeStruct((rows, n), F32),
        grid=(n // tn,),
        in_specs=[
            pl.BlockSpec((rows, d), lambda j: (0, 0)),
            pl.BlockSpec((d, tn), lambda j: (0, j)),
            pl.BlockSpec((1, tn), lambda j: (0, j)),
        ],
        out_specs=pl.BlockSpec((rows, tn), lambda j: (0, j)),
        compiler_params=pltpu.CompilerParams(
            dimension_semantics=("arbitrary",), vmem_limit_bytes=VMEM_LIMIT_BYTES),
        name="ada",
    )(cs, w_ada, b_ada)


def _inproj_norm(x, mod, g1, hb_ref):
    sh1 = mod[:, 0:D_MODEL]
    sc1 = mod[:, D_MODEL:2 * D_MODEL]
    hb_ref[...] = (_rms(x, g1) * (1.0 + sc1) + sh1).astype(BF16)


def _inproj_project(hb_ref, w, z):
    hb = hb_ref[...]
    z["u"][...] = _dot(hb, w["u"][...])
    z["v"][...] = _dot(hb, w["v"][...])
    z["qk"][...] = _dot(hb, w["qk"][...])
    z["o"][...] = _dot(hb, w["o"][...])
    z["vt"][...] = _dot_nt(w["vvt"][...], hb).astype(BF16)
    z["gt"][...] = _dot_nt(w["gt"][...], hb)


def _inproj_finish(z, c, out, seg):
    tb = z["u"].shape[0]
    for g in range(GROUPS):
        gs = slice(g * HD, (g + 1) * HD)
        vg = _rms(_gelu_tanh(z["v"][:, gs]), c["gv"][:, gs]).astype(BF16)
        for ch in range(tb // CHUNK):
            cs = slice(ch * CHUNK, (ch + 1) * CHUNK)
            mixed = _dot(c["ws"][g], vg[cs]) + c["bs"][:, gs]
            out["a"][cs, gs] = (_gelu_tanh(z["u"][cs, gs]) * mixed).astype(BF16)

    zqk = z["qk"][...]
    pos = lax.broadcasted_iota(jnp.int32, (tb, 1), 0) % seg
    prev = jnp.where(pos != 0, pltpu.roll(zqk, 1, 0), 0.0)
    nxt = jnp.where(pos != seg - 1, pltpu.roll(zqk, tb - 1, 0), 0.0)
    cw = c["cw"]
    y = c["cb"][...] + prev * cw[0:1, :] + zqk * cw[1:2, :] + nxt * cw[2:3, :]
    y = y * _sigmoid(y)
    out["qt"][...] = jnp.transpose(y[:, :D_MLSTM]).astype(BF16)
    out["k"][...] = (y[:, D_MLSTM:] * (HD ** -0.5)).astype(BF16)
    out["vt"][...] = z["vt"][...]
    out["o"][...] = z["o"][...]

    assert tb == SCAN
    gt = z["gt"][...] + c["bgt"][...]
    gi_t = gt[0:N_UNIT]
    lf_t = _log_sigmoid(gt[N_UNIT:2 * N_UNIT])
    lane = lax.broadcasted_iota(jnp.int32, (N_UNIT, HD), 1)
    tiles = []
    carry = jnp.zeros((N_UNIT, HD), F32)
    for t in range(SCAN // HD):
        prefix = lf_t[:, t * HD:(t + 1) * HD]
        shift = 1
        while shift < HD:
            prefix = prefix + jnp.where(lane >= shift, pltpu.roll(prefix, shift, 1), 0.0)
            shift *= 2
        tiles.append(prefix + carry)
        carry = carry + jnp.broadcast_to(prefix[:, HD - 1:HD], (N_UNIT, HD))
    prefix = jnp.concatenate(tiles, axis=1)
    b_last = jnp.concatenate([carry] * (SCAN // HD), axis=1)
    row_bwd = lax.broadcasted_iota(jnp.int32, (N_UNIT, SCAN), 0) >= HEADS
    b_row = jnp.where(row_bwd, b_last - prefix + lf_t, prefix)
    g_row = b_last - b_row + gi_t
    g_max = jnp.broadcast_to(jnp.max(g_row, axis=1, keepdims=True), (N_UNIT, SCAN))
    out["gr"][0 * N_UNIT:1 * N_UNIT, :] = b_row
    out["gr"][1 * N_UNIT:2 * N_UNIT, :] = g_row
    out["gr"][2 * N_UNIT:3 * N_UNIT, :] = b_last
    out["gr"][3 * N_UNIT:4 * N_UNIT, :] = g_max
    d_row = gi_t - b_row
    pad = jnp.zeros((HD - N_UNIT, HD), F32)
    for t in range(SCAN // HD):
        ts = slice(t * HD, (t + 1) * HD)
        d_tile = jnp.concatenate([d_row[:, ts], pad], axis=0)
        out["gc"][ts, :] = jnp.transpose(d_tile)[:, 0:N_UNIT]


_INPROJ_Z = ("u", "v", "qk", "o", "vt", "gt")


def _inproj_kernel(x_ref, mod_ref, g1_ref, wu_ref, wv_ref, wqk_ref, wvvt_ref, wo_ref,
                   wgt_ref, bgt_ref, cw_ref, cb_ref, ws_ref, bs_ref, gv_ref,
                   a_ref, qt_ref, k_ref, vt_ref, o_ref, gc_ref, gr_ref,
                   hb_s, *z_s, seg):
    tb = x_ref.shape[0] // 2
    w = {"u": wu_ref, "v": wv_ref, "qk": wqk_ref, "o": wo_ref, "vvt": wvvt_ref, "gt": wgt_ref}
    c = {"gv": gv_ref, "ws": ws_ref, "bs": bs_ref, "cw": cw_ref, "cb": cb_ref, "bgt": bgt_ref}

    @pl.when(pl.program_id(0) == 0)
    def _():
        hb_s[...] = jnp.zeros_like(hb_s)
        for ref in z_s:
            ref[...] = jnp.zeros_like(ref)

    mod = mod_ref[0]
    for half in range(2):
        rows = slice(half * tb, (half + 1) * tb)
        cur, oth = half, 1 - half
        z_cur = {name: ref.at[cur] for name, ref in zip(_INPROJ_Z, z_s)}
        z_oth = {name: ref.at[oth] for name, ref in zip(_INPROJ_Z, z_s)}
        out = {"a": a_ref.at[rows, :], "qt": qt_ref.at[:, rows], "k": k_ref.at[rows, :],
               "vt": vt_ref.at[:, rows], "o": o_ref.at[rows, :], "gc": gc_ref.at[rows, :],
               "gr": gr_ref.at[:, rows]}
        _inproj_project(hb_s.at[oth], w, z_oth)
        _inproj_norm(x_ref[rows, :], mod, g1_ref[...], hb_s.at[cur])
        _inproj_finish(z_cur, c, out, seg)


def _inproj(x, mod, rows_per_mod, seg, p):
    n = x.shape[0]
    tb = TOKEN_BLOCK
    tb2 = 2 * tb
    assert tb % seg == 0 and n % tb2 == 0 and rows_per_mod % tb2 == 0
    nb = n // tb2
    blk_in = lambda i: jnp.minimum(i, nb - 1)
    blk_out = lambda i: jnp.maximum(i - 1, 0)
    tok = lambda width: pl.BlockSpec((tb2, width), lambda i: (blk_out(i), 0))
    tok_t = lambda height: pl.BlockSpec((height, tb2), lambda i: (0, blk_out(i)))
    in_specs = [
        pl.BlockSpec((tb2, D_MODEL), lambda i: (blk_in(i), 0)),
        pl.BlockSpec((1, 1, 6 * D_MODEL), lambda i: (blk_in(i) * tb2 // rows_per_mod, 0, 0)),
        _const_spec((1, D_MODEL)),
        _const_spec((D_MODEL, D_GMLP)),
        _const_spec((D_MODEL, D_GMLP)),
        _const_spec((D_MODEL, 2 * D_MLSTM)),
        _const_spec((D_MLSTM, D_MODEL)),
        _const_spec((D_MODEL, D_MLSTM)),
        _const_spec((2 * N_UNIT, D_MODEL)),
        _const_spec((2 * N_UNIT, 1)),
        _const_spec((3, 2 * D_MLSTM)),
        _const_spec((1, 2 * D_MLSTM)),
        _const_spec((GROUPS, CHUNK, CHUNK)),
        _const_spec((CHUNK, D_GMLP)),
        _const_spec((1, D_GMLP)),
    ]
    out_shape = [
        jax.ShapeDtypeStruct((n, D_GMLP), BF16),
        jax.ShapeDtypeStruct((D_MLSTM, n), BF16),
        jax.ShapeDtypeStruct((n, D_MLSTM), BF16),
        jax.ShapeDtypeStruct((D_MLSTM, n), BF16),
        jax.ShapeDtypeStruct((n, D_MLSTM), F32),
        jax.ShapeDtypeStruct((n, N_UNIT), F32),
        jax.ShapeDtypeStruct((GR_ROWS, n), F32),
    ]
    out_specs = [tok(D_GMLP), tok_t(D_MLSTM), tok(D_MLSTM), tok_t(D_MLSTM), tok(D_MLSTM),
                 tok(N_UNIT), tok_t(GR_ROWS)]
    scratch_shapes = [
        pltpu.VMEM((2, tb, D_MODEL), BF16),
        pltpu.VMEM((2, tb, D_GMLP), F32),
        pltpu.VMEM((2, tb, D_GMLP), F32),
        pltpu.VMEM((2, tb, 2 * D_MLSTM), F32),
        pltpu.VMEM((2, tb, D_MLSTM), F32),
        pltpu.VMEM((2, D_MLSTM, tb), BF16),
        pltpu.VMEM((2, 2 * N_UNIT, tb), F32),
    ]
    return pl.pallas_call(
        functools.partial(_inproj_kernel, seg=seg),
        out_shape=out_shape,
        grid=(nb + 1,),
        in_specs=in_specs,
        out_specs=out_specs,
        scratch_shapes=scratch_shapes,
        compiler_params=pltpu.CompilerParams(
            dimension_semantics=("arbitrary",), vmem_limit_bytes=VMEM_LIMIT_BYTES),
        name="inproj",
    )(x, mod, p["g1"], p["wu"], p["wv"], p["wqk"], p["wvvt"], p["wo"], p["wgt"],
      p["bgt"], p["cw"], p["cb"], p["ws"], p["bs"], p["gv"])


def _mlstm_unit(k, vt, kq, qs, dcol, b_row, g_row, bl_row, gmax_row, st, m_row, mask, carried):
    logw = jnp.where(mask, dcol + b_row, NEG)
    a = b_row + m_row
    mj = jnp.maximum(a, jnp.max(logw, axis=0, keepdims=True))
    w = jnp.exp(logw - mj)
    s = kq * w
    num = _dot(vt, s.astype(BF16))
    den = jnp.sum(s, axis=0, keepdims=True)
    if carried:
        inter = jnp.exp(a - mj)
        num = num + inter * qs[:HD]
        den = den + inter * qs[HD:HD + 1]
    h = num * (1.0 / jnp.maximum(jnp.abs(den), jnp.exp(-mj)))
    m_new = jnp.maximum(bl_row + m_row, gmax_row)
    wc = jnp.exp(g_row - m_new)
    pad_row = lax.broadcasted_iota(jnp.int32, (ST_ROWS - HD, wc.shape[1]), 0)
    v_aug = jnp.concatenate(
        [vt.astype(F32) * wc, jnp.where(pad_row == 0, wc, 0.0)], axis=0).astype(BF16)
    st_new = _dot(v_aug, k)
    if carried:
        decay = jnp.exp(bl_row + m_row - m_new)
        st_new = decay[:, :HD] * st + st_new
    return h, st_new, m_new


def _mlstm_kernel(*refs, has_init, has_out, carried):
    if has_init:
        s0_ref, m0_ref = refs[:2]
        refs = refs[2:]
    (kf_ref, qtf_ref, vtf_ref, gcf_ref, grf_ref,
     kb_ref, qtb_ref, vtb_ref, gcb_ref, grb_ref, hf_ref, hb_ref) = refs[:12]
    refs = refs[12:]
    if has_out:
        co_ref, no_ref, mo_ref = refs[:3]
        refs = refs[3:]
    s_ref, m_ref = refs
    c = pl.program_id(1)
    nc = pl.num_programs(1)

    @pl.when(c == 0)
    def _():
        if has_init:
            s_ref[...] = s0_ref[0]
            m_ref[...] = m0_ref[0]
        else:
            s_ref[...] = jnp.zeros_like(s_ref)
            m_ref[...] = jnp.zeros_like(m_ref)

    si = lax.broadcasted_iota(jnp.int32, (SCAN, SCAN), 0)
    ji = lax.broadcasted_iota(jnp.int32, (SCAN, SCAN), 1)
    dirs = (
        (kf_ref, qtf_ref, vtf_ref, gcf_ref, grf_ref, hf_ref, si <= ji),
        (kb_ref, qtb_ref, vtb_ref, gcb_ref, grb_ref, hb_ref, si >= ji),
    )
    m_all = m_ref[...]
    first = {}
    for d, (k_ref, qt_ref, _, _, _, _, _) in enumerate(dirs):
        for hd in range(HEADS):
            u = d * HEADS + hd
            hs = slice(hd * HD, (hd + 1) * HD)
            kq = _dot(k_ref[:, hs], qt_ref[hs, :])
            qs = _dot(s_ref[u].astype(BF16), qt_ref[hs, :]) if carried else None
            first[u] = (kq, qs)
    m_news = []
    for d, (k_ref, qt_ref, vt_ref, gc_ref, gr_ref, h_ref, mask) in enumerate(dirs):
        for hd in range(HEADS):
            u = d * HEADS + hd
            hs = slice(hd * HD, (hd + 1) * HD)
            h, st_new, m_new = _mlstm_unit(
                k_ref[:, hs], vt_ref[hs, :], *first[u],
                gc_ref[:, u:u + 1],
                gr_ref[u:u + 1, :],
                gr_ref[N_UNIT + u:N_UNIT + u + 1, :],
                gr_ref[2 * N_UNIT + u:2 * N_UNIT + u + 1, :],
                gr_ref[3 * N_UNIT + u:3 * N_UNIT + u + 1, :],
                s_ref[u], m_all[u:u + 1], mask, carried)
            h_ref[hs, :] = h
            s_ref[u] = st_new
            m_news.append(m_new)
    m_next = jnp.concatenate(m_news, axis=0)
    m_ref[...] = m_next

    if has_out:
        @pl.when(c == nc - 1)
        def _():
            for u in range(N_UNIT):
                co_ref[0, u] = jnp.transpose(s_ref[u, 0:HD, :])
                no_ref[0, u:u + 1, :] = s_ref[u, HD:HD + 1, :]
            mo_ref[0] = m_next


def _mlstm(k, qt, vt, gc, gr, batch, s0=None, m0=None, want_state=False):
    n = k.shape[0]
    nc = n // batch // SCAN
    fwd = lambda b, c: b * nc + c
    bwd = lambda b, c: b * nc + nc - 1 - c

    def specs(ix):
        return [
            pl.BlockSpec((SCAN, D_MLSTM), lambda b, c: (ix(b, c), 0)),
            pl.BlockSpec((D_MLSTM, SCAN), lambda b, c: (0, ix(b, c))),
            pl.BlockSpec((D_MLSTM, SCAN), lambda b, c: (0, ix(b, c))),
            pl.BlockSpec((SCAN, N_UNIT), lambda b, c: (ix(b, c), 0)),
            pl.BlockSpec((GR_ROWS, SCAN), lambda b, c: (0, ix(b, c))),
        ]

    in_specs = specs(fwd) + specs(bwd)
    args = [k, qt, vt, gc, gr, k, qt, vt, gc, gr]
    has_init = s0 is not None
    if has_init:
        in_specs = [
            pl.BlockSpec((1, N_UNIT, ST_ROWS, HD), lambda b, c: (b, 0, 0, 0)),
            pl.BlockSpec((1, N_UNIT, SCAN), lambda b, c: (b, 0, 0)),
        ] + in_specs
        args = [s0, m0] + args
    out_shape = [
        jax.ShapeDtypeStruct((D_MLSTM, n), F32),
        jax.ShapeDtypeStruct((D_MLSTM, n), F32),
    ]
    out_specs = [
        pl.BlockSpec((D_MLSTM, SCAN), lambda b, c: (0, fwd(b, c))),
        pl.BlockSpec((D_MLSTM, SCAN), lambda b, c: (0, bwd(b, c))),
    ]
    if want_state:
        out_shape += [
            jax.ShapeDtypeStruct((batch, N_UNIT, HD, HD), F32),
            jax.ShapeDtypeStruct((batch, N_UNIT, HD), F32),
            jax.ShapeDtypeStruct((batch, N_UNIT, SCAN), F32),
        ]
        out_specs += [
            pl.BlockSpec((1, N_UNIT, HD, HD), lambda b, c: (b, 0, 0, 0)),
            pl.BlockSpec((1, N_UNIT, HD), lambda b, c: (b, 0, 0)),
            pl.BlockSpec((1, N_UNIT, SCAN), lambda b, c: (b, 0, 0)),
        ]
    return pl.pallas_call(
        functools.partial(_mlstm_kernel, has_init=has_init, has_out=want_state,
                          carried=has_init or nc > 1),
        out_shape=out_shape,
        grid=(batch, nc),
        in_specs=in_specs,
        out_specs=out_specs,
        scratch_shapes=[pltpu.VMEM((N_UNIT, ST_ROWS, HD), F32), pltpu.VMEM((N_UNIT, SCAN), F32)],
        compiler_params=pltpu.CompilerParams(
            dimension_semantics=("arbitrary", "arbitrary"), vmem_limit_bytes=VMEM_LIMIT_BYTES),
        name="mlstm",
    )(*args)


def _outffn_mix(x, a, hft, hbt, o, mod, c, x1_ref, h2_ref):
    ga1 = mod[:, 2 * D_MODEL:3 * D_MODEL]
    sh2 = mod[:, 3 * D_MODEL:4 * D_MODEL]
    sc2 = mod[:, 4 * D_MODEL:5 * D_MODEL]
    hs = jnp.transpose(hft + hbt)
    sig = _sigmoid(o)
    parts = []
    for hd in range(HEADS):
        sl = slice(hd * HD, (hd + 1) * HD)
        parts.append(_rms(hs[:, sl], c["gh"][:, sl]) * sig[:, sl])
    b_out = jnp.concatenate(parts, axis=-1).astype(BF16)
    mix = _dot(a, c["woa"][...]) + _dot(b_out, c["wob"][...])
    x1 = x + ga1 * mix
    x1_ref[...] = x1
    h2_ref[...] = (_rms(x1, c["g2"][...]) * (1.0 + sc2) + sh2).astype(BF16)


def _outffn_up(h2_ref, x1_in_ref, c, f_ref, x1_out_ref):
    h2 = h2_ref[...]
    u = _dot(h2, c["w1"][...])
    g = _dot(h2, c["w3"][...])
    f_ref[...] = (u * _sigmoid(u) * g).astype(BF16)
    x1_out_ref[...] = x1_in_ref[...]


def _outffn_down(f_ref, x1_ref, mod, c, y_ref):
    ga2 = mod[:, 5 * D_MODEL:6 * D_MODEL]
    x2 = x1_ref[...] + ga2 * _dot(f_ref[...], c["w2"][...])
    y_ref[...] = _rms(x2, c["gf"][...])


def _outffn_kernel(x_ref, a_ref, hft_ref, hbt_ref, o_ref, mod_in_ref, mod_out_ref, gh_ref, woa_ref,
                   wob_ref, g2_ref, w1_ref, w3_ref, w2_ref, gf_ref, y_ref,
                   x1a_s, h2_s, f_s, x1b_s):
    tb = x_ref.shape[0] // 2
    c = {"gh": gh_ref, "woa": woa_ref, "wob": wob_ref, "g2": g2_ref, "w1": w1_ref, "w3": w3_ref,
         "w2": w2_ref, "gf": gf_ref}

    @pl.when(pl.program_id(0) == 0)
    def _():
        for ref in (x1a_s, h2_s, f_s, x1b_s):
            ref[...] = jnp.zeros_like(ref)

    mod_in = mod_in_ref[0]
    mod_out = mod_out_ref[0]
    for half in range(2):
        rows = slice(half * tb, (half + 1) * tb)
        cur, oth = half, 1 - half
        _outffn_down(f_s.at[cur], x1b_s.at[cur], mod_out, c, y_ref.at[rows, :])
        _outffn_mix(x_ref[rows, :], a_ref[rows, :], hft_ref[:, rows], hbt_ref[:, rows],
                    o_ref[rows, :], mod_in, c, x1a_s.at[cur], h2_s.at[cur])
        _outffn_up(h2_s.at[oth], x1a_s.at[oth], c, f_s.at[oth], x1b_s.at[oth])


def _outffn(x, a, hft, hbt, o, mod, rows_per_mod, p):
    n = x.shape[0]
    tb = TOKEN_BLOCK
    tb2 = 2 * tb
    assert n % tb2 == 0 and rows_per_mod % tb2 == 0
    nb = n // tb2
    d_ff = p["w1"].shape[1]
    blk_in = lambda i: jnp.minimum(i, nb - 1)
    blk_out = lambda i: jnp.maximum(i - 1, 0)
    tok = lambda w: pl.BlockSpec((tb2, w), lambda i: (blk_in(i), 0))
    tok_t = lambda h: pl.BlockSpec((h, tb2), lambda i: (0, blk_in(i)))
    mod_spec = lambda blk: pl.BlockSpec(
        (1, 1, 6 * D_MODEL), lambda i: (blk(i) * tb2 // rows_per_mod, 0, 0))
    in_specs = [
        tok(D_MODEL), tok(D_GMLP), tok_t(D_MLSTM), tok_t(D_MLSTM), tok(D_MLSTM),
        mod_spec(blk_in), mod_spec(blk_out),
        _const_spec((1, D_MLSTM)),
        _const_spec((D_GMLP, D_MODEL)),
        _const_spec((D_MLSTM, D_MODEL)),
        _const_spec((1, D_MODEL)),
        _const_spec((D_MODEL, d_ff)),
        _const_spec((D_MODEL, d_ff)),
        _const_spec((d_ff, D_MODEL)),
        _const_spec((1, D_MODEL)),
    ]
    scratch_shapes = [
        pltpu.VMEM((2, tb, D_MODEL), F32),
        pltpu.VMEM((2, tb, D_MODEL), BF16),
        pltpu.VMEM((2, tb, d_ff), BF16),
        pltpu.VMEM((2, tb, D_MODEL), F32),
    ]
    return pl.pallas_call(
        _outffn_kernel,
        out_shape=jax.ShapeDtypeStruct((n, D_MODEL), F32),
        grid=(nb + 1,),
        in_specs=in_specs,
        out_specs=pl.BlockSpec((tb2, D_MODEL), lambda i: (blk_out(i), 0)),
        scratch_shapes=scratch_shapes,
        compiler_params=pltpu.CompilerParams(
            dimension_semantics=("arbitrary",), vmem_limit_bytes=VMEM_LIMIT_BYTES),
        name="outffn",
    )(x, a, hft, hbt, o, mod, mod, p["gh"], p["woa"], p["wob"], p["g2"], p["w1"], p["w3"], p["w2"],
      p["gf"])


def _regroup_gates(g):
    h = HEADS
    return jnp.concatenate([g[:, 0:h], g[:, 2 * h:3 * h], g[:, h:2 * h], g[:, 3 * h:4 * h]], axis=1)


def _layer_params(l, g_norm1, w_in, b_gate, w_s, b_s, g_v, conv_w, conv_b, g_h, w_out, g_norm2,
                  w1, w3, w2, g_final):
    wi = w_in[l].astype(BF16)
    off_v, off_q, off_vv, off_o, off_g = 512, 1024, 2048, 2560, 3072
    wg = _regroup_gates(wi[:, off_g:])
    bg = _regroup_gates(b_gate[l][None, :])
    return {
        "g1": g_norm1[l][None, :],
        "wu": wi[:, :off_v], "wv": wi[:, off_v:off_q], "wqk": wi[:, off_q:off_vv],
        "wvvt": wi[:, off_vv:off_o].T, "wo": wi[:, off_o:off_g],
        "wgt": wg.T, "bgt": bg.T,
        "cw": conv_w[l], "cb": conv_b[l][None, :],
        "ws": w_s[l].astype(BF16),
        "bs": jnp.repeat(b_s[l].T, HD, axis=1),
        "gv": g_v[l].reshape(1, D_GMLP),
        "gh": g_h[l].reshape(1, D_MLSTM),
        "woa": w_out[l][:D_GMLP].astype(BF16), "wob": w_out[l][D_GMLP:].astype(BF16),
        "g2": g_norm2[l][None, :],
        "w1": w1[l].astype(BF16), "w3": w3[l].astype(BF16), "w2": w2[l].astype(BF16),
        "gf": g_final[None, :],
    }


def _trunk(x, mod, rows_per_mod, seg, batch, p, s0=None, m0=None, want_state=False):
    a, qt, k, vt, o, gc, gr = _inproj(x, mod, rows_per_mod, seg, p)
    hft, hbt, *state = _mlstm(k, qt, vt, gc, gr, batch, s0, m0, want_state)
    y = _outffn(x, a, hft, hbt, o, mod, rows_per_mod, p)
    return y, state


def kernel(x_prompt, x_sample, state_C, state_n, state_m, c, c_ctx, w_ada, b_ada, g_norm1, w_in,
           b_gate, w_s, b_s, g_v, conv_w, conv_b, g_h, w_out, g_norm2, w1, w3, w2, g_final):
    bp, tp, d = x_prompt.shape
    bs_, ts, _ = x_sample.shape
    depth = w_in.shape[0]
    assert depth == 1, "final norm is fused into the layer's last kernel"
    xp = x_prompt.reshape(bp * tp, d)
    xs = x_sample.reshape(bs_ * ts, d)

    cs = jnp.zeros((8, d), F32).at[0].set(c_ctx).at[1:1 + bs_].set(c)
    new_c, new_n, new_m = [], [], []
    for l in range(depth):
        p = _layer_params(l, g_norm1, w_in, b_gate, w_s, b_s, g_v, conv_w, conv_b, g_h, w_out,
                          g_norm2, w1, w3, w2, g_final)
        mod = _ada(cs, w_ada[l], b_ada[l][None, :])
        mod_ctx = mod[0:1].reshape(1, 1, 6 * d)
        mod_lat = mod[1:1 + bs_].reshape(bs_, 1, 6 * d)

        xp, (c_ctx_out, n_ctx_out, m_ctx_out) = _trunk(xp, mod_ctx, bp * tp, tp, bp, p,
                                                       want_state=True)
        new_c.append(c_ctx_out.reshape(bp, N_DIR, HEADS, HD, HD))
        new_n.append(n_ctx_out.reshape(bp, N_DIR, HEADS, HD))
        new_m.append(m_ctx_out[..., 0].reshape(bp, N_DIR, HEADS))

        s0 = jnp.concatenate(
            [jnp.swapaxes(state_C[:, l], -1, -2), state_n[:, l][..., None, :],
             jnp.zeros((bs_, N_DIR, HEADS, ST_ROWS - HD - 1, HD), F32)],
            axis=-2).reshape(bs_, N_UNIT, ST_ROWS, HD)
        m0 = jnp.broadcast_to(state_m[:, l].reshape(bs_, N_UNIT, 1), (bs_, N_UNIT, SCAN))
        xs, _ = _trunk(xs, mod_lat, ts, GRID_W, bs_, p, s0, m0)

    return (xp.reshape(bp, tp, d), xs.reshape(bs_, ts, d),
            jnp.stack(new_c, axis=1), jnp.stack(new_n, axis=1), jnp.stack(new_m, axis=1))
```

```python
import functools

import jax
import jax.numpy as jnp
from jax import lax
from jax.experimental import pallas as pl
from jax.experimental.pallas import tpu as pltpu

D_MODEL = 1024
D_GMLP = 512
D_MLSTM = 512
GROUPS = 4
HEADS = 4
HD = 128
CHUNK = 128
SCAN = 256
N_DIR = 2
N_UNIT = N_DIR * HEADS
GRID_W = 64
EPS = 1e-6
NEG = -1e30
ST_ROWS = HD + 16
GR_ROWS = 4 * N_UNIT

TOKEN_BLOCK = 256
VMEM_LIMIT_BYTES = 56 * 1024 * 1024

F32 = jnp.float32
BF16 = jnp.bfloat16


def _rms(x, g):
    return x * lax.rsqrt(jnp.mean(x * x, axis=-1, keepdims=True) + EPS) * g


def _sigmoid(x):
    return 1.0 / (1.0 + jnp.exp(-x))


def _gelu_tanh(x):
    return 0.5 * x * (1.0 + jnp.tanh(0.7978845608028654 * (x + 0.044715 * (x * x * x))))


def _log_sigmoid(x):
    return jnp.minimum(x, 0.0) - jnp.log(1.0 + jnp.exp(-jnp.abs(x)))


def _dot(a, b):
    return jnp.dot(a, b, preferred_element_type=F32)


def _dot_nt(a, b):
    return lax.dot_general(a, b, (((1,), (1,)), ((), ())), preferred_element_type=F32)


def _dot_exact(a, b):
    return jnp.dot(a, b, preferred_element_type=F32, precision=lax.Precision.HIGHEST)


def _pipeline_step(tick):
    i = pl.program_id(0)
    last = pl.num_programs(0) - 1

    @pl.when(i == 0)
    def _():
        tick(0, True, False, False)
        tick(1, True, True, False)

    @pl.when(jnp.logical_and(i > 0, i < last))
    def _():
        tick(0, True, True, True)
        tick(1, True, True, True)

    @pl.when(i == last)
    def _():
        tick(0, False, True, True)
        tick(1, False, False, True)


def _const_spec(shape):
    zeros = (0,) * len(shape)
    return pl.BlockSpec(shape, lambda *_: zeros, pipeline_mode=pl.Buffered(1))


def _ada_kernel(c_ref, w_ref, b_ref, o_ref):
    c = c_ref[...]
    s = (c * _sigmoid(c)).astype(BF16)
    o_ref[...] = _dot(s, w_ref[...].astype(BF16)) + b_ref[...]


def _ada(cs, w_ada, b_ada):
    rows, d = cs.shape
    n = w_ada.shape[1]
    tn = 1024
    return pl.pallas_call(
        _ada_kernel,
        out_shape=jax.ShapeDtypeStruct((rows, n), F32),
        grid=(n // tn,),
        in_specs=[
            pl.BlockSpec((rows, d), lambda j: (0, 0)),
            pl.BlockSpec((d, tn), lambda j: (0, j)),
            pl.BlockSpec((1, tn), lambda j: (0, j)),
        ],
        out_specs=pl.BlockSpec((rows, tn), lambda j: (0, j)),
        compiler_params=pltpu.CompilerParams(
            dimension_semantics=("arbitrary",), vmem_limit_bytes=VMEM_LIMIT_BYTES),
        name="ada",
    )(cs, w_ada, b_ada)


def _inproj_norm(x, mod, g1, hb_ref):
    sh1 = mod[:, 0:D_MODEL]
    sc1 = mod[:, D_MODEL:2 * D_MODEL]
    hb_ref[...] = (_rms(x, g1) * (1.0 + sc1) + sh1).astype(BF16)


def _inproj_project(hb_ref, w, z):
    hb = hb_ref[...]
    z["u"][...] = _dot(hb, w["u"][...])
    z["v"][...] = _dot(hb, w["v"][...])
    z["qk"][...] = _dot(hb, w["qk"][...])
    z["o"][...] = _dot(hb, w["o"][...])
    z["vt"][...] = _dot_nt(w["vvt"][...], hb).astype(BF16)
    z["gt"][...] = _dot_nt(w["gt"][...], hb)


def _inproj_finish(z, c, out, seg):
    tb = z["u"].shape[0]
    for g in range(GROUPS):
        gs = slice(g * HD, (g + 1) * HD)
        vg = _rms(_gelu_tanh(z["v"][:, gs]), c["gv"][:, gs]).astype(BF16)
        for ch in range(tb // CHUNK):
            cs = slice(ch * CHUNK, (ch + 1) * CHUNK)
            mixed = _dot(c["ws"][g], vg[cs]) + c["bs"][:, gs]
            out["a"][cs, gs] = (_gelu_tanh(z["u"][cs, gs]) * mixed).astype(BF16)

    zqk = z["qk"][...]
    pos = lax.broadcasted_iota(jnp.int32, (tb, 1), 0) % seg
    prev = jnp.where(pos != 0, pltpu.roll(zqk, 1, 0), 0.0)
    nxt = jnp.where(pos != seg - 1, pltpu.roll(zqk, tb - 1, 0), 0.0)
    cw = c["cw"]
    y = c["cb"][...] + prev * cw[0:1, :] + zqk * cw[1:2, :] + nxt * cw[2:3, :]
    y = y * _sigmoid(y)
    out["qt"][...] = jnp.transpose(y[:, :D_MLSTM]).astype(BF16)
    out["k"][...] = (y[:, D_MLSTM:] * (HD ** -0.5)).astype(BF16)
    out["vt"][...] = z["vt"][...]
    out["o"][...] = z["o"][...]

    assert tb == SCAN
    gt = z["gt"][...] + c["bgt"][...]
    gi_t = gt[0:N_UNIT]
    lf_t = _log_sigmoid(gt[N_UNIT:2 * N_UNIT])
    lane = lax.broadcasted_iota(jnp.int32, (N_UNIT, HD), 1)
    tiles = []
    carry = jnp.zeros((N_UNIT, HD), F32)
    for t in range(SCAN // HD):
        prefix = lf_t[:, t * HD:(t + 1) * HD]
        shift = 1
        while shift < HD:
            prefix = prefix + jnp.where(lane >= shift, pltpu.roll(prefix, shift, 1), 0.0)
            shift *= 2
        tiles.append(prefix + carry)
        carry = carry + jnp.broadcast_to(prefix[:, HD - 1:HD], (N_UNIT, HD))
    prefix = jnp.concatenate(tiles, axis=1)
    b_last = jnp.concatenate([carry] * (SCAN // HD), axis=1)
    row_bwd = lax.broadcasted_iota(jnp.int32, (N_UNIT, SCAN), 0) >= HEADS
    b_row = jnp.where(row_bwd, b_last - prefix + lf_t, prefix)
    g_row = b_last - b_row + gi_t
    g_max = jnp.broadcast_to(jnp.max(g_row, axis=1, keepdims=True), (N_UNIT, SCAN))
    out["gr"][0 * N_UNIT:1 * N_UNIT, :] = b_row
    out["gr"][1 * N_UNIT:2 * N_UNIT, :] = g_row
    out["gr"][2 * N_UNIT:3 * N_UNIT, :] = b_last
    out["gr"][3 * N_UNIT:4 * N_UNIT, :] = g_max
    d_row = gi_t - b_row
    pad = jnp.zeros((HD - N_UNIT, HD), F32)
    for t in range(SCAN // HD):
        ts = slice(t * HD, (t + 1) * HD)
        d_tile = jnp.concatenate([d_row[:, ts], pad], axis=0)
        out["gc"][ts, :] = jnp.transpose(d_tile)[:, 0:N_UNIT]


_INPROJ_Z = ("u", "v", "qk", "o", "vt", "gt")


def _inproj_kernel(x_ref, mod_ref, g1_ref, wu_ref, wv_ref, wqk_ref, wvvt_ref, wo_ref,
                   wgt_ref, bgt_ref, cw_ref, cb_ref, ws_ref, bs_ref, gv_ref,
                   a_ref, qt_ref, k_ref, vt_ref, o_ref, gc_ref, gr_ref,
                   hb_s, *z_s, seg, blocks_per_mod):
    tb = x_ref.shape[0] // 2
    mod_in = jnp.minimum(pl.program_id(0), pl.num_programs(0) - 2) // blocks_per_mod
    w = {"u": wu_ref, "v": wv_ref, "qk": wqk_ref, "o": wo_ref, "vvt": wvvt_ref, "gt": wgt_ref}
    c = {"gv": gv_ref, "ws": ws_ref, "bs": bs_ref, "cw": cw_ref, "cb": cb_ref, "bgt": bgt_ref}

    def tick(half, norm, project, finish):
        rows = slice(half * tb, (half + 1) * tb)
        cur, oth = half, 1 - half
        z_cur = {name: ref.at[cur] for name, ref in zip(_INPROJ_Z, z_s)}
        z_oth = {name: ref.at[oth] for name, ref in zip(_INPROJ_Z, z_s)}
        out = {"a": a_ref.at[rows, :], "qt": qt_ref.at[:, rows], "k": k_ref.at[rows, :],
               "vt": vt_ref.at[:, rows], "o": o_ref.at[rows, :], "gc": gc_ref.at[rows, :],
               "gr": gr_ref.at[:, rows]}
        if project:
            _inproj_project(hb_s.at[oth], w, z_oth)
        if norm:
            _inproj_norm(x_ref[rows, :], mod_ref[mod_in], g1_ref[...], hb_s.at[cur])
        if finish:
            _inproj_finish(z_cur, c, out, seg)

    _pipeline_step(tick)


def _inproj(x, mod, rows_per_mod, seg, p):
    n = x.shape[0]
    tb = TOKEN_BLOCK
    tb2 = 2 * tb
    assert tb % seg == 0 and n % tb2 == 0 and rows_per_mod % tb2 == 0
    nb = n // tb2
    blk_in = lambda i: jnp.minimum(i, nb - 1)
    blk_out = lambda i: jnp.maximum(i - 1, 0)
    tok = lambda width: pl.BlockSpec((tb2, width), lambda i: (blk_out(i), 0))
    tok_t = lambda height: pl.BlockSpec((height, tb2), lambda i: (0, blk_out(i)))
    in_specs = [
        pl.BlockSpec((tb2, D_MODEL), lambda i: (blk_in(i), 0)),
        _const_spec(mod.shape),
        _const_spec((1, D_MODEL)),
        _const_spec((D_MODEL, D_GMLP)),
        _const_spec((D_MODEL, D_GMLP)),
        _const_spec((D_MODEL, 2 * D_MLSTM)),
        _const_spec((D_MLSTM, D_MODEL)),
        _const_spec((D_MODEL, D_MLSTM)),
        _const_spec((2 * N_UNIT, D_MODEL)),
        _const_spec((2 * N_UNIT, 1)),
        _const_spec((3, 2 * D_MLSTM)),
        _const_spec((1, 2 * D_MLSTM)),
        _const_spec((GROUPS, CHUNK, CHUNK)),
        _const_spec((CHUNK, D_GMLP)),
        _const_spec((1, D_GMLP)),
    ]
    out_shape = [
        jax.ShapeDtypeStruct((n, D_GMLP), BF16),
        jax.ShapeDtypeStruct((D_MLSTM, n), BF16),
        jax.ShapeDtypeStruct((n, D_MLSTM), BF16),
        jax.ShapeDtypeStruct((D_MLSTM, n), BF16),
        jax.ShapeDtypeStruct((n, D_MLSTM), F32),
        jax.ShapeDtypeStruct((n, N_UNIT), F32),
        jax.ShapeDtypeStruct((GR_ROWS, n), F32),
    ]
    out_specs = [tok(D_GMLP), tok_t(D_MLSTM), tok(D_MLSTM), tok_t(D_MLSTM), tok(D_MLSTM),
                 tok(N_UNIT), tok_t(GR_ROWS)]
    scratch_shapes = [
        pltpu.VMEM((2, tb, D_MODEL), BF16),
        pltpu.VMEM((2, tb, D_GMLP), F32),
        pltpu.VMEM((2, tb, D_GMLP), F32),
        pltpu.VMEM((2, tb, 2 * D_MLSTM), F32),
        pltpu.VMEM((2, tb, D_MLSTM), F32),
        pltpu.VMEM((2, D_MLSTM, tb), BF16),
        pltpu.VMEM((2, 2 * N_UNIT, tb), F32),
    ]
    return pl.pallas_call(
        functools.partial(_inproj_kernel, seg=seg, blocks_per_mod=rows_per_mod // tb2),
        out_shape=out_shape,
        grid=(nb + 1,),
        in_specs=in_specs,
        out_specs=out_specs,
        scratch_shapes=scratch_shapes,
        compiler_params=pltpu.CompilerParams(
            dimension_semantics=("arbitrary",), vmem_limit_bytes=VMEM_LIMIT_BYTES),
        name="inproj",
    )(x, mod, p["g1"], p["wu"], p["wv"], p["wqk"], p["wvvt"], p["wo"], p["wgt"],
      p["bgt"], p["cw"], p["cb"], p["ws"], p["bs"], p["gv"])


def _mlstm_unit(k, vt, kq, qs, dcol, b_row, g_row, bl_row, gmax_row, st, m_row, mask, carried):
    logw = jnp.where(mask, dcol + b_row, NEG)
    a = b_row + m_row
    mj = jnp.maximum(a, jnp.max(logw, axis=0, keepdims=True))
    w = jnp.exp(logw - mj)
    s = kq * w
    num = _dot(vt, s.astype(BF16))
    den = jnp.sum(s, axis=0, keepdims=True)
    if carried:
        inter = jnp.exp(a - mj)
        num = num + inter * qs[:HD]
        den = den + inter * qs[HD:HD + 1]
    h = num * (1.0 / jnp.maximum(jnp.abs(den), jnp.exp(-mj)))
    m_new = jnp.maximum(bl_row + m_row, gmax_row)
    wc = jnp.exp(g_row - m_new)
    pad_row = lax.broadcasted_iota(jnp.int32, (ST_ROWS - HD, wc.shape[1]), 0)
    v_aug = jnp.concatenate(
        [vt.astype(F32) * wc, jnp.where(pad_row == 0, wc, 0.0)], axis=0).astype(BF16)
    st_new = _dot(v_aug, k)
    if carried:
        decay = jnp.exp(bl_row + m_row - m_new)
        st_new = decay[:, :HD] * st + st_new
    return h, st_new, m_new


def _mlstm_kernel(*refs, has_init, has_out, carried):
    if has_init:
        s0_ref, m0_ref = refs[:2]
        refs = refs[2:]
    (kf_ref, qtf_ref, vtf_ref, gcf_ref, grf_ref,
     kb_ref, qtb_ref, vtb_ref, gcb_ref, grb_ref, hf_ref, hb_ref) = refs[:12]
    refs = refs[12:]
    if has_out:
        co_ref, no_ref, mo_ref = refs[:3]
        refs = refs[3:]
    s_ref, m_ref = refs
    c = pl.program_id(1)
    nc = pl.num_programs(1)

    @pl.when(c == 0)
    def _():
        if has_init:
            s_ref[...] = s0_ref[0]
            m_ref[...] = m0_ref[0]
        else:
            s_ref[...] = jnp.zeros_like(s_ref)
            m_ref[...] = jnp.zeros_like(m_ref)

    si = lax.broadcasted_iota(jnp.int32, (SCAN, SCAN), 0)
    ji = lax.broadcasted_iota(jnp.int32, (SCAN, SCAN), 1)
    dirs = (
        (kf_ref, qtf_ref, vtf_ref, gcf_ref, grf_ref, hf_ref, si <= ji),
        (kb_ref, qtb_ref, vtb_ref, gcb_ref, grb_ref, hb_ref, si >= ji),
    )
    m_all = m_ref[...]
    first = {}
    for d, (k_ref, qt_ref, _, _, _, _, _) in enumerate(dirs):
        for hd in range(HEADS):
            u = d * HEADS + hd
            hs = slice(hd * HD, (hd + 1) * HD)
            kq = _dot(k_ref[:, hs], qt_ref[hs, :])
            qs = _dot(s_ref[u].astype(BF16), qt_ref[hs, :]) if carried else None
            first[u] = (kq, qs)
    m_news = []
    for d, (k_ref, qt_ref, vt_ref, gc_ref, gr_ref, h_ref, mask) in enumerate(dirs):
        for hd in range(HEADS):
            u = d * HEADS + hd
            hs = slice(hd * HD, (hd + 1) * HD)
            h, st_new, m_new = _mlstm_unit(
                k_ref[:, hs], vt_ref[hs, :], *first[u],
                gc_ref[:, u:u + 1],
                gr_ref[u:u + 1, :],
                gr_ref[N_UNIT + u:N_UNIT + u + 1, :],
                gr_ref[2 * N_UNIT + u:2 * N_UNIT + u + 1, :],
                gr_ref[3 * N_UNIT + u:3 * N_UNIT + u + 1, :],
                s_ref[u], m_all[u:u + 1], mask, carried)
            h_ref[hs, :] = h
            s_ref[u] = st_new
            m_news.append(m_new)
    m_next = jnp.concatenate(m_news, axis=0)
    m_ref[...] = m_next

    if has_out:
        @pl.when(c == nc - 1)
        def _():
            for u in range(N_UNIT):
                co_ref[0, u] = jnp.transpose(s_ref[u, 0:HD, :])
                no_ref[0, u:u + 1, :] = s_ref[u, HD:HD + 1, :]
            mo_ref[0] = m_next


def _mlstm(k, qt, vt, gc, gr, batch, s0=None, m0=None, want_state=False):
    n = k.shape[0]
    nc = n // batch // SCAN
    fwd = lambda b, c: b * nc + c
    bwd = lambda b, c: b * nc + nc - 1 - c

    def specs(ix):
        return [
            pl.BlockSpec((SCAN, D_MLSTM), lambda b, c: (ix(b, c), 0)),
            pl.BlockSpec((D_MLSTM, SCAN), lambda b, c: (0, ix(b, c))),
            pl.BlockSpec((D_MLSTM, SCAN), lambda b, c: (0, ix(b, c))),
            pl.BlockSpec((SCAN, N_UNIT), lambda b, c: (ix(b, c), 0)),
            pl.BlockSpec((GR_ROWS, SCAN), lambda b, c: (0, ix(b, c))),
        ]

    in_specs = specs(fwd) + specs(bwd)
    args = [k, qt, vt, gc, gr, k, qt, vt, gc, gr]
    has_init = s0 is not None
    if has_init:
        in_specs = [
            pl.BlockSpec((1, N_UNIT, ST_ROWS, HD), lambda b, c: (b, 0, 0, 0)),
            pl.BlockSpec((1, N_UNIT, SCAN), lambda b, c: (b, 0, 0)),
        ] + in_specs
        args = [s0, m0] + args
    out_shape = [
        jax.ShapeDtypeStruct((D_MLSTM, n), F32),
        jax.ShapeDtypeStruct((D_MLSTM, n), F32),
    ]
    out_specs = [
        pl.BlockSpec((D_MLSTM, SCAN), lambda b, c: (0, fwd(b, c))),
        pl.BlockSpec((D_MLSTM, SCAN), lambda b, c: (0, bwd(b, c))),
    ]
    if want_state:
        out_shape += [
            jax.ShapeDtypeStruct((batch, N_UNIT, HD, HD), F32),
            jax.ShapeDtypeStruct((batch, N_UNIT, HD), F32),
            jax.ShapeDtypeStruct((batch, N_UNIT, SCAN), F32),
        ]
        out_specs += [
            pl.BlockSpec((1, N_UNIT, HD, HD), lambda b, c: (b, 0, 0, 0)),
            pl.BlockSpec((1, N_UNIT, HD), lambda b, c: (b, 0, 0)),
            pl.BlockSpec((1, N_UNIT, SCAN), lambda b, c: (b, 0, 0)),
        ]
    return pl.pallas_call(
        functools.partial(_mlstm_kernel, has_init=has_init, has_out=want_state,
                          carried=has_init or nc > 1),
        out_shape=out_shape,
        grid=(batch, nc),
        in_specs=in_specs,
        out_specs=out_specs,
        scratch_shapes=[pltpu.VMEM((N_UNIT, ST_ROWS, HD), F32), pltpu.VMEM((N_UNIT, SCAN), F32)],
        compiler_params=pltpu.CompilerParams(
            dimension_semantics=("arbitrary", "arbitrary"), vmem_limit_bytes=VMEM_LIMIT_BYTES),
        name="mlstm",
    )(*args)


def _outffn_mix(x, a, hft, hbt, o, mod, c, x1_ref, h2_ref):
    ga1 = mod[:, 2 * D_MODEL:3 * D_MODEL]
    sh2 = mod[:, 3 * D_MODEL:4 * D_MODEL]
    sc2 = mod[:, 4 * D_MODEL:5 * D_MODEL]
    hs = jnp.transpose(hft + hbt)
    sig = _sigmoid(o)
    parts = []
    for hd in range(HEADS):
        sl = slice(hd * HD, (hd + 1) * HD)
        parts.append(_rms(hs[:, sl], c["gh"][:, sl]) * sig[:, sl])
    b_out = jnp.concatenate(parts, axis=-1).astype(BF16)
    mix = _dot(a, c["woa"][...]) + _dot(b_out, c["wob"][...])
    x1 = x + ga1 * mix
    x1_ref[...] = x1
    h2_ref[...] = (_rms(x1, c["g2"][...]) * (1.0 + sc2) + sh2).astype(BF16)


def _outffn_up(h2_ref, x1_in_ref, c, f_ref, x1_out_ref):
    h2 = h2_ref[...]
    u = _dot(h2, c["w1"][...])
    g = _dot(h2, c["w3"][...])
    f_ref[...] = (u * _sigmoid(u) * g).astype(BF16)
    x1_out_ref[...] = x1_in_ref[...]


def _outffn_down(f_ref, x1_ref, mod, c, y_ref):
    ga2 = mod[:, 5 * D_MODEL:6 * D_MODEL]
    x2 = x1_ref[...] + ga2 * _dot(f_ref[...], c["w2"][...])
    y_ref[...] = _rms(x2, c["gf"][...])


def _outffn_kernel(x_ref, a_ref, hft_ref, hbt_ref, o_ref, mod_ref, gh_ref, woa_ref,
                   wob_ref, g2_ref, w1_ref, w3_ref, w2_ref, gf_ref, y_ref,
                   x1a_s, h2_s, f_s, x1b_s, *, blocks_per_mod):
    tb = x_ref.shape[0] // 2
    i = pl.program_id(0)
    mod_in = jnp.minimum(i, pl.num_programs(0) - 2) // blocks_per_mod
    mod_out = jnp.maximum(i - 1, 0) // blocks_per_mod
    c = {"gh": gh_ref, "woa": woa_ref, "wob": wob_ref, "g2": g2_ref, "w1": w1_ref, "w3": w3_ref,
         "w2": w2_ref, "gf": gf_ref}

    def tick(half, mix, up, down):
        rows = slice(half * tb, (half + 1) * tb)
        cur, oth = half, 1 - half
        if down:
            _outffn_down(f_s.at[cur], x1b_s.at[cur], mod_ref[mod_out], c, y_ref.at[rows, :])
        if mix:
            _outffn_mix(x_ref[rows, :], a_ref[rows, :], hft_ref[:, rows], hbt_ref[:, rows],
                        o_ref[rows, :], mod_ref[mod_in], c, x1a_s.at[cur], h2_s.at[cur])
        if up:
            _outffn_up(h2_s.at[oth], x1a_s.at[oth], c, f_s.at[oth], x1b_s.at[oth])

    _pipeline_step(tick)


def _outffn(x, a, hft, hbt, o, mod, rows_per_mod, p):
    n = x.shape[0]
    tb = TOKEN_BLOCK
    tb2 = 2 * tb
    assert n % tb2 == 0 and rows_per_mod % tb2 == 0
    nb = n // tb2
    d_ff = p["w1"].shape[1]
    blk_in = lambda i: jnp.minimum(i, nb - 1)
    blk_out = lambda i: jnp.maximum(i - 1, 0)
    tok = lambda w: pl.BlockSpec((tb2, w), lambda i: (blk_in(i), 0))
    tok_t = lambda h: pl.BlockSpec((h, tb2), lambda i: (0, blk_in(i)))
    in_specs = [
        tok(D_MODEL), tok(D_GMLP), tok_t(D_MLSTM), tok_t(D_MLSTM), tok(D_MLSTM),
        _const_spec(mod.shape),
        _const_spec((1, D_MLSTM)),
        _const_spec((D_GMLP, D_MODEL)),
        _const_spec((D_MLSTM, D_MODEL)),
        _const_spec((1, D_MODEL)),
        _const_spec((D_MODEL, d_ff)),
        _const_spec((D_MODEL, d_ff)),
        _const_spec((d_ff, D_MODEL)),
        _const_spec((1, D_MODEL)),
    ]
    scratch_shapes = [
        pltpu.VMEM((2, tb, D_MODEL), F32),
        pltpu.VMEM((2, tb, D_MODEL), BF16),
        pltpu.VMEM((2, tb, d_ff), BF16),
        pltpu.VMEM((2, tb, D_MODEL), F32),
    ]
    return pl.pallas_call(
        functools.partial(_outffn_kernel, blocks_per_mod=rows_per_mod // tb2),
        out_shape=jax.ShapeDtypeStruct((n, D_MODEL), F32),
        grid=(nb + 1,),
        in_specs=in_specs,
        out_specs=pl.BlockSpec((tb2, D_MODEL), lambda i: (blk_out(i), 0)),
        scratch_shapes=scratch_shapes,
        compiler_params=pltpu.CompilerParams(
            dimension_semantics=("arbitrary",), vmem_limit_bytes=VMEM_LIMIT_BYTES),
        name="outffn",
    )(x, a, hft, hbt, o, mod, p["gh"], p["woa"], p["wob"], p["g2"], p["w1"], p["w3"], p["w2"],
      p["gf"])


def _regroup_gates(g):
    h = HEADS
    return jnp.concatenate([g[:, 0:h], g[:, 2 * h:3 * h], g[:, h:2 * h], g[:, 3 * h:4 * h]], axis=1)


def _layer_params(l, g_norm1, w_in, b_gate, w_s, b_s, g_v, conv_w, conv_b, g_h, w_out, g_norm2,
                  w1, w3, w2, g_final):
    wi = w_in[l].astype(BF16)
    off_v, off_q, off_vv, off_o, off_g = 512, 1024, 2048, 2560, 3072
    wg = _regroup_gates(wi[:, off_g:])
    bg = _regroup_gates(b_gate[l][None, :])
    return {
        "g1": g_norm1[l][None, :],
        "wu": wi[:, :off_v], "wv": wi[:, off_v:off_q], "wqk": wi[:, off_q:off_vv],
        "wvvt": wi[:, off_vv:off_o].T, "wo": wi[:, off_o:off_g],
        "wgt": wg.T, "bgt": bg.T,
        "cw": conv_w[l], "cb": conv_b[l][None, :],
        "ws": w_s[l].astype(BF16),
        "bs": jnp.repeat(b_s[l].T, HD, axis=1),
        "gv": g_v[l].reshape(1, D_GMLP),
        "gh": g_h[l].reshape(1, D_MLSTM),
        "woa": w_out[l][:D_GMLP].astype(BF16), "wob": w_out[l][D_GMLP:].astype(BF16),
        "g2": g_norm2[l][None, :],
        "w1": w1[l].astype(BF16), "w3": w3[l].astype(BF16), "w2": w2[l].astype(BF16),
        "gf": g_final[None, :],
    }


def _trunk(x, mod, rows_per_mod, seg, batch, p, s0=None, m0=None, want_state=False):
    a, qt, k, vt, o, gc, gr = _inproj(x, mod, rows_per_mod, seg, p)
    hft, hbt, *state = _mlstm(k, qt, vt, gc, gr, batch, s0, m0, want_state)
    y = _outffn(x, a, hft, hbt, o, mod, rows_per_mod, p)
    return y, state


def kernel(x_prompt, x_sample, state_C, state_n, state_m, c, c_ctx, w_ada, b_ada, g_norm1, w_in,
           b_gate, w_s, b_s, g_v, conv_w, conv_b, g_h, w_out, g_norm2, w1, w3, w2, g_final):
    bp, tp, d = x_prompt.shape
    bs_, ts, _ = x_sample.shape
    depth = w_in.shape[0]
    assert depth == 1, "final norm is fused into the layer's last kernel"
    xp = x_prompt.reshape(bp * tp, d)
    xs = x_sample.reshape(bs_ * ts, d)

    cs = jnp.zeros((8, d), F32).at[0].set(c_ctx).at[1:1 + bs_].set(c)
    new_c, new_n, new_m = [], [], []
    for l in range(depth):
        p = _layer_params(l, g_norm1, w_in, b_gate, w_s, b_s, g_v, conv_w, conv_b, g_h, w_out,
                          g_norm2, w1, w3, w2, g_final)
        mod = _ada(cs, w_ada[l], b_ada[l][None, :])
        mod_ctx = mod[0:1].reshape(1, 1, 6 * d)
        mod_lat = mod[1:1 + bs_].reshape(bs_, 1, 6 * d)

        xp, (c_ctx_out, n_ctx_out, m_ctx_out) = _trunk(xp, mod_ctx, bp * tp, tp, bp, p,
                                                       want_state=True)
        new_c.append(c_ctx_out.reshape(bp, N_DIR, HEADS, HD, HD))
        new_n.append(n_ctx_out.reshape(bp, N_DIR, HEADS, HD))
        new_m.append(m_ctx_out[..., 0].reshape(bp, N_DIR, HEADS))

        s0 = jnp.concatenate(
            [jnp.swapaxes(state_C[:, l], -1, -2), state_n[:, l][..., None, :],
             jnp.zeros((bs_, N_DIR, HEADS, ST_ROWS - HD - 1, HD), F32)],
            axis=-2).reshape(bs_, N_UNIT, ST_ROWS, HD)
        m0 = jnp.broadcast_to(state_m[:, l].reshape(bs_, N_UNIT, 1), (bs_, N_UNIT, SCAN))
        xs, _ = _trunk(xs, mod_lat, ts, GRID_W, bs_, p, s0, m0)

    return (xp.reshape(bp, tp, d), xs.reshape(bs_, ts, d),
            jnp.stack(new_c, axis=1), jnp.stack(new_n, axis=1), jnp.stack(new_m, axis=1))
```

```python
import functools

import jax
import jax.numpy as jnp
from jax import lax
from jax.experimental import pallas as pl
from jax.experimental.pallas import tpu as pltpu

D_MODEL = 1024
D_GMLP = 512
D_MLSTM = 512
GROUPS = 4
HEADS = 4
HD = 128
CHUNK = 128
SCAN = 256
N_DIR = 2
N_UNIT = N_DIR * HEADS
GRID_W = 64
EPS = 1e-6
NEG = -1e30
ST_ROWS = HD + 16
GR_ROWS = 4 * N_UNIT

TOKEN_BLOCK = 256
VMEM_LIMIT_BYTES = 56 * 1024 * 1024

F32 = jnp.float32
BF16 = jnp.bfloat16


def _rms(x, g):
    return x * lax.rsqrt(jnp.mean(x * x, axis=-1, keepdims=True) + EPS) * g


def _sigmoid(x):
    return 1.0 / (1.0 + jnp.exp(-x))


def _gelu_tanh(x):
    return 0.5 * x * (1.0 + jnp.tanh(0.7978845608028654 * (x + 0.044715 * (x * x * x))))


def _log_sigmoid(x):
    return jnp.minimum(x, 0.0) - jnp.log(1.0 + jnp.exp(-jnp.abs(x)))


def _dot(a, b):
    return jnp.dot(a, b, preferred_element_type=F32)


def _dot_nt(a, b):
    return lax.dot_general(a, b, (((1,), (1,)), ((), ())), preferred_element_type=F32)


def _dot_exact(a, b):
    return jnp.dot(a, b, preferred_element_type=F32, precision=lax.Precision.HIGHEST)


def _pipeline_step(tick):
    i = pl.program_id(0)
    last = pl.num_programs(0) - 1

    @pl.when(i == 0)
    def _():
        tick(0, True, False, False)
        tick(1, True, True, False)

    @pl.when(jnp.logical_and(i > 0, i < last))
    def _():
        tick(0, True, True, True)
        tick(1, True, True, True)

    @pl.when(i == last)
    def _():
        tick(0, False, True, True)
        tick(1, False, False, True)


def _const_spec(shape):
    zeros = (0,) * len(shape)
    return pl.BlockSpec(shape, lambda *_: zeros, pipeline_mode=pl.Buffered(1))


def _ada_kernel(c_ref, w_ref, b_ref, o_ref):
    c = c_ref[...]
    s = (c * _sigmoid(c)).astype(BF16)
    o_ref[...] = _dot(s, w_ref[...].astype(BF16)) + b_ref[...]


def _ada(cs, w_ada, b_ada):
    rows, d = cs.shape
    n = w_ada.shape[1]
    tn = 1024
    return pl.pallas_call(
        _ada_kernel,
        out_shape=jax.ShapeDtypeStruct((rows, n), F32),
        grid=(n // tn,),
        in_specs=[
            pl.BlockSpec((rows, d), lambda j: (0, 0)),
            pl.BlockSpec((d, tn), lambda j: (0, j)),
            pl.BlockSpec((1, tn), lambda j: (0, j)),
        ],
        out_specs=pl.BlockSpec((rows, tn), lambda j: (0, j)),
        compiler_params=pltpu.CompilerParams(
            dimension_semantics=("arbitrary",), vmem_limit_bytes=VMEM_LIMIT_BYTES),
        name="ada",
    )(cs, w_ada, b_ada)


def _inproj_norm(x, mod, g1, hb_ref):
    sh1 = mod[:, 0:D_MODEL]
    sc1 = mod[:, D_MODEL:2 * D_MODEL]
    hb_ref[...] = (_rms(x, g1) * (1.0 + sc1) + sh1).astype(BF16)


def _inproj_project(hb_ref, w, z):
    hb = hb_ref[...]
    z["u"][...] = _dot(hb, w["u"][...])
    z["v"][...] = _dot(hb, w["v"][...])
    z["qk"][...] = _dot(hb, w["qk"][...])
    z["o"][...] = _dot(hb, w["o"][...])
    z["vt"][...] = _dot_nt(w["vvt"][...], hb).astype(BF16)
    z["gt"][...] = _dot_nt(w["gt"][...], hb)


def _inproj_finish(z, c, out, seg):
    tb = z["u"].shape[0]
    for g in range(GROUPS):
        gs = slice(g * HD, (g + 1) * HD)
        vg = _rms(_gelu_tanh(z["v"][:, gs]), c["gv"][:, gs]).astype(BF16)
        for ch in range(tb // CHUNK):
            cs = slice(ch * CHUNK, (ch + 1) * CHUNK)
            mixed = _dot(c["ws"][g], vg[cs]) + c["bs"][:, gs]
            out["a"][cs, gs] = (_gelu_tanh(z["u"][cs, gs]) * mixed).astype(BF16)

    zqk = z["qk"][...]
    pos = lax.broadcasted_iota(jnp.int32, (tb, 1), 0) % seg
    prev = jnp.where(pos != 0, pltpu.roll(zqk, 1, 0), 0.0)
    nxt = jnp.where(pos != seg - 1, pltpu.roll(zqk, tb - 1, 0), 0.0)
    cw = c["cw"]
    y = c["cb"][...] + prev * cw[0:1, :] + zqk * cw[1:2, :] + nxt * cw[2:3, :]
    y = y * _sigmoid(y)
    out["qt"][...] = jnp.transpose(y[:, :D_MLSTM]).astype(BF16)
    out["k"][...] = (y[:, D_MLSTM:] * (HD ** -0.5)).astype(BF16)
    out["vt"][...] = z["vt"][...]
    out["o"][...] = z["o"][...]

    assert tb == SCAN
    gt = z["gt"][...] + c["bgt"][...]
    gi_t = gt[0:N_UNIT]
    lf_t = _log_sigmoid(gt[N_UNIT:2 * N_UNIT])
    lane = lax.broadcasted_iota(jnp.int32, (N_UNIT, HD), 1)
    tiles = []
    carry = jnp.zeros((N_UNIT, HD), F32)
    for t in range(SCAN // HD):
        prefix = lf_t[:, t * HD:(t + 1) * HD]
        shift = 1
        while shift < HD:
            prefix = prefix + jnp.where(lane >= shift, pltpu.roll(prefix, shift, 1), 0.0)
            shift *= 2
        tiles.append(prefix + carry)
        carry = carry + jnp.broadcast_to(prefix[:, HD - 1:HD], (N_UNIT, HD))
    prefix = jnp.concatenate(tiles, axis=1)
    b_last = jnp.concatenate([carry] * (SCAN // HD), axis=1)
    row_bwd = lax.broadcasted_iota(jnp.int32, (N_UNIT, SCAN), 0) >= HEADS
    b_row = jnp.where(row_bwd, b_last - prefix + lf_t, prefix)
    g_row = b_last - b_row + gi_t
    g_max = jnp.broadcast_to(jnp.max(g_row, axis=1, keepdims=True), (N_UNIT, SCAN))
    out["gr"][0 * N_UNIT:1 * N_UNIT, :] = b_row
    out["gr"][1 * N_UNIT:2 * N_UNIT, :] = g_row
    out["gr"][2 * N_UNIT:3 * N_UNIT, :] = b_last
    out["gr"][3 * N_UNIT:4 * N_UNIT, :] = g_max
    d_row = gi_t - b_row
    pad = jnp.zeros((HD - N_UNIT, HD), F32)
    for t in range(SCAN // HD):
        ts = slice(t * HD, (t + 1) * HD)
        d_tile = jnp.concatenate([d_row[:, ts], pad], axis=0)
        out["gc"][ts, :] = jnp.transpose(d_tile)[:, 0:N_UNIT]


_INPROJ_Z = ("u", "v", "qk", "o", "vt", "gt")


def _inproj_kernel(*refs, seg, blocks_per_mod, n_cast):
    (x_ref, mod_ref, g1_ref, wu_ref, wv_ref, wqk_ref, wvvt_ref, wo_ref,
     wgt_ref, bgt_ref, cw_ref, cb_ref, ws_ref, bs_ref, gv_ref) = refs[:15]
    cast_in = refs[15:15 + n_cast]
    a_ref, qt_ref, k_ref, vt_ref, o_ref, gc_ref, gr_ref = refs[15 + n_cast:22 + n_cast]
    cast_out = refs[22 + n_cast:22 + 2 * n_cast]
    hb_s, *z_s = refs[22 + 2 * n_cast:]
    tb = x_ref.shape[0] // 2
    mod_in = jnp.minimum(pl.program_id(0), pl.num_programs(0) - 2) // blocks_per_mod
    w = {"u": wu_ref, "v": wv_ref, "qk": wqk_ref, "o": wo_ref, "vvt": wvvt_ref, "gt": wgt_ref}
    c = {"gv": gv_ref, "ws": ws_ref, "bs": bs_ref, "cw": cw_ref, "cb": cb_ref, "bgt": bgt_ref}

    def tick(half, norm, project, finish):
        rows = slice(half * tb, (half + 1) * tb)
        cur, oth = half, 1 - half
        z_cur = {name: ref.at[cur] for name, ref in zip(_INPROJ_Z, z_s)}
        z_oth = {name: ref.at[oth] for name, ref in zip(_INPROJ_Z, z_s)}
        out = {"a": a_ref.at[rows, :], "qt": qt_ref.at[:, rows], "k": k_ref.at[rows, :],
               "vt": vt_ref.at[:, rows], "o": o_ref.at[rows, :], "gc": gc_ref.at[rows, :],
               "gr": gr_ref.at[:, rows]}
        if project:
            _inproj_project(hb_s.at[oth], w, z_oth)
        if norm:
            _inproj_norm(x_ref[rows, :], mod_ref[mod_in], g1_ref[...], hb_s.at[cur])
        if finish:
            _inproj_finish(z_cur, c, out, seg)
        if half == 0:
            for src, dst in zip(cast_in, cast_out):
                dst[...] = src[...].astype(BF16)

    _pipeline_step(tick)


def _inproj(x, mod, rows_per_mod, seg, p, cast=()):
    n = x.shape[0]
    tb = TOKEN_BLOCK
    tb2 = 2 * tb
    assert tb % seg == 0 and n % tb2 == 0 and rows_per_mod % tb2 == 0
    nb = n // tb2
    blk_in = lambda i: jnp.minimum(i, nb - 1)
    blk_out = lambda i: jnp.maximum(i - 1, 0)
    tok = lambda width: pl.BlockSpec((tb2, width), lambda i: (blk_out(i), 0))
    tok_t = lambda height: pl.BlockSpec((height, tb2), lambda i: (0, blk_out(i)))
    in_specs = [
        pl.BlockSpec((tb2, D_MODEL), lambda i: (blk_in(i), 0)),
        _const_spec(mod.shape),
        _const_spec((1, D_MODEL)),
        _const_spec((D_MODEL, D_GMLP)),
        _const_spec((D_MODEL, D_GMLP)),
        _const_spec((D_MODEL, 2 * D_MLSTM)),
        _const_spec((D_MLSTM, D_MODEL)),
        _const_spec((D_MODEL, D_MLSTM)),
        _const_spec((2 * N_UNIT, D_MODEL)),
        _const_spec((2 * N_UNIT, 1)),
        _const_spec((3, 2 * D_MLSTM)),
        _const_spec((1, 2 * D_MLSTM)),
        _const_spec((GROUPS, CHUNK, CHUNK)),
        _const_spec((CHUNK, D_GMLP)),
        _const_spec((1, D_GMLP)),
    ]
    out_shape = [
        jax.ShapeDtypeStruct((n, D_GMLP), BF16),
        jax.ShapeDtypeStruct((D_MLSTM, n), BF16),
        jax.ShapeDtypeStruct((n, D_MLSTM), BF16),
        jax.ShapeDtypeStruct((D_MLSTM, n), BF16),
        jax.ShapeDtypeStruct((n, D_MLSTM), F32),
        jax.ShapeDtypeStruct((n, N_UNIT), F32),
        jax.ShapeDtypeStruct((GR_ROWS, n), F32),
    ]
    out_specs = [tok(D_GMLP), tok_t(D_MLSTM), tok(D_MLSTM), tok_t(D_MLSTM), tok(D_MLSTM),
                 tok(N_UNIT), tok_t(GR_ROWS)]
    for wf in cast:
        rows = wf.shape[0] // nb
        assert rows * nb == wf.shape[0] and rows % 16 == 0
        slab = pl.BlockSpec((rows, wf.shape[1]), lambda i: (blk_in(i), 0))
        in_specs.append(slab)
        out_specs.append(slab)
        out_shape.append(jax.ShapeDtypeStruct(wf.shape, BF16))
    scratch_shapes = [
        pltpu.VMEM((2, tb, D_MODEL), BF16),
        pltpu.VMEM((2, tb, D_GMLP), F32),
        pltpu.VMEM((2, tb, D_GMLP), F32),
        pltpu.VMEM((2, tb, 2 * D_MLSTM), F32),
        pltpu.VMEM((2, tb, D_MLSTM), F32),
        pltpu.VMEM((2, D_MLSTM, tb), BF16),
        pltpu.VMEM((2, 2 * N_UNIT, tb), F32),
    ]
    return pl.pallas_call(
        functools.partial(_inproj_kernel, seg=seg, blocks_per_mod=rows_per_mod // tb2,
                          n_cast=len(cast)),
        out_shape=out_shape,
        grid=(nb + 1,),
        in_specs=in_specs,
        out_specs=out_specs,
        scratch_shapes=scratch_shapes,
        compiler_params=pltpu.CompilerParams(
            dimension_semantics=("arbitrary",), vmem_limit_bytes=VMEM_LIMIT_BYTES),
        name="inproj",
    )(x, mod, p["g1"], p["wu"], p["wv"], p["wqk"], p["wvvt"], p["wo"], p["wgt"],
      p["bgt"], p["cw"], p["cb"], p["ws"], p["bs"], p["gv"], *cast)


def _mlstm_unit(k, vt, kq, qs, dcol, b_row, g_row, bl_row, gmax_row, st, m_row, mask, carried):
    logw = jnp.where(mask, dcol + b_row, NEG)
    a = b_row + m_row
    mj = jnp.maximum(a, jnp.max(logw, axis=0, keepdims=True))
    w = jnp.exp(logw - mj)
    s = kq * w
    num = _dot(vt, s.astype(BF16))
    den = jnp.sum(s, axis=0, keepdims=True)
    if carried:
        inter = jnp.exp(a - mj)
        num = num + inter * qs[:HD]
        den = den + inter * qs[HD:HD + 1]
    h = num * (1.0 / jnp.maximum(jnp.abs(den), jnp.exp(-mj)))
    m_new = jnp.maximum(bl_row + m_row, gmax_row)
    wc = jnp.exp(g_row - m_new)
    pad_row = lax.broadcasted_iota(jnp.int32, (ST_ROWS - HD, wc.shape[1]), 0)
    v_aug = jnp.concatenate(
        [vt.astype(F32) * wc, jnp.where(pad_row == 0, wc, 0.0)], axis=0).astype(BF16)
    st_new = _dot(v_aug, k)
    if carried:
        decay = jnp.exp(bl_row + m_row - m_new)
        st_new = decay[:, :HD] * st + st_new
    return h, st_new, m_new


def _mlstm_kernel(*refs, has_init, has_out, carried):
    if has_init:
        s0_ref, m0_ref = refs[:2]
        refs = refs[2:]
    (kf_ref, qtf_ref, vtf_ref, gcf_ref, grf_ref,
     kb_ref, qtb_ref, vtb_ref, gcb_ref, grb_ref, hf_ref, hb_ref) = refs[:12]
    refs = refs[12:]
    if has_out:
        co_ref, no_ref, mo_ref = refs[:3]
        refs = refs[3:]
    s_ref, m_ref = refs
    c = pl.program_id(1)
    nc = pl.num_programs(1)

    @pl.when(c == 0)
    def _():
        if has_init:
            s_ref[...] = s0_ref[0]
            m_ref[...] = m0_ref[0]
        else:
            s_ref[...] = jnp.zeros_like(s_ref)
            m_ref[...] = jnp.zeros_like(m_ref)

    si = lax.broadcasted_iota(jnp.int32, (SCAN, SCAN), 0)
    ji = lax.broadcasted_iota(jnp.int32, (SCAN, SCAN), 1)
    dirs = (
        (kf_ref, qtf_ref, vtf_ref, gcf_ref, grf_ref, hf_ref, si <= ji),
        (kb_ref, qtb_ref, vtb_ref, gcb_ref, grb_ref, hb_ref, si >= ji),
    )
    m_all = m_ref[...]
    first = {}
    for d, (k_ref, qt_ref, _, _, _, _, _) in enumerate(dirs):
        for hd in range(HEADS):
            u = d * HEADS + hd
            hs = slice(hd * HD, (hd + 1) * HD)
            kq = _dot(k_ref[:, hs], qt_ref[hs, :])
            qs = _dot(s_ref[u].astype(BF16), qt_ref[hs, :]) if carried else None
            first[u] = (kq, qs)
    m_news = []
    for d, (k_ref, qt_ref, vt_ref, gc_ref, gr_ref, h_ref, mask) in enumerate(dirs):
        for hd in range(HEADS):
            u = d * HEADS + hd
            hs = slice(hd * HD, (hd + 1) * HD)
            h, st_new, m_new = _mlstm_unit(
                k_ref[:, hs], vt_ref[hs, :], *first[u],
                gc_ref[:, u:u + 1],
                gr_ref[u:u + 1, :],
                gr_ref[N_UNIT + u:N_UNIT + u + 1, :],
                gr_ref[2 * N_UNIT + u:2 * N_UNIT + u + 1, :],
                gr_ref[3 * N_UNIT + u:3 * N_UNIT + u + 1, :],
                s_ref[u], m_all[u:u + 1], mask, carried)
            h_ref[hs, :] = h
            s_ref[u] = st_new
            m_news.append(m_new)
    m_next = jnp.concatenate(m_news, axis=0)
    m_ref[...] = m_next

    if has_out:
        @pl.when(c == nc - 1)
        def _():
            for u in range(N_UNIT):
                co_ref[0, u] = jnp.transpose(s_ref[u, 0:HD, :])
                no_ref[0, u:u + 1, :] = s_ref[u, HD:HD + 1, :]
            mo_ref[0] = m_next


def _mlstm(k, qt, vt, gc, gr, batch, s0=None, m0=None, want_state=False):
    n = k.shape[0]
    nc = n // batch // SCAN
    fwd = lambda b, c: b * nc + c
    bwd = lambda b, c: b * nc + nc - 1 - c

    def specs(ix):
        return [
            pl.BlockSpec((SCAN, D_MLSTM), lambda b, c: (ix(b, c), 0)),
            pl.BlockSpec((D_MLSTM, SCAN), lambda b, c: (0, ix(b, c))),
            pl.BlockSpec((D_MLSTM, SCAN), lambda b, c: (0, ix(b, c))),
            pl.BlockSpec((SCAN, N_UNIT), lambda b, c: (ix(b, c), 0)),
            pl.BlockSpec((GR_ROWS, SCAN), lambda b, c: (0, ix(b, c))),
        ]

    in_specs = specs(fwd) + specs(bwd)
    args = [k, qt, vt, gc, gr, k, qt, vt, gc, gr]
    has_init = s0 is not None
    if has_init:
        in_specs = [
            pl.BlockSpec((1, N_UNIT, ST_ROWS, HD), lambda b, c: (b, 0, 0, 0)),
            pl.BlockSpec((1, N_UNIT, SCAN), lambda b, c: (b, 0, 0)),
        ] + in_specs
        args = [s0, m0] + args
    out_shape = [
        jax.ShapeDtypeStruct((D_MLSTM, n), F32),
        jax.ShapeDtypeStruct((D_MLSTM, n), F32),
    ]
    out_specs = [
        pl.BlockSpec((D_MLSTM, SCAN), lambda b, c: (0, fwd(b, c))),
        pl.BlockSpec((D_MLSTM, SCAN), lambda b, c: (0, bwd(b, c))),
    ]
    if want_state:
        out_shape += [
            jax.ShapeDtypeStruct((batch, N_UNIT, HD, HD), F32),
            jax.ShapeDtypeStruct((batch, N_UNIT, HD), F32),
            jax.ShapeDtypeStruct((batch, N_UNIT, SCAN), F32),
        ]
        out_specs += [
            pl.BlockSpec((1, N_UNIT, HD, HD), lambda b, c: (b, 0, 0, 0)),
            pl.BlockSpec((1, N_UNIT, HD), lambda b, c: (b, 0, 0)),
            pl.BlockSpec((1, N_UNIT, SCAN), lambda b, c: (b, 0, 0)),
        ]
    return pl.pallas_call(
        functools.partial(_mlstm_kernel, has_init=has_init, has_out=want_state,
                          carried=has_init or nc > 1),
        out_shape=out_shape,
        grid=(batch, nc),
        in_specs=in_specs,
        out_specs=out_specs,
        scratch_shapes=[pltpu.VMEM((N_UNIT, ST_ROWS, HD), F32), pltpu.VMEM((N_UNIT, SCAN), F32)],
        compiler_params=pltpu.CompilerParams(
            dimension_semantics=("arbitrary", "arbitrary"), vmem_limit_bytes=VMEM_LIMIT_BYTES),
        name="mlstm",
    )(*args)


def _outffn_mix(x, a, hft, hbt, o, mod, c, x1_ref, h2_ref):
    ga1 = mod[:, 2 * D_MODEL:3 * D_MODEL]
    sh2 = mod[:, 3 * D_MODEL:4 * D_MODEL]
    sc2 = mod[:, 4 * D_MODEL:5 * D_MODEL]
    hs = jnp.transpose(hft + hbt)
    sig = _sigmoid(o)
    parts = []
    for hd in range(HEADS):
        sl = slice(hd * HD, (hd + 1) * HD)
        parts.append(_rms(hs[:, sl], c["gh"][:, sl]) * sig[:, sl])
    b_out = jnp.concatenate(parts, axis=-1).astype(BF16)
    mix = _dot(a, c["wout"][0:D_GMLP, :]) + _dot(b_out, c["wout"][D_GMLP:, :])
    x1 = x + ga1 * mix
    x1_ref[...] = x1
    h2_ref[...] = (_rms(x1, c["g2"][...]) * (1.0 + sc2) + sh2).astype(BF16)


def _outffn_up(h2_ref, x1_in_ref, c, f_ref, x1_out_ref):
    h2 = h2_ref[...]
    u = _dot(h2, c["w1"][...])
    g = _dot(h2, c["w3"][...])
    f_ref[...] = (u * _sigmoid(u) * g).astype(BF16)
    x1_out_ref[...] = x1_in_ref[...]


def _outffn_down(f_ref, x1_ref, mod, c, y_ref):
    ga2 = mod[:, 5 * D_MODEL:6 * D_MODEL]
    x2 = x1_ref[...] + ga2 * _dot(f_ref[...], c["w2"][...])
    y_ref[...] = _rms(x2, c["gf"][...])


def _outffn_kernel(x_ref, a_ref, hft_ref, hbt_ref, o_ref, mod_ref, gh_ref, wout_ref,
                   g2_ref, w1_ref, w3_ref, w2_ref, gf_ref, y_ref,
                   x1a_s, h2_s, f_s, x1b_s, *, blocks_per_mod):
    tb = x_ref.shape[0] // 2
    i = pl.program_id(0)
    mod_in = jnp.minimum(i, pl.num_programs(0) - 2) // blocks_per_mod
    mod_out = jnp.maximum(i - 1, 0) // blocks_per_mod
    c = {"gh": gh_ref, "wout": wout_ref, "g2": g2_ref, "w1": w1_ref, "w3": w3_ref, "w2": w2_ref,
         "gf": gf_ref}

    def tick(half, mix, up, down):
        rows = slice(half * tb, (half + 1) * tb)
        cur, oth = half, 1 - half
        if down:
            _outffn_down(f_s.at[cur], x1b_s.at[cur], mod_ref[mod_out], c, y_ref.at[rows, :])
        if mix:
            _outffn_mix(x_ref[rows, :], a_ref[rows, :], hft_ref[:, rows], hbt_ref[:, rows],
                        o_ref[rows, :], mod_ref[mod_in], c, x1a_s.at[cur], h2_s.at[cur])
        if up:
            _outffn_up(h2_s.at[oth], x1a_s.at[oth], c, f_s.at[oth], x1b_s.at[oth])

    _pipeline_step(tick)


def _outffn(x, a, hft, hbt, o, mod, rows_per_mod, p):
    n = x.shape[0]
    tb = TOKEN_BLOCK
    tb2 = 2 * tb
    assert n % tb2 == 0 and rows_per_mod % tb2 == 0
    nb = n // tb2
    d_ff = p["w1"].shape[1]
    blk_in = lambda i: jnp.minimum(i, nb - 1)
    blk_out = lambda i: jnp.maximum(i - 1, 0)
    tok = lambda w: pl.BlockSpec((tb2, w), lambda i: (blk_in(i), 0))
    tok_t = lambda h: pl.BlockSpec((h, tb2), lambda i: (0, blk_in(i)))
    in_specs = [
        tok(D_MODEL), tok(D_GMLP), tok_t(D_MLSTM), tok_t(D_MLSTM), tok(D_MLSTM),
        _const_spec(mod.shape),
        _const_spec((1, D_MLSTM)),
        _const_spec((D_GMLP + D_MLSTM, D_MODEL)),
        _const_spec((1, D_MODEL)),
        _const_spec((D_MODEL, d_ff)),
        _const_spec((D_MODEL, d_ff)),
        _const_spec((d_ff, D_MODEL)),
        _const_spec((1, D_MODEL)),
    ]
    scratch_shapes = [
        pltpu.VMEM((2, tb, D_MODEL), F32),
        pltpu.VMEM((2, tb, D_MODEL), BF16),
        pltpu.VMEM((2, tb, d_ff), BF16),
        pltpu.VMEM((2, tb, D_MODEL), F32),
    ]
    return pl.pallas_call(
        functools.partial(_outffn_kernel, blocks_per_mod=rows_per_mod // tb2),
        out_shape=jax.ShapeDtypeStruct((n, D_MODEL), F32),
        grid=(nb + 1,),
        in_specs=in_specs,
        out_specs=pl.BlockSpec((tb2, D_MODEL), lambda i: (blk_out(i), 0)),
        scratch_shapes=scratch_shapes,
        compiler_params=pltpu.CompilerParams(
            dimension_semantics=("arbitrary",), vmem_limit_bytes=VMEM_LIMIT_BYTES),
        name="outffn",
    )(x, a, hft, hbt, o, mod, p["gh"], p["wout"], p["g2"], p["w1"], p["w3"], p["w2"], p["gf"])


def _regroup_gates(g):
    h = HEADS
    return jnp.concatenate([g[:, 0:h], g[:, 2 * h:3 * h], g[:, h:2 * h], g[:, 3 * h:4 * h]], axis=1)


def _layer_params(l, g_norm1, w_in, b_gate, w_s, b_s, g_v, conv_w, conv_b, g_h, g_norm2, g_final):
    wi = w_in[l].astype(BF16)
    off_v, off_q, off_vv, off_o, off_g = 512, 1024, 2048, 2560, 3072
    wg = _regroup_gates(wi[:, off_g:])
    bg = _regroup_gates(b_gate[l][None, :])
    return {
        "g1": g_norm1[l][None, :],
        "wu": wi[:, :off_v], "wv": wi[:, off_v:off_q], "wqk": wi[:, off_q:off_vv],
        "wvvt": wi[:, off_vv:off_o].T, "wo": wi[:, off_o:off_g],
        "wgt": wg.T, "bgt": bg.T,
        "cw": conv_w[l], "cb": conv_b[l][None, :],
        "ws": w_s[l].astype(BF16),
        "bs": jnp.repeat(b_s[l].T, HD, axis=1),
        "gv": g_v[l].reshape(1, D_GMLP),
        "gh": g_h[l].reshape(1, D_MLSTM),
        "g2": g_norm2[l][None, :],
        "gf": g_final[None, :],
    }


_LATE_WEIGHTS = ("wout", "w1", "w3", "w2")


def _trunk(x, mod, rows_per_mod, seg, batch, p, late_f32=None, s0=None, m0=None, want_state=False):
    cast = () if late_f32 is None else tuple(late_f32[name] for name in _LATE_WEIGHTS)
    a, qt, k, vt, o, gc, gr, *late = _inproj(x, mod, rows_per_mod, seg, p, cast)
    if late:
        p = dict(p, **dict(zip(_LATE_WEIGHTS, late)))
    hft, hbt, *state = _mlstm(k, qt, vt, gc, gr, batch, s0, m0, want_state)
    y = _outffn(x, a, hft, hbt, o, mod, rows_per_mod, p)
    return y, state, p


def kernel(x_prompt, x_sample, state_C, state_n, state_m, c, c_ctx, w_ada, b_ada, g_norm1, w_in,
           b_gate, w_s, b_s, g_v, conv_w, conv_b, g_h, w_out, g_norm2, w1, w3, w2, g_final):
    bp, tp, d = x_prompt.shape
    bs_, ts, _ = x_sample.shape
    depth = w_in.shape[0]
    assert depth == 1, "final norm is fused into the layer's last kernel"
    xp = x_prompt.reshape(bp * tp, d)
    xs = x_sample.reshape(bs_ * ts, d)

    cs = jnp.zeros((8, d), F32).at[0].set(c_ctx).at[1:1 + bs_].set(c)
    new_c, new_n, new_m = [], [], []
    for l in range(depth):
        p = _layer_params(l, g_norm1, w_in, b_gate, w_s, b_s, g_v, conv_w, conv_b, g_h, g_norm2,
                          g_final)
        late_f32 = {"wout": w_out[l], "w1": w1[l], "w3": w3[l], "w2": w2[l]}
        mod = _ada(cs, w_ada[l], b_ada[l][None, :])
        mod_ctx = mod[0:1].reshape(1, 1, 6 * d)
        mod_lat = mod[1:1 + bs_].reshape(bs_, 1, 6 * d)

        xp, (c_ctx_out, n_ctx_out, m_ctx_out), p = _trunk(
            xp, mod_ctx, bp * tp, tp, bp, p, late_f32, want_state=True)
        new_c.append(c_ctx_out.reshape(bp, N_DIR, HEADS, HD, HD))
        new_n.append(n_ctx_out.reshape(bp, N_DIR, HEADS, HD))
        new_m.append(m_ctx_out[..., 0].reshape(bp, N_DIR, HEADS))

        s0 = jnp.concatenate(
            [jnp.swapaxes(state_C[:, l], -1, -2), state_n[:, l][..., None, :],
             jnp.zeros((bs_, N_DIR, HEADS, ST_ROWS - HD - 1, HD), F32)],
            axis=-2).reshape(bs_, N_UNIT, ST_ROWS, HD)
        m0 = jnp.broadcast_to(state_m[:, l].reshape(bs_, N_UNIT, 1), (bs_, N_UNIT, SCAN))
        xs, _, _ = _trunk(xs, mod_lat, ts, GRID_W, bs_, p, s0=s0, m0=m0)

    return (xp.reshape(bp, tp, d), xs.reshape(bs_, ts, d),
            jnp.stack(new_c, axis=1), jnp.stack(new_n, axis=1), jnp.stack(new_m, axis=1))
```

```python
import functools

import jax
import jax.numpy as jnp
from jax import lax
from jax.experimental import pallas as pl
from jax.experimental.pallas import tpu as pltpu

D_MODEL = 1024
D_GMLP = 512
D_MLSTM = 512
GROUPS = 4
HEADS = 4
HD = 128
CHUNK = 128
SCAN = 256
N_DIR = 2
N_UNIT = N_DIR * HEADS
GRID_W = 64
EPS = 1e-6
NEG = -1e30
ST_ROWS = HD + 16
GR_ROWS = 5 * N_UNIT

TOKEN_BLOCK = 256
VMEM_LIMIT_BYTES = 56 * 1024 * 1024

F32 = jnp.float32
BF16 = jnp.bfloat16


def _rms(x, g):
    return x * lax.rsqrt(jnp.mean(x * x, axis=-1, keepdims=True) + EPS) * g


def _sigmoid(x):
    return 1.0 / (1.0 + jnp.exp(-x))


def _gelu_tanh(x):
    return 0.5 * x * (1.0 + jnp.tanh(0.7978845608028654 * (x + 0.044715 * (x * x * x))))


def _log_sigmoid(x):
    return jnp.minimum(x, 0.0) - jnp.log(1.0 + jnp.exp(-jnp.abs(x)))


def _dot(a, b):
    return jnp.dot(a, b, preferred_element_type=F32)


def _dot_nt(a, b):
    return lax.dot_general(a, b, (((1,), (1,)), ((), ())), preferred_element_type=F32)


def _dot_exact(a, b):
    return jnp.dot(a, b, preferred_element_type=F32, precision=lax.Precision.HIGHEST)


def _pipeline_step(tick):
    i = pl.program_id(0)
    last = pl.num_programs(0) - 1

    @pl.when(i == 0)
    def _():
        tick(0, True, False, False)
        tick(1, True, True, False)

    @pl.when(jnp.logical_and(i > 0, i < last))
    def _():
        tick(0, True, True, True)
        tick(1, True, True, True)

    @pl.when(i == last)
    def _():
        tick(0, False, True, True)
        tick(1, False, False, True)


def _const_spec(shape):
    zeros = (0,) * len(shape)
    return pl.BlockSpec(shape, lambda *_: zeros, pipeline_mode=pl.Buffered(1))


def _ada_kernel(c_ref, w_ref, b_ref, o_ref):
    c = c_ref[...]
    s = (c * _sigmoid(c)).astype(BF16)
    o_ref[...] = _dot(s, w_ref[...].astype(BF16)) + b_ref[...]


def _ada(cs, w_ada, b_ada):
    rows, d = cs.shape
    n = w_ada.shape[1]
    tn = 1024
    return pl.pallas_call(
        _ada_kernel,
        out_shape=jax.ShapeDtypeStruct((rows, n), F32),
        grid=(n // tn,),
        in_specs=[
            pl.BlockSpec((rows, d), lambda j: (0, 0)),
            pl.BlockSpec((d, tn), lambda j: (0, j)),
            pl.BlockSpec((1, tn), lambda j: (0, j)),
        ],
        out_specs=pl.BlockSpec((rows, tn), lambda j: (0, j)),
        compiler_params=pltpu.CompilerParams(
            dimension_semantics=("arbitrary",), vmem_limit_bytes=VMEM_LIMIT_BYTES),
        name="ada",
    )(cs, w_ada, b_ada)


def _inproj_norm(x, mod, g1, hb_ref):
    sh1 = mod[:, 0:D_MODEL]
    sc1 = mod[:, D_MODEL:2 * D_MODEL]
    hb_ref[...] = (_rms(x, g1) * (1.0 + sc1) + sh1).astype(BF16)


def _inproj_project(hb_ref, w, z):
    hb = hb_ref[...]
    z["u"][...] = _dot(hb, w["u"][...])
    z["v"][...] = _dot(hb, w["v"][...])
    z["qk"][...] = _dot(hb, w["qk"][...])
    z["o"][...] = _dot(hb, w["o"][...])
    z["vt"][...] = _dot_nt(w["vvt"][...], hb).astype(BF16)
    z["gt"][...] = _dot_nt(w["gt"][...], hb)


def _inproj_finish(z, c, out, seg):
    tb = z["u"].shape[0]
    for g in range(GROUPS):
        gs = slice(g * HD, (g + 1) * HD)
        vg = _rms(_gelu_tanh(z["v"][:, gs]), c["gv"][:, gs]).astype(BF16)
        for ch in range(tb // CHUNK):
            cs = slice(ch * CHUNK, (ch + 1) * CHUNK)
            mixed = _dot(c["ws"][g], vg[cs]) + c["bs"][:, gs]
            out["a"][cs, gs] = (_gelu_tanh(z["u"][cs, gs]) * mixed).astype(BF16)

    zqk = z["qk"][...]
    pos = lax.broadcasted_iota(jnp.int32, (tb, 1), 0) % seg
    prev = jnp.where(pos != 0, pltpu.roll(zqk, 1, 0), 0.0)
    nxt = jnp.where(pos != seg - 1, pltpu.roll(zqk, tb - 1, 0), 0.0)
    cw = c["cw"]
    y = c["cb"][...] + prev * cw[0:1, :] + zqk * cw[1:2, :] + nxt * cw[2:3, :]
    y = y * _sigmoid(y)
    out["qt"][...] = jnp.transpose(y[:, :D_MLSTM]).astype(BF16)
    out["k"][...] = (y[:, D_MLSTM:] * (HD ** -0.5)).astype(BF16)
    out["vt"][...] = z["vt"][...]
    out["o"][...] = z["o"][...]

    assert tb == SCAN
    gt = z["gt"][...] + c["bgt"][...]
    gi_t = gt[0:N_UNIT]
    lf_t = _log_sigmoid(gt[N_UNIT:2 * N_UNIT])
    lane = lax.broadcasted_iota(jnp.int32, (N_UNIT, HD), 1)
    tiles = []
    carry = jnp.zeros((N_UNIT, HD), F32)
    for t in range(SCAN // HD):
        prefix = lf_t[:, t * HD:(t + 1) * HD]
        shift = 1
        while shift < HD:
            prefix = prefix + jnp.where(lane >= shift, pltpu.roll(prefix, shift, 1), 0.0)
            shift *= 2
        tiles.append(prefix + carry)
        carry = carry + jnp.broadcast_to(prefix[:, HD - 1:HD], (N_UNIT, HD))
    prefix = jnp.concatenate(tiles, axis=1)
    b_last = jnp.concatenate([carry] * (SCAN // HD), axis=1)
    row_bwd = lax.broadcasted_iota(jnp.int32, (N_UNIT, SCAN), 0) >= HEADS
    b_row = jnp.where(row_bwd, b_last - prefix + lf_t, prefix)
    g_row = b_last - b_row + gi_t
    g_max = jnp.broadcast_to(jnp.max(g_row, axis=1, keepdims=True), (N_UNIT, SCAN))
    out["gr"][0 * N_UNIT:1 * N_UNIT, :] = b_row
    out["gr"][1 * N_UNIT:2 * N_UNIT, :] = g_row
    out["gr"][2 * N_UNIT:3 * N_UNIT, :] = b_last
    out["gr"][3 * N_UNIT:4 * N_UNIT, :] = g_max
    out["gr"][4 * N_UNIT:5 * N_UNIT, :] = gi_t - b_row


_INPROJ_Z = ("u", "v", "qk", "o", "vt", "gt")


def _inproj_kernel(*refs, seg, blocks_per_mod, n_cast):
    (x_ref, mod_ref, g1_ref, wu_ref, wv_ref, wqk_ref, wvvt_ref, wo_ref,
     wgt_ref, bgt_ref, cw_ref, cb_ref, ws_ref, bs_ref, gv_ref) = refs[:15]
    cast_in = refs[15:15 + n_cast]
    a_ref, qt_ref, k_ref, vt_ref, o_ref, gr_ref = refs[15 + n_cast:21 + n_cast]
    cast_out = refs[21 + n_cast:21 + 2 * n_cast]
    hb_s, *z_s = refs[21 + 2 * n_cast:]
    tb = x_ref.shape[0] // 2
    mod_in = jnp.minimum(pl.program_id(0), pl.num_programs(0) - 2) // blocks_per_mod
    w = {"u": wu_ref, "v": wv_ref, "qk": wqk_ref, "o": wo_ref, "vvt": wvvt_ref, "gt": wgt_ref}
    c = {"gv": gv_ref, "ws": ws_ref, "bs": bs_ref, "cw": cw_ref, "cb": cb_ref, "bgt": bgt_ref}

    def tick(half, norm, project, finish):
        rows = slice(half * tb, (half + 1) * tb)
        cur, oth = half, 1 - half
        z_cur = {name: ref.at[cur] for name, ref in zip(_INPROJ_Z, z_s)}
        z_oth = {name: ref.at[oth] for name, ref in zip(_INPROJ_Z, z_s)}
        out = {"a": a_ref.at[rows, :], "qt": qt_ref.at[:, rows], "k": k_ref.at[rows, :],
               "vt": vt_ref.at[:, rows], "o": o_ref.at[rows, :], "gr": gr_ref.at[:, rows]}
        if project:
            _inproj_project(hb_s.at[oth], w, z_oth)
        if norm:
            _inproj_norm(x_ref[rows, :], mod_ref[mod_in], g1_ref[...], hb_s.at[cur])
        if finish:
            _inproj_finish(z_cur, c, out, seg)
        if half == 0:
            for src, dst in zip(cast_in, cast_out):
                dst[...] = src[...].astype(BF16)

    _pipeline_step(tick)


def _inproj(x, mod, rows_per_mod, seg, p, cast=()):
    n = x.shape[0]
    tb = TOKEN_BLOCK
    tb2 = 2 * tb
    assert tb % seg == 0 and n % tb2 == 0 and rows_per_mod % tb2 == 0
    nb = n // tb2
    blk_in = lambda i: jnp.minimum(i, nb - 1)
    blk_out = lambda i: jnp.maximum(i - 1, 0)
    tok = lambda width: pl.BlockSpec((tb2, width), lambda i: (blk_out(i), 0))
    tok_t = lambda height: pl.BlockSpec((height, tb2), lambda i: (0, blk_out(i)))
    in_specs = [
        pl.BlockSpec((tb2, D_MODEL), lambda i: (blk_in(i), 0)),
        _const_spec(mod.shape),
        _const_spec((1, D_MODEL)),
        _const_spec((D_MODEL, D_GMLP)),
        _const_spec((D_MODEL, D_GMLP)),
        _const_spec((D_MODEL, 2 * D_MLSTM)),
        _const_spec((D_MLSTM, D_MODEL)),
        _const_spec((D_MODEL, D_MLSTM)),
        _const_spec((2 * N_UNIT, D_MODEL)),
        _const_spec((2 * N_UNIT, 1)),
        _const_spec((3, 2 * D_MLSTM)),
        _const_spec((1, 2 * D_MLSTM)),
        _const_spec((GROUPS, CHUNK, CHUNK)),
        _const_spec((CHUNK, D_GMLP)),
        _const_spec((1, D_GMLP)),
    ]
    out_shape = [
        jax.ShapeDtypeStruct((n, D_GMLP), BF16),
        jax.ShapeDtypeStruct((D_MLSTM, n), BF16),
        jax.ShapeDtypeStruct((n, D_MLSTM), BF16),
        jax.ShapeDtypeStruct((D_MLSTM, n), BF16),
        jax.ShapeDtypeStruct((n, D_MLSTM), F32),
        jax.ShapeDtypeStruct((GR_ROWS, n), F32),
    ]
    out_specs = [tok(D_GMLP), tok_t(D_MLSTM), tok(D_MLSTM), tok_t(D_MLSTM), tok(D_MLSTM),
                 tok_t(GR_ROWS)]
    for wf in cast:
        rows = wf.shape[0] // nb
        assert rows * nb == wf.shape[0] and rows % 16 == 0
        slab = pl.BlockSpec((rows, wf.shape[1]), lambda i: (blk_in(i), 0))
        in_specs.append(slab)
        out_specs.append(slab)
        out_shape.append(jax.ShapeDtypeStruct(wf.shape, BF16))
    scratch_shapes = [
        pltpu.VMEM((2, tb, D_MODEL), BF16),
        pltpu.VMEM((2, tb, D_GMLP), F32),
        pltpu.VMEM((2, tb, D_GMLP), F32),
        pltpu.VMEM((2, tb, 2 * D_MLSTM), F32),
        pltpu.VMEM((2, tb, D_MLSTM), F32),
        pltpu.VMEM((2, D_MLSTM, tb), BF16),
        pltpu.VMEM((2, 2 * N_UNIT, tb), F32),
    ]
    return pl.pallas_call(
        functools.partial(_inproj_kernel, seg=seg, blocks_per_mod=rows_per_mod // tb2,
                          n_cast=len(cast)),
        out_shape=out_shape,
        grid=(nb + 1,),
        in_specs=in_specs,
        out_specs=out_specs,
        scratch_shapes=scratch_shapes,
        compiler_params=pltpu.CompilerParams(
            dimension_semantics=("arbitrary",), vmem_limit_bytes=VMEM_LIMIT_BYTES),
        name="inproj",
    )(x, mod, p["g1"], p["wu"], p["wv"], p["wqk"], p["wvvt"], p["wo"], p["wgt"],
      p["bgt"], p["cw"], p["cb"], p["ws"], p["bs"], p["gv"], *cast)


def _lane_broadcast_column(row):
    n = row.shape[1]
    tiles = [jnp.transpose(jnp.broadcast_to(row[:, t * HD:(t + 1) * HD], (HD, HD)))
             for t in range(n // HD)]
    col = jnp.concatenate(tiles, axis=0)
    return jnp.concatenate([col] * (n // HD), axis=1)


def _mlstm_unit(k, vt, kq, qs, d_row, b_row, g_row, bl_row, gmax_row, st, m_row, mask, carried):
    logw = jnp.where(mask, _lane_broadcast_column(d_row) + b_row, NEG)
    a = b_row + m_row
    mj = jnp.maximum(a, jnp.max(logw, axis=0, keepdims=True))
    w = jnp.exp(logw - mj)
    s = kq * w
    num = _dot(vt, s.astype(BF16))
    den = jnp.sum(s, axis=0, keepdims=True)
    if carried:
        inter = jnp.exp(a - mj)
        num = num + inter * qs[:HD]
        den = den + inter * qs[HD:HD + 1]
    h = num * (1.0 / jnp.maximum(jnp.abs(den), jnp.exp(-mj)))
    m_new = jnp.maximum(bl_row + m_row, gmax_row)
    wc = jnp.exp(g_row - m_new)
    pad_row = lax.broadcasted_iota(jnp.int32, (ST_ROWS - HD, wc.shape[1]), 0)
    v_aug = jnp.concatenate(
        [vt.astype(F32) * wc, jnp.where(pad_row == 0, wc, 0.0)], axis=0).astype(BF16)
    st_new = _dot(v_aug, k)
    if carried:
        decay = jnp.exp(bl_row + m_row - m_new)
        st_new = decay[:, :HD] * st + st_new
    return h, st_new, m_new


def _mlstm_kernel(*refs, has_init, has_out, carried):
    if has_init:
        s0_ref, m0_ref = refs[:2]
        refs = refs[2:]
    (kf_ref, qtf_ref, vtf_ref, grf_ref,
     kb_ref, qtb_ref, vtb_ref, grb_ref, hf_ref, hb_ref) = refs[:10]
    refs = refs[10:]
    if has_out:
        co_ref, no_ref, mo_ref = refs[:3]
        refs = refs[3:]
    s_ref, m_ref = refs
    c = pl.program_id(1)
    nc = pl.num_programs(1)

    @pl.when(c == 0)
    def _():
        if has_init:
            s_ref[...] = s0_ref[0]
            m_ref[...] = m0_ref[0]
        else:
            s_ref[...] = jnp.zeros_like(s_ref)
            m_ref[...] = jnp.zeros_like(m_ref)

    si = lax.broadcasted_iota(jnp.int32, (SCAN, SCAN), 0)
    ji = lax.broadcasted_iota(jnp.int32, (SCAN, SCAN), 1)
    dirs = (
        (kf_ref, qtf_ref, vtf_ref, grf_ref, hf_ref, si <= ji),
        (kb_ref, qtb_ref, vtb_ref, grb_ref, hb_ref, si >= ji),
    )
    m_all = m_ref[...]
    first = {}
    for d, (k_ref, qt_ref, _, _, _, _) in enumerate(dirs):
        for hd in range(HEADS):
            u = d * HEADS + hd
            hs = slice(hd * HD, (hd + 1) * HD)
            kq = _dot(k_ref[:, hs], qt_ref[hs, :])
            qs = _dot(s_ref[u].astype(BF16), qt_ref[hs, :]) if carried else None
            first[u] = (kq, qs)
    m_news = []
    for d, (k_ref, qt_ref, vt_ref, gr_ref, h_ref, mask) in enumerate(dirs):
        for hd in range(HEADS):
            u = d * HEADS + hd
            hs = slice(hd * HD, (hd + 1) * HD)
            h, st_new, m_new = _mlstm_unit(
                k_ref[:, hs], vt_ref[hs, :], *first[u],
                gr_ref[4 * N_UNIT + u:4 * N_UNIT + u + 1, :],
                gr_ref[u:u + 1, :],
                gr_ref[N_UNIT + u:N_UNIT + u + 1, :],
                gr_ref[2 * N_UNIT + u:2 * N_UNIT + u + 1, :],
                gr_ref[3 * N_UNIT + u:3 * N_UNIT + u + 1, :],
                s_ref[u], m_all[u:u + 1], mask, carried)
            h_ref[hs, :] = h
            s_ref[u] = st_new
            m_news.append(m_new)
    m_next = jnp.concatenate(m_news, axis=0)
    m_ref[...] = m_next

    if has_out:
        @pl.when(c == nc - 1)
        def _():
            for u in range(N_UNIT):
                co_ref[0, u] = jnp.transpose(s_ref[u, 0:HD, :])
                no_ref[0, u:u + 1, :] = s_ref[u, HD:HD + 1, :]
            mo_ref[0] = m_next


def _mlstm(k, qt, vt, gr, batch, s0=None, m0=None, want_state=False):
    n = k.shape[0]
    nc = n // batch // SCAN
    fwd = lambda b, c: b * nc + c
    bwd = lambda b, c: b * nc + nc - 1 - c

    def specs(ix):
        return [
            pl.BlockSpec((SCAN, D_MLSTM), lambda b, c: (ix(b, c), 0)),
            pl.BlockSpec((D_MLSTM, SCAN), lambda b, c: (0, ix(b, c))),
            pl.BlockSpec((D_MLSTM, SCAN), lambda b, c: (0, ix(b, c))),
            pl.BlockSpec((GR_ROWS, SCAN), lambda b, c: (0, ix(b, c))),
        ]

    in_specs = specs(fwd) + specs(bwd)
    args = [k, qt, vt, gr, k, qt, vt, gr]
    has_init = s0 is not None
    if has_init:
        in_specs = [
            pl.BlockSpec((1, N_UNIT, ST_ROWS, HD), lambda b, c: (b, 0, 0, 0)),
            pl.BlockSpec((1, N_UNIT, SCAN), lambda b, c: (b, 0, 0)),
        ] + in_specs
        args = [s0, m0] + args
    out_shape = [
        jax.ShapeDtypeStruct((D_MLSTM, n), F32),
        jax.ShapeDtypeStruct((D_MLSTM, n), F32),
    ]
    out_specs = [
        pl.BlockSpec((D_MLSTM, SCAN), lambda b, c: (0, fwd(b, c))),
        pl.BlockSpec((D_MLSTM, SCAN), lambda b, c: (0, bwd(b, c))),
    ]
    if want_state:
        out_shape += [
            jax.ShapeDtypeStruct((batch, N_UNIT, HD, HD), F32),
            jax.ShapeDtypeStruct((batch, N_UNIT, HD), F32),
            jax.ShapeDtypeStruct((batch, N_UNIT, SCAN), F32),
        ]
        out_specs += [
            pl.BlockSpec((1, N_UNIT, HD, HD), lambda b, c: (b, 0, 0, 0)),
            pl.BlockSpec((1, N_UNIT, HD), lambda b, c: (b, 0, 0)),
            pl.BlockSpec((1, N_UNIT, SCAN), lambda b, c: (b, 0, 0)),
        ]
    return pl.pallas_call(
        functools.partial(_mlstm_kernel, has_init=has_init, has_out=want_state,
                          carried=has_init or nc > 1),
        out_shape=out_shape,
        grid=(batch, nc),
        in_specs=in_specs,
        out_specs=out_specs,
        scratch_shapes=[pltpu.VMEM((N_UNIT, ST_ROWS, HD), F32), pltpu.VMEM((N_UNIT, SCAN), F32)],
        compiler_params=pltpu.CompilerParams(
            dimension_semantics=("arbitrary", "arbitrary"), vmem_limit_bytes=VMEM_LIMIT_BYTES),
        name="mlstm",
    )(*args)


def _outffn_mix(x, a, hft, hbt, o, mod, c, x1_ref, h2_ref):
    ga1 = mod[:, 2 * D_MODEL:3 * D_MODEL]
    sh2 = mod[:, 3 * D_MODEL:4 * D_MODEL]
    sc2 = mod[:, 4 * D_MODEL:5 * D_MODEL]
    hs = jnp.transpose(hft + hbt)
    sig = _sigmoid(o)
    parts = []
    for hd in range(HEADS):
        sl = slice(hd * HD, (hd + 1) * HD)
        parts.append(_rms(hs[:, sl], c["gh"][:, sl]) * sig[:, sl])
    b_out = jnp.concatenate(parts, axis=-1).astype(BF16)
    mix = _dot(a, c["wout"][0:D_GMLP, :]) + _dot(b_out, c["wout"][D_GMLP:, :])
    x1 = x + ga1 * mix
    x1_ref[...] = x1
    h2_ref[...] = (_rms(x1, c["g2"][...]) * (1.0 + sc2) + sh2).astype(BF16)


def _outffn_up(h2_ref, x1_in_ref, c, f_ref, x1_out_ref):
    h2 = h2_ref[...]
    u = _dot(h2, c["w1"][...])
    g = _dot(h2, c["w3"][...])
    f_ref[...] = (u * _sigmoid(u) * g).astype(BF16)
    x1_out_ref[...] = x1_in_ref[...]


def _outffn_down(f_ref, x1_ref, mod, c, y_ref):
    ga2 = mod[:, 5 * D_MODEL:6 * D_MODEL]
    x2 = x1_ref[...] + ga2 * _dot(f_ref[...], c["w2"][...])
    y_ref[...] = _rms(x2, c["gf"][...])


def _outffn_kernel(x_ref, a_ref, hft_ref, hbt_ref, o_ref, mod_ref, gh_ref, wout_ref,
                   g2_ref, w1_ref, w3_ref, w2_ref, gf_ref, y_ref,
                   x1a_s, h2_s, f_s, x1b_s, *, blocks_per_mod):
    tb = x_ref.shape[0] // 2
    i = pl.program_id(0)
    mod_in = jnp.minimum(i, pl.num_programs(0) - 2) // blocks_per_mod
    mod_out = jnp.maximum(i - 1, 0) // blocks_per_mod
    c = {"gh": gh_ref, "wout": wout_ref, "g2": g2_ref, "w1": w1_ref, "w3": w3_ref, "w2": w2_ref,
         "gf": gf_ref}

    def tick(half, mix, up, down):
        rows = slice(half * tb, (half + 1) * tb)
        cur, oth = half, 1 - half
        if down:
            _outffn_down(f_s.at[cur], x1b_s.at[cur], mod_ref[mod_out], c, y_ref.at[rows, :])
        if mix:
            _outffn_mix(x_ref[rows, :], a_ref[rows, :], hft_ref[:, rows], hbt_ref[:, rows],
                        o_ref[rows, :], mod_ref[mod_in], c, x1a_s.at[cur], h2_s.at[cur])
        if up:
            _outffn_up(h2_s.at[oth], x1a_s.at[oth], c, f_s.at[oth], x1b_s.at[oth])

    _pipeline_step(tick)


def _outffn(x, a, hft, hbt, o, mod, rows_per_mod, p):
    n = x.shape[0]
    tb = TOKEN_BLOCK
    tb2 = 2 * tb
    assert n % tb2 == 0 and rows_per_mod % tb2 == 0
    nb = n // tb2
    d_ff = p["w1"].shape[1]
    blk_in = lambda i: jnp.minimum(i, nb - 1)
    blk_out = lambda i: jnp.maximum(i - 1, 0)
    tok = lambda w: pl.BlockSpec((tb2, w), lambda i: (blk_in(i), 0))
    tok_t = lambda h: pl.BlockSpec((h, tb2), lambda i: (0, blk_in(i)))
    in_specs = [
        tok(D_MODEL), tok(D_GMLP), tok_t(D_MLSTM), tok_t(D_MLSTM), tok(D_MLSTM),
        _const_spec(mod.shape),
        _const_spec((1, D_MLSTM)),
        _const_spec((D_GMLP + D_MLSTM, D_MODEL)),
        _const_spec((1, D_MODEL)),
        _const_spec((D_MODEL, d_ff)),
        _const_spec((D_MODEL, d_ff)),
        _const_spec((d_ff, D_MODEL)),
        _const_spec((1, D_MODEL)),
    ]
    scratch_shapes = [
        pltpu.VMEM((2, tb, D_MODEL), F32),
        pltpu.VMEM((2, tb, D_MODEL), BF16),
        pltpu.VMEM((2, tb, d_ff), BF16),
        pltpu.VMEM((2, tb, D_MODEL), F32),
    ]
    return pl.pallas_call(
        functools.partial(_outffn_kernel, blocks_per_mod=rows_per_mod // tb2),
        out_shape=jax.ShapeDtypeStruct((n, D_MODEL), F32),
        grid=(nb + 1,),
        in_specs=in_specs,
        out_specs=pl.BlockSpec((tb2, D_MODEL), lambda i: (blk_out(i), 0)),
        scratch_shapes=scratch_shapes,
        compiler_params=pltpu.CompilerParams(
            dimension_semantics=("arbitrary",), vmem_limit_bytes=VMEM_LIMIT_BYTES),
        name="outffn",
    )(x, a, hft, hbt, o, mod, p["gh"], p["wout"], p["g2"], p["w1"], p["w3"], p["w2"], p["gf"])


def _regroup_gates(g):
    h = HEADS
    return jnp.concatenate([g[:, 0:h], g[:, 2 * h:3 * h], g[:, h:2 * h], g[:, 3 * h:4 * h]], axis=1)


def _layer_params(l, g_norm1, w_in, b_gate, w_s, b_s, g_v, conv_w, conv_b, g_h, g_norm2, g_final):
    wi = w_in[l].astype(BF16)
    off_v, off_q, off_vv, off_o, off_g = 512, 1024, 2048, 2560, 3072
    wg = _regroup_gates(wi[:, off_g:])
    bg = _regroup_gates(b_gate[l][None, :])
    return {
        "g1": g_norm1[l][None, :],
        "wu": wi[:, :off_v], "wv": wi[:, off_v:off_q], "wqk": wi[:, off_q:off_vv],
        "wvvt": wi[:, off_vv:off_o].T, "wo": wi[:, off_o:off_g],
        "wgt": wg.T, "bgt": bg.T,
        "cw": conv_w[l], "cb": conv_b[l][None, :],
        "ws": w_s[l].astype(BF16),
        "bs": jnp.repeat(b_s[l].T, HD, axis=1),
        "gv": g_v[l].reshape(1, D_GMLP),
        "gh": g_h[l].reshape(1, D_MLSTM),
        "g2": g_norm2[l][None, :],
        "gf": g_final[None, :],
    }


_LATE_WEIGHTS = ("wout", "w1", "w3", "w2")


def _trunk(x, mod, rows_per_mod, seg, batch, p, late_f32=None, s0=None, m0=None, want_state=False):
    cast = () if late_f32 is None else tuple(late_f32[name] for name in _LATE_WEIGHTS)
    a, qt, k, vt, o, gr, *late = _inproj(x, mod, rows_per_mod, seg, p, cast)
    if late:
        p = dict(p, **dict(zip(_LATE_WEIGHTS, late)))
    hft, hbt, *state = _mlstm(k, qt, vt, gr, batch, s0, m0, want_state)
    y = _outffn(x, a, hft, hbt, o, mod, rows_per_mod, p)
    return y, state, p


def kernel(x_prompt, x_sample, state_C, state_n, state_m, c, c_ctx, w_ada, b_ada, g_norm1, w_in,
           b_gate, w_s, b_s, g_v, conv_w, conv_b, g_h, w_out, g_norm2, w1, w3, w2, g_final):
    bp, tp, d = x_prompt.shape
    bs_, ts, _ = x_sample.shape
    depth = w_in.shape[0]
    assert depth == 1, "final norm is fused into the layer's last kernel"
    xp = x_prompt.reshape(bp * tp, d)
    xs = x_sample.reshape(bs_ * ts, d)

    cs = jnp.zeros((8, d), F32).at[0].set(c_ctx).at[1:1 + bs_].set(c)
    new_c, new_n, new_m = [], [], []
    for l in range(depth):
        p = _layer_params(l, g_norm1, w_in, b_gate, w_s, b_s, g_v, conv_w, conv_b, g_h, g_norm2,
                          g_final)
        late_f32 = {"wout": w_out[l], "w1": w1[l], "w3": w3[l], "w2": w2[l]}
        mod = _ada(cs, w_ada[l], b_ada[l][None, :])
        mod_ctx = mod[0:1].reshape(1, 1, 6 * d)
        mod_lat = mod[1:1 + bs_].reshape(bs_, 1, 6 * d)

        xp, (c_ctx_out, n_ctx_out, m_ctx_out), p = _trunk(
            xp, mod_ctx, bp * tp, tp, bp, p, late_f32, want_state=True)
        new_c.append(c_ctx_out.reshape(bp, N_DIR, HEADS, HD, HD))
        new_n.append(n_ctx_out.reshape(bp, N_DIR, HEADS, HD))
        new_m.append(m_ctx_out[..., 0].reshape(bp, N_DIR, HEADS))

        s0 = jnp.concatenate(
            [jnp.swapaxes(state_C[:, l], -1, -2), state_n[:, l][..., None, :],
             jnp.zeros((bs_, N_DIR, HEADS, ST_ROWS - HD - 1, HD), F32)],
            axis=-2).reshape(bs_, N_UNIT, ST_ROWS, HD)
        m0 = jnp.broadcast_to(state_m[:, l].reshape(bs_, N_UNIT, 1), (bs_, N_UNIT, SCAN))
        xs, _, _ = _trunk(xs, mod_lat, ts, GRID_W, bs_, p, s0=s0, m0=m0)

    return (xp.reshape(bp, tp, d), xs.reshape(bs_, ts, d),
            jnp.stack(new_c, axis=1), jnp.stack(new_n, axis=1), jnp.stack(new_m, axis=1))
```

```python
import functools

import jax
import jax.numpy as jnp
from jax import lax
from jax.experimental import pallas as pl
from jax.experimental.pallas import tpu as pltpu

D_MODEL = 1024
D_GMLP = 512
D_MLSTM = 512
GROUPS = 4
HEADS = 4
HD = 128
CHUNK = 128
SCAN = 256
N_DIR = 2
N_UNIT = N_DIR * HEADS
GRID_W = 64
EPS = 1e-6
NEG = -1e30
ST_ROWS = HD + 16
GR_ROWS = 5 * N_UNIT

TOKEN_BLOCK = 256
CTX_SEQS_PER_STEP = 2
LAT_CHUNKS_PER_STEP = 2
VMEM_LIMIT_BYTES = 56 * 1024 * 1024

F32 = jnp.float32
BF16 = jnp.bfloat16


def _rms(x, g):
    return x * lax.rsqrt(jnp.mean(x * x, axis=-1, keepdims=True) + EPS) * g


def _sigmoid(x):
    return 1.0 / (1.0 + jnp.exp(-x))


def _gelu_tanh(x):
    return 0.5 * x * (1.0 + jnp.tanh(0.7978845608028654 * (x + 0.044715 * (x * x * x))))


def _log_sigmoid(x):
    return jnp.minimum(x, 0.0) - jnp.log(1.0 + jnp.exp(-jnp.abs(x)))


def _dot(a, b):
    return jnp.dot(a, b, preferred_element_type=F32)


def _dot_nt(a, b):
    return lax.dot_general(a, b, (((1,), (1,)), ((), ())), preferred_element_type=F32)


def _dot_exact(a, b):
    return jnp.dot(a, b, preferred_element_type=F32, precision=lax.Precision.HIGHEST)


def _pipeline_step(tick):
    i = pl.program_id(0)
    last = pl.num_programs(0) - 1

    @pl.when(i == 0)
    def _():
        tick(0, True, False, False)
        tick(1, True, True, False)

    @pl.when(jnp.logical_and(i > 0, i < last))
    def _():
        tick(0, True, True, True)
        tick(1, True, True, True)

    @pl.when(i == last)
    def _():
        tick(0, False, True, True)
        tick(1, False, False, True)


def _const_spec(shape):
    zeros = (0,) * len(shape)
    return pl.BlockSpec(shape, lambda *_: zeros, pipeline_mode=pl.Buffered(1))


def _ada_kernel(c_ref, w_ref, b_ref, o_ref):
    c = c_ref[...]
    s = (c * _sigmoid(c)).astype(BF16)
    o_ref[...] = _dot(s, w_ref[...].astype(BF16)) + b_ref[...]


def _ada(cs, w_ada, b_ada):
    rows, d = cs.shape
    n = w_ada.shape[1]
    tn = 1024
    return pl.pallas_call(
        _ada_kernel,
        out_shape=jax.ShapeDtypeStruct((rows, n), F32),
        grid=(n // tn,),
        in_specs=[
            pl.BlockSpec((rows, d), lambda j: (0, 0)),
            pl.BlockSpec((d, tn), lambda j: (0, j)),
            pl.BlockSpec((1, tn), lambda j: (0, j)),
        ],
        out_specs=pl.BlockSpec((rows, tn), lambda j: (0, j)),
        compiler_params=pltpu.CompilerParams(
            dimension_semantics=("arbitrary",), vmem_limit_bytes=VMEM_LIMIT_BYTES),
        name="ada",
    )(cs, w_ada, b_ada)


def _inproj_norm(x, mod, g1, hb_ref):
    sh1 = mod[:, 0:D_MODEL]
    sc1 = mod[:, D_MODEL:2 * D_MODEL]
    hb_ref[...] = (_rms(x, g1) * (1.0 + sc1) + sh1).astype(BF16)


def _inproj_project(hb_ref, w, z):
    hb = hb_ref[...]
    z["u"][...] = _dot(hb, w["u"][...])
    z["v"][...] = _dot(hb, w["v"][...])
    z["qk"][...] = _dot(hb, w["qk"][...])
    z["o"][...] = _dot(hb, w["o"][...])
    z["vt"][...] = _dot_nt(w["vvt"][...], hb).astype(BF16)
    z["gt"][...] = _dot_nt(w["gt"][...], hb)


def _inproj_finish(z, c, out, seg):
    tb = z["u"].shape[0]
    for g in range(GROUPS):
        gs = slice(g * HD, (g + 1) * HD)
        vg = _rms(_gelu_tanh(z["v"][:, gs]), c["gv"][:, gs]).astype(BF16)
        for ch in range(tb // CHUNK):
            cs = slice(ch * CHUNK, (ch + 1) * CHUNK)
            mixed = _dot(c["ws"][g], vg[cs]) + c["bs"][:, gs]
            out["a"][cs, gs] = (_gelu_tanh(z["u"][cs, gs]) * mixed).astype(BF16)

    zqk = z["qk"][...]
    pos = lax.broadcasted_iota(jnp.int32, (tb, 1), 0) % seg
    prev = jnp.where(pos != 0, pltpu.roll(zqk, 1, 0), 0.0)
    nxt = jnp.where(pos != seg - 1, pltpu.roll(zqk, tb - 1, 0), 0.0)
    cw = c["cw"]
    y = c["cb"][...] + prev * cw[0:1, :] + zqk * cw[1:2, :] + nxt * cw[2:3, :]
    y = y * _sigmoid(y)
    out["qt"][...] = jnp.transpose(y[:, :D_MLSTM]).astype(BF16)
    out["k"][...] = (y[:, D_MLSTM:] * (HD ** -0.5)).astype(BF16)
    out["vt"][...] = z["vt"][...]
    out["o"][...] = z["o"][...]

    assert tb == SCAN
    gt = z["gt"][...] + c["bgt"][...]
    gi_t = gt[0:N_UNIT]
    lf_t = _log_sigmoid(gt[N_UNIT:2 * N_UNIT])
    lane = lax.broadcasted_iota(jnp.int32, (N_UNIT, HD), 1)
    tiles = []
    carry = jnp.zeros((N_UNIT, HD), F32)
    for t in range(SCAN // HD):
        prefix = lf_t[:, t * HD:(t + 1) * HD]
        shift = 1
        while shift < HD:
            prefix = prefix + jnp.where(lane >= shift, pltpu.roll(prefix, shift, 1), 0.0)
            shift *= 2
        tiles.append(prefix + carry)
        carry = carry + jnp.broadcast_to(prefix[:, HD - 1:HD], (N_UNIT, HD))
    prefix = jnp.concatenate(tiles, axis=1)
    b_last = jnp.concatenate([carry] * (SCAN // HD), axis=1)
    row_bwd = lax.broadcasted_iota(jnp.int32, (N_UNIT, SCAN), 0) >= HEADS
    b_row = jnp.where(row_bwd, b_last - prefix + lf_t, prefix)
    g_row = b_last - b_row + gi_t
    g_max = jnp.broadcast_to(jnp.max(g_row, axis=1, keepdims=True), (N_UNIT, SCAN))
    out["gr"][0 * N_UNIT:1 * N_UNIT, :] = b_row
    out["gr"][1 * N_UNIT:2 * N_UNIT, :] = g_row
    out["gr"][2 * N_UNIT:3 * N_UNIT, :] = b_last
    out["gr"][3 * N_UNIT:4 * N_UNIT, :] = g_max
    out["gr"][4 * N_UNIT:5 * N_UNIT, :] = gi_t - b_row


_INPROJ_Z = ("u", "v", "qk", "o", "vt", "gt")


def _inproj_kernel(*refs, seg, blocks_per_mod, n_cast):
    (x_ref, mod_ref, g1_ref, wu_ref, wv_ref, wqk_ref, wvvt_ref, wo_ref,
     wgt_ref, bgt_ref, cw_ref, cb_ref, ws_ref, bs_ref, gv_ref) = refs[:15]
    cast_in = refs[15:15 + n_cast]
    a_ref, qt_ref, k_ref, vt_ref, o_ref, gr_ref = refs[15 + n_cast:21 + n_cast]
    cast_out = refs[21 + n_cast:21 + 2 * n_cast]
    hb_s, *z_s = refs[21 + 2 * n_cast:]
    tb = x_ref.shape[0] // 2
    mod_in = jnp.minimum(pl.program_id(0), pl.num_programs(0) - 2) // blocks_per_mod
    w = {"u": wu_ref, "v": wv_ref, "qk": wqk_ref, "o": wo_ref, "vvt": wvvt_ref, "gt": wgt_ref}
    c = {"gv": gv_ref, "ws": ws_ref, "bs": bs_ref, "cw": cw_ref, "cb": cb_ref, "bgt": bgt_ref}

    def tick(half, norm, project, finish):
        rows = slice(half * tb, (half + 1) * tb)
        cur, oth = half, 1 - half
        z_cur = {name: ref.at[cur] for name, ref in zip(_INPROJ_Z, z_s)}
        z_oth = {name: ref.at[oth] for name, ref in zip(_INPROJ_Z, z_s)}
        out = {"a": a_ref.at[rows, :], "qt": qt_ref.at[:, rows], "k": k_ref.at[rows, :],
               "vt": vt_ref.at[:, rows], "o": o_ref.at[rows, :], "gr": gr_ref.at[:, rows]}
        if project:
            _inproj_project(hb_s.at[oth], w, z_oth)
        if norm:
            _inproj_norm(x_ref[rows, :], mod_ref[mod_in], g1_ref[...], hb_s.at[cur])
        if finish:
            _inproj_finish(z_cur, c, out, seg)
        if half == 0:
            for src, dst in zip(cast_in, cast_out):
                dst[...] = src[...].astype(BF16)

    _pipeline_step(tick)


def _inproj(x, mod, rows_per_mod, seg, p, cast=()):
    n = x.shape[0]
    tb = TOKEN_BLOCK
    tb2 = 2 * tb
    assert tb % seg == 0 and n % tb2 == 0 and rows_per_mod % tb2 == 0
    nb = n // tb2
    blk_in = lambda i: jnp.minimum(i, nb - 1)
    blk_out = lambda i: jnp.maximum(i - 1, 0)
    tok = lambda width: pl.BlockSpec((tb2, width), lambda i: (blk_out(i), 0))
    tok_t = lambda height: pl.BlockSpec((height, tb2), lambda i: (0, blk_out(i)))
    in_specs = [
        pl.BlockSpec((tb2, D_MODEL), lambda i: (blk_in(i), 0)),
        _const_spec(mod.shape),
        _const_spec((1, D_MODEL)),
        _const_spec((D_MODEL, D_GMLP)),
        _const_spec((D_MODEL, D_GMLP)),
        _const_spec((D_MODEL, 2 * D_MLSTM)),
        _const_spec((D_MLSTM, D_MODEL)),
        _const_spec((D_MODEL, D_MLSTM)),
        _const_spec((2 * N_UNIT, D_MODEL)),
        _const_spec((2 * N_UNIT, 1)),
        _const_spec((3, 2 * D_MLSTM)),
        _const_spec((1, 2 * D_MLSTM)),
        _const_spec((GROUPS, CHUNK, CHUNK)),
        _const_spec((CHUNK, D_GMLP)),
        _const_spec((1, D_GMLP)),
    ]
    out_shape = [
        jax.ShapeDtypeStruct((n, D_GMLP), BF16),
        jax.ShapeDtypeStruct((D_MLSTM, n), BF16),
        jax.ShapeDtypeStruct((n, D_MLSTM), BF16),
        jax.ShapeDtypeStruct((D_MLSTM, n), BF16),
        jax.ShapeDtypeStruct((n, D_MLSTM), F32),
        jax.ShapeDtypeStruct((GR_ROWS, n), F32),
    ]
    out_specs = [tok(D_GMLP), tok_t(D_MLSTM), tok(D_MLSTM), tok_t(D_MLSTM), tok(D_MLSTM),
                 tok_t(GR_ROWS)]
    for wf in cast:
        rows = wf.shape[0] // nb
        assert rows * nb == wf.shape[0] and rows % 16 == 0
        slab = pl.BlockSpec((rows, wf.shape[1]), lambda i: (blk_in(i), 0))
        in_specs.append(slab)
        out_specs.append(slab)
        out_shape.append(jax.ShapeDtypeStruct(wf.shape, BF16))
    scratch_shapes = [
        pltpu.VMEM((2, tb, D_MODEL), BF16),
        pltpu.VMEM((2, tb, D_GMLP), F32),
        pltpu.VMEM((2, tb, D_GMLP), F32),
        pltpu.VMEM((2, tb, 2 * D_MLSTM), F32),
        pltpu.VMEM((2, tb, D_MLSTM), F32),
        pltpu.VMEM((2, D_MLSTM, tb), BF16),
        pltpu.VMEM((2, 2 * N_UNIT, tb), F32),
    ]
    return pl.pallas_call(
        functools.partial(_inproj_kernel, seg=seg, blocks_per_mod=rows_per_mod // tb2,
                          n_cast=len(cast)),
        out_shape=out_shape,
        grid=(nb + 1,),
        in_specs=in_specs,
        out_specs=out_specs,
        scratch_shapes=scratch_shapes,
        compiler_params=pltpu.CompilerParams(
            dimension_semantics=("arbitrary",), vmem_limit_bytes=VMEM_LIMIT_BYTES),
        name="inproj",
    )(x, mod, p["g1"], p["wu"], p["wv"], p["wqk"], p["wvvt"], p["wo"], p["wgt"],
      p["bgt"], p["cw"], p["cb"], p["ws"], p["bs"], p["gv"], *cast)


def _lane_broadcast_column(row):
    n = row.shape[1]
    tiles = [jnp.transpose(jnp.broadcast_to(row[:, t * HD:(t + 1) * HD], (HD, HD)))
             for t in range(n // HD)]
    col = jnp.concatenate(tiles, axis=0)
    return jnp.concatenate([col] * (n // HD), axis=1)


def _mlstm_unit(k, vt, kq, qs, d_row, b_row, g_row, bl_row, gmax_row, st, m_row, mask, carried):
    logw = jnp.where(mask, _lane_broadcast_column(d_row) + b_row, NEG)
    a = b_row + m_row
    mj = jnp.maximum(a, jnp.max(logw, axis=0, keepdims=True))
    w = jnp.exp(logw - mj)
    s = kq * w
    num = _dot(vt, s.astype(BF16))
    den = jnp.sum(s, axis=0, keepdims=True)
    if carried:
        inter = jnp.exp(a - mj)
        num = num + inter * qs[:HD]
        den = den + inter * qs[HD:HD + 1]
    h = num * (1.0 / jnp.maximum(jnp.abs(den), jnp.exp(-mj)))
    m_new = jnp.maximum(bl_row + m_row, gmax_row)
    wc = jnp.exp(g_row - m_new)
    pad_row = lax.broadcasted_iota(jnp.int32, (ST_ROWS - HD, wc.shape[1]), 0)
    v_aug = jnp.concatenate(
        [vt.astype(F32) * wc, jnp.where(pad_row == 0, wc, 0.0)], axis=0).astype(BF16)
    st_new = _dot(v_aug, k)
    if carried:
        decay = jnp.exp(bl_row + m_row - m_new)
        st_new = decay[:, :HD] * st + st_new
    return h, st_new, m_new


def _mlstm_kernel(*refs, has_init, has_out, rounds, by_sequence):
    if has_init:
        s0_ref, m0_ref = refs[:2]
        refs = refs[2:]
    if by_sequence:
        k_ref, qt_ref, vt_ref, gr_ref = refs[:4]
        fwd_in = bwd_in = (k_ref, qt_ref, vt_ref, gr_ref)
        refs = refs[4:]
    else:
        fwd_in, bwd_in = refs[:4], refs[4:8]
        refs = refs[8:]
    hf_ref, hb_ref = refs[:2]
    refs = refs[2:]
    if has_out:
        co_ref, no_ref, mo_ref = refs[:3]
        refs = refs[3:]
    s_ref, m_ref = refs
    c = pl.program_id(1)
    nc = pl.num_programs(1)
    carried = not by_sequence

    if carried:
        @pl.when(c == 0)
        def _():
            if has_init:
                s_ref[...] = s0_ref[0]
                m_ref[...] = m0_ref[0]
            else:
                s_ref[...] = jnp.zeros_like(s_ref)
                m_ref[...] = jnp.zeros_like(m_ref)

    si = lax.broadcasted_iota(jnp.int32, (SCAN, SCAN), 0)
    ji = lax.broadcasted_iota(jnp.int32, (SCAN, SCAN), 1)
    dirs = ((fwd_in, hf_ref, si <= ji), (bwd_in, hb_ref, si >= ji))

    def chunk_cols(d, r):
        pos = rounds - 1 - r if (d == 1 and not by_sequence) else r
        return slice(pos * SCAN, (pos + 1) * SCAN)

    kq = {}
    for r in range(rounds):
        for d, ((k_ref, qt_ref, _, _), _, _) in enumerate(dirs):
            cs = chunk_cols(d, r)
            for hd in range(HEADS):
                hs = slice(hd * HD, (hd + 1) * HD)
                kq[r, d, hd] = _dot(k_ref[cs, hs], qt_ref[hs, cs])

    m_cur = None
    if carried:
        m_all = m_ref[...]
        m_cur = [m_all[u:u + 1] for u in range(N_UNIT)]
    zero_row = jnp.zeros((1, SCAN), F32)
    for r in range(rounds):
        slot = r * N_UNIT if by_sequence else 0
        for d, ((k_ref, qt_ref, vt_ref, gr_ref), h_ref, mask) in enumerate(dirs):
            cs = chunk_cols(d, r)
            for hd in range(HEADS):
                u = d * HEADS + hd
                hs = slice(hd * HD, (hd + 1) * HD)
                st = s_ref[slot + u] if carried else None
                qs = _dot(st.astype(BF16), qt_ref[hs, cs]) if carried else None
                h, st_new, m_new = _mlstm_unit(
                    k_ref[cs, hs], vt_ref[hs, cs], kq[r, d, hd], qs,
                    gr_ref[4 * N_UNIT + u:4 * N_UNIT + u + 1, cs],
                    gr_ref[u:u + 1, cs],
                    gr_ref[N_UNIT + u:N_UNIT + u + 1, cs],
                    gr_ref[2 * N_UNIT + u:2 * N_UNIT + u + 1, cs],
                    gr_ref[3 * N_UNIT + u:3 * N_UNIT + u + 1, cs],
                    st, m_cur[u] if carried else zero_row, mask, carried)
                h_ref[hs, cs] = h
                s_ref[slot + u] = st_new
                if carried:
                    m_cur[u] = m_new
                else:
                    m_ref[slot + u:slot + u + 1, :] = m_new
    if carried:
        m_ref[...] = jnp.concatenate(m_cur, axis=0)

    if has_out:
        @pl.when(c == nc - 1)
        def _():
            for j in range(s_ref.shape[0]):
                q, u = divmod(j, N_UNIT)
                co_ref[q, u] = jnp.transpose(s_ref[j, 0:HD, :])
                no_ref[q, u:u + 1, :] = s_ref[j, HD:HD + 1, :]
                mo_ref[q, u:u + 1, :] = m_ref[j:j + 1, :]


def _mlstm(k, qt, vt, gr, batch, rounds, s0=None, m0=None, want_state=False):
    n = k.shape[0]
    nc = n // batch // SCAN
    by_sequence = nc == 1
    width = rounds * SCAN
    has_init = s0 is not None
    if by_sequence:
        assert batch % rounds == 0 and not has_init
        grid = (batch // rounds, 1)
        slots = rounds
        fwd = bwd = lambda b, c: b
    else:
        assert nc % rounds == 0 and not want_state
        steps = nc // rounds
        grid = (batch, steps)
        slots = 1
        fwd = lambda b, c: b * steps + c
        bwd = lambda b, c: b * steps + steps - 1 - c

    def specs(ix):
        return [
            pl.BlockSpec((width, D_MLSTM), lambda b, c: (ix(b, c), 0)),
            pl.BlockSpec((D_MLSTM, width), lambda b, c: (0, ix(b, c))),
            pl.BlockSpec((D_MLSTM, width), lambda b, c: (0, ix(b, c))),
            pl.BlockSpec((GR_ROWS, width), lambda b, c: (0, ix(b, c))),
        ]

    in_specs = specs(fwd) if by_sequence else specs(fwd) + specs(bwd)
    args = [k, qt, vt, gr] if by_sequence else [k, qt, vt, gr, k, qt, vt, gr]
    if has_init:
        in_specs = [
            pl.BlockSpec((1, N_UNIT, ST_ROWS, HD), lambda b, c: (b, 0, 0, 0)),
            pl.BlockSpec((1, N_UNIT, SCAN), lambda b, c: (b, 0, 0)),
        ] + in_specs
        args = [s0, m0] + args
    out_shape = [
        jax.ShapeDtypeStruct((D_MLSTM, n), F32),
        jax.ShapeDtypeStruct((D_MLSTM, n), F32),
    ]
    out_specs = [
        pl.BlockSpec((D_MLSTM, width), lambda b, c: (0, fwd(b, c))),
        pl.BlockSpec((D_MLSTM, width), lambda b, c: (0, bwd(b, c))),
    ]
    if want_state:
        out_shape += [
            jax.ShapeDtypeStruct((batch, N_UNIT, HD, HD), F32),
            jax.ShapeDtypeStruct((batch, N_UNIT, HD), F32),
            jax.ShapeDtypeStruct((batch, N_UNIT, SCAN), F32),
        ]
        out_specs += [
            pl.BlockSpec((slots, N_UNIT, HD, HD), lambda b, c: (b, 0, 0, 0)),
            pl.BlockSpec((slots, N_UNIT, HD), lambda b, c: (b, 0, 0)),
            pl.BlockSpec((slots, N_UNIT, SCAN), lambda b, c: (b, 0, 0)),
        ]
    return pl.pallas_call(
        functools.partial(_mlstm_kernel, has_init=has_init, has_out=want_state, rounds=rounds,
                          by_sequence=by_sequence),
        out_shape=out_shape,
        grid=grid,
        in_specs=in_specs,
        out_specs=out_specs,
        scratch_shapes=[pltpu.VMEM((slots * N_UNIT, ST_ROWS, HD), F32),
                        pltpu.VMEM((slots * N_UNIT, SCAN), F32)],
        compiler_params=pltpu.CompilerParams(
            dimension_semantics=("arbitrary", "arbitrary"), vmem_limit_bytes=VMEM_LIMIT_BYTES),
        name="mlstm",
    )(*args)


def _outffn_mix(x, a, hft, hbt, o, mod, c, x1_ref, h2_ref):
    ga1 = mod[:, 2 * D_MODEL:3 * D_MODEL]
    sh2 = mod[:, 3 * D_MODEL:4 * D_MODEL]
    sc2 = mod[:, 4 * D_MODEL:5 * D_MODEL]
    hs = jnp.transpose(hft + hbt)
    sig = _sigmoid(o)
    parts = []
    for hd in range(HEADS):
        sl = slice(hd * HD, (hd + 1) * HD)
        parts.append(_rms(hs[:, sl], c["gh"][:, sl]) * sig[:, sl])
    b_out = jnp.concatenate(parts, axis=-1).astype(BF16)
    mix = _dot(a, c["wout"][0:D_GMLP, :]) + _dot(b_out, c["wout"][D_GMLP:, :])
    x1 = x + ga1 * mix
    x1_ref[...] = x1
    h2_ref[...] = (_rms(x1, c["g2"][...]) * (1.0 + sc2) + sh2).astype(BF16)


def _outffn_up(h2_ref, x1_in_ref, c, f_ref, x1_out_ref):
    h2 = h2_ref[...]
    u = _dot(h2, c["w1"][...])
    g = _dot(h2, c["w3"][...])
    f_ref[...] = (u * _sigmoid(u) * g).astype(BF16)
    x1_out_ref[...] = x1_in_ref[...]


def _outffn_down(f_ref, x1_ref, mod, c, y_ref):
    ga2 = mod[:, 5 * D_MODEL:6 * D_MODEL]
    x2 = x1_ref[...] + ga2 * _dot(f_ref[...], c["w2"][...])
    y_ref[...] = _rms(x2, c["gf"][...])


def _outffn_kernel(x_ref, a_ref, hft_ref, hbt_ref, o_ref, mod_ref, gh_ref, wout_ref,
                   g2_ref, w1_ref, w3_ref, w2_ref, gf_ref, y_ref,
                   x1a_s, h2_s, f_s, x1b_s, *, blocks_per_mod):
    tb = x_ref.shape[0] // 2
    i = pl.program_id(0)
    mod_in = jnp.minimum(i, pl.num_programs(0) - 2) // blocks_per_mod
    mod_out = jnp.maximum(i - 1, 0) // blocks_per_mod
    c = {"gh": gh_ref, "wout": wout_ref, "g2": g2_ref, "w1": w1_ref, "w3": w3_ref, "w2": w2_ref,
         "gf": gf_ref}

    def tick(half, mix, up, down):
        rows = slice(half * tb, (half + 1) * tb)
        cur, oth = half, 1 - half
        if down:
            _outffn_down(f_s.at[cur], x1b_s.at[cur], mod_ref[mod_out], c, y_ref.at[rows, :])
        if mix:
            _outffn_mix(x_ref[rows, :], a_ref[rows, :], hft_ref[:, rows], hbt_ref[:, rows],
                        o_ref[rows, :], mod_ref[mod_in], c, x1a_s.at[cur], h2_s.at[cur])
        if up:
            _outffn_up(h2_s.at[oth], x1a_s.at[oth], c, f_s.at[oth], x1b_s.at[oth])

    _pipeline_step(tick)


def _outffn(x, a, hft, hbt, o, mod, rows_per_mod, p):
    n = x.shape[0]
    tb = TOKEN_BLOCK
    tb2 = 2 * tb
    assert n % tb2 == 0 and rows_per_mod % tb2 == 0
    nb = n // tb2
    d_ff = p["w1"].shape[1]
    blk_in = lambda i: jnp.minimum(i, nb - 1)
    blk_out = lambda i: jnp.maximum(i - 1, 0)
    tok = lambda w: pl.BlockSpec((tb2, w), lambda i: (blk_in(i), 0))
    tok_t = lambda h: pl.BlockSpec((h, tb2), lambda i: (0, blk_in(i)))
    in_specs = [
        tok(D_MODEL), tok(D_GMLP), tok_t(D_MLSTM), tok_t(D_MLSTM), tok(D_MLSTM),
        _const_spec(mod.shape),
        _const_spec((1, D_MLSTM)),
        _const_spec((D_GMLP + D_MLSTM, D_MODEL)),
        _const_spec((1, D_MODEL)),
        _const_spec((D_MODEL, d_ff)),
        _const_spec((D_MODEL, d_ff)),
        _const_spec((d_ff, D_MODEL)),
        _const_spec((1, D_MODEL)),
    ]
    scratch_shapes = [
        pltpu.VMEM((2, tb, D_MODEL), F32),
        pltpu.VMEM((2, tb, D_MODEL), BF16),
        pltpu.VMEM((2, tb, d_ff), BF16),
        pltpu.VMEM((2, tb, D_MODEL), F32),
    ]
    return pl.pallas_call(
        functools.partial(_outffn_kernel, blocks_per_mod=rows_per_mod // tb2),
        out_shape=jax.ShapeDtypeStruct((n, D_MODEL), F32),
        grid=(nb + 1,),
        in_specs=in_specs,
        out_specs=pl.BlockSpec((tb2, D_MODEL), lambda i: (blk_out(i), 0)),
        scratch_shapes=scratch_shapes,
        compiler_params=pltpu.CompilerParams(
            dimension_semantics=("arbitrary",), vmem_limit_bytes=VMEM_LIMIT_BYTES),
        name="outffn",
    )(x, a, hft, hbt, o, mod, p["gh"], p["wout"], p["g2"], p["w1"], p["w3"], p["w2"], p["gf"])


def _regroup_gates(g):
    h = HEADS
    return jnp.concatenate([g[:, 0:h], g[:, 2 * h:3 * h], g[:, h:2 * h], g[:, 3 * h:4 * h]], axis=1)


def _layer_params(l, g_norm1, w_in, b_gate, w_s, b_s, g_v, conv_w, conv_b, g_h, g_norm2, g_final):
    wi = w_in[l].astype(BF16)
    off_v, off_q, off_vv, off_o, off_g = 512, 1024, 2048, 2560, 3072
    wg = _regroup_gates(wi[:, off_g:])
    bg = _regroup_gates(b_gate[l][None, :])
    return {
        "g1": g_norm1[l][None, :],
        "wu": wi[:, :off_v], "wv": wi[:, off_v:off_q], "wqk": wi[:, off_q:off_vv],
        "wvvt": wi[:, off_vv:off_o].T, "wo": wi[:, off_o:off_g],
        "wgt": wg.T, "bgt": bg.T,
        "cw": conv_w[l], "cb": conv_b[l][None, :],
        "ws": w_s[l].astype(BF16),
        "bs": jnp.repeat(b_s[l].T, HD, axis=1),
        "gv": g_v[l].reshape(1, D_GMLP),
        "gh": g_h[l].reshape(1, D_MLSTM),
        "g2": g_norm2[l][None, :],
        "gf": g_final[None, :],
    }


_LATE_WEIGHTS = ("wout", "w1", "w3", "w2")


def _trunk(x, mod, rows_per_mod, seg, batch, rounds, p, late_f32=None, s0=None, m0=None,
           want_state=False):
    cast = () if late_f32 is None else tuple(late_f32[name] for name in _LATE_WEIGHTS)
    a, qt, k, vt, o, gr, *late = _inproj(x, mod, rows_per_mod, seg, p, cast)
    if late:
        p = dict(p, **dict(zip(_LATE_WEIGHTS, late)))
    hft, hbt, *state = _mlstm(k, qt, vt, gr, batch, rounds, s0, m0, want_state)
    y = _outffn(x, a, hft, hbt, o, mod, rows_per_mod, p)
    return y, state, p


def kernel(x_prompt, x_sample, state_C, state_n, state_m, c, c_ctx, w_ada, b_ada, g_norm1, w_in,
           b_gate, w_s, b_s, g_v, conv_w, conv_b, g_h, w_out, g_norm2, w1, w3, w2, g_final):
    bp, tp, d = x_prompt.shape
    bs_, ts, _ = x_sample.shape
    depth = w_in.shape[0]
    assert depth == 1, "final norm is fused into the layer's last kernel"
    xp = x_prompt.reshape(bp * tp, d)
    xs = x_sample.reshape(bs_ * ts, d)

    cs = jnp.zeros((8, d), F32).at[0].set(c_ctx).at[1:1 + bs_].set(c)
    new_c, new_n, new_m = [], [], []
    for l in range(depth):
        p = _layer_params(l, g_norm1, w_in, b_gate, w_s, b_s, g_v, conv_w, conv_b, g_h, g_norm2,
                          g_final)
        late_f32 = {"wout": w_out[l], "w1": w1[l], "w3": w3[l], "w2": w2[l]}
        mod = _ada(cs, w_ada[l], b_ada[l][None, :])
        mod_ctx = mod[0:1].reshape(1, 1, 6 * d)
        mod_lat = mod[1:1 + bs_].reshape(bs_, 1, 6 * d)

        xp, (c_ctx_out, n_ctx_out, m_ctx_out), p = _trunk(
            xp, mod_ctx, bp * tp, tp, bp, CTX_SEQS_PER_STEP, p, late_f32, want_state=True)
        new_c.append(c_ctx_out.reshape(bp, N_DIR, HEADS, HD, HD))
        new_n.append(n_ctx_out.reshape(bp, N_DIR, HEADS, HD))
        new_m.append(m_ctx_out[..., 0].reshape(bp, N_DIR, HEADS))

        s0 = jnp.concatenate(
            [jnp.swapaxes(state_C[:, l], -1, -2), state_n[:, l][..., None, :],
             jnp.zeros((bs_, N_DIR, HEADS, ST_ROWS - HD - 1, HD), F32)],
            axis=-2).reshape(bs_, N_UNIT, ST_ROWS, HD)
        m0 = jnp.broadcast_to(state_m[:, l].reshape(bs_, N_UNIT, 1), (bs_, N_UNIT, SCAN))
        xs, _, _ = _trunk(xs, mod_lat, ts, GRID_W, bs_, LAT_CHUNKS_PER_STEP, p, s0=s0, m0=m0)

    return (xp.reshape(bp, tp, d), xs.reshape(bs_, ts, d),
            jnp.stack(new_c, axis=1), jnp.stack(new_n, axis=1), jnp.stack(new_m, axis=1))
```

```python
import functools

import jax
import jax.numpy as jnp
from jax import lax
from jax.experimental import pallas as pl
from jax.experimental.pallas import tpu as pltpu

D_MODEL = 1024
D_GMLP = 512
D_MLSTM = 512
GROUPS = 4
HEADS = 4
HD = 128
CHUNK = 128
SCAN = 256
N_DIR = 2
N_UNIT = N_DIR * HEADS
GRID_W = 64
EPS = 1e-6
NEG = -1e30
ST_ROWS = HD + 16
GR_ROWS = 5 * N_UNIT

TOKEN_BLOCK = 256
CTX_SEQS_PER_STEP = 4
LAT_CHUNKS_PER_STEP = 4
VMEM_LIMIT_BYTES = 56 * 1024 * 1024

F32 = jnp.float32
BF16 = jnp.bfloat16


def _rms(x, g):
    return x * lax.rsqrt(jnp.mean(x * x, axis=-1, keepdims=True) + EPS) * g


def _sigmoid(x):
    return 1.0 / (1.0 + jnp.exp(-x))


def _gelu_tanh(x):
    return 0.5 * x * (1.0 + jnp.tanh(0.7978845608028654 * (x + 0.044715 * (x * x * x))))


def _log_sigmoid(x):
    return jnp.minimum(x, 0.0) - jnp.log(1.0 + jnp.exp(-jnp.abs(x)))


def _dot(a, b):
    return jnp.dot(a, b, preferred_element_type=F32)


def _dot_nt(a, b):
    return lax.dot_general(a, b, (((1,), (1,)), ((), ())), preferred_element_type=F32)


def _dot_exact(a, b):
    return jnp.dot(a, b, preferred_element_type=F32, precision=lax.Precision.HIGHEST)


def _pipeline_step(tick):
    i = pl.program_id(0)
    last = pl.num_programs(0) - 1

    @pl.when(i == 0)
    def _():
        tick(0, True, False, False)
        tick(1, True, True, False)

    @pl.when(jnp.logical_and(i > 0, i < last))
    def _():
        tick(0, True, True, True)
        tick(1, True, True, True)

    @pl.when(i == last)
    def _():
        tick(0, False, True, True)
        tick(1, False, False, True)


def _const_spec(shape):
    zeros = (0,) * len(shape)
    return pl.BlockSpec(shape, lambda *_: zeros, pipeline_mode=pl.Buffered(1))


def _ada_kernel(c_ref, w_ref, b_ref, o_ref):
    c = c_ref[...]
    s = (c * _sigmoid(c)).astype(BF16)
    o_ref[...] = _dot(s, w_ref[...].astype(BF16)) + b_ref[...]


def _ada(cs, w_ada, b_ada):
    rows, d = cs.shape
    n = w_ada.shape[1]
    tn = 1024
    return pl.pallas_call(
        _ada_kernel,
        out_shape=jax.ShapeDtypeStruct((rows, n), F32),
        grid=(n // tn,),
        in_specs=[
            pl.BlockSpec((rows, d), lambda j: (0, 0)),
            pl.BlockSpec((d, tn), lambda j: (0, j)),
            pl.BlockSpec((1, tn), lambda j: (0, j)),
        ],
        out_specs=pl.BlockSpec((rows, tn), lambda j: (0, j)),
        compiler_params=pltpu.CompilerParams(
            dimension_semantics=("arbitrary",), vmem_limit_bytes=VMEM_LIMIT_BYTES),
        name="ada",
    )(cs, w_ada, b_ada)


def _inproj_norm(x, mod, g1, hb_ref):
    sh1 = mod[:, 0:D_MODEL]
    sc1 = mod[:, D_MODEL:2 * D_MODEL]
    hb_ref[...] = (_rms(x, g1) * (1.0 + sc1) + sh1).astype(BF16)


def _inproj_project(hb_ref, w, z):
    hb = hb_ref[...]
    z["u"][...] = _dot(hb, w["u"][...])
    z["v"][...] = _dot(hb, w["v"][...])
    z["qk"][...] = _dot(hb, w["qk"][...])
    z["o"][...] = _dot(hb, w["o"][...])
    z["vt"][...] = _dot_nt(w["vvt"][...], hb).astype(BF16)
    z["gt"][...] = _dot_nt(w["gt"][...], hb)


def _inproj_finish(z, c, out, seg):
    tb = z["u"].shape[0]
    for g in range(GROUPS):
        gs = slice(g * HD, (g + 1) * HD)
        vg = _rms(_gelu_tanh(z["v"][:, gs]), c["gv"][:, gs]).astype(BF16)
        for ch in range(tb // CHUNK):
            cs = slice(ch * CHUNK, (ch + 1) * CHUNK)
            mixed = _dot(c["ws"][g], vg[cs]) + c["bs"][:, gs]
            out["a"][cs, gs] = (_gelu_tanh(z["u"][cs, gs]) * mixed).astype(BF16)

    zqk = z["qk"][...]
    pos = lax.broadcasted_iota(jnp.int32, (tb, 1), 0) % seg
    prev = jnp.where(pos != 0, pltpu.roll(zqk, 1, 0), 0.0)
    nxt = jnp.where(pos != seg - 1, pltpu.roll(zqk, tb - 1, 0), 0.0)
    cw = c["cw"]
    y = c["cb"][...] + prev * cw[0:1, :] + zqk * cw[1:2, :] + nxt * cw[2:3, :]
    y = y * _sigmoid(y)
    out["qt"][...] = jnp.transpose(y[:, :D_MLSTM]).astype(BF16)
    out["k"][...] = (y[:, D_MLSTM:] * (HD ** -0.5)).astype(BF16)
    out["vt"][...] = z["vt"][...]
    out["o"][...] = z["o"][...]

    assert tb == SCAN
    gt = z["gt"][...] + c["bgt"][...]
    gi_t = gt[0:N_UNIT]
    lf_t = _log_sigmoid(gt[N_UNIT:2 * N_UNIT])
    lane = lax.broadcasted_iota(jnp.int32, (N_UNIT, HD), 1)
    tiles = []
    carry = jnp.zeros((N_UNIT, HD), F32)
    for t in range(SCAN // HD):
        prefix = lf_t[:, t * HD:(t + 1) * HD]
        shift = 1
        while shift < HD:
            prefix = prefix + jnp.where(lane >= shift, pltpu.roll(prefix, shift, 1), 0.0)
            shift *= 2
        tiles.append(prefix + carry)
        carry = carry + jnp.broadcast_to(prefix[:, HD - 1:HD], (N_UNIT, HD))
    prefix = jnp.concatenate(tiles, axis=1)
    b_last = jnp.concatenate([carry] * (SCAN // HD), axis=1)
    row_bwd = lax.broadcasted_iota(jnp.int32, (N_UNIT, SCAN), 0) >= HEADS
    b_row = jnp.where(row_bwd, b_last - prefix + lf_t, prefix)
    g_row = b_last - b_row + gi_t
    g_max = jnp.broadcast_to(jnp.max(g_row, axis=1, keepdims=True), (N_UNIT, SCAN))
    out["gr"][0 * N_UNIT:1 * N_UNIT, :] = b_row
    out["gr"][1 * N_UNIT:2 * N_UNIT, :] = g_row
    out["gr"][2 * N_UNIT:3 * N_UNIT, :] = b_last
    out["gr"][3 * N_UNIT:4 * N_UNIT, :] = g_max
    out["gr"][4 * N_UNIT:5 * N_UNIT, :] = gi_t - b_row


_INPROJ_Z = ("u", "v", "qk", "o", "vt", "gt")


def _inproj_kernel(*refs, seg, blocks_per_mod, n_cast):
    (x_ref, mod_ref, g1_ref, wu_ref, wv_ref, wqk_ref, wvvt_ref, wo_ref,
     wgt_ref, bgt_ref, cw_ref, cb_ref, ws_ref, bs_ref, gv_ref) = refs[:15]
    cast_in = refs[15:15 + n_cast]
    a_ref, qt_ref, k_ref, vt_ref, o_ref, gr_ref = refs[15 + n_cast:21 + n_cast]
    cast_out = refs[21 + n_cast:21 + 2 * n_cast]
    hb_s, *z_s = refs[21 + 2 * n_cast:]
    tb = x_ref.shape[0] // 2
    mod_in = jnp.minimum(pl.program_id(0), pl.num_programs(0) - 2) // blocks_per_mod
    w = {"u": wu_ref, "v": wv_ref, "qk": wqk_ref, "o": wo_ref, "vvt": wvvt_ref, "gt": wgt_ref}
    c = {"gv": gv_ref, "ws": ws_ref, "bs": bs_ref, "cw": cw_ref, "cb": cb_ref, "bgt": bgt_ref}

    def tick(half, norm, project, finish):
        rows = slice(half * tb, (half + 1) * tb)
        cur, oth = half, 1 - half
        z_cur = {name: ref.at[cur] for name, ref in zip(_INPROJ_Z, z_s)}
        z_oth = {name: ref.at[oth] for name, ref in zip(_INPROJ_Z, z_s)}
        out = {"a": a_ref.at[rows, :], "qt": qt_ref.at[:, rows], "k": k_ref.at[rows, :],
               "vt": vt_ref.at[:, rows], "o": o_ref.at[rows, :], "gr": gr_ref.at[:, rows]}
        if project:
            _inproj_project(hb_s.at[oth], w, z_oth)
        if norm:
            _inproj_norm(x_ref[rows, :], mod_ref[mod_in], g1_ref[...], hb_s.at[cur])
        if finish:
            _inproj_finish(z_cur, c, out, seg)
        if half == 0:
            for src, dst in zip(cast_in, cast_out):
                dst[...] = src[...].astype(BF16)

    _pipeline_step(tick)


def _inproj(x, mod, rows_per_mod, seg, p, cast=()):
    n = x.shape[0]
    tb = TOKEN_BLOCK
    tb2 = 2 * tb
    assert tb % seg == 0 and n % tb2 == 0 and rows_per_mod % tb2 == 0
    nb = n // tb2
    blk_in = lambda i: jnp.minimum(i, nb - 1)
    blk_out = lambda i: jnp.maximum(i - 1, 0)
    tok = lambda width: pl.BlockSpec((tb2, width), lambda i: (blk_out(i), 0))
    tok_t = lambda height: pl.BlockSpec((height, tb2), lambda i: (0, blk_out(i)))
    in_specs = [
        pl.BlockSpec((tb2, D_MODEL), lambda i: (blk_in(i), 0)),
        _const_spec(mod.shape),
        _const_spec((1, D_MODEL)),
        _const_spec((D_MODEL, D_GMLP)),
        _const_spec((D_MODEL, D_GMLP)),
        _const_spec((D_MODEL, 2 * D_MLSTM)),
        _const_spec((D_MLSTM, D_MODEL)),
        _const_spec((D_MODEL, D_MLSTM)),
        _const_spec((2 * N_UNIT, D_MODEL)),
        _const_spec((2 * N_UNIT, 1)),
        _const_spec((3, 2 * D_MLSTM)),
        _const_spec((1, 2 * D_MLSTM)),
        _const_spec((GROUPS, CHUNK, CHUNK)),
        _const_spec((CHUNK, D_GMLP)),
        _const_spec((1, D_GMLP)),
    ]
    out_shape = [
        jax.ShapeDtypeStruct((n, D_GMLP), BF16),
        jax.ShapeDtypeStruct((D_MLSTM, n), BF16),
        jax.ShapeDtypeStruct((n, D_MLSTM), BF16),
        jax.ShapeDtypeStruct((D_MLSTM, n), BF16),
        jax.ShapeDtypeStruct((n, D_MLSTM), F32),
        jax.ShapeDtypeStruct((GR_ROWS, n), F32),
    ]
    out_specs = [tok(D_GMLP), tok_t(D_MLSTM), tok(D_MLSTM), tok_t(D_MLSTM), tok(D_MLSTM),
                 tok_t(GR_ROWS)]
    for wf in cast:
        rows = wf.shape[0] // nb
        assert rows * nb == wf.shape[0] and rows % 16 == 0
        slab = pl.BlockSpec((rows, wf.shape[1]), lambda i: (blk_in(i), 0))
        in_specs.append(slab)
        out_specs.append(slab)
        out_shape.append(jax.ShapeDtypeStruct(wf.shape, BF16))
    scratch_shapes = [
        pltpu.VMEM((2, tb, D_MODEL), BF16),
        pltpu.VMEM((2, tb, D_GMLP), F32),
        pltpu.VMEM((2, tb, D_GMLP), F32),
        pltpu.VMEM((2, tb, 2 * D_MLSTM), F32),
        pltpu.VMEM((2, tb, D_MLSTM), F32),
        pltpu.VMEM((2, D_MLSTM, tb), BF16),
        pltpu.VMEM((2, 2 * N_UNIT, tb), F32),
    ]
    return pl.pallas_call(
        functools.partial(_inproj_kernel, seg=seg, blocks_per_mod=rows_per_mod // tb2,
                          n_cast=len(cast)),
        out_shape=out_shape,
        grid=(nb + 1,),
        in_specs=in_specs,
        out_specs=out_specs,
        scratch_shapes=scratch_shapes,
        compiler_params=pltpu.CompilerParams(
            dimension_semantics=("arbitrary",), vmem_limit_bytes=VMEM_LIMIT_BYTES),
        name="inproj",
    )(x, mod, p["g1"], p["wu"], p["wv"], p["wqk"], p["wvvt"], p["wo"], p["wgt"],
      p["bgt"], p["cw"], p["cb"], p["ws"], p["bs"], p["gv"], *cast)


def _lane_broadcast_column(row):
    n = row.shape[1]
    tiles = [jnp.transpose(jnp.broadcast_to(row[:, t * HD:(t + 1) * HD], (HD, HD)))
             for t in range(n // HD)]
    col = jnp.concatenate(tiles, axis=0)
    return jnp.concatenate([col] * (n // HD), axis=1)


def _mlstm_unit(k, vt, kq, qs, d_row, b_row, g_row, bl_row, gmax_row, st, m_row, mask, carried):
    logw = jnp.where(mask, _lane_broadcast_column(d_row) + b_row, NEG)
    a = b_row + m_row
    mj = jnp.maximum(a, jnp.max(logw, axis=0, keepdims=True))
    w = jnp.exp(logw - mj)
    s = kq * w
    num = _dot(vt, s.astype(BF16))
    den = jnp.sum(s, axis=0, keepdims=True)
    if carried:
        inter = jnp.exp(a - mj)
        num = num + inter * qs[:HD]
        den = den + inter * qs[HD:HD + 1]
    h = num * (1.0 / jnp.maximum(jnp.abs(den), jnp.exp(-mj)))
    m_new = jnp.maximum(bl_row + m_row, gmax_row)
    wc = jnp.exp(g_row - m_new)
    pad_row = lax.broadcasted_iota(jnp.int32, (ST_ROWS - HD, wc.shape[1]), 0)
    v_aug = jnp.concatenate(
        [vt.astype(F32) * wc, jnp.where(pad_row == 0, wc, 0.0)], axis=0).astype(BF16)
    st_new = _dot(v_aug, k)
    if carried:
        decay = jnp.exp(bl_row + m_row - m_new)
        st_new = decay[:, :HD] * st + st_new
    return h, st_new, m_new


def _mlstm_kernel(*refs, has_init, has_out, rounds, by_sequence):
    if has_init:
        s0_ref, m0_ref = refs[:2]
        refs = refs[2:]
    if by_sequence:
        k_ref, qt_ref, vt_ref, gr_ref = refs[:4]
        fwd_in = bwd_in = (k_ref, qt_ref, vt_ref, gr_ref)
        refs = refs[4:]
    else:
        fwd_in, bwd_in = refs[:4], refs[4:8]
        refs = refs[8:]
    hf_ref, hb_ref = refs[:2]
    refs = refs[2:]
    if has_out:
        co_ref, no_ref, mo_ref = refs[:3]
        refs = refs[3:]
    s_ref, m_ref = refs
    c = pl.program_id(1)
    nc = pl.num_programs(1)
    carried = not by_sequence

    if carried:
        @pl.when(c == 0)
        def _():
            if has_init:
                s_ref[...] = s0_ref[0]
                m_ref[...] = m0_ref[0]
            else:
                s_ref[...] = jnp.zeros_like(s_ref)
                m_ref[...] = jnp.zeros_like(m_ref)

    si = lax.broadcasted_iota(jnp.int32, (SCAN, SCAN), 0)
    ji = lax.broadcasted_iota(jnp.int32, (SCAN, SCAN), 1)
    dirs = ((fwd_in, hf_ref, si <= ji), (bwd_in, hb_ref, si >= ji))

    def chunk_cols(d, r):
        pos = rounds - 1 - r if (d == 1 and not by_sequence) else r
        return slice(pos * SCAN, (pos + 1) * SCAN)

    kq = {}
    for r in range(rounds):
        for d, ((k_ref, qt_ref, _, _), _, _) in enumerate(dirs):
            cs = chunk_cols(d, r)
            for hd in range(HEADS):
                hs = slice(hd * HD, (hd + 1) * HD)
                kq[r, d, hd] = _dot(k_ref[cs, hs], qt_ref[hs, cs])

    m_cur = None
    if carried:
        m_all = m_ref[...]
        m_cur = [m_all[u:u + 1] for u in range(N_UNIT)]
    zero_row = jnp.zeros((1, SCAN), F32)
    for r in range(rounds):
        slot = r * N_UNIT if by_sequence else 0
        for d, ((k_ref, qt_ref, vt_ref, gr_ref), h_ref, mask) in enumerate(dirs):
            cs = chunk_cols(d, r)
            for hd in range(HEADS):
                u = d * HEADS + hd
                hs = slice(hd * HD, (hd + 1) * HD)
                st = s_ref[slot + u] if carried else None
                qs = _dot(st.astype(BF16), qt_ref[hs, cs]) if carried else None
                h, st_new, m_new = _mlstm_unit(
                    k_ref[cs, hs], vt_ref[hs, cs], kq[r, d, hd], qs,
                    gr_ref[4 * N_UNIT + u:4 * N_UNIT + u + 1, cs],
                    gr_ref[u:u + 1, cs],
                    gr_ref[N_UNIT + u:N_UNIT + u + 1, cs],
                    gr_ref[2 * N_UNIT + u:2 * N_UNIT + u + 1, cs],
                    gr_ref[3 * N_UNIT + u:3 * N_UNIT + u + 1, cs],
                    st, m_cur[u] if carried else zero_row, mask, carried)
                h_ref[hs, cs] = h
                s_ref[slot + u] = st_new
                if carried:
                    m_cur[u] = m_new
                else:
                    m_ref[slot + u:slot + u + 1, :] = m_new
    if carried:
        m_ref[...] = jnp.concatenate(m_cur, axis=0)

    if has_out:
        @pl.when(c == nc - 1)
        def _():
            for j in range(s_ref.shape[0]):
                q, u = divmod(j, N_UNIT)
                co_ref[q, u] = jnp.transpose(s_ref[j, 0:HD, :])
                no_ref[q, u:u + 1, :] = s_ref[j, HD:HD + 1, :]
                mo_ref[q, u:u + 1, :] = m_ref[j:j + 1, :]


def _mlstm(k, qt, vt, gr, batch, rounds, s0=None, m0=None, want_state=False):
    n = k.shape[0]
    nc = n // batch // SCAN
    by_sequence = nc == 1
    width = rounds * SCAN
    has_init = s0 is not None
    if by_sequence:
        assert batch % rounds == 0 and not has_init
        grid = (batch // rounds, 1)
        slots = rounds
        fwd = bwd = lambda b, c: b
    else:
        assert nc % rounds == 0 and not want_state
        steps = nc // rounds
        grid = (batch, steps)
        slots = 1
        fwd = lambda b, c: b * steps + c
        bwd = lambda b, c: b * steps + steps - 1 - c

    def specs(ix):
        return [
            pl.BlockSpec((width, D_MLSTM), lambda b, c: (ix(b, c), 0)),
            pl.BlockSpec((D_MLSTM, width), lambda b, c: (0, ix(b, c))),
            pl.BlockSpec((D_MLSTM, width), lambda b, c: (0, ix(b, c))),
            pl.BlockSpec((GR_ROWS, width), lambda b, c: (0, ix(b, c))),
        ]

    in_specs = specs(fwd) if by_sequence else specs(fwd) + specs(bwd)
    args = [k, qt, vt, gr] if by_sequence else [k, qt, vt, gr, k, qt, vt, gr]
    if has_init:
        in_specs = [
            pl.BlockSpec((1, N_UNIT, ST_ROWS, HD), lambda b, c: (b, 0, 0, 0)),
            pl.BlockSpec((1, N_UNIT, SCAN), lambda b, c: (b, 0, 0)),
        ] + in_specs
        args = [s0, m0] + args
    out_shape = [
        jax.ShapeDtypeStruct((D_MLSTM, n), F32),
        jax.ShapeDtypeStruct((D_MLSTM, n), F32),
    ]
    out_specs = [
        pl.BlockSpec((D_MLSTM, width), lambda b, c: (0, fwd(b, c))),
        pl.BlockSpec((D_MLSTM, width), lambda b, c: (0, bwd(b, c))),
    ]
    if want_state:
        out_shape += [
            jax.ShapeDtypeStruct((batch, N_UNIT, HD, HD), F32),
            jax.ShapeDtypeStruct((batch, N_UNIT, HD), F32),
            jax.ShapeDtypeStruct((batch, N_UNIT, SCAN), F32),
        ]
        out_specs += [
            pl.BlockSpec((slots, N_UNIT, HD, HD), lambda b, c: (b, 0, 0, 0)),
            pl.BlockSpec((slots, N_UNIT, HD), lambda b, c: (b, 0, 0)),
            pl.BlockSpec((slots, N_UNIT, SCAN), lambda b, c: (b, 0, 0)),
        ]
    return pl.pallas_call(
        functools.partial(_mlstm_kernel, has_init=has_init, has_out=want_state, rounds=rounds,
                          by_sequence=by_sequence),
        out_shape=out_shape,
        grid=grid,
        in_specs=in_specs,
        out_specs=out_specs,
        scratch_shapes=[pltpu.VMEM((slots * N_UNIT, ST_ROWS, HD), F32),
                        pltpu.VMEM((slots * N_UNIT, SCAN), F32)],
        compiler_params=pltpu.CompilerParams(
            dimension_semantics=("arbitrary", "arbitrary"), vmem_limit_bytes=VMEM_LIMIT_BYTES),
        name="mlstm",
    )(*args)


def _outffn_mix(x, a, hft, hbt, o, mod, c, x1_ref, h2_ref):
    ga1 = mod[:, 2 * D_MODEL:3 * D_MODEL]
    sh2 = mod[:, 3 * D_MODEL:4 * D_MODEL]
    sc2 = mod[:, 4 * D_MODEL:5 * D_MODEL]
    hs = jnp.transpose(hft + hbt)
    sig = _sigmoid(o)
    parts = []
    for hd in range(HEADS):
        sl = slice(hd * HD, (hd + 1) * HD)
        parts.append(_rms(hs[:, sl], c["gh"][:, sl]) * sig[:, sl])
    b_out = jnp.concatenate(parts, axis=-1).astype(BF16)
    mix = _dot(a, c["wout"][0:D_GMLP, :]) + _dot(b_out, c["wout"][D_GMLP:, :])
    x1 = x + ga1 * mix
    x1_ref[...] = x1
    h2_ref[...] = (_rms(x1, c["g2"][...]) * (1.0 + sc2) + sh2).astype(BF16)


def _outffn_up(h2_ref, x1_in_ref, c, f_ref, x1_out_ref):
    h2 = h2_ref[...]
    u = _dot(h2, c["w1"][...])
    g = _dot(h2, c["w3"][...])
    f_ref[...] = (u * _sigmoid(u) * g).astype(BF16)
    x1_out_ref[...] = x1_in_ref[...]


def _outffn_down(f_ref, x1_ref, mod, c, y_ref):
    ga2 = mod[:, 5 * D_MODEL:6 * D_MODEL]
    x2 = x1_ref[...] + ga2 * _dot(f_ref[...], c["w2"][...])
    y_ref[...] = _rms(x2, c["gf"][...])


def _outffn_kernel(x_ref, a_ref, hft_ref, hbt_ref, o_ref, mod_ref, gh_ref, wout_ref,
                   g2_ref, w1_ref, w3_ref, w2_ref, gf_ref, y_ref,
                   x1a_s, h2_s, f_s, x1b_s, *, blocks_per_mod):
    tb = x_ref.shape[0] // 2
    i = pl.program_id(0)
    mod_in = jnp.minimum(i, pl.num_programs(0) - 2) // blocks_per_mod
    mod_out = jnp.maximum(i - 1, 0) // blocks_per_mod
    c = {"gh": gh_ref, "wout": wout_ref, "g2": g2_ref, "w1": w1_ref, "w3": w3_ref, "w2": w2_ref,
         "gf": gf_ref}

    def tick(half, mix, up, down):
        rows = slice(half * tb, (half + 1) * tb)
        cur, oth = half, 1 - half
        if down:
            _outffn_down(f_s.at[cur], x1b_s.at[cur], mod_ref[mod_out], c, y_ref.at[rows, :])
        if mix:
            _outffn_mix(x_ref[rows, :], a_ref[rows, :], hft_ref[:, rows], hbt_ref[:, rows],
                        o_ref[rows, :], mod_ref[mod_in], c, x1a_s.at[cur], h2_s.at[cur])
        if up:
            _outffn_up(h2_s.at[oth], x1a_s.at[oth], c, f_s.at[oth], x1b_s.at[oth])

    _pipeline_step(tick)


def _outffn(x, a, hft, hbt, o, mod, rows_per_mod, p):
    n = x.shape[0]
    tb = TOKEN_BLOCK
    tb2 = 2 * tb
    assert n % tb2 == 0 and rows_per_mod % tb2 == 0
    nb = n // tb2
    d_ff = p["w1"].shape[1]
    blk_in = lambda i: jnp.minimum(i, nb - 1)
    blk_out = lambda i: jnp.maximum(i - 1, 0)
    tok = lambda w: pl.BlockSpec((tb2, w), lambda i: (blk_in(i), 0))
    tok_t = lambda h: pl.BlockSpec((h, tb2), lambda i: (0, blk_in(i)))
    in_specs = [
        tok(D_MODEL), tok(D_GMLP), tok_t(D_MLSTM), tok_t(D_MLSTM), tok(D_MLSTM),
        _const_spec(mod.shape),
        _const_spec((1, D_MLSTM)),
        _const_spec((D_GMLP + D_MLSTM, D_MODEL)),
        _const_spec((1, D_MODEL)),
        _const_spec((D_MODEL, d_ff)),
        _const_spec((D_MODEL, d_ff)),
        _const_spec((d_ff, D_MODEL)),
        _const_spec((1, D_MODEL)),
    ]
    scratch_shapes = [
        pltpu.VMEM((2, tb, D_MODEL), F32),
        pltpu.VMEM((2, tb, D_MODEL), BF16),
        pltpu.VMEM((2, tb, d_ff), BF16),
        pltpu.VMEM((2, tb, D_MODEL), F32),
    ]
    return pl.pallas_call(
        functools.partial(_outffn_kernel, blocks_per_mod=rows_per_mod // tb2),
        out_shape=jax.ShapeDtypeStruct((n, D_MODEL), F32),
        grid=(nb + 1,),
        in_specs=in_specs,
        out_specs=pl.BlockSpec((tb2, D_MODEL), lambda i: (blk_out(i), 0)),
        scratch_shapes=scratch_shapes,
        compiler_params=pltpu.CompilerParams(
            dimension_semantics=("arbitrary",), vmem_limit_bytes=VMEM_LIMIT_BYTES),
        name="outffn",
    )(x, a, hft, hbt, o, mod, p["gh"], p["wout"], p["g2"], p["w1"], p["w3"], p["w2"], p["gf"])


def _regroup_gates(g):
    h = HEADS
    return jnp.concatenate([g[:, 0:h], g[:, 2 * h:3 * h], g[:, h:2 * h], g[:, 3 * h:4 * h]], axis=1)


def _layer_params(l, g_norm1, w_in, b_gate, w_s, b_s, g_v, conv_w, conv_b, g_h, g_norm2, g_final):
    wi = w_in[l].astype(BF16)
    off_v, off_q, off_vv, off_o, off_g = 512, 1024, 2048, 2560, 3072
    wg = _regroup_gates(wi[:, off_g:])
    bg = _regroup_gates(b_gate[l][None, :])
    return {
        "g1": g_norm1[l][None, :],
        "wu": wi[:, :off_v], "wv": wi[:, off_v:off_q], "wqk": wi[:, off_q:off_vv],
        "wvvt": wi[:, off_vv:off_o].T, "wo": wi[:, off_o:off_g],
        "wgt": wg.T, "bgt": bg.T,
        "cw": conv_w[l], "cb": conv_b[l][None, :],
        "ws": w_s[l].astype(BF16),
        "bs": jnp.repeat(b_s[l].T, HD, axis=1),
        "gv": g_v[l].reshape(1, D_GMLP),
        "gh": g_h[l].reshape(1, D_MLSTM),
        "g2": g_norm2[l][None, :],
        "gf": g_final[None, :],
    }


_LATE_WEIGHTS = ("wout", "w1", "w3", "w2")


def _trunk(x, mod, rows_per_mod, seg, batch, rounds, p, late_f32=None, s0=None, m0=None,
           want_state=False):
    cast = () if late_f32 is None else tuple(late_f32[name] for name in _LATE_WEIGHTS)
    a, qt, k, vt, o, gr, *late = _inproj(x, mod, rows_per_mod, seg, p, cast)
    if late:
        p = dict(p, **dict(zip(_LATE_WEIGHTS, late)))
    hft, hbt, *state = _mlstm(k, qt, vt, gr, batch, rounds, s0, m0, want_state)
    y = _outffn(x, a, hft, hbt, o, mod, rows_per_mod, p)
    return y, state, p


def kernel(x_prompt, x_sample, state_C, state_n, state_m, c, c_ctx, w_ada, b_ada, g_norm1, w_in,
           b_gate, w_s, b_s, g_v, conv_w, conv_b, g_h, w_out, g_norm2, w1, w3, w2, g_final):
    bp, tp, d = x_prompt.shape
    bs_, ts, _ = x_sample.shape
    depth = w_in.shape[0]
    assert depth == 1, "final norm is fused into the layer's last kernel"
    xp = x_prompt.reshape(bp * tp, d)
    xs = x_sample.reshape(bs_ * ts, d)

    cs = jnp.zeros((8, d), F32).at[0].set(c_ctx).at[1:1 + bs_].set(c)
    new_c, new_n, new_m = [], [], []
    for l in range(depth):
        p = _layer_params(l, g_norm1, w_in, b_gate, w_s, b_s, g_v, conv_w, conv_b, g_h, g_norm2,
                          g_final)
        late_f32 = {"wout": w_out[l], "w1": w1[l], "w3": w3[l], "w2": w2[l]}
        mod = _ada(cs, w_ada[l], b_ada[l][None, :])
        mod_ctx = mod[0:1].reshape(1, 1, 6 * d)
        mod_lat = mod[1:1 + bs_].reshape(bs_, 1, 6 * d)

        xp, (c_ctx_out, n_ctx_out, m_ctx_out), p = _trunk(
            xp, mod_ctx, bp * tp, tp, bp, CTX_SEQS_PER_STEP, p, late_f32, want_state=True)
        new_c.append(c_ctx_out.reshape(bp, N_DIR, HEADS, HD, HD))
        new_n.append(n_ctx_out.reshape(bp, N_DIR, HEADS, HD))
        new_m.append(m_ctx_out[..., 0].reshape(bp, N_DIR, HEADS))

        s0 = jnp.concatenate(
            [jnp.swapaxes(state_C[:, l], -1, -2), state_n[:, l][..., None, :],
             jnp.zeros((bs_, N_DIR, HEADS, ST_ROWS - HD - 1, HD), F32)],
            axis=-2).reshape(bs_, N_UNIT, ST_ROWS, HD)
        m0 = jnp.broadcast_to(state_m[:, l].reshape(bs_, N_UNIT, 1), (bs_, N_UNIT, SCAN))
        xs, _, _ = _trunk(xs, mod_lat, ts, GRID_W, bs_, LAT_CHUNKS_PER_STEP, p, s0=s0, m0=m0)

    return (xp.reshape(bp, tp, d), xs.reshape(bs_, ts, d),
            jnp.stack(new_c, axis=1), jnp.stack(new_n, axis=1), jnp.stack(new_m, axis=1))
```

```python
import functools

import jax
import jax.numpy as jnp
from jax import lax
from jax.experimental import pallas as pl
from jax.experimental.pallas import tpu as pltpu

D_MODEL = 1024
D_GMLP = 512
D_MLSTM = 512
GROUPS = 4
HEADS = 4
HD = 128
CHUNK = 128
SCAN = 256
N_DIR = 2
N_UNIT = N_DIR * HEADS
GRID_W = 64
OFF_V, OFF_Q, OFF_VV, OFF_O, OFF_G = 512, 1024, 2048, 2560, 3072
D_IN = OFF_G + 2 * N_UNIT
EPS = 1e-6
NEG = -1e30
ST_ROWS = HD + 16
GR_ROWS = 5 * N_UNIT

TOKEN_BLOCK = 256
CTX_SEQS_PER_STEP = 2
LAT_CHUNKS_PER_STEP = 2
VMEM_LIMIT_BYTES = 56 * 1024 * 1024

F32 = jnp.float32
BF16 = jnp.bfloat16


def _rms(x, g):
    return x * lax.rsqrt(jnp.mean(x * x, axis=-1, keepdims=True) + EPS) * g


def _sigmoid(x):
    return 1.0 / (1.0 + jnp.exp(-x))


def _gelu_tanh(x):
    return 0.5 * x * (1.0 + jnp.tanh(0.7978845608028654 * (x + 0.044715 * (x * x * x))))


def _log_sigmoid(x):
    return jnp.minimum(x, 0.0) - jnp.log(1.0 + jnp.exp(-jnp.abs(x)))


def _dot(a, b):
    return jnp.dot(a, b, preferred_element_type=F32)


def _dot_nt(a, b):
    return lax.dot_general(a, b, (((1,), (1,)), ((), ())), preferred_element_type=F32)


def _dot_exact(a, b):
    return jnp.dot(a, b, preferred_element_type=F32, precision=lax.Precision.HIGHEST)


def _pipeline_step(tick):
    i = pl.program_id(0)
    last = pl.num_programs(0) - 1

    @pl.when(i == 0)
    def _():
        tick(0, True, False, False)
        tick(1, True, True, False)

    @pl.when(jnp.logical_and(i > 0, i < last))
    def _():
        tick(0, True, True, True)
        tick(1, True, True, True)

    @pl.when(i == last)
    def _():
        tick(0, False, True, True)
        tick(1, False, False, True)


def _const_spec(shape):
    zeros = (0,) * len(shape)
    return pl.BlockSpec(shape, lambda *_: zeros, pipeline_mode=pl.Buffered(1))


def _ada_kernel(c_ref, w_ref, b_ref, o_ref):
    c = c_ref[...]
    s = (c * _sigmoid(c)).astype(BF16)
    o_ref[...] = _dot(s, w_ref[...].astype(BF16)) + b_ref[...]


def _ada(cs, w_ada, b_ada):
    rows, d = cs.shape
    n = w_ada.shape[1]
    tn = 1024
    return pl.pallas_call(
        _ada_kernel,
        out_shape=jax.ShapeDtypeStruct((rows, n), F32),
        grid=(n // tn,),
        in_specs=[
            pl.BlockSpec((rows, d), lambda j: (0, 0)),
            pl.BlockSpec((d, tn), lambda j: (0, j)),
            pl.BlockSpec((1, tn), lambda j: (0, j)),
        ],
        out_specs=pl.BlockSpec((rows, tn), lambda j: (0, j)),
        compiler_params=pltpu.CompilerParams(
            dimension_semantics=("arbitrary",), vmem_limit_bytes=VMEM_LIMIT_BYTES),
        name="ada",
    )(cs, w_ada, b_ada)


def _inproj_norm(x, mod, g1, hb_ref):
    sh1 = mod[:, 0:D_MODEL]
    sc1 = mod[:, D_MODEL:2 * D_MODEL]
    hb_ref[...] = (_rms(x, g1) * (1.0 + sc1) + sh1).astype(BF16)


def _inproj_project(hb_ref, w, z):
    hb = hb_ref[...]
    z["u"][...] = _dot(hb, w["u"][...])
    z["v"][...] = _dot(hb, w["v"][...])
    z["qk"][...] = _dot(hb, w["qk"][...])
    z["o"][...] = _dot(hb, w["o"][...])
    z["vt"][...] = _dot_nt(w["vvt"][...], hb).astype(BF16)
    z["gt"][...] = _dot_nt(w["gt"][...], hb)


def _inproj_finish(z, c, out, seg):
    tb = z["u"].shape[0]
    for g in range(GROUPS):
        gs = slice(g * HD, (g + 1) * HD)
        vg = _rms(_gelu_tanh(z["v"][:, gs]), c["gv"][:, gs]).astype(BF16)
        for ch in range(tb // CHUNK):
            cs = slice(ch * CHUNK, (ch + 1) * CHUNK)
            mixed = _dot(c["ws"][g], vg[cs]) + c["bs"][:, gs]
            out["a"][cs, gs] = (_gelu_tanh(z["u"][cs, gs]) * mixed).astype(BF16)

    zqk = z["qk"][...]
    pos = lax.broadcasted_iota(jnp.int32, (tb, 1), 0) % seg
    prev = jnp.where(pos != 0, pltpu.roll(zqk, 1, 0), 0.0)
    nxt = jnp.where(pos != seg - 1, pltpu.roll(zqk, tb - 1, 0), 0.0)
    cw = c["cw"]
    y = c["cb"][...] + prev * cw[0:1, :] + zqk * cw[1:2, :] + nxt * cw[2:3, :]
    y = y * _sigmoid(y)
    out["qt"][...] = jnp.transpose(y[:, :D_MLSTM]).astype(BF16)
    out["k"][...] = (y[:, D_MLSTM:] * (HD ** -0.5)).astype(BF16)
    out["vt"][...] = z["vt"][...]
    out["o"][...] = z["o"][...]

    assert tb == SCAN
    gt = z["gt"][...] + c["bgt"][...]
    gi_t = gt[0:N_UNIT]
    lf_t = _log_sigmoid(gt[N_UNIT:2 * N_UNIT])
    lane = lax.broadcasted_iota(jnp.int32, (N_UNIT, HD), 1)
    tiles = []
    carry = jnp.zeros((N_UNIT, HD), F32)
    for t in range(SCAN // HD):
        prefix = lf_t[:, t * HD:(t + 1) * HD]
        shift = 1
        while shift < HD:
            prefix = prefix + jnp.where(lane >= shift, pltpu.roll(prefix, shift, 1), 0.0)
            shift *= 2
        tiles.append(prefix + carry)
        carry = carry + jnp.broadcast_to(prefix[:, HD - 1:HD], (N_UNIT, HD))
    prefix = jnp.concatenate(tiles, axis=1)
    b_last = jnp.concatenate([carry] * (SCAN // HD), axis=1)
    row_bwd = lax.broadcasted_iota(jnp.int32, (N_UNIT, SCAN), 0) >= HEADS
    b_row = jnp.where(row_bwd, b_last - prefix + lf_t, prefix)
    g_row = b_last - b_row + gi_t
    g_max = jnp.broadcast_to(jnp.max(g_row, axis=1, keepdims=True), (N_UNIT, SCAN))
    out["gr"][0 * N_UNIT:1 * N_UNIT, :] = b_row
    out["gr"][1 * N_UNIT:2 * N_UNIT, :] = g_row
    out["gr"][2 * N_UNIT:3 * N_UNIT, :] = b_last
    out["gr"][3 * N_UNIT:4 * N_UNIT, :] = g_max
    out["gr"][4 * N_UNIT:5 * N_UNIT, :] = gi_t - b_row


_INPROJ_Z = ("u", "v", "qk", "o", "vt", "gt")


def _inproj_kernel(*refs, seg, blocks_per_mod, n_cast):
    (x_ref, mod_ref, g1_ref, wi_ref, wvvt_ref,
     wgt_ref, bgt_ref, cw_ref, cb_ref, ws_ref, bs_ref, gv_ref) = refs[:12]
    cast_in = refs[12:12 + n_cast]
    a_ref, qt_ref, k_ref, vt_ref, o_ref, gr_ref = refs[12 + n_cast:18 + n_cast]
    cast_out = refs[18 + n_cast:18 + 2 * n_cast]
    hb_s, *z_s = refs[18 + 2 * n_cast:]
    tb = x_ref.shape[0] // 2
    mod_in = jnp.minimum(pl.program_id(0), pl.num_programs(0) - 2) // blocks_per_mod
    w = {"u": wi_ref.at[:, 0:OFF_V], "v": wi_ref.at[:, OFF_V:OFF_Q], "qk": wi_ref.at[:, OFF_Q:OFF_VV],
         "o": wi_ref.at[:, OFF_O:OFF_G], "vvt": wvvt_ref, "gt": wgt_ref}
    c = {"gv": gv_ref, "ws": ws_ref, "bs": bs_ref, "cw": cw_ref, "cb": cb_ref, "bgt": bgt_ref}

    def tick(half, norm, project, finish):
        rows = slice(half * tb, (half + 1) * tb)
        cur, oth = half, 1 - half
        z_cur = {name: ref.at[cur] for name, ref in zip(_INPROJ_Z, z_s)}
        z_oth = {name: ref.at[oth] for name, ref in zip(_INPROJ_Z, z_s)}
        out = {"a": a_ref.at[rows, :], "qt": qt_ref.at[:, rows], "k": k_ref.at[rows, :],
               "vt": vt_ref.at[:, rows], "o": o_ref.at[rows, :], "gr": gr_ref.at[:, rows]}
        if project:
            _inproj_project(hb_s.at[oth], w, z_oth)
        if norm:
            _inproj_norm(x_ref[rows, :], mod_ref[mod_in], g1_ref[...], hb_s.at[cur])
        if finish:
            _inproj_finish(z_cur, c, out, seg)
        if half == 0:
            for src, dst in zip(cast_in, cast_out):
                dst[...] = src[...].astype(BF16)

    _pipeline_step(tick)


def _inproj(x, mod, rows_per_mod, seg, p, cast=()):
    n = x.shape[0]
    tb = TOKEN_BLOCK
    tb2 = 2 * tb
    assert tb % seg == 0 and n % tb2 == 0 and rows_per_mod % tb2 == 0
    nb = n // tb2
    blk_in = lambda i: jnp.minimum(i, nb - 1)
    blk_out = lambda i: jnp.maximum(i - 1, 0)
    tok = lambda width: pl.BlockSpec((tb2, width), lambda i: (blk_out(i), 0))
    tok_t = lambda height: pl.BlockSpec((height, tb2), lambda i: (0, blk_out(i)))
    in_specs = [
        pl.BlockSpec((tb2, D_MODEL), lambda i: (blk_in(i), 0)),
        _const_spec(mod.shape),
        _const_spec((1, D_MODEL)),
        _const_spec((D_MODEL, D_IN)),
        _const_spec((D_MLSTM, D_MODEL)),
        _const_spec((2 * N_UNIT, D_MODEL)),
        _const_spec((2 * N_UNIT, 1)),
        _const_spec((3, 2 * D_MLSTM)),
        _const_spec((1, 2 * D_MLSTM)),
        _const_spec((GROUPS, CHUNK, CHUNK)),
        _const_spec((CHUNK, D_GMLP)),
        _const_spec((1, D_GMLP)),
    ]
    out_shape = [
        jax.ShapeDtypeStruct((n, D_GMLP), BF16),
        jax.ShapeDtypeStruct((D_MLSTM, n), BF16),
        jax.ShapeDtypeStruct((n, D_MLSTM), BF16),
        jax.ShapeDtypeStruct((D_MLSTM, n), BF16),
        jax.ShapeDtypeStruct((n, D_MLSTM), F32),
        jax.ShapeDtypeStruct((GR_ROWS, n), F32),
    ]
    out_specs = [tok(D_GMLP), tok_t(D_MLSTM), tok(D_MLSTM), tok_t(D_MLSTM), tok(D_MLSTM),
                 tok_t(GR_ROWS)]
    for wf in cast:
        rows = wf.shape[0] // nb
        assert rows * nb == wf.shape[0] and rows % 16 == 0
        slab = pl.BlockSpec((rows, wf.shape[1]), lambda i: (blk_in(i), 0))
        in_specs.append(slab)
        out_specs.append(slab)
        out_shape.append(jax.ShapeDtypeStruct(wf.shape, BF16))
    scratch_shapes = [
        pltpu.VMEM((2, tb, D_MODEL), BF16),
        pltpu.VMEM((2, tb, D_GMLP), F32),
        pltpu.VMEM((2, tb, D_GMLP), F32),
        pltpu.VMEM((2, tb, 2 * D_MLSTM), F32),
        pltpu.VMEM((2, tb, D_MLSTM), F32),
        pltpu.VMEM((2, D_MLSTM, tb), BF16),
        pltpu.VMEM((2, 2 * N_UNIT, tb), F32),
    ]
    return pl.pallas_call(
        functools.partial(_inproj_kernel, seg=seg, blocks_per_mod=rows_per_mod // tb2,
                          n_cast=len(cast)),
        out_shape=out_shape,
        grid=(nb + 1,),
        in_specs=in_specs,
        out_specs=out_specs,
        scratch_shapes=scratch_shapes,
        compiler_params=pltpu.CompilerParams(
            dimension_semantics=("arbitrary",), vmem_limit_bytes=VMEM_LIMIT_BYTES),
        name="inproj",
    )(x, mod, p["g1"], p["wi"], p["wvvt"], p["wgt"],
      p["bgt"], p["cw"], p["cb"], p["ws"], p["bs"], p["gv"], *cast)


def _lane_broadcast_column(row):
    n = row.shape[1]
    tiles = [jnp.transpose(jnp.broadcast_to(row[:, t * HD:(t + 1) * HD], (HD, HD)))
             for t in range(n // HD)]
    col = jnp.concatenate(tiles, axis=0)
    return jnp.concatenate([col] * (n // HD), axis=1)


def _mlstm_unit(k, vt, kq, qs, d_row, b_row, g_row, bl_row, gmax_row, st, m_row, mask, carried):
    logw = jnp.where(mask, _lane_broadcast_column(d_row) + b_row, NEG)
    a = b_row + m_row
    mj = jnp.maximum(a, jnp.max(logw, axis=0, keepdims=True))
    w = jnp.exp(logw - mj)
    s = kq * w
    num = _dot(vt, s.astype(BF16))
    den = jnp.sum(s, axis=0, keepdims=True)
    if carried:
        inter = jnp.exp(a - mj)
        num = num + inter * qs[:HD]
        den = den + inter * qs[HD:HD + 1]
    h = num * (1.0 / jnp.maximum(jnp.abs(den), jnp.exp(-mj)))
    m_new = jnp.maximum(bl_row + m_row, gmax_row)
    wc = jnp.exp(g_row - m_new)
    pad_row = lax.broadcasted_iota(jnp.int32, (ST_ROWS - HD, wc.shape[1]), 0)
    v_aug = jnp.concatenate(
        [vt.astype(F32) * wc, jnp.where(pad_row == 0, wc, 0.0)], axis=0).astype(BF16)
    st_new = _dot(v_aug, k)
    if carried:
        decay = jnp.exp(bl_row + m_row - m_new)
        st_new = decay[:, :HD] * st + st_new
    return h, st_new, m_new


def _mlstm_kernel(*refs, has_init, has_out, rounds, by_sequence):
    if has_init:
        s0_ref, m0_ref = refs[:2]
        refs = refs[2:]
    if by_sequence:
        k_ref, qt_ref, vt_ref, gr_ref = refs[:4]
        fwd_in = bwd_in = (k_ref, qt_ref, vt_ref, gr_ref)
        refs = refs[4:]
    else:
        fwd_in, bwd_in = refs[:4], refs[4:8]
        refs = refs[8:]
    hf_ref, hb_ref = refs[:2]
    refs = refs[2:]
    if has_out:
        co_ref, no_ref, mo_ref = refs[:3]
        refs = refs[3:]
    s_ref, m_ref = refs
    c = pl.program_id(1)
    nc = pl.num_programs(1)
    carried = not by_sequence

    if carried:
        @pl.when(c == 0)
        def _():
            if has_init:
                s_ref[...] = s0_ref[0]
                m_ref[...] = m0_ref[0]
            else:
                s_ref[...] = jnp.zeros_like(s_ref)
                m_ref[...] = jnp.zeros_like(m_ref)

    si = lax.broadcasted_iota(jnp.int32, (SCAN, SCAN), 0)
    ji = lax.broadcasted_iota(jnp.int32, (SCAN, SCAN), 1)
    dirs = ((fwd_in, hf_ref, si <= ji), (bwd_in, hb_ref, si >= ji))

    def chunk_cols(d, r):
        pos = rounds - 1 - r if (d == 1 and not by_sequence) else r
        return slice(pos * SCAN, (pos + 1) * SCAN)

    kq = {}
    for r in range(rounds):
        for d, ((k_ref, qt_ref, _, _), _, _) in enumerate(dirs):
            cs = chunk_cols(d, r)
            for hd in range(HEADS):
                hs = slice(hd * HD, (hd + 1) * HD)
                kq[r, d, hd] = _dot(k_ref[cs, hs], qt_ref[hs, cs])

    m_cur = None
    if carried:
        m_all = m_ref[...]
        m_cur = [m_all[u:u + 1] for u in range(N_UNIT)]
    zero_row = jnp.zeros((1, SCAN), F32)
    for r in range(rounds):
        slot = r * N_UNIT if by_sequence else 0
        for d, ((k_ref, qt_ref, vt_ref, gr_ref), h_ref, mask) in enumerate(dirs):
            cs = chunk_cols(d, r)
            for hd in range(HEADS):
                u = d * HEADS + hd
                hs = slice(hd * HD, (hd + 1) * HD)
                st = s_ref[slot + u] if carried else None
                qs = _dot(st.astype(BF16), qt_ref[hs, cs]) if carried else None
                h, st_new, m_new = _mlstm_unit(
                    k_ref[cs, hs], vt_ref[hs, cs], kq[r, d, hd], qs,
                    gr_ref[4 * N_UNIT + u:4 * N_UNIT + u + 1, cs],
                    gr_ref[u:u + 1, cs],
                    gr_ref[N_UNIT + u:N_UNIT + u + 1, cs],
                    gr_ref[2 * N_UNIT + u:2 * N_UNIT + u + 1, cs],
                    gr_ref[3 * N_UNIT + u:3 * N_UNIT + u + 1, cs],
                    st, m_cur[u] if carried else zero_row, mask, carried)
                h_ref[hs, cs] = h
                s_ref[slot + u] = st_new
                if carried:
                    m_cur[u] = m_new
                else:
                    m_ref[slot + u:slot + u + 1, :] = m_new
    if carried:
        m_ref[...] = jnp.concatenate(m_cur, axis=0)

    if has_out:
        @pl.when(c == nc - 1)
        def _():
            for j in range(s_ref.shape[0]):
                q, u = divmod(j, N_UNIT)
                co_ref[q, u] = jnp.transpose(s_ref[j, 0:HD, :])
                no_ref[q, u:u + 1, :] = s_ref[j, HD:HD + 1, :]
                mo_ref[q, u:u + 1, :] = m_ref[j:j + 1, :]


def _mlstm(k, qt, vt, gr, batch, rounds, s0=None, m0=None, want_state=False):
    n = k.shape[0]
    nc = n // batch // SCAN
    by_sequence = nc == 1
    width = rounds * SCAN
    has_init = s0 is not None
    if by_sequence:
        assert batch % rounds == 0 and not has_init
        grid = (batch // rounds, 1)
        slots = rounds
        fwd = bwd = lambda b, c: b
    else:
        assert nc % rounds == 0 and not want_state
        steps = nc // rounds
        grid = (batch, steps)
        slots = 1
        fwd = lambda b, c: b * steps + c
        bwd = lambda b, c: b * steps + steps - 1 - c

    def specs(ix):
        return [
            pl.BlockSpec((width, D_MLSTM), lambda b, c: (ix(b, c), 0)),
            pl.BlockSpec((D_MLSTM, width), lambda b, c: (0, ix(b, c))),
            pl.BlockSpec((D_MLSTM, width), lambda b, c: (0, ix(b, c))),
            pl.BlockSpec((GR_ROWS, width), lambda b, c: (0, ix(b, c))),
        ]

    in_specs = specs(fwd) if by_sequence else specs(fwd) + specs(bwd)
    args = [k, qt, vt, gr] if by_sequence else [k, qt, vt, gr, k, qt, vt, gr]
    if has_init:
        in_specs = [
            pl.BlockSpec((1, N_UNIT, ST_ROWS, HD), lambda b, c: (b, 0, 0, 0)),
            pl.BlockSpec((1, N_UNIT, SCAN), lambda b, c: (b, 0, 0)),
        ] + in_specs
        args = [s0, m0] + args
    out_shape = [
        jax.ShapeDtypeStruct((D_MLSTM, n), F32),
        jax.ShapeDtypeStruct((D_MLSTM, n), F32),
    ]
    out_specs = [
        pl.BlockSpec((D_MLSTM, width), lambda b, c: (0, fwd(b, c))),
        pl.BlockSpec((D_MLSTM, width), lambda b, c: (0, bwd(b, c))),
    ]
    if want_state:
        out_shape += [
            jax.ShapeDtypeStruct((batch, N_UNIT, HD, HD), F32),
            jax.ShapeDtypeStruct((batch, N_UNIT, HD), F32),
            jax.ShapeDtypeStruct((batch, N_UNIT, SCAN), F32),
        ]
        out_specs += [
            pl.BlockSpec((slots, N_UNIT, HD, HD), lambda b, c: (b, 0, 0, 0)),
            pl.BlockSpec((slots, N_UNIT, HD), lambda b, c: (b, 0, 0)),
            pl.BlockSpec((slots, N_UNIT, SCAN), lambda b, c: (b, 0, 0)),
        ]
    return pl.pallas_call(
        functools.partial(_mlstm_kernel, has_init=has_init, has_out=want_state, rounds=rounds,
                          by_sequence=by_sequence),
        out_shape=out_shape,
        grid=grid,
        in_specs=in_specs,
        out_specs=out_specs,
        scratch_shapes=[pltpu.VMEM((slots * N_UNIT, ST_ROWS, HD), F32),
                        pltpu.VMEM((slots * N_UNIT, SCAN), F32)],
        compiler_params=pltpu.CompilerParams(
            dimension_semantics=("arbitrary", "arbitrary"), vmem_limit_bytes=VMEM_LIMIT_BYTES),
        name="mlstm",
    )(*args)


def _outffn_mix(x, a, hft, hbt, o, mod, c, x1_ref, h2_ref):
    ga1 = mod[:, 2 * D_MODEL:3 * D_MODEL]
    sh2 = mod[:, 3 * D_MODEL:4 * D_MODEL]
    sc2 = mod[:, 4 * D_MODEL:5 * D_MODEL]
    hs = jnp.transpose(hft + hbt)
    sig = _sigmoid(o)
    parts = []
    for hd in range(HEADS):
        sl = slice(hd * HD, (hd + 1) * HD)
        parts.append(_rms(hs[:, sl], c["gh"][:, sl]) * sig[:, sl])
    b_out = jnp.concatenate(parts, axis=-1).astype(BF16)
    mix = _dot(a, c["wout"][0:D_GMLP, :]) + _dot(b_out, c["wout"][D_GMLP:, :])
    x1 = x + ga1 * mix
    x1_ref[...] = x1
    h2_ref[...] = (_rms(x1, c["g2"][...]) * (1.0 + sc2) + sh2).astype(BF16)


def _outffn_up(h2_ref, x1_in_ref, c, f_ref, x1_out_ref):
    h2 = h2_ref[...]
    u = _dot(h2, c["w1"][...])
    g = _dot(h2, c["w3"][...])
    f_ref[...] = (u * _sigmoid(u) * g).astype(BF16)
    x1_out_ref[...] = x1_in_ref[...]


def _outffn_down(f_ref, x1_ref, mod, c, y_ref):
    ga2 = mod[:, 5 * D_MODEL:6 * D_MODEL]
    x2 = x1_ref[...] + ga2 * _dot(f_ref[...], c["w2"][...])
    y_ref[...] = _rms(x2, c["gf"][...])


def _outffn_kernel(x_ref, a_ref, hft_ref, hbt_ref, o_ref, mod_ref, gh_ref, wout_ref,
                   g2_ref, w1_ref, w3_ref, w2_ref, gf_ref, y_ref,
                   x1a_s, h2_s, f_s, x1b_s, *, blocks_per_mod):
    tb = x_ref.shape[0] // 2
    i = pl.program_id(0)
    mod_in = jnp.minimum(i, pl.num_programs(0) - 2) // blocks_per_mod
    mod_out = jnp.maximum(i - 1, 0) // blocks_per_mod
    c = {"gh": gh_ref, "wout": wout_ref, "g2": g2_ref, "w1": w1_ref, "w3": w3_ref, "w2": w2_ref,
         "gf": gf_ref}

    def tick(half, mix, up, down):
        rows = slice(half * tb, (half + 1) * tb)
        cur, oth = half, 1 - half
        if down:
            _outffn_down(f_s.at[cur], x1b_s.at[cur], mod_ref[mod_out], c, y_ref.at[rows, :])
        if mix:
            _outffn_mix(x_ref[rows, :], a_ref[rows, :], hft_ref[:, rows], hbt_ref[:, rows],
                        o_ref[rows, :], mod_ref[mod_in], c, x1a_s.at[cur], h2_s.at[cur])
        if up:
            _outffn_up(h2_s.at[oth], x1a_s.at[oth], c, f_s.at[oth], x1b_s.at[oth])

    _pipeline_step(tick)


def _outffn(x, a, hft, hbt, o, mod, rows_per_mod, p):
    n = x.shape[0]
    tb = TOKEN_BLOCK
    tb2 = 2 * tb
    assert n % tb2 == 0 and rows_per_mod % tb2 == 0
    nb = n // tb2
    d_ff = p["w1"].shape[1]
    blk_in = lambda i: jnp.minimum(i, nb - 1)
    blk_out = lambda i: jnp.maximum(i - 1, 0)
    tok = lambda w: pl.BlockSpec((tb2, w), lambda i: (blk_in(i), 0))
    tok_t = lambda h: pl.BlockSpec((h, tb2), lambda i: (0, blk_in(i)))
    in_specs = [
        tok(D_MODEL), tok(D_GMLP), tok_t(D_MLSTM), tok_t(D_MLSTM), tok(D_MLSTM),
        _const_spec(mod.shape),
        _const_spec((1, D_MLSTM)),
        _const_spec((D_GMLP + D_MLSTM, D_MODEL)),
        _const_spec((1, D_MODEL)),
        _const_spec((D_MODEL, d_ff)),
        _const_spec((D_MODEL, d_ff)),
        _const_spec((d_ff, D_MODEL)),
        _const_spec((1, D_MODEL)),
    ]
    scratch_shapes = [
        pltpu.VMEM((2, tb, D_MODEL), F32),
        pltpu.VMEM((2, tb, D_MODEL), BF16),
        pltpu.VMEM((2, tb, d_ff), BF16),
        pltpu.VMEM((2, tb, D_MODEL), F32),
    ]
    return pl.pallas_call(
        functools.partial(_outffn_kernel, blocks_per_mod=rows_per_mod // tb2),
        out_shape=jax.ShapeDtypeStruct((n, D_MODEL), F32),
        grid=(nb + 1,),
        in_specs=in_specs,
        out_specs=pl.BlockSpec((tb2, D_MODEL), lambda i: (blk_out(i), 0)),
        scratch_shapes=scratch_shapes,
        compiler_params=pltpu.CompilerParams(
            dimension_semantics=("arbitrary",), vmem_limit_bytes=VMEM_LIMIT_BYTES),
        name="outffn",
    )(x, a, hft, hbt, o, mod, p["gh"], p["wout"], p["g2"], p["w1"], p["w3"], p["w2"], p["gf"])


def _regroup_gates(g):
    h = HEADS
    return jnp.concatenate([g[:, 0:h], g[:, 2 * h:3 * h], g[:, h:2 * h], g[:, 3 * h:4 * h]], axis=1)


def _layer_params(l, g_norm1, w_in, b_gate, w_s, b_s, g_v, conv_w, conv_b, g_h, g_norm2, g_final):
    wg = _regroup_gates(w_in[l][:, OFF_G:])
    bg = _regroup_gates(b_gate[l][None, :])
    return {
        "g1": g_norm1[l][None, :],
        "wi": w_in[l].astype(BF16),
        "wvvt": w_in[l][:, OFF_VV:OFF_O].T.astype(BF16),
        "wgt": wg.T.astype(BF16), "bgt": bg.T,
        "cw": conv_w[l], "cb": conv_b[l][None, :],
        "ws": w_s[l].astype(BF16),
        "bs": jnp.repeat(b_s[l].T, HD, axis=1),
        "gv": g_v[l].reshape(1, D_GMLP),
        "gh": g_h[l].reshape(1, D_MLSTM),
        "g2": g_norm2[l][None, :],
        "gf": g_final[None, :],
    }


_LATE_WEIGHTS = ("wout", "w1", "w3", "w2")


def _trunk(x, mod, rows_per_mod, seg, batch, rounds, p, late_f32=None, s0=None, m0=None,
           want_state=False):
    cast = () if late_f32 is None else tuple(late_f32[name] for name in _LATE_WEIGHTS)
    a, qt, k, vt, o, gr, *late = _inproj(x, mod, rows_per_mod, seg, p, cast)
    if late:
        p = dict(p, **dict(zip(_LATE_WEIGHTS, late)))
    hft, hbt, *state = _mlstm(k, qt, vt, gr, batch, rounds, s0, m0, want_state)
    y = _outffn(x, a, hft, hbt, o, mod, rows_per_mod, p)
    return y, state, p


def kernel(x_prompt, x_sample, state_C, state_n, state_m, c, c_ctx, w_ada, b_ada, g_norm1, w_in,
           b_gate, w_s, b_s, g_v, conv_w, conv_b, g_h, w_out, g_norm2, w1, w3, w2, g_final):
    bp, tp, d = x_prompt.shape
    bs_, ts, _ = x_sample.shape
    depth = w_in.shape[0]
    assert depth == 1, "final norm is fused into the layer's last kernel"
    xp = x_prompt.reshape(bp * tp, d)
    xs = x_sample.reshape(bs_ * ts, d)

    cs = jnp.zeros((8, d), F32).at[0].set(c_ctx).at[1:1 + bs_].set(c)
    new_c, new_n, new_m = [], [], []
    for l in range(depth):
        p = _layer_params(l, g_norm1, w_in, b_gate, w_s, b_s, g_v, conv_w, conv_b, g_h, g_norm2,
                          g_final)
        late_f32 = {"wout": w_out[l], "w1": w1[l], "w3": w3[l], "w2": w2[l]}
        mod = _ada(cs, w_ada[l], b_ada[l][None, :])
        mod_ctx = mod[0:1].reshape(1, 1, 6 * d)
        mod_lat = mod[1:1 + bs_].reshape(bs_, 1, 6 * d)

        xp, (c_ctx_out, n_ctx_out, m_ctx_out), p = _trunk(
            xp, mod_ctx, bp * tp, tp, bp, CTX_SEQS_PER_STEP, p, late_f32, want_state=True)
        new_c.append(c_ctx_out.reshape(bp, N_DIR, HEADS, HD, HD))
        new_n.append(n_ctx_out.reshape(bp, N_DIR, HEADS, HD))
        new_m.append(m_ctx_out[..., 0].reshape(bp, N_DIR, HEADS))

        s0 = jnp.concatenate(
            [jnp.swapaxes(state_C[:, l], -1, -2), state_n[:, l][..., None, :],
             jnp.zeros((bs_, N_DIR, HEADS, ST_ROWS - HD - 1, HD), F32)],
            axis=-2).reshape(bs_, N_UNIT, ST_ROWS, HD)
        m0 = jnp.broadcast_to(state_m[:, l].reshape(bs_, N_UNIT, 1), (bs_, N_UNIT, SCAN))
        xs, _, _ = _trunk(xs, mod_lat, ts, GRID_W, bs_, LAT_CHUNKS_PER_STEP, p, s0=s0, m0=m0)

    return (xp.reshape(bp, tp, d), xs.reshape(bs_, ts, d),
            jnp.stack(new_c, axis=1), jnp.stack(new_n, axis=1), jnp.stack(new_m, axis=1))
```

```python
import functools

import jax
import jax.numpy as jnp
from jax import lax
from jax.experimental import pallas as pl
from jax.experimental.pallas import tpu as pltpu

D_MODEL = 1024
D_GMLP = 512
D_MLSTM = 512
GROUPS = 4
HEADS = 4
HD = 128
CHUNK = 128
SCAN = 256
N_DIR = 2
N_UNIT = N_DIR * HEADS
GRID_W = 64
OFF_V, OFF_Q, OFF_VV, OFF_O, OFF_G = 512, 1024, 2048, 2560, 3072
D_IN = OFF_G + 2 * N_UNIT
EPS = 1e-6
NEG = -1e30
ST_ROWS = HD + 16
GR_ROWS = 5 * N_UNIT

TOKEN_BLOCK = 256
CTX_SEQS_PER_STEP = 2
LAT_CHUNKS_PER_STEP = 2
VMEM_LIMIT_BYTES = 56 * 1024 * 1024

F32 = jnp.float32
BF16 = jnp.bfloat16


def _rms(x, g):
    return x * lax.rsqrt(jnp.mean(x * x, axis=-1, keepdims=True) + EPS) * g


def _sigmoid(x):
    return 1.0 / (1.0 + jnp.exp(-x))


def _gelu_tanh(x):
    return 0.5 * x * (1.0 + jnp.tanh(0.7978845608028654 * (x + 0.044715 * (x * x * x))))


def _log_sigmoid(x):
    return jnp.minimum(x, 0.0) - jnp.log(1.0 + jnp.exp(-jnp.abs(x)))


def _dot(a, b):
    return jnp.dot(a, b, preferred_element_type=F32)


def _dot_nt(a, b):
    return lax.dot_general(a, b, (((1,), (1,)), ((), ())), preferred_element_type=F32)


def _dot_exact(a, b):
    return jnp.dot(a, b, preferred_element_type=F32, precision=lax.Precision.HIGHEST)


def _pipeline_step(tick):
    i = pl.program_id(0)
    last = pl.num_programs(0) - 1

    @pl.when(i == 0)
    def _():
        tick(0, True, False, False)
        tick(1, True, True, False)

    @pl.when(jnp.logical_and(i > 0, i < last))
    def _():
        tick(0, True, True, True)
        tick(1, True, True, True)

    @pl.when(i == last)
    def _():
        tick(0, False, True, True)
        tick(1, False, False, True)


def _const_spec(shape):
    zeros = (0,) * len(shape)
    return pl.BlockSpec(shape, lambda *_: zeros, pipeline_mode=pl.Buffered(1))


def _ada_kernel(c_ref, w_ref, b_ref, o_ref):
    c = c_ref[...]
    s = (c * _sigmoid(c)).astype(BF16)
    o_ref[...] = _dot(s, w_ref[...].astype(BF16)) + b_ref[...]


def _ada(cs, w_ada, b_ada):
    rows, d = cs.shape
    n = w_ada.shape[1]
    tn = 1024
    return pl.pallas_call(
        _ada_kernel,
        out_shape=jax.ShapeDtypeStruct((rows, n), F32),
        grid=(n // tn,),
        in_specs=[
            pl.BlockSpec((rows, d), lambda j: (0, 0)),
            pl.BlockSpec((d, tn), lambda j: (0, j)),
            pl.BlockSpec((1, tn), lambda j: (0, j)),
        ],
        out_specs=pl.BlockSpec((rows, tn), lambda j: (0, j)),
        compiler_params=pltpu.CompilerParams(
            dimension_semantics=("arbitrary",), vmem_limit_bytes=VMEM_LIMIT_BYTES),
        name="ada",
    )(cs, w_ada, b_ada)


def _inproj_norm(x, mod, g1, hb_ref):
    sh1 = mod[:, 0:D_MODEL]
    sc1 = mod[:, D_MODEL:2 * D_MODEL]
    hb_ref[...] = (_rms(x, g1) * (1.0 + sc1) + sh1).astype(BF16)


def _inproj_project(hb_ref, w, z):
    hb = hb_ref[...]
    z["u"][...] = _dot_nt(hb, w["u"][...])
    z["v"][...] = _dot_nt(hb, w["v"][...])
    z["qk"][...] = _dot_nt(hb, w["qk"][...])
    z["o"][...] = _dot_nt(hb, w["o"][...])
    z["vt"][...] = _dot_nt(w["vv"][...], hb).astype(BF16)
    z["gt"][...] = _dot_nt(w["gt"][...].astype(BF16), hb)


def _inproj_finish(z, c, out, seg):
    tb = z["u"].shape[0]
    for g in range(GROUPS):
        gs = slice(g * HD, (g + 1) * HD)
        vg = _rms(_gelu_tanh(z["v"][:, gs]), c["gv"][:, gs]).astype(BF16)
        for ch in range(tb // CHUNK):
            cs = slice(ch * CHUNK, (ch + 1) * CHUNK)
            mixed = _dot(c["ws"][g], vg[cs]) + c["bs"][:, gs]
            out["a"][cs, gs] = (_gelu_tanh(z["u"][cs, gs]) * mixed).astype(BF16)

    zqk = z["qk"][...]
    pos = lax.broadcasted_iota(jnp.int32, (tb, 1), 0) % seg
    prev = jnp.where(pos != 0, pltpu.roll(zqk, 1, 0), 0.0)
    nxt = jnp.where(pos != seg - 1, pltpu.roll(zqk, tb - 1, 0), 0.0)
    cw = c["cw"]
    y = c["cb"][...] + prev * cw[0:1, :] + zqk * cw[1:2, :] + nxt * cw[2:3, :]
    y = y * _sigmoid(y)
    out["qt"][...] = jnp.transpose(y[:, :D_MLSTM]).astype(BF16)
    out["k"][...] = (y[:, D_MLSTM:] * (HD ** -0.5)).astype(BF16)
    out["vt"][...] = z["vt"][...]
    out["o"][...] = z["o"][...]

    assert tb == SCAN
    gt = z["gt"][...] + c["bgt"][...]
    gi_t = gt[0:N_UNIT]
    lf_t = _log_sigmoid(gt[N_UNIT:2 * N_UNIT])
    lane = lax.broadcasted_iota(jnp.int32, (N_UNIT, HD), 1)
    tiles = []
    carry = jnp.zeros((N_UNIT, HD), F32)
    for t in range(SCAN // HD):
        prefix = lf_t[:, t * HD:(t + 1) * HD]
        shift = 1
        while shift < HD:
            prefix = prefix + jnp.where(lane >= shift, pltpu.roll(prefix, shift, 1), 0.0)
            shift *= 2
        tiles.append(prefix + carry)
        carry = carry + jnp.broadcast_to(prefix[:, HD - 1:HD], (N_UNIT, HD))
    prefix = jnp.concatenate(tiles, axis=1)
    b_last = jnp.concatenate([carry] * (SCAN // HD), axis=1)
    row_bwd = lax.broadcasted_iota(jnp.int32, (N_UNIT, SCAN), 0) >= HEADS
    b_row = jnp.where(row_bwd, b_last - prefix + lf_t, prefix)
    g_row = b_last - b_row + gi_t
    g_max = jnp.broadcast_to(jnp.max(g_row, axis=1, keepdims=True), (N_UNIT, SCAN))
    out["gr"][0 * N_UNIT:1 * N_UNIT, :] = b_row
    out["gr"][1 * N_UNIT:2 * N_UNIT, :] = g_row
    out["gr"][2 * N_UNIT:3 * N_UNIT, :] = b_last
    out["gr"][3 * N_UNIT:4 * N_UNIT, :] = g_max
    out["gr"][4 * N_UNIT:5 * N_UNIT, :] = gi_t - b_row


_INPROJ_Z = ("u", "v", "qk", "o", "vt", "gt")


def _inproj_kernel(*refs, seg, blocks_per_mod, n_cast):
    (x_ref, mod_ref, g1_ref, wit_ref,
     wgt_ref, bgt_ref, cw_ref, cb_ref, ws_ref, bs_ref, gv_ref) = refs[:11]
    cast_in = refs[11:11 + n_cast]
    a_ref, qt_ref, k_ref, vt_ref, o_ref, gr_ref = refs[11 + n_cast:17 + n_cast]
    cast_out = refs[17 + n_cast:17 + 2 * n_cast]
    hb_s, *z_s = refs[17 + 2 * n_cast:]
    tb = x_ref.shape[0] // 2
    mod_in = jnp.minimum(pl.program_id(0), pl.num_programs(0) - 2) // blocks_per_mod
    w = {"u": wit_ref.at[0:OFF_V, :], "v": wit_ref.at[OFF_V:OFF_Q, :],
         "qk": wit_ref.at[OFF_Q:OFF_VV, :], "vv": wit_ref.at[OFF_VV:OFF_O, :],
         "o": wit_ref.at[OFF_O:OFF_G, :], "gt": wgt_ref}
    c = {"gv": gv_ref, "ws": ws_ref, "bs": bs_ref, "cw": cw_ref, "cb": cb_ref, "bgt": bgt_ref}

    def tick(half, norm, project, finish):
        rows = slice(half * tb, (half + 1) * tb)
        cur, oth = half, 1 - half
        z_cur = {name: ref.at[cur] for name, ref in zip(_INPROJ_Z, z_s)}
        z_oth = {name: ref.at[oth] for name, ref in zip(_INPROJ_Z, z_s)}
        out = {"a": a_ref.at[rows, :], "qt": qt_ref.at[:, rows], "k": k_ref.at[rows, :],
               "vt": vt_ref.at[:, rows], "o": o_ref.at[rows, :], "gr": gr_ref.at[:, rows]}
        if project:
            _inproj_project(hb_s.at[oth], w, z_oth)
        if norm:
            _inproj_norm(x_ref[rows, :], mod_ref[mod_in], g1_ref[...], hb_s.at[cur])
        if finish:
            _inproj_finish(z_cur, c, out, seg)
        if half == 0:
            for src, dst in zip(cast_in, cast_out):
                dst[...] = src[...].astype(BF16)

    _pipeline_step(tick)


def _inproj(x, mod, rows_per_mod, seg, p, cast=()):
    n = x.shape[0]
    tb = TOKEN_BLOCK
    tb2 = 2 * tb
    assert tb % seg == 0 and n % tb2 == 0 and rows_per_mod % tb2 == 0
    nb = n // tb2
    blk_in = lambda i: jnp.minimum(i, nb - 1)
    blk_out = lambda i: jnp.maximum(i - 1, 0)
    tok = lambda width: pl.BlockSpec((tb2, width), lambda i: (blk_out(i), 0))
    tok_t = lambda height: pl.BlockSpec((height, tb2), lambda i: (0, blk_out(i)))
    in_specs = [
        pl.BlockSpec((tb2, D_MODEL), lambda i: (blk_in(i), 0)),
        _const_spec(mod.shape),
        _const_spec((1, D_MODEL)),
        _const_spec((D_IN, D_MODEL)),
        _const_spec((2 * N_UNIT, D_MODEL)),
        _const_spec((2 * N_UNIT, 1)),
        _const_spec((3, 2 * D_MLSTM)),
        _const_spec((1, 2 * D_MLSTM)),
        _const_spec((GROUPS, CHUNK, CHUNK)),
        _const_spec((CHUNK, D_GMLP)),
        _const_spec((1, D_GMLP)),
    ]
    out_shape = [
        jax.ShapeDtypeStruct((n, D_GMLP), BF16),
        jax.ShapeDtypeStruct((D_MLSTM, n), BF16),
        jax.ShapeDtypeStruct((n, D_MLSTM), BF16),
        jax.ShapeDtypeStruct((D_MLSTM, n), BF16),
        jax.ShapeDtypeStruct((n, D_MLSTM), F32),
        jax.ShapeDtypeStruct((GR_ROWS, n), F32),
    ]
    out_specs = [tok(D_GMLP), tok_t(D_MLSTM), tok(D_MLSTM), tok_t(D_MLSTM), tok(D_MLSTM),
                 tok_t(GR_ROWS)]
    for wf in cast:
        rows = wf.shape[0] // nb
        assert rows * nb == wf.shape[0] and rows % 16 == 0
        slab = pl.BlockSpec((rows, wf.shape[1]), lambda i: (blk_in(i), 0))
        in_specs.append(slab)
        out_specs.append(slab)
        out_shape.append(jax.ShapeDtypeStruct(wf.shape, BF16))
    scratch_shapes = [
        pltpu.VMEM((2, tb, D_MODEL), BF16),
        pltpu.VMEM((2, tb, D_GMLP), F32),
        pltpu.VMEM((2, tb, D_GMLP), F32),
        pltpu.VMEM((2, tb, 2 * D_MLSTM), F32),
        pltpu.VMEM((2, tb, D_MLSTM), F32),
        pltpu.VMEM((2, D_MLSTM, tb), BF16),
        pltpu.VMEM((2, 2 * N_UNIT, tb), F32),
    ]
    return pl.pallas_call(
        functools.partial(_inproj_kernel, seg=seg, blocks_per_mod=rows_per_mod // tb2,
                          n_cast=len(cast)),
        out_shape=out_shape,
        grid=(nb + 1,),
        in_specs=in_specs,
        out_specs=out_specs,
        scratch_shapes=scratch_shapes,
        compiler_params=pltpu.CompilerParams(
            dimension_semantics=("arbitrary",), vmem_limit_bytes=VMEM_LIMIT_BYTES),
        name="inproj",
    )(x, mod, p["g1"], p["wit"], p["wgt"],
      p["bgt"], p["cw"], p["cb"], p["ws"], p["bs"], p["gv"], *cast)


def _lane_broadcast_column(row):
    n = row.shape[1]
    tiles = [jnp.transpose(jnp.broadcast_to(row[:, t * HD:(t + 1) * HD], (HD, HD)))
             for t in range(n // HD)]
    col = jnp.concatenate(tiles, axis=0)
    return jnp.concatenate([col] * (n // HD), axis=1)


def _mlstm_unit(k, vt, kq, qs, d_row, b_row, g_row, bl_row, gmax_row, st, m_row, mask, carried):
    logw = jnp.where(mask, _lane_broadcast_column(d_row) + b_row, NEG)
    a = b_row + m_row
    mj = jnp.maximum(a, jnp.max(logw, axis=0, keepdims=True))
    w = jnp.exp(logw - mj)
    s = kq * w
    num = _dot(vt, s.astype(BF16))
    den = jnp.sum(s, axis=0, keepdims=True)
    if carried:
        inter = jnp.exp(a - mj)
        num = num + inter * qs[:HD]
        den = den + inter * qs[HD:HD + 1]
    h = num * (1.0 / jnp.maximum(jnp.abs(den), jnp.exp(-mj)))
    m_new = jnp.maximum(bl_row + m_row, gmax_row)
    wc = jnp.exp(g_row - m_new)
    pad_row = lax.broadcasted_iota(jnp.int32, (ST_ROWS - HD, wc.shape[1]), 0)
    v_aug = jnp.concatenate(
        [vt.astype(F32) * wc, jnp.where(pad_row == 0, wc, 0.0)], axis=0).astype(BF16)
    st_new = _dot(v_aug, k)
    if carried:
        decay = jnp.exp(bl_row + m_row - m_new)
        st_new = decay[:, :HD] * st + st_new
    return h, st_new, m_new


def _mlstm_kernel(*refs, has_init, has_out, rounds, by_sequence):
    if has_init:
        s0_ref, m0_ref = refs[:2]
        refs = refs[2:]
    if by_sequence:
        k_ref, qt_ref, vt_ref, gr_ref = refs[:4]
        fwd_in = bwd_in = (k_ref, qt_ref, vt_ref, gr_ref)
        refs = refs[4:]
    else:
        fwd_in, bwd_in = refs[:4], refs[4:8]
        refs = refs[8:]
    hf_ref, hb_ref = refs[:2]
    refs = refs[2:]
    if has_out:
        co_ref, no_ref, mo_ref = refs[:3]
        refs = refs[3:]
    s_ref, m_ref = refs
    c = pl.program_id(1)
    nc = pl.num_programs(1)
    carried = not by_sequence

    if carried:
        @pl.when(c == 0)
        def _():
            if has_init:
                s_ref[...] = s0_ref[0]
                m_ref[...] = m0_ref[0]
            else:
                s_ref[...] = jnp.zeros_like(s_ref)
                m_ref[...] = jnp.zeros_like(m_ref)

    si = lax.broadcasted_iota(jnp.int32, (SCAN, SCAN), 0)
    ji = lax.broadcasted_iota(jnp.int32, (SCAN, SCAN), 1)
    dirs = ((fwd_in, hf_ref, si <= ji), (bwd_in, hb_ref, si >= ji))

    def chunk_cols(d, r):
        pos = rounds - 1 - r if (d == 1 and not by_sequence) else r
        return slice(pos * SCAN, (pos + 1) * SCAN)

    kq = {}
    for r in range(rounds):
        for d, ((k_ref, qt_ref, _, _), _, _) in enumerate(dirs):
            cs = chunk_cols(d, r)
            for hd in range(HEADS):
                hs = slice(hd * HD, (hd + 1) * HD)
                kq[r, d, hd] = _dot(k_ref[cs, hs], qt_ref[hs, cs])

    m_cur = None
    if carried:
        m_all = m_ref[...]
        m_cur = [m_all[u:u + 1] for u in range(N_UNIT)]
    zero_row = jnp.zeros((1, SCAN), F32)
    for r in range(rounds):
        slot = r * N_UNIT if by_sequence else 0
        for d, ((k_ref, qt_ref, vt_ref, gr_ref), h_ref, mask) in enumerate(dirs):
            cs = chunk_cols(d, r)
            for hd in range(HEADS):
                u = d * HEADS + hd
                hs = slice(hd * HD, (hd + 1) * HD)
                st = s_ref[slot + u] if carried else None
                qs = _dot(st.astype(BF16), qt_ref[hs, cs]) if carried else None
                h, st_new, m_new = _mlstm_unit(
                    k_ref[cs, hs], vt_ref[hs, cs], kq[r, d, hd], qs,
                    gr_ref[4 * N_UNIT + u:4 * N_UNIT + u + 1, cs],
                    gr_ref[u:u + 1, cs],
                    gr_ref[N_UNIT + u:N_UNIT + u + 1, cs],
                    gr_ref[2 * N_UNIT + u:2 * N_UNIT + u + 1, cs],
                    gr_ref[3 * N_UNIT + u:3 * N_UNIT + u + 1, cs],
                    st, m_cur[u] if carried else zero_row, mask, carried)
                h_ref[hs, cs] = h
                s_ref[slot + u] = st_new
                if carried:
                    m_cur[u] = m_new
                else:
                    m_ref[slot + u:slot + u + 1, :] = m_new
    if carried:
        m_ref[...] = jnp.concatenate(m_cur, axis=0)

    if has_out:
        @pl.when(c == nc - 1)
        def _():
            for j in range(s_ref.shape[0]):
                q, u = divmod(j, N_UNIT)
                co_ref[q, u] = jnp.transpose(s_ref[j, 0:HD, :])
                no_ref[q, u:u + 1, :] = s_ref[j, HD:HD + 1, :]
                mo_ref[q, u:u + 1, :] = m_ref[j:j + 1, :]


def _mlstm(k, qt, vt, gr, batch, rounds, s0=None, m0=None, want_state=False):
    n = k.shape[0]
    nc = n // batch // SCAN
    by_sequence = nc == 1
    width = rounds * SCAN
    has_init = s0 is not None
    if by_sequence:
        assert batch % rounds == 0 and not has_init
        grid = (batch // rounds, 1)
        slots = rounds
        fwd = bwd = lambda b, c: b
    else:
        assert nc % rounds == 0 and not want_state
        steps = nc // rounds
        grid = (batch, steps)
        slots = 1
        fwd = lambda b, c: b * steps + c
        bwd = lambda b, c: b * steps + steps - 1 - c

    def specs(ix):
        return [
            pl.BlockSpec((width, D_MLSTM), lambda b, c: (ix(b, c), 0)),
            pl.BlockSpec((D_MLSTM, width), lambda b, c: (0, ix(b, c))),
            pl.BlockSpec((D_MLSTM, width), lambda b, c: (0, ix(b, c))),
            pl.BlockSpec((GR_ROWS, width), lambda b, c: (0, ix(b, c))),
        ]

    in_specs = specs(fwd) if by_sequence else specs(fwd) + specs(bwd)
    args = [k, qt, vt, gr] if by_sequence else [k, qt, vt, gr, k, qt, vt, gr]
    if has_init:
        in_specs = [
            pl.BlockSpec((1, N_UNIT, ST_ROWS, HD), lambda b, c: (b, 0, 0, 0)),
            pl.BlockSpec((1, N_UNIT, SCAN), lambda b, c: (b, 0, 0)),
        ] + in_specs
        args = [s0, m0] + args
    out_shape = [
        jax.ShapeDtypeStruct((D_MLSTM, n), F32),
        jax.ShapeDtypeStruct((D_MLSTM, n), F32),
    ]
    out_specs = [
        pl.BlockSpec((D_MLSTM, width), lambda b, c: (0, fwd(b, c))),
        pl.BlockSpec((D_MLSTM, width), lambda b, c: (0, bwd(b, c))),
    ]
    if want_state:
        out_shape += [
            jax.ShapeDtypeStruct((batch, N_UNIT, HD, HD), F32),
            jax.ShapeDtypeStruct((batch, N_UNIT, HD), F32),
            jax.ShapeDtypeStruct((batch, N_UNIT, SCAN), F32),
        ]
        out_specs += [
            pl.BlockSpec((slots, N_UNIT, HD, HD), lambda b, c: (b, 0, 0, 0)),
            pl.BlockSpec((slots, N_UNIT, HD), lambda b, c: (b, 0, 0)),
            pl.BlockSpec((slots, N_UNIT, SCAN), lambda b, c: (b, 0, 0)),
        ]
    return pl.pallas_call(
        functools.partial(_mlstm_kernel, has_init=has_init, has_out=want_state, rounds=rounds,
                          by_sequence=by_sequence),
        out_shape=out_shape,
        grid=grid,
        in_specs=in_specs,
        out_specs=out_specs,
        scratch_shapes=[pltpu.VMEM((slots * N_UNIT, ST_ROWS, HD), F32),
                        pltpu.VMEM((slots * N_UNIT, SCAN), F32)],
        compiler_params=pltpu.CompilerParams(
            dimension_semantics=("arbitrary", "arbitrary"), vmem_limit_bytes=VMEM_LIMIT_BYTES),
        name="mlstm",
    )(*args)


def _outffn_mix(x, a, hft, hbt, o, mod, c, x1_ref, h2_ref):
    ga1 = mod[:, 2 * D_MODEL:3 * D_MODEL]
    sh2 = mod[:, 3 * D_MODEL:4 * D_MODEL]
    sc2 = mod[:, 4 * D_MODEL:5 * D_MODEL]
    hs = jnp.transpose(hft + hbt)
    sig = _sigmoid(o)
    parts = []
    for hd in range(HEADS):
        sl = slice(hd * HD, (hd + 1) * HD)
        parts.append(_rms(hs[:, sl], c["gh"][:, sl]) * sig[:, sl])
    b_out = jnp.concatenate(parts, axis=-1).astype(BF16)
    mix = _dot(a, c["wout"][0:D_GMLP, :]) + _dot(b_out, c["wout"][D_GMLP:, :])
    x1 = x + ga1 * mix
    x1_ref[...] = x1
    h2_ref[...] = (_rms(x1, c["g2"][...]) * (1.0 + sc2) + sh2).astype(BF16)


def _outffn_up(h2_ref, x1_in_ref, c, f_ref, x1_out_ref):
    h2 = h2_ref[...]
    u = _dot(h2, c["w1"][...])
    g = _dot(h2, c["w3"][...])
    f_ref[...] = (u * _sigmoid(u) * g).astype(BF16)
    x1_out_ref[...] = x1_in_ref[...]


def _outffn_down(f_ref, x1_ref, mod, c, y_ref):
    ga2 = mod[:, 5 * D_MODEL:6 * D_MODEL]
    x2 = x1_ref[...] + ga2 * _dot(f_ref[...], c["w2"][...])
    y_ref[...] = _rms(x2, c["gf"][...])


def _outffn_kernel(x_ref, a_ref, hft_ref, hbt_ref, o_ref, mod_ref, gh_ref, wout_ref,
                   g2_ref, w1_ref, w3_ref, w2_ref, gf_ref, y_ref,
                   x1a_s, h2_s, f_s, x1b_s, *, blocks_per_mod):
    tb = x_ref.shape[0] // 2
    i = pl.program_id(0)
    mod_in = jnp.minimum(i, pl.num_programs(0) - 2) // blocks_per_mod
    mod_out = jnp.maximum(i - 1, 0) // blocks_per_mod
    c = {"gh": gh_ref, "wout": wout_ref, "g2": g2_ref, "w1": w1_ref, "w3": w3_ref, "w2": w2_ref,
         "gf": gf_ref}

    def tick(half, mix, up, down):
        rows = slice(half * tb, (half + 1) * tb)
        cur, oth = half, 1 - half
        if down:
            _outffn_down(f_s.at[cur], x1b_s.at[cur], mod_ref[mod_out], c, y_ref.at[rows, :])
        if mix:
            _outffn_mix(x_ref[rows, :], a_ref[rows, :], hft_ref[:, rows], hbt_ref[:, rows],
                        o_ref[rows, :], mod_ref[mod_in], c, x1a_s.at[cur], h2_s.at[cur])
        if up:
            _outffn_up(h2_s.at[oth], x1a_s.at[oth], c, f_s.at[oth], x1b_s.at[oth])

    _pipeline_step(tick)


def _outffn(x, a, hft, hbt, o, mod, rows_per_mod, p):
    n = x.shape[0]
    tb = TOKEN_BLOCK
    tb2 = 2 * tb
    assert n % tb2 == 0 and rows_per_mod % tb2 == 0
    nb = n // tb2
    d_ff = p["w1"].shape[1]
    blk_in = lambda i: jnp.minimum(i, nb - 1)
    blk_out = lambda i: jnp.maximum(i - 1, 0)
    tok = lambda w: pl.BlockSpec((tb2, w), lambda i: (blk_in(i), 0))
    tok_t = lambda h: pl.BlockSpec((h, tb2), lambda i: (0, blk_in(i)))
    in_specs = [
        tok(D_MODEL), tok(D_GMLP), tok_t(D_MLSTM), tok_t(D_MLSTM), tok(D_MLSTM),
        _const_spec(mod.shape),
        _const_spec((1, D_MLSTM)),
        _const_spec((D_GMLP + D_MLSTM, D_MODEL)),
        _const_spec((1, D_MODEL)),
        _const_spec((D_MODEL, d_ff)),
        _const_spec((D_MODEL, d_ff)),
        _const_spec((d_ff, D_MODEL)),
        _const_spec((1, D_MODEL)),
    ]
    scratch_shapes = [
        pltpu.VMEM((2, tb, D_MODEL), F32),
        pltpu.VMEM((2, tb, D_MODEL), BF16),
        pltpu.VMEM((2, tb, d_ff), BF16),
        pltpu.VMEM((2, tb, D_MODEL), F32),
    ]
    return pl.pallas_call(
        functools.partial(_outffn_kernel, blocks_per_mod=rows_per_mod // tb2),
        out_shape=jax.ShapeDtypeStruct((n, D_MODEL), F32),
        grid=(nb + 1,),
        in_specs=in_specs,
        out_specs=pl.BlockSpec((tb2, D_MODEL), lambda i: (blk_out(i), 0)),
        scratch_shapes=scratch_shapes,
        compiler_params=pltpu.CompilerParams(
            dimension_semantics=("arbitrary",), vmem_limit_bytes=VMEM_LIMIT_BYTES),
        name="outffn",
    )(x, a, hft, hbt, o, mod, p["gh"], p["wout"], p["g2"], p["w1"], p["w3"], p["w2"], p["gf"])


def _regroup_gates(g):
    h = HEADS
    return jnp.concatenate([g[:, 0:h], g[:, 2 * h:3 * h], g[:, h:2 * h], g[:, 3 * h:4 * h]], axis=1)


def _layer_params(l, g_norm1, w_in, b_gate, w_s, b_s, g_v, conv_w, conv_b, g_h, g_norm2, g_final):
    wg = _regroup_gates(w_in[l][:, OFF_G:])
    bg = _regroup_gates(b_gate[l][None, :])
    return {
        "g1": g_norm1[l][None, :],
        "wit": w_in[l].T.astype(BF16),
        "wgt": wg.T, "bgt": bg.T,
        "cw": conv_w[l], "cb": conv_b[l][None, :],
        "ws": w_s[l].astype(BF16),
        "bs": jnp.repeat(b_s[l].T, HD, axis=1),
        "gv": g_v[l].reshape(1, D_GMLP),
        "gh": g_h[l].reshape(1, D_MLSTM),
        "g2": g_norm2[l][None, :],
        "gf": g_final[None, :],
    }


_LATE_WEIGHTS = ("wout", "w1", "w3", "w2")


def _trunk(x, mod, rows_per_mod, seg, batch, rounds, p, late_f32=None, s0=None, m0=None,
           want_state=False):
    cast = () if late_f32 is None else tuple(late_f32[name] for name in _LATE_WEIGHTS)
    a, qt, k, vt, o, gr, *late = _inproj(x, mod, rows_per_mod, seg, p, cast)
    if late:
        p = dict(p, **dict(zip(_LATE_WEIGHTS, late)))
    hft, hbt, *state = _mlstm(k, qt, vt, gr, batch, rounds, s0, m0, want_state)
    y = _outffn(x, a, hft, hbt, o, mod, rows_per_mod, p)
    return y, state, p


def kernel(x_prompt, x_sample, state_C, state_n, state_m, c, c_ctx, w_ada, b_ada, g_norm1, w_in,
           b_gate, w_s, b_s, g_v, conv_w, conv_b, g_h, w_out, g_norm2, w1, w3, w2, g_final):
    bp, tp, d = x_prompt.shape
    bs_, ts, _ = x_sample.shape
    depth = w_in.shape[0]
    assert depth == 1, "final norm is fused into the layer's last kernel"
    xp = x_prompt.reshape(bp * tp, d)
    xs = x_sample.reshape(bs_ * ts, d)

    cs = jnp.zeros((8, d), F32).at[0].set(c_ctx).at[1:1 + bs_].set(c)
    new_c, new_n, new_m = [], [], []
    for l in range(depth):
        p = _layer_params(l, g_norm1, w_in, b_gate, w_s, b_s, g_v, conv_w, conv_b, g_h, g_norm2,
                          g_final)
        late_f32 = {"wout": w_out[l], "w1": w1[l], "w3": w3[l], "w2": w2[l]}
        mod = _ada(cs, w_ada[l], b_ada[l][None, :])
        mod_ctx = mod[0:1].reshape(1, 1, 6 * d)
        mod_lat = mod[1:1 + bs_].reshape(bs_, 1, 6 * d)

        xp, (c_ctx_out, n_ctx_out, m_ctx_out), p = _trunk(
            xp, mod_ctx, bp * tp, tp, bp, CTX_SEQS_PER_STEP, p, late_f32, want_state=True)
        new_c.append(c_ctx_out.reshape(bp, N_DIR, HEADS, HD, HD))
        new_n.append(n_ctx_out.reshape(bp, N_DIR, HEADS, HD))
        new_m.append(m_ctx_out[..., 0].reshape(bp, N_DIR, HEADS))

        s0 = jnp.concatenate(
            [jnp.swapaxes(state_C[:, l], -1, -2), state_n[:, l][..., None, :],
             jnp.zeros((bs_, N_DIR, HEADS, ST_ROWS - HD - 1, HD), F32)],
            axis=-2).reshape(bs_, N_UNIT, ST_ROWS, HD)
        m0 = jnp.broadcast_to(state_m[:, l].reshape(bs_, N_UNIT, 1), (bs_, N_UNIT, SCAN))
        xs, _, _ = _trunk(xs, mod_lat, ts, GRID_W, bs_, LAT_CHUNKS_PER_STEP, p, s0=s0, m0=m0)

    return (xp.reshape(bp, tp, d), xs.reshape(bs_, ts, d),
            jnp.stack(new_c, axis=1), jnp.stack(new_n, axis=1), jnp.stack(new_m, axis=1))
```

```python
import functools

import jax
import jax.numpy as jnp
from jax import lax
from jax.experimental import pallas as pl
from jax.experimental.pallas import tpu as pltpu

D_MODEL = 1024
D_GMLP = 512
D_MLSTM = 512
GROUPS = 4
HEADS = 4
HD = 128
CHUNK = 128
SCAN = 256
N_DIR = 2
N_UNIT = N_DIR * HEADS
GRID_W = 64
OFF_V, OFF_Q, OFF_VV, OFF_O, OFF_G = 512, 1024, 2048, 2560, 3072
D_IN = OFF_G + 2 * N_UNIT
EPS = 1e-6
NEG = -1e30
ST_ROWS = HD + 16
GR_ROWS = 5 * N_UNIT

TOKEN_BLOCK = 256
CTX_SEQS_PER_STEP = 2
LAT_CHUNKS_PER_STEP = 2
VMEM_LIMIT_BYTES = 56 * 1024 * 1024

F32 = jnp.float32
BF16 = jnp.bfloat16


def _rms(x, g):
    return x * lax.rsqrt(jnp.mean(x * x, axis=-1, keepdims=True) + EPS) * g


def _sigmoid(x):
    return 1.0 / (1.0 + jnp.exp(-x))


def _gelu_tanh(x):
    return 0.5 * x * (1.0 + jnp.tanh(0.7978845608028654 * (x + 0.044715 * (x * x * x))))


def _log_sigmoid(x):
    return jnp.minimum(x, 0.0) - jnp.log(1.0 + jnp.exp(-jnp.abs(x)))


def _dot(a, b):
    return jnp.dot(a, b, preferred_element_type=F32)


def _dot_nt(a, b):
    return lax.dot_general(a, b, (((1,), (1,)), ((), ())), preferred_element_type=F32)


def _dot_exact(a, b):
    return jnp.dot(a, b, preferred_element_type=F32, precision=lax.Precision.HIGHEST)


def _pipeline_step(tick):
    i = pl.program_id(0)
    last = pl.num_programs(0) - 1

    @pl.when(i == 0)
    def _():
        tick(0, True, False, False)
        tick(1, True, True, False)

    @pl.when(jnp.logical_and(i > 0, i < last))
    def _():
        tick(0, True, True, True)
        tick(1, True, True, True)

    @pl.when(i == last)
    def _():
        tick(0, False, True, True)
        tick(1, False, False, True)


def _const_spec(shape):
    zeros = (0,) * len(shape)
    return pl.BlockSpec(shape, lambda *_: zeros, pipeline_mode=pl.Buffered(1))


def _ada_kernel(c_ref, w_ref, b_ref, o_ref):
    c = c_ref[...]
    s = (c * _sigmoid(c)).astype(BF16)
    o_ref[...] = _dot(s, w_ref[...].astype(BF16)) + b_ref[...]


def _ada(cs, w_ada, b_ada, n):
    rows, d = cs.shape
    tn = 1024
    return pl.pallas_call(
        _ada_kernel,
        out_shape=jax.ShapeDtypeStruct((rows, n), F32),
        grid=(n // tn,),
        in_specs=[
            pl.BlockSpec((rows, d), lambda j: (0, 0)),
            pl.BlockSpec((d, tn), lambda j: (0, j)),
            pl.BlockSpec((1, tn), lambda j: (0, j)),
        ],
        out_specs=pl.BlockSpec((rows, tn), lambda j: (0, j)),
        compiler_params=pltpu.CompilerParams(
            dimension_semantics=("arbitrary",), vmem_limit_bytes=VMEM_LIMIT_BYTES),
        name="ada",
    )(cs, w_ada, b_ada)


def _inproj_norm(x, mod, g1, hb_ref):
    sh1 = mod[:, 0:D_MODEL]
    sc1 = mod[:, D_MODEL:2 * D_MODEL]
    hb_ref[...] = (_rms(x, g1) * (1.0 + sc1) + sh1).astype(BF16)


def _inproj_project(hb_ref, w, z):
    hb = hb_ref[...]
    z["u"][...] = _dot_nt(hb, w["u"][...])
    z["v"][...] = _dot_nt(hb, w["v"][...])
    z["qk"][...] = _dot_nt(hb, w["qk"][...])
    z["o"][...] = _dot_nt(hb, w["o"][...])
    z["vt"][...] = _dot_nt(w["vv"][...], hb).astype(BF16)
    z["gt"][...] = _dot_nt(w["gt"][...].astype(BF16), hb)


def _inproj_finish(z, c, out, seg):
    tb = z["u"].shape[0]
    for g in range(GROUPS):
        gs = slice(g * HD, (g + 1) * HD)
        vg = _rms(_gelu_tanh(z["v"][:, gs]), c["gv"][:, gs]).astype(BF16)
        for ch in range(tb // CHUNK):
            cs = slice(ch * CHUNK, (ch + 1) * CHUNK)
            mixed = _dot(c["ws"][g], vg[cs]) + c["bs"][:, gs]
            out["a"][cs, gs] = (_gelu_tanh(z["u"][cs, gs]) * mixed).astype(BF16)

    zqk = z["qk"][...]
    pos = lax.broadcasted_iota(jnp.int32, (tb, 1), 0) % seg
    prev = jnp.where(pos != 0, pltpu.roll(zqk, 1, 0), 0.0)
    nxt = jnp.where(pos != seg - 1, pltpu.roll(zqk, tb - 1, 0), 0.0)
    cw = c["cw"]
    y = c["cb"][...] + prev * cw[0:1, :] + zqk * cw[1:2, :] + nxt * cw[2:3, :]
    y = y * _sigmoid(y)
    out["qt"][...] = jnp.transpose(y[:, :D_MLSTM]).astype(BF16)
    out["k"][...] = (y[:, D_MLSTM:] * (HD ** -0.5)).astype(BF16)
    out["vt"][...] = z["vt"][...]
    out["o"][...] = z["o"][...]

    assert tb == SCAN
    gt = z["gt"][...] + c["bgt"][...]
    gi_t = gt[0:N_UNIT]
    lf_t = _log_sigmoid(gt[N_UNIT:2 * N_UNIT])
    lane = lax.broadcasted_iota(jnp.int32, (N_UNIT, HD), 1)
    tiles = []
    carry = jnp.zeros((N_UNIT, HD), F32)
    for t in range(SCAN // HD):
        prefix = lf_t[:, t * HD:(t + 1) * HD]
        shift = 1
        while shift < HD:
            prefix = prefix + jnp.where(lane >= shift, pltpu.roll(prefix, shift, 1), 0.0)
            shift *= 2
        tiles.append(prefix + carry)
        carry = carry + jnp.broadcast_to(prefix[:, HD - 1:HD], (N_UNIT, HD))
    prefix = jnp.concatenate(tiles, axis=1)
    b_last = jnp.concatenate([carry] * (SCAN // HD), axis=1)
    row_bwd = lax.broadcasted_iota(jnp.int32, (N_UNIT, SCAN), 0) >= HEADS
    b_row = jnp.where(row_bwd, b_last - prefix + lf_t, prefix)
    g_row = b_last - b_row + gi_t
    g_max = jnp.broadcast_to(jnp.max(g_row, axis=1, keepdims=True), (N_UNIT, SCAN))
    out["gr"][0 * N_UNIT:1 * N_UNIT, :] = b_row
    out["gr"][1 * N_UNIT:2 * N_UNIT, :] = g_row
    out["gr"][2 * N_UNIT:3 * N_UNIT, :] = b_last
    out["gr"][3 * N_UNIT:4 * N_UNIT, :] = g_max
    out["gr"][4 * N_UNIT:5 * N_UNIT, :] = gi_t - b_row


_INPROJ_Z = ("u", "v", "qk", "o", "vt", "gt")


def _inproj_kernel(*refs, seg, blocks_per_mod, mod_row0, n_cast, ada_tail):
    refs = list(refs)
    take = lambda k: [refs.pop(0) for _ in range(k)]
    (x_ref, mod_ref, g1_ref, wit_ref,
     wgt_ref, bgt_ref, cw_ref, cb_ref, ws_ref, bs_ref, gv_ref) = take(11)
    cast_in = take(n_cast)
    ada_in = take(3) if ada_tail else None
    a_ref, qt_ref, k_ref, vt_ref, o_ref, gr_ref = take(6)
    cast_out = take(n_cast)
    ada_out = take(1)[0] if ada_tail else None
    hb_s, *z_s = refs
    tb = x_ref.shape[0] // 2
    mod_in = mod_row0 + jnp.minimum(pl.program_id(0), pl.num_programs(0) - 2) // blocks_per_mod
    w = {"u": wit_ref.at[0:OFF_V, :], "v": wit_ref.at[OFF_V:OFF_Q, :],
         "qk": wit_ref.at[OFF_Q:OFF_VV, :], "vv": wit_ref.at[OFF_VV:OFF_O, :],
         "o": wit_ref.at[OFF_O:OFF_G, :], "gt": wgt_ref}
    c = {"gv": gv_ref, "ws": ws_ref, "bs": bs_ref, "cw": cw_ref, "cb": cb_ref, "bgt": bgt_ref}

    def tick(half, norm, project, finish):
        rows = slice(half * tb, (half + 1) * tb)
        cur, oth = half, 1 - half
        z_cur = {name: ref.at[cur] for name, ref in zip(_INPROJ_Z, z_s)}
        z_oth = {name: ref.at[oth] for name, ref in zip(_INPROJ_Z, z_s)}
        out = {"a": a_ref.at[rows, :], "qt": qt_ref.at[:, rows], "k": k_ref.at[rows, :],
               "vt": vt_ref.at[:, rows], "o": o_ref.at[rows, :], "gr": gr_ref.at[:, rows]}
        if project:
            _inproj_project(hb_s.at[oth], w, z_oth)
        if norm:
            _inproj_norm(x_ref[rows, :], mod_ref[mod_in], g1_ref[...], hb_s.at[cur])
        if finish:
            _inproj_finish(z_cur, c, out, seg)
        if half == 0:
            for src, dst in zip(cast_in, cast_out):
                dst[...] = src[...].astype(BF16)
            if ada_tail:
                _ada_kernel(*ada_in, ada_out)

    _pipeline_step(tick)


def _inproj(x, mod, mod_row0, rows_per_mod, seg, p, cast=(), ada_tail=None):
    n = x.shape[0]
    tb = TOKEN_BLOCK
    tb2 = 2 * tb
    assert tb % seg == 0 and n % tb2 == 0 and rows_per_mod % tb2 == 0
    nb = n // tb2
    blk_in = lambda i: jnp.minimum(i, nb - 1)
    blk_out = lambda i: jnp.maximum(i - 1, 0)
    tok = lambda width: pl.BlockSpec((tb2, width), lambda i: (blk_out(i), 0))
    tok_t = lambda height: pl.BlockSpec((height, tb2), lambda i: (0, blk_out(i)))
    in_specs = [
        pl.BlockSpec((tb2, D_MODEL), lambda i: (blk_in(i), 0)),
        _const_spec(mod.shape),
        _const_spec((1, D_MODEL)),
        _const_spec((D_IN, D_MODEL)),
        _const_spec((2 * N_UNIT, D_MODEL)),
        _const_spec((2 * N_UNIT, 1)),
        _const_spec((3, 2 * D_MLSTM)),
        _const_spec((1, 2 * D_MLSTM)),
        _const_spec((GROUPS, CHUNK, CHUNK)),
        _const_spec((CHUNK, D_GMLP)),
        _const_spec((1, D_GMLP)),
    ]
    out_shape = [
        jax.ShapeDtypeStruct((n, D_GMLP), BF16),
        jax.ShapeDtypeStruct((D_MLSTM, n), BF16),
        jax.ShapeDtypeStruct((n, D_MLSTM), BF16),
        jax.ShapeDtypeStruct((D_MLSTM, n), BF16),
        jax.ShapeDtypeStruct((n, D_MLSTM), F32),
        jax.ShapeDtypeStruct((GR_ROWS, n), F32),
    ]
    out_specs = [tok(D_GMLP), tok_t(D_MLSTM), tok(D_MLSTM), tok_t(D_MLSTM), tok(D_MLSTM),
                 tok_t(GR_ROWS)]
    for wf in cast:
        rows = wf.shape[0] // nb
        assert rows * nb == wf.shape[0] and rows % 16 == 0
        slab = pl.BlockSpec((rows, wf.shape[1]), lambda i: (blk_in(i), 0))
        in_specs.append(slab)
        out_specs.append(slab)
        out_shape.append(jax.ShapeDtypeStruct(wf.shape, BF16))
    side = list(cast)
    if ada_tail is not None:
        cs, w_ada, b_ada, col0 = ada_tail
        width = (w_ada.shape[1] - col0) // nb
        assert col0 % width == 0 and width * nb == w_ada.shape[1] - col0 and width % 128 == 0
        in_specs += [
            _const_spec(cs.shape),
            pl.BlockSpec((w_ada.shape[0], width), lambda i: (0, col0 // width + blk_in(i))),
            pl.BlockSpec((1, width), lambda i: (0, col0 // width + blk_in(i))),
        ]
        out_specs.append(pl.BlockSpec((cs.shape[0], width), lambda i: (0, blk_in(i))))
        out_shape.append(jax.ShapeDtypeStruct((cs.shape[0], w_ada.shape[1] - col0), F32))
        side += [cs, w_ada, b_ada]
    scratch_shapes = [
        pltpu.VMEM((2, tb, D_MODEL), BF16),
        pltpu.VMEM((2, tb, D_GMLP), F32),
        pltpu.VMEM((2, tb, D_GMLP), F32),
        pltpu.VMEM((2, tb, 2 * D_MLSTM), F32),
        pltpu.VMEM((2, tb, D_MLSTM), F32),
        pltpu.VMEM((2, D_MLSTM, tb), BF16),
        pltpu.VMEM((2, 2 * N_UNIT, tb), F32),
    ]
    return pl.pallas_call(
        functools.partial(_inproj_kernel, seg=seg, blocks_per_mod=rows_per_mod // tb2,
                          mod_row0=mod_row0, n_cast=len(cast), ada_tail=ada_tail is not None),
        out_shape=out_shape,
        grid=(nb + 1,),
        in_specs=in_specs,
        out_specs=out_specs,
        scratch_shapes=scratch_shapes,
        compiler_params=pltpu.CompilerParams(
            dimension_semantics=("arbitrary",), vmem_limit_bytes=VMEM_LIMIT_BYTES),
        name="inproj",
    )(x, mod, p["g1"], p["wit"], p["wgt"],
      p["bgt"], p["cw"], p["cb"], p["ws"], p["bs"], p["gv"], *side)


def _lane_broadcast_column(row):
    n = row.shape[1]
    tiles = [jnp.transpose(jnp.broadcast_to(row[:, t * HD:(t + 1) * HD], (HD, HD)))
             for t in range(n // HD)]
    col = jnp.concatenate(tiles, axis=0)
    return jnp.concatenate([col] * (n // HD), axis=1)


def _mlstm_unit(k, vt, kq, qs, d_row, b_row, g_row, bl_row, gmax_row, st, m_row, mask, carried):
    logw = jnp.where(mask, _lane_broadcast_column(d_row) + b_row, NEG)
    a = b_row + m_row
    mj = jnp.maximum(a, jnp.max(logw, axis=0, keepdims=True))
    w = jnp.exp(logw - mj)
    s = kq * w
    num = _dot(vt, s.astype(BF16))
    den = jnp.sum(s, axis=0, keepdims=True)
    if carried:
        inter = jnp.exp(a - mj)
        num = num + inter * qs[:HD]
        den = den + inter * qs[HD:HD + 1]
    h = num * (1.0 / jnp.maximum(jnp.abs(den), jnp.exp(-mj)))
    m_new = jnp.maximum(bl_row + m_row, gmax_row)
    wc = jnp.exp(g_row - m_new)
    pad_row = lax.broadcasted_iota(jnp.int32, (ST_ROWS - HD, wc.shape[1]), 0)
    v_aug = jnp.concatenate(
        [vt.astype(F32) * wc, jnp.where(pad_row == 0, wc, 0.0)], axis=0).astype(BF16)
    st_new = _dot(v_aug, k)
    if carried:
        decay = jnp.exp(bl_row + m_row - m_new)
        st_new = decay[:, :HD] * st + st_new
    return h, st_new, m_new


def _mlstm_kernel(*refs, has_init, has_out, rounds, by_sequence):
    if has_init:
        s0_ref, m0_ref = refs[:2]
        refs = refs[2:]
    if by_sequence:
        k_ref, qt_ref, vt_ref, gr_ref = refs[:4]
        fwd_in = bwd_in = (k_ref, qt_ref, vt_ref, gr_ref)
        refs = refs[4:]
    else:
        fwd_in, bwd_in = refs[:4], refs[4:8]
        refs = refs[8:]
    hf_ref, hb_ref = refs[:2]
    refs = refs[2:]
    if has_out:
        co_ref, no_ref, mo_ref = refs[:3]
        refs = refs[3:]
    s_ref, m_ref = refs
    c = pl.program_id(1)
    nc = pl.num_programs(1)
    carried = not by_sequence

    if carried:
        @pl.when(c == 0)
        def _():
            if has_init:
                s_ref[...] = s0_ref[0]
                m_ref[...] = m0_ref[0]
            else:
                s_ref[...] = jnp.zeros_like(s_ref)
                m_ref[...] = jnp.zeros_like(m_ref)

    si = lax.broadcasted_iota(jnp.int32, (SCAN, SCAN), 0)
    ji = lax.broadcasted_iota(jnp.int32, (SCAN, SCAN), 1)
    dirs = ((fwd_in, hf_ref, si <= ji), (bwd_in, hb_ref, si >= ji))

    def chunk_cols(d, r):
        pos = rounds - 1 - r if (d == 1 and not by_sequence) else r
        return slice(pos * SCAN, (pos + 1) * SCAN)

    kq = {}
    for r in range(rounds):
        for d, ((k_ref, qt_ref, _, _), _, _) in enumerate(dirs):
            cs = chunk_cols(d, r)
            for hd in range(HEADS):
                hs = slice(hd * HD, (hd + 1) * HD)
                kq[r, d, hd] = _dot(k_ref[cs, hs], qt_ref[hs, cs])

    m_cur = None
    if carried:
        m_all = m_ref[...]
        m_cur = [m_all[u:u + 1] for u in range(N_UNIT)]
    zero_row = jnp.zeros((1, SCAN), F32)
    for r in range(rounds):
        slot = r * N_UNIT if by_sequence else 0
        for d, ((k_ref, qt_ref, vt_ref, gr_ref), h_ref, mask) in enumerate(dirs):
            cs = chunk_cols(d, r)
            for hd in range(HEADS):
                u = d * HEADS + hd
                hs = slice(hd * HD, (hd + 1) * HD)
                st = s_ref[slot + u] if carried else None
                qs = _dot(st.astype(BF16), qt_ref[hs, cs]) if carried else None
                h, st_new, m_new = _mlstm_unit(
                    k_ref[cs, hs], vt_ref[hs, cs], kq[r, d, hd], qs,
                    gr_ref[4 * N_UNIT + u:4 * N_UNIT + u + 1, cs],
                    gr_ref[u:u + 1, cs],
                    gr_ref[N_UNIT + u:N_UNIT + u + 1, cs],
                    gr_ref[2 * N_UNIT + u:2 * N_UNIT + u + 1, cs],
                    gr_ref[3 * N_UNIT + u:3 * N_UNIT + u + 1, cs],
                    st, m_cur[u] if carried else zero_row, mask, carried)
                h_ref[hs, cs] = h
                s_ref[slot + u] = st_new
                if carried:
                    m_cur[u] = m_new
                else:
                    m_ref[slot + u:slot + u + 1, :] = m_new
    if carried:
        m_ref[...] = jnp.concatenate(m_cur, axis=0)

    if has_out:
        @pl.when(c == nc - 1)
        def _():
            for j in range(s_ref.shape[0]):
                q, u = divmod(j, N_UNIT)
                co_ref[q, u] = jnp.transpose(s_ref[j, 0:HD, :])
                no_ref[q, u:u + 1, :] = s_ref[j, HD:HD + 1, :]
                mo_ref[q, u:u + 1, :] = m_ref[j:j + 1, :]


def _mlstm(k, qt, vt, gr, batch, rounds, s0=None, m0=None, want_state=False):
    n = k.shape[0]
    nc = n // batch // SCAN
    by_sequence = nc == 1
    width = rounds * SCAN
    has_init = s0 is not None
    if by_sequence:
        assert batch % rounds == 0 and not has_init
        grid = (batch // rounds, 1)
        slots = rounds
        fwd = bwd = lambda b, c: b
    else:
        assert nc % rounds == 0 and not want_state
        steps = nc // rounds
        grid = (batch, steps)
        slots = 1
        fwd = lambda b, c: b * steps + c
        bwd = lambda b, c: b * steps + steps - 1 - c

    def specs(ix):
        return [
            pl.BlockSpec((width, D_MLSTM), lambda b, c: (ix(b, c), 0)),
            pl.BlockSpec((D_MLSTM, width), lambda b, c: (0, ix(b, c))),
            pl.BlockSpec((D_MLSTM, width), lambda b, c: (0, ix(b, c))),
            pl.BlockSpec((GR_ROWS, width), lambda b, c: (0, ix(b, c))),
        ]

    in_specs = specs(fwd) if by_sequence else specs(fwd) + specs(bwd)
    args = [k, qt, vt, gr] if by_sequence else [k, qt, vt, gr, k, qt, vt, gr]
    if has_init:
        in_specs = [
            pl.BlockSpec((1, N_UNIT, ST_ROWS, HD), lambda b, c: (b, 0, 0, 0)),
            pl.BlockSpec((1, N_UNIT, SCAN), lambda b, c: (b, 0, 0)),
        ] + in_specs
        args = [s0, m0] + args
    out_shape = [
        jax.ShapeDtypeStruct((D_MLSTM, n), F32),
        jax.ShapeDtypeStruct((D_MLSTM, n), F32),
    ]
    out_specs = [
        pl.BlockSpec((D_MLSTM, width), lambda b, c: (0, fwd(b, c))),
        pl.BlockSpec((D_MLSTM, width), lambda b, c: (0, bwd(b, c))),
    ]
    if want_state:
        out_shape += [
            jax.ShapeDtypeStruct((batch, N_UNIT, HD, HD), F32),
            jax.ShapeDtypeStruct((batch, N_UNIT, HD), F32),
            jax.ShapeDtypeStruct((batch, N_UNIT, SCAN), F32),
        ]
        out_specs += [
            pl.BlockSpec((slots, N_UNIT, HD, HD), lambda b, c: (b, 0, 0, 0)),
            pl.BlockSpec((slots, N_UNIT, HD), lambda b, c: (b, 0, 0)),
            pl.BlockSpec((slots, N_UNIT, SCAN), lambda b, c: (b, 0, 0)),
        ]
    return pl.pallas_call(
        functools.partial(_mlstm_kernel, has_init=has_init, has_out=want_state, rounds=rounds,
                          by_sequence=by_sequence),
        out_shape=out_shape,
        grid=grid,
        in_specs=in_specs,
        out_specs=out_specs,
        scratch_shapes=[pltpu.VMEM((slots * N_UNIT, ST_ROWS, HD), F32),
                        pltpu.VMEM((slots * N_UNIT, SCAN), F32)],
        compiler_params=pltpu.CompilerParams(
            dimension_semantics=("arbitrary", "arbitrary"), vmem_limit_bytes=VMEM_LIMIT_BYTES),
        name="mlstm",
    )(*args)


def _outffn_mix(x, a, hft, hbt, o, mod, c, x1_ref, h2_ref):
    ga1 = mod[:, 0:D_MODEL]
    sh2 = mod[:, D_MODEL:2 * D_MODEL]
    sc2 = mod[:, 2 * D_MODEL:3 * D_MODEL]
    hs = jnp.transpose(hft + hbt)
    sig = _sigmoid(o)
    parts = []
    for hd in range(HEADS):
        sl = slice(hd * HD, (hd + 1) * HD)
        parts.append(_rms(hs[:, sl], c["gh"][:, sl]) * sig[:, sl])
    b_out = jnp.concatenate(parts, axis=-1).astype(BF16)
    mix = _dot(a, c["wout"][0:D_GMLP, :]) + _dot(b_out, c["wout"][D_GMLP:, :])
    x1 = x + ga1 * mix
    x1_ref[...] = x1
    h2_ref[...] = (_rms(x1, c["g2"][...]) * (1.0 + sc2) + sh2).astype(BF16)


def _outffn_up(h2_ref, x1_in_ref, c, f_ref, x1_out_ref):
    h2 = h2_ref[...]
    u = _dot(h2, c["w1"][...])
    g = _dot(h2, c["w3"][...])
    f_ref[...] = (u * _sigmoid(u) * g).astype(BF16)
    x1_out_ref[...] = x1_in_ref[...]


def _outffn_down(f_ref, x1_ref, mod, c, y_ref):
    ga2 = mod[:, 3 * D_MODEL:4 * D_MODEL]
    x2 = x1_ref[...] + ga2 * _dot(f_ref[...], c["w2"][...])
    y_ref[...] = _rms(x2, c["gf"][...])


def _outffn_kernel(x_ref, a_ref, hft_ref, hbt_ref, o_ref, mod_ref, gh_ref, wout_ref,
                   g2_ref, w1_ref, w3_ref, w2_ref, gf_ref, y_ref,
                   x1a_s, h2_s, f_s, x1b_s, *, blocks_per_mod, mod_row0):
    tb = x_ref.shape[0] // 2
    i = pl.program_id(0)
    mod_in = mod_row0 + jnp.minimum(i, pl.num_programs(0) - 2) // blocks_per_mod
    mod_out = mod_row0 + jnp.maximum(i - 1, 0) // blocks_per_mod
    c = {"gh": gh_ref, "wout": wout_ref, "g2": g2_ref, "w1": w1_ref, "w3": w3_ref, "w2": w2_ref,
         "gf": gf_ref}

    def tick(half, mix, up, down):
        rows = slice(half * tb, (half + 1) * tb)
        cur, oth = half, 1 - half
        if down:
            _outffn_down(f_s.at[cur], x1b_s.at[cur], mod_ref[mod_out], c, y_ref.at[rows, :])
        if mix:
            _outffn_mix(x_ref[rows, :], a_ref[rows, :], hft_ref[:, rows], hbt_ref[:, rows],
                        o_ref[rows, :], mod_ref[mod_in], c, x1a_s.at[cur], h2_s.at[cur])
        if up:
            _outffn_up(h2_s.at[oth], x1a_s.at[oth], c, f_s.at[oth], x1b_s.at[oth])

    _pipeline_step(tick)


def _outffn(x, a, hft, hbt, o, mod, mod_row0, rows_per_mod, p):
    n = x.shape[0]
    tb = TOKEN_BLOCK
    tb2 = 2 * tb
    assert n % tb2 == 0 and rows_per_mod % tb2 == 0
    nb = n // tb2
    d_ff = p["w1"].shape[1]
    blk_in = lambda i: jnp.minimum(i, nb - 1)
    blk_out = lambda i: jnp.maximum(i - 1, 0)
    tok = lambda w: pl.BlockSpec((tb2, w), lambda i: (blk_in(i), 0))
    tok_t = lambda h: pl.BlockSpec((h, tb2), lambda i: (0, blk_in(i)))
    in_specs = [
        tok(D_MODEL), tok(D_GMLP), tok_t(D_MLSTM), tok_t(D_MLSTM), tok(D_MLSTM),
        _const_spec(mod.shape),
        _const_spec((1, D_MLSTM)),
        _const_spec((D_GMLP + D_MLSTM, D_MODEL)),
        _const_spec((1, D_MODEL)),
        _const_spec((D_MODEL, d_ff)),
        _const_spec((D_MODEL, d_ff)),
        _const_spec((d_ff, D_MODEL)),
        _const_spec((1, D_MODEL)),
    ]
    scratch_shapes = [
        pltpu.VMEM((2, tb, D_MODEL), F32),
        pltpu.VMEM((2, tb, D_MODEL), BF16),
        pltpu.VMEM((2, tb, d_ff), BF16),
        pltpu.VMEM((2, tb, D_MODEL), F32),
    ]
    return pl.pallas_call(
        functools.partial(_outffn_kernel, blocks_per_mod=rows_per_mod // tb2, mod_row0=mod_row0),
        out_shape=jax.ShapeDtypeStruct((n, D_MODEL), F32),
        grid=(nb + 1,),
        in_specs=in_specs,
        out_specs=pl.BlockSpec((tb2, D_MODEL), lambda i: (blk_out(i), 0)),
        scratch_shapes=scratch_shapes,
        compiler_params=pltpu.CompilerParams(
            dimension_semantics=("arbitrary",), vmem_limit_bytes=VMEM_LIMIT_BYTES),
        name="outffn",
    )(x, a, hft, hbt, o, mod, p["gh"], p["wout"], p["g2"], p["w1"], p["w3"], p["w2"], p["gf"])


def _regroup_gates(g):
    h = HEADS
    return jnp.concatenate([g[:, 0:h], g[:, 2 * h:3 * h], g[:, h:2 * h], g[:, 3 * h:4 * h]], axis=1)


def _layer_params(l, g_norm1, w_in, b_gate, w_s, b_s, g_v, conv_w, conv_b, g_h, g_norm2, g_final):
    wg = _regroup_gates(w_in[l][:, OFF_G:])
    bg = _regroup_gates(b_gate[l][None, :])
    return {
        "g1": g_norm1[l][None, :],
        "wit": w_in[l].T.astype(BF16),
        "wgt": wg.T, "bgt": bg.T,
        "cw": conv_w[l], "cb": conv_b[l][None, :],
        "ws": w_s[l].astype(BF16),
        "bs": jnp.repeat(b_s[l].T, HD, axis=1),
        "gv": g_v[l].reshape(1, D_GMLP),
        "gh": g_h[l].reshape(1, D_MLSTM),
        "g2": g_norm2[l][None, :],
        "gf": g_final[None, :],
    }


_LATE_WEIGHTS = ("wout", "w1", "w3", "w2")


def _trunk(x, mod_row0, rows_per_mod, seg, batch, rounds, p, late=None, s0=None, m0=None,
           want_state=False):
    cast, ada_tail = ((), None) if late is None else (
        tuple(late[0][name] for name in _LATE_WEIGHTS), late[1])
    a, qt, k, vt, o, gr, *side = _inproj(x, p["mod_head"], mod_row0, rows_per_mod, seg, p, cast,
                                         ada_tail)
    if late is not None:
        p = dict(p, **dict(zip(_LATE_WEIGHTS, side[:-1])))
        p["mod_tail"] = side[-1].reshape(side[-1].shape[0], 1, side[-1].shape[1])
    hft, hbt, *state = _mlstm(k, qt, vt, gr, batch, rounds, s0, m0, want_state)
    y = _outffn(x, a, hft, hbt, o, p["mod_tail"], mod_row0, rows_per_mod, p)
    return y, state, p


def kernel(x_prompt, x_sample, state_C, state_n, state_m, c, c_ctx, w_ada, b_ada, g_norm1, w_in,
           b_gate, w_s, b_s, g_v, conv_w, conv_b, g_h, w_out, g_norm2, w1, w3, w2, g_final):
    bp, tp, d = x_prompt.shape
    bs_, ts, _ = x_sample.shape
    depth = w_in.shape[0]
    assert depth == 1, "final norm is fused into the layer's last kernel"
    xp = x_prompt.reshape(bp * tp, d)
    xs = x_sample.reshape(bs_ * ts, d)

    cs = jnp.zeros((8, d), F32).at[0].set(c_ctx).at[1:1 + bs_].set(c)
    new_c, new_n, new_m = [], [], []
    for l in range(depth):
        p = _layer_params(l, g_norm1, w_in, b_gate, w_s, b_s, g_v, conv_w, conv_b, g_h, g_norm2,
                          g_final)
        late_f32 = {"wout": w_out[l], "w1": w1[l], "w3": w3[l], "w2": w2[l]}
        p["mod_head"] = _ada(cs, w_ada[l], b_ada[l][None, :], 2 * d).reshape(8, 1, 2 * d)
        ada_tail = (cs, w_ada[l], b_ada[l][None, :], 2 * d)

        xp, (c_ctx_out, n_ctx_out, m_ctx_out), p = _trunk(
            xp, 0, bp * tp, tp, bp, CTX_SEQS_PER_STEP, p, (late_f32, ada_tail), want_state=True)
        new_c.append(c_ctx_out.reshape(bp, N_DIR, HEADS, HD, HD))
        new_n.append(n_ctx_out.reshape(bp, N_DIR, HEADS, HD))
        new_m.append(m_ctx_out[..., 0].reshape(bp, N_DIR, HEADS))

        s0 = jnp.concatenate(
            [jnp.swapaxes(state_C[:, l], -1, -2), state_n[:, l][..., None, :],
             jnp.zeros((bs_, N_DIR, HEADS, ST_ROWS - HD - 1, HD), F32)],
            axis=-2).reshape(bs_, N_UNIT, ST_ROWS, HD)
        m0 = jnp.broadcast_to(state_m[:, l].reshape(bs_, N_UNIT, 1), (bs_, N_UNIT, SCAN))
        xs, _, _ = _trunk(xs, 1, ts, GRID_W, bs_, LAT_CHUNKS_PER_STEP, p, s0=s0, m0=m0)

    return (xp.reshape(bp, tp, d), xs.reshape(bs_, ts, d),
            jnp.stack(new_c, axis=1), jnp.stack(new_n, axis=1), jnp.stack(new_m, axis=1))
```

```python
import functools

import jax
import jax.numpy as jnp
from jax import lax
from jax.experimental import pallas as pl
from jax.experimental.pallas import tpu as pltpu

D_MODEL = 1024
D_GMLP = 512
D_MLSTM = 512
GROUPS = 4
HEADS = 4
HD = 128
CHUNK = 128
SCAN = 256
N_DIR = 2
N_UNIT = N_DIR * HEADS
GRID_W = 64
OFF_V, OFF_Q, OFF_VV, OFF_O, OFF_G = 512, 1024, 2048, 2560, 3072
D_IN = OFF_G + 2 * N_DIR * HEADS
EPS = 1e-6
NEG = -1e30
ST_ROWS = HD + 16
N_GATE = 2 * N_UNIT
GR_ROWS = 5 * N_GATE

TOKEN_BLOCK = 256
CTX_SEQS_PER_STEP = 2
LAT_CHUNKS_PER_STEP = 2
VMEM_LIMIT_BYTES = 56 * 1024 * 1024

F32 = jnp.float32
BF16 = jnp.bfloat16


def _rms(x, g):
    return x * lax.rsqrt(jnp.mean(x * x, axis=-1, keepdims=True) + EPS) * g


def _sigmoid(x):
    return 1.0 / (1.0 + jnp.exp(-x))


def _gelu_tanh(x):
    return 0.5 * x * (1.0 + jnp.tanh(0.7978845608028654 * (x + 0.044715 * (x * x * x))))


def _log_sigmoid(x):
    return jnp.minimum(x, 0.0) - jnp.log(1.0 + jnp.exp(-jnp.abs(x)))


def _dot(a, b):
    return jnp.dot(a, b, preferred_element_type=F32)


def _dot_nt(a, b):
    return lax.dot_general(a, b, (((1,), (1,)), ((), ())), preferred_element_type=F32)


def _dot_exact(a, b):
    return jnp.dot(a, b, preferred_element_type=F32, precision=lax.Precision.HIGHEST)


def _pipeline_step(tick):
    i = pl.program_id(0)
    last = pl.num_programs(0) - 1

    @pl.when(i == 0)
    def _():
        tick(0, True, False, False)
        tick(1, True, True, False)

    @pl.when(jnp.logical_and(i > 0, i < last))
    def _():
        tick(0, True, True, True)
        tick(1, True, True, True)

    @pl.when(i == last)
    def _():
        tick(0, False, True, True)
        tick(1, False, False, True)


def _const_spec(shape):
    zeros = (0,) * len(shape)
    return pl.BlockSpec(shape, lambda *_: zeros, pipeline_mode=pl.Buffered(1))


def _ada_kernel(c_ref, w_ref, b_ref, o_ref):
    c = c_ref[...]
    s = (c * _sigmoid(c)).astype(BF16)
    o_ref[...] = _dot(s, w_ref[...].astype(BF16)) + b_ref[...]


def _ada(cs, w_ada, b_ada, n):
    rows, d = cs.shape
    tn = 1024
    return pl.pallas_call(
        _ada_kernel,
        out_shape=jax.ShapeDtypeStruct((rows, n), F32),
        grid=(n // tn,),
        in_specs=[
            pl.BlockSpec((rows, d), lambda j: (0, 0)),
            pl.BlockSpec((d, tn), lambda j: (0, j)),
            pl.BlockSpec((1, tn), lambda j: (0, j)),
        ],
        out_specs=pl.BlockSpec((rows, tn), lambda j: (0, j)),
        compiler_params=pltpu.CompilerParams(
            dimension_semantics=("arbitrary",), vmem_limit_bytes=VMEM_LIMIT_BYTES),
        name="ada",
    )(cs, w_ada, b_ada)


def _inproj_norm(x, mod, g1, hb_ref):
    sh1 = mod[:, 0:D_MODEL]
    sc1 = mod[:, D_MODEL:2 * D_MODEL]
    hb_ref[...] = (_rms(x, g1) * (1.0 + sc1) + sh1).astype(BF16)


def _inproj_project(hb_ref, w, z):
    hb = hb_ref[...]
    z["u"][...] = _dot_nt(hb, w["u"][...])
    z["v"][...] = _dot_nt(hb, w["v"][...])
    z["qk"][...] = _dot_nt(hb, w["qk"][...])
    z["o"][...] = _dot_nt(hb, w["o"][...])
    z["vt"][...] = _dot_nt(w["vv"][...], hb).astype(BF16)
    z["gt"][...] = _dot_nt(w["gt"][...], hb)


def _inproj_finish(z, c, out, seg):
    tb = z["u"].shape[0]
    for g in range(GROUPS):
        gs = slice(g * HD, (g + 1) * HD)
        vg = _rms(_gelu_tanh(z["v"][:, gs]), c["gv"][:, gs]).astype(BF16)
        for ch in range(tb // CHUNK):
            cs = slice(ch * CHUNK, (ch + 1) * CHUNK)
            mixed = _dot(c["ws"][g], vg[cs]) + c["bs"][:, gs]
            out["a"][cs, gs] = (_gelu_tanh(z["u"][cs, gs]) * mixed).astype(BF16)

    zqk = z["qk"][...]
    pos = lax.broadcasted_iota(jnp.int32, (tb, 1), 0) % seg
    prev = jnp.where(pos != 0, pltpu.roll(zqk, 1, 0), 0.0)
    nxt = jnp.where(pos != seg - 1, pltpu.roll(zqk, tb - 1, 0), 0.0)
    cw = c["cw"]
    y = c["cb"][...] + prev * cw[0:1, :] + zqk * cw[1:2, :] + nxt * cw[2:3, :]
    y = y * _sigmoid(y)
    out["qt"][...] = jnp.transpose(y[:, :D_MLSTM]).astype(BF16)
    out["k"][...] = (y[:, D_MLSTM:] * (HD ** -0.5)).astype(BF16)
    out["vt"][...] = z["vt"][...]
    out["o"][...] = z["o"][...]

    assert tb == SCAN
    gt = z["gt"][...] + c["bgt"][...]
    gi = pltpu.roll(gt, HEADS, 0)
    lf = _log_sigmoid(gt)
    lane = lax.broadcasted_iota(jnp.int32, (N_GATE, HD), 1)
    tiles = []
    carry = jnp.zeros((N_GATE, HD), F32)
    for t in range(SCAN // HD):
        prefix = lf[:, t * HD:(t + 1) * HD]
        shift = 1
        while shift < HD:
            prefix = prefix + jnp.where(lane >= shift, pltpu.roll(prefix, shift, 1), 0.0)
            shift *= 2
        tiles.append(prefix + carry)
        carry = carry + jnp.broadcast_to(prefix[:, HD - 1:HD], (N_GATE, HD))
    prefix = jnp.concatenate(tiles, axis=1)
    b_last = jnp.concatenate([carry] * (SCAN // HD), axis=1)
    row_bwd = lax.broadcasted_iota(jnp.int32, (N_GATE, SCAN), 0) >= N_GATE // N_DIR
    b_row = jnp.where(row_bwd, b_last - prefix + lf, prefix)
    g_row = b_last - b_row + gi
    g_max = jnp.broadcast_to(jnp.max(g_row, axis=1, keepdims=True), (N_GATE, SCAN))
    for sec, stat in enumerate((b_row, g_row, b_last, g_max, gi - b_row)):
        out["gr"][sec * N_GATE:(sec + 1) * N_GATE, :] = stat


_INPROJ_Z = ("u", "v", "qk", "o", "vt", "gt")


def _inproj_kernel(*refs, seg, blocks_per_mod, mod_row0, n_cast, ada_tail):
    refs = list(refs)
    take = lambda k: [refs.pop(0) for _ in range(k)]
    (x_ref, mod_ref, g1_ref, wit_ref,
     bgt_ref, cw_ref, cb_ref, ws_ref, bs_ref, gv_ref) = take(10)
    cast_in = take(n_cast)
    ada_in = take(3) if ada_tail else None
    a_ref, qt_ref, k_ref, vt_ref, o_ref, gr_ref = take(6)
    cast_out = take(n_cast)
    ada_out = take(1)[0] if ada_tail else None
    hb_s, *z_s = refs
    tb = x_ref.shape[0] // 2
    mod_in = mod_row0 + jnp.minimum(pl.program_id(0), pl.num_programs(0) - 2) // blocks_per_mod
    w = {"u": wit_ref.at[0:OFF_V, :], "v": wit_ref.at[OFF_V:OFF_Q, :],
         "qk": wit_ref.at[OFF_Q:OFF_VV, :], "vv": wit_ref.at[OFF_VV:OFF_O, :],
         "o": wit_ref.at[OFF_O:OFF_G, :], "gt": wit_ref.at[OFF_G:D_IN, :]}
    c = {"gv": gv_ref, "ws": ws_ref, "bs": bs_ref, "cw": cw_ref, "cb": cb_ref, "bgt": bgt_ref}

    def tick(half, norm, project, finish):
        rows = slice(half * tb, (half + 1) * tb)
        cur, oth = half, 1 - half
        z_cur = {name: ref.at[cur] for name, ref in zip(_INPROJ_Z, z_s)}
        z_oth = {name: ref.at[oth] for name, ref in zip(_INPROJ_Z, z_s)}
        out = {"a": a_ref.at[rows, :], "qt": qt_ref.at[:, rows], "k": k_ref.at[rows, :],
               "vt": vt_ref.at[:, rows], "o": o_ref.at[rows, :], "gr": gr_ref.at[:, rows]}
        if project:
            _inproj_project(hb_s.at[oth], w, z_oth)
        if norm:
            _inproj_norm(x_ref[rows, :], mod_ref[pl.ds(mod_in, 1), :], g1_ref[...], hb_s.at[cur])
        if finish:
            _inproj_finish(z_cur, c, out, seg)
        if half == 0:
            for src, dst in zip(cast_in, cast_out):
                dst[...] = src[...].astype(BF16)
            if ada_tail:
                _ada_kernel(*ada_in, ada_out)

    _pipeline_step(tick)


def _inproj(x, mod, mod_row0, rows_per_mod, seg, p, cast=(), ada_tail=None):
    n = x.shape[0]
    tb = TOKEN_BLOCK
    tb2 = 2 * tb
    assert tb % seg == 0 and n % tb2 == 0 and rows_per_mod % tb2 == 0
    nb = n // tb2
    blk_in = lambda i: jnp.minimum(i, nb - 1)
    blk_out = lambda i: jnp.maximum(i - 1, 0)
    tok = lambda width: pl.BlockSpec((tb2, width), lambda i: (blk_out(i), 0))
    tok_t = lambda height: pl.BlockSpec((height, tb2), lambda i: (0, blk_out(i)))
    in_specs = [
        pl.BlockSpec((tb2, D_MODEL), lambda i: (blk_in(i), 0)),
        _const_spec(mod.shape),
        _const_spec((1, D_MODEL)),
        _const_spec((D_IN, D_MODEL)),
        _const_spec((N_GATE, 1)),
        _const_spec((3, 2 * D_MLSTM)),
        _const_spec((1, 2 * D_MLSTM)),
        _const_spec((GROUPS, CHUNK, CHUNK)),
        _const_spec((CHUNK, D_GMLP)),
        _const_spec((1, D_GMLP)),
    ]
    out_shape = [
        jax.ShapeDtypeStruct((n, D_GMLP), BF16),
        jax.ShapeDtypeStruct((D_MLSTM, n), BF16),
        jax.ShapeDtypeStruct((n, D_MLSTM), BF16),
        jax.ShapeDtypeStruct((D_MLSTM, n), BF16),
        jax.ShapeDtypeStruct((n, D_MLSTM), F32),
        jax.ShapeDtypeStruct((GR_ROWS, n), F32),
    ]
    out_specs = [tok(D_GMLP), tok_t(D_MLSTM), tok(D_MLSTM), tok_t(D_MLSTM), tok(D_MLSTM),
                 tok_t(GR_ROWS)]
    for wf in cast:
        rows = wf.shape[0] // nb
        assert rows * nb == wf.shape[0] and rows % 16 == 0
        slab = pl.BlockSpec((rows, wf.shape[1]), lambda i: (blk_in(i), 0))
        in_specs.append(slab)
        out_specs.append(slab)
        out_shape.append(jax.ShapeDtypeStruct(wf.shape, BF16))
    side = list(cast)
    if ada_tail is not None:
        cs, w_ada, b_ada, col0 = ada_tail
        width = (w_ada.shape[1] - col0) // nb
        assert col0 % width == 0 and width * nb == w_ada.shape[1] - col0 and width % 128 == 0
        in_specs += [
            _const_spec(cs.shape),
            pl.BlockSpec((w_ada.shape[0], width), lambda i: (0, col0 // width + blk_in(i))),
            pl.BlockSpec((1, width), lambda i: (0, col0 // width + blk_in(i))),
        ]
        out_specs.append(pl.BlockSpec((cs.shape[0], width), lambda i: (0, blk_in(i))))
        out_shape.append(jax.ShapeDtypeStruct((cs.shape[0], w_ada.shape[1] - col0), F32))
        side += [cs, w_ada, b_ada]
    scratch_shapes = [
        pltpu.VMEM((2, tb, D_MODEL), BF16),
        pltpu.VMEM((2, tb, D_GMLP), F32),
        pltpu.VMEM((2, tb, D_GMLP), F32),
        pltpu.VMEM((2, tb, 2 * D_MLSTM), F32),
        pltpu.VMEM((2, tb, D_MLSTM), F32),
        pltpu.VMEM((2, D_MLSTM, tb), BF16),
        pltpu.VMEM((2, N_GATE, tb), F32),
    ]
    return pl.pallas_call(
        functools.partial(_inproj_kernel, seg=seg, blocks_per_mod=rows_per_mod // tb2,
                          mod_row0=mod_row0, n_cast=len(cast), ada_tail=ada_tail is not None),
        out_shape=out_shape,
        grid=(nb + 1,),
        in_specs=in_specs,
        out_specs=out_specs,
        scratch_shapes=scratch_shapes,
        compiler_params=pltpu.CompilerParams(
            dimension_semantics=("arbitrary",), vmem_limit_bytes=VMEM_LIMIT_BYTES),
        name="inproj",
    )(x, mod, p["g1"], p["wit"],
      p["bgt"], p["cw"], p["cb"], p["ws"], p["bs"], p["gv"], *side)


def _lane_broadcast_column(row):
    n = row.shape[1]
    tiles = [jnp.transpose(jnp.broadcast_to(row[:, t * HD:(t + 1) * HD], (HD, HD)))
             for t in range(n // HD)]
    col = jnp.concatenate(tiles, axis=0)
    return jnp.concatenate([col] * (n // HD), axis=1)


def _mlstm_unit(k, vt, kq, qs, d_row, b_row, g_row, bl_row, gmax_row, st, m_row, mask, carried):
    logw = jnp.where(mask, _lane_broadcast_column(d_row) + b_row, NEG)
    a = b_row + m_row
    mj = jnp.maximum(a, jnp.max(logw, axis=0, keepdims=True))
    w = jnp.exp(logw - mj)
    s = kq * w
    num = _dot(vt, s.astype(BF16))
    den = jnp.sum(s, axis=0, keepdims=True)
    if carried:
        inter = jnp.exp(a - mj)
        num = num + inter * qs[:HD]
        den = den + inter * qs[HD:HD + 1]
    h = num * (1.0 / jnp.maximum(jnp.abs(den), jnp.exp(-mj)))
    m_new = jnp.maximum(bl_row + m_row, gmax_row)
    wc = jnp.exp(g_row - m_new)
    pad_row = lax.broadcasted_iota(jnp.int32, (ST_ROWS - HD, wc.shape[1]), 0)
    v_aug = jnp.concatenate(
        [vt.astype(F32) * wc, jnp.where(pad_row == 0, wc, 0.0)], axis=0).astype(BF16)
    st_new = _dot(v_aug, k)
    if carried:
        decay = jnp.exp(bl_row + m_row - m_new)
        st_new = decay[:, :HD] * st + st_new
    return h, st_new, m_new


def _mlstm_kernel(*refs, has_init, has_out, rounds, by_sequence):
    if has_init:
        s0_ref, m0_ref = refs[:2]
        refs = refs[2:]
    if by_sequence:
        k_ref, qt_ref, vt_ref, gr_ref = refs[:4]
        fwd_in = bwd_in = (k_ref, qt_ref, vt_ref, gr_ref)
        refs = refs[4:]
    else:
        fwd_in, bwd_in = refs[:4], refs[4:8]
        refs = refs[8:]
    hf_ref, hb_ref = refs[:2]
    refs = refs[2:]
    if has_out:
        co_ref, no_ref, mo_ref = refs[:3]
        refs = refs[3:]
    s_ref, m_ref = refs
    c = pl.program_id(1)
    nc = pl.num_programs(1)
    carried = not by_sequence

    if carried:
        @pl.when(c == 0)
        def _():
            if has_init:
                s_ref[...] = s0_ref[0]
                m_ref[...] = m0_ref[0]
            else:
                s_ref[...] = jnp.zeros_like(s_ref)
                m_ref[...] = jnp.zeros_like(m_ref)

    si = lax.broadcasted_iota(jnp.int32, (SCAN, SCAN), 0)
    ji = lax.broadcasted_iota(jnp.int32, (SCAN, SCAN), 1)
    dirs = ((fwd_in, hf_ref, si <= ji), (bwd_in, hb_ref, si >= ji))

    def chunk_cols(d, r):
        pos = rounds - 1 - r if (d == 1 and not by_sequence) else r
        return slice(pos * SCAN, (pos + 1) * SCAN)

    kq = {}
    for r in range(rounds):
        for d, ((k_ref, qt_ref, _, _), _, _) in enumerate(dirs):
            cs = chunk_cols(d, r)
            for hd in range(HEADS):
                hs = slice(hd * HD, (hd + 1) * HD)
                kq[r, d, hd] = _dot(k_ref[cs, hs], qt_ref[hs, cs])

    m_cur = None
    if carried:
        m_all = m_ref[...]
        m_cur = [m_all[u:u + 1] for u in range(N_UNIT)]
    zero_row = jnp.zeros((1, SCAN), F32)
    for r in range(rounds):
        slot = r * N_UNIT if by_sequence else 0
        for d, ((k_ref, qt_ref, vt_ref, gr_ref), h_ref, mask) in enumerate(dirs):
            cs = chunk_cols(d, r)
            for hd in range(HEADS):
                u = d * HEADS + hd
                row = d * 2 * HEADS + HEADS + hd
                hs = slice(hd * HD, (hd + 1) * HD)
                st = s_ref[slot + u] if carried else None
                qs = _dot(st.astype(BF16), qt_ref[hs, cs]) if carried else None
                h, st_new, m_new = _mlstm_unit(
                    k_ref[cs, hs], vt_ref[hs, cs], kq[r, d, hd], qs,
                    *(gr_ref[sec * N_GATE + row:sec * N_GATE + row + 1, cs]
                      for sec in (4, 0, 1, 2, 3)),
                    st, m_cur[u] if carried else zero_row, mask, carried)
                h_ref[hs, cs] = h
                s_ref[slot + u] = st_new
                if carried:
                    m_cur[u] = m_new
                else:
                    m_ref[slot + u:slot + u + 1, :] = m_new
    if carried:
        m_ref[...] = jnp.concatenate(m_cur, axis=0)

    if has_out:
        @pl.when(c == nc - 1)
        def _():
            for j in range(s_ref.shape[0]):
                q, u = divmod(j, N_UNIT)
                co_ref[q, u] = jnp.transpose(s_ref[j, 0:HD, :])
                no_ref[q, u:u + 1, :] = s_ref[j, HD:HD + 1, :]
                mo_ref[q, u:u + 1, :] = m_ref[j:j + 1, :]


def _mlstm(k, qt, vt, gr, batch, rounds, s0=None, m0=None, want_state=False):
    n = k.shape[0]
    nc = n // batch // SCAN
    by_sequence = nc == 1
    width = rounds * SCAN
    has_init = s0 is not None
    if by_sequence:
        assert batch % rounds == 0 and not has_init
        grid = (batch // rounds, 1)
        slots = rounds
        fwd = bwd = lambda b, c: b
    else:
        assert nc % rounds == 0 and not want_state
        steps = nc // rounds
        grid = (batch, steps)
        slots = 1
        fwd = lambda b, c: b * steps + c
        bwd = lambda b, c: b * steps + steps - 1 - c

    def specs(ix):
        return [
            pl.BlockSpec((width, D_MLSTM), lambda b, c: (ix(b, c), 0)),
            pl.BlockSpec((D_MLSTM, width), lambda b, c: (0, ix(b, c))),
            pl.BlockSpec((D_MLSTM, width), lambda b, c: (0, ix(b, c))),
            pl.BlockSpec((GR_ROWS, width), lambda b, c: (0, ix(b, c))),
        ]

    in_specs = specs(fwd) if by_sequence else specs(fwd) + specs(bwd)
    args = [k, qt, vt, gr] if by_sequence else [k, qt, vt, gr, k, qt, vt, gr]
    if has_init:
        in_specs = [
            pl.BlockSpec((1, N_UNIT, ST_ROWS, HD), lambda b, c: (b, 0, 0, 0)),
            pl.BlockSpec((1, N_UNIT, SCAN), lambda b, c: (b, 0, 0)),
        ] + in_specs
        args = [s0, m0] + args
    out_shape = [
        jax.ShapeDtypeStruct((D_MLSTM, n), F32),
        jax.ShapeDtypeStruct((D_MLSTM, n), F32),
    ]
    out_specs = [
        pl.BlockSpec((D_MLSTM, width), lambda b, c: (0, fwd(b, c))),
        pl.BlockSpec((D_MLSTM, width), lambda b, c: (0, bwd(b, c))),
    ]
    if want_state:
        out_shape += [
            jax.ShapeDtypeStruct((batch, N_UNIT, HD, HD), F32),
            jax.ShapeDtypeStruct((batch, N_UNIT, HD), F32),
            jax.ShapeDtypeStruct((batch, N_UNIT, SCAN), F32),
        ]
        out_specs += [
            pl.BlockSpec((slots, N_UNIT, HD, HD), lambda b, c: (b, 0, 0, 0)),
            pl.BlockSpec((slots, N_UNIT, HD), lambda b, c: (b, 0, 0)),
            pl.BlockSpec((slots, N_UNIT, SCAN), lambda b, c: (b, 0, 0)),
        ]
    return pl.pallas_call(
        functools.partial(_mlstm_kernel, has_init=has_init, has_out=want_state, rounds=rounds,
                          by_sequence=by_sequence),
        out_shape=out_shape,
        grid=grid,
        in_specs=in_specs,
        out_specs=out_specs,
        scratch_shapes=[pltpu.VMEM((slots * N_UNIT, ST_ROWS, HD), F32),
                        pltpu.VMEM((slots * N_UNIT, SCAN), F32)],
        compiler_params=pltpu.CompilerParams(
            dimension_semantics=("arbitrary", "arbitrary"), vmem_limit_bytes=VMEM_LIMIT_BYTES),
        name="mlstm",
    )(*args)


def _outffn_mix(x, a, hft, hbt, o, mod, c, x1_ref, h2_ref):
    ga1 = mod[:, 0:D_MODEL]
    sh2 = mod[:, D_MODEL:2 * D_MODEL]
    sc2 = mod[:, 2 * D_MODEL:3 * D_MODEL]
    hs = jnp.transpose(hft + hbt)
    sig = _sigmoid(o)
    parts = []
    for hd in range(HEADS):
        sl = slice(hd * HD, (hd + 1) * HD)
        parts.append(_rms(hs[:, sl], c["gh"][:, sl]) * sig[:, sl])
    b_out = jnp.concatenate(parts, axis=-1).astype(BF16)
    mix = _dot(a, c["wout"][0:D_GMLP, :]) + _dot(b_out, c["wout"][D_GMLP:, :])
    x1 = x + ga1 * mix
    x1_ref[...] = x1
    h2_ref[...] = (_rms(x1, c["g2"][...]) * (1.0 + sc2) + sh2).astype(BF16)


def _outffn_up(h2_ref, x1_in_ref, c, f_ref, x1_out_ref):
    h2 = h2_ref[...]
    u = _dot(h2, c["w1"][...])
    g = _dot(h2, c["w3"][...])
    f_ref[...] = (u * _sigmoid(u) * g).astype(BF16)
    x1_out_ref[...] = x1_in_ref[...]


def _outffn_down(f_ref, x1_ref, mod, c, y_ref):
    ga2 = mod[:, 3 * D_MODEL:4 * D_MODEL]
    x2 = x1_ref[...] + ga2 * _dot(f_ref[...], c["w2"][...])
    y_ref[...] = _rms(x2, c["gf"][...])


def _outffn_kernel(x_ref, a_ref, hft_ref, hbt_ref, o_ref, mod_ref, gh_ref, wout_ref,
                   g2_ref, w1_ref, w3_ref, w2_ref, gf_ref, y_ref,
                   x1a_s, h2_s, f_s, x1b_s, *, blocks_per_mod, mod_row0):
    tb = x_ref.shape[0] // 2
    i = pl.program_id(0)
    mod_in = mod_row0 + jnp.minimum(i, pl.num_programs(0) - 2) // blocks_per_mod
    mod_out = mod_row0 + jnp.maximum(i - 1, 0) // blocks_per_mod
    c = {"gh": gh_ref, "wout": wout_ref, "g2": g2_ref, "w1": w1_ref, "w3": w3_ref, "w2": w2_ref,
         "gf": gf_ref}

    def tick(half, mix, up, down):
        rows = slice(half * tb, (half + 1) * tb)
        cur, oth = half, 1 - half
        if down:
            _outffn_down(f_s.at[cur], x1b_s.at[cur], mod_ref[pl.ds(mod_out, 1), :], c,
                         y_ref.at[rows, :])
        if mix:
            _outffn_mix(x_ref[rows, :], a_ref[rows, :], hft_ref[:, rows], hbt_ref[:, rows],
                        o_ref[rows, :], mod_ref[pl.ds(mod_in, 1), :], c, x1a_s.at[cur],
                        h2_s.at[cur])
        if up:
            _outffn_up(h2_s.at[oth], x1a_s.at[oth], c, f_s.at[oth], x1b_s.at[oth])

    _pipeline_step(tick)


def _outffn(x, a, hft, hbt, o, mod, mod_row0, rows_per_mod, p):
    n = x.shape[0]
    tb = TOKEN_BLOCK
    tb2 = 2 * tb
    assert n % tb2 == 0 and rows_per_mod % tb2 == 0
    nb = n // tb2
    d_ff = p["w1"].shape[1]
    blk_in = lambda i: jnp.minimum(i, nb - 1)
    blk_out = lambda i: jnp.maximum(i - 1, 0)
    tok = lambda w: pl.BlockSpec((tb2, w), lambda i: (blk_in(i), 0))
    tok_t = lambda h: pl.BlockSpec((h, tb2), lambda i: (0, blk_in(i)))
    in_specs = [
        tok(D_MODEL), tok(D_GMLP), tok_t(D_MLSTM), tok_t(D_MLSTM), tok(D_MLSTM),
        _const_spec(mod.shape),
        _const_spec((1, D_MLSTM)),
        _const_spec((D_GMLP + D_MLSTM, D_MODEL)),
        _const_spec((1, D_MODEL)),
        _const_spec((D_MODEL, d_ff)),
        _const_spec((D_MODEL, d_ff)),
        _const_spec((d_ff, D_MODEL)),
        _const_spec((1, D_MODEL)),
    ]
    scratch_shapes = [
        pltpu.VMEM((2, tb, D_MODEL), F32),
        pltpu.VMEM((2, tb, D_MODEL), BF16),
        pltpu.VMEM((2, tb, d_ff), BF16),
        pltpu.VMEM((2, tb, D_MODEL), F32),
    ]
    return pl.pallas_call(
        functools.partial(_outffn_kernel, blocks_per_mod=rows_per_mod // tb2, mod_row0=mod_row0),
        out_shape=jax.ShapeDtypeStruct((n, D_MODEL), F32),
        grid=(nb + 1,),
        in_specs=in_specs,
        out_specs=pl.BlockSpec((tb2, D_MODEL), lambda i: (blk_out(i), 0)),
        scratch_shapes=scratch_shapes,
        compiler_params=pltpu.CompilerParams(
            dimension_semantics=("arbitrary",), vmem_limit_bytes=VMEM_LIMIT_BYTES),
        name="outffn",
    )(x, a, hft, hbt, o, mod, p["gh"], p["wout"], p["g2"], p["w1"], p["w3"], p["w2"], p["gf"])


def _layer_params(l, g_norm1, w_in, b_gate, w_s, b_s, g_v, conv_w, conv_b, g_h, g_norm2, g_final):
    return {
        "g1": g_norm1[l][None, :],
        "wit": w_in[l].T.astype(BF16),
        "bgt": b_gate[l][:, None],
        "cw": conv_w[l], "cb": conv_b[l][None, :],
        "ws": w_s[l].astype(BF16),
        "bs": jnp.repeat(b_s[l].T, HD, axis=1),
        "gv": g_v[l].reshape(1, D_GMLP),
        "gh": g_h[l].reshape(1, D_MLSTM),
        "g2": g_norm2[l][None, :],
        "gf": g_final[None, :],
    }


_LATE_WEIGHTS = ("wout", "w1", "w3", "w2")


def _trunk(x, mod_row0, rows_per_mod, seg, batch, rounds, p, late=None, s0=None, m0=None,
           want_state=False):
    cast, ada_tail = ((), None) if late is None else (
        tuple(late[0][name] for name in _LATE_WEIGHTS), late[1])
    a, qt, k, vt, o, gr, *side = _inproj(x, p["mod_head"], mod_row0, rows_per_mod, seg, p, cast,
                                         ada_tail)
    if late is not None:
        p = dict(p, **dict(zip(_LATE_WEIGHTS, side[:-1])))
        p["mod_tail"] = side[-1]
    hft, hbt, *state = _mlstm(k, qt, vt, gr, batch, rounds, s0, m0, want_state)
    y = _outffn(x, a, hft, hbt, o, p["mod_tail"], mod_row0, rows_per_mod, p)
    return y, state, p


def kernel(x_prompt, x_sample, state_C, state_n, state_m, c, c_ctx, w_ada, b_ada, g_norm1, w_in,
           b_gate, w_s, b_s, g_v, conv_w, conv_b, g_h, w_out, g_norm2, w1, w3, w2, g_final):
    bp, tp, d = x_prompt.shape
    bs_, ts, _ = x_sample.shape
    depth = w_in.shape[0]
    assert depth == 1, "final norm is fused into the layer's last kernel"
    xp = x_prompt.reshape(bp * tp, d)
    xs = x_sample.reshape(bs_ * ts, d)

    cs = jnp.zeros((8, d), F32).at[0].set(c_ctx).at[1:1 + bs_].set(c)
    new_c, new_n, new_m = [], [], []
    for l in range(depth):
        p = _layer_params(l, g_norm1, w_in, b_gate, w_s, b_s, g_v, conv_w, conv_b, g_h, g_norm2,
                          g_final)
        late_f32 = {"wout": w_out[l], "w1": w1[l], "w3": w3[l], "w2": w2[l]}
        p["mod_head"] = _ada(cs, w_ada[l], b_ada[l][None, :], 2 * d)
        ada_tail = (cs, w_ada[l], b_ada[l][None, :], 2 * d)

        xp, (c_ctx_out, n_ctx_out, m_ctx_out), p = _trunk(
            xp, 0, bp * tp, tp, bp, CTX_SEQS_PER_STEP, p, (late_f32, ada_tail), want_state=True)
        new_c.append(c_ctx_out.reshape(bp, N_DIR, HEADS, HD, HD))
        new_n.append(n_ctx_out.reshape(bp, N_DIR, HEADS, HD))
        new_m.append(m_ctx_out[..., 0].reshape(bp, N_DIR, HEADS))

        s0 = jnp.concatenate(
            [jnp.swapaxes(state_C[:, l], -1, -2), state_n[:, l][..., None, :],
             jnp.zeros((bs_, N_DIR, HEADS, ST_ROWS - HD - 1, HD), F32)],
            axis=-2).reshape(bs_, N_UNIT, ST_ROWS, HD)
        m0 = jnp.broadcast_to(state_m[:, l].reshape(bs_, N_UNIT, 1), (bs_, N_UNIT, SCAN))
        xs, _, _ = _trunk(xs, 1, ts, GRID_W, bs_, LAT_CHUNKS_PER_STEP, p, s0=s0, m0=m0)

    return (xp.reshape(bp, tp, d), xs.reshape(bs_, ts, d),
            jnp.stack(new_c, axis=1), jnp.stack(new_n, axis=1), jnp.stack(new_m, axis=1))
```

```python
import functools

import jax
import jax.numpy as jnp
from jax import lax
from jax.experimental import pallas as pl
from jax.experimental.pallas import tpu as pltpu

D_MODEL = 1024
D_GMLP = 512
D_MLSTM = 512
GROUPS = 4
HEADS = 4
HD = 128
CHUNK = 128
SCAN = 256
N_DIR = 2
N_UNIT = N_DIR * HEADS
GRID_W = 64
OFF_V, OFF_Q, OFF_VV, OFF_O, OFF_G = 512, 1024, 2048, 2560, 3072
D_IN = OFF_G + 2 * N_DIR * HEADS
EPS = 1e-6
NEG = -1e30
ST_ROWS = HD + 16
N_GATE = 2 * N_UNIT
GR_ROWS = 5 * N_GATE

TOKEN_BLOCK = 256
CTX_SEQS_PER_STEP = 2
LAT_CHUNKS_PER_STEP = 2
VMEM_LIMIT_BYTES = 56 * 1024 * 1024

F32 = jnp.float32
BF16 = jnp.bfloat16


def _rms(x, g):
    return x * lax.rsqrt(jnp.mean(x * x, axis=-1, keepdims=True) + EPS) * g


def _sigmoid(x):
    return 1.0 / (1.0 + jnp.exp(-x))


def _gelu_tanh(x):
    return 0.5 * x * (1.0 + jnp.tanh(0.7978845608028654 * (x + 0.044715 * (x * x * x))))


def _log_sigmoid(x):
    return jnp.minimum(x, 0.0) - jnp.log(1.0 + jnp.exp(-jnp.abs(x)))


def _dot(a, b):
    return jnp.dot(a, b, preferred_element_type=F32)


def _dot_nt(a, b):
    return lax.dot_general(a, b, (((1,), (1,)), ((), ())), preferred_element_type=F32)


def _dot_exact(a, b):
    return jnp.dot(a, b, preferred_element_type=F32, precision=lax.Precision.HIGHEST)


def _pipeline_step(tick):
    i = pl.program_id(0)
    last = pl.num_programs(0) - 1

    @pl.when(i == 0)
    def _():
        tick(0, True, False, False)
        tick(1, True, True, False)

    @pl.when(jnp.logical_and(i > 0, i < last))
    def _():
        tick(0, True, True, True)
        tick(1, True, True, True)

    @pl.when(i == last)
    def _():
        tick(0, False, True, True)
        tick(1, False, False, True)


def _const_spec(shape):
    zeros = (0,) * len(shape)
    return pl.BlockSpec(shape, lambda *_: zeros, pipeline_mode=pl.Buffered(1))


def _ada_kernel(c_ref, w_ref, b_ref, o_ref):
    c = c_ref[...]
    s = (c * _sigmoid(c)).astype(BF16)
    o_ref[...] = _dot(s, w_ref[...].astype(BF16)) + b_ref[...]


def _ada(cs, w_ada, b_ada, n):
    rows, d = cs.shape
    tn = 1024
    return pl.pallas_call(
        _ada_kernel,
        out_shape=jax.ShapeDtypeStruct((rows, n), F32),
        grid=(n // tn,),
        in_specs=[
            pl.BlockSpec((rows, d), lambda j: (0, 0)),
            pl.BlockSpec((d, tn), lambda j: (0, j)),
            pl.BlockSpec((1, tn), lambda j: (0, j)),
        ],
        out_specs=pl.BlockSpec((rows, tn), lambda j: (0, j)),
        compiler_params=pltpu.CompilerParams(
            dimension_semantics=("arbitrary",), vmem_limit_bytes=VMEM_LIMIT_BYTES),
        name="ada",
    )(cs, w_ada, b_ada)


def _inproj_norm(x, mod, g1, hb_ref):
    sh1 = mod[:, 0:D_MODEL]
    sc1 = mod[:, D_MODEL:2 * D_MODEL]
    hb_ref[...] = (_rms(x, g1) * (1.0 + sc1) + sh1).astype(BF16)


def _inproj_project(hb_ref, w, z):
    hb = hb_ref[...]
    z["u"][...] = _dot_nt(hb, w["u"][...])
    z["v"][...] = _dot_nt(hb, w["v"][...])
    z["qk"][...] = _dot_nt(hb, w["qk"][...])
    z["o"][...] = _dot_nt(hb, w["o"][...])
    z["vt"][...] = _dot_nt(w["vv"][...], hb).astype(BF16)
    z["gt"][...] = _dot_nt(w["gt"][...], hb)


def _inproj_finish(z, c, out, seg):
    tb = z["u"].shape[0]
    for g in range(GROUPS):
        gs = slice(g * HD, (g + 1) * HD)
        vg = _rms(_gelu_tanh(z["v"][:, gs]), c["gv"][:, gs]).astype(BF16)
        for ch in range(tb // CHUNK):
            cs = slice(ch * CHUNK, (ch + 1) * CHUNK)
            mixed = _dot(c["ws"][g], vg[cs]) + c["bs"][:, gs]
            out["a"][cs, gs] = (_gelu_tanh(z["u"][cs, gs]) * mixed).astype(BF16)

    zqk = z["qk"][...]
    pos = lax.broadcasted_iota(jnp.int32, (tb, 1), 0) % seg
    prev = jnp.where(pos != 0, pltpu.roll(zqk, 1, 0), 0.0)
    nxt = jnp.where(pos != seg - 1, pltpu.roll(zqk, tb - 1, 0), 0.0)
    cw = c["cw"]
    y = c["cb"][...] + prev * cw[0:1, :] + zqk * cw[1:2, :] + nxt * cw[2:3, :]
    y = y * _sigmoid(y)
    out["qt"][...] = jnp.transpose(y[:, :D_MLSTM]).astype(BF16)
    out["k"][...] = (y[:, D_MLSTM:] * (HD ** -0.5)).astype(BF16)
    out["vt"][...] = z["vt"][...]
    out["o"][...] = z["o"][...]

    assert tb == SCAN
    gt = z["gt"][...] + c["bgt"][...]
    gi = pltpu.roll(gt, HEADS, 0)
    lf = _log_sigmoid(gt)
    lane = lax.broadcasted_iota(jnp.int32, (N_GATE, HD), 1)
    tiles = []
    carry = jnp.zeros((N_GATE, HD), F32)
    for t in range(SCAN // HD):
        prefix = lf[:, t * HD:(t + 1) * HD]
        shift = 1
        while shift < HD:
            prefix = prefix + jnp.where(lane >= shift, pltpu.roll(prefix, shift, 1), 0.0)
            shift *= 2
        tiles.append(prefix + carry)
        carry = carry + jnp.broadcast_to(prefix[:, HD - 1:HD], (N_GATE, HD))
    prefix = jnp.concatenate(tiles, axis=1)
    b_last = jnp.concatenate([carry] * (SCAN // HD), axis=1)
    row_bwd = lax.broadcasted_iota(jnp.int32, (N_GATE, SCAN), 0) >= N_GATE // N_DIR
    b_row = jnp.where(row_bwd, b_last - prefix + lf, prefix)
    g_row = b_last - b_row + gi
    g_max = jnp.broadcast_to(jnp.max(g_row, axis=1, keepdims=True), (N_GATE, SCAN))
    for sec, stat in enumerate((b_row, g_row, b_last, g_max, gi - b_row)):
        out["gr"][sec * N_GATE:(sec + 1) * N_GATE, :] = stat


_INPROJ_Z = ("u", "v", "qk", "o", "vt", "gt")


def _inproj_kernel(*refs, seg, blocks_per_mod, mod_row0, n_cast, ada_tail):
    refs = list(refs)
    take = lambda k: [refs.pop(0) for _ in range(k)]
    (x_ref, mod_ref, g1_ref, wit_ref,
     bgt_ref, cw_ref, cb_ref, ws_ref, bs_ref, gv_ref) = take(10)
    cast_in = take(n_cast)
    ada_in = take(3) if ada_tail else None
    a_ref, qt_ref, k_ref, vt_ref, o_ref, gr_ref = take(6)
    cast_out = take(n_cast)
    ada_out = take(1)[0] if ada_tail else None
    hb_s, *z_s = refs
    tb = x_ref.shape[0] // 2
    mod_in = mod_row0 + jnp.minimum(pl.program_id(0), pl.num_programs(0) - 2) // blocks_per_mod
    w = {"u": wit_ref.at[0:OFF_V, :], "v": wit_ref.at[OFF_V:OFF_Q, :],
         "qk": wit_ref.at[OFF_Q:OFF_VV, :], "vv": wit_ref.at[OFF_VV:OFF_O, :],
         "o": wit_ref.at[OFF_O:OFF_G, :], "gt": wit_ref.at[OFF_G:D_IN, :]}
    c = {"gv": gv_ref, "ws": ws_ref, "bs": bs_ref, "cw": cw_ref, "cb": cb_ref, "bgt": bgt_ref}

    def tick(half, norm, project, finish):
        rows = slice(half * tb, (half + 1) * tb)
        cur, oth = half, 1 - half
        z_cur = {name: ref.at[cur] for name, ref in zip(_INPROJ_Z, z_s)}
        z_oth = {name: ref.at[oth] for name, ref in zip(_INPROJ_Z, z_s)}
        out = {"a": a_ref.at[rows, :], "qt": qt_ref.at[:, rows], "k": k_ref.at[rows, :],
               "vt": vt_ref.at[:, rows], "o": o_ref.at[rows, :], "gr": gr_ref.at[:, rows]}
        if project:
            _inproj_project(hb_s.at[oth], w, z_oth)
        if norm:
            _inproj_norm(x_ref[rows, :], mod_ref[pl.ds(mod_in, 1), :], g1_ref[...], hb_s.at[cur])
        if finish:
            _inproj_finish(z_cur, c, out, seg)
        if half == 0:
            for src, dst in zip(cast_in, cast_out):
                dst[...] = src[...].astype(BF16)
            if ada_tail:
                _ada_kernel(*ada_in, ada_out)

    _pipeline_step(tick)


def _inproj(x, mod, mod_row0, rows_per_mod, seg, p, cast=(), ada_tail=None):
    n = x.shape[0]
    tb = TOKEN_BLOCK
    tb2 = 2 * tb
    assert tb % seg == 0 and n % tb2 == 0 and rows_per_mod % tb2 == 0
    nb = n // tb2
    blk_in = lambda i: jnp.minimum(i, nb - 1)
    blk_out = lambda i: jnp.maximum(i - 1, 0)
    tok = lambda width: pl.BlockSpec((tb2, width), lambda i: (blk_out(i), 0))
    tok_t = lambda height: pl.BlockSpec((height, tb2), lambda i: (0, blk_out(i)))
    in_specs = [
        pl.BlockSpec((tb2, D_MODEL), lambda i: (blk_in(i), 0)),
        _const_spec(mod.shape),
        _const_spec((1, D_MODEL)),
        _const_spec((D_IN, D_MODEL)),
        _const_spec((N_GATE, 1)),
        _const_spec((3, 2 * D_MLSTM)),
        _const_spec((1, 2 * D_MLSTM)),
        _const_spec((GROUPS, CHUNK, CHUNK)),
        _const_spec((CHUNK, D_GMLP)),
        _const_spec((1, D_GMLP)),
    ]
    out_shape = [
        jax.ShapeDtypeStruct((n, D_GMLP), BF16),
        jax.ShapeDtypeStruct((D_MLSTM, n), BF16),
        jax.ShapeDtypeStruct((n, D_MLSTM), BF16),
        jax.ShapeDtypeStruct((D_MLSTM, n), BF16),
        jax.ShapeDtypeStruct((n, D_MLSTM), F32),
        jax.ShapeDtypeStruct((GR_ROWS, n), F32),
    ]
    out_specs = [tok(D_GMLP), tok_t(D_MLSTM), tok(D_MLSTM), tok_t(D_MLSTM), tok(D_MLSTM),
                 tok_t(GR_ROWS)]
    for wf in cast:
        rows = wf.shape[0] // nb
        assert rows * nb == wf.shape[0] and rows % 16 == 0
        slab = pl.BlockSpec((rows, wf.shape[1]), lambda i: (blk_in(i), 0))
        in_specs.append(slab)
        out_specs.append(slab)
        out_shape.append(jax.ShapeDtypeStruct(wf.shape, BF16))
    side = list(cast)
    if ada_tail is not None:
        cs, w_ada, b_ada, col0 = ada_tail
        width = (w_ada.shape[1] - col0) // nb
        assert col0 % width == 0 and width * nb == w_ada.shape[1] - col0 and width % 128 == 0
        in_specs += [
            _const_spec(cs.shape),
            pl.BlockSpec((w_ada.shape[0], width), lambda i: (0, col0 // width + blk_in(i))),
            pl.BlockSpec((1, width), lambda i: (0, col0 // width + blk_in(i))),
        ]
        out_specs.append(pl.BlockSpec((cs.shape[0], width), lambda i: (0, blk_in(i))))
        out_shape.append(jax.ShapeDtypeStruct((cs.shape[0], w_ada.shape[1] - col0), F32))
        side += [cs, w_ada, b_ada]
    scratch_shapes = [
        pltpu.VMEM((2, tb, D_MODEL), BF16),
        pltpu.VMEM((2, tb, D_GMLP), F32),
        pltpu.VMEM((2, tb, D_GMLP), F32),
        pltpu.VMEM((2, tb, 2 * D_MLSTM), F32),
        pltpu.VMEM((2, tb, D_MLSTM), F32),
        pltpu.VMEM((2, D_MLSTM, tb), BF16),
        pltpu.VMEM((2, N_GATE, tb), F32),
    ]
    return pl.pallas_call(
        functools.partial(_inproj_kernel, seg=seg, blocks_per_mod=rows_per_mod // tb2,
                          mod_row0=mod_row0, n_cast=len(cast), ada_tail=ada_tail is not None),
        out_shape=out_shape,
        grid=(nb + 1,),
        in_specs=in_specs,
        out_specs=out_specs,
        scratch_shapes=scratch_shapes,
        compiler_params=pltpu.CompilerParams(
            dimension_semantics=("arbitrary",), vmem_limit_bytes=VMEM_LIMIT_BYTES),
        name="inproj",
    )(x, mod, p["g1"], p["wit"],
      p["bgt"], p["cw"], p["cb"], p["ws"], p["bs"], p["gv"], *side)


def _lane_broadcast_column(row):
    n = row.shape[1]
    tiles = [jnp.transpose(jnp.broadcast_to(row[:, t * HD:(t + 1) * HD], (HD, HD)))
             for t in range(n // HD)]
    col = jnp.concatenate(tiles, axis=0)
    return jnp.concatenate([col] * (n // HD), axis=1)


def _mlstm_unit(k, vt, kq, qs, d_row, b_row, g_row, bl_row, gmax_row, st, m_row, mask, carried):
    logw = jnp.where(mask, _lane_broadcast_column(d_row) + b_row, NEG)
    a = b_row + m_row
    mj = jnp.maximum(a, jnp.max(logw, axis=0, keepdims=True))
    w = jnp.exp(logw - mj)
    s = kq * w
    num = _dot(vt, s.astype(BF16))
    den = jnp.sum(s, axis=0, keepdims=True)
    if carried:
        inter = jnp.exp(a - mj)
        num = num + inter * qs[:HD]
        den = den + inter * qs[HD:HD + 1]
    h = num * (1.0 / jnp.maximum(jnp.abs(den), jnp.exp(-mj)))
    m_new = jnp.maximum(bl_row + m_row, gmax_row)
    wc = jnp.exp(g_row - m_new)
    pad_row = lax.broadcasted_iota(jnp.int32, (ST_ROWS - HD, wc.shape[1]), 0)
    v_aug = jnp.concatenate(
        [vt.astype(F32) * wc, jnp.where(pad_row == 0, wc, 0.0)], axis=0).astype(BF16)
    st_new = _dot(v_aug, k)
    if carried:
        decay = jnp.exp(bl_row + m_row - m_new)
        st_new = decay[:, :HD] * st + st_new
    return h, st_new, m_new


def _mlstm_kernel(*refs, has_init, has_out, rounds, by_sequence):
    if has_init:
        s0_ref, m0_ref = refs[:2]
        refs = refs[2:]
    if by_sequence:
        k_ref, qt_ref, vt_ref, gr_ref = refs[:4]
        fwd_in = bwd_in = (k_ref, qt_ref, vt_ref, gr_ref)
        refs = refs[4:]
    else:
        fwd_in, bwd_in = refs[:4], refs[4:8]
        refs = refs[8:]
    hf_ref, hb_ref = refs[:2]
    refs = refs[2:]
    if has_out:
        co_ref, no_ref, mo_ref = refs[:3]
        refs = refs[3:]
    s_ref, m_ref = refs
    c = pl.program_id(1)
    nc = pl.num_programs(1)
    carried = not by_sequence

    if carried:
        @pl.when(c == 0)
        def _():
            if has_init:
                s_ref[...] = s0_ref[0]
                m_ref[...] = m0_ref[0]
            else:
                s_ref[...] = jnp.zeros_like(s_ref)
                m_ref[...] = jnp.zeros_like(m_ref)

    si = lax.broadcasted_iota(jnp.int32, (SCAN, SCAN), 0)
    ji = lax.broadcasted_iota(jnp.int32, (SCAN, SCAN), 1)
    dirs = ((fwd_in, hf_ref, si <= ji), (bwd_in, hb_ref, si >= ji))

    def chunk_cols(d, r):
        pos = rounds - 1 - r if (d == 1 and not by_sequence) else r
        return slice(pos * SCAN, (pos + 1) * SCAN)

    kq = {}
    for r in range(rounds):
        for d, ((k_ref, qt_ref, _, _), _, _) in enumerate(dirs):
            cs = chunk_cols(d, r)
            for hd in range(HEADS):
                hs = slice(hd * HD, (hd + 1) * HD)
                kq[r, d, hd] = _dot(k_ref[cs, hs], qt_ref[hs, cs])

    m_cur = None
    if carried:
        m_all = m_ref[...]
        m_cur = [m_all[u:u + 1] for u in range(N_UNIT)]
    zero_row = jnp.zeros((1, SCAN), F32)
    for r in range(rounds):
        slot = r * N_UNIT if by_sequence else 0
        for d, ((k_ref, qt_ref, vt_ref, gr_ref), h_ref, mask) in enumerate(dirs):
            cs = chunk_cols(d, r)
            for hd in range(HEADS):
                u = d * HEADS + hd
                row = d * 2 * HEADS + HEADS + hd
                hs = slice(hd * HD, (hd + 1) * HD)
                st = s_ref[slot + u] if carried else None
                qs = _dot(st.astype(BF16), qt_ref[hs, cs]) if carried else None
                h, st_new, m_new = _mlstm_unit(
                    k_ref[cs, hs], vt_ref[hs, cs], kq[r, d, hd], qs,
                    *(gr_ref[sec * N_GATE + row:sec * N_GATE + row + 1, cs]
                      for sec in (4, 0, 1, 2, 3)),
                    st, m_cur[u] if carried else zero_row, mask, carried)
                h_ref[hs, cs] = h
                s_ref[slot + u] = st_new
                if carried:
                    m_cur[u] = m_new
                else:
                    m_ref[slot + u:slot + u + 1, :] = m_new
    if carried:
        m_ref[...] = jnp.concatenate(m_cur, axis=0)

    if has_out:
        @pl.when(c == nc - 1)
        def _():
            for j in range(s_ref.shape[0]):
                q, u = divmod(j, N_UNIT)
                co_ref[q, u] = jnp.transpose(s_ref[j, 0:HD, :])
                no_ref[q, u:u + 1, :] = s_ref[j, HD:HD + 1, :]
                mo_ref[q, u:u + 1, :] = m_ref[j:j + 1, :]


def _mlstm(k, qt, vt, gr, batch, rounds, s0=None, m0=None, want_state=False):
    n = k.shape[0]
    nc = n // batch // SCAN
    by_sequence = nc == 1
    width = rounds * SCAN
    has_init = s0 is not None
    if by_sequence:
        assert batch % rounds == 0 and not has_init
        grid = (batch // rounds, 1)
        slots = rounds
        fwd = bwd = lambda b, c: b
    else:
        assert nc % rounds == 0 and not want_state
        steps = nc // rounds
        grid = (batch, steps)
        slots = 1
        fwd = lambda b, c: b * steps + c
        bwd = lambda b, c: b * steps + steps - 1 - c

    def specs(ix):
        return [
            pl.BlockSpec((width, D_MLSTM), lambda b, c: (ix(b, c), 0)),
            pl.BlockSpec((D_MLSTM, width), lambda b, c: (0, ix(b, c))),
            pl.BlockSpec((D_MLSTM, width), lambda b, c: (0, ix(b, c))),
            pl.BlockSpec((GR_ROWS, width), lambda b, c: (0, ix(b, c))),
        ]

    in_specs = specs(fwd) if by_sequence else specs(fwd) + specs(bwd)
    args = [k, qt, vt, gr] if by_sequence else [k, qt, vt, gr, k, qt, vt, gr]
    if has_init:
        in_specs = [
            pl.BlockSpec((1, N_UNIT, ST_ROWS, HD), lambda b, c: (b, 0, 0, 0)),
            pl.BlockSpec((1, N_UNIT, SCAN), lambda b, c: (b, 0, 0)),
        ] + in_specs
        args = [s0, m0] + args
    out_shape = [
        jax.ShapeDtypeStruct((D_MLSTM, n), F32),
        jax.ShapeDtypeStruct((D_MLSTM, n), F32),
    ]
    out_specs = [
        pl.BlockSpec((D_MLSTM, width), lambda b, c: (0, fwd(b, c))),
        pl.BlockSpec((D_MLSTM, width), lambda b, c: (0, bwd(b, c))),
    ]
    if want_state:
        out_shape += [
            jax.ShapeDtypeStruct((batch, N_UNIT, HD, HD), F32),
            jax.ShapeDtypeStruct((batch, N_UNIT, HD), F32),
            jax.ShapeDtypeStruct((batch, N_UNIT, SCAN), F32),
        ]
        out_specs += [
            pl.BlockSpec((slots, N_UNIT, HD, HD), lambda b, c: (b, 0, 0, 0)),
            pl.BlockSpec((slots, N_UNIT, HD), lambda b, c: (b, 0, 0)),
            pl.BlockSpec((slots, N_UNIT, SCAN), lambda b, c: (b, 0, 0)),
        ]
    return pl.pallas_call(
        functools.partial(_mlstm_kernel, has_init=has_init, has_out=want_state, rounds=rounds,
                          by_sequence=by_sequence),
        out_shape=out_shape,
        grid=grid,
        in_specs=in_specs,
        out_specs=out_specs,
        scratch_shapes=[pltpu.VMEM((slots * N_UNIT, ST_ROWS, HD), F32),
                        pltpu.VMEM((slots * N_UNIT, SCAN), F32)],
        compiler_params=pltpu.CompilerParams(
            dimension_semantics=("arbitrary", "arbitrary"), vmem_limit_bytes=VMEM_LIMIT_BYTES),
        name="mlstm",
    )(*args)


def _outffn_mix(x, a, hft, hbt, o, mod, c, x1_ref, h2_ref):
    ga1 = mod[:, 0:D_MODEL]
    sh2 = mod[:, D_MODEL:2 * D_MODEL]
    sc2 = mod[:, 2 * D_MODEL:3 * D_MODEL]
    hs = jnp.transpose(hft + hbt)
    sig = _sigmoid(o)
    parts = []
    for hd in range(HEADS):
        sl = slice(hd * HD, (hd + 1) * HD)
        parts.append(_rms(hs[:, sl], c["gh"][:, sl]) * sig[:, sl])
    b_out = jnp.concatenate(parts, axis=-1).astype(BF16)
    mix = _dot(a, c["wout"][0:D_GMLP, :]) + _dot(b_out, c["wout"][D_GMLP:, :])
    x1 = x + ga1 * mix
    x1_ref[...] = x1
    h2_ref[...] = (_rms(x1, c["g2"][...]) * (1.0 + sc2) + sh2).astype(BF16)


def _outffn_up(h2_ref, x1_in_ref, c, f_ref, x1_out_ref):
    h2 = h2_ref[...]
    u = _dot(h2, c["w1"][...])
    g = _dot(h2, c["w3"][...])
    f_ref[...] = (u * _sigmoid(u) * g).astype(BF16)
    x1_out_ref[...] = x1_in_ref[...]


def _outffn_down(f_ref, x1_ref, mod, c, y_ref):
    ga2 = mod[:, 3 * D_MODEL:4 * D_MODEL]
    x2 = x1_ref[...] + ga2 * _dot(f_ref[...], c["w2"][...])
    y_ref[...] = _rms(x2, c["gf"][...])


def _outffn_kernel(x_ref, a_ref, hft_ref, hbt_ref, o_ref, mod_ref, gh_ref, wout_hbm,
                   g2_ref, w1_hbm, w3_hbm, w2_hbm, gf_ref, y_ref,
                   x1a_s, h2_s, f_s, x1b_s, wout_s, w1_s, w3_s, w2_s, w_sem,
                   *, blocks_per_mod, mod_row0):
    tb = x_ref.shape[0] // 2
    i = pl.program_id(0)
    mod_in = mod_row0 + jnp.minimum(i, pl.num_programs(0) - 2) // blocks_per_mod
    mod_out = mod_row0 + jnp.maximum(i - 1, 0) // blocks_per_mod
    c = {"gh": gh_ref, "wout": wout_s, "g2": g2_ref, "w1": w1_s, "w3": w3_s, "w2": w2_s,
         "gf": gf_ref}
    weight_copy = {
        name: pltpu.make_async_copy(src, dst, w_sem.at[j])
        for j, (name, src, dst) in enumerate((("wout", wout_hbm, wout_s), ("w1", w1_hbm, w1_s),
                                              ("w3", w3_hbm, w3_s), ("w2", w2_hbm, w2_s)))}

    def tick(half, mix, up, down):
        rows = slice(half * tb, (half + 1) * tb)
        cur, oth = half, 1 - half
        first_tick = mix and not up and not down
        second_tick = mix and up and not down
        if first_tick:
            for name in ("wout", "w1", "w3", "w2"):
                weight_copy[name].start()
            weight_copy["wout"].wait()
        if down:
            _outffn_down(f_s.at[cur], x1b_s.at[cur], mod_ref[pl.ds(mod_out, 1), :], c,
                         y_ref.at[rows, :])
        if mix:
            _outffn_mix(x_ref[rows, :], a_ref[rows, :], hft_ref[:, rows], hbt_ref[:, rows],
                        o_ref[rows, :], mod_ref[pl.ds(mod_in, 1), :], c, x1a_s.at[cur],
                        h2_s.at[cur])
        if second_tick:
            weight_copy["w1"].wait()
            weight_copy["w3"].wait()
        if up:
            _outffn_up(h2_s.at[oth], x1a_s.at[oth], c, f_s.at[oth], x1b_s.at[oth])
        if second_tick:
            weight_copy["w2"].wait()

    _pipeline_step(tick)


def _outffn(x, a, hft, hbt, o, mod, mod_row0, rows_per_mod, p):
    n = x.shape[0]
    tb = TOKEN_BLOCK
    tb2 = 2 * tb
    assert n % tb2 == 0 and rows_per_mod % tb2 == 0
    nb = n // tb2
    d_ff = p["w1"].shape[1]
    blk_in = lambda i: jnp.minimum(i, nb - 1)
    blk_out = lambda i: jnp.maximum(i - 1, 0)
    tok = lambda w: pl.BlockSpec((tb2, w), lambda i: (blk_in(i), 0))
    tok_t = lambda h: pl.BlockSpec((h, tb2), lambda i: (0, blk_in(i)))
    weight_hbm = pl.BlockSpec(memory_space=pl.ANY)
    in_specs = [
        tok(D_MODEL), tok(D_GMLP), tok_t(D_MLSTM), tok_t(D_MLSTM), tok(D_MLSTM),
        _const_spec(mod.shape),
        _const_spec((1, D_MLSTM)),
        weight_hbm,
        _const_spec((1, D_MODEL)),
        weight_hbm,
        weight_hbm,
        weight_hbm,
        _const_spec((1, D_MODEL)),
    ]
    scratch_shapes = [
        pltpu.VMEM((2, tb, D_MODEL), F32),
        pltpu.VMEM((2, tb, D_MODEL), BF16),
        pltpu.VMEM((2, tb, d_ff), BF16),
        pltpu.VMEM((2, tb, D_MODEL), F32),
        pltpu.VMEM(p["wout"].shape, BF16),
        pltpu.VMEM(p["w1"].shape, BF16),
        pltpu.VMEM(p["w3"].shape, BF16),
        pltpu.VMEM(p["w2"].shape, BF16),
        pltpu.SemaphoreType.DMA((4,)),
    ]
    return pl.pallas_call(
        functools.partial(_outffn_kernel, blocks_per_mod=rows_per_mod // tb2, mod_row0=mod_row0),
        out_shape=jax.ShapeDtypeStruct((n, D_MODEL), F32),
        grid=(nb + 1,),
        in_specs=in_specs,
        out_specs=pl.BlockSpec((tb2, D_MODEL), lambda i: (blk_out(i), 0)),
        scratch_shapes=scratch_shapes,
        compiler_params=pltpu.CompilerParams(
            dimension_semantics=("arbitrary",), vmem_limit_bytes=VMEM_LIMIT_BYTES),
        name="outffn",
    )(x, a, hft, hbt, o, mod, p["gh"], p["wout"], p["g2"], p["w1"], p["w3"], p["w2"], p["gf"])


def _layer_params(l, g_norm1, w_in, b_gate, w_s, b_s, g_v, conv_w, conv_b, g_h, g_norm2, g_final):
    return {
        "g1": g_norm1[l][None, :],
        "wit": w_in[l].T.astype(BF16),
        "bgt": b_gate[l][:, None],
        "cw": conv_w[l], "cb": conv_b[l][None, :],
        "ws": w_s[l].astype(BF16),
        "bs": jnp.repeat(b_s[l].T, HD, axis=1),
        "gv": g_v[l].reshape(1, D_GMLP),
        "gh": g_h[l].reshape(1, D_MLSTM),
        "g2": g_norm2[l][None, :],
        "gf": g_final[None, :],
    }


_LATE_WEIGHTS = ("wout", "w1", "w3", "w2")


def _trunk(x, mod_row0, rows_per_mod, seg, batch, rounds, p, late=None, s0=None, m0=None,
           want_state=False):
    cast, ada_tail = ((), None) if late is None else (
        tuple(late[0][name] for name in _LATE_WEIGHTS), late[1])
    a, qt, k, vt, o, gr, *side = _inproj(x, p["mod_head"], mod_row0, rows_per_mod, seg, p, cast,
                                         ada_tail)
    if late is not None:
        p = dict(p, **dict(zip(_LATE_WEIGHTS, side[:-1])))
        p["mod_tail"] = side[-1]
    hft, hbt, *state = _mlstm(k, qt, vt, gr, batch, rounds, s0, m0, want_state)
    y = _outffn(x, a, hft, hbt, o, p["mod_tail"], mod_row0, rows_per_mod, p)
    return y, state, p


def kernel(x_prompt, x_sample, state_C, state_n, state_m, c, c_ctx, w_ada, b_ada, g_norm1, w_in,
           b_gate, w_s, b_s, g_v, conv_w, conv_b, g_h, w_out, g_norm2, w1, w3, w2, g_final):
    bp, tp, d = x_prompt.shape
    bs_, ts, _ = x_sample.shape
    depth = w_in.shape[0]
    assert depth == 1, "final norm is fused into the layer's last kernel"
    xp = x_prompt.reshape(bp * tp, d)
    xs = x_sample.reshape(bs_ * ts, d)

    cs = jnp.zeros((8, d), F32).at[0].set(c_ctx).at[1:1 + bs_].set(c)
    new_c, new_n, new_m = [], [], []
    for l in range(depth):
        p = _layer_params(l, g_norm1, w_in, b_gate, w_s, b_s, g_v, conv_w, conv_b, g_h, g_norm2,
                          g_final)
        late_f32 = {"wout": w_out[l], "w1": w1[l], "w3": w3[l], "w2": w2[l]}
        p["mod_head"] = _ada(cs, w_ada[l], b_ada[l][None, :], 2 * d)
        ada_tail = (cs, w_ada[l], b_ada[l][None, :], 2 * d)

        xp, (c_ctx_out, n_ctx_out, m_ctx_out), p = _trunk(
            xp, 0, bp * tp, tp, bp, CTX_SEQS_PER_STEP, p, (late_f32, ada_tail), want_state=True)
        new_c.append(c_ctx_out.reshape(bp, N_DIR, HEADS, HD, HD))
        new_n.append(n_ctx_out.reshape(bp, N_DIR, HEADS, HD))
        new_m.append(m_ctx_out[..., 0].reshape(bp, N_DIR, HEADS))

        s0 = jnp.concatenate(
            [jnp.swapaxes(state_C[:, l], -1, -2), state_n[:, l][..., None, :],
             jnp.zeros((bs_, N_DIR, HEADS, ST_ROWS - HD - 1, HD), F32)],
            axis=-2).reshape(bs_, N_UNIT, ST_ROWS, HD)
        m0 = jnp.broadcast_to(state_m[:, l].reshape(bs_, N_UNIT, 1), (bs_, N_UNIT, SCAN))
        xs, _, _ = _trunk(xs, 1, ts, GRID_W, bs_, LAT_CHUNKS_PER_STEP, p, s0=s0, m0=m0)

    return (xp.reshape(bp, tp, d), xs.reshape(bs_, ts, d),
            jnp.stack(new_c, axis=1), jnp.stack(new_n, axis=1), jnp.stack(new_m, axis=1))
```

```python
import functools

import jax
import jax.numpy as jnp
from jax import lax
from jax.experimental import pallas as pl
from jax.experimental.pallas import tpu as pltpu

D_MODEL = 1024
D_GMLP = 512
D_MLSTM = 512
GROUPS = 4
HEADS = 4
HD = 128
CHUNK = 128
SCAN = 256
N_DIR = 2
N_UNIT = N_DIR * HEADS
GRID_W = 64
OFF_V, OFF_Q, OFF_VV, OFF_O, OFF_G = 512, 1024, 2048, 2560, 3072
D_IN = OFF_G + 2 * N_DIR * HEADS
EPS = 1e-6
NEG = -1e30
ST_ROWS = HD + 16
N_GATE = 2 * N_UNIT
GR_ROWS = 5 * N_GATE

TOKEN_BLOCK = 256
CTX_SEQS_PER_STEP = 2
LAT_CHUNKS_PER_STEP = 2
VMEM_LIMIT_BYTES = 56 * 1024 * 1024

F32 = jnp.float32
BF16 = jnp.bfloat16


def _rms(x, g):
    return x * lax.rsqrt(jnp.mean(x * x, axis=-1, keepdims=True) + EPS) * g


def _sigmoid(x):
    return 1.0 / (1.0 + jnp.exp(-x))


def _gelu_tanh(x):
    return 0.5 * x * (1.0 + jnp.tanh(0.7978845608028654 * (x + 0.044715 * (x * x * x))))


def _log_sigmoid(x):
    return jnp.minimum(x, 0.0) - jnp.log(1.0 + jnp.exp(-jnp.abs(x)))


def _dot(a, b):
    return jnp.dot(a, b, preferred_element_type=F32)


def _dot_nt(a, b):
    return lax.dot_general(a, b, (((1,), (1,)), ((), ())), preferred_element_type=F32)


def _dot_exact(a, b):
    return jnp.dot(a, b, preferred_element_type=F32, precision=lax.Precision.HIGHEST)


def _pipeline_step(tick):
    i = pl.program_id(0)
    last = pl.num_programs(0) - 1

    @pl.when(i == 0)
    def _():
        tick(0, True, False, False)
        tick(1, True, True, False)

    @pl.when(jnp.logical_and(i > 0, i < last))
    def _():
        tick(0, True, True, True)
        tick(1, True, True, True)

    @pl.when(i == last)
    def _():
        tick(0, False, True, True)
        tick(1, False, False, True)


def _const_spec(shape):
    zeros = (0,) * len(shape)
    return pl.BlockSpec(shape, lambda *_: zeros, pipeline_mode=pl.Buffered(1))


def _ada_kernel(c_ref, w_ref, b_ref, o_ref):
    c = c_ref[...]
    s = (c * _sigmoid(c)).astype(BF16)
    o_ref[...] = _dot(s, w_ref[...].astype(BF16)) + b_ref[...]


def _ada(cs, w_ada, b_ada, n):
    rows, d = cs.shape
    tn = 1024
    return pl.pallas_call(
        _ada_kernel,
        out_shape=jax.ShapeDtypeStruct((rows, n), F32),
        grid=(n // tn,),
        in_specs=[
            pl.BlockSpec((rows, d), lambda j: (0, 0)),
            pl.BlockSpec((d, tn), lambda j: (0, j)),
            pl.BlockSpec((1, tn), lambda j: (0, j)),
        ],
        out_specs=pl.BlockSpec((rows, tn), lambda j: (0, j)),
        compiler_params=pltpu.CompilerParams(
            dimension_semantics=("arbitrary",), vmem_limit_bytes=VMEM_LIMIT_BYTES),
        name="ada",
    )(cs, w_ada, b_ada)


def _inproj_norm(x, mod, g1, hb_ref):
    sh1 = mod[:, 0:D_MODEL]
    sc1 = mod[:, D_MODEL:2 * D_MODEL]
    hb_ref[...] = (_rms(x, g1) * (1.0 + sc1) + sh1).astype(BF16)


def _inproj_project(hb_ref, w, z):
    hb = hb_ref[...]
    z["u"][...] = _dot_nt(hb, w["u"][...])
    z["v"][...] = _dot_nt(hb, w["v"][...])
    z["qk"][...] = _dot_nt(hb, w["qk"][...])
    z["o"][...] = _dot_nt(hb, w["o"][...])
    z["vt"][...] = _dot_nt(w["vv"][...], hb).astype(BF16)
    z["gt"][...] = _dot_nt(w["gt"][...], hb)


def _inproj_finish(z, c, out, seg):
    tb = z["u"].shape[0]
    for g in range(GROUPS):
        gs = slice(g * HD, (g + 1) * HD)
        vg = _rms(_gelu_tanh(z["v"][:, gs]), c["gv"][:, gs]).astype(BF16)
        for ch in range(tb // CHUNK):
            cs = slice(ch * CHUNK, (ch + 1) * CHUNK)
            mixed = _dot(c["ws"][g], vg[cs]) + c["bs"][:, gs]
            out["a"][cs, gs] = (_gelu_tanh(z["u"][cs, gs]) * mixed).astype(BF16)

    zqk = z["qk"][...]
    pos = lax.broadcasted_iota(jnp.int32, (tb, 1), 0) % seg
    prev = jnp.where(pos != 0, pltpu.roll(zqk, 1, 0), 0.0)
    nxt = jnp.where(pos != seg - 1, pltpu.roll(zqk, tb - 1, 0), 0.0)
    cw = c["cw"]
    y = c["cb"][...] + prev * cw[0:1, :] + zqk * cw[1:2, :] + nxt * cw[2:3, :]
    y = y * _sigmoid(y)
    out["qt"][...] = jnp.transpose(y[:, :D_MLSTM]).astype(BF16)
    out["k"][...] = (y[:, D_MLSTM:] * (HD ** -0.5)).astype(BF16)
    out["vt"][...] = z["vt"][...]
    out["o"][...] = z["o"][...]

    assert tb == SCAN
    gt = z["gt"][...] + c["bgt"][...]
    gi = pltpu.roll(gt, HEADS, 0)
    lf = _log_sigmoid(gt)
    lane = lax.broadcasted_iota(jnp.int32, (N_GATE, HD), 1)
    tiles = []
    carry = jnp.zeros((N_GATE, HD), F32)
    for t in range(SCAN // HD):
        prefix = lf[:, t * HD:(t + 1) * HD]
        shift = 1
        while shift < HD:
            prefix = prefix + jnp.where(lane >= shift, pltpu.roll(prefix, shift, 1), 0.0)
            shift *= 2
        tiles.append(prefix + carry)
        carry = carry + jnp.broadcast_to(prefix[:, HD - 1:HD], (N_GATE, HD))
    prefix = jnp.concatenate(tiles, axis=1)
    b_last = jnp.concatenate([carry] * (SCAN // HD), axis=1)
    row_bwd = lax.broadcasted_iota(jnp.int32, (N_GATE, SCAN), 0) >= N_GATE // N_DIR
    b_row = jnp.where(row_bwd, b_last - prefix + lf, prefix)
    g_row = b_last - b_row + gi
    g_max = jnp.broadcast_to(jnp.max(g_row, axis=1, keepdims=True), (N_GATE, SCAN))
    for sec, stat in enumerate((b_row, g_row, b_last, g_max, gi - b_row)):
        out["gr"][sec * N_GATE:(sec + 1) * N_GATE, :] = stat


_INPROJ_Z = ("u", "v", "qk", "o", "vt", "gt")


def _inproj_kernel(*refs, seg, blocks_per_mod, mod_row0, n_cast, ada_tail):
    refs = list(refs)
    take = lambda k: [refs.pop(0) for _ in range(k)]
    (x_ref, mod_ref, g1_ref, wit_ref,
     bgt_ref, cw_ref, cb_ref, ws_ref, bs_ref, gv_ref) = take(10)
    cast_in = take(n_cast)
    ada_in = take(3) if ada_tail else None
    a_ref, qt_ref, k_ref, vt_ref, o_ref, gr_ref = take(6)
    cast_out = take(n_cast)
    ada_out = take(1)[0] if ada_tail else None
    hb_s, *z_s = refs
    tb = x_ref.shape[0] // 2
    mod_in = mod_row0 + jnp.minimum(pl.program_id(0), pl.num_programs(0) - 2) // blocks_per_mod
    w = {"u": wit_ref.at[0:OFF_V, :], "v": wit_ref.at[OFF_V:OFF_Q, :],
         "qk": wit_ref.at[OFF_Q:OFF_VV, :], "vv": wit_ref.at[OFF_VV:OFF_O, :],
         "o": wit_ref.at[OFF_O:OFF_G, :], "gt": wit_ref.at[OFF_G:D_IN, :]}
    c = {"gv": gv_ref, "ws": ws_ref, "bs": bs_ref, "cw": cw_ref, "cb": cb_ref, "bgt": bgt_ref}

    def tick(half, norm, project, finish):
        rows = slice(half * tb, (half + 1) * tb)
        cur, oth = half, 1 - half
        z_cur = {name: ref.at[cur] for name, ref in zip(_INPROJ_Z, z_s)}
        z_oth = {name: ref.at[oth] for name, ref in zip(_INPROJ_Z, z_s)}
        out = {"a": a_ref.at[rows, :], "qt": qt_ref.at[:, rows], "k": k_ref.at[rows, :],
               "vt": vt_ref.at[:, rows], "o": o_ref.at[rows, :], "gr": gr_ref.at[:, rows]}
        if project:
            _inproj_project(hb_s.at[oth], w, z_oth)
        if norm:
            _inproj_norm(x_ref[rows, :], mod_ref[pl.ds(mod_in, 1), :], g1_ref[...], hb_s.at[cur])
        if finish:
            _inproj_finish(z_cur, c, out, seg)
        if half == 0:
            for src, dst in zip(cast_in, cast_out):
                dst[...] = src[...].astype(BF16)
            if ada_tail:
                _ada_kernel(*ada_in, ada_out)

    _pipeline_step(tick)


def _inproj(x, mod, mod_row0, rows_per_mod, seg, p, cast=(), ada_tail=None):
    n = x.shape[0]
    tb = TOKEN_BLOCK
    tb2 = 2 * tb
    assert tb % seg == 0 and n % tb2 == 0 and rows_per_mod % tb2 == 0
    nb = n // tb2
    blk_in = lambda i: jnp.minimum(i, nb - 1)
    blk_out = lambda i: jnp.maximum(i - 1, 0)
    tok = lambda width: pl.BlockSpec((tb2, width), lambda i: (blk_out(i), 0))
    tok_t = lambda height: pl.BlockSpec((height, tb2), lambda i: (0, blk_out(i)))
    in_specs = [
        pl.BlockSpec((tb2, D_MODEL), lambda i: (blk_in(i), 0)),
        _const_spec(mod.shape),
        _const_spec((1, D_MODEL)),
        _const_spec((D_IN, D_MODEL)),
        _const_spec((N_GATE, 1)),
        _const_spec((3, 2 * D_MLSTM)),
        _const_spec((1, 2 * D_MLSTM)),
        _const_spec((GROUPS, CHUNK, CHUNK)),
        _const_spec((CHUNK, D_GMLP)),
        _const_spec((1, D_GMLP)),
    ]
    out_shape = [
        jax.ShapeDtypeStruct((n, D_GMLP), BF16),
        jax.ShapeDtypeStruct((D_MLSTM, n), BF16),
        jax.ShapeDtypeStruct((n, D_MLSTM), BF16),
        jax.ShapeDtypeStruct((D_MLSTM, n), BF16),
        jax.ShapeDtypeStruct((n, D_MLSTM), F32),
        jax.ShapeDtypeStruct((GR_ROWS, n), F32),
    ]
    out_specs = [tok(D_GMLP), tok_t(D_MLSTM), tok(D_MLSTM), tok_t(D_MLSTM), tok(D_MLSTM),
                 tok_t(GR_ROWS)]
    for wf in cast:
        rows = wf.shape[0] // nb
        assert rows * nb == wf.shape[0] and rows % 16 == 0
        slab = pl.BlockSpec((rows, wf.shape[1]), lambda i: (blk_in(i), 0))
        in_specs.append(slab)
        out_specs.append(slab)
        out_shape.append(jax.ShapeDtypeStruct(wf.shape, BF16))
    side = list(cast)
    if ada_tail is not None:
        cs, w_ada, b_ada, col0 = ada_tail
        width = (w_ada.shape[1] - col0) // nb
        assert col0 % width == 0 and width * nb == w_ada.shape[1] - col0 and width % 128 == 0
        in_specs += [
            _const_spec(cs.shape),
            pl.BlockSpec((w_ada.shape[0], width), lambda i: (0, col0 // width + blk_in(i))),
            pl.BlockSpec((1, width), lambda i: (0, col0 // width + blk_in(i))),
        ]
        out_specs.append(pl.BlockSpec((cs.shape[0], width), lambda i: (0, blk_in(i))))
        out_shape.append(jax.ShapeDtypeStruct((cs.shape[0], w_ada.shape[1] - col0), F32))
        side += [cs, w_ada, b_ada]
    scratch_shapes = [
        pltpu.VMEM((2, tb, D_MODEL), BF16),
        pltpu.VMEM((2, tb, D_GMLP), F32),
        pltpu.VMEM((2, tb, D_GMLP), F32),
        pltpu.VMEM((2, tb, 2 * D_MLSTM), F32),
        pltpu.VMEM((2, tb, D_MLSTM), F32),
        pltpu.VMEM((2, D_MLSTM, tb), BF16),
        pltpu.VMEM((2, N_GATE, tb), F32),
    ]
    return pl.pallas_call(
        functools.partial(_inproj_kernel, seg=seg, blocks_per_mod=rows_per_mod // tb2,
                          mod_row0=mod_row0, n_cast=len(cast), ada_tail=ada_tail is not None),
        out_shape=out_shape,
        grid=(nb + 1,),
        in_specs=in_specs,
        out_specs=out_specs,
        scratch_shapes=scratch_shapes,
        compiler_params=pltpu.CompilerParams(
            dimension_semantics=("arbitrary",), vmem_limit_bytes=VMEM_LIMIT_BYTES),
        name="inproj",
    )(x, mod, p["g1"], p["wit"],
      p["bgt"], p["cw"], p["cb"], p["ws"], p["bs"], p["gv"], *side)


def _lane_broadcast_column(row):
    n = row.shape[1]
    tiles = [jnp.transpose(jnp.broadcast_to(row[:, t * HD:(t + 1) * HD], (HD, HD)))
             for t in range(n // HD)]
    col = jnp.concatenate(tiles, axis=0)
    return jnp.concatenate([col] * (n // HD), axis=1)


def _mlstm_unit(k, vt, kq, qs, d_row, b_row, g_row, bl_row, gmax_row, st, m_row, mask, carried):
    logw = jnp.where(mask, _lane_broadcast_column(d_row) + b_row, NEG)
    a = b_row + m_row
    mj = jnp.maximum(a, jnp.max(logw, axis=0, keepdims=True))
    w = jnp.exp(logw - mj)
    s = kq * w
    num = _dot(vt, s.astype(BF16))
    den = jnp.sum(s, axis=0, keepdims=True)
    if carried:
        inter = jnp.exp(a - mj)
        num = num + inter * qs[:HD]
        den = den + inter * qs[HD:HD + 1]
    h = num * (1.0 / jnp.maximum(jnp.abs(den), jnp.exp(-mj)))
    m_new = jnp.maximum(bl_row + m_row, gmax_row)
    wc = jnp.exp(g_row - m_new)
    pad_row = lax.broadcasted_iota(jnp.int32, (ST_ROWS - HD, wc.shape[1]), 0)
    v_aug = jnp.concatenate(
        [vt.astype(F32) * wc, jnp.where(pad_row == 0, wc, 0.0)], axis=0).astype(BF16)
    st_new = _dot(v_aug, k)
    if carried:
        decay = jnp.exp(bl_row + m_row - m_new)
        st_new = decay[:, :HD] * st + st_new
    return h, st_new, m_new


def _mlstm_kernel(*refs, has_init, has_out, rounds, by_sequence):
    if has_init:
        s0_ref, m0_ref = refs[:2]
        refs = refs[2:]
    if by_sequence:
        k_ref, qt_ref, vt_ref, gr_ref = refs[:4]
        fwd_in = bwd_in = (k_ref, qt_ref, vt_ref, gr_ref)
        refs = refs[4:]
    else:
        fwd_in, bwd_in = refs[:4], refs[4:8]
        refs = refs[8:]
    hf_ref, hb_ref = refs[:2]
    refs = refs[2:]
    if has_out:
        co_ref, no_ref, mo_ref = refs[:3]
        refs = refs[3:]
    s_ref, m_ref = refs
    c = pl.program_id(1)
    nc = pl.num_programs(1)
    carried = not by_sequence

    if carried:
        @pl.when(c == 0)
        def _():
            if has_init:
                s_ref[...] = s0_ref[0]
                m_ref[...] = m0_ref[0]
            else:
                s_ref[...] = jnp.zeros_like(s_ref)
                m_ref[...] = jnp.zeros_like(m_ref)

    si = lax.broadcasted_iota(jnp.int32, (SCAN, SCAN), 0)
    ji = lax.broadcasted_iota(jnp.int32, (SCAN, SCAN), 1)
    dirs = ((fwd_in, hf_ref, si <= ji), (bwd_in, hb_ref, si >= ji))

    def chunk_cols(d, r):
        pos = rounds - 1 - r if (d == 1 and not by_sequence) else r
        return slice(pos * SCAN, (pos + 1) * SCAN)

    kq = {}
    for r in range(rounds):
        for d, ((k_ref, qt_ref, _, _), _, _) in enumerate(dirs):
            cs = chunk_cols(d, r)
            for hd in range(HEADS):
                hs = slice(hd * HD, (hd + 1) * HD)
                kq[r, d, hd] = _dot(k_ref[cs, hs], qt_ref[hs, cs])

    m_cur = None
    if carried:
        m_all = m_ref[...]
        m_cur = [m_all[u:u + 1] for u in range(N_UNIT)]
    zero_row = jnp.zeros((1, SCAN), F32)
    for r in range(rounds):
        slot = r * N_UNIT if by_sequence else 0
        for d, ((k_ref, qt_ref, vt_ref, gr_ref), h_ref, mask) in enumerate(dirs):
            cs = chunk_cols(d, r)
            for hd in range(HEADS):
                u = d * HEADS + hd
                row = d * 2 * HEADS + HEADS + hd
                hs = slice(hd * HD, (hd + 1) * HD)
                st = s_ref[slot + u] if carried else None
                qs = _dot(st.astype(BF16), qt_ref[hs, cs]) if carried else None
                h, st_new, m_new = _mlstm_unit(
                    k_ref[cs, hs], vt_ref[hs, cs], kq[r, d, hd], qs,
                    *(gr_ref[sec * N_GATE + row:sec * N_GATE + row + 1, cs]
                      for sec in (4, 0, 1, 2, 3)),
                    st, m_cur[u] if carried else zero_row, mask, carried)
                h_ref[hs, cs] = h
                s_ref[slot + u] = st_new
                if carried:
                    m_cur[u] = m_new
                else:
                    m_ref[slot + u:slot + u + 1, :] = m_new
    if carried:
        m_ref[...] = jnp.concatenate(m_cur, axis=0)

    if has_out:
        @pl.when(c == nc - 1)
        def _():
            for j in range(s_ref.shape[0]):
                q, u = divmod(j, N_UNIT)
                co_ref[q, u] = jnp.transpose(s_ref[j, 0:HD, :])
                no_ref[q, u:u + 1, :] = s_ref[j, HD:HD + 1, :]
                mo_ref[q, u:u + 1, :] = m_ref[j:j + 1, :]


def _mlstm(k, qt, vt, gr, batch, rounds, s0=None, m0=None, want_state=False):
    n = k.shape[0]
    nc = n // batch // SCAN
    by_sequence = nc == 1
    width = rounds * SCAN
    has_init = s0 is not None
    if by_sequence:
        assert batch % rounds == 0 and not has_init
        grid = (batch // rounds, 1)
        slots = rounds
        fwd = bwd = lambda b, c: b
    else:
        assert nc % rounds == 0 and not want_state
        steps = nc // rounds
        grid = (batch, steps)
        slots = 1
        fwd = lambda b, c: b * steps + c
        bwd = lambda b, c: b * steps + steps - 1 - c

    def specs(ix):
        return [
            pl.BlockSpec((width, D_MLSTM), lambda b, c: (ix(b, c), 0)),
            pl.BlockSpec((D_MLSTM, width), lambda b, c: (0, ix(b, c))),
            pl.BlockSpec((D_MLSTM, width), lambda b, c: (0, ix(b, c))),
            pl.BlockSpec((GR_ROWS, width), lambda b, c: (0, ix(b, c))),
        ]

    in_specs = specs(fwd) if by_sequence else specs(fwd) + specs(bwd)
    args = [k, qt, vt, gr] if by_sequence else [k, qt, vt, gr, k, qt, vt, gr]
    if has_init:
        in_specs = [
            pl.BlockSpec((1, N_UNIT, ST_ROWS, HD), lambda b, c: (b, 0, 0, 0)),
            pl.BlockSpec((1, N_UNIT, SCAN), lambda b, c: (b, 0, 0)),
        ] + in_specs
        args = [s0, m0] + args
    out_shape = [
        jax.ShapeDtypeStruct((D_MLSTM, n), F32),
        jax.ShapeDtypeStruct((D_MLSTM, n), F32),
    ]
    out_specs = [
        pl.BlockSpec((D_MLSTM, width), lambda b, c: (0, fwd(b, c))),
        pl.BlockSpec((D_MLSTM, width), lambda b, c: (0, bwd(b, c))),
    ]
    if want_state:
        out_shape += [
            jax.ShapeDtypeStruct((batch, N_UNIT, HD, HD), F32),
            jax.ShapeDtypeStruct((batch, N_UNIT, HD), F32),
            jax.ShapeDtypeStruct((batch, N_UNIT, SCAN), F32),
        ]
        out_specs += [
            pl.BlockSpec((slots, N_UNIT, HD, HD), lambda b, c: (b, 0, 0, 0)),
            pl.BlockSpec((slots, N_UNIT, HD), lambda b, c: (b, 0, 0)),
            pl.BlockSpec((slots, N_UNIT, SCAN), lambda b, c: (b, 0, 0)),
        ]
    return pl.pallas_call(
        functools.partial(_mlstm_kernel, has_init=has_init, has_out=want_state, rounds=rounds,
                          by_sequence=by_sequence),
        out_shape=out_shape,
        grid=grid,
        in_specs=in_specs,
        out_specs=out_specs,
        scratch_shapes=[pltpu.VMEM((slots * N_UNIT, ST_ROWS, HD), F32),
                        pltpu.VMEM((slots * N_UNIT, SCAN), F32)],
        compiler_params=pltpu.CompilerParams(
            dimension_semantics=("arbitrary", "arbitrary"), vmem_limit_bytes=VMEM_LIMIT_BYTES),
        name="mlstm",
    )(*args)


def _outffn_mix(x, a, hft, hbt, o, mod, c, x1_ref, h2_ref):
    ga1 = mod[:, 0:D_MODEL]
    sh2 = mod[:, D_MODEL:2 * D_MODEL]
    sc2 = mod[:, 2 * D_MODEL:3 * D_MODEL]
    hs = jnp.transpose(hft + hbt)
    sig = _sigmoid(o)
    parts = []
    for hd in range(HEADS):
        sl = slice(hd * HD, (hd + 1) * HD)
        parts.append(_rms(hs[:, sl], c["gh"][:, sl]) * sig[:, sl])
    b_out = jnp.concatenate(parts, axis=-1).astype(BF16)
    mix = _dot(a, c["wout"][0:D_GMLP, :]) + _dot(b_out, c["wout"][D_GMLP:, :])
    x1 = x + ga1 * mix
    x1_ref[...] = x1
    h2_ref[...] = (_rms(x1, c["g2"][...]) * (1.0 + sc2) + sh2).astype(BF16)


def _outffn_up(h2_ref, x1_in_ref, c, f_ref, x1_out_ref):
    h2 = h2_ref[...]
    u = _dot(h2, c["w1"][...])
    g = _dot(h2, c["w3"][...])
    f_ref[...] = (u * _sigmoid(u) * g).astype(BF16)
    x1_out_ref[...] = x1_in_ref[...]


def _outffn_down(f_ref, x1_ref, mod, c, y_ref):
    ga2 = mod[:, 3 * D_MODEL:4 * D_MODEL]
    x2 = x1_ref[...] + ga2 * _dot(f_ref[...], c["w2"][...])
    y_ref[...] = _rms(x2, c["gf"][...])


def _outffn_kernel(x_ref, a_ref, hft_ref, hbt_ref, o_ref, mod_ref, gh_ref, wout_hbm,
                   g2_ref, w1_hbm, w3_hbm, w2_hbm, gf_ref, y_ref,
                   x1a_s, h2_s, f_s, x1b_s, wout_s, w1_s, w3_s, w2_s, w_sem,
                   *, blocks_per_mod, mod_row0):
    tb = x_ref.shape[0] // 2
    i = pl.program_id(0)
    mod_in = mod_row0 + jnp.minimum(i, pl.num_programs(0) - 2) // blocks_per_mod
    mod_out = mod_row0 + jnp.maximum(i - 1, 0) // blocks_per_mod
    c = {"gh": gh_ref, "wout": wout_s, "g2": g2_ref, "w1": w1_s, "w3": w3_s, "w2": w2_s,
         "gf": gf_ref}
    weight_copy = {
        name: pltpu.make_async_copy(src, dst, w_sem.at[j])
        for j, (name, src, dst) in enumerate((("wout", wout_hbm, wout_s), ("w1", w1_hbm, w1_s),
                                              ("w3", w3_hbm, w3_s), ("w2", w2_hbm, w2_s)))}

    def tick(half, mix, up, down):
        rows = slice(half * tb, (half + 1) * tb)
        cur, oth = half, 1 - half
        first_tick = mix and not up and not down
        second_tick = mix and up and not down
        if first_tick:
            for name in ("wout", "w1", "w3", "w2"):
                weight_copy[name].start()
            weight_copy["wout"].wait()
        if down:
            _outffn_down(f_s.at[cur], x1b_s.at[cur], mod_ref[pl.ds(mod_out, 1), :], c,
                         y_ref.at[rows, :])
        if mix:
            _outffn_mix(x_ref[rows, :], a_ref[rows, :], hft_ref[:, rows], hbt_ref[:, rows],
                        o_ref[rows, :], mod_ref[pl.ds(mod_in, 1), :], c, x1a_s.at[cur],
                        h2_s.at[cur])
        if second_tick:
            weight_copy["w1"].wait()
            weight_copy["w3"].wait()
        if up:
            _outffn_up(h2_s.at[oth], x1a_s.at[oth], c, f_s.at[oth], x1b_s.at[oth])
        if second_tick:
            weight_copy["w2"].wait()

    _pipeline_step(tick)


def _outffn(x, a, hft, hbt, o, mod, mod_row0, rows_per_mod, p):
    n = x.shape[0]
    tb = TOKEN_BLOCK
    tb2 = 2 * tb
    assert n % tb2 == 0 and rows_per_mod % tb2 == 0
    nb = n // tb2
    d_ff = p["w1"].shape[1]
    blk_in = lambda i: jnp.minimum(i, nb - 1)
    blk_out = lambda i: jnp.maximum(i - 1, 0)
    tok = lambda w: pl.BlockSpec((tb2, w), lambda i: (blk_in(i), 0))
    tok_t = lambda h: pl.BlockSpec((h, tb2), lambda i: (0, blk_in(i)))
    weight_hbm = pl.BlockSpec(memory_space=pl.ANY)
    in_specs = [
        tok(D_MODEL), tok(D_GMLP), tok_t(D_MLSTM), tok_t(D_MLSTM), tok(D_MLSTM),
        _const_spec(mod.shape),
        _const_spec((1, D_MLSTM)),
        weight_hbm,
        _const_spec((1, D_MODEL)),
        weight_hbm,
        weight_hbm,
        weight_hbm,
        _const_spec((1, D_MODEL)),
    ]
    scratch_shapes = [
        pltpu.VMEM((2, tb, D_MODEL), F32),
        pltpu.VMEM((2, tb, D_MODEL), BF16),
        pltpu.VMEM((2, tb, d_ff), BF16),
        pltpu.VMEM((2, tb, D_MODEL), F32),
        pltpu.VMEM(p["wout"].shape, BF16),
        pltpu.VMEM(p["w1"].shape, BF16),
        pltpu.VMEM(p["w3"].shape, BF16),
        pltpu.VMEM(p["w2"].shape, BF16),
        pltpu.SemaphoreType.DMA((4,)),
    ]
    return pl.pallas_call(
        functools.partial(_outffn_kernel, blocks_per_mod=rows_per_mod // tb2, mod_row0=mod_row0),
        out_shape=jax.ShapeDtypeStruct((n, D_MODEL), F32),
        grid=(nb + 1,),
        in_specs=in_specs,
        out_specs=pl.BlockSpec((tb2, D_MODEL), lambda i: (blk_out(i), 0)),
        scratch_shapes=scratch_shapes,
        compiler_params=pltpu.CompilerParams(
            dimension_semantics=("arbitrary",), vmem_limit_bytes=VMEM_LIMIT_BYTES),
        name="outffn",
    )(x, a, hft, hbt, o, mod, p["gh"], p["wout"], p["g2"], p["w1"], p["w3"], p["w2"], p["gf"])


def _layer_params(l, g_norm1, w_in, b_gate, w_s, b_s, g_v, conv_w, conv_b, g_h, g_norm2, g_final):
    return {
        "g1": g_norm1[l][None, :],
        "wit": w_in[l].T.astype(BF16),
        "bgt": b_gate[l][:, None],
        "cw": conv_w[l], "cb": conv_b[l][None, :],
        "ws": w_s[l].astype(BF16),
        "bs": jnp.repeat(b_s[l].T, HD, axis=1),
        "gv": g_v[l].reshape(1, D_GMLP),
        "gh": g_h[l].reshape(1, D_MLSTM),
        "g2": g_norm2[l][None, :],
        "gf": g_final[None, :],
    }


_LATE_WEIGHTS = ("wout", "w1", "w3", "w2")


def _trunk_front(x, mod_row0, rows_per_mod, seg, p, late=None):
    cast, ada_tail = ((), None) if late is None else (
        tuple(late[0][name] for name in _LATE_WEIGHTS), late[1])
    a, qt, k, vt, o, gr, *side = _inproj(x, p["mod_head"], mod_row0, rows_per_mod, seg, p, cast,
                                         ada_tail)
    if late is not None:
        p = dict(p, **dict(zip(_LATE_WEIGHTS, side[:-1])))
        p["mod_tail"] = side[-1]
    return (a, qt, k, vt, o), gr, p


def _trunk_back(x, front, gr, mod_row0, rows_per_mod, batch, rounds, p, s0=None, m0=None,
                want_state=False):
    a, qt, k, vt, o = front
    hft, hbt, *state = _mlstm(k, qt, vt, gr, batch, rounds, s0, m0, want_state)
    y = _outffn(x, a, hft, hbt, o, p["mod_tail"], mod_row0, rows_per_mod, p)
    return y, state


def kernel(x_prompt, x_sample, state_C, state_n, state_m, c, c_ctx, w_ada, b_ada, g_norm1, w_in,
           b_gate, w_s, b_s, g_v, conv_w, conv_b, g_h, w_out, g_norm2, w1, w3, w2, g_final):
    bp, tp, d = x_prompt.shape
    bs_, ts, _ = x_sample.shape
    depth = w_in.shape[0]
    assert depth == 1, "final norm is fused into the layer's last kernel"
    xp = x_prompt.reshape(bp * tp, d)
    xs = x_sample.reshape(bs_ * ts, d)

    cs = jnp.zeros((8, d), F32).at[0].set(c_ctx).at[1:1 + bs_].set(c)
    new_c, new_n, new_m = [], [], []
    for l in range(depth):
        p = _layer_params(l, g_norm1, w_in, b_gate, w_s, b_s, g_v, conv_w, conv_b, g_h, g_norm2,
                          g_final)
        late_f32 = {"wout": w_out[l], "w1": w1[l], "w3": w3[l], "w2": w2[l]}
        p["mod_head"] = _ada(cs, w_ada[l], b_ada[l][None, :], 2 * d)
        ada_tail = (cs, w_ada[l], b_ada[l][None, :], 2 * d)

        front_lat, gr_lat, p = _trunk_front(xs, 1, ts, GRID_W, p, (late_f32, ada_tail))
        front_ctx, gr_ctx, _ = _trunk_front(xp, 0, bp * tp, tp, p)

        xp, (c_ctx_out, n_ctx_out, m_ctx_out) = _trunk_back(
            xp, front_ctx, gr_ctx, 0, bp * tp, bp, CTX_SEQS_PER_STEP, p, want_state=True)
        new_c.append(c_ctx_out.reshape(bp, N_DIR, HEADS, HD, HD))
        new_n.append(n_ctx_out.reshape(bp, N_DIR, HEADS, HD))
        new_m.append(m_ctx_out[..., 0].reshape(bp, N_DIR, HEADS))

        s0 = jnp.concatenate(
            [jnp.swapaxes(state_C[:, l], -1, -2), state_n[:, l][..., None, :],
             jnp.zeros((bs_, N_DIR, HEADS, ST_ROWS - HD - 1, HD), F32)],
            axis=-2).reshape(bs_, N_UNIT, ST_ROWS, HD)
        m0 = jnp.broadcast_to(state_m[:, l].reshape(bs_, N_UNIT, 1), (bs_, N_UNIT, SCAN))
        xs, _ = _trunk_back(xs, front_lat, gr_lat, 1, ts, bs_, LAT_CHUNKS_PER_STEP, p, s0=s0, m0=m0)

    return (xp.reshape(bp, tp, d), xs.reshape(bs_, ts, d),
            jnp.stack(new_c, axis=1), jnp.stack(new_n, axis=1), jnp.stack(new_m, axis=1))
```

```python
import functools

import jax
import jax.numpy as jnp
from jax import lax
from jax.experimental import pallas as pl
from jax.experimental.pallas import tpu as pltpu

D_MODEL = 1024
D_GMLP = 512
D_MLSTM = 512
GROUPS = 4
HEADS = 4
HD = 128
CHUNK = 128
SCAN = 256
N_DIR = 2
N_UNIT = N_DIR * HEADS
GRID_W = 64
OFF_V, OFF_Q, OFF_VV, OFF_O, OFF_G = 512, 1024, 2048, 2560, 3072
D_IN = OFF_G + 2 * N_DIR * HEADS
EPS = 1e-6
NEG = -1e30
ST_ROWS = HD + 16
N_GATE = 2 * N_UNIT
GR_ROWS = 5 * N_GATE

TOKEN_BLOCK = 256
CTX_SEQS_PER_STEP = 2
LAT_CHUNKS_PER_STEP = 2
VMEM_LIMIT_BYTES = 56 * 1024 * 1024

F32 = jnp.float32
BF16 = jnp.bfloat16


def _rms(x, g):
    return x * lax.rsqrt(jnp.mean(x * x, axis=-1, keepdims=True) + EPS) * g


LOG2E = 1.4426950408889634
GELU_K = 0.7978845608028654
GELU_C = 0.044715


def _sigmoid(x):
    return 1.0 / (1.0 + jnp.exp2(x * (-LOG2E)))


def _gelu_tanh(x):
    t = (x * x) * (-2.0 * LOG2E * GELU_K * GELU_C) + (-2.0 * LOG2E * GELU_K)
    return x * (1.0 / (1.0 + jnp.exp2(x * t)))


def _log_sigmoid(x):
    return jnp.minimum(x, 0.0) - jnp.log(1.0 + jnp.exp(-jnp.abs(x)))


def _dot(a, b):
    return jnp.dot(a, b, preferred_element_type=F32)


def _dot_nt(a, b):
    return lax.dot_general(a, b, (((1,), (1,)), ((), ())), preferred_element_type=F32)


def _dot_exact(a, b):
    return jnp.dot(a, b, preferred_element_type=F32, precision=lax.Precision.HIGHEST)


def _pipeline_step(tick):
    i = pl.program_id(0)
    last = pl.num_programs(0) - 1

    @pl.when(i == 0)
    def _():
        tick(0, True, False, False)
        tick(1, True, True, False)

    @pl.when(jnp.logical_and(i > 0, i < last))
    def _():
        tick(0, True, True, True)
        tick(1, True, True, True)

    @pl.when(i == last)
    def _():
        tick(0, False, True, True)
        tick(1, False, False, True)


def _const_spec(shape):
    zeros = (0,) * len(shape)
    return pl.BlockSpec(shape, lambda *_: zeros, pipeline_mode=pl.Buffered(1))


def _ada_kernel(c_ref, w_ref, b_ref, o_ref):
    c = c_ref[...]
    s = (c * _sigmoid(c)).astype(BF16)
    o_ref[...] = _dot(s, w_ref[...].astype(BF16)) + b_ref[...]


def _ada(cs, w_ada, b_ada, n):
    rows, d = cs.shape
    tn = 1024
    return pl.pallas_call(
        _ada_kernel,
        out_shape=jax.ShapeDtypeStruct((rows, n), F32),
        grid=(n // tn,),
        in_specs=[
            pl.BlockSpec((rows, d), lambda j: (0, 0)),
            pl.BlockSpec((d, tn), lambda j: (0, j)),
            pl.BlockSpec((1, tn), lambda j: (0, j)),
        ],
        out_specs=pl.BlockSpec((rows, tn), lambda j: (0, j)),
        compiler_params=pltpu.CompilerParams(
            dimension_semantics=("arbitrary",), vmem_limit_bytes=VMEM_LIMIT_BYTES),
        name="ada",
    )(cs, w_ada, b_ada)


def _inproj_norm(x, mod, g1, hb_ref):
    sh1 = mod[:, 0:D_MODEL]
    sc1 = mod[:, D_MODEL:2 * D_MODEL]
    hb_ref[...] = (_rms(x, g1 * (1.0 + sc1)) + sh1).astype(BF16)


def _inproj_project(hb_ref, w, z):
    hb = hb_ref[...]
    z["u"][...] = _dot_nt(hb, w["u"][...])
    z["v"][...] = _dot_nt(hb, w["v"][...])
    z["qk"][...] = _dot_nt(hb, w["qk"][...])
    z["o"][...] = _dot_nt(hb, w["o"][...])
    z["vt"][...] = _dot_nt(w["vv"][...], hb).astype(BF16)
    z["gt"][...] = _dot_nt(w["gt"][...], hb)


def _inproj_finish(z, c, out, seg):
    tb = z["u"].shape[0]
    for g in range(GROUPS):
        gs = slice(g * HD, (g + 1) * HD)
        vg = _rms(_gelu_tanh(z["v"][:, gs]), c["gv"][:, gs]).astype(BF16)
        for ch in range(tb // CHUNK):
            cs = slice(ch * CHUNK, (ch + 1) * CHUNK)
            mixed = _dot(c["ws"][g], vg[cs]) + c["bs"][:, gs]
            out["a"][cs, gs] = (_gelu_tanh(z["u"][cs, gs]) * mixed).astype(BF16)

    zqk = z["qk"][...]
    pos = lax.broadcasted_iota(jnp.int32, (tb, 1), 0) % seg
    prev = jnp.where(pos != 0, pltpu.roll(zqk, 1, 0), 0.0)
    nxt = jnp.where(pos != seg - 1, pltpu.roll(zqk, tb - 1, 0), 0.0)
    cw = c["cw"]
    y = c["cb"][...] + prev * cw[0:1, :] + zqk * cw[1:2, :] + nxt * cw[2:3, :]
    y = y * _sigmoid(y)
    out["qt"][...] = jnp.transpose(y[:, :D_MLSTM]).astype(BF16)
    out["k"][...] = (y[:, D_MLSTM:] * (HD ** -0.5)).astype(BF16)
    out["vt"][...] = z["vt"][...]
    out["o"][...] = z["o"][...]

    assert tb == SCAN
    gt = z["gt"][...] + c["bgt"][...]
    gi = pltpu.roll(gt, HEADS, 0)
    lf = _log_sigmoid(gt)
    lane = lax.broadcasted_iota(jnp.int32, (N_GATE, HD), 1)
    tiles = []
    carry = jnp.zeros((N_GATE, HD), F32)
    for t in range(SCAN // HD):
        prefix = lf[:, t * HD:(t + 1) * HD]
        shift = 1
        while shift < HD:
            prefix = prefix + jnp.where(lane >= shift, pltpu.roll(prefix, shift, 1), 0.0)
            shift *= 2
        tiles.append(prefix + carry)
        carry = carry + jnp.broadcast_to(prefix[:, HD - 1:HD], (N_GATE, HD))
    prefix = jnp.concatenate(tiles, axis=1)
    b_last = jnp.concatenate([carry] * (SCAN // HD), axis=1)
    row_bwd = lax.broadcasted_iota(jnp.int32, (N_GATE, SCAN), 0) >= N_GATE // N_DIR
    b_row = jnp.where(row_bwd, b_last - prefix + lf, prefix)
    g_row = b_last - b_row + gi
    g_max = jnp.broadcast_to(jnp.max(g_row, axis=1, keepdims=True), (N_GATE, SCAN))
    for sec, stat in enumerate((b_row, g_row, b_last, g_max, gi - b_row)):
        out["gr"][sec * N_GATE:(sec + 1) * N_GATE, :] = stat


_INPROJ_Z = ("u", "v", "qk", "o", "vt", "gt")


def _inproj_kernel(*refs, seg, blocks_per_mod, mod_row0, n_cast, ada_tail):
    refs = list(refs)
    take = lambda k: [refs.pop(0) for _ in range(k)]
    (x_ref, mod_ref, g1_ref, wit_ref,
     bgt_ref, cw_ref, cb_ref, ws_ref, bs_ref, gv_ref) = take(10)
    cast_in = take(n_cast)
    ada_in = take(3) if ada_tail else None
    a_ref, qt_ref, k_ref, vt_ref, o_ref, gr_ref = take(6)
    cast_out = take(n_cast)
    ada_out = take(1)[0] if ada_tail else None
    hb_s, *z_s = refs
    tb = x_ref.shape[0] // 2
    mod_in = mod_row0 + jnp.minimum(pl.program_id(0), pl.num_programs(0) - 2) // blocks_per_mod
    w = {"u": wit_ref.at[0:OFF_V, :], "v": wit_ref.at[OFF_V:OFF_Q, :],
         "qk": wit_ref.at[OFF_Q:OFF_VV, :], "vv": wit_ref.at[OFF_VV:OFF_O, :],
         "o": wit_ref.at[OFF_O:OFF_G, :], "gt": wit_ref.at[OFF_G:D_IN, :]}
    c = {"gv": gv_ref, "ws": ws_ref, "bs": bs_ref, "cw": cw_ref, "cb": cb_ref, "bgt": bgt_ref}

    def tick(half, norm, project, finish):
        rows = slice(half * tb, (half + 1) * tb)
        cur, oth = half, 1 - half
        z_cur = {name: ref.at[cur] for name, ref in zip(_INPROJ_Z, z_s)}
        z_oth = {name: ref.at[oth] for name, ref in zip(_INPROJ_Z, z_s)}
        out = {"a": a_ref.at[rows, :], "qt": qt_ref.at[:, rows], "k": k_ref.at[rows, :],
               "vt": vt_ref.at[:, rows], "o": o_ref.at[rows, :], "gr": gr_ref.at[:, rows]}
        if project:
            _inproj_project(hb_s.at[oth], w, z_oth)
        if norm:
            _inproj_norm(x_ref[rows, :], mod_ref[pl.ds(mod_in, 1), :], g1_ref[...], hb_s.at[cur])
        if finish:
            _inproj_finish(z_cur, c, out, seg)
        if half == 0:
            for src, dst in zip(cast_in, cast_out):
                dst[...] = src[...].astype(BF16)
            if ada_tail:
                _ada_kernel(*ada_in, ada_out)

    _pipeline_step(tick)


def _inproj(x, mod, mod_row0, rows_per_mod, seg, p, cast=(), ada_tail=None):
    n = x.shape[0]
    tb = TOKEN_BLOCK
    tb2 = 2 * tb
    assert tb % seg == 0 and n % tb2 == 0 and rows_per_mod % tb2 == 0
    nb = n // tb2
    blk_in = lambda i: jnp.minimum(i, nb - 1)
    blk_out = lambda i: jnp.maximum(i - 1, 0)
    tok = lambda width: pl.BlockSpec((tb2, width), lambda i: (blk_out(i), 0))
    tok_t = lambda height: pl.BlockSpec((height, tb2), lambda i: (0, blk_out(i)))
    in_specs = [
        pl.BlockSpec((tb2, D_MODEL), lambda i: (blk_in(i), 0)),
        _const_spec(mod.shape),
        _const_spec((1, D_MODEL)),
        _const_spec((D_IN, D_MODEL)),
        _const_spec((N_GATE, 1)),
        _const_spec((3, 2 * D_MLSTM)),
        _const_spec((1, 2 * D_MLSTM)),
        _const_spec((GROUPS, CHUNK, CHUNK)),
        _const_spec((CHUNK, D_GMLP)),
        _const_spec((1, D_GMLP)),
    ]
    out_shape = [
        jax.ShapeDtypeStruct((n, D_GMLP), BF16),
        jax.ShapeDtypeStruct((D_MLSTM, n), BF16),
        jax.ShapeDtypeStruct((n, D_MLSTM), BF16),
        jax.ShapeDtypeStruct((D_MLSTM, n), BF16),
        jax.ShapeDtypeStruct((n, D_MLSTM), F32),
        jax.ShapeDtypeStruct((GR_ROWS, n), F32),
    ]
    out_specs = [tok(D_GMLP), tok_t(D_MLSTM), tok(D_MLSTM), tok_t(D_MLSTM), tok(D_MLSTM),
                 tok_t(GR_ROWS)]
    for wf in cast:
        rows = wf.shape[0] // nb
        assert rows * nb == wf.shape[0] and rows % 16 == 0
        slab = pl.BlockSpec((rows, wf.shape[1]), lambda i: (blk_in(i), 0))
        in_specs.append(slab)
        out_specs.append(slab)
        out_shape.append(jax.ShapeDtypeStruct(wf.shape, BF16))
    side = list(cast)
    if ada_tail is not None:
        cs, w_ada, b_ada, col0 = ada_tail
        width = (w_ada.shape[1] - col0) // nb
        assert col0 % width == 0 and width * nb == w_ada.shape[1] - col0 and width % 128 == 0
        in_specs += [
            _const_spec(cs.shape),
            pl.BlockSpec((w_ada.shape[0], width), lambda i: (0, col0 // width + blk_in(i))),
            pl.BlockSpec((1, width), lambda i: (0, col0 // width + blk_in(i))),
        ]
        out_specs.append(pl.BlockSpec((cs.shape[0], width), lambda i: (0, blk_in(i))))
        out_shape.append(jax.ShapeDtypeStruct((cs.shape[0], w_ada.shape[1] - col0), F32))
        side += [cs, w_ada, b_ada]
    scratch_shapes = [
        pltpu.VMEM((2, tb, D_MODEL), BF16),
        pltpu.VMEM((2, tb, D_GMLP), F32),
        pltpu.VMEM((2, tb, D_GMLP), F32),
        pltpu.VMEM((2, tb, 2 * D_MLSTM), F32),
        pltpu.VMEM((2, tb, D_MLSTM), F32),
        pltpu.VMEM((2, D_MLSTM, tb), BF16),
        pltpu.VMEM((2, N_GATE, tb), F32),
    ]
    return pl.pallas_call(
        functools.partial(_inproj_kernel, seg=seg, blocks_per_mod=rows_per_mod // tb2,
                          mod_row0=mod_row0, n_cast=len(cast), ada_tail=ada_tail is not None),
        out_shape=out_shape,
        grid=(nb + 1,),
        in_specs=in_specs,
        out_specs=out_specs,
        scratch_shapes=scratch_shapes,
        compiler_params=pltpu.CompilerParams(
            dimension_semantics=("arbitrary",), vmem_limit_bytes=VMEM_LIMIT_BYTES),
        name="inproj",
    )(x, mod, p["g1"], p["wit"],
      p["bgt"], p["cw"], p["cb"], p["ws"], p["bs"], p["gv"], *side)


def _lane_broadcast_column(row):
    n = row.shape[1]
    tiles = [jnp.transpose(jnp.broadcast_to(row[:, t * HD:(t + 1) * HD], (HD, HD)))
             for t in range(n // HD)]
    col = jnp.concatenate(tiles, axis=0)
    return jnp.concatenate([col] * (n // HD), axis=1)


def _mlstm_unit(k, vt, kq, qs, d_row, b_row, g_row, bl_row, gmax_row, st, m_row, mask, carried):
    logw = jnp.where(mask, _lane_broadcast_column(d_row) + b_row, NEG)
    a = b_row + m_row
    mj = jnp.maximum(a, jnp.max(logw, axis=0, keepdims=True))
    w = jnp.exp(logw - mj)
    s = kq * w
    num = _dot(vt, s.astype(BF16))
    den = jnp.sum(s, axis=0, keepdims=True)
    if carried:
        inter = jnp.exp(a - mj)
        num = num + inter * qs[:HD]
        den = den + inter * qs[HD:HD + 1]
    h = num * (1.0 / jnp.maximum(jnp.abs(den), jnp.exp(-mj)))
    m_new = jnp.maximum(bl_row + m_row, gmax_row)
    wc = jnp.exp(g_row - m_new)
    pad_row = lax.broadcasted_iota(jnp.int32, (ST_ROWS - HD, wc.shape[1]), 0)
    v_aug = jnp.concatenate(
        [vt.astype(F32) * wc, jnp.where(pad_row == 0, wc, 0.0)], axis=0).astype(BF16)
    st_new = _dot(v_aug, k)
    if carried:
        decay = jnp.exp(bl_row + m_row - m_new)
        st_new = decay[:, :HD] * st + st_new
    return h, st_new, m_new


def _mlstm_kernel(*refs, has_init, has_out, rounds, by_sequence):
    if has_init:
        s0_ref, m0_ref = refs[:2]
        refs = refs[2:]
    if by_sequence:
        k_ref, qt_ref, vt_ref, gr_ref = refs[:4]
        fwd_in = bwd_in = (k_ref, qt_ref, vt_ref, gr_ref)
        refs = refs[4:]
    else:
        fwd_in, bwd_in = refs[:4], refs[4:8]
        refs = refs[8:]
    hf_ref, hb_ref = refs[:2]
    refs = refs[2:]
    if has_out:
        co_ref, no_ref, mo_ref = refs[:3]
        refs = refs[3:]
    s_ref, m_ref = refs
    c = pl.program_id(1)
    nc = pl.num_programs(1)
    carried = not by_sequence

    if carried:
        @pl.when(c == 0)
        def _():
            if has_init:
                s_ref[...] = s0_ref[0]
                m_ref[...] = m0_ref[0]
            else:
                s_ref[...] = jnp.zeros_like(s_ref)
                m_ref[...] = jnp.zeros_like(m_ref)

    si = lax.broadcasted_iota(jnp.int32, (SCAN, SCAN), 0)
    ji = lax.broadcasted_iota(jnp.int32, (SCAN, SCAN), 1)
    dirs = ((fwd_in, hf_ref, si <= ji), (bwd_in, hb_ref, si >= ji))

    def chunk_cols(d, r):
        pos = rounds - 1 - r if (d == 1 and not by_sequence) else r
        return slice(pos * SCAN, (pos + 1) * SCAN)

    kq = {}
    for r in range(rounds):
        for d, ((k_ref, qt_ref, _, _), _, _) in enumerate(dirs):
            cs = chunk_cols(d, r)
            for hd in range(HEADS):
                hs = slice(hd * HD, (hd + 1) * HD)
                kq[r, d, hd] = _dot(k_ref[cs, hs], qt_ref[hs, cs])

    m_cur = None
    if carried:
        m_all = m_ref[...]
        m_cur = [m_all[u:u + 1] for u in range(N_UNIT)]
    zero_row = jnp.zeros((1, SCAN), F32)
    for r in range(rounds):
        slot = r * N_UNIT if by_sequence else 0
        for d, ((k_ref, qt_ref, vt_ref, gr_ref), h_ref, mask) in enumerate(dirs):
            cs = chunk_cols(d, r)
            for hd in range(HEADS):
                u = d * HEADS + hd
                row = d * 2 * HEADS + HEADS + hd
                hs = slice(hd * HD, (hd + 1) * HD)
                st = s_ref[slot + u] if carried else None
                qs = _dot(st.astype(BF16), qt_ref[hs, cs]) if carried else None
                h, st_new, m_new = _mlstm_unit(
                    k_ref[cs, hs], vt_ref[hs, cs], kq[r, d, hd], qs,
                    *(gr_ref[sec * N_GATE + row:sec * N_GATE + row + 1, cs]
                      for sec in (4, 0, 1, 2, 3)),
                    st, m_cur[u] if carried else zero_row, mask, carried)
                h_ref[hs, cs] = h
                s_ref[slot + u] = st_new
                if carried:
                    m_cur[u] = m_new
                else:
                    m_ref[slot + u:slot + u + 1, :] = m_new
    if carried:
        m_ref[...] = jnp.concatenate(m_cur, axis=0)

    if has_out:
        @pl.when(c == nc - 1)
        def _():
            for j in range(s_ref.shape[0]):
                q, u = divmod(j, N_UNIT)
                co_ref[q, u] = jnp.transpose(s_ref[j, 0:HD, :])
                no_ref[q, u:u + 1, :] = s_ref[j, HD:HD + 1, :]
                mo_ref[q, u:u + 1, :] = m_ref[j:j + 1, :]


def _mlstm(k, qt, vt, gr, batch, rounds, s0=None, m0=None, want_state=False):
    n = k.shape[0]
    nc = n // batch // SCAN
    by_sequence = nc == 1
    width = rounds * SCAN
    has_init = s0 is not None
    if by_sequence:
        assert batch % rounds == 0 and not has_init
        grid = (batch // rounds, 1)
        slots = rounds
        fwd = bwd = lambda b, c: b
    else:
        assert nc % rounds == 0 and not want_state
        steps = nc // rounds
        grid = (batch, steps)
        slots = 1
        fwd = lambda b, c: b * steps + c
        bwd = lambda b, c: b * steps + steps - 1 - c

    def specs(ix):
        return [
            pl.BlockSpec((width, D_MLSTM), lambda b, c: (ix(b, c), 0)),
            pl.BlockSpec((D_MLSTM, width), lambda b, c: (0, ix(b, c))),
            pl.BlockSpec((D_MLSTM, width), lambda b, c: (0, ix(b, c))),
            pl.BlockSpec((GR_ROWS, width), lambda b, c: (0, ix(b, c))),
        ]

    in_specs = specs(fwd) if by_sequence else specs(fwd) + specs(bwd)
    args = [k, qt, vt, gr] if by_sequence else [k, qt, vt, gr, k, qt, vt, gr]
    if has_init:
        in_specs = [
            pl.BlockSpec((1, N_UNIT, ST_ROWS, HD), lambda b, c: (b, 0, 0, 0)),
            pl.BlockSpec((1, N_UNIT, SCAN), lambda b, c: (b, 0, 0)),
        ] + in_specs
        args = [s0, m0] + args
    out_shape = [
        jax.ShapeDtypeStruct((D_MLSTM, n), F32),
        jax.ShapeDtypeStruct((D_MLSTM, n), F32),
    ]
    out_specs = [
        pl.BlockSpec((D_MLSTM, width), lambda b, c: (0, fwd(b, c))),
        pl.BlockSpec((D_MLSTM, width), lambda b, c: (0, bwd(b, c))),
    ]
    if want_state:
        out_shape += [
            jax.ShapeDtypeStruct((batch, N_UNIT, HD, HD), F32),
            jax.ShapeDtypeStruct((batch, N_UNIT, HD), F32),
            jax.ShapeDtypeStruct((batch, N_UNIT, SCAN), F32),
        ]
        out_specs += [
            pl.BlockSpec((slots, N_UNIT, HD, HD), lambda b, c: (b, 0, 0, 0)),
            pl.BlockSpec((slots, N_UNIT, HD), lambda b, c: (b, 0, 0)),
            pl.BlockSpec((slots, N_UNIT, SCAN), lambda b, c: (b, 0, 0)),
        ]
    return pl.pallas_call(
        functools.partial(_mlstm_kernel, has_init=has_init, has_out=want_state, rounds=rounds,
                          by_sequence=by_sequence),
        out_shape=out_shape,
        grid=grid,
        in_specs=in_specs,
        out_specs=out_specs,
        scratch_shapes=[pltpu.VMEM((slots * N_UNIT, ST_ROWS, HD), F32),
                        pltpu.VMEM((slots * N_UNIT, SCAN), F32)],
        compiler_params=pltpu.CompilerParams(
            dimension_semantics=("arbitrary", "arbitrary"), vmem_limit_bytes=VMEM_LIMIT_BYTES),
        name="mlstm",
    )(*args)


def _outffn_mix(x, a, hft, hbt, o, mod, c, x1_ref, h2_ref):
    ga1 = mod[:, 0:D_MODEL]
    sh2 = mod[:, D_MODEL:2 * D_MODEL]
    sc2 = mod[:, 2 * D_MODEL:3 * D_MODEL]
    hs = jnp.transpose(hft + hbt)
    sig = _sigmoid(o)
    parts = []
    for hd in range(HEADS):
        sl = slice(hd * HD, (hd + 1) * HD)
        parts.append(_rms(hs[:, sl], c["gh"][:, sl]) * sig[:, sl])
    b_out = jnp.concatenate(parts, axis=-1).astype(BF16)
    mix = _dot(a, c["wout"][0:D_GMLP, :]) + _dot(b_out, c["wout"][D_GMLP:, :])
    x1 = x + ga1 * mix
    x1_ref[...] = x1
    h2_ref[...] = (_rms(x1, c["g2"][...] * (1.0 + sc2)) + sh2).astype(BF16)


def _outffn_up(h2_ref, x1_in_ref, c, f_ref, x1_out_ref):
    h2 = h2_ref[...]
    u = _dot(h2, c["w1"][...])
    g = _dot(h2, c["w3"][...])
    f_ref[...] = (u * _sigmoid(u) * g).astype(BF16)
    x1_out_ref[...] = x1_in_ref[...]


def _outffn_down(f_ref, x1_ref, mod, c, y_ref):
    ga2 = mod[:, 3 * D_MODEL:4 * D_MODEL]
    x2 = x1_ref[...] + ga2 * _dot(f_ref[...], c["w2"][...])
    y_ref[...] = _rms(x2, c["gf"][...])


def _outffn_kernel(x_ref, a_ref, hft_ref, hbt_ref, o_ref, mod_ref, gh_ref, wout_hbm,
                   g2_ref, w1_hbm, w3_hbm, w2_hbm, gf_ref, y_ref,
                   x1a_s, h2_s, f_s, x1b_s, wout_s, w1_s, w3_s, w2_s, w_sem,
                   *, blocks_per_mod, mod_row0):
    tb = x_ref.shape[0] // 2
    i = pl.program_id(0)
    mod_in = mod_row0 + jnp.minimum(i, pl.num_programs(0) - 2) // blocks_per_mod
    mod_out = mod_row0 + jnp.maximum(i - 1, 0) // blocks_per_mod
    c = {"gh": gh_ref, "wout": wout_s, "g2": g2_ref, "w1": w1_s, "w3": w3_s, "w2": w2_s,
         "gf": gf_ref}
    weight_copy = {
        name: pltpu.make_async_copy(src, dst, w_sem.at[j])
        for j, (name, src, dst) in enumerate((("wout", wout_hbm, wout_s), ("w1", w1_hbm, w1_s),
                                              ("w3", w3_hbm, w3_s), ("w2", w2_hbm, w2_s)))}

    def tick(half, mix, up, down):
        rows = slice(half * tb, (half + 1) * tb)
        cur, oth = half, 1 - half
        first_tick = mix and not up and not down
        second_tick = mix and up and not down
        if first_tick:
            for name in ("wout", "w1", "w3", "w2"):
                weight_copy[name].start()
            weight_copy["wout"].wait()
        if down:
            _outffn_down(f_s.at[cur], x1b_s.at[cur], mod_ref[pl.ds(mod_out, 1), :], c,
                         y_ref.at[rows, :])
        if mix:
            _outffn_mix(x_ref[rows, :], a_ref[rows, :], hft_ref[:, rows], hbt_ref[:, rows],
                        o_ref[rows, :], mod_ref[pl.ds(mod_in, 1), :], c, x1a_s.at[cur],
                        h2_s.at[cur])
        if second_tick:
            weight_copy["w1"].wait()
            weight_copy["w3"].wait()
        if up:
            _outffn_up(h2_s.at[oth], x1a_s.at[oth], c, f_s.at[oth], x1b_s.at[oth])
        if second_tick:
            weight_copy["w2"].wait()

    _pipeline_step(tick)


def _outffn(x, a, hft, hbt, o, mod, mod_row0, rows_per_mod, p):
    n = x.shape[0]
    tb = TOKEN_BLOCK
    tb2 = 2 * tb
    assert n % tb2 == 0 and rows_per_mod % tb2 == 0
    nb = n // tb2
    d_ff = p["w1"].shape[1]
    blk_in = lambda i: jnp.minimum(i, nb - 1)
    blk_out = lambda i: jnp.maximum(i - 1, 0)
    tok = lambda w: pl.BlockSpec((tb2, w), lambda i: (blk_in(i), 0))
    tok_t = lambda h: pl.BlockSpec((h, tb2), lambda i: (0, blk_in(i)))
    weight_hbm = pl.BlockSpec(memory_space=pl.ANY)
    in_specs = [
        tok(D_MODEL), tok(D_GMLP), tok_t(D_MLSTM), tok_t(D_MLSTM), tok(D_MLSTM),
        _const_spec(mod.shape),
        _const_spec((1, D_MLSTM)),
        weight_hbm,
        _const_spec((1, D_MODEL)),
        weight_hbm,
        weight_hbm,
        weight_hbm,
        _const_spec((1, D_MODEL)),
    ]
    scratch_shapes = [
        pltpu.VMEM((2, tb, D_MODEL), F32),
        pltpu.VMEM((2, tb, D_MODEL), BF16),
        pltpu.VMEM((2, tb, d_ff), BF16),
        pltpu.VMEM((2, tb, D_MODEL), F32),
        pltpu.VMEM(p["wout"].shape, BF16),
        pltpu.VMEM(p["w1"].shape, BF16),
        pltpu.VMEM(p["w3"].shape, BF16),
        pltpu.VMEM(p["w2"].shape, BF16),
        pltpu.SemaphoreType.DMA((4,)),
    ]
    return pl.pallas_call(
        functools.partial(_outffn_kernel, blocks_per_mod=rows_per_mod // tb2, mod_row0=mod_row0),
        out_shape=jax.ShapeDtypeStruct((n, D_MODEL), F32),
        grid=(nb + 1,),
        in_specs=in_specs,
        out_specs=pl.BlockSpec((tb2, D_MODEL), lambda i: (blk_out(i), 0)),
        scratch_shapes=scratch_shapes,
        compiler_params=pltpu.CompilerParams(
            dimension_semantics=("arbitrary",), vmem_limit_bytes=VMEM_LIMIT_BYTES),
        name="outffn",
    )(x, a, hft, hbt, o, mod, p["gh"], p["wout"], p["g2"], p["w1"], p["w3"], p["w2"], p["gf"])


def _layer_params(l, g_norm1, w_in, b_gate, w_s, b_s, g_v, conv_w, conv_b, g_h, g_norm2, g_final):
    return {
        "g1": g_norm1[l][None, :],
        "wit": w_in[l].T.astype(BF16),
        "bgt": b_gate[l][:, None],
        "cw": conv_w[l], "cb": conv_b[l][None, :],
        "ws": w_s[l].astype(BF16),
        "bs": jnp.repeat(b_s[l].T, HD, axis=1),
        "gv": g_v[l].reshape(1, D_GMLP),
        "gh": g_h[l].reshape(1, D_MLSTM),
        "g2": g_norm2[l][None, :],
        "gf": g_final[None, :],
    }


_LATE_WEIGHTS = ("wout", "w1", "w3", "w2")


def _trunk_front(x, mod_row0, rows_per_mod, seg, p, late=None):
    cast, ada_tail = ((), None) if late is None else (
        tuple(late[0][name] for name in _LATE_WEIGHTS), late[1])
    a, qt, k, vt, o, gr, *side = _inproj(x, p["mod_head"], mod_row0, rows_per_mod, seg, p, cast,
                                         ada_tail)
    if late is not None:
        p = dict(p, **dict(zip(_LATE_WEIGHTS, side[:-1])))
        p["mod_tail"] = side[-1]
    return (a, qt, k, vt, o), gr, p


def _trunk_back(x, front, gr, mod_row0, rows_per_mod, batch, rounds, p, s0=None, m0=None,
                want_state=False):
    a, qt, k, vt, o = front
    hft, hbt, *state = _mlstm(k, qt, vt, gr, batch, rounds, s0, m0, want_state)
    y = _outffn(x, a, hft, hbt, o, p["mod_tail"], mod_row0, rows_per_mod, p)
    return y, state


def kernel(x_prompt, x_sample, state_C, state_n, state_m, c, c_ctx, w_ada, b_ada, g_norm1, w_in,
           b_gate, w_s, b_s, g_v, conv_w, conv_b, g_h, w_out, g_norm2, w1, w3, w2, g_final):
    bp, tp, d = x_prompt.shape
    bs_, ts, _ = x_sample.shape
    depth = w_in.shape[0]
    assert depth == 1, "final norm is fused into the layer's last kernel"
    xp = x_prompt.reshape(bp * tp, d)
    xs = x_sample.reshape(bs_ * ts, d)

    cs = jnp.zeros((8, d), F32).at[0].set(c_ctx).at[1:1 + bs_].set(c)
    new_c, new_n, new_m = [], [], []
    for l in range(depth):
        p = _layer_params(l, g_norm1, w_in, b_gate, w_s, b_s, g_v, conv_w, conv_b, g_h, g_norm2,
                          g_final)
        late_f32 = {"wout": w_out[l], "w1": w1[l], "w3": w3[l], "w2": w2[l]}
        p["mod_head"] = _ada(cs, w_ada[l], b_ada[l][None, :], 2 * d)
        ada_tail = (cs, w_ada[l], b_ada[l][None, :], 2 * d)

        front_lat, gr_lat, p = _trunk_front(xs, 1, ts, GRID_W, p, (late_f32, ada_tail))
        front_ctx, gr_ctx, _ = _trunk_front(xp, 0, bp * tp, tp, p)

        xp, (c_ctx_out, n_ctx_out, m_ctx_out) = _trunk_back(
            xp, front_ctx, gr_ctx, 0, bp * tp, bp, CTX_SEQS_PER_STEP, p, want_state=True)
        new_c.append(c_ctx_out.reshape(bp, N_DIR, HEADS, HD, HD))
        new_n.append(n_ctx_out.reshape(bp, N_DIR, HEADS, HD))
        new_m.append(m_ctx_out[..., 0].reshape(bp, N_DIR, HEADS))

        s0 = jnp.concatenate(
            [jnp.swapaxes(state_C[:, l], -1, -2), state_n[:, l][..., None, :],
             jnp.zeros((bs_, N_DIR, HEADS, ST_ROWS - HD - 1, HD), F32)],
            axis=-2).reshape(bs_, N_UNIT, ST_ROWS, HD)
        m0 = jnp.broadcast_to(state_m[:, l].reshape(bs_, N_UNIT, 1), (bs_, N_UNIT, SCAN))
        xs, _ = _trunk_back(xs, front_lat, gr_lat, 1, ts, bs_, LAT_CHUNKS_PER_STEP, p, s0=s0, m0=m0)

    return (xp.reshape(bp, tp, d), xs.reshape(bs_, ts, d),
            jnp.stack(new_c, axis=1), jnp.stack(new_n, axis=1), jnp.stack(new_m, axis=1))
```

```python
import functools

import jax
import jax.numpy as jnp
from jax import lax
from jax.experimental import pallas as pl
from jax.experimental.pallas import tpu as pltpu

D_MODEL = 1024
D_GMLP = 512
D_MLSTM = 512
GROUPS = 4
HEADS = 4
HD = 128
CHUNK = 128
SCAN = 256
N_DIR = 2
N_UNIT = N_DIR * HEADS
GRID_W = 64
OFF_V, OFF_Q, OFF_VV, OFF_O, OFF_G = 512, 1024, 2048, 2560, 3072
D_IN = OFF_G + 2 * N_DIR * HEADS
EPS = 1e-6
NEG = -1e30
ST_ROWS = HD + 16
N_GATE = 2 * N_UNIT
GR_ROWS = 5 * N_GATE

TOKEN_BLOCK = 256
CTX_SEQS_PER_STEP = 2
LAT_CHUNKS_PER_STEP = 2
VMEM_LIMIT_BYTES = 56 * 1024 * 1024

F32 = jnp.float32
BF16 = jnp.bfloat16


def _rms(x, g):
    return x * lax.rsqrt(jnp.mean(x * x, axis=-1, keepdims=True) + EPS) * g


LOG2E = 1.4426950408889634
GELU_K = 0.7978845608028654
GELU_C = 0.044715


def _sigmoid(x):
    return 1.0 / (1.0 + jnp.exp2(x * (-LOG2E)))


def _gelu_tanh(x):
    t = (x * x) * (-2.0 * LOG2E * GELU_K * GELU_C) + (-2.0 * LOG2E * GELU_K)
    return x * (1.0 / (1.0 + jnp.exp2(x * t)))


def _log_sigmoid(x):
    return jnp.minimum(x, 0.0) - jnp.log(1.0 + jnp.exp(-jnp.abs(x)))


def _dot(a, b):
    return jnp.dot(a, b, preferred_element_type=F32)


def _dot_nt(a, b):
    return lax.dot_general(a, b, (((1,), (1,)), ((), ())), preferred_element_type=F32)


def _dot_exact(a, b):
    return jnp.dot(a, b, preferred_element_type=F32, precision=lax.Precision.HIGHEST)


def _pipeline_step(tick):
    i = pl.program_id(0)
    last = pl.num_programs(0) - 1

    @pl.when(i == 0)
    def _():
        tick(0, True, False, False)
        tick(1, True, True, False)

    @pl.when(jnp.logical_and(i > 0, i < last))
    def _():
        tick(0, True, True, True)
        tick(1, True, True, True)

    @pl.when(i == last)
    def _():
        tick(0, False, True, True)
        tick(1, False, False, True)


def _const_spec(shape):
    zeros = (0,) * len(shape)
    return pl.BlockSpec(shape, lambda *_: zeros, pipeline_mode=pl.Buffered(1))


def _ada_kernel(c_ref, w_ref, b_ref, o_ref):
    c = c_ref[...]
    s = (c * _sigmoid(c)).astype(BF16)
    o_ref[...] = _dot(s, w_ref[...].astype(BF16)) + b_ref[...]


def _ada(cs, w_ada, b_ada, n):
    rows, d = cs.shape
    tn = 1024
    return pl.pallas_call(
        _ada_kernel,
        out_shape=jax.ShapeDtypeStruct((rows, n), F32),
        grid=(n // tn,),
        in_specs=[
            pl.BlockSpec((rows, d), lambda j: (0, 0)),
            pl.BlockSpec((d, tn), lambda j: (0, j)),
            pl.BlockSpec((1, tn), lambda j: (0, j)),
        ],
        out_specs=pl.BlockSpec((rows, tn), lambda j: (0, j)),
        compiler_params=pltpu.CompilerParams(
            dimension_semantics=("arbitrary",), vmem_limit_bytes=VMEM_LIMIT_BYTES),
        name="ada",
    )(cs, w_ada, b_ada)


def _inproj_norm(x, mod, g1, hb_ref):
    sh1 = mod[:, 0:D_MODEL]
    sc1 = mod[:, D_MODEL:2 * D_MODEL]
    hb_ref[...] = (_rms(x, g1 * (1.0 + sc1)) + sh1).astype(BF16)


def _inproj_project(hb_ref, w, z):
    hb = hb_ref[...]
    z["u"][...] = _dot_nt(hb, w["u"][...])
    z["v"][...] = _dot_nt(hb, w["v"][...])
    z["qk"][...] = _dot_nt(hb, w["qk"][...])
    z["o"][...] = _dot_nt(hb, w["o"][...])
    z["vt"][...] = _dot_nt(w["vv"][...], hb).astype(BF16)
    z["gt"][...] = _dot_nt(w["gt"][...], hb)


def _inproj_finish(z, c, out, seg):
    tb = z["u"].shape[0]
    for g in range(GROUPS):
        gs = slice(g * HD, (g + 1) * HD)
        vg = _rms(_gelu_tanh(z["v"][:, gs]), c["gv"][:, gs]).astype(BF16)
        for ch in range(tb // CHUNK):
            cs = slice(ch * CHUNK, (ch + 1) * CHUNK)
            mixed = _dot(c["ws"][g], vg[cs]) + c["bs"][:, gs]
            out["a"][cs, gs] = (_gelu_tanh(z["u"][cs, gs]) * mixed).astype(BF16)

    zqk = z["qk"][...]
    pos = lax.broadcasted_iota(jnp.int32, (tb, 1), 0) % seg
    prev = jnp.where(pos != 0, pltpu.roll(zqk, 1, 0), 0.0)
    nxt = jnp.where(pos != seg - 1, pltpu.roll(zqk, tb - 1, 0), 0.0)
    cw = c["cw"]
    y = c["cb"][...] + prev * cw[0:1, :] + zqk * cw[1:2, :] + nxt * cw[2:3, :]
    y = y * _sigmoid(y)
    out["qt"][...] = jnp.transpose(y[:, :D_MLSTM]).astype(BF16)
    out["k"][...] = (y[:, D_MLSTM:] * (HD ** -0.5)).astype(BF16)
    out["vt"][...] = z["vt"][...]
    out["o"][...] = z["o"][...]

    assert tb == SCAN
    gt = z["gt"][...] + c["bgt"][...]
    gi = pltpu.roll(gt, HEADS, 0)
    lf = _log_sigmoid(gt)
    lane = lax.broadcasted_iota(jnp.int32, (N_GATE, HD), 1)
    tiles = []
    carry = jnp.zeros((N_GATE, HD), F32)
    for t in range(SCAN // HD):
        prefix = lf[:, t * HD:(t + 1) * HD]
        shift = 1
        while shift < HD:
            prefix = prefix + jnp.where(lane >= shift, pltpu.roll(prefix, shift, 1), 0.0)
            shift *= 2
        tiles.append(prefix + carry)
        carry = carry + jnp.broadcast_to(prefix[:, HD - 1:HD], (N_GATE, HD))
    prefix = jnp.concatenate(tiles, axis=1)
    b_last = jnp.concatenate([carry] * (SCAN // HD), axis=1)
    row_bwd = lax.broadcasted_iota(jnp.int32, (N_GATE, SCAN), 0) >= N_GATE // N_DIR
    b_row = jnp.where(row_bwd, b_last - prefix + lf, prefix)
    g_row = b_last - b_row + gi
    g_max = jnp.broadcast_to(jnp.max(g_row, axis=1, keepdims=True), (N_GATE, SCAN))
    for sec, stat in enumerate((b_row, g_row, b_last, g_max, gi - b_row)):
        out["gr"][sec * N_GATE:(sec + 1) * N_GATE, :] = stat * LOG2E


_INPROJ_Z = ("u", "v", "qk", "o", "vt", "gt")


def _inproj_kernel(*refs, seg, blocks_per_mod, mod_row0, n_cast, ada_tail):
    refs = list(refs)
    take = lambda k: [refs.pop(0) for _ in range(k)]
    (x_ref, mod_ref, g1_ref, wit_ref,
     bgt_ref, cw_ref, cb_ref, ws_ref, bs_ref, gv_ref) = take(10)
    cast_in = take(n_cast)
    ada_in = take(3) if ada_tail else None
    a_ref, qt_ref, k_ref, vt_ref, o_ref, gr_ref = take(6)
    cast_out = take(n_cast)
    ada_out = take(1)[0] if ada_tail else None
    hb_s, *z_s = refs
    tb = x_ref.shape[0] // 2
    mod_in = mod_row0 + jnp.minimum(pl.program_id(0), pl.num_programs(0) - 2) // blocks_per_mod
    w = {"u": wit_ref.at[0:OFF_V, :], "v": wit_ref.at[OFF_V:OFF_Q, :],
         "qk": wit_ref.at[OFF_Q:OFF_VV, :], "vv": wit_ref.at[OFF_VV:OFF_O, :],
         "o": wit_ref.at[OFF_O:OFF_G, :], "gt": wit_ref.at[OFF_G:D_IN, :]}
    c = {"gv": gv_ref, "ws": ws_ref, "bs": bs_ref, "cw": cw_ref, "cb": cb_ref, "bgt": bgt_ref}

    def tick(half, norm, project, finish):
        rows = slice(half * tb, (half + 1) * tb)
        cur, oth = half, 1 - half
        z_cur = {name: ref.at[cur] for name, ref in zip(_INPROJ_Z, z_s)}
        z_oth = {name: ref.at[oth] for name, ref in zip(_INPROJ_Z, z_s)}
        out = {"a": a_ref.at[rows, :], "qt": qt_ref.at[:, rows], "k": k_ref.at[rows, :],
               "vt": vt_ref.at[:, rows], "o": o_ref.at[rows, :], "gr": gr_ref.at[:, rows]}
        if project:
            _inproj_project(hb_s.at[oth], w, z_oth)
        if norm:
            _inproj_norm(x_ref[rows, :], mod_ref[pl.ds(mod_in, 1), :], g1_ref[...], hb_s.at[cur])
        if finish:
            _inproj_finish(z_cur, c, out, seg)
        if half == 0:
            for src, dst in zip(cast_in, cast_out):
                dst[...] = src[...].astype(BF16)
            if ada_tail:
                _ada_kernel(*ada_in, ada_out)

    _pipeline_step(tick)


def _inproj(x, mod, mod_row0, rows_per_mod, seg, p, cast=(), ada_tail=None):
    n = x.shape[0]
    tb = TOKEN_BLOCK
    tb2 = 2 * tb
    assert tb % seg == 0 and n % tb2 == 0 and rows_per_mod % tb2 == 0
    nb = n // tb2
    blk_in = lambda i: jnp.minimum(i, nb - 1)
    blk_out = lambda i: jnp.maximum(i - 1, 0)
    tok = lambda width: pl.BlockSpec((tb2, width), lambda i: (blk_out(i), 0))
    tok_t = lambda height: pl.BlockSpec((height, tb2), lambda i: (0, blk_out(i)))
    in_specs = [
        pl.BlockSpec((tb2, D_MODEL), lambda i: (blk_in(i), 0)),
        _const_spec(mod.shape),
        _const_spec((1, D_MODEL)),
        _const_spec((D_IN, D_MODEL)),
        _const_spec((N_GATE, 1)),
        _const_spec((3, 2 * D_MLSTM)),
        _const_spec((1, 2 * D_MLSTM)),
        _const_spec((GROUPS, CHUNK, CHUNK)),
        _const_spec((CHUNK, D_GMLP)),
        _const_spec((1, D_GMLP)),
    ]
    out_shape = [
        jax.ShapeDtypeStruct((n, D_GMLP), BF16),
        jax.ShapeDtypeStruct((D_MLSTM, n), BF16),
        jax.ShapeDtypeStruct((n, D_MLSTM), BF16),
        jax.ShapeDtypeStruct((D_MLSTM, n), BF16),
        jax.ShapeDtypeStruct((n, D_MLSTM), F32),
        jax.ShapeDtypeStruct((GR_ROWS, n), F32),
    ]
    out_specs = [tok(D_GMLP), tok_t(D_MLSTM), tok(D_MLSTM), tok_t(D_MLSTM), tok(D_MLSTM),
                 tok_t(GR_ROWS)]
    for wf in cast:
        rows = wf.shape[0] // nb
        assert rows * nb == wf.shape[0] and rows % 16 == 0
        slab = pl.BlockSpec((rows, wf.shape[1]), lambda i: (blk_in(i), 0))
        in_specs.append(slab)
        out_specs.append(slab)
        out_shape.append(jax.ShapeDtypeStruct(wf.shape, BF16))
    side = list(cast)
    if ada_tail is not None:
        cs, w_ada, b_ada, col0 = ada_tail
        width = (w_ada.shape[1] - col0) // nb
        assert col0 % width == 0 and width * nb == w_ada.shape[1] - col0 and width % 128 == 0
        in_specs += [
            _const_spec(cs.shape),
            pl.BlockSpec((w_ada.shape[0], width), lambda i: (0, col0 // width + blk_in(i))),
            pl.BlockSpec((1, width), lambda i: (0, col0 // width + blk_in(i))),
        ]
        out_specs.append(pl.BlockSpec((cs.shape[0], width), lambda i: (0, blk_in(i))))
        out_shape.append(jax.ShapeDtypeStruct((cs.shape[0], w_ada.shape[1] - col0), F32))
        side += [cs, w_ada, b_ada]
    scratch_shapes = [
        pltpu.VMEM((2, tb, D_MODEL), BF16),
        pltpu.VMEM((2, tb, D_GMLP), F32),
        pltpu.VMEM((2, tb, D_GMLP), F32),
        pltpu.VMEM((2, tb, 2 * D_MLSTM), F32),
        pltpu.VMEM((2, tb, D_MLSTM), F32),
        pltpu.VMEM((2, D_MLSTM, tb), BF16),
        pltpu.VMEM((2, N_GATE, tb), F32),
    ]
    return pl.pallas_call(
        functools.partial(_inproj_kernel, seg=seg, blocks_per_mod=rows_per_mod // tb2,
                          mod_row0=mod_row0, n_cast=len(cast), ada_tail=ada_tail is not None),
        out_shape=out_shape,
        grid=(nb + 1,),
        in_specs=in_specs,
        out_specs=out_specs,
        scratch_shapes=scratch_shapes,
        compiler_params=pltpu.CompilerParams(
            dimension_semantics=("arbitrary",), vmem_limit_bytes=VMEM_LIMIT_BYTES),
        name="inproj",
    )(x, mod, p["g1"], p["wit"],
      p["bgt"], p["cw"], p["cb"], p["ws"], p["bs"], p["gv"], *side)


def _lane_broadcast_column(row):
    n = row.shape[1]
    tiles = [jnp.transpose(jnp.broadcast_to(row[:, t * HD:(t + 1) * HD], (HD, HD)))
             for t in range(n // HD)]
    col = jnp.concatenate(tiles, axis=0)
    return jnp.concatenate([col] * (n // HD), axis=1)


def _mlstm_unit(k, vt, kq, qs, d_row, b_row, g_row, bl_row, gmax_row, st, m_row, mask, carried):
    logw = jnp.where(mask, _lane_broadcast_column(d_row) + b_row, NEG)
    a = b_row + m_row
    mj = jnp.maximum(a, jnp.max(logw, axis=0, keepdims=True))
    w = jnp.exp2(logw - mj)
    s = kq * w
    num = _dot(vt, s.astype(BF16))
    den = jnp.sum(s, axis=0, keepdims=True)
    if carried:
        inter = jnp.exp2(a - mj)
        num = num + inter * qs[:HD]
        den = den + inter * qs[HD:HD + 1]
    h = num * (1.0 / jnp.maximum(jnp.abs(den), jnp.exp2(-mj)))
    m_new = jnp.maximum(bl_row + m_row, gmax_row)
    wc = jnp.exp2(g_row - m_new)
    pad_row = lax.broadcasted_iota(jnp.int32, (ST_ROWS - HD, wc.shape[1]), 0)
    v_aug = jnp.concatenate(
        [vt.astype(F32) * wc, jnp.where(pad_row == 0, wc, 0.0)], axis=0).astype(BF16)
    st_new = _dot(v_aug, k)
    if carried:
        decay = jnp.exp2(bl_row + m_row - m_new)
        st_new = decay[:, :HD] * st + st_new
    return h, st_new, m_new


def _mlstm_kernel(*refs, has_init, has_out, rounds, by_sequence):
    if has_init:
        s0_ref, m0_ref = refs[:2]
        refs = refs[2:]
    if by_sequence:
        k_ref, qt_ref, vt_ref, gr_ref = refs[:4]
        fwd_in = bwd_in = (k_ref, qt_ref, vt_ref, gr_ref)
        refs = refs[4:]
    else:
        fwd_in, bwd_in = refs[:4], refs[4:8]
        refs = refs[8:]
    hf_ref, hb_ref = refs[:2]
    refs = refs[2:]
    if has_out:
        co_ref, no_ref, mo_ref = refs[:3]
        refs = refs[3:]
    s_ref, m_ref = refs
    c = pl.program_id(1)
    nc = pl.num_programs(1)
    carried = not by_sequence

    if carried:
        @pl.when(c == 0)
        def _():
            if has_init:
                s_ref[...] = s0_ref[0]
                m_ref[...] = m0_ref[0]
            else:
                s_ref[...] = jnp.zeros_like(s_ref)
                m_ref[...] = jnp.zeros_like(m_ref)

    si = lax.broadcasted_iota(jnp.int32, (SCAN, SCAN), 0)
    ji = lax.broadcasted_iota(jnp.int32, (SCAN, SCAN), 1)
    dirs = ((fwd_in, hf_ref, si <= ji), (bwd_in, hb_ref, si >= ji))

    def chunk_cols(d, r):
        pos = rounds - 1 - r if (d == 1 and not by_sequence) else r
        return slice(pos * SCAN, (pos + 1) * SCAN)

    kq = {}
    for r in range(rounds):
        for d, ((k_ref, qt_ref, _, _), _, _) in enumerate(dirs):
            cs = chunk_cols(d, r)
            for hd in range(HEADS):
                hs = slice(hd * HD, (hd + 1) * HD)
                kq[r, d, hd] = _dot(k_ref[cs, hs], qt_ref[hs, cs])

    m_cur = None
    if carried:
        m_all = m_ref[...]
        m_cur = [m_all[u:u + 1] for u in range(N_UNIT)]
    zero_row = jnp.zeros((1, SCAN), F32)
    for r in range(rounds):
        slot = r * N_UNIT if by_sequence else 0
        for d, ((k_ref, qt_ref, vt_ref, gr_ref), h_ref, mask) in enumerate(dirs):
            cs = chunk_cols(d, r)
            for hd in range(HEADS):
                u = d * HEADS + hd
                row = d * 2 * HEADS + HEADS + hd
                hs = slice(hd * HD, (hd + 1) * HD)
                st = s_ref[slot + u] if carried else None
                qs = _dot(st.astype(BF16), qt_ref[hs, cs]) if carried else None
                h, st_new, m_new = _mlstm_unit(
                    k_ref[cs, hs], vt_ref[hs, cs], kq[r, d, hd], qs,
                    *(gr_ref[sec * N_GATE + row:sec * N_GATE + row + 1, cs]
                      for sec in (4, 0, 1, 2, 3)),
                    st, m_cur[u] if carried else zero_row, mask, carried)
                h_ref[hs, cs] = h
                s_ref[slot + u] = st_new
                if carried:
                    m_cur[u] = m_new
                else:
                    m_ref[slot + u:slot + u + 1, :] = m_new
    if carried:
        m_ref[...] = jnp.concatenate(m_cur, axis=0)

    if has_out:
        @pl.when(c == nc - 1)
        def _():
            for j in range(s_ref.shape[0]):
                q, u = divmod(j, N_UNIT)
                co_ref[q, u] = jnp.transpose(s_ref[j, 0:HD, :])
                no_ref[q, u:u + 1, :] = s_ref[j, HD:HD + 1, :]
                mo_ref[q, u:u + 1, :] = m_ref[j:j + 1, :]


def _mlstm(k, qt, vt, gr, batch, rounds, s0=None, m0=None, want_state=False):
    n = k.shape[0]
    nc = n // batch // SCAN
    by_sequence = nc == 1
    width = rounds * SCAN
    has_init = s0 is not None
    if by_sequence:
        assert batch % rounds == 0 and not has_init
        grid = (batch // rounds, 1)
        slots = rounds
        fwd = bwd = lambda b, c: b
    else:
        assert nc % rounds == 0 and not want_state
        steps = nc // rounds
        grid = (batch, steps)
        slots = 1
        fwd = lambda b, c: b * steps + c
        bwd = lambda b, c: b * steps + steps - 1 - c

    def specs(ix):
        return [
            pl.BlockSpec((width, D_MLSTM), lambda b, c: (ix(b, c), 0)),
            pl.BlockSpec((D_MLSTM, width), lambda b, c: (0, ix(b, c))),
            pl.BlockSpec((D_MLSTM, width), lambda b, c: (0, ix(b, c))),
            pl.BlockSpec((GR_ROWS, width), lambda b, c: (0, ix(b, c))),
        ]

    in_specs = specs(fwd) if by_sequence else specs(fwd) + specs(bwd)
    args = [k, qt, vt, gr] if by_sequence else [k, qt, vt, gr, k, qt, vt, gr]
    if has_init:
        in_specs = [
            pl.BlockSpec((1, N_UNIT, ST_ROWS, HD), lambda b, c: (b, 0, 0, 0)),
            pl.BlockSpec((1, N_UNIT, SCAN), lambda b, c: (b, 0, 0)),
        ] + in_specs
        args = [s0, m0] + args
    out_shape = [
        jax.ShapeDtypeStruct((D_MLSTM, n), F32),
        jax.ShapeDtypeStruct((D_MLSTM, n), F32),
    ]
    out_specs = [
        pl.BlockSpec((D_MLSTM, width), lambda b, c: (0, fwd(b, c))),
        pl.BlockSpec((D_MLSTM, width), lambda b, c: (0, bwd(b, c))),
    ]
    if want_state:
        out_shape += [
            jax.ShapeDtypeStruct((batch, N_UNIT, HD, HD), F32),
            jax.ShapeDtypeStruct((batch, N_UNIT, HD), F32),
            jax.ShapeDtypeStruct((batch, N_UNIT, SCAN), F32),
        ]
        out_specs += [
            pl.BlockSpec((slots, N_UNIT, HD, HD), lambda b, c: (b, 0, 0, 0)),
            pl.BlockSpec((slots, N_UNIT, HD), lambda b, c: (b, 0, 0)),
            pl.BlockSpec((slots, N_UNIT, SCAN), lambda b, c: (b, 0, 0)),
        ]
    return pl.pallas_call(
        functools.partial(_mlstm_kernel, has_init=has_init, has_out=want_state, rounds=rounds,
                          by_sequence=by_sequence),
        out_shape=out_shape,
        grid=grid,
        in_specs=in_specs,
        out_specs=out_specs,
        scratch_shapes=[pltpu.VMEM((slots * N_UNIT, ST_ROWS, HD), F32),
                        pltpu.VMEM((slots * N_UNIT, SCAN), F32)],
        compiler_params=pltpu.CompilerParams(
            dimension_semantics=("arbitrary", "arbitrary"), vmem_limit_bytes=VMEM_LIMIT_BYTES),
        name="mlstm",
    )(*args)


def _outffn_mix(x, a, hft, hbt, o, mod, c, x1_ref, h2_ref):
    ga1 = mod[:, 0:D_MODEL]
    sh2 = mod[:, D_MODEL:2 * D_MODEL]
    sc2 = mod[:, 2 * D_MODEL:3 * D_MODEL]
    hs = jnp.transpose(hft + hbt)
    sig = _sigmoid(o)
    parts = []
    for hd in range(HEADS):
        sl = slice(hd * HD, (hd + 1) * HD)
        parts.append(_rms(hs[:, sl], c["gh"][:, sl]) * sig[:, sl])
    b_out = jnp.concatenate(parts, axis=-1).astype(BF16)
    mix = _dot(a, c["wout"][0:D_GMLP, :]) + _dot(b_out, c["wout"][D_GMLP:, :])
    x1 = x + ga1 * mix
    x1_ref[...] = x1
    h2_ref[...] = (_rms(x1, c["g2"][...] * (1.0 + sc2)) + sh2).astype(BF16)


def _outffn_up(h2_ref, x1_in_ref, c, f_ref, x1_out_ref):
    h2 = h2_ref[...]
    u = _dot(h2, c["w1"][...])
    g = _dot(h2, c["w3"][...])
    f_ref[...] = (u * _sigmoid(u) * g).astype(BF16)
    x1_out_ref[...] = x1_in_ref[...]


def _outffn_down(f_ref, x1_ref, mod, c, y_ref):
    ga2 = mod[:, 3 * D_MODEL:4 * D_MODEL]
    x2 = x1_ref[...] + ga2 * _dot(f_ref[...], c["w2"][...])
    y_ref[...] = _rms(x2, c["gf"][...])


def _outffn_kernel(x_ref, a_ref, hft_ref, hbt_ref, o_ref, mod_ref, gh_ref, wout_hbm,
                   g2_ref, w1_hbm, w3_hbm, w2_hbm, gf_ref, y_ref,
                   x1a_s, h2_s, f_s, x1b_s, wout_s, w1_s, w3_s, w2_s, w_sem,
                   *, blocks_per_mod, mod_row0):
    tb = x_ref.shape[0] // 2
    i = pl.program_id(0)
    mod_in = mod_row0 + jnp.minimum(i, pl.num_programs(0) - 2) // blocks_per_mod
    mod_out = mod_row0 + jnp.maximum(i - 1, 0) // blocks_per_mod
    c = {"gh": gh_ref, "wout": wout_s, "g2": g2_ref, "w1": w1_s, "w3": w3_s, "w2": w2_s,
         "gf": gf_ref}
    weight_copy = {
        name: pltpu.make_async_copy(src, dst, w_sem.at[j])
        for j, (name, src, dst) in enumerate((("wout", wout_hbm, wout_s), ("w1", w1_hbm, w1_s),
                                              ("w3", w3_hbm, w3_s), ("w2", w2_hbm, w2_s)))}

    def tick(half, mix, up, down):
        rows = slice(half * tb, (half + 1) * tb)
        cur, oth = half, 1 - half
        first_tick = mix and not up and not down
        second_tick = mix and up and not down
        if first_tick:
            for name in ("wout", "w1", "w3", "w2"):
                weight_copy[name].start()
            weight_copy["wout"].wait()
        if down:
            _outffn_down(f_s.at[cur], x1b_s.at[cur], mod_ref[pl.ds(mod_out, 1), :], c,
                         y_ref.at[rows, :])
        if mix:
            _outffn_mix(x_ref[rows, :], a_ref[rows, :], hft_ref[:, rows], hbt_ref[:, rows],
                        o_ref[rows, :], mod_ref[pl.ds(mod_in, 1), :], c, x1a_s.at[cur],
                        h2_s.at[cur])
        if second_tick:
            weight_copy["w1"].wait()
            weight_copy["w3"].wait()
        if up:
            _outffn_up(h2_s.at[oth], x1a_s.at[oth], c, f_s.at[oth], x1b_s.at[oth])
        if second_tick:
            weight_copy["w2"].wait()

    _pipeline_step(tick)


def _outffn(x, a, hft, hbt, o, mod, mod_row0, rows_per_mod, p):
    n = x.shape[0]
    tb = TOKEN_BLOCK
    tb2 = 2 * tb
    assert n % tb2 == 0 and rows_per_mod % tb2 == 0
    nb = n // tb2
    d_ff = p["w1"].shape[1]
    blk_in = lambda i: jnp.minimum(i, nb - 1)
    blk_out = lambda i: jnp.maximum(i - 1, 0)
    tok = lambda w: pl.BlockSpec((tb2, w), lambda i: (blk_in(i), 0))
    tok_t = lambda h: pl.BlockSpec((h, tb2), lambda i: (0, blk_in(i)))
    weight_hbm = pl.BlockSpec(memory_space=pl.ANY)
    in_specs = [
        tok(D_MODEL), tok(D_GMLP), tok_t(D_MLSTM), tok_t(D_MLSTM), tok(D_MLSTM),
        _const_spec(mod.shape),
        _const_spec((1, D_MLSTM)),
        weight_hbm,
        _const_spec((1, D_MODEL)),
        weight_hbm,
        weight_hbm,
        weight_hbm,
        _const_spec((1, D_MODEL)),
    ]
    scratch_shapes = [
        pltpu.VMEM((2, tb, D_MODEL), F32),
        pltpu.VMEM((2, tb, D_MODEL), BF16),
        pltpu.VMEM((2, tb, d_ff), BF16),
        pltpu.VMEM((2, tb, D_MODEL), F32),
        pltpu.VMEM(p["wout"].shape, BF16),
        pltpu.VMEM(p["w1"].shape, BF16),
        pltpu.VMEM(p["w3"].shape, BF16),
        pltpu.VMEM(p["w2"].shape, BF16),
        pltpu.SemaphoreType.DMA((4,)),
    ]
    return pl.pallas_call(
        functools.partial(_outffn_kernel, blocks_per_mod=rows_per_mod // tb2, mod_row0=mod_row0),
        out_shape=jax.ShapeDtypeStruct((n, D_MODEL), F32),
        grid=(nb + 1,),
        in_specs=in_specs,
        out_specs=pl.BlockSpec((tb2, D_MODEL), lambda i: (blk_out(i), 0)),
        scratch_shapes=scratch_shapes,
        compiler_params=pltpu.CompilerParams(
            dimension_semantics=("arbitrary",), vmem_limit_bytes=VMEM_LIMIT_BYTES),
        name="outffn",
    )(x, a, hft, hbt, o, mod, p["gh"], p["wout"], p["g2"], p["w1"], p["w3"], p["w2"], p["gf"])


def _layer_params(l, g_norm1, w_in, b_gate, w_s, b_s, g_v, conv_w, conv_b, g_h, g_norm2, g_final):
    return {
        "g1": g_norm1[l][None, :],
        "wit": w_in[l].T.astype(BF16),
        "bgt": b_gate[l][:, None],
        "cw": conv_w[l], "cb": conv_b[l][None, :],
        "ws": w_s[l].astype(BF16),
        "bs": jnp.repeat(b_s[l].T, HD, axis=1),
        "gv": g_v[l].reshape(1, D_GMLP),
        "gh": g_h[l].reshape(1, D_MLSTM),
        "g2": g_norm2[l][None, :],
        "gf": g_final[None, :],
    }


_LATE_WEIGHTS = ("wout", "w1", "w3", "w2")


def _trunk_front(x, mod_row0, rows_per_mod, seg, p, late=None):
    cast, ada_tail = ((), None) if late is None else (
        tuple(late[0][name] for name in _LATE_WEIGHTS), late[1])
    a, qt, k, vt, o, gr, *side = _inproj(x, p["mod_head"], mod_row0, rows_per_mod, seg, p, cast,
                                         ada_tail)
    if late is not None:
        p = dict(p, **dict(zip(_LATE_WEIGHTS, side[:-1])))
        p["mod_tail"] = side[-1]
    return (a, qt, k, vt, o), gr, p


def _trunk_back(x, front, gr, mod_row0, rows_per_mod, batch, rounds, p, s0=None, m0=None,
                want_state=False):
    a, qt, k, vt, o = front
    hft, hbt, *state = _mlstm(k, qt, vt, gr, batch, rounds, s0, m0, want_state)
    y = _outffn(x, a, hft, hbt, o, p["mod_tail"], mod_row0, rows_per_mod, p)
    return y, state


def kernel(x_prompt, x_sample, state_C, state_n, state_m, c, c_ctx, w_ada, b_ada, g_norm1, w_in,
           b_gate, w_s, b_s, g_v, conv_w, conv_b, g_h, w_out, g_norm2, w1, w3, w2, g_final):
    bp, tp, d = x_prompt.shape
    bs_, ts, _ = x_sample.shape
    depth = w_in.shape[0]
    assert depth == 1, "final norm is fused into the layer's last kernel"
    xp = x_prompt.reshape(bp * tp, d)
    xs = x_sample.reshape(bs_ * ts, d)

    cs = jnp.zeros((8, d), F32).at[0].set(c_ctx).at[1:1 + bs_].set(c)
    new_c, new_n, new_m = [], [], []
    for l in range(depth):
        p = _layer_params(l, g_norm1, w_in, b_gate, w_s, b_s, g_v, conv_w, conv_b, g_h, g_norm2,
                          g_final)
        late_f32 = {"wout": w_out[l], "w1": w1[l], "w3": w3[l], "w2": w2[l]}
        p["mod_head"] = _ada(cs, w_ada[l], b_ada[l][None, :], 2 * d)
        ada_tail = (cs, w_ada[l], b_ada[l][None, :], 2 * d)

        front_lat, gr_lat, p = _trunk_front(xs, 1, ts, GRID_W, p, (late_f32, ada_tail))
        front_ctx, gr_ctx, _ = _trunk_front(xp, 0, bp * tp, tp, p)

        xp, (c_ctx_out, n_ctx_out, m_ctx_out) = _trunk_back(
            xp, front_ctx, gr_ctx, 0, bp * tp, bp, CTX_SEQS_PER_STEP, p, want_state=True)
        new_c.append(c_ctx_out.reshape(bp, N_DIR, HEADS, HD, HD))
        new_n.append(n_ctx_out.reshape(bp, N_DIR, HEADS, HD))
        new_m.append((m_ctx_out[..., 0] * (1.0 / LOG2E)).reshape(bp, N_DIR, HEADS))

        s0 = jnp.concatenate(
            [jnp.swapaxes(state_C[:, l], -1, -2), state_n[:, l][..., None, :],
             jnp.zeros((bs_, N_DIR, HEADS, ST_ROWS - HD - 1, HD), F32)],
            axis=-2).reshape(bs_, N_UNIT, ST_ROWS, HD)
        m0 = jnp.broadcast_to((state_m[:, l] * LOG2E).reshape(bs_, N_UNIT, 1),
                              (bs_, N_UNIT, SCAN))
        xs, _ = _trunk_back(xs, front_lat, gr_lat, 1, ts, bs_, LAT_CHUNKS_PER_STEP, p, s0=s0, m0=m0)

    return (xp.reshape(bp, tp, d), xs.reshape(bs_, ts, d),
            jnp.stack(new_c, axis=1), jnp.stack(new_n, axis=1), jnp.stack(new_m, axis=1))
```

```python
import functools

import jax
import jax.numpy as jnp
from jax import lax
from jax.experimental import pallas as pl
from jax.experimental.pallas import tpu as pltpu

D_MODEL = 1024
D_GMLP = 512
D_MLSTM = 512
GROUPS = 4
HEADS = 4
HD = 128
CHUNK = 128
SCAN = 256
N_DIR = 2
N_UNIT = N_DIR * HEADS
GRID_W = 64
OFF_V, OFF_Q, OFF_VV, OFF_O, OFF_G = 512, 1024, 2048, 2560, 3072
D_IN = OFF_G + 2 * N_DIR * HEADS
EPS = 1e-6
NEG = -1e30
ST_ROWS = HD + 16
N_GATE = 2 * N_UNIT
GR_ROWS = 5 * N_GATE

TOKEN_BLOCK = 256
CTX_SEQS_PER_STEP = 2
LAT_CHUNKS_PER_STEP = 2
VMEM_LIMIT_BYTES = 56 * 1024 * 1024

F32 = jnp.float32
BF16 = jnp.bfloat16


def _rms(x, g):
    return x * lax.rsqrt(jnp.mean(x * x, axis=-1, keepdims=True) + EPS) * g


LOG2E = 1.4426950408889634
GELU_K = 0.7978845608028654
GELU_C = 0.044715


def _sigmoid(x):
    return 1.0 / (1.0 + jnp.exp2(x * (-LOG2E)))


def _gelu_tanh(x):
    t = (x * x) * (-2.0 * LOG2E * GELU_K * GELU_C) + (-2.0 * LOG2E * GELU_K)
    return x * (1.0 / (1.0 + jnp.exp2(x * t)))


def _log_sigmoid(x):
    return jnp.minimum(x, 0.0) - jnp.log(1.0 + jnp.exp(-jnp.abs(x)))


def _dot(a, b):
    return jnp.dot(a, b, preferred_element_type=F32)


def _dot_nt(a, b):
    return lax.dot_general(a, b, (((1,), (1,)), ((), ())), preferred_element_type=F32)


def _dot_exact(a, b):
    return jnp.dot(a, b, preferred_element_type=F32, precision=lax.Precision.HIGHEST)


def _pipeline_step(tick):
    i = pl.program_id(0)
    last = pl.num_programs(0) - 1

    @pl.when(i == 0)
    def _():
        tick(0, True, False, False)
        tick(1, True, True, False)

    @pl.when(jnp.logical_and(i > 0, i < last))
    def _():
        tick(0, True, True, True)
        tick(1, True, True, True)

    @pl.when(i == last)
    def _():
        tick(0, False, True, True)
        tick(1, False, False, True)


def _const_spec(shape):
    zeros = (0,) * len(shape)
    return pl.BlockSpec(shape, lambda *_: zeros, pipeline_mode=pl.Buffered(1))


def _ada_kernel(c_ref, w_ref, b_ref, o_ref):
    c = c_ref[...]
    s = (c * _sigmoid(c)).astype(BF16)
    o_ref[...] = _dot(s, w_ref[...].astype(BF16)) + b_ref[...]


def _ada(cs, w_ada, b_ada, n):
    rows, d = cs.shape
    tn = 1024
    return pl.pallas_call(
        _ada_kernel,
        out_shape=jax.ShapeDtypeStruct((rows, n), F32),
        grid=(n // tn,),
        in_specs=[
            pl.BlockSpec((rows, d), lambda j: (0, 0)),
            pl.BlockSpec((d, tn), lambda j: (0, j)),
            pl.BlockSpec((1, tn), lambda j: (0, j)),
        ],
        out_specs=pl.BlockSpec((rows, tn), lambda j: (0, j)),
        compiler_params=pltpu.CompilerParams(
            dimension_semantics=("arbitrary",), vmem_limit_bytes=VMEM_LIMIT_BYTES),
        name="ada",
    )(cs, w_ada, b_ada)


def _inproj_norm(x, mod, g1, hb_ref):
    sh1 = mod[:, 0:D_MODEL]
    sc1 = mod[:, D_MODEL:2 * D_MODEL]
    hb_ref[...] = (_rms(x, g1 * (1.0 + sc1)) + sh1).astype(BF16)


def _inproj_project(hb_ref, w, z):
    hb = hb_ref[...]
    z["u"][...] = _dot_nt(hb, w["u"][...])
    z["v"][...] = _dot_nt(hb, w["v"][...])
    z["qk"][...] = _dot_nt(hb, w["qk"][...])
    z["o"][...] = _dot_nt(hb, w["o"][...])
    z["vt"][...] = _dot_nt(w["vv"][...], hb).astype(BF16)
    z["gt"][...] = _dot_nt(w["gt"][...], hb)


def _inproj_finish(z, c, out, seg):
    tb = z["u"].shape[0]
    for g in range(GROUPS):
        gs = slice(g * HD, (g + 1) * HD)
        vg = _rms(_gelu_tanh(z["v"][:, gs]), c["gv"][:, gs]).astype(BF16)
        for ch in range(tb // CHUNK):
            cs = slice(ch * CHUNK, (ch + 1) * CHUNK)
            mixed = _dot(c["ws"][g], vg[cs]) + c["bs"][:, gs]
            out["a"][cs, gs] = (_gelu_tanh(z["u"][cs, gs]) * mixed).astype(BF16)

    zqk = z["qk"][...]
    pos = lax.broadcasted_iota(jnp.int32, (tb, 1), 0) % seg
    prev = jnp.where(pos != 0, pltpu.roll(zqk, 1, 0), 0.0)
    nxt = jnp.where(pos != seg - 1, pltpu.roll(zqk, tb - 1, 0), 0.0)
    cw = c["cw"]
    y = c["cb"][...] + prev * cw[0:1, :] + zqk * cw[1:2, :] + nxt * cw[2:3, :]
    y = y * _sigmoid(y)
    out["qt"][...] = jnp.transpose(y[:, :D_MLSTM]).astype(BF16)
    out["k"][...] = (y[:, D_MLSTM:] * (HD ** -0.5)).astype(BF16)
    out["vt"][...] = z["vt"][...]
    out["o"][...] = z["o"][...]

    assert tb == SCAN
    gt = z["gt"][...] + c["bgt"][...]
    gi = pltpu.roll(gt, HEADS, 0)
    lf = _log_sigmoid(gt)
    lane = lax.broadcasted_iota(jnp.int32, (N_GATE, HD), 1)
    tiles = []
    carry = jnp.zeros((N_GATE, HD), F32)
    for t in range(SCAN // HD):
        prefix = lf[:, t * HD:(t + 1) * HD]
        shift = 1
        while shift < HD:
            prefix = prefix + jnp.where(lane >= shift, pltpu.roll(prefix, shift, 1), 0.0)
            shift *= 2
        tiles.append(prefix + carry)
        carry = carry + jnp.broadcast_to(prefix[:, HD - 1:HD], (N_GATE, HD))
    prefix = jnp.concatenate(tiles, axis=1)
    b_last = jnp.concatenate([carry] * (SCAN // HD), axis=1)
    row_bwd = lax.broadcasted_iota(jnp.int32, (N_GATE, SCAN), 0) >= N_GATE // N_DIR
    b_row = jnp.where(row_bwd, b_last - prefix + lf, prefix)
    g_row = b_last - b_row + gi
    g_max = jnp.broadcast_to(jnp.max(g_row, axis=1, keepdims=True), (N_GATE, SCAN))
    for sec, stat in enumerate((b_row, g_row, b_last, g_max, gi - b_row)):
        out["gr"][sec * N_GATE:(sec + 1) * N_GATE, :] = stat * LOG2E


_INPROJ_Z = ("u", "v", "qk", "o", "vt", "gt")


def _inproj_kernel(*refs, seg, blocks_per_mod, mod_row0, n_cast, ada_tail):
    refs = list(refs)
    take = lambda k: [refs.pop(0) for _ in range(k)]
    (x_ref, mod_ref, g1_ref, wit_ref,
     bgt_ref, cw_ref, cb_ref, ws_ref, bs_ref, gv_ref) = take(10)
    cast_in = take(n_cast)
    ada_in = take(3) if ada_tail else None
    a_ref, qt_ref, k_ref, vt_ref, o_ref, gr_ref = take(6)
    cast_out = take(n_cast)
    ada_out = take(1)[0] if ada_tail else None
    hb_s, *z_s = refs
    tb = x_ref.shape[0] // 2
    mod_in = mod_row0 + jnp.minimum(pl.program_id(0), pl.num_programs(0) - 2) // blocks_per_mod
    w = {"u": wit_ref.at[0:OFF_V, :], "v": wit_ref.at[OFF_V:OFF_Q, :],
         "qk": wit_ref.at[OFF_Q:OFF_VV, :], "vv": wit_ref.at[OFF_VV:OFF_O, :],
         "o": wit_ref.at[OFF_O:OFF_G, :], "gt": wit_ref.at[OFF_G:D_IN, :]}
    c = {"gv": gv_ref, "ws": ws_ref, "bs": bs_ref, "cw": cw_ref, "cb": cb_ref, "bgt": bgt_ref}

    def tick(half, norm, project, finish):
        rows = slice(half * tb, (half + 1) * tb)
        cur, oth = half, 1 - half
        z_cur = {name: ref.at[cur] for name, ref in zip(_INPROJ_Z, z_s)}
        z_oth = {name: ref.at[oth] for name, ref in zip(_INPROJ_Z, z_s)}
        out = {"a": a_ref.at[rows, :], "qt": qt_ref.at[:, rows], "k": k_ref.at[rows, :],
               "vt": vt_ref.at[:, rows], "o": o_ref.at[rows, :], "gr": gr_ref.at[:, rows]}
        if project:
            _inproj_project(hb_s.at[oth], w, z_oth)
        if norm:
            _inproj_norm(x_ref[rows, :], mod_ref[pl.ds(mod_in, 1), :], g1_ref[...], hb_s.at[cur])
        if finish:
            _inproj_finish(z_cur, c, out, seg)
        if half == 0:
            for src, dst in zip(cast_in, cast_out):
                dst[...] = src[...].astype(BF16)
            if ada_tail:
                _ada_kernel(*ada_in, ada_out)

    _pipeline_step(tick)


def _inproj(x, mod, mod_row0, rows_per_mod, seg, p, cast=(), ada_tail=None):
    n = x.shape[0]
    tb = TOKEN_BLOCK
    tb2 = 2 * tb
    assert tb % seg == 0 and n % tb2 == 0 and rows_per_mod % tb2 == 0
    nb = n // tb2
    blk_in = lambda i: jnp.minimum(i, nb - 1)
    blk_out = lambda i: jnp.maximum(i - 1, 0)
    tok = lambda width: pl.BlockSpec((tb2, width), lambda i: (blk_out(i), 0))
    tok_t = lambda height: pl.BlockSpec((height, tb2), lambda i: (0, blk_out(i)))
    in_specs = [
        pl.BlockSpec((tb2, D_MODEL), lambda i: (blk_in(i), 0)),
        _const_spec(mod.shape),
        _const_spec((1, D_MODEL)),
        _const_spec((D_IN, D_MODEL)),
        _const_spec((N_GATE, 1)),
        _const_spec((3, 2 * D_MLSTM)),
        _const_spec((1, 2 * D_MLSTM)),
        _const_spec((GROUPS, CHUNK, CHUNK)),
        _const_spec((CHUNK, D_GMLP)),
        _const_spec((1, D_GMLP)),
    ]
    out_shape = [
        jax.ShapeDtypeStruct((n, D_GMLP), BF16),
        jax.ShapeDtypeStruct((D_MLSTM, n), BF16),
        jax.ShapeDtypeStruct((n, D_MLSTM), BF16),
        jax.ShapeDtypeStruct((D_MLSTM, n), BF16),
        jax.ShapeDtypeStruct((n, D_MLSTM), F32),
        jax.ShapeDtypeStruct((GR_ROWS, n), F32),
    ]
    out_specs = [tok(D_GMLP), tok_t(D_MLSTM), tok(D_MLSTM), tok_t(D_MLSTM), tok(D_MLSTM),
                 tok_t(GR_ROWS)]
    for wf in cast:
        rows = wf.shape[0] // nb
        assert rows * nb == wf.shape[0] and rows % 16 == 0
        slab = pl.BlockSpec((rows, wf.shape[1]), lambda i: (blk_in(i), 0))
        in_specs.append(slab)
        out_specs.append(slab)
        out_shape.append(jax.ShapeDtypeStruct(wf.shape, BF16))
    side = list(cast)
    if ada_tail is not None:
        cs, w_ada, b_ada, col0 = ada_tail
        width = (w_ada.shape[1] - col0) // nb
        assert col0 % width == 0 and width * nb == w_ada.shape[1] - col0 and width % 128 == 0
        in_specs += [
            _const_spec(cs.shape),
            pl.BlockSpec((w_ada.shape[0], width), lambda i: (0, col0 // width + blk_in(i))),
            pl.BlockSpec((1, width), lambda i: (0, col0 // width + blk_in(i))),
        ]
        out_specs.append(pl.BlockSpec((cs.shape[0], width), lambda i: (0, blk_in(i))))
        out_shape.append(jax.ShapeDtypeStruct((cs.shape[0], w_ada.shape[1] - col0), F32))
        side += [cs, w_ada, b_ada]
    scratch_shapes = [
        pltpu.VMEM((2, tb, D_MODEL), BF16),
        pltpu.VMEM((2, tb, D_GMLP), F32),
        pltpu.VMEM((2, tb, D_GMLP), F32),
        pltpu.VMEM((2, tb, 2 * D_MLSTM), F32),
        pltpu.VMEM((2, tb, D_MLSTM), F32),
        pltpu.VMEM((2, D_MLSTM, tb), BF16),
        pltpu.VMEM((2, N_GATE, tb), F32),
    ]
    return pl.pallas_call(
        functools.partial(_inproj_kernel, seg=seg, blocks_per_mod=rows_per_mod // tb2,
                          mod_row0=mod_row0, n_cast=len(cast), ada_tail=ada_tail is not None),
        out_shape=out_shape,
        grid=(nb + 1,),
        in_specs=in_specs,
        out_specs=out_specs,
        scratch_shapes=scratch_shapes,
        compiler_params=pltpu.CompilerParams(
            dimension_semantics=("arbitrary",), vmem_limit_bytes=VMEM_LIMIT_BYTES),
        name="inproj",
    )(x, mod, p["g1"], p["wit"],
      p["bgt"], p["cw"], p["cb"], p["ws"], p["bs"], p["gv"], *side)


def _lane_broadcast_column(row):
    n = row.shape[1]
    tiles = [jnp.transpose(jnp.broadcast_to(row[:, t * HD:(t + 1) * HD], (HD, HD)))
             for t in range(n // HD)]
    col = jnp.concatenate(tiles, axis=0)
    return jnp.concatenate([col] * (n // HD), axis=1)


def _mlstm_unit(k, vt, kq, qs, d_row, b_row, g_row, bl_row, gmax_row, st, m_row, mask, carried):
    logw = jnp.where(mask, _lane_broadcast_column(d_row) + b_row, NEG)
    a = b_row + m_row
    mj = jnp.maximum(a, jnp.max(logw, axis=0, keepdims=True))
    w = jnp.exp2(logw - mj)
    s = kq * w
    num = _dot(vt, s.astype(BF16))
    den = jnp.sum(s, axis=0, keepdims=True)
    if carried:
        inter = jnp.exp2(a - mj)
        num = num + inter * qs[:HD]
        den = den + inter * qs[HD:HD + 1]
    h = num * (1.0 / jnp.maximum(jnp.abs(den), jnp.exp2(-mj)))
    m_new = jnp.maximum(bl_row + m_row, gmax_row)
    wc = jnp.exp2(g_row - m_new)
    pad_row = lax.broadcasted_iota(jnp.int32, (ST_ROWS - HD, wc.shape[1]), 0)
    v_aug = jnp.concatenate(
        [vt.astype(F32) * wc, jnp.where(pad_row == 0, wc, 0.0)], axis=0).astype(BF16)
    st_new = _dot(v_aug, k)
    if carried:
        decay = jnp.exp2(bl_row + m_row - m_new)
        st_new = decay[:, :HD] * st + st_new
    return h, st_new, m_new


def _mlstm_kernel(*refs, has_init, has_out, rounds, by_sequence):
    if has_init:
        s0_ref, m0_ref = refs[:2]
        refs = refs[2:]
    if by_sequence:
        k_ref, qt_ref, vt_ref, gr_ref = refs[:4]
        fwd_in = bwd_in = (k_ref, qt_ref, vt_ref, gr_ref)
        refs = refs[4:]
    else:
        fwd_in, bwd_in = refs[:4], refs[4:8]
        refs = refs[8:]
    hf_ref, hb_ref = refs[:2]
    refs = refs[2:]
    if has_out:
        co_ref, no_ref, mo_ref = refs[:3]
        refs = refs[3:]
    s_ref, m_ref = refs
    c = pl.program_id(1)
    nc = pl.num_programs(1)
    carried = not by_sequence

    if carried:
        @pl.when(c == 0)
        def _():
            if has_init:
                s_ref[...] = s0_ref[0]
                m_ref[...] = m0_ref[0]
            else:
                s_ref[...] = jnp.zeros_like(s_ref)
                m_ref[...] = jnp.zeros_like(m_ref)

    si = lax.broadcasted_iota(jnp.int32, (SCAN, SCAN), 0)
    ji = lax.broadcasted_iota(jnp.int32, (SCAN, SCAN), 1)
    dirs = ((fwd_in, hf_ref, si <= ji), (bwd_in, hb_ref, si >= ji))

    def chunk_cols(d, r):
        pos = rounds - 1 - r if (d == 1 and not by_sequence) else r
        return slice(pos * SCAN, (pos + 1) * SCAN)

    kq = {}
    for r in range(rounds):
        for d, ((k_ref, qt_ref, _, _), _, _) in enumerate(dirs):
            cs = chunk_cols(d, r)
            for hd in range(HEADS):
                hs = slice(hd * HD, (hd + 1) * HD)
                kq[r, d, hd] = _dot(k_ref[cs, hs], qt_ref[hs, cs])

    m_cur = None
    if carried:
        m_all = m_ref[...]
        m_cur = [m_all[u:u + 1] for u in range(N_UNIT)]
    zero_row = jnp.zeros((1, SCAN), F32)
    for r in range(rounds):
        slot = r * N_UNIT if by_sequence else 0
        for d, ((k_ref, qt_ref, vt_ref, gr_ref), h_ref, mask) in enumerate(dirs):
            cs = chunk_cols(d, r)
            for hd in range(HEADS):
                u = d * HEADS + hd
                row = d * 2 * HEADS + HEADS + hd
                hs = slice(hd * HD, (hd + 1) * HD)
                st = s_ref[slot + u] if carried else None
                qs = _dot(st.astype(BF16), qt_ref[hs, cs]) if carried else None
                h, st_new, m_new = _mlstm_unit(
                    k_ref[cs, hs], vt_ref[hs, cs], kq[r, d, hd], qs,
                    *(gr_ref[sec * N_GATE + row:sec * N_GATE + row + 1, cs]
                      for sec in (4, 0, 1, 2, 3)),
                    st, m_cur[u] if carried else zero_row, mask, carried)
                h_ref[hs, cs] = h
                s_ref[slot + u] = st_new
                if carried:
                    m_cur[u] = m_new
                else:
                    m_ref[slot + u:slot + u + 1, :] = m_new
    if carried:
        m_ref[...] = jnp.concatenate(m_cur, axis=0)

    if has_out:
        @pl.when(c == nc - 1)
        def _():
            for j in range(s_ref.shape[0]):
                q, u = divmod(j, N_UNIT)
                co_ref[q, u] = jnp.transpose(s_ref[j, 0:HD, :])
                no_ref[q, u:u + 1, :] = s_ref[j, HD:HD + 1, :]
                mo_ref[q, u:u + 1, :] = m_ref[j:j + 1, :]


def _mlstm(k, qt, vt, gr, batch, rounds, s0=None, m0=None, want_state=False):
    n = k.shape[0]
    nc = n // batch // SCAN
    by_sequence = nc == 1
    width = rounds * SCAN
    has_init = s0 is not None
    if by_sequence:
        assert batch % rounds == 0 and not has_init
        grid = (batch // rounds, 1)
        slots = rounds
        fwd = bwd = lambda b, c: b
    else:
        assert nc % rounds == 0 and not want_state
        steps = nc // rounds
        grid = (batch, steps)
        slots = 1
        fwd = lambda b, c: b * steps + c
        bwd = lambda b, c: b * steps + steps - 1 - c

    def specs(ix):
        return [
            pl.BlockSpec((width, D_MLSTM), lambda b, c: (ix(b, c), 0)),
            pl.BlockSpec((D_MLSTM, width), lambda b, c: (0, ix(b, c))),
            pl.BlockSpec((D_MLSTM, width), lambda b, c: (0, ix(b, c))),
            pl.BlockSpec((GR_ROWS, width), lambda b, c: (0, ix(b, c))),
        ]

    in_specs = specs(fwd) if by_sequence else specs(fwd) + specs(bwd)
    args = [k, qt, vt, gr] if by_sequence else [k, qt, vt, gr, k, qt, vt, gr]
    if has_init:
        in_specs = [
            pl.BlockSpec((1, N_UNIT, ST_ROWS, HD), lambda b, c: (b, 0, 0, 0)),
            pl.BlockSpec((1, N_UNIT, SCAN), lambda b, c: (b, 0, 0)),
        ] + in_specs
        args = [s0, m0] + args
    out_shape = [
        jax.ShapeDtypeStruct((D_MLSTM, n), F32),
        jax.ShapeDtypeStruct((D_MLSTM, n), F32),
    ]
    out_specs = [
        pl.BlockSpec((D_MLSTM, width), lambda b, c: (0, fwd(b, c))),
        pl.BlockSpec((D_MLSTM, width), lambda b, c: (0, bwd(b, c))),
    ]
    if want_state:
        out_shape += [
            jax.ShapeDtypeStruct((batch, N_UNIT, HD, HD), F32),
            jax.ShapeDtypeStruct((batch, N_UNIT, HD), F32),
            jax.ShapeDtypeStruct((batch, N_UNIT, SCAN), F32),
        ]
        out_specs += [
            pl.BlockSpec((slots, N_UNIT, HD, HD), lambda b, c: (b, 0, 0, 0)),
            pl.BlockSpec((slots, N_UNIT, HD), lambda b, c: (b, 0, 0)),
            pl.BlockSpec((slots, N_UNIT, SCAN), lambda b, c: (b, 0, 0)),
        ]
    return pl.pallas_call(
        functools.partial(_mlstm_kernel, has_init=has_init, has_out=want_state, rounds=rounds,
                          by_sequence=by_sequence),
        out_shape=out_shape,
        grid=grid,
        in_specs=in_specs,
        out_specs=out_specs,
        scratch_shapes=[pltpu.VMEM((slots * N_UNIT, ST_ROWS, HD), F32),
                        pltpu.VMEM((slots * N_UNIT, SCAN), F32)],
        compiler_params=pltpu.CompilerParams(
            dimension_semantics=("arbitrary", "arbitrary"), vmem_limit_bytes=VMEM_LIMIT_BYTES),
        name="mlstm",
    )(*args)


def _outffn_mix(x, a, hft, hbt, o, mod, c, x1_ref, h2_ref):
    ga1 = mod[:, 0:D_MODEL]
    sh2 = mod[:, D_MODEL:2 * D_MODEL]
    sc2 = mod[:, 2 * D_MODEL:3 * D_MODEL]
    hs = jnp.transpose(hft + hbt)
    sig = _sigmoid(o)
    parts = []
    for hd in range(HEADS):
        sl = slice(hd * HD, (hd + 1) * HD)
        parts.append(_rms(hs[:, sl], c["gh"][:, sl]) * sig[:, sl])
    b_out = jnp.concatenate(parts, axis=-1).astype(BF16)
    mix = _dot(a, c["wout"][0:D_GMLP, :]) + _dot(b_out, c["wout"][D_GMLP:, :])
    x1 = x + ga1 * mix
    x1_ref[...] = x1
    h2_ref[...] = (_rms(x1, c["g2"][...] * (1.0 + sc2)) + sh2).astype(BF16)


def _outffn_up(h2_ref, c, f_ref):
    h2 = h2_ref[...]
    u = _dot(h2, c["w1"][...])
    g = _dot(h2, c["w3"][...])
    f_ref[...] = (u * _sigmoid(u) * g).astype(BF16)


def _outffn_down(f_ref, x1_ref, mod, c, y_ref):
    ga2 = mod[:, 3 * D_MODEL:4 * D_MODEL]
    x2 = x1_ref[...] + ga2 * _dot(f_ref[...], c["w2"][...])
    y_ref[...] = _rms(x2, c["gf"][...])


def _outffn_kernel(x_ref, a_ref, hft_ref, hbt_ref, o_ref, mod_ref, gh_ref, wout_hbm,
                   g2_ref, w1_hbm, w3_hbm, w2_hbm, gf_ref, y_ref,
                   x1_s, h2_s, f_s, wout_s, w1_s, w3_s, w2_s, w_sem,
                   *, blocks_per_mod, mod_row0):
    tb = x_ref.shape[0] // 2
    i = pl.program_id(0)
    mod_in = mod_row0 + jnp.minimum(i, pl.num_programs(0) - 2) // blocks_per_mod
    mod_out = mod_row0 + jnp.maximum(i - 1, 0) // blocks_per_mod
    c = {"gh": gh_ref, "wout": wout_s, "g2": g2_ref, "w1": w1_s, "w3": w3_s, "w2": w2_s,
         "gf": gf_ref}
    weight_copy = {
        name: pltpu.make_async_copy(src, dst, w_sem.at[j])
        for j, (name, src, dst) in enumerate((("wout", wout_hbm, wout_s), ("w1", w1_hbm, w1_s),
                                              ("w3", w3_hbm, w3_s), ("w2", w2_hbm, w2_s)))}

    def tick(half, mix, up, down):
        rows = slice(half * tb, (half + 1) * tb)
        cur, oth = half, 1 - half
        first_tick = mix and not up and not down
        second_tick = mix and up and not down
        if first_tick:
            for name in ("wout", "w1", "w3", "w2"):
                weight_copy[name].start()
            weight_copy["wout"].wait()
        if down:
            _outffn_down(f_s.at[cur], x1_s.at[cur], mod_ref[pl.ds(mod_out, 1), :], c,
                         y_ref.at[rows, :])
        if mix:
            _outffn_mix(x_ref[rows, :], a_ref[rows, :], hft_ref[:, rows], hbt_ref[:, rows],
                        o_ref[rows, :], mod_ref[pl.ds(mod_in, 1), :], c, x1_s.at[cur],
                        h2_s.at[cur])
        if second_tick:
            weight_copy["w1"].wait()
            weight_copy["w3"].wait()
        if up:
            _outffn_up(h2_s.at[oth], c, f_s.at[oth])
        if second_tick:
            weight_copy["w2"].wait()

    _pipeline_step(tick)


def _outffn(x, a, hft, hbt, o, mod, mod_row0, rows_per_mod, p):
    n = x.shape[0]
    tb = TOKEN_BLOCK
    tb2 = 2 * tb
    assert n % tb2 == 0 and rows_per_mod % tb2 == 0
    nb = n // tb2
    d_ff = p["w1"].shape[1]
    blk_in = lambda i: jnp.minimum(i, nb - 1)
    blk_out = lambda i: jnp.maximum(i - 1, 0)
    tok = lambda w: pl.BlockSpec((tb2, w), lambda i: (blk_in(i), 0))
    tok_t = lambda h: pl.BlockSpec((h, tb2), lambda i: (0, blk_in(i)))
    weight_hbm = pl.BlockSpec(memory_space=pl.ANY)
    in_specs = [
        tok(D_MODEL), tok(D_GMLP), tok_t(D_MLSTM), tok_t(D_MLSTM), tok(D_MLSTM),
        _const_spec(mod.shape),
        _const_spec((1, D_MLSTM)),
        weight_hbm,
        _const_spec((1, D_MODEL)),
        weight_hbm,
        weight_hbm,
        weight_hbm,
        _const_spec((1, D_MODEL)),
    ]
    scratch_shapes = [
        pltpu.VMEM((2, tb, D_MODEL), F32),
        pltpu.VMEM((2, tb, D_MODEL), BF16),
        pltpu.VMEM((2, tb, d_ff), BF16),
        pltpu.VMEM(p["wout"].shape, BF16),
        pltpu.VMEM(p["w1"].shape, BF16),
        pltpu.VMEM(p["w3"].shape, BF16),
        pltpu.VMEM(p["w2"].shape, BF16),
        pltpu.SemaphoreType.DMA((4,)),
    ]
    return pl.pallas_call(
        functools.partial(_outffn_kernel, blocks_per_mod=rows_per_mod // tb2, mod_row0=mod_row0),
        out_shape=jax.ShapeDtypeStruct((n, D_MODEL), F32),
        grid=(nb + 1,),
        in_specs=in_specs,
        out_specs=pl.BlockSpec((tb2, D_MODEL), lambda i: (blk_out(i), 0)),
        scratch_shapes=scratch_shapes,
        compiler_params=pltpu.CompilerParams(
            dimension_semantics=("arbitrary",), vmem_limit_bytes=VMEM_LIMIT_BYTES),
        name="outffn",
    )(x, a, hft, hbt, o, mod, p["gh"], p["wout"], p["g2"], p["w1"], p["w3"], p["w2"], p["gf"])


def _layer_params(l, g_norm1, w_in, b_gate, w_s, b_s, g_v, conv_w, conv_b, g_h, g_norm2, g_final):
    return {
        "g1": g_norm1[l][None, :],
        "wit": w_in[l].T.astype(BF16),
        "bgt": b_gate[l][:, None],
        "cw": conv_w[l], "cb": conv_b[l][None, :],
        "ws": w_s[l].astype(BF16),
        "bs": jnp.repeat(b_s[l].T, HD, axis=1),
        "gv": g_v[l].reshape(1, D_GMLP),
        "gh": g_h[l].reshape(1, D_MLSTM),
        "g2": g_norm2[l][None, :],
        "gf": g_final[None, :],
    }


_LATE_WEIGHTS = ("wout", "w1", "w3", "w2")


def _trunk_front(x, mod_row0, rows_per_mod, seg, p, late=None):
    cast, ada_tail = ((), None) if late is None else (
        tuple(late[0][name] for name in _LATE_WEIGHTS), late[1])
    a, qt, k, vt, o, gr, *side = _inproj(x, p["mod_head"], mod_row0, rows_per_mod, seg, p, cast,
                                         ada_tail)
    if late is not None:
        p = dict(p, **dict(zip(_LATE_WEIGHTS, side[:-1])))
        p["mod_tail"] = side[-1]
    return (a, qt, k, vt, o), gr, p


def _trunk_back(x, front, gr, mod_row0, rows_per_mod, batch, rounds, p, s0=None, m0=None,
                want_state=False):
    a, qt, k, vt, o = front
    hft, hbt, *state = _mlstm(k, qt, vt, gr, batch, rounds, s0, m0, want_state)
    y = _outffn(x, a, hft, hbt, o, p["mod_tail"], mod_row0, rows_per_mod, p)
    return y, state


def kernel(x_prompt, x_sample, state_C, state_n, state_m, c, c_ctx, w_ada, b_ada, g_norm1, w_in,
           b_gate, w_s, b_s, g_v, conv_w, conv_b, g_h, w_out, g_norm2, w1, w3, w2, g_final):
    bp, tp, d = x_prompt.shape
    bs_, ts, _ = x_sample.shape
    depth = w_in.shape[0]
    assert depth == 1, "final norm is fused into the layer's last kernel"
    xp = x_prompt.reshape(bp * tp, d)
    xs = x_sample.reshape(bs_ * ts, d)

    cs = jnp.zeros((8, d), F32).at[0].set(c_ctx).at[1:1 + bs_].set(c)
    new_c, new_n, new_m = [], [], []
    for l in range(depth):
        p = _layer_params(l, g_norm1, w_in, b_gate, w_s, b_s, g_v, conv_w, conv_b, g_h, g_norm2,
                          g_final)
        late_f32 = {"wout": w_out[l], "w1": w1[l], "w3": w3[l], "w2": w2[l]}
        p["mod_head"] = _ada(cs, w_ada[l], b_ada[l][None, :], 2 * d)
        ada_tail = (cs, w_ada[l], b_ada[l][None, :], 2 * d)

        front_lat, gr_lat, p = _trunk_front(xs, 1, ts, GRID_W, p, (late_f32, ada_tail))
        front_ctx, gr_ctx, _ = _trunk_front(xp, 0, bp * tp, tp, p)

        xp, (c_ctx_out, n_ctx_out, m_ctx_out) = _trunk_back(
            xp, front_ctx, gr_ctx, 0, bp * tp, bp, CTX_SEQS_PER_STEP, p, want_state=True)
        new_c.append(c_ctx_out.reshape(bp, N_DIR, HEADS, HD, HD))
        new_n.append(n_ctx_out.reshape(bp, N_DIR, HEADS, HD))
        new_m.append((m_ctx_out[..., 0] * (1.0 / LOG2E)).reshape(bp, N_DIR, HEADS))

        s0 = jnp.concatenate(
            [jnp.swapaxes(state_C[:, l], -1, -2), state_n[:, l][..., None, :],
             jnp.zeros((bs_, N_DIR, HEADS, ST_ROWS - HD - 1, HD), F32)],
            axis=-2).reshape(bs_, N_UNIT, ST_ROWS, HD)
        m0 = jnp.broadcast_to((state_m[:, l] * LOG2E).reshape(bs_, N_UNIT, 1),
                              (bs_, N_UNIT, SCAN))
        xs, _ = _trunk_back(xs, front_lat, gr_lat, 1, ts, bs_, LAT_CHUNKS_PER_STEP, p, s0=s0, m0=m0)

    return (xp.reshape(bp, tp, d), xs.reshape(bs_, ts, d),
            jnp.stack(new_c, axis=1), jnp.stack(new_n, axis=1), jnp.stack(new_m, axis=1))
```

```python
import functools

import jax
import jax.numpy as jnp
from jax import lax
from jax.experimental import pallas as pl
from jax.experimental.pallas import tpu as pltpu

D_MODEL = 1024
D_GMLP = 512
D_MLSTM = 512
GROUPS = 4
HEADS = 4
HD = 128
CHUNK = 128
SCAN = 256
N_DIR = 2
N_UNIT = N_DIR * HEADS
GRID_W = 64
OFF_V, OFF_Q, OFF_VV, OFF_O, OFF_G = 512, 1024, 2048, 2560, 3072
D_IN = OFF_G + 2 * N_DIR * HEADS
EPS = 1e-6
NEG = -1e30
ST_ROWS = HD + 16
N_GATE = 2 * N_UNIT
GR_ROWS = 5 * N_GATE

TOKEN_BLOCK = 256
CTX_SEQS_PER_STEP = 2
LAT_CHUNKS_PER_STEP = 2
VMEM_LIMIT_BYTES = 56 * 1024 * 1024

F32 = jnp.float32
BF16 = jnp.bfloat16


def _rms(x, g):
    n = x.shape[-1]
    return x * lax.rsqrt(jnp.sum(x * x, axis=-1, keepdims=True) + n * EPS) * (g * n ** 0.5)


LOG2E = 1.4426950408889634
GELU_K = 0.7978845608028654
GELU_C = 0.044715


def _sigmoid(x):
    return 1.0 / (1.0 + jnp.exp2(x * (-LOG2E)))


def _gelu_tanh(x):
    t = (x * x) * (-2.0 * LOG2E * GELU_K * GELU_C) + (-2.0 * LOG2E * GELU_K)
    return x * (1.0 / (1.0 + jnp.exp2(x * t)))


def _log_sigmoid(x):
    return jnp.minimum(x, 0.0) - jnp.log(1.0 + jnp.exp(-jnp.abs(x)))


def _dot(a, b):
    return jnp.dot(a, b, preferred_element_type=F32)


def _dot_nt(a, b):
    return lax.dot_general(a, b, (((1,), (1,)), ((), ())), preferred_element_type=F32)


def _dot_exact(a, b):
    return jnp.dot(a, b, preferred_element_type=F32, precision=lax.Precision.HIGHEST)


def _pipeline_step(tick):
    i = pl.program_id(0)
    last = pl.num_programs(0) - 1

    @pl.when(i == 0)
    def _():
        tick(0, True, False, False)
        tick(1, True, True, False)

    @pl.when(jnp.logical_and(i > 0, i < last))
    def _():
        tick(0, True, True, True)
        tick(1, True, True, True)

    @pl.when(i == last)
    def _():
        tick(0, False, True, True)
        tick(1, False, False, True)


def _const_spec(shape):
    zeros = (0,) * len(shape)
    return pl.BlockSpec(shape, lambda *_: zeros, pipeline_mode=pl.Buffered(1))


def _ada_kernel(c_ref, w_ref, b_ref, o_ref):
    c = c_ref[...]
    s = (c * _sigmoid(c)).astype(BF16)
    o_ref[...] = _dot(s, w_ref[...].astype(BF16)) + b_ref[...]


def _ada(cs, w_ada, b_ada, n):
    rows, d = cs.shape
    tn = 1024
    return pl.pallas_call(
        _ada_kernel,
        out_shape=jax.ShapeDtypeStruct((rows, n), F32),
        grid=(n // tn,),
        in_specs=[
            pl.BlockSpec((rows, d), lambda j: (0, 0)),
            pl.BlockSpec((d, tn), lambda j: (0, j)),
            pl.BlockSpec((1, tn), lambda j: (0, j)),
        ],
        out_specs=pl.BlockSpec((rows, tn), lambda j: (0, j)),
        compiler_params=pltpu.CompilerParams(
            dimension_semantics=("arbitrary",), vmem_limit_bytes=VMEM_LIMIT_BYTES),
        name="ada",
    )(cs, w_ada, b_ada)


def _inproj_norm(x, mod, g1, hb_ref):
    sh1 = mod[:, 0:D_MODEL]
    sc1 = mod[:, D_MODEL:2 * D_MODEL]
    hb_ref[...] = (_rms(x, g1 * (1.0 + sc1)) + sh1).astype(BF16)


def _inproj_project(hb_ref, w, z):
    hb = hb_ref[...]
    z["u"][...] = _dot_nt(hb, w["u"][...])
    z["v"][...] = _dot_nt(hb, w["v"][...])
    z["qk"][...] = _dot_nt(hb, w["qk"][...])
    z["o"][...] = _dot_nt(hb, w["o"][...])
    z["vt"][...] = _dot_nt(w["vv"][...], hb).astype(BF16)
    z["gt"][...] = _dot_nt(w["gt"][...], hb)


def _inproj_finish(z, c, out, seg):
    tb = z["u"].shape[0]
    for g in range(GROUPS):
        gs = slice(g * HD, (g + 1) * HD)
        vg = _rms(_gelu_tanh(z["v"][:, gs]), c["gv"][:, gs]).astype(BF16)
        for ch in range(tb // CHUNK):
            cs = slice(ch * CHUNK, (ch + 1) * CHUNK)
            mixed = _dot(c["ws"][g], vg[cs]) + c["bs"][:, gs]
            out["a"][cs, gs] = (_gelu_tanh(z["u"][cs, gs]) * mixed).astype(BF16)

    zqk = z["qk"][...]
    pos = lax.broadcasted_iota(jnp.int32, (tb, 1), 0) % seg
    prev = jnp.where(pos != 0, pltpu.roll(zqk, 1, 0), 0.0)
    nxt = jnp.where(pos != seg - 1, pltpu.roll(zqk, tb - 1, 0), 0.0)
    cw = c["cw"]
    y = c["cb"][...] + prev * cw[0:1, :] + zqk * cw[1:2, :] + nxt * cw[2:3, :]
    y = y * _sigmoid(y)
    out["qt"][...] = jnp.transpose(y[:, :D_MLSTM]).astype(BF16)
    out["k"][...] = (y[:, D_MLSTM:] * (HD ** -0.5)).astype(BF16)
    out["vt"][...] = z["vt"][...]
    out["o"][...] = z["o"][...]

    assert tb == SCAN
    gt = z["gt"][...] + c["bgt"][...]
    gi = pltpu.roll(gt, HEADS, 0)
    lf = _log_sigmoid(gt)
    lane = lax.broadcasted_iota(jnp.int32, (N_GATE, HD), 1)
    tiles = []
    carry = jnp.zeros((N_GATE, HD), F32)
    for t in range(SCAN // HD):
        prefix = lf[:, t * HD:(t + 1) * HD]
        shift = 1
        while shift < HD:
            prefix = prefix + jnp.where(lane >= shift, pltpu.roll(prefix, shift, 1), 0.0)
            shift *= 2
        tiles.append(prefix + carry)
        carry = carry + jnp.broadcast_to(prefix[:, HD - 1:HD], (N_GATE, HD))
    prefix = jnp.concatenate(tiles, axis=1)
    b_last = jnp.concatenate([carry] * (SCAN // HD), axis=1)
    row_bwd = lax.broadcasted_iota(jnp.int32, (N_GATE, SCAN), 0) >= N_GATE // N_DIR
    b_row = jnp.where(row_bwd, b_last - prefix + lf, prefix)
    g_row = b_last - b_row + gi
    g_max = jnp.broadcast_to(jnp.max(g_row, axis=1, keepdims=True), (N_GATE, SCAN))
    for sec, stat in enumerate((b_row, g_row, b_last, g_max, gi - b_row)):
        out["gr"][sec * N_GATE:(sec + 1) * N_GATE, :] = stat * LOG2E


_INPROJ_Z = ("u", "v", "qk", "o", "vt", "gt")


def _inproj_kernel(*refs, seg, blocks_per_mod, mod_row0, n_cast, ada_tail):
    refs = list(refs)
    take = lambda k: [refs.pop(0) for _ in range(k)]
    (x_ref, mod_ref, g1_ref, wit_ref,
     bgt_ref, cw_ref, cb_ref, ws_ref, bs_ref, gv_ref) = take(10)
    cast_in = take(n_cast)
    ada_in = take(3) if ada_tail else None
    a_ref, qt_ref, k_ref, vt_ref, o_ref, gr_ref = take(6)
    cast_out = take(n_cast)
    ada_out = take(1)[0] if ada_tail else None
    hb_s, *z_s = refs
    tb = x_ref.shape[0] // 2
    mod_in = mod_row0 + jnp.minimum(pl.program_id(0), pl.num_programs(0) - 2) // blocks_per_mod
    w = {"u": wit_ref.at[0:OFF_V, :], "v": wit_ref.at[OFF_V:OFF_Q, :],
         "qk": wit_ref.at[OFF_Q:OFF_VV, :], "vv": wit_ref.at[OFF_VV:OFF_O, :],
         "o": wit_ref.at[OFF_O:OFF_G, :], "gt": wit_ref.at[OFF_G:D_IN, :]}
    c = {"gv": gv_ref, "ws": ws_ref, "bs": bs_ref, "cw": cw_ref, "cb": cb_ref, "bgt": bgt_ref}

    def tick(half, norm, project, finish):
        rows = slice(half * tb, (half + 1) * tb)
        cur, oth = half, 1 - half
        z_cur = {name: ref.at[cur] for name, ref in zip(_INPROJ_Z, z_s)}
        z_oth = {name: ref.at[oth] for name, ref in zip(_INPROJ_Z, z_s)}
        out = {"a": a_ref.at[rows, :], "qt": qt_ref.at[:, rows], "k": k_ref.at[rows, :],
               "vt": vt_ref.at[:, rows], "o": o_ref.at[rows, :], "gr": gr_ref.at[:, rows]}
        if project:
            _inproj_project(hb_s.at[oth], w, z_oth)
        if norm:
            _inproj_norm(x_ref[rows, :], mod_ref[pl.ds(mod_in, 1), :], g1_ref[...], hb_s.at[cur])
        if finish:
            _inproj_finish(z_cur, c, out, seg)
        if half == 0:
            for src, dst in zip(cast_in, cast_out):
                dst[...] = src[...].astype(BF16)
            if ada_tail:
                _ada_kernel(*ada_in, ada_out)

    _pipeline_step(tick)


def _inproj(x, mod, mod_row0, rows_per_mod, seg, p, cast=(), ada_tail=None):
    n = x.shape[0]
    tb = TOKEN_BLOCK
    tb2 = 2 * tb
    assert tb % seg == 0 and n % tb2 == 0 and rows_per_mod % tb2 == 0
    nb = n // tb2
    blk_in = lambda i: jnp.minimum(i, nb - 1)
    blk_out = lambda i: jnp.maximum(i - 1, 0)
    tok = lambda width: pl.BlockSpec((tb2, width), lambda i: (blk_out(i), 0))
    tok_t = lambda height: pl.BlockSpec((height, tb2), lambda i: (0, blk_out(i)))
    in_specs = [
        pl.BlockSpec((tb2, D_MODEL), lambda i: (blk_in(i), 0)),
        _const_spec(mod.shape),
        _const_spec((1, D_MODEL)),
        _const_spec((D_IN, D_MODEL)),
        _const_spec((N_GATE, 1)),
        _const_spec((3, 2 * D_MLSTM)),
        _const_spec((1, 2 * D_MLSTM)),
        _const_spec((GROUPS, CHUNK, CHUNK)),
        _const_spec((CHUNK, D_GMLP)),
        _const_spec((1, D_GMLP)),
    ]
    out_shape = [
        jax.ShapeDtypeStruct((n, D_GMLP), BF16),
        jax.ShapeDtypeStruct((D_MLSTM, n), BF16),
        jax.ShapeDtypeStruct((n, D_MLSTM), BF16),
        jax.ShapeDtypeStruct((D_MLSTM, n), BF16),
        jax.ShapeDtypeStruct((n, D_MLSTM), F32),
        jax.ShapeDtypeStruct((GR_ROWS, n), F32),
    ]
    out_specs = [tok(D_GMLP), tok_t(D_MLSTM), tok(D_MLSTM), tok_t(D_MLSTM), tok(D_MLSTM),
                 tok_t(GR_ROWS)]
    for wf in cast:
        rows = wf.shape[0] // nb
        assert rows * nb == wf.shape[0] and rows % 16 == 0
        slab = pl.BlockSpec((rows, wf.shape[1]), lambda i: (blk_in(i), 0))
        in_specs.append(slab)
        out_specs.append(slab)
        out_shape.append(jax.ShapeDtypeStruct(wf.shape, BF16))
    side = list(cast)
    if ada_tail is not None:
        cs, w_ada, b_ada, col0 = ada_tail
        width = (w_ada.shape[1] - col0) // nb
        assert col0 % width == 0 and width * nb == w_ada.shape[1] - col0 and width % 128 == 0
        in_specs += [
            _const_spec(cs.shape),
            pl.BlockSpec((w_ada.shape[0], width), lambda i: (0, col0 // width + blk_in(i))),
            pl.BlockSpec((1, width), lambda i: (0, col0 // width + blk_in(i))),
        ]
        out_specs.append(pl.BlockSpec((cs.shape[0], width), lambda i: (0, blk_in(i))))
        out_shape.append(jax.ShapeDtypeStruct((cs.shape[0], w_ada.shape[1] - col0), F32))
        side += [cs, w_ada, b_ada]
    scratch_shapes = [
        pltpu.VMEM((2, tb, D_MODEL), BF16),
        pltpu.VMEM((2, tb, D_GMLP), F32),
        pltpu.VMEM((2, tb, D_GMLP), F32),
        pltpu.VMEM((2, tb, 2 * D_MLSTM), F32),
        pltpu.VMEM((2, tb, D_MLSTM), F32),
        pltpu.VMEM((2, D_MLSTM, tb), BF16),
        pltpu.VMEM((2, N_GATE, tb), F32),
    ]
    return pl.pallas_call(
        functools.partial(_inproj_kernel, seg=seg, blocks_per_mod=rows_per_mod // tb2,
                          mod_row0=mod_row0, n_cast=len(cast), ada_tail=ada_tail is not None),
        out_shape=out_shape,
        grid=(nb + 1,),
        in_specs=in_specs,
        out_specs=out_specs,
        scratch_shapes=scratch_shapes,
        compiler_params=pltpu.CompilerParams(
            dimension_semantics=("arbitrary",), vmem_limit_bytes=VMEM_LIMIT_BYTES),
        name="inproj",
    )(x, mod, p["g1"], p["wit"],
      p["bgt"], p["cw"], p["cb"], p["ws"], p["bs"], p["gv"], *side)


def _lane_broadcast_column(row):
    n = row.shape[1]
    tiles = [jnp.transpose(jnp.broadcast_to(row[:, t * HD:(t + 1) * HD], (HD, HD)))
             for t in range(n // HD)]
    col = jnp.concatenate(tiles, axis=0)
    return jnp.concatenate([col] * (n // HD), axis=1)


def _mlstm_unit(k, vt, kq, qs, d_row, b_row, g_row, bl_row, gmax_row, st, m_row, mask, carried):
    logw = jnp.where(mask, _lane_broadcast_column(d_row) + b_row, NEG)
    a = b_row + m_row
    mj = jnp.maximum(a, jnp.max(logw, axis=0, keepdims=True))
    w = jnp.exp2(logw - mj)
    s = kq * w
    num = _dot(vt, s.astype(BF16))
    den = jnp.sum(s, axis=0, keepdims=True)
    if carried:
        inter = jnp.exp2(a - mj)
        num = num + inter * qs[:HD]
        den = den + inter * qs[HD:HD + 1]
    h = num * (1.0 / jnp.maximum(jnp.abs(den), jnp.exp2(-mj)))
    m_new = jnp.maximum(bl_row + m_row, gmax_row)
    wc = jnp.exp2(g_row - m_new)
    pad_row = lax.broadcasted_iota(jnp.int32, (ST_ROWS - HD, wc.shape[1]), 0)
    v_aug = jnp.concatenate(
        [vt.astype(F32) * wc, jnp.where(pad_row == 0, wc, 0.0)], axis=0).astype(BF16)
    st_new = _dot(v_aug, k)
    if carried:
        decay = jnp.exp2(bl_row + m_row - m_new)
        st_new = decay[:, :HD] * st + st_new
    return h, st_new, m_new


def _mlstm_kernel(*refs, has_init, has_out, rounds, by_sequence):
    if has_init:
        s0_ref, m0_ref = refs[:2]
        refs = refs[2:]
    if by_sequence:
        k_ref, qt_ref, vt_ref, gr_ref = refs[:4]
        fwd_in = bwd_in = (k_ref, qt_ref, vt_ref, gr_ref)
        refs = refs[4:]
    else:
        fwd_in, bwd_in = refs[:4], refs[4:8]
        refs = refs[8:]
    hf_ref, hb_ref = refs[:2]
    refs = refs[2:]
    if has_out:
        co_ref, no_ref, mo_ref = refs[:3]
        refs = refs[3:]
    s_ref, m_ref = refs
    c = pl.program_id(1)
    nc = pl.num_programs(1)
    carried = not by_sequence

    if carried:
        @pl.when(c == 0)
        def _():
            if has_init:
                s_ref[...] = s0_ref[0]
                m_ref[...] = m0_ref[0]
            else:
                s_ref[...] = jnp.zeros_like(s_ref)
                m_ref[...] = jnp.zeros_like(m_ref)

    si = lax.broadcasted_iota(jnp.int32, (SCAN, SCAN), 0)
    ji = lax.broadcasted_iota(jnp.int32, (SCAN, SCAN), 1)
    dirs = ((fwd_in, hf_ref, si <= ji), (bwd_in, hb_ref, si >= ji))

    def chunk_cols(d, r):
        pos = rounds - 1 - r if (d == 1 and not by_sequence) else r
        return slice(pos * SCAN, (pos + 1) * SCAN)

    kq = {}
    for r in range(rounds):
        for d, ((k_ref, qt_ref, _, _), _, _) in enumerate(dirs):
            cs = chunk_cols(d, r)
            for hd in range(HEADS):
                hs = slice(hd * HD, (hd + 1) * HD)
                kq[r, d, hd] = _dot(k_ref[cs, hs], qt_ref[hs, cs])

    m_cur = None
    if carried:
        m_all = m_ref[...]
        m_cur = [m_all[u:u + 1] for u in range(N_UNIT)]
    zero_row = jnp.zeros((1, SCAN), F32)
    for r in range(rounds):
        slot = r * N_UNIT if by_sequence else 0
        for d, ((k_ref, qt_ref, vt_ref, gr_ref), h_ref, mask) in enumerate(dirs):
            cs = chunk_cols(d, r)
            for hd in range(HEADS):
                u = d * HEADS + hd
                row = d * 2 * HEADS + HEADS + hd
                hs = slice(hd * HD, (hd + 1) * HD)
                st = s_ref[slot + u] if carried else None
                qs = _dot(st.astype(BF16), qt_ref[hs, cs]) if carried else None
                h, st_new, m_new = _mlstm_unit(
                    k_ref[cs, hs], vt_ref[hs, cs], kq[r, d, hd], qs,
                    *(gr_ref[sec * N_GATE + row:sec * N_GATE + row + 1, cs]
                      for sec in (4, 0, 1, 2, 3)),
                    st, m_cur[u] if carried else zero_row, mask, carried)
                h_ref[hs, cs] = h
                s_ref[slot + u] = st_new
                if carried:
                    m_cur[u] = m_new
                else:
                    m_ref[slot + u:slot + u + 1, :] = m_new
    if carried:
        m_ref[...] = jnp.concatenate(m_cur, axis=0)

    if has_out:
        @pl.when(c == nc - 1)
        def _():
            for j in range(s_ref.shape[0]):
                q, u = divmod(j, N_UNIT)
                co_ref[q, u] = jnp.transpose(s_ref[j, 0:HD, :])
                no_ref[q, u:u + 1, :] = s_ref[j, HD:HD + 1, :]
                mo_ref[q, u:u + 1, :] = m_ref[j:j + 1, :]


def _mlstm(k, qt, vt, gr, batch, rounds, s0=None, m0=None, want_state=False):
    n = k.shape[0]
    nc = n // batch // SCAN
    by_sequence = nc == 1
    width = rounds * SCAN
    has_init = s0 is not None
    if by_sequence:
        assert batch % rounds == 0 and not has_init
        grid = (batch // rounds, 1)
        slots = rounds
        fwd = bwd = lambda b, c: b
    else:
        assert nc % rounds == 0 and not want_state
        steps = nc // rounds
        grid = (batch, steps)
        slots = 1
        fwd = lambda b, c: b * steps + c
        bwd = lambda b, c: b * steps + steps - 1 - c

    def specs(ix):
        return [
            pl.BlockSpec((width, D_MLSTM), lambda b, c: (ix(b, c), 0)),
            pl.BlockSpec((D_MLSTM, width), lambda b, c: (0, ix(b, c))),
            pl.BlockSpec((D_MLSTM, width), lambda b, c: (0, ix(b, c))),
            pl.BlockSpec((GR_ROWS, width), lambda b, c: (0, ix(b, c))),
        ]

    in_specs = specs(fwd) if by_sequence else specs(fwd) + specs(bwd)
    args = [k, qt, vt, gr] if by_sequence else [k, qt, vt, gr, k, qt, vt, gr]
    if has_init:
        in_specs = [
            pl.BlockSpec((1, N_UNIT, ST_ROWS, HD), lambda b, c: (b, 0, 0, 0)),
            pl.BlockSpec((1, N_UNIT, SCAN), lambda b, c: (b, 0, 0)),
        ] + in_specs
        args = [s0, m0] + args
    out_shape = [
        jax.ShapeDtypeStruct((D_MLSTM, n), F32),
        jax.ShapeDtypeStruct((D_MLSTM, n), F32),
    ]
    out_specs = [
        pl.BlockSpec((D_MLSTM, width), lambda b, c: (0, fwd(b, c))),
        pl.BlockSpec((D_MLSTM, width), lambda b, c: (0, bwd(b, c))),
    ]
    if want_state:
        out_shape += [
            jax.ShapeDtypeStruct((batch, N_UNIT, HD, HD), F32),
            jax.ShapeDtypeStruct((batch, N_UNIT, HD), F32),
            jax.ShapeDtypeStruct((batch, N_UNIT, SCAN), F32),
        ]
        out_specs += [
            pl.BlockSpec((slots, N_UNIT, HD, HD), lambda b, c: (b, 0, 0, 0)),
            pl.BlockSpec((slots, N_UNIT, HD), lambda b, c: (b, 0, 0)),
            pl.BlockSpec((slots, N_UNIT, SCAN), lambda b, c: (b, 0, 0)),
        ]
    return pl.pallas_call(
        functools.partial(_mlstm_kernel, has_init=has_init, has_out=want_state, rounds=rounds,
                          by_sequence=by_sequence),
        out_shape=out_shape,
        grid=grid,
        in_specs=in_specs,
        out_specs=out_specs,
        scratch_shapes=[pltpu.VMEM((slots * N_UNIT, ST_ROWS, HD), F32),
                        pltpu.VMEM((slots * N_UNIT, SCAN), F32)],
        compiler_params=pltpu.CompilerParams(
            dimension_semantics=("arbitrary", "arbitrary"), vmem_limit_bytes=VMEM_LIMIT_BYTES),
        name="mlstm",
    )(*args)


def _outffn_mix(x, a, hft, hbt, o, mod, c, x1_ref, h2_ref):
    ga1 = mod[:, 0:D_MODEL]
    sh2 = mod[:, D_MODEL:2 * D_MODEL]
    sc2 = mod[:, 2 * D_MODEL:3 * D_MODEL]
    hs = jnp.transpose(hft + hbt)
    sig = _sigmoid(o)
    parts = []
    for hd in range(HEADS):
        sl = slice(hd * HD, (hd + 1) * HD)
        parts.append(_rms(hs[:, sl], c["gh"][:, sl]) * sig[:, sl])
    b_out = jnp.concatenate(parts, axis=-1).astype(BF16)
    mix = _dot(a, c["wout"][0:D_GMLP, :]) + _dot(b_out, c["wout"][D_GMLP:, :])
    x1 = x + ga1 * mix
    x1_ref[...] = x1
    h2_ref[...] = (_rms(x1, c["g2"][...] * (1.0 + sc2)) + sh2).astype(BF16)


def _outffn_up(h2_ref, c, f_ref):
    h2 = h2_ref[...]
    u = _dot(h2, c["w1"][...])
    g = _dot(h2, c["w3"][...])
    f_ref[...] = (u * _sigmoid(u) * g).astype(BF16)


def _outffn_down(f_ref, x1_ref, mod, c, y_ref):
    ga2 = mod[:, 3 * D_MODEL:4 * D_MODEL]
    x2 = x1_ref[...] + ga2 * _dot(f_ref[...], c["w2"][...])
    y_ref[...] = _rms(x2, c["gf"][...])


def _outffn_kernel(x_ref, a_ref, hft_ref, hbt_ref, o_ref, mod_ref, gh_ref, wout_hbm,
                   g2_ref, w1_hbm, w3_hbm, w2_hbm, gf_ref, y_ref,
                   x1_s, h2_s, f_s, wout_s, w1_s, w3_s, w2_s, w_sem,
                   *, blocks_per_mod, mod_row0):
    tb = x_ref.shape[0] // 2
    i = pl.program_id(0)
    mod_in = mod_row0 + jnp.minimum(i, pl.num_programs(0) - 2) // blocks_per_mod
    mod_out = mod_row0 + jnp.maximum(i - 1, 0) // blocks_per_mod
    c = {"gh": gh_ref, "wout": wout_s, "g2": g2_ref, "w1": w1_s, "w3": w3_s, "w2": w2_s,
         "gf": gf_ref}
    weight_copy = {
        name: pltpu.make_async_copy(src, dst, w_sem.at[j])
        for j, (name, src, dst) in enumerate((("wout", wout_hbm, wout_s), ("w1", w1_hbm, w1_s),
                                              ("w3", w3_hbm, w3_s), ("w2", w2_hbm, w2_s)))}

    def tick(half, mix, up, down):
        rows = slice(half * tb, (half + 1) * tb)
        cur, oth = half, 1 - half
        first_tick = mix and not up and not down
        second_tick = mix and up and not down
        if first_tick:
            for name in ("wout", "w1", "w3", "w2"):
                weight_copy[name].start()
            weight_copy["wout"].wait()
        if down:
            _outffn_down(f_s.at[cur], x1_s.at[cur], mod_ref[pl.ds(mod_out, 1), :], c,
                         y_ref.at[rows, :])
        if mix:
            _outffn_mix(x_ref[rows, :], a_ref[rows, :], hft_ref[:, rows], hbt_ref[:, rows],
                        o_ref[rows, :], mod_ref[pl.ds(mod_in, 1), :], c, x1_s.at[cur],
                        h2_s.at[cur])
        if second_tick:
            weight_copy["w1"].wait()
            weight_copy["w3"].wait()
        if up:
            _outffn_up(h2_s.at[oth], c, f_s.at[oth])
        if second_tick:
            weight_copy["w2"].wait()

    _pipeline_step(tick)


def _outffn(x, a, hft, hbt, o, mod, mod_row0, rows_per_mod, p):
    n = x.shape[0]
    tb = TOKEN_BLOCK
    tb2 = 2 * tb
    assert n % tb2 == 0 and rows_per_mod % tb2 == 0
    nb = n // tb2
    d_ff = p["w1"].shape[1]
    blk_in = lambda i: jnp.minimum(i, nb - 1)
    blk_out = lambda i: jnp.maximum(i - 1, 0)
    tok = lambda w: pl.BlockSpec((tb2, w), lambda i: (blk_in(i), 0))
    tok_t = lambda h: pl.BlockSpec((h, tb2), lambda i: (0, blk_in(i)))
    weight_hbm = pl.BlockSpec(memory_space=pl.ANY)
    in_specs = [
        tok(D_MODEL), tok(D_GMLP), tok_t(D_MLSTM), tok_t(D_MLSTM), tok(D_MLSTM),
        _const_spec(mod.shape),
        _const_spec((1, D_MLSTM)),
        weight_hbm,
        _const_spec((1, D_MODEL)),
        weight_hbm,
        weight_hbm,
        weight_hbm,
        _const_spec((1, D_MODEL)),
    ]
    scratch_shapes = [
        pltpu.VMEM((2, tb, D_MODEL), F32),
        pltpu.VMEM((2, tb, D_MODEL), BF16),
        pltpu.VMEM((2, tb, d_ff), BF16),
        pltpu.VMEM(p["wout"].shape, BF16),
        pltpu.VMEM(p["w1"].shape, BF16),
        pltpu.VMEM(p["w3"].shape, BF16),
        pltpu.VMEM(p["w2"].shape, BF16),
        pltpu.SemaphoreType.DMA((4,)),
    ]
    return pl.pallas_call(
        functools.partial(_outffn_kernel, blocks_per_mod=rows_per_mod // tb2, mod_row0=mod_row0),
        out_shape=jax.ShapeDtypeStruct((n, D_MODEL), F32),
        grid=(nb + 1,),
        in_specs=in_specs,
        out_specs=pl.BlockSpec((tb2, D_MODEL), lambda i: (blk_out(i), 0)),
        scratch_shapes=scratch_shapes,
        compiler_params=pltpu.CompilerParams(
            dimension_semantics=("arbitrary",), vmem_limit_bytes=VMEM_LIMIT_BYTES),
        name="outffn",
    )(x, a, hft, hbt, o, mod, p["gh"], p["wout"], p["g2"], p["w1"], p["w3"], p["w2"], p["gf"])


def _layer_params(l, g_norm1, w_in, b_gate, w_s, b_s, g_v, conv_w, conv_b, g_h, g_norm2, g_final):
    return {
        "g1": g_norm1[l][None, :],
        "wit": w_in[l].T.astype(BF16),
        "bgt": b_gate[l][:, None],
        "cw": conv_w[l], "cb": conv_b[l][None, :],
        "ws": w_s[l].astype(BF16),
        "bs": jnp.repeat(b_s[l].T, HD, axis=1),
        "gv": g_v[l].reshape(1, D_GMLP),
        "gh": g_h[l].reshape(1, D_MLSTM),
        "g2": g_norm2[l][None, :],
        "gf": g_final[None, :],
    }


_LATE_WEIGHTS = ("wout", "w1", "w3", "w2")


def _trunk_front(x, mod_row0, rows_per_mod, seg, p, late=None):
    cast, ada_tail = ((), None) if late is None else (
        tuple(late[0][name] for name in _LATE_WEIGHTS), late[1])
    a, qt, k, vt, o, gr, *side = _inproj(x, p["mod_head"], mod_row0, rows_per_mod, seg, p, cast,
                                         ada_tail)
    if late is not None:
        p = dict(p, **dict(zip(_LATE_WEIGHTS, side[:-1])))
        p["mod_tail"] = side[-1]
    return (a, qt, k, vt, o), gr, p


def _trunk_back(x, front, gr, mod_row0, rows_per_mod, batch, rounds, p, s0=None, m0=None,
                want_state=False):
    a, qt, k, vt, o = front
    hft, hbt, *state = _mlstm(k, qt, vt, gr, batch, rounds, s0, m0, want_state)
    y = _outffn(x, a, hft, hbt, o, p["mod_tail"], mod_row0, rows_per_mod, p)
    return y, state


def kernel(x_prompt, x_sample, state_C, state_n, state_m, c, c_ctx, w_ada, b_ada, g_norm1, w_in,
           b_gate, w_s, b_s, g_v, conv_w, conv_b, g_h, w_out, g_norm2, w1, w3, w2, g_final):
    bp, tp, d = x_prompt.shape
    bs_, ts, _ = x_sample.shape
    depth = w_in.shape[0]
    assert depth == 1, "final norm is fused into the layer's last kernel"
    xp = x_prompt.reshape(bp * tp, d)
    xs = x_sample.reshape(bs_ * ts, d)

    cs = jnp.zeros((8, d), F32).at[0].set(c_ctx).at[1:1 + bs_].set(c)
    new_c, new_n, new_m = [], [], []
    for l in range(depth):
        p = _layer_params(l, g_norm1, w_in, b_gate, w_s, b_s, g_v, conv_w, conv_b, g_h, g_norm2,
                          g_final)
        late_f32 = {"wout": w_out[l], "w1": w1[l], "w3": w3[l], "w2": w2[l]}
        p["mod_head"] = _ada(cs, w_ada[l], b_ada[l][None, :], 2 * d)
        ada_tail = (cs, w_ada[l], b_ada[l][None, :], 2 * d)

        front_lat, gr_lat, p = _trunk_front(xs, 1, ts, GRID_W, p, (late_f32, ada_tail))
        front_ctx, gr_ctx, _ = _trunk_front(xp, 0, bp * tp, tp, p)

        xp, (c_ctx_out, n_ctx_out, m_ctx_out) = _trunk_back(
            xp, front_ctx, gr_ctx, 0, bp * tp, bp, CTX_SEQS_PER_STEP, p, want_state=True)
        new_c.append(c_ctx_out.reshape(bp, N_DIR, HEADS, HD, HD))
        new_n.append(n_ctx_out.reshape(bp, N_DIR, HEADS, HD))
        new_m.append((m_ctx_out[..., 0] * (1.0 / LOG2E)).reshape(bp, N_DIR, HEADS))

        s0 = jnp.concatenate(
            [jnp.swapaxes(state_C[:, l], -1, -2), state_n[:, l][..., None, :],
             jnp.zeros((bs_, N_DIR, HEADS, ST_ROWS - HD - 1, HD), F32)],
            axis=-2).reshape(bs_, N_UNIT, ST_ROWS, HD)
        m0 = jnp.broadcast_to((state_m[:, l] * LOG2E).reshape(bs_, N_UNIT, 1),
                              (bs_, N_UNIT, SCAN))
        xs, _ = _trunk_back(xs, front_lat, gr_lat, 1, ts, bs_, LAT_CHUNKS_PER_STEP, p, s0=s0, m0=m0)

    return (xp.reshape(bp, tp, d), xs.reshape(bs_, ts, d),
            jnp.stack(new_c, axis=1), jnp.stack(new_n, axis=1), jnp.stack(new_m, axis=1))
```

```python
import functools

import jax
import jax.numpy as jnp
from jax import lax
from jax.experimental import pallas as pl
from jax.experimental.pallas import tpu as pltpu

D_MODEL = 1024
D_GMLP = 512
D_MLSTM = 512
GROUPS = 4
HEADS = 4
HD = 128
CHUNK = 128
SCAN = 256
N_DIR = 2
N_UNIT = N_DIR * HEADS
GRID_W = 64
OFF_V, OFF_Q, OFF_VV, OFF_O, OFF_G = 512, 1024, 2048, 2560, 3072
D_IN = OFF_G + 2 * N_DIR * HEADS
EPS = 1e-6
NEG = -1e30
ST_ROWS = HD + 16
N_GATE = 2 * N_UNIT
GR_ROWS = 5 * N_GATE

TOKEN_BLOCK = 256
CTX_SEQS_PER_STEP = 2
LAT_CHUNKS_PER_STEP = 2
VMEM_LIMIT_BYTES = 56 * 1024 * 1024

F32 = jnp.float32
BF16 = jnp.bfloat16


def _rms(x, g):
    n = x.shape[-1]
    return x * lax.rsqrt(jnp.sum(x * x, axis=-1, keepdims=True) + n * EPS) * (g * n ** 0.5)


LOG2E = 1.4426950408889634
GELU_K = 0.7978845608028654
GELU_C = 0.044715


def _sigmoid(x):
    return 1.0 / (1.0 + jnp.exp2(x * (-LOG2E)))


def _gelu_tanh(x):
    t = (x * x) * (-2.0 * LOG2E * GELU_K * GELU_C) + (-2.0 * LOG2E * GELU_K)
    return x * (1.0 / (1.0 + jnp.exp2(x * t)))


def _log_sigmoid(x):
    return jnp.minimum(x, 0.0) - jnp.log(1.0 + jnp.exp(-jnp.abs(x)))


def _dot(a, b):
    return jnp.dot(a, b, preferred_element_type=F32)


def _dot_nt(a, b):
    return lax.dot_general(a, b, (((1,), (1,)), ((), ())), preferred_element_type=F32)


def _dot_exact(a, b):
    return jnp.dot(a, b, preferred_element_type=F32, precision=lax.Precision.HIGHEST)


def _pipeline_step(tick):
    i = pl.program_id(0)
    last = pl.num_programs(0) - 1

    @pl.when(i == 0)
    def _():
        tick(0, True, False, False)
        tick(1, True, True, False)

    @pl.when(jnp.logical_and(i > 0, i < last))
    def _():
        tick(0, True, True, True)
        tick(1, True, True, True)

    @pl.when(i == last)
    def _():
        tick(0, False, True, True)
        tick(1, False, False, True)


def _const_spec(shape):
    zeros = (0,) * len(shape)
    return pl.BlockSpec(shape, lambda *_: zeros, pipeline_mode=pl.Buffered(1))


def _ada_kernel(c_ref, w_ref, b_ref, o_ref):
    c = c_ref[...]
    s = (c * _sigmoid(c)).astype(BF16)
    o_ref[...] = _dot(s, w_ref[...].astype(BF16)) + b_ref[...]


def _ada_and_round_kernel(c_ref, w_ref, b_ref, wf_ref, o_ref, wb_ref):
    _ada_kernel(c_ref, w_ref, b_ref, o_ref)

    @pl.when(pl.program_id(0) == 0)
    def _():
        wb_ref[...] = wf_ref[...].astype(BF16)


def _ada(cs, w_ada, b_ada, n, w_f32):
    rows, d = cs.shape
    tn = 1024
    return pl.pallas_call(
        _ada_and_round_kernel,
        out_shape=[jax.ShapeDtypeStruct((rows, n), F32), jax.ShapeDtypeStruct(w_f32.shape, BF16)],
        grid=(n // tn,),
        in_specs=[
            pl.BlockSpec((rows, d), lambda j: (0, 0)),
            pl.BlockSpec((d, tn), lambda j: (0, j)),
            pl.BlockSpec((1, tn), lambda j: (0, j)),
            _const_spec(w_f32.shape),
        ],
        out_specs=[pl.BlockSpec((rows, tn), lambda j: (0, j)),
                   pl.BlockSpec(w_f32.shape, lambda j: (0, 0))],
        compiler_params=pltpu.CompilerParams(
            dimension_semantics=("arbitrary",), vmem_limit_bytes=VMEM_LIMIT_BYTES),
        name="ada",
    )(cs, w_ada, b_ada, w_f32)


def _inproj_norm(x, mod, g1, hb_ref):
    sh1 = mod[:, 0:D_MODEL]
    sc1 = mod[:, D_MODEL:2 * D_MODEL]
    hb_ref[...] = (_rms(x, g1 * (1.0 + sc1)) + sh1).astype(BF16)


def _inproj_project(hb_ref, w, z):
    hb = hb_ref[...]
    z["u"][...] = _dot_nt(hb, w["u"][...])
    z["v"][...] = _dot_nt(hb, w["v"][...])
    z["qk"][...] = _dot_nt(hb, w["qk"][...])
    z["o"][...] = _dot_nt(hb, w["o"][...])
    z["vt"][...] = _dot_nt(w["vv"][...], hb).astype(BF16)
    z["gt"][...] = _dot_nt(w["gt"][...], hb)


def _inproj_finish(z, c, out, seg):
    tb = z["u"].shape[0]
    for g in range(GROUPS):
        gs = slice(g * HD, (g + 1) * HD)
        vg = _rms(_gelu_tanh(z["v"][:, gs]), c["gv"][:, gs]).astype(BF16)
        for ch in range(tb // CHUNK):
            cs = slice(ch * CHUNK, (ch + 1) * CHUNK)
            mixed = _dot(c["ws"][g], vg[cs]) + c["bs"][:, gs]
            out["a"][cs, gs] = (_gelu_tanh(z["u"][cs, gs]) * mixed).astype(BF16)

    zqk = z["qk"][...]
    pos = lax.broadcasted_iota(jnp.int32, (tb, 1), 0) % seg
    prev = jnp.where(pos != 0, pltpu.roll(zqk, 1, 0), 0.0)
    nxt = jnp.where(pos != seg - 1, pltpu.roll(zqk, tb - 1, 0), 0.0)
    cw = c["cw"]
    y = c["cb"][...] + prev * cw[0:1, :] + zqk * cw[1:2, :] + nxt * cw[2:3, :]
    y = y * _sigmoid(y)
    out["qt"][...] = jnp.transpose(y[:, :D_MLSTM]).astype(BF16)
    out["k"][...] = (y[:, D_MLSTM:] * (HD ** -0.5)).astype(BF16)
    out["vt"][...] = z["vt"][...]
    out["o"][...] = z["o"][...]

    assert tb == SCAN
    gt = z["gt"][...] + c["bgt"][...]
    gi = pltpu.roll(gt, HEADS, 0)
    lf = _log_sigmoid(gt)
    lane = lax.broadcasted_iota(jnp.int32, (N_GATE, HD), 1)
    tiles = []
    carry = jnp.zeros((N_GATE, HD), F32)
    for t in range(SCAN // HD):
        prefix = lf[:, t * HD:(t + 1) * HD]
        shift = 1
        while shift < HD:
            prefix = prefix + jnp.where(lane >= shift, pltpu.roll(prefix, shift, 1), 0.0)
            shift *= 2
        tiles.append(prefix + carry)
        carry = carry + jnp.broadcast_to(prefix[:, HD - 1:HD], (N_GATE, HD))
    prefix = jnp.concatenate(tiles, axis=1)
    b_last = jnp.concatenate([carry] * (SCAN // HD), axis=1)
    row_bwd = lax.broadcasted_iota(jnp.int32, (N_GATE, SCAN), 0) >= N_GATE // N_DIR
    b_row = jnp.where(row_bwd, b_last - prefix + lf, prefix)
    g_row = b_last - b_row + gi
    g_max = jnp.broadcast_to(jnp.max(g_row, axis=1, keepdims=True), (N_GATE, SCAN))
    for sec, stat in enumerate((b_row, g_row, b_last, g_max, gi - b_row)):
        out["gr"][sec * N_GATE:(sec + 1) * N_GATE, :] = stat * LOG2E


_INPROJ_Z = ("u", "v", "qk", "o", "vt", "gt")


def _inproj_kernel(*refs, seg, blocks_per_mod, mod_row0, n_cast, ada_tail):
    refs = list(refs)
    take = lambda k: [refs.pop(0) for _ in range(k)]
    (x_ref, mod_ref, g1_ref, wit_ref,
     bgt_ref, cw_ref, cb_ref, ws_ref, bs_ref, gv_ref) = take(10)
    cast_in = take(n_cast)
    ada_in = take(3) if ada_tail else None
    a_ref, qt_ref, k_ref, vt_ref, o_ref, gr_ref = take(6)
    cast_out = take(n_cast)
    ada_out = take(1)[0] if ada_tail else None
    hb_s, *z_s = refs
    tb = x_ref.shape[0] // 2
    mod_in = mod_row0 + jnp.minimum(pl.program_id(0), pl.num_programs(0) - 2) // blocks_per_mod
    w = {"u": wit_ref.at[0:OFF_V, :], "v": wit_ref.at[OFF_V:OFF_Q, :],
         "qk": wit_ref.at[OFF_Q:OFF_VV, :], "vv": wit_ref.at[OFF_VV:OFF_O, :],
         "o": wit_ref.at[OFF_O:OFF_G, :], "gt": wit_ref.at[OFF_G:D_IN, :]}
    c = {"gv": gv_ref, "ws": ws_ref, "bs": bs_ref, "cw": cw_ref, "cb": cb_ref, "bgt": bgt_ref}

    def tick(half, norm, project, finish):
        rows = slice(half * tb, (half + 1) * tb)
        cur, oth = half, 1 - half
        z_cur = {name: ref.at[cur] for name, ref in zip(_INPROJ_Z, z_s)}
        z_oth = {name: ref.at[oth] for name, ref in zip(_INPROJ_Z, z_s)}
        out = {"a": a_ref.at[rows, :], "qt": qt_ref.at[:, rows], "k": k_ref.at[rows, :],
               "vt": vt_ref.at[:, rows], "o": o_ref.at[rows, :], "gr": gr_ref.at[:, rows]}
        if project:
            _inproj_project(hb_s.at[oth], w, z_oth)
        if norm:
            _inproj_norm(x_ref[rows, :], mod_ref[pl.ds(mod_in, 1), :], g1_ref[...], hb_s.at[cur])
        if finish:
            _inproj_finish(z_cur, c, out, seg)
        if half == 0:
            for src, dst in zip(cast_in, cast_out):
                dst[...] = src[...].astype(BF16)
            if ada_tail:
                _ada_kernel(*ada_in, ada_out)

    _pipeline_step(tick)


def _inproj(x, mod, mod_row0, rows_per_mod, seg, p, cast=(), ada_tail=None):
    n = x.shape[0]
    tb = TOKEN_BLOCK
    tb2 = 2 * tb
    assert tb % seg == 0 and n % tb2 == 0 and rows_per_mod % tb2 == 0
    nb = n // tb2
    blk_in = lambda i: jnp.minimum(i, nb - 1)
    blk_out = lambda i: jnp.maximum(i - 1, 0)
    tok = lambda width: pl.BlockSpec((tb2, width), lambda i: (blk_out(i), 0))
    tok_t = lambda height: pl.BlockSpec((height, tb2), lambda i: (0, blk_out(i)))
    in_specs = [
        pl.BlockSpec((tb2, D_MODEL), lambda i: (blk_in(i), 0)),
        _const_spec(mod.shape),
        _const_spec((1, D_MODEL)),
        _const_spec((D_IN, D_MODEL)),
        _const_spec((N_GATE, 1)),
        _const_spec((3, 2 * D_MLSTM)),
        _const_spec((1, 2 * D_MLSTM)),
        _const_spec((GROUPS, CHUNK, CHUNK)),
        _const_spec((CHUNK, D_GMLP)),
        _const_spec((1, D_GMLP)),
    ]
    out_shape = [
        jax.ShapeDtypeStruct((n, D_GMLP), BF16),
        jax.ShapeDtypeStruct((D_MLSTM, n), BF16),
        jax.ShapeDtypeStruct((n, D_MLSTM), BF16),
        jax.ShapeDtypeStruct((D_MLSTM, n), BF16),
        jax.ShapeDtypeStruct((n, D_MLSTM), F32),
        jax.ShapeDtypeStruct((GR_ROWS, n), F32),
    ]
    out_specs = [tok(D_GMLP), tok_t(D_MLSTM), tok(D_MLSTM), tok_t(D_MLSTM), tok(D_MLSTM),
                 tok_t(GR_ROWS)]
    for wf in cast:
        rows = wf.shape[0] // nb
        assert rows * nb == wf.shape[0] and rows % 16 == 0
        slab = pl.BlockSpec((rows, wf.shape[1]), lambda i: (blk_in(i), 0))
        in_specs.append(slab)
        out_specs.append(slab)
        out_shape.append(jax.ShapeDtypeStruct(wf.shape, BF16))
    side = list(cast)
    if ada_tail is not None:
        cs, w_ada, b_ada, col0 = ada_tail
        width = (w_ada.shape[1] - col0) // nb
        assert col0 % width == 0 and width * nb == w_ada.shape[1] - col0 and width % 128 == 0
        in_specs += [
            _const_spec(cs.shape),
            pl.BlockSpec((w_ada.shape[0], width), lambda i: (0, col0 // width + blk_in(i))),
            pl.BlockSpec((1, width), lambda i: (0, col0 // width + blk_in(i))),
        ]
        out_specs.append(pl.BlockSpec((cs.shape[0], width), lambda i: (0, blk_in(i))))
        out_shape.append(jax.ShapeDtypeStruct((cs.shape[0], w_ada.shape[1] - col0), F32))
        side += [cs, w_ada, b_ada]
    scratch_shapes = [
        pltpu.VMEM((2, tb, D_MODEL), BF16),
        pltpu.VMEM((2, tb, D_GMLP), F32),
        pltpu.VMEM((2, tb, D_GMLP), F32),
        pltpu.VMEM((2, tb, 2 * D_MLSTM), F32),
        pltpu.VMEM((2, tb, D_MLSTM), F32),
        pltpu.VMEM((2, D_MLSTM, tb), BF16),
        pltpu.VMEM((2, N_GATE, tb), F32),
    ]
    return pl.pallas_call(
        functools.partial(_inproj_kernel, seg=seg, blocks_per_mod=rows_per_mod // tb2,
                          mod_row0=mod_row0, n_cast=len(cast), ada_tail=ada_tail is not None),
        out_shape=out_shape,
        grid=(nb + 1,),
        in_specs=in_specs,
        out_specs=out_specs,
        scratch_shapes=scratch_shapes,
        compiler_params=pltpu.CompilerParams(
            dimension_semantics=("arbitrary",), vmem_limit_bytes=VMEM_LIMIT_BYTES),
        name="inproj",
    )(x, mod, p["g1"], p["wit"],
      p["bgt"], p["cw"], p["cb"], p["ws"], p["bs"], p["gv"], *side)


def _lane_broadcast_column(row):
    n = row.shape[1]
    tiles = [jnp.transpose(jnp.broadcast_to(row[:, t * HD:(t + 1) * HD], (HD, HD)))
             for t in range(n // HD)]
    col = jnp.concatenate(tiles, axis=0)
    return jnp.concatenate([col] * (n // HD), axis=1)


def _mlstm_unit(k, vt, kq, qs, d_row, b_row, g_row, bl_row, gmax_row, st, m_row, mask, carried):
    logw = jnp.where(mask, _lane_broadcast_column(d_row) + b_row, NEG)
    a = b_row + m_row
    mj = jnp.maximum(a, jnp.max(logw, axis=0, keepdims=True))
    w = jnp.exp2(logw - mj)
    s = kq * w
    num = _dot(vt, s.astype(BF16))
    den = jnp.sum(s, axis=0, keepdims=True)
    if carried:
        inter = jnp.exp2(a - mj)
        num = num + inter * qs[:HD]
        den = den + inter * qs[HD:HD + 1]
    h = num * (1.0 / jnp.maximum(jnp.abs(den), jnp.exp2(-mj)))
    m_new = jnp.maximum(bl_row + m_row, gmax_row)
    wc = jnp.exp2(g_row - m_new)
    pad_row = lax.broadcasted_iota(jnp.int32, (ST_ROWS - HD, wc.shape[1]), 0)
    v_aug = jnp.concatenate(
        [vt.astype(F32) * wc, jnp.where(pad_row == 0, wc, 0.0)], axis=0).astype(BF16)
    st_new = _dot(v_aug, k)
    if carried:
        decay = jnp.exp2(bl_row + m_row - m_new)
        st_new = decay[:, :HD] * st + st_new
    return h, st_new, m_new


def _mlstm_kernel(*refs, has_init, has_out, rounds, by_sequence):
    if has_init:
        s0_ref, m0_ref = refs[:2]
        refs = refs[2:]
    if by_sequence:
        k_ref, qt_ref, vt_ref, gr_ref = refs[:4]
        fwd_in = bwd_in = (k_ref, qt_ref, vt_ref, gr_ref)
        refs = refs[4:]
    else:
        fwd_in, bwd_in = refs[:4], refs[4:8]
        refs = refs[8:]
    hf_ref, hb_ref = refs[:2]
    refs = refs[2:]
    if has_out:
        co_ref, no_ref, mo_ref = refs[:3]
        refs = refs[3:]
    s_ref, m_ref = refs
    c = pl.program_id(1)
    nc = pl.num_programs(1)
    carried = not by_sequence

    if carried:
        @pl.when(c == 0)
        def _():
            if has_init:
                s_ref[...] = s0_ref[0]
                m_ref[...] = m0_ref[0]
            else:
                s_ref[...] = jnp.zeros_like(s_ref)
                m_ref[...] = jnp.zeros_like(m_ref)

    si = lax.broadcasted_iota(jnp.int32, (SCAN, SCAN), 0)
    ji = lax.broadcasted_iota(jnp.int32, (SCAN, SCAN), 1)
    dirs = ((fwd_in, hf_ref, si <= ji), (bwd_in, hb_ref, si >= ji))

    def chunk_cols(d, r):
        pos = rounds - 1 - r if (d == 1 and not by_sequence) else r
        return slice(pos * SCAN, (pos + 1) * SCAN)

    kq = {}
    for r in range(rounds):
        for d, ((k_ref, qt_ref, _, _), _, _) in enumerate(dirs):
            cs = chunk_cols(d, r)
            for hd in range(HEADS):
                hs = slice(hd * HD, (hd + 1) * HD)
                kq[r, d, hd] = _dot(k_ref[cs, hs], qt_ref[hs, cs])

    m_cur = None
    if carried:
        m_all = m_ref[...]
        m_cur = [m_all[u:u + 1] for u in range(N_UNIT)]
    zero_row = jnp.zeros((1, SCAN), F32)
    for r in range(rounds):
        slot = r * N_UNIT if by_sequence else 0
        for d, ((k_ref, qt_ref, vt_ref, gr_ref), h_ref, mask) in enumerate(dirs):
            cs = chunk_cols(d, r)
            for hd in range(HEADS):
                u = d * HEADS + hd
                row = d * 2 * HEADS + HEADS + hd
                hs = slice(hd * HD, (hd + 1) * HD)
                st = s_ref[slot + u] if carried else None
                qs = _dot(st.astype(BF16), qt_ref[hs, cs]) if carried else None
                h, st_new, m_new = _mlstm_unit(
                    k_ref[cs, hs], vt_ref[hs, cs], kq[r, d, hd], qs,
                    *(gr_ref[sec * N_GATE + row:sec * N_GATE + row + 1, cs]
                      for sec in (4, 0, 1, 2, 3)),
                    st, m_cur[u] if carried else zero_row, mask, carried)
                h_ref[hs, cs] = h
                s_ref[slot + u] = st_new
                if carried:
                    m_cur[u] = m_new
                else:
                    m_ref[slot + u:slot + u + 1, :] = m_new
    if carried:
        m_ref[...] = jnp.concatenate(m_cur, axis=0)

    if has_out:
        @pl.when(c == nc - 1)
        def _():
            for j in range(s_ref.shape[0]):
                q, u = divmod(j, N_UNIT)
                co_ref[q, u] = jnp.transpose(s_ref[j, 0:HD, :])
                no_ref[q, u:u + 1, :] = s_ref[j, HD:HD + 1, :]
                mo_ref[q, u:u + 1, :] = m_ref[j:j + 1, :]


def _mlstm(k, qt, vt, gr, batch, rounds, s0=None, m0=None, want_state=False):
    n = k.shape[0]
    nc = n // batch // SCAN
    by_sequence = nc == 1
    width = rounds * SCAN
    has_init = s0 is not None
    if by_sequence:
        assert batch % rounds == 0 and not has_init
        grid = (batch // rounds, 1)
        slots = rounds
        fwd = bwd = lambda b, c: b
    else:
        assert nc % rounds == 0 and not want_state
        steps = nc // rounds
        grid = (batch, steps)
        slots = 1
        fwd = lambda b, c: b * steps + c
        bwd = lambda b, c: b * steps + steps - 1 - c

    def specs(ix):
        return [
            pl.BlockSpec((width, D_MLSTM), lambda b, c: (ix(b, c), 0)),
            pl.BlockSpec((D_MLSTM, width), lambda b, c: (0, ix(b, c))),
            pl.BlockSpec((D_MLSTM, width), lambda b, c: (0, ix(b, c))),
            pl.BlockSpec((GR_ROWS, width), lambda b, c: (0, ix(b, c))),
        ]

    in_specs = specs(fwd) if by_sequence else specs(fwd) + specs(bwd)
    args = [k, qt, vt, gr] if by_sequence else [k, qt, vt, gr, k, qt, vt, gr]
    if has_init:
        in_specs = [
            pl.BlockSpec((1, N_UNIT, ST_ROWS, HD), lambda b, c: (b, 0, 0, 0)),
            pl.BlockSpec((1, N_UNIT, SCAN), lambda b, c: (b, 0, 0)),
        ] + in_specs
        args = [s0, m0] + args
    out_shape = [
        jax.ShapeDtypeStruct((D_MLSTM, n), F32),
        jax.ShapeDtypeStruct((D_MLSTM, n), F32),
    ]
    out_specs = [
        pl.BlockSpec((D_MLSTM, width), lambda b, c: (0, fwd(b, c))),
        pl.BlockSpec((D_MLSTM, width), lambda b, c: (0, bwd(b, c))),
    ]
    if want_state:
        out_shape += [
            jax.ShapeDtypeStruct((batch, N_UNIT, HD, HD), F32),
            jax.ShapeDtypeStruct((batch, N_UNIT, HD), F32),
            jax.ShapeDtypeStruct((batch, N_UNIT, SCAN), F32),
        ]
        out_specs += [
            pl.BlockSpec((slots, N_UNIT, HD, HD), lambda b, c: (b, 0, 0, 0)),
            pl.BlockSpec((slots, N_UNIT, HD), lambda b, c: (b, 0, 0)),
            pl.BlockSpec((slots, N_UNIT, SCAN), lambda b, c: (b, 0, 0)),
        ]
    return pl.pallas_call(
        functools.partial(_mlstm_kernel, has_init=has_init, has_out=want_state, rounds=rounds,
                          by_sequence=by_sequence),
        out_shape=out_shape,
        grid=grid,
        in_specs=in_specs,
        out_specs=out_specs,
        scratch_shapes=[pltpu.VMEM((slots * N_UNIT, ST_ROWS, HD), F32),
                        pltpu.VMEM((slots * N_UNIT, SCAN), F32)],
        compiler_params=pltpu.CompilerParams(
            dimension_semantics=("arbitrary", "arbitrary"), vmem_limit_bytes=VMEM_LIMIT_BYTES),
        name="mlstm",
    )(*args)


def _outffn_mix(x, a, hft, hbt, o, mod, c, x1_ref, h2_ref):
    ga1 = mod[:, 0:D_MODEL]
    sh2 = mod[:, D_MODEL:2 * D_MODEL]
    sc2 = mod[:, 2 * D_MODEL:3 * D_MODEL]
    hs = jnp.transpose(hft + hbt)
    sig = _sigmoid(o)
    parts = []
    for hd in range(HEADS):
        sl = slice(hd * HD, (hd + 1) * HD)
        parts.append(_rms(hs[:, sl], c["gh"][:, sl]) * sig[:, sl])
    b_out = jnp.concatenate(parts, axis=-1).astype(BF16)
    mix = _dot(a, c["wout"][0:D_GMLP, :]) + _dot(b_out, c["wout"][D_GMLP:, :])
    x1 = x + ga1 * mix
    x1_ref[...] = x1
    h2_ref[...] = (_rms(x1, c["g2"][...] * (1.0 + sc2)) + sh2).astype(BF16)


def _outffn_up(h2_ref, c, f_ref):
    h2 = h2_ref[...]
    u = _dot(h2, c["w1"][...])
    g = _dot(h2, c["w3"][...])
    f_ref[...] = (u * _sigmoid(u) * g).astype(BF16)


def _outffn_down(f_ref, x1_ref, mod, c, y_ref):
    ga2 = mod[:, 3 * D_MODEL:4 * D_MODEL]
    x2 = x1_ref[...] + ga2 * _dot(f_ref[...], c["w2"][...])
    y_ref[...] = _rms(x2, c["gf"][...])


def _outffn_kernel(x_ref, a_ref, hft_ref, hbt_ref, o_ref, mod_ref, gh_ref, wout_hbm,
                   g2_ref, w1_hbm, w3_hbm, w2_hbm, gf_ref, y_ref,
                   x1_s, h2_s, f_s, wout_s, w1_s, w3_s, w2_s, w_sem,
                   *, blocks_per_mod, mod_row0):
    tb = x_ref.shape[0] // 2
    i = pl.program_id(0)
    mod_in = mod_row0 + jnp.minimum(i, pl.num_programs(0) - 2) // blocks_per_mod
    mod_out = mod_row0 + jnp.maximum(i - 1, 0) // blocks_per_mod
    c = {"gh": gh_ref, "wout": wout_s, "g2": g2_ref, "w1": w1_s, "w3": w3_s, "w2": w2_s,
         "gf": gf_ref}
    weight_copy = {
        name: pltpu.make_async_copy(src, dst, w_sem.at[j])
        for j, (name, src, dst) in enumerate((("wout", wout_hbm, wout_s), ("w1", w1_hbm, w1_s),
                                              ("w3", w3_hbm, w3_s), ("w2", w2_hbm, w2_s)))}

    def tick(half, mix, up, down):
        rows = slice(half * tb, (half + 1) * tb)
        cur, oth = half, 1 - half
        first_tick = mix and not up and not down
        second_tick = mix and up and not down
        if first_tick:
            for name in ("wout", "w1", "w3", "w2"):
                weight_copy[name].start()
            weight_copy["wout"].wait()
        if down:
            _outffn_down(f_s.at[cur], x1_s.at[cur], mod_ref[pl.ds(mod_out, 1), :], c,
                         y_ref.at[rows, :])
        if mix:
            _outffn_mix(x_ref[rows, :], a_ref[rows, :], hft_ref[:, rows], hbt_ref[:, rows],
                        o_ref[rows, :], mod_ref[pl.ds(mod_in, 1), :], c, x1_s.at[cur],
                        h2_s.at[cur])
        if second_tick:
            weight_copy["w1"].wait()
            weight_copy["w3"].wait()
        if up:
            _outffn_up(h2_s.at[oth], c, f_s.at[oth])
        if second_tick:
            weight_copy["w2"].wait()

    _pipeline_step(tick)


def _outffn(x, a, hft, hbt, o, mod, mod_row0, rows_per_mod, p):
    n = x.shape[0]
    tb = TOKEN_BLOCK
    tb2 = 2 * tb
    assert n % tb2 == 0 and rows_per_mod % tb2 == 0
    nb = n // tb2
    d_ff = p["w1"].shape[1]
    blk_in = lambda i: jnp.minimum(i, nb - 1)
    blk_out = lambda i: jnp.maximum(i - 1, 0)
    tok = lambda w: pl.BlockSpec((tb2, w), lambda i: (blk_in(i), 0))
    tok_t = lambda h: pl.BlockSpec((h, tb2), lambda i: (0, blk_in(i)))
    weight_hbm = pl.BlockSpec(memory_space=pl.ANY)
    in_specs = [
        tok(D_MODEL), tok(D_GMLP), tok_t(D_MLSTM), tok_t(D_MLSTM), tok(D_MLSTM),
        _const_spec(mod.shape),
        _const_spec((1, D_MLSTM)),
        weight_hbm,
        _const_spec((1, D_MODEL)),
        weight_hbm,
        weight_hbm,
        weight_hbm,
        _const_spec((1, D_MODEL)),
    ]
    scratch_shapes = [
        pltpu.VMEM((2, tb, D_MODEL), F32),
        pltpu.VMEM((2, tb, D_MODEL), BF16),
        pltpu.VMEM((2, tb, d_ff), BF16),
        pltpu.VMEM(p["wout"].shape, BF16),
        pltpu.VMEM(p["w1"].shape, BF16),
        pltpu.VMEM(p["w3"].shape, BF16),
        pltpu.VMEM(p["w2"].shape, BF16),
        pltpu.SemaphoreType.DMA((4,)),
    ]
    return pl.pallas_call(
        functools.partial(_outffn_kernel, blocks_per_mod=rows_per_mod // tb2, mod_row0=mod_row0),
        out_shape=jax.ShapeDtypeStruct((n, D_MODEL), F32),
        grid=(nb + 1,),
        in_specs=in_specs,
        out_specs=pl.BlockSpec((tb2, D_MODEL), lambda i: (blk_out(i), 0)),
        scratch_shapes=scratch_shapes,
        compiler_params=pltpu.CompilerParams(
            dimension_semantics=("arbitrary",), vmem_limit_bytes=VMEM_LIMIT_BYTES),
        name="outffn",
    )(x, a, hft, hbt, o, mod, p["gh"], p["wout"], p["g2"], p["w1"], p["w3"], p["w2"], p["gf"])


def _layer_params(l, g_norm1, b_gate, w_s, b_s, g_v, conv_w, conv_b, g_h, g_norm2, g_final):
    return {
        "g1": g_norm1[l][None, :],
        "bgt": b_gate[l][:, None],
        "cw": conv_w[l], "cb": conv_b[l][None, :],
        "ws": w_s[l].astype(BF16),
        "bs": jnp.repeat(b_s[l].T, HD, axis=1),
        "gv": g_v[l].reshape(1, D_GMLP),
        "gh": g_h[l].reshape(1, D_MLSTM),
        "g2": g_norm2[l][None, :],
        "gf": g_final[None, :],
    }


_LATE_WEIGHTS = ("wout", "w1", "w3", "w2")


def _trunk_front(x, mod_row0, rows_per_mod, seg, p, late=None):
    cast, ada_tail = ((), None) if late is None else (
        tuple(late[0][name] for name in _LATE_WEIGHTS), late[1])
    a, qt, k, vt, o, gr, *side = _inproj(x, p["mod_head"], mod_row0, rows_per_mod, seg, p, cast,
                                         ada_tail)
    if late is not None:
        p = dict(p, **dict(zip(_LATE_WEIGHTS, side[:-1])))
        p["mod_tail"] = side[-1]
    return (a, qt, k, vt, o), gr, p


def _trunk_back(x, front, gr, mod_row0, rows_per_mod, batch, rounds, p, s0=None, m0=None,
                want_state=False):
    a, qt, k, vt, o = front
    hft, hbt, *state = _mlstm(k, qt, vt, gr, batch, rounds, s0, m0, want_state)
    y = _outffn(x, a, hft, hbt, o, p["mod_tail"], mod_row0, rows_per_mod, p)
    return y, state


def kernel(x_prompt, x_sample, state_C, state_n, state_m, c, c_ctx, w_ada, b_ada, g_norm1, w_in,
           b_gate, w_s, b_s, g_v, conv_w, conv_b, g_h, w_out, g_norm2, w1, w3, w2, g_final):
    bp, tp, d = x_prompt.shape
    bs_, ts, _ = x_sample.shape
    depth = w_in.shape[0]
    assert depth == 1, "final norm is fused into the layer's last kernel"
    xp = x_prompt.reshape(bp * tp, d)
    xs = x_sample.reshape(bs_ * ts, d)

    cs = jnp.zeros((8, d), F32).at[0].set(c_ctx).at[1:1 + bs_].set(c)
    new_c, new_n, new_m = [], [], []
    for l in range(depth):
        p = _layer_params(l, g_norm1, b_gate, w_s, b_s, g_v, conv_w, conv_b, g_h, g_norm2, g_final)
        late_f32 = {"wout": w_out[l], "w1": w1[l], "w3": w3[l], "w2": w2[l]}
        p["mod_head"], p["wit"] = _ada(cs, w_ada[l], b_ada[l][None, :], 2 * d, w_in[l].T)
        ada_tail = (cs, w_ada[l], b_ada[l][None, :], 2 * d)

        front_lat, gr_lat, p = _trunk_front(xs, 1, ts, GRID_W, p, (late_f32, ada_tail))
        front_ctx, gr_ctx, _ = _trunk_front(xp, 0, bp * tp, tp, p)

        xp, (c_ctx_out, n_ctx_out, m_ctx_out) = _trunk_back(
            xp, front_ctx, gr_ctx, 0, bp * tp, bp, CTX_SEQS_PER_STEP, p, want_state=True)
        new_c.append(c_ctx_out.reshape(bp, N_DIR, HEADS, HD, HD))
        new_n.append(n_ctx_out.reshape(bp, N_DIR, HEADS, HD))
        new_m.append((m_ctx_out[..., 0] * (1.0 / LOG2E)).reshape(bp, N_DIR, HEADS))

        s0 = jnp.concatenate(
            [jnp.swapaxes(state_C[:, l], -1, -2), state_n[:, l][..., None, :],
             jnp.zeros((bs_, N_DIR, HEADS, ST_ROWS - HD - 1, HD), F32)],
            axis=-2).reshape(bs_, N_UNIT, ST_ROWS, HD)
        m0 = jnp.broadcast_to((state_m[:, l] * LOG2E).reshape(bs_, N_UNIT, 1),
                              (bs_, N_UNIT, SCAN))
        xs, _ = _trunk_back(xs, front_lat, gr_lat, 1, ts, bs_, LAT_CHUNKS_PER_STEP, p, s0=s0, m0=m0)

    return (xp.reshape(bp, tp, d), xs.reshape(bs_, ts, d),
            jnp.stack(new_c, axis=1), jnp.stack(new_n, axis=1), jnp.stack(new_m, axis=1))
```

```python
import functools

import jax
import jax.numpy as jnp
from jax import lax
from jax.experimental import pallas as pl
from jax.experimental.pallas import tpu as pltpu

D_MODEL = 1024
D_GMLP = 512
D_MLSTM = 512
GROUPS = 4
HEADS = 4
HD = 128
CHUNK = 128
SCAN = 256
N_DIR = 2
N_UNIT = N_DIR * HEADS
GRID_W = 64
OFF_V, OFF_Q, OFF_VV, OFF_O, OFF_G = 512, 1024, 2048, 2560, 3072
D_IN = OFF_G + 2 * N_DIR * HEADS
EPS = 1e-6
NEG = -1e30
ST_ROWS = HD + 16
N_GATE = 2 * N_UNIT
GR_ROWS = 5 * N_GATE

TOKEN_BLOCK = 256
CTX_SEQS_PER_STEP = 2
LAT_CHUNKS_PER_STEP = 2
VMEM_LIMIT_BYTES = 56 * 1024 * 1024

F32 = jnp.float32
BF16 = jnp.bfloat16


def _rms(x, g):
    n = x.shape[-1]
    return x * lax.rsqrt(jnp.sum(x * x, axis=-1, keepdims=True) + n * EPS) * (g * n ** 0.5)


LOG2E = 1.4426950408889634
GELU_K = 0.7978845608028654
GELU_C = 0.044715


def _sigmoid(x):
    return 1.0 / (1.0 + jnp.exp2(x * (-LOG2E)))


def _gelu_tanh(x):
    t = (x * x) * (-2.0 * LOG2E * GELU_K * GELU_C) + (-2.0 * LOG2E * GELU_K)
    return x * (1.0 / (1.0 + jnp.exp2(x * t)))


def _log_sigmoid(x):
    return jnp.minimum(x, 0.0) - jnp.log(1.0 + jnp.exp(-jnp.abs(x)))


def _dot(a, b):
    return jnp.dot(a, b, preferred_element_type=F32)


def _dot_nt(a, b):
    return lax.dot_general(a, b, (((1,), (1,)), ((), ())), preferred_element_type=F32)


def _dot_exact(a, b):
    return jnp.dot(a, b, preferred_element_type=F32, precision=lax.Precision.HIGHEST)


def _pipeline_step(tick):
    i = pl.program_id(0)
    last = pl.num_programs(0) - 1

    @pl.when(i == 0)
    def _():
        tick(0, True, False, False)
        tick(1, True, True, False)

    @pl.when(jnp.logical_and(i > 0, i < last))
    def _():
        tick(0, True, True, True)
        tick(1, True, True, True)

    @pl.when(i == last)
    def _():
        tick(0, False, True, True)
        tick(1, False, False, True)


def _const_spec(shape):
    zeros = (0,) * len(shape)
    return pl.BlockSpec(shape, lambda *_: zeros, pipeline_mode=pl.Buffered(1))


def _ada_kernel(c_ref, w_ref, b_ref, o_ref):
    c = c_ref[...]
    s = (c * _sigmoid(c)).astype(BF16)
    o_ref[...] = _dot(s, w_ref[...].astype(BF16)) + b_ref[...]


def _ada_and_round_kernel(c_ref, w_ref, b_ref, wf_ref, o_ref, wb_ref):
    _ada_kernel(c_ref, w_ref, b_ref, o_ref)

    @pl.when(pl.program_id(0) == 0)
    def _():
        wb_ref[...] = wf_ref[...].astype(BF16)


def _ada(cs, w_ada, b_ada, n, w_f32):
    rows, d = cs.shape
    tn = 1024
    return pl.pallas_call(
        _ada_and_round_kernel,
        out_shape=[jax.ShapeDtypeStruct((rows, n), F32), jax.ShapeDtypeStruct(w_f32.shape, BF16)],
        grid=(n // tn,),
        in_specs=[
            pl.BlockSpec((rows, d), lambda j: (0, 0)),
            pl.BlockSpec((d, tn), lambda j: (0, j)),
            pl.BlockSpec((1, tn), lambda j: (0, j)),
            _const_spec(w_f32.shape),
        ],
        out_specs=[pl.BlockSpec((rows, tn), lambda j: (0, j)),
                   pl.BlockSpec(w_f32.shape, lambda j: (0, 0))],
        compiler_params=pltpu.CompilerParams(
            dimension_semantics=("arbitrary",), vmem_limit_bytes=VMEM_LIMIT_BYTES),
        name="ada",
    )(cs, w_ada, b_ada, w_f32)


def _inproj_norm(x, mod, g1, hb_ref):
    sh1 = mod[:, 0:D_MODEL]
    sc1 = mod[:, D_MODEL:2 * D_MODEL]
    hb_ref[...] = (_rms(x, g1 * (1.0 + sc1)) + sh1).astype(BF16)


def _inproj_project(hb_ref, w, z):
    hb = hb_ref[...]
    z["u"][...] = _dot_nt(hb, w["u"][...])
    z["v"][...] = _dot_nt(hb, w["v"][...])
    z["qk"][...] = _dot_nt(hb, w["qk"][...])
    z["o"][...] = _dot_nt(hb, w["o"][...])
    z["vt"][...] = _dot_nt(w["vv"][...], hb).astype(BF16)
    z["gt"][...] = _dot_nt(w["gt"][...], hb)


def _inproj_finish(z, c, out, seg):
    tb = z["u"].shape[0]
    for g in range(GROUPS):
        gs = slice(g * HD, (g + 1) * HD)
        vg = _rms(_gelu_tanh(z["v"][:, gs]), c["gv"][:, gs]).astype(BF16)
        for ch in range(tb // CHUNK):
            cs = slice(ch * CHUNK, (ch + 1) * CHUNK)
            mixed = _dot(c["ws"][g], vg[cs]) + c["bs"][:, gs]
            out["a"][cs, gs] = (_gelu_tanh(z["u"][cs, gs]) * mixed).astype(BF16)

    zqk = z["qk"][...]
    pos = lax.broadcasted_iota(jnp.int32, (tb, 1), 0) % seg
    prev = jnp.where(pos != 0, pltpu.roll(zqk, 1, 0), 0.0)
    nxt = jnp.where(pos != seg - 1, pltpu.roll(zqk, tb - 1, 0), 0.0)
    cw = c["cw"]
    y = c["cb"][...] + prev * cw[0:1, :] + zqk * cw[1:2, :] + nxt * cw[2:3, :]
    y = y * _sigmoid(y)
    out["qt"][...] = jnp.transpose(y[:, :D_MLSTM]).astype(BF16)
    out["k"][...] = (y[:, D_MLSTM:] * (HD ** -0.5)).astype(BF16)
    out["vt"][...] = z["vt"][...]
    out["o"][...] = z["o"][...]

    assert tb == SCAN
    gt = z["gt"][...] + c["bgt"][...]
    gi = pltpu.roll(gt, HEADS, 0)
    lf = _log_sigmoid(gt)
    lane = lax.broadcasted_iota(jnp.int32, (N_GATE, HD), 1)
    tiles = []
    carry = jnp.zeros((N_GATE, HD), F32)
    for t in range(SCAN // HD):
        prefix = lf[:, t * HD:(t + 1) * HD]
        shift = 1
        while shift < HD:
            prefix = prefix + jnp.where(lane >= shift, pltpu.roll(prefix, shift, 1), 0.0)
            shift *= 2
        tiles.append(prefix + carry)
        carry = carry + jnp.broadcast_to(prefix[:, HD - 1:HD], (N_GATE, HD))
    prefix = jnp.concatenate(tiles, axis=1)
    b_last = jnp.concatenate([carry] * (SCAN // HD), axis=1)
    row_bwd = lax.broadcasted_iota(jnp.int32, (N_GATE, SCAN), 0) >= N_GATE // N_DIR
    b_row = jnp.where(row_bwd, b_last - prefix + lf, prefix)
    g_row = b_last - b_row + gi
    g_max = jnp.broadcast_to(jnp.max(g_row, axis=1, keepdims=True), (N_GATE, SCAN))
    for sec, stat in enumerate((b_row, g_row, b_last, g_max, gi - b_row)):
        out["gr"][sec * N_GATE:(sec + 1) * N_GATE, :] = stat * LOG2E


_INPROJ_Z = ("u", "v", "qk", "o", "vt", "gt")


def _inproj_kernel(*refs, seg, blocks_per_mod, mod_row0, n_cast, ada_tail):
    refs = list(refs)
    take = lambda k: [refs.pop(0) for _ in range(k)]
    (x_ref, mod_ref, g1_ref, wit_ref,
     bgt_ref, cw_ref, cb_ref, ws_ref, bs_ref, gv_ref) = take(10)
    cast_in = take(n_cast)
    ada_in = take(3) if ada_tail else None
    ak_ref, qvt_ref, o_ref, gr_ref = take(4)
    cast_out = take(n_cast)
    ada_out = take(1)[0] if ada_tail else None
    hb_s, *z_s = refs
    tb = x_ref.shape[0] // 2
    mod_in = mod_row0 + jnp.minimum(pl.program_id(0), pl.num_programs(0) - 2) // blocks_per_mod
    w = {"u": wit_ref.at[0:OFF_V, :], "v": wit_ref.at[OFF_V:OFF_Q, :],
         "qk": wit_ref.at[OFF_Q:OFF_VV, :], "vv": wit_ref.at[OFF_VV:OFF_O, :],
         "o": wit_ref.at[OFF_O:OFF_G, :], "gt": wit_ref.at[OFF_G:D_IN, :]}
    c = {"gv": gv_ref, "ws": ws_ref, "bs": bs_ref, "cw": cw_ref, "cb": cb_ref, "bgt": bgt_ref}

    def tick(half, norm, project, finish):
        rows = slice(half * tb, (half + 1) * tb)
        cur, oth = half, 1 - half
        z_cur = {name: ref.at[cur] for name, ref in zip(_INPROJ_Z, z_s)}
        z_oth = {name: ref.at[oth] for name, ref in zip(_INPROJ_Z, z_s)}
        out = {"a": ak_ref.at[rows, 0:D_GMLP], "k": ak_ref.at[rows, D_GMLP:],
               "qt": qvt_ref.at[0:D_MLSTM, rows], "vt": qvt_ref.at[D_MLSTM:, rows],
               "o": o_ref.at[rows, :], "gr": gr_ref.at[:, rows]}
        if project:
            _inproj_project(hb_s.at[oth], w, z_oth)
        if norm:
            _inproj_norm(x_ref[rows, :], mod_ref[pl.ds(mod_in, 1), :], g1_ref[...], hb_s.at[cur])
        if finish:
            _inproj_finish(z_cur, c, out, seg)
        if half == 0:
            for src, dst in zip(cast_in, cast_out):
                dst[...] = src[...].astype(BF16)
            if ada_tail:
                _ada_kernel(*ada_in, ada_out)

    _pipeline_step(tick)


def _inproj(x, mod, mod_row0, rows_per_mod, seg, p, cast=(), ada_tail=None):
    n = x.shape[0]
    tb = TOKEN_BLOCK
    tb2 = 2 * tb
    assert tb % seg == 0 and n % tb2 == 0 and rows_per_mod % tb2 == 0
    nb = n // tb2
    blk_in = lambda i: jnp.minimum(i, nb - 1)
    blk_out = lambda i: jnp.maximum(i - 1, 0)
    tok = lambda width: pl.BlockSpec((tb2, width), lambda i: (blk_out(i), 0))
    tok_t = lambda height: pl.BlockSpec((height, tb2), lambda i: (0, blk_out(i)))
    in_specs = [
        pl.BlockSpec((tb2, D_MODEL), lambda i: (blk_in(i), 0)),
        _const_spec(mod.shape),
        _const_spec((1, D_MODEL)),
        _const_spec((D_IN, D_MODEL)),
        _const_spec((N_GATE, 1)),
        _const_spec((3, 2 * D_MLSTM)),
        _const_spec((1, 2 * D_MLSTM)),
        _const_spec((GROUPS, CHUNK, CHUNK)),
        _const_spec((CHUNK, D_GMLP)),
        _const_spec((1, D_GMLP)),
    ]
    out_shape = [
        jax.ShapeDtypeStruct((n, D_GMLP + D_MLSTM), BF16),
        jax.ShapeDtypeStruct((2 * D_MLSTM, n), BF16),
        jax.ShapeDtypeStruct((n, D_MLSTM), F32),
        jax.ShapeDtypeStruct((GR_ROWS, n), F32),
    ]
    out_specs = [tok(D_GMLP + D_MLSTM), tok_t(2 * D_MLSTM), tok(D_MLSTM), tok_t(GR_ROWS)]
    for wf in cast:
        rows = wf.shape[0] // nb
        assert rows * nb == wf.shape[0] and rows % 16 == 0
        slab = pl.BlockSpec((rows, wf.shape[1]), lambda i: (blk_in(i), 0))
        in_specs.append(slab)
        out_specs.append(slab)
        out_shape.append(jax.ShapeDtypeStruct(wf.shape, BF16))
    side = list(cast)
    if ada_tail is not None:
        cs, w_ada, b_ada, col0 = ada_tail
        width = (w_ada.shape[1] - col0) // nb
        assert col0 % width == 0 and width * nb == w_ada.shape[1] - col0 and width % 128 == 0
        in_specs += [
            _const_spec(cs.shape),
            pl.BlockSpec((w_ada.shape[0], width), lambda i: (0, col0 // width + blk_in(i))),
            pl.BlockSpec((1, width), lambda i: (0, col0 // width + blk_in(i))),
        ]
        out_specs.append(pl.BlockSpec((cs.shape[0], width), lambda i: (0, blk_in(i))))
        out_shape.append(jax.ShapeDtypeStruct((cs.shape[0], w_ada.shape[1] - col0), F32))
        side += [cs, w_ada, b_ada]
    scratch_shapes = [
        pltpu.VMEM((2, tb, D_MODEL), BF16),
        pltpu.VMEM((2, tb, D_GMLP), F32),
        pltpu.VMEM((2, tb, D_GMLP), F32),
        pltpu.VMEM((2, tb, 2 * D_MLSTM), F32),
        pltpu.VMEM((2, tb, D_MLSTM), F32),
        pltpu.VMEM((2, D_MLSTM, tb), BF16),
        pltpu.VMEM((2, N_GATE, tb), F32),
    ]
    return pl.pallas_call(
        functools.partial(_inproj_kernel, seg=seg, blocks_per_mod=rows_per_mod // tb2,
                          mod_row0=mod_row0, n_cast=len(cast), ada_tail=ada_tail is not None),
        out_shape=out_shape,
        grid=(nb + 1,),
        in_specs=in_specs,
        out_specs=out_specs,
        scratch_shapes=scratch_shapes,
        compiler_params=pltpu.CompilerParams(
            dimension_semantics=("arbitrary",), vmem_limit_bytes=VMEM_LIMIT_BYTES),
        name="inproj",
    )(x, mod, p["g1"], p["wit"],
      p["bgt"], p["cw"], p["cb"], p["ws"], p["bs"], p["gv"], *side)


def _lane_broadcast_column(row):
    n = row.shape[1]
    tiles = [jnp.transpose(jnp.broadcast_to(row[:, t * HD:(t + 1) * HD], (HD, HD)))
             for t in range(n // HD)]
    col = jnp.concatenate(tiles, axis=0)
    return jnp.concatenate([col] * (n // HD), axis=1)


def _mlstm_unit(k, vt, kq, qs, d_row, b_row, g_row, bl_row, gmax_row, st, m_row, mask, carried):
    logw = jnp.where(mask, _lane_broadcast_column(d_row) + b_row, NEG)
    a = b_row + m_row
    mj = jnp.maximum(a, jnp.max(logw, axis=0, keepdims=True))
    w = jnp.exp2(logw - mj)
    s = kq * w
    num = _dot(vt, s.astype(BF16))
    den = jnp.sum(s, axis=0, keepdims=True)
    if carried:
        inter = jnp.exp2(a - mj)
        num = num + inter * qs[:HD]
        den = den + inter * qs[HD:HD + 1]
    h = num * (1.0 / jnp.maximum(jnp.abs(den), jnp.exp2(-mj)))
    m_new = jnp.maximum(bl_row + m_row, gmax_row)
    wc = jnp.exp2(g_row - m_new)
    pad_row = lax.broadcasted_iota(jnp.int32, (ST_ROWS - HD, wc.shape[1]), 0)
    v_aug = jnp.concatenate(
        [vt.astype(F32) * wc, jnp.where(pad_row == 0, wc, 0.0)], axis=0).astype(BF16)
    st_new = _dot(v_aug, k)
    if carried:
        decay = jnp.exp2(bl_row + m_row - m_new)
        st_new = decay[:, :HD] * st + st_new
    return h, st_new, m_new


def _mlstm_kernel(*refs, has_init, has_out, rounds, by_sequence):
    if has_init:
        s0_ref, m0_ref = refs[:2]
        refs = refs[2:]
    def split(k_ref, qvt_ref, gr_ref):
        return k_ref, qvt_ref.at[0:D_MLSTM, :], qvt_ref.at[D_MLSTM:, :], gr_ref

    if by_sequence:
        fwd_in = bwd_in = split(*refs[:3])
        refs = refs[3:]
    else:
        fwd_in, bwd_in = split(*refs[:3]), split(*refs[3:6])
        refs = refs[6:]
    hf_ref, hb_ref = refs[:2]
    refs = refs[2:]
    if has_out:
        co_ref, no_ref, mo_ref = refs[:3]
        refs = refs[3:]
    s_ref, m_ref = refs
    c = pl.program_id(1)
    nc = pl.num_programs(1)
    carried = not by_sequence

    if carried:
        @pl.when(c == 0)
        def _():
            if has_init:
                s_ref[...] = s0_ref[0]
                m_ref[...] = m0_ref[0]
            else:
                s_ref[...] = jnp.zeros_like(s_ref)
                m_ref[...] = jnp.zeros_like(m_ref)

    si = lax.broadcasted_iota(jnp.int32, (SCAN, SCAN), 0)
    ji = lax.broadcasted_iota(jnp.int32, (SCAN, SCAN), 1)
    dirs = ((fwd_in, hf_ref, si <= ji), (bwd_in, hb_ref, si >= ji))

    def chunk_cols(d, r):
        pos = rounds - 1 - r if (d == 1 and not by_sequence) else r
        return slice(pos * SCAN, (pos + 1) * SCAN)

    kq = {}
    for r in range(rounds):
        for d, ((k_ref, qt_ref, _, _), _, _) in enumerate(dirs):
            cs = chunk_cols(d, r)
            for hd in range(HEADS):
                hs = slice(hd * HD, (hd + 1) * HD)
                kq[r, d, hd] = _dot(k_ref[cs, hs], qt_ref[hs, cs])

    m_cur = None
    if carried:
        m_all = m_ref[...]
        m_cur = [m_all[u:u + 1] for u in range(N_UNIT)]
    zero_row = jnp.zeros((1, SCAN), F32)
    for r in range(rounds):
        slot = r * N_UNIT if by_sequence else 0
        for d, ((k_ref, qt_ref, vt_ref, gr_ref), h_ref, mask) in enumerate(dirs):
            cs = chunk_cols(d, r)
            for hd in range(HEADS):
                u = d * HEADS + hd
                row = d * 2 * HEADS + HEADS + hd
                hs = slice(hd * HD, (hd + 1) * HD)
                st = s_ref[slot + u] if carried else None
                qs = _dot(st.astype(BF16), qt_ref[hs, cs]) if carried else None
                h, st_new, m_new = _mlstm_unit(
                    k_ref[cs, hs], vt_ref[hs, cs], kq[r, d, hd], qs,
                    *(gr_ref[sec * N_GATE + row:sec * N_GATE + row + 1, cs]
                      for sec in (4, 0, 1, 2, 3)),
                    st, m_cur[u] if carried else zero_row, mask, carried)
                h_ref[hs, cs] = h
                s_ref[slot + u] = st_new
                if carried:
                    m_cur[u] = m_new
                else:
                    m_ref[slot + u:slot + u + 1, :] = m_new
    if carried:
        m_ref[...] = jnp.concatenate(m_cur, axis=0)

    if has_out:
        @pl.when(c == nc - 1)
        def _():
            for j in range(s_ref.shape[0]):
                q, u = divmod(j, N_UNIT)
                co_ref[q, u] = jnp.transpose(s_ref[j, 0:HD, :])
                no_ref[q, u:u + 1, :] = s_ref[j, HD:HD + 1, :]
                mo_ref[q, u:u + 1, :] = m_ref[j:j + 1, :]


def _mlstm(ak, qvt, gr, batch, rounds, s0=None, m0=None, want_state=False):
    n = ak.shape[0]
    nc = n // batch // SCAN
    by_sequence = nc == 1
    width = rounds * SCAN
    has_init = s0 is not None
    if by_sequence:
        assert batch % rounds == 0 and not has_init
        grid = (batch // rounds, 1)
        slots = rounds
        fwd = bwd = lambda b, c: b
    else:
        assert nc % rounds == 0 and not want_state
        steps = nc // rounds
        grid = (batch, steps)
        slots = 1
        fwd = lambda b, c: b * steps + c
        bwd = lambda b, c: b * steps + steps - 1 - c

    def specs(ix):
        return [
            pl.BlockSpec((width, D_MLSTM), lambda b, c: (ix(b, c), 1)),
            pl.BlockSpec((2 * D_MLSTM, width), lambda b, c: (0, ix(b, c))),
            pl.BlockSpec((GR_ROWS, width), lambda b, c: (0, ix(b, c))),
        ]

    in_specs = specs(fwd) if by_sequence else specs(fwd) + specs(bwd)
    args = [ak, qvt, gr] if by_sequence else [ak, qvt, gr, ak, qvt, gr]
    if has_init:
        in_specs = [
            pl.BlockSpec((1, N_UNIT, ST_ROWS, HD), lambda b, c: (b, 0, 0, 0)),
            pl.BlockSpec((1, N_UNIT, SCAN), lambda b, c: (b, 0, 0)),
        ] + in_specs
        args = [s0, m0] + args
    out_shape = [
        jax.ShapeDtypeStruct((D_MLSTM, n), F32),
        jax.ShapeDtypeStruct((D_MLSTM, n), F32),
    ]
    out_specs = [
        pl.BlockSpec((D_MLSTM, width), lambda b, c: (0, fwd(b, c))),
        pl.BlockSpec((D_MLSTM, width), lambda b, c: (0, bwd(b, c))),
    ]
    if want_state:
        out_shape += [
            jax.ShapeDtypeStruct((batch, N_UNIT, HD, HD), F32),
            jax.ShapeDtypeStruct((batch, N_UNIT, HD), F32),
            jax.ShapeDtypeStruct((batch, N_UNIT, SCAN), F32),
        ]
        out_specs += [
            pl.BlockSpec((slots, N_UNIT, HD, HD), lambda b, c: (b, 0, 0, 0)),
            pl.BlockSpec((slots, N_UNIT, HD), lambda b, c: (b, 0, 0)),
            pl.BlockSpec((slots, N_UNIT, SCAN), lambda b, c: (b, 0, 0)),
        ]
    return pl.pallas_call(
        functools.partial(_mlstm_kernel, has_init=has_init, has_out=want_state, rounds=rounds,
                          by_sequence=by_sequence),
        out_shape=out_shape,
        grid=grid,
        in_specs=in_specs,
        out_specs=out_specs,
        scratch_shapes=[pltpu.VMEM((slots * N_UNIT, ST_ROWS, HD), F32),
                        pltpu.VMEM((slots * N_UNIT, SCAN), F32)],
        compiler_params=pltpu.CompilerParams(
            dimension_semantics=("arbitrary", "arbitrary"), vmem_limit_bytes=VMEM_LIMIT_BYTES),
        name="mlstm",
    )(*args)


def _outffn_mix(x, a, hft, hbt, o, mod, c, x1_ref, h2_ref):
    ga1 = mod[:, 0:D_MODEL]
    sh2 = mod[:, D_MODEL:2 * D_MODEL]
    sc2 = mod[:, 2 * D_MODEL:3 * D_MODEL]
    hs = jnp.transpose(hft + hbt)
    sig = _sigmoid(o)
    parts = []
    for hd in range(HEADS):
        sl = slice(hd * HD, (hd + 1) * HD)
        parts.append(_rms(hs[:, sl], c["gh"][:, sl]) * sig[:, sl])
    b_out = jnp.concatenate(parts, axis=-1).astype(BF16)
    mix = _dot(a, c["wout"][0:D_GMLP, :]) + _dot(b_out, c["wout"][D_GMLP:, :])
    x1 = x + ga1 * mix
    x1_ref[...] = x1
    h2_ref[...] = (_rms(x1, c["g2"][...] * (1.0 + sc2)) + sh2).astype(BF16)


def _outffn_up(h2_ref, c, f_ref):
    h2 = h2_ref[...]
    u = _dot(h2, c["w1"][...])
    g = _dot(h2, c["w3"][...])
    f_ref[...] = (u * _sigmoid(u) * g).astype(BF16)


def _outffn_down(f_ref, x1_ref, mod, c, y_ref):
    ga2 = mod[:, 3 * D_MODEL:4 * D_MODEL]
    x2 = x1_ref[...] + ga2 * _dot(f_ref[...], c["w2"][...])
    y_ref[...] = _rms(x2, c["gf"][...])


def _outffn_kernel(x_ref, a_ref, hft_ref, hbt_ref, o_ref, mod_ref, gh_ref, wout_hbm,
                   g2_ref, w1_hbm, w3_hbm, w2_hbm, gf_ref, y_ref,
                   x1_s, h2_s, f_s, wout_s, w1_s, w3_s, w2_s, w_sem,
                   *, blocks_per_mod, mod_row0):
    tb = x_ref.shape[0] // 2
    i = pl.program_id(0)
    mod_in = mod_row0 + jnp.minimum(i, pl.num_programs(0) - 2) // blocks_per_mod
    mod_out = mod_row0 + jnp.maximum(i - 1, 0) // blocks_per_mod
    c = {"gh": gh_ref, "wout": wout_s, "g2": g2_ref, "w1": w1_s, "w3": w3_s, "w2": w2_s,
         "gf": gf_ref}
    weight_copy = {
        name: pltpu.make_async_copy(src, dst, w_sem.at[j])
        for j, (name, src, dst) in enumerate((("wout", wout_hbm, wout_s), ("w1", w1_hbm, w1_s),
                                              ("w3", w3_hbm, w3_s), ("w2", w2_hbm, w2_s)))}

    def tick(half, mix, up, down):
        rows = slice(half * tb, (half + 1) * tb)
        cur, oth = half, 1 - half
        first_tick = mix and not up and not down
        second_tick = mix and up and not down
        if first_tick:
            for name in ("wout", "w1", "w3", "w2"):
                weight_copy[name].start()
            weight_copy["wout"].wait()
        if down:
            _outffn_down(f_s.at[cur], x1_s.at[cur], mod_ref[pl.ds(mod_out, 1), :], c,
                         y_ref.at[rows, :])
        if mix:
            _outffn_mix(x_ref[rows, :], a_ref[rows, :], hft_ref[:, rows], hbt_ref[:, rows],
                        o_ref[rows, :], mod_ref[pl.ds(mod_in, 1), :], c, x1_s.at[cur],
                        h2_s.at[cur])
        if second_tick:
            weight_copy["w1"].wait()
            weight_copy["w3"].wait()
        if up:
            _outffn_up(h2_s.at[oth], c, f_s.at[oth])
        if second_tick:
            weight_copy["w2"].wait()

    _pipeline_step(tick)


def _outffn(x, ak, hft, hbt, o, mod, mod_row0, rows_per_mod, p):
    n = x.shape[0]
    tb = TOKEN_BLOCK
    tb2 = 2 * tb
    assert n % tb2 == 0 and rows_per_mod % tb2 == 0
    nb = n // tb2
    d_ff = p["w1"].shape[1]
    blk_in = lambda i: jnp.minimum(i, nb - 1)
    blk_out = lambda i: jnp.maximum(i - 1, 0)
    tok = lambda w: pl.BlockSpec((tb2, w), lambda i: (blk_in(i), 0))
    tok_t = lambda h: pl.BlockSpec((h, tb2), lambda i: (0, blk_in(i)))
    weight_hbm = pl.BlockSpec(memory_space=pl.ANY)
    in_specs = [
        tok(D_MODEL), tok(D_GMLP), tok_t(D_MLSTM), tok_t(D_MLSTM), tok(D_MLSTM),
        _const_spec(mod.shape),
        _const_spec((1, D_MLSTM)),
        weight_hbm,
        _const_spec((1, D_MODEL)),
        weight_hbm,
        weight_hbm,
        weight_hbm,
        _const_spec((1, D_MODEL)),
    ]
    scratch_shapes = [
        pltpu.VMEM((2, tb, D_MODEL), F32),
        pltpu.VMEM((2, tb, D_MODEL), BF16),
        pltpu.VMEM((2, tb, d_ff), BF16),
        pltpu.VMEM(p["wout"].shape, BF16),
        pltpu.VMEM(p["w1"].shape, BF16),
        pltpu.VMEM(p["w3"].shape, BF16),
        pltpu.VMEM(p["w2"].shape, BF16),
        pltpu.SemaphoreType.DMA((4,)),
    ]
    return pl.pallas_call(
        functools.partial(_outffn_kernel, blocks_per_mod=rows_per_mod // tb2, mod_row0=mod_row0),
        out_shape=jax.ShapeDtypeStruct((n, D_MODEL), F32),
        grid=(nb + 1,),
        in_specs=in_specs,
        out_specs=pl.BlockSpec((tb2, D_MODEL), lambda i: (blk_out(i), 0)),
        scratch_shapes=scratch_shapes,
        compiler_params=pltpu.CompilerParams(
            dimension_semantics=("arbitrary",), vmem_limit_bytes=VMEM_LIMIT_BYTES),
        name="outffn",
    )(x, ak, hft, hbt, o, mod, p["gh"], p["wout"], p["g2"], p["w1"], p["w3"], p["w2"], p["gf"])


def _layer_params(l, g_norm1, b_gate, w_s, b_s, g_v, conv_w, conv_b, g_h, g_norm2, g_final):
    return {
        "g1": g_norm1[l][None, :],
        "bgt": b_gate[l][:, None],
        "cw": conv_w[l], "cb": conv_b[l][None, :],
        "ws": w_s[l].astype(BF16),
        "bs": jnp.repeat(b_s[l].T, HD, axis=1),
        "gv": g_v[l].reshape(1, D_GMLP),
        "gh": g_h[l].reshape(1, D_MLSTM),
        "g2": g_norm2[l][None, :],
        "gf": g_final[None, :],
    }


_LATE_WEIGHTS = ("wout", "w1", "w3", "w2")


def _trunk_front(x, mod_row0, rows_per_mod, seg, p, late=None):
    cast, ada_tail = ((), None) if late is None else (
        tuple(late[0][name] for name in _LATE_WEIGHTS), late[1])
    ak, qvt, o, gr, *side = _inproj(x, p["mod_head"], mod_row0, rows_per_mod, seg, p, cast,
                                    ada_tail)
    if late is not None:
        p = dict(p, **dict(zip(_LATE_WEIGHTS, side[:-1])))
        p["mod_tail"] = side[-1]
    return (ak, qvt, o), gr, p


def _trunk_back(x, front, gr, mod_row0, rows_per_mod, batch, rounds, p, s0=None, m0=None,
                want_state=False):
    ak, qvt, o = front
    hft, hbt, *state = _mlstm(ak, qvt, gr, batch, rounds, s0, m0, want_state)
    y = _outffn(x, ak, hft, hbt, o, p["mod_tail"], mod_row0, rows_per_mod, p)
    return y, state


def kernel(x_prompt, x_sample, state_C, state_n, state_m, c, c_ctx, w_ada, b_ada, g_norm1, w_in,
           b_gate, w_s, b_s, g_v, conv_w, conv_b, g_h, w_out, g_norm2, w1, w3, w2, g_final):
    bp, tp, d = x_prompt.shape
    bs_, ts, _ = x_sample.shape
    depth = w_in.shape[0]
    assert depth == 1, "final norm is fused into the layer's last kernel"
    xp = x_prompt.reshape(bp * tp, d)
    xs = x_sample.reshape(bs_ * ts, d)

    cs = jnp.zeros((8, d), F32).at[0].set(c_ctx).at[1:1 + bs_].set(c)
    new_c, new_n, new_m = [], [], []
    for l in range(depth):
        p = _layer_params(l, g_norm1, b_gate, w_s, b_s, g_v, conv_w, conv_b, g_h, g_norm2, g_final)
        late_f32 = {"wout": w_out[l], "w1": w1[l], "w3": w3[l], "w2": w2[l]}
        p["mod_head"], p["wit"] = _ada(cs, w_ada[l], b_ada[l][None, :], 2 * d, w_in[l].T)
        ada_tail = (cs, w_ada[l], b_ada[l][None, :], 2 * d)

        front_lat, gr_lat, p = _trunk_front(xs, 1, ts, GRID_W, p, (late_f32, ada_tail))
        front_ctx, gr_ctx, _ = _trunk_front(xp, 0, bp * tp, tp, p)

        xp, (c_ctx_out, n_ctx_out, m_ctx_out) = _trunk_back(
            xp, front_ctx, gr_ctx, 0, bp * tp, bp, CTX_SEQS_PER_STEP, p, want_state=True)
        new_c.append(c_ctx_out.reshape(bp, N_DIR, HEADS, HD, HD))
        new_n.append(n_ctx_out.reshape(bp, N_DIR, HEADS, HD))
        new_m.append((m_ctx_out[..., 0] * (1.0 / LOG2E)).reshape(bp, N_DIR, HEADS))

        s0 = jnp.concatenate(
            [jnp.swapaxes(state_C[:, l], -1, -2), state_n[:, l][..., None, :],
             jnp.zeros((bs_, N_DIR, HEADS, ST_ROWS - HD - 1, HD), F32)],
            axis=-2).reshape(bs_, N_UNIT, ST_ROWS, HD)
        m0 = jnp.broadcast_to((state_m[:, l] * LOG2E).reshape(bs_, N_UNIT, 1),
                              (bs_, N_UNIT, SCAN))
        xs, _ = _trunk_back(xs, front_lat, gr_lat, 1, ts, bs_, LAT_CHUNKS_PER_STEP, p, s0=s0, m0=m0)

    return (xp.reshape(bp, tp, d), xs.reshape(bs_, ts, d),
            jnp.stack(new_c, axis=1), jnp.stack(new_n, axis=1), jnp.stack(new_m, axis=1))
```

```python
import functools

import jax
import jax.numpy as jnp
from jax import lax
from jax.experimental import pallas as pl
from jax.experimental.pallas import tpu as pltpu

D_MODEL = 1024
D_GMLP = 512
D_MLSTM = 512
GROUPS = 4
HEADS = 4
HD = 128
CHUNK = 128
SCAN = 256
N_DIR = 2
N_UNIT = N_DIR * HEADS
GRID_W = 64
OFF_V, OFF_Q, OFF_VV, OFF_O, OFF_G = 512, 1024, 2048, 2560, 3072
D_IN = OFF_G + 2 * N_DIR * HEADS
EPS = 1e-6
NEG = -1e30
ST_ROWS = HD + 16
N_GATE = 2 * N_UNIT
GR_ROWS = 5 * N_GATE

TOKEN_BLOCK = 256
CTX_SEQS_PER_STEP = 2
LAT_CHUNKS_PER_STEP = 2
VMEM_LIMIT_BYTES = 56 * 1024 * 1024

F32 = jnp.float32
BF16 = jnp.bfloat16


def _rms(x, g):
    n = x.shape[-1]
    return x * lax.rsqrt(jnp.sum(x * x, axis=-1, keepdims=True) + n * EPS) * (g * n ** 0.5)


LOG2E = 1.4426950408889634
GELU_K = 0.7978845608028654
GELU_C = 0.044715


def _sigmoid(x):
    return 1.0 / (1.0 + jnp.exp2(x * (-LOG2E)))


def _gelu_tanh(x):
    t = (x * x) * (-2.0 * LOG2E * GELU_K * GELU_C) + (-2.0 * LOG2E * GELU_K)
    return x * (1.0 / (1.0 + jnp.exp2(x * t)))


def _log_sigmoid(x):
    return jnp.minimum(x, 0.0) - jnp.log(1.0 + jnp.exp(-jnp.abs(x)))


def _dot(a, b):
    return jnp.dot(a, b, preferred_element_type=F32)


def _dot_nt(a, b):
    return lax.dot_general(a, b, (((1,), (1,)), ((), ())), preferred_element_type=F32)


def _dot_exact(a, b):
    return jnp.dot(a, b, preferred_element_type=F32, precision=lax.Precision.HIGHEST)


def _pipeline_step(tick):
    i = pl.program_id(0)
    last = pl.num_programs(0) - 1

    @pl.when(i == 0)
    def _():
        tick(0, True, False, False)
        tick(1, True, True, False)

    @pl.when(jnp.logical_and(i > 0, i < last))
    def _():
        tick(0, True, True, True)
        tick(1, True, True, True)

    @pl.when(i == last)
    def _():
        tick(0, False, True, True)
        tick(1, False, False, True)


def _const_spec(shape):
    zeros = (0,) * len(shape)
    return pl.BlockSpec(shape, lambda *_: zeros, pipeline_mode=pl.Buffered(1))


def _ada_kernel(c_ref, w_ref, b_ref, o_ref):
    c = c_ref[...]
    s = (c * _sigmoid(c)).astype(BF16)
    o_ref[...] = _dot(s, w_ref[...].astype(BF16)) + b_ref[...]


def _ada_and_round_kernel(c_ref, w_ref, b_ref, wf_ref, o_ref, wb_ref):
    _ada_kernel(c_ref, w_ref, b_ref, o_ref)

    @pl.when(pl.program_id(0) == 0)
    def _():
        wb_ref[...] = wf_ref[...].astype(BF16)


def _ada(cs, w_ada, b_ada, n, w_f32):
    rows, d = cs.shape
    tn = 1024
    return pl.pallas_call(
        _ada_and_round_kernel,
        out_shape=[jax.ShapeDtypeStruct((rows, n), F32), jax.ShapeDtypeStruct(w_f32.shape, BF16)],
        grid=(n // tn,),
        in_specs=[
            pl.BlockSpec((rows, d), lambda j: (0, 0)),
            pl.BlockSpec((d, tn), lambda j: (0, j)),
            pl.BlockSpec((1, tn), lambda j: (0, j)),
            _const_spec(w_f32.shape),
        ],
        out_specs=[pl.BlockSpec((rows, tn), lambda j: (0, j)),
                   pl.BlockSpec(w_f32.shape, lambda j: (0, 0))],
        compiler_params=pltpu.CompilerParams(
            dimension_semantics=("arbitrary",), vmem_limit_bytes=VMEM_LIMIT_BYTES),
        name="ada",
    )(cs, w_ada, b_ada, w_f32)


def _inproj_norm(x, mod, g1, hb_ref):
    sh1 = mod[:, 0:D_MODEL]
    sc1 = mod[:, D_MODEL:2 * D_MODEL]
    hb_ref[...] = (_rms(x, g1 * (1.0 + sc1)) + sh1).astype(BF16)


def _inproj_project(hb_ref, w, z):
    hb = hb_ref[...]
    z["u"][...] = _dot_nt(hb, w["u"][...])
    z["v"][...] = _dot_nt(hb, w["v"][...])
    z["qk"][...] = _dot_nt(hb, w["qk"][...])
    z["o"][...] = _dot_nt(hb, w["o"][...])
    z["vt"][...] = _dot_nt(w["vv"][...], hb).astype(BF16)
    z["gt"][...] = _dot_nt(w["gt"][...], hb)


def _inproj_finish(z, c, out, seg):
    tb = z["u"].shape[0]
    for g in range(GROUPS):
        gs = slice(g * HD, (g + 1) * HD)
        vg = _rms(_gelu_tanh(z["v"][:, gs]), c["gv"][:, gs]).astype(BF16)
        for ch in range(tb // CHUNK):
            cs = slice(ch * CHUNK, (ch + 1) * CHUNK)
            mixed = _dot(c["ws"][g], vg[cs]) + c["bs"][:, gs]
            out["a"][cs, gs] = (_gelu_tanh(z["u"][cs, gs]) * mixed).astype(BF16)

    zqk = z["qk"][...]
    pos = lax.broadcasted_iota(jnp.int32, (tb, 1), 0) % seg
    prev = jnp.where(pos != 0, pltpu.roll(zqk, 1, 0), 0.0)
    nxt = jnp.where(pos != seg - 1, pltpu.roll(zqk, tb - 1, 0), 0.0)
    cw = c["cw"]
    y = c["cb"][...] + prev * cw[0:1, :] + zqk * cw[1:2, :] + nxt * cw[2:3, :]
    y = y * _sigmoid(y)
    out["qt"][...] = jnp.transpose(y[:, :D_MLSTM]).astype(BF16)
    out["k"][...] = (y[:, D_MLSTM:] * (HD ** -0.5)).astype(BF16)
    out["vt"][...] = z["vt"][...]
    out["o"][...] = z["o"][...]

    assert tb == SCAN
    gt = z["gt"][...] + c["bgt"][...]
    gi = pltpu.roll(gt, HEADS, 0)
    lf = _log_sigmoid(gt)
    lane = lax.broadcasted_iota(jnp.int32, (N_GATE, HD), 1)
    tiles = []
    carry = jnp.zeros((N_GATE, HD), F32)
    for t in range(SCAN // HD):
        prefix = lf[:, t * HD:(t + 1) * HD]
        shift = 1
        while shift < HD:
            prefix = prefix + jnp.where(lane >= shift, pltpu.roll(prefix, shift, 1), 0.0)
            shift *= 2
        tiles.append(prefix + carry)
        carry = carry + jnp.broadcast_to(prefix[:, HD - 1:HD], (N_GATE, HD))
    prefix = jnp.concatenate(tiles, axis=1)
    b_last = jnp.concatenate([carry] * (SCAN // HD), axis=1)
    row_bwd = lax.broadcasted_iota(jnp.int32, (N_GATE, SCAN), 0) >= N_GATE // N_DIR
    b_row = jnp.where(row_bwd, b_last - prefix + lf, prefix)
    g_row = b_last - b_row + gi
    g_max = jnp.broadcast_to(jnp.max(g_row, axis=1, keepdims=True), (N_GATE, SCAN))
    for sec, stat in enumerate((b_row, g_row, b_last, g_max, gi - b_row)):
        out["gr"][sec * N_GATE:(sec + 1) * N_GATE, :] = stat * LOG2E


_INPROJ_Z = ("u", "v", "qk", "o", "vt", "gt")


def _inproj_kernel(*refs, seg, blocks_per_mod, mod_row0, n_cast, ada_tail):
    refs = list(refs)
    take = lambda k: [refs.pop(0) for _ in range(k)]
    (x_ref, mod_ref, g1_ref, wit_ref,
     bgt_ref, cw_ref, cb_ref, ws_ref, bs_ref, gv_ref) = take(10)
    cast_in = take(n_cast)
    ada_in = take(3) if ada_tail else None
    ak_ref, qvt_ref, o_ref, gr_ref = take(4)
    cast_out = take(n_cast)
    ada_out = take(1)[0] if ada_tail else None
    hb_s, *z_s = refs
    tb = x_ref.shape[0] // 2
    mod_in = mod_row0 + jnp.minimum(pl.program_id(0), pl.num_programs(0) - 2) // blocks_per_mod
    w = {"u": wit_ref.at[0:OFF_V, :], "v": wit_ref.at[OFF_V:OFF_Q, :],
         "qk": wit_ref.at[OFF_Q:OFF_VV, :], "vv": wit_ref.at[OFF_VV:OFF_O, :],
         "o": wit_ref.at[OFF_O:OFF_G, :], "gt": wit_ref.at[OFF_G:D_IN, :]}
    c = {"gv": gv_ref, "ws": ws_ref, "bs": bs_ref, "cw": cw_ref, "cb": cb_ref, "bgt": bgt_ref}

    def tick(half, norm, project, finish):
        rows = slice(half * tb, (half + 1) * tb)
        cur, oth = half, 1 - half
        z_cur = {name: ref.at[cur] for name, ref in zip(_INPROJ_Z, z_s)}
        z_oth = {name: ref.at[oth] for name, ref in zip(_INPROJ_Z, z_s)}
        out = {"a": ak_ref.at[0, rows, :], "k": ak_ref.at[1, rows, :],
               "qt": qvt_ref.at[0:D_MLSTM, rows], "vt": qvt_ref.at[D_MLSTM:, rows],
               "o": o_ref.at[rows, :], "gr": gr_ref.at[:, rows]}
        if project:
            _inproj_project(hb_s.at[oth], w, z_oth)
        if norm:
            _inproj_norm(x_ref[rows, :], mod_ref[pl.ds(mod_in, 1), :], g1_ref[...], hb_s.at[cur])
        if finish:
            _inproj_finish(z_cur, c, out, seg)
        if half == 0:
            for src, dst in zip(cast_in, cast_out):
                dst[...] = src[...].astype(BF16)
            if ada_tail:
                _ada_kernel(*ada_in, ada_out)

    _pipeline_step(tick)


def _inproj(x, mod, mod_row0, rows_per_mod, seg, p, cast=(), ada_tail=None):
    n = x.shape[0]
    tb = TOKEN_BLOCK
    tb2 = 2 * tb
    assert tb % seg == 0 and n % tb2 == 0 and rows_per_mod % tb2 == 0 and D_GMLP == D_MLSTM
    nb = n // tb2
    blk_in = lambda i: jnp.minimum(i, nb - 1)
    blk_out = lambda i: jnp.maximum(i - 1, 0)
    tok = lambda width: pl.BlockSpec((tb2, width), lambda i: (blk_out(i), 0))
    in_specs = [
        pl.BlockSpec((tb2, D_MODEL), lambda i: (blk_in(i), 0)),
        _const_spec(mod.shape),
        _const_spec((1, D_MODEL)),
        _const_spec((D_IN, D_MODEL)),
        _const_spec((N_GATE, 1)),
        _const_spec((3, 2 * D_MLSTM)),
        _const_spec((1, 2 * D_MLSTM)),
        _const_spec((GROUPS, CHUNK, CHUNK)),
        _const_spec((CHUNK, D_GMLP)),
        _const_spec((1, D_GMLP)),
    ]
    out_shape = [
        jax.ShapeDtypeStruct((nb, 2, tb2, D_MLSTM), BF16),
        jax.ShapeDtypeStruct((nb, 2 * D_MLSTM, tb2), BF16),
        jax.ShapeDtypeStruct((n, D_MLSTM), F32),
        jax.ShapeDtypeStruct((nb, GR_ROWS, tb2), F32),
    ]
    per_block = lambda *shape: pl.BlockSpec(
        (None,) + shape, lambda i: (blk_out(i),) + (0,) * len(shape))
    out_specs = [per_block(2, tb2, D_MLSTM), per_block(2 * D_MLSTM, tb2), tok(D_MLSTM),
                 per_block(GR_ROWS, tb2)]
    for wf in cast:
        rows = wf.shape[0] // nb
        assert rows * nb == wf.shape[0] and rows % 16 == 0
        slab = pl.BlockSpec((rows, wf.shape[1]), lambda i: (blk_in(i), 0))
        in_specs.append(slab)
        out_specs.append(slab)
        out_shape.append(jax.ShapeDtypeStruct(wf.shape, BF16))
    side = list(cast)
    if ada_tail is not None:
        cs, w_ada, b_ada, col0 = ada_tail
        width = (w_ada.shape[1] - col0) // nb
        assert col0 % width == 0 and width * nb == w_ada.shape[1] - col0 and width % 128 == 0
        in_specs += [
            _const_spec(cs.shape),
            pl.BlockSpec((w_ada.shape[0], width), lambda i: (0, col0 // width + blk_in(i))),
            pl.BlockSpec((1, width), lambda i: (0, col0 // width + blk_in(i))),
        ]
        out_specs.append(pl.BlockSpec((cs.shape[0], width), lambda i: (0, blk_in(i))))
        out_shape.append(jax.ShapeDtypeStruct((cs.shape[0], w_ada.shape[1] - col0), F32))
        side += [cs, w_ada, b_ada]
    scratch_shapes = [
        pltpu.VMEM((2, tb, D_MODEL), BF16),
        pltpu.VMEM((2, tb, D_GMLP), F32),
        pltpu.VMEM((2, tb, D_GMLP), F32),
        pltpu.VMEM((2, tb, 2 * D_MLSTM), F32),
        pltpu.VMEM((2, tb, D_MLSTM), F32),
        pltpu.VMEM((2, D_MLSTM, tb), BF16),
        pltpu.VMEM((2, N_GATE, tb), F32),
    ]
    return pl.pallas_call(
        functools.partial(_inproj_kernel, seg=seg, blocks_per_mod=rows_per_mod // tb2,
                          mod_row0=mod_row0, n_cast=len(cast), ada_tail=ada_tail is not None),
        out_shape=out_shape,
        grid=(nb + 1,),
        in_specs=in_specs,
        out_specs=out_specs,
        scratch_shapes=scratch_shapes,
        compiler_params=pltpu.CompilerParams(
            dimension_semantics=("arbitrary",), vmem_limit_bytes=VMEM_LIMIT_BYTES),
        name="inproj",
    )(x, mod, p["g1"], p["wit"],
      p["bgt"], p["cw"], p["cb"], p["ws"], p["bs"], p["gv"], *side)


def _lane_broadcast_column(row):
    n = row.shape[1]
    tiles = [jnp.transpose(jnp.broadcast_to(row[:, t * HD:(t + 1) * HD], (HD, HD)))
             for t in range(n // HD)]
    col = jnp.concatenate(tiles, axis=0)
    return jnp.concatenate([col] * (n // HD), axis=1)


def _mlstm_unit(k, vt, kq, qs, d_row, b_row, g_row, bl_row, gmax_row, st, m_row, mask, carried):
    logw = jnp.where(mask, _lane_broadcast_column(d_row) + b_row, NEG)
    a = b_row + m_row
    mj = jnp.maximum(a, jnp.max(logw, axis=0, keepdims=True))
    w = jnp.exp2(logw - mj)
    s = kq * w
    num = _dot(vt, s.astype(BF16))
    den = jnp.sum(s, axis=0, keepdims=True)
    if carried:
        inter = jnp.exp2(a - mj)
        num = num + inter * qs[:HD]
        den = den + inter * qs[HD:HD + 1]
    h = num * (1.0 / jnp.maximum(jnp.abs(den), jnp.exp2(-mj)))
    m_new = jnp.maximum(bl_row + m_row, gmax_row)
    wc = jnp.exp2(g_row - m_new)
    pad_row = lax.broadcasted_iota(jnp.int32, (ST_ROWS - HD, wc.shape[1]), 0)
    v_aug = jnp.concatenate(
        [vt.astype(F32) * wc, jnp.where(pad_row == 0, wc, 0.0)], axis=0).astype(BF16)
    st_new = _dot(v_aug, k)
    if carried:
        decay = jnp.exp2(bl_row + m_row - m_new)
        st_new = decay[:, :HD] * st + st_new
    return h, st_new, m_new


def _mlstm_kernel(*refs, has_init, has_out, rounds, by_sequence):
    if has_init:
        s0_ref, m0_ref = refs[:2]
        refs = refs[2:]
    def split(k_ref, qvt_ref, gr_ref):
        return k_ref, qvt_ref.at[0:D_MLSTM, :], qvt_ref.at[D_MLSTM:, :], gr_ref

    if by_sequence:
        fwd_in = bwd_in = split(*refs[:3])
        refs = refs[3:]
    else:
        fwd_in, bwd_in = split(*refs[:3]), split(*refs[3:6])
        refs = refs[6:]
    hf_ref, hb_ref = refs[:2]
    refs = refs[2:]
    if has_out:
        co_ref, no_ref, mo_ref = refs[:3]
        refs = refs[3:]
    s_ref, m_ref = refs
    c = pl.program_id(1)
    nc = pl.num_programs(1)
    carried = not by_sequence

    if carried:
        @pl.when(c == 0)
        def _():
            if has_init:
                s_ref[...] = s0_ref[0]
                m_ref[...] = m0_ref[0]
            else:
                s_ref[...] = jnp.zeros_like(s_ref)
                m_ref[...] = jnp.zeros_like(m_ref)

    si = lax.broadcasted_iota(jnp.int32, (SCAN, SCAN), 0)
    ji = lax.broadcasted_iota(jnp.int32, (SCAN, SCAN), 1)
    dirs = ((fwd_in, hf_ref, si <= ji), (bwd_in, hb_ref, si >= ji))

    def chunk_cols(d, r):
        pos = rounds - 1 - r if (d == 1 and not by_sequence) else r
        return slice(pos * SCAN, (pos + 1) * SCAN)

    kq = {}
    for r in range(rounds):
        for d, ((k_ref, qt_ref, _, _), _, _) in enumerate(dirs):
            cs = chunk_cols(d, r)
            for hd in range(HEADS):
                hs = slice(hd * HD, (hd + 1) * HD)
                kq[r, d, hd] = _dot(k_ref[cs, hs], qt_ref[hs, cs])

    m_cur = None
    if carried:
        m_all = m_ref[...]
        m_cur = [m_all[u:u + 1] for u in range(N_UNIT)]
    zero_row = jnp.zeros((1, SCAN), F32)
    for r in range(rounds):
        slot = r * N_UNIT if by_sequence else 0
        for d, ((k_ref, qt_ref, vt_ref, gr_ref), h_ref, mask) in enumerate(dirs):
            cs = chunk_cols(d, r)
            for hd in range(HEADS):
                u = d * HEADS + hd
                row = d * 2 * HEADS + HEADS + hd
                hs = slice(hd * HD, (hd + 1) * HD)
                st = s_ref[slot + u] if carried else None
                qs = _dot(st.astype(BF16), qt_ref[hs, cs]) if carried else None
                h, st_new, m_new = _mlstm_unit(
                    k_ref[cs, hs], vt_ref[hs, cs], kq[r, d, hd], qs,
                    *(gr_ref[sec * N_GATE + row:sec * N_GATE + row + 1, cs]
                      for sec in (4, 0, 1, 2, 3)),
                    st, m_cur[u] if carried else zero_row, mask, carried)
                h_ref[hs, cs] = h
                s_ref[slot + u] = st_new
                if carried:
                    m_cur[u] = m_new
                else:
                    m_ref[slot + u:slot + u + 1, :] = m_new
    if carried:
        m_ref[...] = jnp.concatenate(m_cur, axis=0)

    if has_out:
        @pl.when(c == nc - 1)
        def _():
            for j in range(s_ref.shape[0]):
                q, u = divmod(j, N_UNIT)
                co_ref[q, u] = jnp.transpose(s_ref[j, 0:HD, :])
                no_ref[q, u:u + 1, :] = s_ref[j, HD:HD + 1, :]
                mo_ref[q, u:u + 1, :] = m_ref[j:j + 1, :]


def _mlstm(ak, qvt, gr, batch, rounds, s0=None, m0=None, want_state=False):
    width = rounds * SCAN
    assert ak.shape[2] == width, "scan steps use the input-projection blocks as they are"
    n = ak.shape[0] * width
    nc = n // batch // SCAN
    by_sequence = nc == 1
    has_init = s0 is not None
    if by_sequence:
        assert batch % rounds == 0 and not has_init
        grid = (batch // rounds, 1)
        slots = rounds
        fwd = bwd = lambda b, c: b
    else:
        assert nc % rounds == 0 and not want_state
        steps = nc // rounds
        grid = (batch, steps)
        slots = 1
        fwd = lambda b, c: b * steps + c
        bwd = lambda b, c: b * steps + steps - 1 - c

    def specs(ix):
        return [
            pl.BlockSpec((None, None, width, D_MLSTM), lambda b, c: (ix(b, c), 1, 0, 0)),
            pl.BlockSpec((None, 2 * D_MLSTM, width), lambda b, c: (ix(b, c), 0, 0)),
            pl.BlockSpec((None, GR_ROWS, width), lambda b, c: (ix(b, c), 0, 0)),
        ]

    in_specs = specs(fwd) if by_sequence else specs(fwd) + specs(bwd)
    args = [ak, qvt, gr] if by_sequence else [ak, qvt, gr, ak, qvt, gr]
    if has_init:
        in_specs = [
            pl.BlockSpec((1, N_UNIT, ST_ROWS, HD), lambda b, c: (b, 0, 0, 0)),
            pl.BlockSpec((1, N_UNIT, SCAN), lambda b, c: (b, 0, 0)),
        ] + in_specs
        args = [s0, m0] + args
    out_shape = [
        jax.ShapeDtypeStruct((n // width, D_MLSTM, width), F32),
        jax.ShapeDtypeStruct((n // width, D_MLSTM, width), F32),
    ]
    out_specs = [
        pl.BlockSpec((None, D_MLSTM, width), lambda b, c: (fwd(b, c), 0, 0)),
        pl.BlockSpec((None, D_MLSTM, width), lambda b, c: (bwd(b, c), 0, 0)),
    ]
    if want_state:
        out_shape += [
            jax.ShapeDtypeStruct((batch, N_UNIT, HD, HD), F32),
            jax.ShapeDtypeStruct((batch, N_UNIT, HD), F32),
            jax.ShapeDtypeStruct((batch, N_UNIT, SCAN), F32),
        ]
        out_specs += [
            pl.BlockSpec((slots, N_UNIT, HD, HD), lambda b, c: (b, 0, 0, 0)),
            pl.BlockSpec((slots, N_UNIT, HD), lambda b, c: (b, 0, 0)),
            pl.BlockSpec((slots, N_UNIT, SCAN), lambda b, c: (b, 0, 0)),
        ]
    return pl.pallas_call(
        functools.partial(_mlstm_kernel, has_init=has_init, has_out=want_state, rounds=rounds,
                          by_sequence=by_sequence),
        out_shape=out_shape,
        grid=grid,
        in_specs=in_specs,
        out_specs=out_specs,
        scratch_shapes=[pltpu.VMEM((slots * N_UNIT, ST_ROWS, HD), F32),
                        pltpu.VMEM((slots * N_UNIT, SCAN), F32)],
        compiler_params=pltpu.CompilerParams(
            dimension_semantics=("arbitrary", "arbitrary"), vmem_limit_bytes=VMEM_LIMIT_BYTES),
        name="mlstm",
    )(*args)


def _outffn_mix(x, a, hft, hbt, o, mod, c, x1_ref, h2_ref):
    ga1 = mod[:, 0:D_MODEL]
    sh2 = mod[:, D_MODEL:2 * D_MODEL]
    sc2 = mod[:, 2 * D_MODEL:3 * D_MODEL]
    hs = jnp.transpose(hft + hbt)
    sig = _sigmoid(o)
    parts = []
    for hd in range(HEADS):
        sl = slice(hd * HD, (hd + 1) * HD)
        parts.append(_rms(hs[:, sl], c["gh"][:, sl]) * sig[:, sl])
    b_out = jnp.concatenate(parts, axis=-1).astype(BF16)
    mix = _dot(a, c["wout"][0:D_GMLP, :]) + _dot(b_out, c["wout"][D_GMLP:, :])
    x1 = x + ga1 * mix
    x1_ref[...] = x1
    h2_ref[...] = (_rms(x1, c["g2"][...] * (1.0 + sc2)) + sh2).astype(BF16)


def _outffn_up(h2_ref, c, f_ref):
    h2 = h2_ref[...]
    u = _dot(h2, c["w1"][...])
    g = _dot(h2, c["w3"][...])
    f_ref[...] = (u * _sigmoid(u) * g).astype(BF16)


def _outffn_down(f_ref, x1_ref, mod, c, y_ref):
    ga2 = mod[:, 3 * D_MODEL:4 * D_MODEL]
    x2 = x1_ref[...] + ga2 * _dot(f_ref[...], c["w2"][...])
    y_ref[...] = _rms(x2, c["gf"][...])


def _outffn_kernel(x_ref, a_ref, hft_ref, hbt_ref, o_ref, mod_ref, gh_ref, wout_hbm,
                   g2_ref, w1_hbm, w3_hbm, w2_hbm, gf_ref, y_ref,
                   x1_s, h2_s, f_s, wout_s, w1_s, w3_s, w2_s, w_sem,
                   *, blocks_per_mod, mod_row0):
    tb = x_ref.shape[0] // 2
    i = pl.program_id(0)
    mod_in = mod_row0 + jnp.minimum(i, pl.num_programs(0) - 2) // blocks_per_mod
    mod_out = mod_row0 + jnp.maximum(i - 1, 0) // blocks_per_mod
    c = {"gh": gh_ref, "wout": wout_s, "g2": g2_ref, "w1": w1_s, "w3": w3_s, "w2": w2_s,
         "gf": gf_ref}
    weight_copy = {
        name: pltpu.make_async_copy(src, dst, w_sem.at[j])
        for j, (name, src, dst) in enumerate((("wout", wout_hbm, wout_s), ("w1", w1_hbm, w1_s),
                                              ("w3", w3_hbm, w3_s), ("w2", w2_hbm, w2_s)))}

    def tick(half, mix, up, down):
        rows = slice(half * tb, (half + 1) * tb)
        cur, oth = half, 1 - half
        first_tick = mix and not up and not down
        second_tick = mix and up and not down
        if first_tick:
            for name in ("wout", "w1", "w3", "w2"):
                weight_copy[name].start()
            weight_copy["wout"].wait()
        if down:
            _outffn_down(f_s.at[cur], x1_s.at[cur], mod_ref[pl.ds(mod_out, 1), :], c,
                         y_ref.at[rows, :])
        if mix:
            _outffn_mix(x_ref[rows, :], a_ref[rows, :], hft_ref[:, rows], hbt_ref[:, rows],
                        o_ref[rows, :], mod_ref[pl.ds(mod_in, 1), :], c, x1_s.at[cur],
                        h2_s.at[cur])
        if second_tick:
            weight_copy["w1"].wait()
            weight_copy["w3"].wait()
        if up:
            _outffn_up(h2_s.at[oth], c, f_s.at[oth])
        if second_tick:
            weight_copy["w2"].wait()

    _pipeline_step(tick)


def _outffn(x, ak, hft, hbt, o, mod, mod_row0, rows_per_mod, p):
    n = x.shape[0]
    tb = TOKEN_BLOCK
    tb2 = 2 * tb
    assert n % tb2 == 0 and rows_per_mod % tb2 == 0
    nb = n // tb2
    d_ff = p["w1"].shape[1]
    blk_in = lambda i: jnp.minimum(i, nb - 1)
    blk_out = lambda i: jnp.maximum(i - 1, 0)
    tok = lambda w: pl.BlockSpec((tb2, w), lambda i: (blk_in(i), 0))
    tok_t = lambda h: pl.BlockSpec((None, h, tb2), lambda i: (blk_in(i), 0, 0))
    weight_hbm = pl.BlockSpec(memory_space=pl.ANY)
    in_specs = [
        tok(D_MODEL),
        pl.BlockSpec((None, None, tb2, D_GMLP), lambda i: (blk_in(i), 0, 0, 0)),
        tok_t(D_MLSTM), tok_t(D_MLSTM), tok(D_MLSTM),
        _const_spec(mod.shape),
        _const_spec((1, D_MLSTM)),
        weight_hbm,
        _const_spec((1, D_MODEL)),
        weight_hbm,
        weight_hbm,
        weight_hbm,
        _const_spec((1, D_MODEL)),
    ]
    scratch_shapes = [
        pltpu.VMEM((2, tb, D_MODEL), F32),
        pltpu.VMEM((2, tb, D_MODEL), BF16),
        pltpu.VMEM((2, tb, d_ff), BF16),
        pltpu.VMEM(p["wout"].shape, BF16),
        pltpu.VMEM(p["w1"].shape, BF16),
        pltpu.VMEM(p["w3"].shape, BF16),
        pltpu.VMEM(p["w2"].shape, BF16),
        pltpu.SemaphoreType.DMA((4,)),
    ]
    return pl.pallas_call(
        functools.partial(_outffn_kernel, blocks_per_mod=rows_per_mod // tb2, mod_row0=mod_row0),
        out_shape=jax.ShapeDtypeStruct((n, D_MODEL), F32),
        grid=(nb + 1,),
        in_specs=in_specs,
        out_specs=pl.BlockSpec((tb2, D_MODEL), lambda i: (blk_out(i), 0)),
        scratch_shapes=scratch_shapes,
        compiler_params=pltpu.CompilerParams(
            dimension_semantics=("arbitrary",), vmem_limit_bytes=VMEM_LIMIT_BYTES),
        name="outffn",
    )(x, ak, hft, hbt, o, mod, p["gh"], p["wout"], p["g2"], p["w1"], p["w3"], p["w2"], p["gf"])


def _layer_params(l, g_norm1, b_gate, w_s, b_s, g_v, conv_w, conv_b, g_h, g_norm2, g_final):
    return {
        "g1": g_norm1[l][None, :],
        "bgt": b_gate[l][:, None],
        "cw": conv_w[l], "cb": conv_b[l][None, :],
        "ws": w_s[l].astype(BF16),
        "bs": jnp.repeat(b_s[l].T, HD, axis=1),
        "gv": g_v[l].reshape(1, D_GMLP),
        "gh": g_h[l].reshape(1, D_MLSTM),
        "g2": g_norm2[l][None, :],
        "gf": g_final[None, :],
    }


_LATE_WEIGHTS = ("wout", "w1", "w3", "w2")


def _trunk_front(x, mod_row0, rows_per_mod, seg, p, late=None):
    cast, ada_tail = ((), None) if late is None else (
        tuple(late[0][name] for name in _LATE_WEIGHTS), late[1])
    ak, qvt, o, gr, *side = _inproj(x, p["mod_head"], mod_row0, rows_per_mod, seg, p, cast,
                                    ada_tail)
    if late is not None:
        p = dict(p, **dict(zip(_LATE_WEIGHTS, side[:-1])))
        p["mod_tail"] = side[-1]
    return (ak, qvt, o), gr, p


def _trunk_back(x, front, gr, mod_row0, rows_per_mod, batch, rounds, p, s0=None, m0=None,
                want_state=False):
    ak, qvt, o = front
    hft, hbt, *state = _mlstm(ak, qvt, gr, batch, rounds, s0, m0, want_state)
    y = _outffn(x, ak, hft, hbt, o, p["mod_tail"], mod_row0, rows_per_mod, p)
    return y, state


def kernel(x_prompt, x_sample, state_C, state_n, state_m, c, c_ctx, w_ada, b_ada, g_norm1, w_in,
           b_gate, w_s, b_s, g_v, conv_w, conv_b, g_h, w_out, g_norm2, w1, w3, w2, g_final):
    bp, tp, d = x_prompt.shape
    bs_, ts, _ = x_sample.shape
    depth = w_in.shape[0]
    assert depth == 1, "final norm is fused into the layer's last kernel"
    xp = x_prompt.reshape(bp * tp, d)
    xs = x_sample.reshape(bs_ * ts, d)

    cs = jnp.zeros((8, d), F32).at[0].set(c_ctx).at[1:1 + bs_].set(c)
    new_c, new_n, new_m = [], [], []
    for l in range(depth):
        p = _layer_params(l, g_norm1, b_gate, w_s, b_s, g_v, conv_w, conv_b, g_h, g_norm2, g_final)
        late_f32 = {"wout": w_out[l], "w1": w1[l], "w3": w3[l], "w2": w2[l]}
        p["mod_head"], p["wit"] = _ada(cs, w_ada[l], b_ada[l][None, :], 2 * d, w_in[l].T)
        ada_tail = (cs, w_ada[l], b_ada[l][None, :], 2 * d)

        front_lat, gr_lat, p = _trunk_front(xs, 1, ts, GRID_W, p, (late_f32, ada_tail))
        front_ctx, gr_ctx, _ = _trunk_front(xp, 0, bp * tp, tp, p)

        xp, (c_ctx_out, n_ctx_out, m_ctx_out) = _trunk_back(
            xp, front_ctx, gr_ctx, 0, bp * tp, bp, CTX_SEQS_PER_STEP, p, want_state=True)
        new_c.append(c_ctx_out.reshape(bp, N_DIR, HEADS, HD, HD))
        new_n.append(n_ctx_out.reshape(bp, N_DIR, HEADS, HD))
        new_m.append((m_ctx_out[..., 0] * (1.0 / LOG2E)).reshape(bp, N_DIR, HEADS))

        s0 = jnp.concatenate(
            [jnp.swapaxes(state_C[:, l], -1, -2), state_n[:, l][..., None, :],
             jnp.zeros((bs_, N_DIR, HEADS, ST_ROWS - HD - 1, HD), F32)],
            axis=-2).reshape(bs_, N_UNIT, ST_ROWS, HD)
        m0 = jnp.broadcast_to((state_m[:, l] * LOG2E).reshape(bs_, N_UNIT, 1),
                              (bs_, N_UNIT, SCAN))
        xs, _ = _trunk_back(xs, front_lat, gr_lat, 1, ts, bs_, LAT_CHUNKS_PER_STEP, p, s0=s0, m0=m0)

    return (xp.reshape(bp, tp, d), xs.reshape(bs_, ts, d),
            jnp.stack(new_c, axis=1), jnp.stack(new_n, axis=1), jnp.stack(new_m, axis=1))
```

```python
import functools

import jax
import jax.numpy as jnp
from jax import lax
from jax.experimental import pallas as pl
from jax.experimental.pallas import tpu as pltpu

D_MODEL = 1024
D_GMLP = 512
D_MLSTM = 512
GROUPS = 4
HEADS = 4
HD = 128
CHUNK = 128
SCAN = 256
N_DIR = 2
N_UNIT = N_DIR * HEADS
GRID_W = 64
OFF_V, OFF_Q, OFF_VV, OFF_O, OFF_G = 512, 1024, 2048, 2560, 3072
D_IN = OFF_G + 2 * N_DIR * HEADS
EPS = 1e-6
NEG = -1e30
ST_ROWS = HD + 16
N_GATE = 2 * N_UNIT
GR_ROWS = 5 * N_GATE

TOKEN_BLOCK = 256
CTX_SEQS_PER_STEP = 2
LAT_CHUNKS_PER_STEP = 2
VMEM_LIMIT_BYTES = 56 * 1024 * 1024

F32 = jnp.float32
BF16 = jnp.bfloat16


def _rms(x, g):
    n = x.shape[-1]
    return x * lax.rsqrt(jnp.sum(x * x, axis=-1, keepdims=True) + n * EPS) * (g * n ** 0.5)


LOG2E = 1.4426950408889634
GELU_K = 0.7978845608028654
GELU_C = 0.044715


def _sigmoid(x):
    return 1.0 / (1.0 + jnp.exp2(x * (-LOG2E)))


def _gelu_tanh(x):
    t = (x * x) * (-2.0 * LOG2E * GELU_K * GELU_C) + (-2.0 * LOG2E * GELU_K)
    return x * (1.0 / (1.0 + jnp.exp2(x * t)))


def _log_sigmoid(x):
    return jnp.minimum(x, 0.0) - jnp.log(1.0 + jnp.exp(-jnp.abs(x)))


def _dot(a, b):
    return jnp.dot(a, b, preferred_element_type=F32)


def _dot_nt(a, b):
    return lax.dot_general(a, b, (((1,), (1,)), ((), ())), preferred_element_type=F32)


def _dot_exact(a, b):
    return jnp.dot(a, b, preferred_element_type=F32, precision=lax.Precision.HIGHEST)


def _pipeline_step(tick):
    i = pl.program_id(0)
    last = pl.num_programs(0) - 1

    @pl.when(i == 0)
    def _():
        tick(0, True, False, False)
        tick(1, True, True, False)

    @pl.when(jnp.logical_and(i > 0, i < last))
    def _():
        tick(0, True, True, True)
        tick(1, True, True, True)

    @pl.when(i == last)
    def _():
        tick(0, False, True, True)
        tick(1, False, False, True)


def _const_spec(shape):
    zeros = (0,) * len(shape)
    return pl.BlockSpec(shape, lambda *_: zeros, pipeline_mode=pl.Buffered(1))


def _ada_kernel(c_ref, w_ref, b_ref, o_ref):
    c = c_ref[...]
    s = (c * _sigmoid(c)).astype(BF16)
    o_ref[...] = _dot(s, w_ref[...].astype(BF16)) + b_ref[...]


def _ada_and_round_kernel(c_ref, w_ref, b_ref, wf_ref, o_ref, wb_ref):
    _ada_kernel(c_ref, w_ref, b_ref, o_ref)

    @pl.when(pl.program_id(0) == 0)
    def _():
        wb_ref[...] = wf_ref[...].astype(BF16)


def _ada(cs, w_ada, b_ada, n, w_f32):
    rows, d = cs.shape
    tn = 1024
    return pl.pallas_call(
        _ada_and_round_kernel,
        out_shape=[jax.ShapeDtypeStruct((rows, n), F32), jax.ShapeDtypeStruct(w_f32.shape, BF16)],
        grid=(n // tn,),
        in_specs=[
            pl.BlockSpec((rows, d), lambda j: (0, 0)),
            pl.BlockSpec((d, tn), lambda j: (0, j)),
            pl.BlockSpec((1, tn), lambda j: (0, j)),
            _const_spec(w_f32.shape),
        ],
        out_specs=[pl.BlockSpec((rows, tn), lambda j: (0, j)),
                   pl.BlockSpec(w_f32.shape, lambda j: (0, 0))],
        compiler_params=pltpu.CompilerParams(
            dimension_semantics=("arbitrary",), vmem_limit_bytes=VMEM_LIMIT_BYTES),
        name="ada",
    )(cs, w_ada, b_ada, w_f32)


def _inproj_norm(x, mod, g1, hb_ref):
    sh1 = mod[:, 0:D_MODEL]
    sc1 = mod[:, D_MODEL:2 * D_MODEL]
    hb_ref[...] = (_rms(x, g1 * (1.0 + sc1)) + sh1).astype(BF16)


def _inproj_project(hb_ref, w, z):
    hb = hb_ref[...]
    z["u"][...] = _dot_nt(hb, w["u"][...])
    z["v"][...] = _dot_nt(hb, w["v"][...])
    z["qk"][...] = _dot_nt(hb, w["qk"][...])
    z["o"][...] = _dot_nt(hb, w["o"][...])
    z["vt"][...] = _dot_nt(w["vv"][...], hb).astype(BF16)
    z["gt"][...] = _dot_nt(w["gt"][...], hb)


def _inproj_finish(z, c, out, seg):
    tb = z["u"].shape[0]
    for g in range(GROUPS):
        gs = slice(g * HD, (g + 1) * HD)
        vg = _rms(_gelu_tanh(z["v"][:, gs]), c["gv"][:, gs]).astype(BF16)
        for ch in range(tb // CHUNK):
            cs = slice(ch * CHUNK, (ch + 1) * CHUNK)
            mixed = _dot(c["ws"][g], vg[cs]) + c["bs"][:, gs]
            out["a"][cs, gs] = (_gelu_tanh(z["u"][cs, gs]) * mixed).astype(BF16)

    zqk = z["qk"][...]
    pos = lax.broadcasted_iota(jnp.int32, (tb, 1), 0) % seg
    prev = jnp.where(pos != 0, pltpu.roll(zqk, 1, 0), 0.0)
    nxt = jnp.where(pos != seg - 1, pltpu.roll(zqk, tb - 1, 0), 0.0)
    cw = c["cw"]
    y = c["cb"][...] + prev * cw[0:1, :] + zqk * cw[1:2, :] + nxt * cw[2:3, :]
    y = y * _sigmoid(y)
    out["qt"][...] = jnp.transpose(y[:, :D_MLSTM]).astype(BF16)
    out["k"][...] = (y[:, D_MLSTM:] * (HD ** -0.5)).astype(BF16)
    out["vt"][...] = z["vt"][...]
    out["o"][...] = z["o"][...]

    assert tb == SCAN
    gt = z["gt"][...] + c["bgt"][...]
    gi = pltpu.roll(gt, HEADS, 0)
    lf = _log_sigmoid(gt)
    lane = lax.broadcasted_iota(jnp.int32, (N_GATE, HD), 1)
    tiles = []
    carry = jnp.zeros((N_GATE, HD), F32)
    for t in range(SCAN // HD):
        prefix = lf[:, t * HD:(t + 1) * HD]
        shift = 1
        while shift < HD:
            prefix = prefix + jnp.where(lane >= shift, pltpu.roll(prefix, shift, 1), 0.0)
            shift *= 2
        tiles.append(prefix + carry)
        carry = carry + jnp.broadcast_to(prefix[:, HD - 1:HD], (N_GATE, HD))
    prefix = jnp.concatenate(tiles, axis=1)
    b_last = jnp.concatenate([carry] * (SCAN // HD), axis=1)
    row_bwd = lax.broadcasted_iota(jnp.int32, (N_GATE, SCAN), 0) >= N_GATE // N_DIR
    b_row = jnp.where(row_bwd, b_last - prefix + lf, prefix)
    g_row = b_last - b_row + gi
    g_max = jnp.broadcast_to(jnp.max(g_row, axis=1, keepdims=True), (N_GATE, SCAN))
    for sec, stat in enumerate((b_row, g_row, b_last, g_max, gi - b_row)):
        out["gr"][sec * N_GATE:(sec + 1) * N_GATE, :] = stat * LOG2E


_INPROJ_Z = ("u", "v", "qk", "o", "vt", "gt")


def _inproj_kernel(*refs, seg, blocks_per_mod, mod_row0, n_cast, ada_tail):
    refs = list(refs)
    take = lambda k: [refs.pop(0) for _ in range(k)]
    (x_ref, mod_ref, g1_ref, wit_ref,
     bgt_ref, cw_ref, cb_ref, ws_ref, bs_ref, gv_ref) = take(10)
    cast_in = take(n_cast)
    ada_in = take(3) if ada_tail else None
    ak_ref, qvt_ref, o_ref, gr_ref = take(4)
    cast_out = take(n_cast)
    ada_out = take(1)[0] if ada_tail else None
    hb_s, *z_s = refs
    tb = x_ref.shape[0] // 2
    mod_in = mod_row0 + jnp.minimum(pl.program_id(0), pl.num_programs(0) - 2) // blocks_per_mod
    w = {"u": wit_ref.at[0:OFF_V, :], "v": wit_ref.at[OFF_V:OFF_Q, :],
         "qk": wit_ref.at[OFF_Q:OFF_VV, :], "vv": wit_ref.at[OFF_VV:OFF_O, :],
         "o": wit_ref.at[OFF_O:OFF_G, :], "gt": wit_ref.at[OFF_G:D_IN, :]}
    c = {"gv": gv_ref, "ws": ws_ref, "bs": bs_ref, "cw": cw_ref, "cb": cb_ref, "bgt": bgt_ref}

    def tick(half, norm, project, finish):
        rows = slice(half * tb, (half + 1) * tb)
        cur, oth = half, 1 - half
        z_cur = {name: ref.at[cur] for name, ref in zip(_INPROJ_Z, z_s)}
        z_oth = {name: ref.at[oth] for name, ref in zip(_INPROJ_Z, z_s)}
        out = {"a": ak_ref.at[rows, 0:D_GMLP], "k": ak_ref.at[rows, D_GMLP:],
               "qt": qvt_ref.at[0:D_MLSTM, rows], "vt": qvt_ref.at[D_MLSTM:, rows],
               "o": o_ref.at[rows, :], "gr": gr_ref.at[:, rows]}
        if project:
            _inproj_project(hb_s.at[oth], w, z_oth)
        if norm:
            _inproj_norm(x_ref[rows, :], mod_ref[pl.ds(mod_in, 1), :], g1_ref[...], hb_s.at[cur])
        if finish:
            _inproj_finish(z_cur, c, out, seg)
        if half == 0:
            for src, dst in zip(cast_in, cast_out):
                dst[...] = src[...].astype(BF16)
            if ada_tail:
                _ada_kernel(*ada_in, ada_out)

    _pipeline_step(tick)


def _inproj(x, mod, mod_row0, rows_per_mod, seg, p, cast=(), ada_tail=None):
    n = x.shape[0]
    tb = TOKEN_BLOCK
    tb2 = 2 * tb
    assert tb % seg == 0 and n % tb2 == 0 and rows_per_mod % tb2 == 0
    nb = n // tb2
    blk_in = lambda i: jnp.minimum(i, nb - 1)
    blk_out = lambda i: jnp.maximum(i - 1, 0)
    tok = lambda width: pl.BlockSpec((tb2, width), lambda i: (blk_out(i), 0))
    tok_t = lambda height: pl.BlockSpec((height, tb2), lambda i: (0, blk_out(i)))
    in_specs = [
        pl.BlockSpec((tb2, D_MODEL), lambda i: (blk_in(i), 0)),
        _const_spec(mod.shape),
        _const_spec((1, D_MODEL)),
        _const_spec((D_IN, D_MODEL)),
        _const_spec((N_GATE, 1)),
        _const_spec((3, 2 * D_MLSTM)),
        _const_spec((1, 2 * D_MLSTM)),
        _const_spec((GROUPS, CHUNK, CHUNK)),
        _const_spec((CHUNK, D_GMLP)),
        _const_spec((1, D_GMLP)),
    ]
    out_shape = [
        jax.ShapeDtypeStruct((n, D_GMLP + D_MLSTM), BF16),
        jax.ShapeDtypeStruct((2 * D_MLSTM, n), BF16),
        jax.ShapeDtypeStruct((n, D_MLSTM), F32),
        jax.ShapeDtypeStruct((GR_ROWS, n), F32),
    ]
    out_specs = [tok(D_GMLP + D_MLSTM), tok_t(2 * D_MLSTM), tok(D_MLSTM), tok_t(GR_ROWS)]
    for wf in cast:
        rows = wf.shape[0] // nb
        assert rows * nb == wf.shape[0] and rows % 16 == 0
        slab = pl.BlockSpec((rows, wf.shape[1]), lambda i: (blk_in(i), 0))
        in_specs.append(slab)
        out_specs.append(slab)
        out_shape.append(jax.ShapeDtypeStruct(wf.shape, BF16))
    side = list(cast)
    if ada_tail is not None:
        cs, w_ada, b_ada, col0 = ada_tail
        width = (w_ada.shape[1] - col0) // nb
        assert col0 % width == 0 and width * nb == w_ada.shape[1] - col0 and width % 128 == 0
        in_specs += [
            _const_spec(cs.shape),
            pl.BlockSpec((w_ada.shape[0], width), lambda i: (0, col0 // width + blk_in(i))),
            pl.BlockSpec((1, width), lambda i: (0, col0 // width + blk_in(i))),
        ]
        out_specs.append(pl.BlockSpec((cs.shape[0], width), lambda i: (0, blk_in(i))))
        out_shape.append(jax.ShapeDtypeStruct((cs.shape[0], w_ada.shape[1] - col0), F32))
        side += [cs, w_ada, b_ada]
    scratch_shapes = [
        pltpu.VMEM((2, tb, D_MODEL), BF16),
        pltpu.VMEM((2, tb, D_GMLP), F32),
        pltpu.VMEM((2, tb, D_GMLP), F32),
        pltpu.VMEM((2, tb, 2 * D_MLSTM), F32),
        pltpu.VMEM((2, tb, D_MLSTM), F32),
        pltpu.VMEM((2, D_MLSTM, tb), BF16),
        pltpu.VMEM((2, N_GATE, tb), F32),
    ]
    return pl.pallas_call(
        functools.partial(_inproj_kernel, seg=seg, blocks_per_mod=rows_per_mod // tb2,
                          mod_row0=mod_row0, n_cast=len(cast), ada_tail=ada_tail is not None),
        out_shape=out_shape,
        grid=(nb + 1,),
        in_specs=in_specs,
        out_specs=out_specs,
        scratch_shapes=scratch_shapes,
        compiler_params=pltpu.CompilerParams(
            dimension_semantics=("arbitrary",), vmem_limit_bytes=VMEM_LIMIT_BYTES),
        name="inproj",
    )(x, mod, p["g1"], p["wit"],
      p["bgt"], p["cw"], p["cb"], p["ws"], p["bs"], p["gv"], *side)


def _lane_broadcast_column(row):
    n = row.shape[1]
    tiles = [jnp.transpose(jnp.broadcast_to(row[:, t * HD:(t + 1) * HD], (HD, HD)))
             for t in range(n // HD)]
    col = jnp.concatenate(tiles, axis=0)
    return jnp.concatenate([col] * (n // HD), axis=1)


def _mlstm_unit(k, vt, kq, qs, d_row, b_row, g_row, bl_row, gmax_row, st, m_row, mask, carried):
    logw = jnp.where(mask, _lane_broadcast_column(d_row) + b_row, NEG)
    a = b_row + m_row
    mj = jnp.maximum(a, jnp.max(logw, axis=0, keepdims=True))
    w = jnp.exp2(logw - mj)
    s = kq * w
    num = _dot(vt, s.astype(BF16))
    den = jnp.sum(s, axis=0, keepdims=True)
    if carried:
        inter = jnp.exp2(a - mj)
        num = num + inter * qs[:HD]
        den = den + inter * qs[HD:HD + 1]
    h = num * (1.0 / jnp.maximum(jnp.abs(den), jnp.exp2(-mj)))
    m_new = jnp.maximum(bl_row + m_row, gmax_row)
    wc = jnp.exp2(g_row - m_new)
    pad_row = lax.broadcasted_iota(jnp.int32, (ST_ROWS - HD, wc.shape[1]), 0)
    v_aug = jnp.concatenate(
        [vt.astype(F32) * wc, jnp.where(pad_row == 0, wc, 0.0)], axis=0).astype(BF16)
    st_new = _dot(v_aug, k)
    if carried:
        decay = jnp.exp2(bl_row + m_row - m_new)
        st_new = decay[:, :HD] * st + st_new
    return h, st_new, m_new


def _mlstm_kernel(*refs, has_init, has_out, rounds, by_sequence):
    if has_init:
        s0_ref, m0_ref = refs[:2]
        refs = refs[2:]
    def split(k_ref, qvt_ref, gr_ref):
        return k_ref, qvt_ref.at[0:D_MLSTM, :], qvt_ref.at[D_MLSTM:, :], gr_ref

    if by_sequence:
        fwd_in = bwd_in = split(*refs[:3])
        refs = refs[3:]
    else:
        fwd_in, bwd_in = split(*refs[:3]), split(*refs[3:6])
        refs = refs[6:]
    hf_ref, hb_ref = refs[:2]
    refs = refs[2:]
    if has_out:
        co_ref, no_ref, mo_ref = refs[:3]
        refs = refs[3:]
    s_ref, m_ref = refs
    c = pl.program_id(1)
    nc = pl.num_programs(1)
    carried = not by_sequence

    if carried:
        @pl.when(c == 0)
        def _():
            if has_init:
                s_ref[...] = s0_ref[0]
                m_ref[...] = m0_ref[0]
            else:
                s_ref[...] = jnp.zeros_like(s_ref)
                m_ref[...] = jnp.zeros_like(m_ref)

    si = lax.broadcasted_iota(jnp.int32, (SCAN, SCAN), 0)
    ji = lax.broadcasted_iota(jnp.int32, (SCAN, SCAN), 1)
    dirs = ((fwd_in, hf_ref, si <= ji), (bwd_in, hb_ref, si >= ji))

    def chunk_cols(d, r):
        pos = rounds - 1 - r if (d == 1 and not by_sequence) else r
        return slice(pos * SCAN, (pos + 1) * SCAN)

    kq = {}
    for r in range(rounds):
        for d, ((k_ref, qt_ref, _, _), _, _) in enumerate(dirs):
            cs = chunk_cols(d, r)
            for hd in range(HEADS):
                hs = slice(hd * HD, (hd + 1) * HD)
                kq[r, d, hd] = _dot(k_ref[cs, hs], qt_ref[hs, cs])

    m_cur = None
    if carried:
        m_all = m_ref[...]
        m_cur = [m_all[u:u + 1] for u in range(N_UNIT)]
    zero_row = jnp.zeros((1, SCAN), F32)
    for r in range(rounds):
        slot = r * N_UNIT if by_sequence else 0
        for d, ((k_ref, qt_ref, vt_ref, gr_ref), h_ref, mask) in enumerate(dirs):
            cs = chunk_cols(d, r)
            for hd in range(HEADS):
                u = d * HEADS + hd
                row = d * 2 * HEADS + HEADS + hd
                hs = slice(hd * HD, (hd + 1) * HD)
                st = s_ref[slot + u] if carried else None
                qs = _dot(st.astype(BF16), qt_ref[hs, cs]) if carried else None
                h, st_new, m_new = _mlstm_unit(
                    k_ref[cs, hs], vt_ref[hs, cs], kq[r, d, hd], qs,
                    *(gr_ref[sec * N_GATE + row:sec * N_GATE + row + 1, cs]
                      for sec in (4, 0, 1, 2, 3)),
                    st, m_cur[u] if carried else zero_row, mask, carried)
                h_ref[hs, cs] = h
                s_ref[slot + u] = st_new
                if carried:
                    m_cur[u] = m_new
                else:
                    m_ref[slot + u:slot + u + 1, :] = m_new
    if carried:
        m_ref[...] = jnp.concatenate(m_cur, axis=0)

    if has_out:
        @pl.when(c == nc - 1)
        def _():
            for j in range(s_ref.shape[0]):
                q, u = divmod(j, N_UNIT)
                co_ref[q, u] = jnp.transpose(s_ref[j, 0:HD, :])
                no_ref[q, u:u + 1, :] = s_ref[j, HD:HD + 1, :]
                mo_ref[q, u:u + 1, :] = m_ref[j:j + 1, :]


def _mlstm(ak, qvt, gr, batch, rounds, s0=None, m0=None, want_state=False):
    n = ak.shape[0]
    nc = n // batch // SCAN
    by_sequence = nc == 1
    width = rounds * SCAN
    has_init = s0 is not None
    if by_sequence:
        assert batch % rounds == 0 and not has_init
        grid = (batch // rounds, 1)
        slots = rounds
        fwd = bwd = lambda b, c: b
    else:
        assert nc % rounds == 0 and not want_state
        steps = nc // rounds
        grid = (batch, steps)
        slots = 1
        fwd = lambda b, c: b * steps + c
        bwd = lambda b, c: b * steps + steps - 1 - c

    def specs(ix):
        return [
            pl.BlockSpec((width, D_MLSTM), lambda b, c: (ix(b, c), 1)),
            pl.BlockSpec((2 * D_MLSTM, width), lambda b, c: (0, ix(b, c))),
            pl.BlockSpec((GR_ROWS, width), lambda b, c: (0, ix(b, c))),
        ]

    in_specs = specs(fwd) if by_sequence else specs(fwd) + specs(bwd)
    args = [ak, qvt, gr] if by_sequence else [ak, qvt, gr, ak, qvt, gr]
    if has_init:
        in_specs = [
            pl.BlockSpec((1, N_UNIT, ST_ROWS, HD), lambda b, c: (b, 0, 0, 0)),
            pl.BlockSpec((1, N_UNIT, SCAN), lambda b, c: (b, 0, 0)),
        ] + in_specs
        args = [s0, m0] + args
    out_shape = [
        jax.ShapeDtypeStruct((D_MLSTM, n), F32),
        jax.ShapeDtypeStruct((D_MLSTM, n), F32),
    ]
    out_specs = [
        pl.BlockSpec((D_MLSTM, width), lambda b, c: (0, fwd(b, c))),
        pl.BlockSpec((D_MLSTM, width), lambda b, c: (0, bwd(b, c))),
    ]
    if want_state:
        out_shape += [
            jax.ShapeDtypeStruct((batch, N_UNIT, HD, HD), F32),
            jax.ShapeDtypeStruct((batch, N_UNIT, HD), F32),
            jax.ShapeDtypeStruct((batch, N_UNIT, SCAN), F32),
        ]
        out_specs += [
            pl.BlockSpec((slots, N_UNIT, HD, HD), lambda b, c: (b, 0, 0, 0)),
            pl.BlockSpec((slots, N_UNIT, HD), lambda b, c: (b, 0, 0)),
            pl.BlockSpec((slots, N_UNIT, SCAN), lambda b, c: (b, 0, 0)),
        ]
    return pl.pallas_call(
        functools.partial(_mlstm_kernel, has_init=has_init, has_out=want_state, rounds=rounds,
                          by_sequence=by_sequence),
        out_shape=out_shape,
        grid=grid,
        in_specs=in_specs,
        out_specs=out_specs,
        scratch_shapes=[pltpu.VMEM((slots * N_UNIT, ST_ROWS, HD), F32),
                        pltpu.VMEM((slots * N_UNIT, SCAN), F32)],
        compiler_params=pltpu.CompilerParams(
            dimension_semantics=("arbitrary", "arbitrary"), vmem_limit_bytes=VMEM_LIMIT_BYTES),
        name="mlstm",
    )(*args)


def _outffn_mix(x, a, hft, hbt, o, mod, c, x1_ref, h2_ref):
    ga1 = mod[:, 0:D_MODEL]
    sh2 = mod[:, D_MODEL:2 * D_MODEL]
    sc2 = mod[:, 2 * D_MODEL:3 * D_MODEL]
    hs = jnp.transpose(hft + hbt)
    sig = _sigmoid(o)
    parts = []
    for hd in range(HEADS):
        sl = slice(hd * HD, (hd + 1) * HD)
        parts.append(_rms(hs[:, sl], c["gh"][:, sl]) * sig[:, sl])
    b_out = jnp.concatenate(parts, axis=-1).astype(BF16)
    mix = _dot(a, c["wout"][0:D_GMLP, :]) + _dot(b_out, c["wout"][D_GMLP:, :])
    x1 = x + ga1 * mix
    x1_ref[...] = x1
    h2_ref[...] = (_rms(x1, c["g2"][...] * (1.0 + sc2)) + sh2).astype(BF16)


def _outffn_up(h2_ref, c, f_ref):
    h2 = h2_ref[...]
    u = _dot(h2, c["w1"][...])
    g = _dot(h2, c["w3"][...])
    f_ref[...] = (u * _sigmoid(u) * g).astype(BF16)


def _outffn_down(f_ref, x1_ref, mod, c, y_ref):
    ga2 = mod[:, 3 * D_MODEL:4 * D_MODEL]
    x2 = x1_ref[...] + ga2 * _dot(f_ref[...], c["w2"][...])
    y_ref[...] = _rms(x2, c["gf"][...])


def _outffn_kernel(x_ref, a_ref, hft_ref, hbt_ref, o_ref, mod_ref, gh_ref, wout_hbm,
                   g2_ref, w1_hbm, w3_hbm, w2_hbm, gf_ref, y_ref,
                   x1_s, h2_s, f_s, wout_s, w1_s, w3_s, w2_s, w_sem,
                   *, blocks_per_mod, mod_row0):
    tb = x_ref.shape[0] // 2
    i = pl.program_id(0)
    mod_in = mod_row0 + jnp.minimum(i, pl.num_programs(0) - 2) // blocks_per_mod
    mod_out = mod_row0 + jnp.maximum(i - 1, 0) // blocks_per_mod
    c = {"gh": gh_ref, "wout": wout_s, "g2": g2_ref, "w1": w1_s, "w3": w3_s, "w2": w2_s,
         "gf": gf_ref}
    weight_copy = {
        name: pltpu.make_async_copy(src, dst, w_sem.at[j])
        for j, (name, src, dst) in enumerate((("wout", wout_hbm, wout_s), ("w1", w1_hbm, w1_s),
                                              ("w3", w3_hbm, w3_s), ("w2", w2_hbm, w2_s)))}

    def tick(half, mix, up, down):
        rows = slice(half * tb, (half + 1) * tb)
        cur, oth = half, 1 - half
        first_tick = mix and not up and not down
        second_tick = mix and up and not down
        if first_tick:
            for name in ("wout", "w1", "w3", "w2"):
                weight_copy[name].start()
            weight_copy["wout"].wait()
        if down:
            _outffn_down(f_s.at[cur], x1_s.at[cur], mod_ref[pl.ds(mod_out, 1), :], c,
                         y_ref.at[rows, :])
        if mix:
            _outffn_mix(x_ref[rows, :], a_ref[rows, :], hft_ref[:, rows], hbt_ref[:, rows],
                        o_ref[rows, :], mod_ref[pl.ds(mod_in, 1), :], c, x1_s.at[cur],
                        h2_s.at[cur])
        if second_tick:
            weight_copy["w1"].wait()
            weight_copy["w3"].wait()
        if up:
            _outffn_up(h2_s.at[oth], c, f_s.at[oth])
        if second_tick:
            weight_copy["w2"].wait()

    _pipeline_step(tick)


def _outffn(x, ak, hft, hbt, o, mod, mod_row0, rows_per_mod, p):
    n = x.shape[0]
    tb = TOKEN_BLOCK
    tb2 = 2 * tb
    assert n % tb2 == 0 and rows_per_mod % tb2 == 0
    nb = n // tb2
    d_ff = p["w1"].shape[1]
    blk_in = lambda i: jnp.minimum(i, nb - 1)
    blk_out = lambda i: jnp.maximum(i - 1, 0)
    tok = lambda w: pl.BlockSpec((tb2, w), lambda i: (blk_in(i), 0))
    tok_t = lambda h: pl.BlockSpec((h, tb2), lambda i: (0, blk_in(i)))
    weight_hbm = pl.BlockSpec(memory_space=pl.ANY)
    in_specs = [
        tok(D_MODEL), tok(D_GMLP), tok_t(D_MLSTM), tok_t(D_MLSTM), tok(D_MLSTM),
        _const_spec(mod.shape),
        _const_spec((1, D_MLSTM)),
        weight_hbm,
        _const_spec((1, D_MODEL)),
        weight_hbm,
        weight_hbm,
        weight_hbm,
        _const_spec((1, D_MODEL)),
    ]
    scratch_shapes = [
        pltpu.VMEM((2, tb, D_MODEL), F32),
        pltpu.VMEM((2, tb, D_MODEL), BF16),
        pltpu.VMEM((2, tb, d_ff), BF16),
        pltpu.VMEM(p["wout"].shape, BF16),
        pltpu.VMEM(p["w1"].shape, BF16),
        pltpu.VMEM(p["w3"].shape, BF16),
        pltpu.VMEM(p["w2"].shape, BF16),
        pltpu.SemaphoreType.DMA((4,)),
    ]
    return pl.pallas_call(
        functools.partial(_outffn_kernel, blocks_per_mod=rows_per_mod // tb2, mod_row0=mod_row0),
        out_shape=jax.ShapeDtypeStruct((n, D_MODEL), F32),
        grid=(nb + 1,),
        in_specs=in_specs,
        out_specs=pl.BlockSpec((tb2, D_MODEL), lambda i: (blk_out(i), 0)),
        scratch_shapes=scratch_shapes,
        compiler_params=pltpu.CompilerParams(
            dimension_semantics=("arbitrary",), vmem_limit_bytes=VMEM_LIMIT_BYTES),
        name="outffn",
    )(x, ak, hft, hbt, o, mod, p["gh"], p["wout"], p["g2"], p["w1"], p["w3"], p["w2"], p["gf"])


def _layer_params(l, g_norm1, b_gate, w_s, b_s, g_v, conv_w, conv_b, g_h, g_norm2, g_final):
    return {
        "g1": g_norm1[l][None, :],
        "bgt": b_gate[l][:, None],
        "cw": conv_w[l], "cb": conv_b[l][None, :],
        "ws": w_s[l].astype(BF16),
        "bs": jnp.repeat(b_s[l].T, HD, axis=1),
        "gv": g_v[l].reshape(1, D_GMLP),
        "gh": g_h[l].reshape(1, D_MLSTM),
        "g2": g_norm2[l][None, :],
        "gf": g_final[None, :],
    }


_LATE_WEIGHTS = ("wout", "w1", "w3", "w2")


def _trunk_front(x, mod_row0, rows_per_mod, seg, p, late_f32=None, ada_tail=None):
    cast = () if late_f32 is None else tuple(late_f32[name] for name in _LATE_WEIGHTS)
    ak, qvt, o, gr, *side = _inproj(x, p["mod_head"], mod_row0, rows_per_mod, seg, p, cast,
                                    ada_tail)
    p = dict(p, **dict(zip(_LATE_WEIGHTS, side[:len(cast)])))
    if ada_tail is not None:
        p["mod_tail"] = side[-1]
    return (ak, qvt, o), gr, p


def _trunk_back(x, front, gr, mod_row0, rows_per_mod, batch, rounds, p, s0=None, m0=None,
                want_state=False):
    ak, qvt, o = front
    hft, hbt, *state = _mlstm(ak, qvt, gr, batch, rounds, s0, m0, want_state)
    y = _outffn(x, ak, hft, hbt, o, p["mod_tail"], mod_row0, rows_per_mod, p)
    return y, state


def kernel(x_prompt, x_sample, state_C, state_n, state_m, c, c_ctx, w_ada, b_ada, g_norm1, w_in,
           b_gate, w_s, b_s, g_v, conv_w, conv_b, g_h, w_out, g_norm2, w1, w3, w2, g_final):
    bp, tp, d = x_prompt.shape
    bs_, ts, _ = x_sample.shape
    depth = w_in.shape[0]
    assert depth == 1, "final norm is fused into the layer's last kernel"
    xp = x_prompt.reshape(bp * tp, d)
    xs = x_sample.reshape(bs_ * ts, d)

    cs = jnp.zeros((8, d), F32).at[0].set(c_ctx).at[1:1 + bs_].set(c)
    new_c, new_n, new_m = [], [], []
    for l in range(depth):
        p = _layer_params(l, g_norm1, b_gate, w_s, b_s, g_v, conv_w, conv_b, g_h, g_norm2, g_final)
        late_f32 = {"wout": w_out[l], "w1": w1[l], "w3": w3[l], "w2": w2[l]}
        p["mod_head"], p["wit"] = _ada(cs, w_ada[l], b_ada[l][None, :], 2 * d, w_in[l].T)
        ada_tail = (cs, w_ada[l], b_ada[l][None, :], 2 * d)

        front_lat, gr_lat, p = _trunk_front(xs, 1, ts, GRID_W, p, late_f32=late_f32)
        front_ctx, gr_ctx, p = _trunk_front(xp, 0, bp * tp, tp, p, ada_tail=ada_tail)

        xp, (c_ctx_out, n_ctx_out, m_ctx_out) = _trunk_back(
            xp, front_ctx, gr_ctx, 0, bp * tp, bp, CTX_SEQS_PER_STEP, p, want_state=True)
        new_c.append(c_ctx_out.reshape(bp, N_DIR, HEADS, HD, HD))
        new_n.append(n_ctx_out.reshape(bp, N_DIR, HEADS, HD))
        new_m.append((m_ctx_out[..., 0] * (1.0 / LOG2E)).reshape(bp, N_DIR, HEADS))

        s0 = jnp.concatenate(
            [jnp.swapaxes(state_C[:, l], -1, -2), state_n[:, l][..., None, :],
             jnp.zeros((bs_, N_DIR, HEADS, ST_ROWS - HD - 1, HD), F32)],
            axis=-2).reshape(bs_, N_UNIT, ST_ROWS, HD)
        m0 = jnp.broadcast_to((state_m[:, l] * LOG2E).reshape(bs_, N_UNIT, 1),
                              (bs_, N_UNIT, SCAN))
        xs, _ = _trunk_back(xs, front_lat, gr_lat, 1, ts, bs_, LAT_CHUNKS_PER_STEP, p, s0=s0, m0=m0)

    return (xp.reshape(bp, tp, d), xs.reshape(bs_, ts, d),
            jnp.stack(new_c, axis=1), jnp.stack(new_n, axis=1), jnp.stack(new_m, axis=1))
```

```python
import functools

import jax
import jax.numpy as jnp
from jax import lax
from jax.experimental import pallas as pl
from jax.experimental.pallas import tpu as pltpu

D_MODEL = 1024
D_GMLP = 512
D_MLSTM = 512
GROUPS = 4
HEADS = 4
HD = 128
CHUNK = 128
SCAN = 256
N_DIR = 2
N_UNIT = N_DIR * HEADS
GRID_W = 64
OFF_V, OFF_Q, OFF_VV, OFF_O, OFF_G = 512, 1024, 2048, 2560, 3072
D_IN = OFF_G + 2 * N_DIR * HEADS
EPS = 1e-6
NEG = -1e30
ST_ROWS = HD + 16
N_GATE = 2 * N_UNIT
GR_ROWS = 5 * N_GATE

TOKEN_BLOCK = 256
CTX_SEQS_PER_STEP = 2
LAT_CHUNKS_PER_STEP = 2
VMEM_LIMIT_BYTES = 56 * 1024 * 1024

F32 = jnp.float32
BF16 = jnp.bfloat16


def _rms(x, g):
    n = x.shape[-1]
    return x * lax.rsqrt(jnp.sum(x * x, axis=-1, keepdims=True) + n * EPS) * (g * n ** 0.5)


LOG2E = 1.4426950408889634
GELU_K = 0.7978845608028654
GELU_C = 0.044715


def _sigmoid(x):
    return 1.0 / (1.0 + jnp.exp2(x * (-LOG2E)))


def _gelu_tanh(x):
    t = (x * x) * (-2.0 * LOG2E * GELU_K * GELU_C) + (-2.0 * LOG2E * GELU_K)
    return x * (1.0 / (1.0 + jnp.exp2(x * t)))


def _log_sigmoid(x):
    return jnp.minimum(x, 0.0) - jnp.log(1.0 + jnp.exp(-jnp.abs(x)))


def _dot(a, b):
    return jnp.dot(a, b, preferred_element_type=F32)


def _dot_nt(a, b):
    return lax.dot_general(a, b, (((1,), (1,)), ((), ())), preferred_element_type=F32)


def _dot_exact(a, b):
    return jnp.dot(a, b, preferred_element_type=F32, precision=lax.Precision.HIGHEST)


def _pipeline_step(tick):
    i = pl.program_id(0)
    last = pl.num_programs(0) - 1

    @pl.when(i == 0)
    def _():
        tick(0, True, False, False)
        tick(1, True, True, False)

    @pl.when(jnp.logical_and(i > 0, i < last))
    def _():
        tick(0, True, True, True)
        tick(1, True, True, True)

    @pl.when(i == last)
    def _():
        tick(0, False, True, True)
        tick(1, False, False, True)


def _const_spec(shape):
    zeros = (0,) * len(shape)
    return pl.BlockSpec(shape, lambda *_: zeros, pipeline_mode=pl.Buffered(1))


def _ada_kernel(c_ref, w_ref, b_ref, o_ref):
    c = c_ref[...]
    s = (c * _sigmoid(c)).astype(BF16)
    o_ref[...] = _dot(s, w_ref[...].astype(BF16)) + b_ref[...]


def _ada_and_round_kernel(c_ref, w_ref, b_ref, wf_ref, o_ref, wb_ref):
    _ada_kernel(c_ref, w_ref, b_ref, o_ref)

    @pl.when(pl.program_id(0) == 0)
    def _():
        wb_ref[...] = wf_ref[...].astype(BF16)


def _ada(cs, w_ada, b_ada, n, w_f32):
    rows, d = cs.shape
    tn = 1024
    return pl.pallas_call(
        _ada_and_round_kernel,
        out_shape=[jax.ShapeDtypeStruct((rows, n), F32), jax.ShapeDtypeStruct(w_f32.shape, BF16)],
        grid=(n // tn,),
        in_specs=[
            pl.BlockSpec((rows, d), lambda j: (0, 0)),
            pl.BlockSpec((d, tn), lambda j: (0, j)),
            pl.BlockSpec((1, tn), lambda j: (0, j)),
            _const_spec(w_f32.shape),
        ],
        out_specs=[pl.BlockSpec((rows, tn), lambda j: (0, j)),
                   pl.BlockSpec(w_f32.shape, lambda j: (0, 0))],
        compiler_params=pltpu.CompilerParams(
            dimension_semantics=("arbitrary",), vmem_limit_bytes=VMEM_LIMIT_BYTES),
        name="ada",
    )(cs, w_ada, b_ada, w_f32)


def _inproj_norm(x, mod, g1, hb_ref):
    sh1 = mod[:, 0:D_MODEL]
    sc1 = mod[:, D_MODEL:2 * D_MODEL]
    hb_ref[...] = (_rms(x, g1 * (1.0 + sc1)) + sh1).astype(BF16)


def _inproj_project(hb_ref, w, z):
    hb = hb_ref[...]
    z["u"][...] = _dot_nt(hb, w["u"][...])
    z["v"][...] = _dot_nt(hb, w["v"][...])
    z["qk"][...] = _dot_nt(hb, w["qk"][...])
    z["o"][...] = _dot_nt(hb, w["o"][...])
    z["vt"][...] = _dot_nt(w["vv"][...], hb).astype(BF16)
    z["gt"][...] = _dot_nt(w["gt"][...], hb)


def _inproj_finish(z, c, out, seg):
    tb = z["u"].shape[0]
    for g in range(GROUPS):
        gs = slice(g * HD, (g + 1) * HD)
        vg = _rms(_gelu_tanh(z["v"][:, gs]), c["gv"][:, gs]).astype(BF16)
        for ch in range(tb // CHUNK):
            cs = slice(ch * CHUNK, (ch + 1) * CHUNK)
            mixed = _dot(c["ws"][g], vg[cs]) + c["bs"][:, gs]
            out["a"][cs, gs] = (_gelu_tanh(z["u"][cs, gs]) * mixed).astype(BF16)

    zqk = z["qk"][...]
    pos = lax.broadcasted_iota(jnp.int32, (tb, 1), 0) % seg
    prev = jnp.where(pos != 0, pltpu.roll(zqk, 1, 0), 0.0)
    nxt = jnp.where(pos != seg - 1, pltpu.roll(zqk, tb - 1, 0), 0.0)
    cw = c["cw"]
    y = c["cb"][...] + prev * cw[0:1, :] + zqk * cw[1:2, :] + nxt * cw[2:3, :]
    y = y * _sigmoid(y)
    out["qt"][...] = jnp.transpose(y[:, :D_MLSTM]).astype(BF16)
    out["k"][...] = (y[:, D_MLSTM:] * (HD ** -0.5)).astype(BF16)
    out["vt"][...] = z["vt"][...]
    out["o"][...] = z["o"][...]

    assert tb == SCAN
    gt = z["gt"][...] + c["bgt"][...]
    gi = pltpu.roll(gt, HEADS, 0)
    lf = _log_sigmoid(gt)
    lane = lax.broadcasted_iota(jnp.int32, (N_GATE, HD), 1)
    tiles = []
    carry = jnp.zeros((N_GATE, HD), F32)
    for t in range(SCAN // HD):
        prefix = lf[:, t * HD:(t + 1) * HD]
        shift = 1
        while shift < HD:
            prefix = prefix + jnp.where(lane >= shift, pltpu.roll(prefix, shift, 1), 0.0)
            shift *= 2
        tiles.append(prefix + carry)
        carry = carry + jnp.broadcast_to(prefix[:, HD - 1:HD], (N_GATE, HD))
    prefix = jnp.concatenate(tiles, axis=1)
    b_last = jnp.concatenate([carry] * (SCAN // HD), axis=1)
    row_bwd = lax.broadcasted_iota(jnp.int32, (N_GATE, SCAN), 0) >= N_GATE // N_DIR
    b_row = jnp.where(row_bwd, b_last - prefix + lf, prefix)
    g_row = b_last - b_row + gi
    g_max = jnp.broadcast_to(jnp.max(g_row, axis=1, keepdims=True), (N_GATE, SCAN))
    for sec, stat in enumerate((b_row, g_row, b_last, g_max, gi - b_row)):
        out["gr"][sec * N_GATE:(sec + 1) * N_GATE, :] = stat * LOG2E


_INPROJ_Z = ("u", "v", "qk", "o", "vt", "gt")


def _inproj_kernel(*refs, seg, blocks_per_mod, mod_row0, n_cast, ada_tail):
    refs = list(refs)
    take = lambda k: [refs.pop(0) for _ in range(k)]
    (x_ref, mod_ref, g1_ref, wit_ref,
     bgt_ref, cw_ref, cb_ref, ws_ref, bs_ref, gv_ref) = take(10)
    cast_in = take(n_cast)
    ada_in = take(3) if ada_tail else None
    ak_ref, qvt_ref, o_ref, gr_ref = take(4)
    cast_out = take(n_cast)
    ada_out = take(1)[0] if ada_tail else None
    hb_s, *z_s = refs
    tb = x_ref.shape[0] // 2
    mod_in = mod_row0 + jnp.minimum(pl.program_id(0), pl.num_programs(0) - 2) // blocks_per_mod
    w = {"u": wit_ref.at[0:OFF_V, :], "v": wit_ref.at[OFF_V:OFF_Q, :],
         "qk": wit_ref.at[OFF_Q:OFF_VV, :], "vv": wit_ref.at[OFF_VV:OFF_O, :],
         "o": wit_ref.at[OFF_O:OFF_G, :], "gt": wit_ref.at[OFF_G:D_IN, :]}
    c = {"gv": gv_ref, "ws": ws_ref, "bs": bs_ref, "cw": cw_ref, "cb": cb_ref, "bgt": bgt_ref}

    def tick(half, norm, project, finish):
        rows = slice(half * tb, (half + 1) * tb)
        cur, oth = half, 1 - half
        z_cur = {name: ref.at[cur] for name, ref in zip(_INPROJ_Z, z_s)}
        z_oth = {name: ref.at[oth] for name, ref in zip(_INPROJ_Z, z_s)}
        out = {"a": ak_ref.at[rows, 0:D_GMLP], "k": ak_ref.at[rows, D_GMLP:],
               "qt": qvt_ref.at[0:D_MLSTM, rows], "vt": qvt_ref.at[D_MLSTM:, rows],
               "o": o_ref.at[rows, :], "gr": gr_ref.at[:, rows]}
        if project:
            _inproj_project(hb_s.at[oth], w, z_oth)
        if norm:
            _inproj_norm(x_ref[rows, :], mod_ref[pl.ds(mod_in, 1), :], g1_ref[...], hb_s.at[cur])
        if finish:
            _inproj_finish(z_cur, c, out, seg)
        if half == 0:
            for src, dst in zip(cast_in, cast_out):
                dst[...] = src[...].astype(BF16)
            if ada_tail:
                _ada_kernel(*ada_in, ada_out)

    _pipeline_step(tick)


def _inproj(x, mod, mod_row0, rows_per_mod, seg, p, cast=(), ada_tail=None):
    n = x.shape[0]
    tb = TOKEN_BLOCK
    tb2 = 2 * tb
    assert tb % seg == 0 and n % tb2 == 0 and rows_per_mod % tb2 == 0
    nb = n // tb2
    blk_in = lambda i: jnp.minimum(i, nb - 1)
    blk_out = lambda i: jnp.maximum(i - 1, 0)
    tok = lambda width: pl.BlockSpec((tb2, width), lambda i: (blk_out(i), 0))
    tok_t = lambda height: pl.BlockSpec((height, tb2), lambda i: (0, blk_out(i)))
    in_specs = [
        pl.BlockSpec((tb2, D_MODEL), lambda i: (blk_in(i), 0)),
        _const_spec(mod.shape),
        _const_spec((1, D_MODEL)),
        _const_spec((D_IN, D_MODEL)),
        _const_spec((N_GATE, 1)),
        _const_spec((3, 2 * D_MLSTM)),
        _const_spec((1, 2 * D_MLSTM)),
        _const_spec((GROUPS, CHUNK, CHUNK)),
        _const_spec((CHUNK, D_GMLP)),
        _const_spec((1, D_GMLP)),
    ]
    out_shape = [
        jax.ShapeDtypeStruct((n, D_GMLP + D_MLSTM), BF16),
        jax.ShapeDtypeStruct((2 * D_MLSTM, n), BF16),
        jax.ShapeDtypeStruct((n, D_MLSTM), F32),
        jax.ShapeDtypeStruct((GR_ROWS, n), F32),
    ]
    out_specs = [tok(D_GMLP + D_MLSTM), tok_t(2 * D_MLSTM), tok(D_MLSTM), tok_t(GR_ROWS)]
    for wf in cast:
        rows = wf.shape[0] // nb
        assert rows * nb == wf.shape[0] and rows % 16 == 0
        slab = pl.BlockSpec((rows, wf.shape[1]), lambda i: (blk_in(i), 0))
        in_specs.append(slab)
        out_specs.append(slab)
        out_shape.append(jax.ShapeDtypeStruct(wf.shape, BF16))
    side = list(cast)
    if ada_tail is not None:
        cs, w_ada, b_ada, col0 = ada_tail
        width = (w_ada.shape[1] - col0) // nb
        assert col0 % width == 0 and width * nb == w_ada.shape[1] - col0 and width % 128 == 0
        in_specs += [
            _const_spec(cs.shape),
            pl.BlockSpec((w_ada.shape[0], width), lambda i: (0, col0 // width + blk_in(i))),
            pl.BlockSpec((1, width), lambda i: (0, col0 // width + blk_in(i))),
        ]
        out_specs.append(pl.BlockSpec((cs.shape[0], width), lambda i: (0, blk_in(i))))
        out_shape.append(jax.ShapeDtypeStruct((cs.shape[0], w_ada.shape[1] - col0), F32))
        side += [cs, w_ada, b_ada]
    scratch_shapes = [
        pltpu.VMEM((2, tb, D_MODEL), BF16),
        pltpu.VMEM((2, tb, D_GMLP), F32),
        pltpu.VMEM((2, tb, D_GMLP), F32),
        pltpu.VMEM((2, tb, 2 * D_MLSTM), F32),
        pltpu.VMEM((2, tb, D_MLSTM), F32),
        pltpu.VMEM((2, D_MLSTM, tb), BF16),
        pltpu.VMEM((2, N_GATE, tb), F32),
    ]
    return pl.pallas_call(
        functools.partial(_inproj_kernel, seg=seg, blocks_per_mod=rows_per_mod // tb2,
                          mod_row0=mod_row0, n_cast=len(cast), ada_tail=ada_tail is not None),
        out_shape=out_shape,
        grid=(nb + 1,),
        in_specs=in_specs,
        out_specs=out_specs,
        scratch_shapes=scratch_shapes,
        compiler_params=pltpu.CompilerParams(
            dimension_semantics=("arbitrary",), vmem_limit_bytes=VMEM_LIMIT_BYTES),
        name="inproj",
    )(x, mod, p["g1"], p["wit"],
      p["bgt"], p["cw"], p["cb"], p["ws"], p["bs"], p["gv"], *side)


def _lane_broadcast_column(row):
    n = row.shape[1]
    tiles = [jnp.transpose(jnp.broadcast_to(row[:, t * HD:(t + 1) * HD], (HD, HD)))
             for t in range(n // HD)]
    col = jnp.concatenate(tiles, axis=0)
    return jnp.concatenate([col] * (n // HD), axis=1)


def _mlstm_unit(k, vt, kq, qs, d_row, b_row, g_row, bl_row, gmax_row, st, m_row, mask, carried):
    logw = jnp.where(mask, _lane_broadcast_column(d_row) + b_row, NEG)
    a = b_row + m_row
    mj = jnp.maximum(a, jnp.max(logw, axis=0, keepdims=True))
    w = jnp.exp2(logw - mj)
    s = kq * w
    num = _dot(vt, s.astype(BF16))
    den = jnp.sum(s, axis=0, keepdims=True)
    if carried:
        inter = jnp.exp2(a - mj)
        num = num + inter * qs[:HD]
        den = den + inter * qs[HD:HD + 1]
    h = num * (1.0 / jnp.maximum(jnp.abs(den), jnp.exp2(-mj)))
    m_new = jnp.maximum(bl_row + m_row, gmax_row)
    wc = jnp.exp2(g_row - m_new)
    pad_row = lax.broadcasted_iota(jnp.int32, (ST_ROWS - HD, wc.shape[1]), 0)
    v_aug = jnp.concatenate(
        [vt.astype(F32) * wc, jnp.where(pad_row == 0, wc, 0.0)], axis=0).astype(BF16)
    st_new = _dot(v_aug, k)
    if carried:
        decay = jnp.exp2(bl_row + m_row - m_new)
        st_new = decay[:, :HD] * st + st_new
    return h, st_new, m_new


def _mlstm_kernel(*refs, has_init, has_out, rounds, by_sequence):
    if has_init:
        s0_ref, m0_ref = refs[:2]
        refs = refs[2:]
    def split(k_ref, qvt_ref, gr_ref):
        return k_ref, qvt_ref.at[0:D_MLSTM, :], qvt_ref.at[D_MLSTM:, :], gr_ref

    if by_sequence:
        fwd_in = bwd_in = split(*refs[:3])
        refs = refs[3:]
    else:
        fwd_in, bwd_in = split(*refs[:3]), split(*refs[3:6])
        refs = refs[6:]
    hf_ref, hb_ref = refs[:2]
    refs = refs[2:]
    if has_out:
        co_ref, no_ref, mo_ref = refs[:3]
        refs = refs[3:]
    s_ref, m_ref = refs
    c = pl.program_id(1)
    nc = pl.num_programs(1)
    carried = not by_sequence

    if carried:
        @pl.when(c == 0)
        def _():
            if has_init:
                s_ref[...] = s0_ref[0]
                m_ref[...] = m0_ref[0]
            else:
                s_ref[...] = jnp.zeros_like(s_ref)
                m_ref[...] = jnp.zeros_like(m_ref)

    si = lax.broadcasted_iota(jnp.int32, (SCAN, SCAN), 0)
    ji = lax.broadcasted_iota(jnp.int32, (SCAN, SCAN), 1)
    dirs = ((fwd_in, hf_ref, si <= ji), (bwd_in, hb_ref, si >= ji))

    def chunk_cols(d, r):
        pos = rounds - 1 - r if (d == 1 and not by_sequence) else r
        return slice(pos * SCAN, (pos + 1) * SCAN)

    kq = {}
    for r in range(rounds):
        for d, ((k_ref, qt_ref, _, _), _, _) in enumerate(dirs):
            cs = chunk_cols(d, r)
            for hd in range(HEADS):
                hs = slice(hd * HD, (hd + 1) * HD)
                kq[r, d, hd] = _dot(k_ref[cs, hs], qt_ref[hs, cs])

    m_cur = None
    if carried:
        m_all = m_ref[...]
        m_cur = [m_all[u:u + 1] for u in range(N_UNIT)]
    zero_row = jnp.zeros((1, SCAN), F32)
    for r in range(rounds):
        slot = r * N_UNIT if by_sequence else 0
        for d, ((k_ref, qt_ref, vt_ref, gr_ref), h_ref, mask) in enumerate(dirs):
            cs = chunk_cols(d, r)
            for hd in range(HEADS):
                u = d * HEADS + hd
                row = d * 2 * HEADS + HEADS + hd
                hs = slice(hd * HD, (hd + 1) * HD)
                st = s_ref[slot + u] if carried else None
                qs = _dot(st.astype(BF16), qt_ref[hs, cs]) if carried else None
                h, st_new, m_new = _mlstm_unit(
                    k_ref[cs, hs], vt_ref[hs, cs], kq[r, d, hd], qs,
                    *(gr_ref[sec * N_GATE + row:sec * N_GATE + row + 1, cs]
                      for sec in (4, 0, 1, 2, 3)),
                    st, m_cur[u] if carried else zero_row, mask, carried)
                h_ref[hs, cs] = h
                s_ref[slot + u] = st_new
                if carried:
                    m_cur[u] = m_new
                else:
                    m_ref[slot + u:slot + u + 1, :] = m_new
    if carried:
        m_ref[...] = jnp.concatenate(m_cur, axis=0)

    if has_out:
        @pl.when(c == nc - 1)
        def _():
            for j in range(s_ref.shape[0]):
                q, u = divmod(j, N_UNIT)
                co_ref[q, u] = jnp.transpose(s_ref[j, 0:HD, :])
                no_ref[q, u:u + 1, :] = s_ref[j, HD:HD + 1, :]
                mo_ref[q, u:u + 1, :] = m_ref[j:j + 1, :]


def _mlstm(ak, qvt, gr, batch, rounds, s0=None, m0=None, want_state=False):
    n = ak.shape[0]
    nc = n // batch // SCAN
    by_sequence = nc == 1
    width = rounds * SCAN
    has_init = s0 is not None
    if by_sequence:
        assert batch % rounds == 0 and not has_init
        grid = (batch // rounds, 1)
        slots = rounds
        fwd = bwd = lambda b, c: b
    else:
        assert nc % rounds == 0 and not want_state
        steps = nc // rounds
        grid = (batch, steps)
        slots = 1
        fwd = lambda b, c: b * steps + c
        bwd = lambda b, c: b * steps + steps - 1 - c

    def specs(ix):
        return [
            pl.BlockSpec((width, D_MLSTM), lambda b, c: (ix(b, c), 1)),
            pl.BlockSpec((2 * D_MLSTM, width), lambda b, c: (0, ix(b, c))),
            pl.BlockSpec((GR_ROWS, width), lambda b, c: (0, ix(b, c))),
        ]

    in_specs = specs(fwd) if by_sequence else specs(fwd) + specs(bwd)
    args = [ak, qvt, gr] if by_sequence else [ak, qvt, gr, ak, qvt, gr]
    if has_init:
        in_specs = [
            pl.BlockSpec((1, N_UNIT, ST_ROWS, HD), lambda b, c: (b, 0, 0, 0)),
            pl.BlockSpec((1, N_UNIT, SCAN), lambda b, c: (b, 0, 0)),
        ] + in_specs
        args = [s0, m0] + args
    out_shape = [
        jax.ShapeDtypeStruct((D_MLSTM, n), F32),
        jax.ShapeDtypeStruct((D_MLSTM, n), F32),
    ]
    out_specs = [
        pl.BlockSpec((D_MLSTM, width), lambda b, c: (0, fwd(b, c))),
        pl.BlockSpec((D_MLSTM, width), lambda b, c: (0, bwd(b, c))),
    ]
    if want_state:
        out_shape += [
            jax.ShapeDtypeStruct((batch, N_UNIT, HD, HD), F32),
            jax.ShapeDtypeStruct((batch, N_UNIT, HD), F32),
            jax.ShapeDtypeStruct((batch, N_UNIT, SCAN), F32),
        ]
        out_specs += [
            pl.BlockSpec((slots, N_UNIT, HD, HD), lambda b, c: (b, 0, 0, 0)),
            pl.BlockSpec((slots, N_UNIT, HD), lambda b, c: (b, 0, 0)),
            pl.BlockSpec((slots, N_UNIT, SCAN), lambda b, c: (b, 0, 0)),
        ]
    return pl.pallas_call(
        functools.partial(_mlstm_kernel, has_init=has_init, has_out=want_state, rounds=rounds,
                          by_sequence=by_sequence),
        out_shape=out_shape,
        grid=grid,
        in_specs=in_specs,
        out_specs=out_specs,
        scratch_shapes=[pltpu.VMEM((slots * N_UNIT, ST_ROWS, HD), F32),
                        pltpu.VMEM((slots * N_UNIT, SCAN), F32)],
        compiler_params=pltpu.CompilerParams(
            dimension_semantics=("arbitrary", "arbitrary"), vmem_limit_bytes=VMEM_LIMIT_BYTES),
        name="mlstm",
    )(*args)


def _scan_single_chunk(ak_ref, qvt_ref, gr_ref, cols, kq, co_ref, no_ref, mo_ref, q):
    si = lax.broadcasted_iota(jnp.int32, (SCAN, SCAN), 0)
    ji = lax.broadcasted_iota(jnp.int32, (SCAN, SCAN), 1)
    masks = (si <= ji, si >= ji)
    zero_row = jnp.zeros((1, SCAN), F32)
    heads = []
    for hd in range(HEADS):
        h_sum = None
        for d in range(N_DIR):
            u = d * HEADS + hd
            row = d * 2 * HEADS + HEADS + hd
            h, st_new, m_new = _mlstm_unit(
                ak_ref[cols, D_GMLP + hd * HD:D_GMLP + (hd + 1) * HD],
                qvt_ref[D_MLSTM + hd * HD:D_MLSTM + (hd + 1) * HD, cols], kq[hd], None,
                *(gr_ref[sec * N_GATE + row:sec * N_GATE + row + 1, cols]
                  for sec in (4, 0, 1, 2, 3)),
                None, zero_row, masks[d], False)
            co_ref[q, u] = jnp.transpose(st_new[0:HD, :])
            no_ref[q, u:u + 1, :] = st_new[HD:HD + 1, :]
            mo_ref[q, u:u + 1, :] = m_new
            h_sum = h if h_sum is None else h_sum + h
        heads.append(h_sum)
    return jnp.concatenate(heads, axis=0)


def _outffn_mix(x, a, hst, o, mod, c, x1_ref, h2_ref):
    ga1 = mod[:, 0:D_MODEL]
    sh2 = mod[:, D_MODEL:2 * D_MODEL]
    sc2 = mod[:, 2 * D_MODEL:3 * D_MODEL]
    hs = jnp.transpose(hst)
    sig = _sigmoid(o)
    parts = []
    for hd in range(HEADS):
        sl = slice(hd * HD, (hd + 1) * HD)
        parts.append(_rms(hs[:, sl], c["gh"][:, sl]) * sig[:, sl])
    b_out = jnp.concatenate(parts, axis=-1).astype(BF16)
    mix = _dot(a, c["wout"][0:D_GMLP, :]) + _dot(b_out, c["wout"][D_GMLP:, :])
    x1 = x + ga1 * mix
    x1_ref[...] = x1
    h2_ref[...] = (_rms(x1, c["g2"][...] * (1.0 + sc2)) + sh2).astype(BF16)


def _outffn_up(h2_ref, c, f_ref):
    h2 = h2_ref[...]
    u = _dot(h2, c["w1"][...])
    g = _dot(h2, c["w3"][...])
    f_ref[...] = (u * _sigmoid(u) * g).astype(BF16)


def _outffn_down(f_ref, x1_ref, mod, c, y_ref):
    ga2 = mod[:, 3 * D_MODEL:4 * D_MODEL]
    x2 = x1_ref[...] + ga2 * _dot(f_ref[...], c["w2"][...])
    y_ref[...] = _rms(x2, c["gf"][...])


def _outffn_kernel(x_ref, a_ref, t1_ref, t2_ref, o_ref, mod_ref, gh_ref, wout_hbm,
                   g2_ref, w1_hbm, w3_hbm, w2_hbm, gf_ref, y_ref, *refs,
                   blocks_per_mod, mod_row0, inline_scan):
    if inline_scan:
        co_ref, no_ref, mo_ref = refs[:3]
        refs = refs[3:]
    x1_s, h2_s, f_s, wout_s, w1_s, w3_s, w2_s, w_sem = refs
    tb = x_ref.shape[0] // 2
    i = pl.program_id(0)
    mod_in = mod_row0 + jnp.minimum(i, pl.num_programs(0) - 2) // blocks_per_mod
    mod_out = mod_row0 + jnp.maximum(i - 1, 0) // blocks_per_mod
    c = {"gh": gh_ref, "wout": wout_s, "g2": g2_ref, "w1": w1_s, "w3": w3_s, "w2": w2_s,
         "gf": gf_ref}
    weight_copy = {
        name: pltpu.make_async_copy(src, dst, w_sem.at[j])
        for j, (name, src, dst) in enumerate((("wout", wout_hbm, wout_s), ("w1", w1_hbm, w1_s),
                                              ("w3", w3_hbm, w3_s), ("w2", w2_hbm, w2_s)))}

    def tick(half, mix, up, down):
        rows = slice(half * tb, (half + 1) * tb)
        cur, oth = half, 1 - half
        first_tick = mix and not up and not down
        second_tick = mix and up and not down
        if first_tick:
            for name in ("wout", "w1", "w3", "w2"):
                weight_copy[name].start()
            weight_copy["wout"].wait()
        if mix and inline_scan:
            kq = [_dot(a_ref[rows, D_GMLP + hd * HD:D_GMLP + (hd + 1) * HD],
                       t1_ref[hd * HD:(hd + 1) * HD, rows]) for hd in range(HEADS)]
        if down:
            _outffn_down(f_s.at[cur], x1_s.at[cur], mod_ref[pl.ds(mod_out, 1), :], c,
                         y_ref.at[rows, :])
        if mix:
            if inline_scan:
                hst = _scan_single_chunk(a_ref, t1_ref, t2_ref, rows, kq, co_ref, no_ref, mo_ref,
                                         half)
            else:
                hst = t1_ref[:, rows] + t2_ref[:, rows]
            _outffn_mix(x_ref[rows, :], a_ref[rows, 0:D_GMLP], hst, o_ref[rows, :],
                        mod_ref[pl.ds(mod_in, 1), :], c, x1_s.at[cur], h2_s.at[cur])
        if second_tick:
            weight_copy["w1"].wait()
            weight_copy["w3"].wait()
        if up:
            _outffn_up(h2_s.at[oth], c, f_s.at[oth])
        if second_tick:
            weight_copy["w2"].wait()

    _pipeline_step(tick)


def _outffn(x, ak, t1, t2, o, mod, mod_row0, rows_per_mod, p, inline_scan):
    n = x.shape[0]
    tb = TOKEN_BLOCK
    tb2 = 2 * tb
    assert n % tb2 == 0 and rows_per_mod % tb2 == 0
    nb = n // tb2
    d_ff = p["w1"].shape[1]
    blk_in = lambda i: jnp.minimum(i, nb - 1)
    blk_out = lambda i: jnp.maximum(i - 1, 0)
    tok = lambda w: pl.BlockSpec((tb2, w), lambda i: (blk_in(i), 0))
    tok_t = lambda h: pl.BlockSpec((h, tb2), lambda i: (0, blk_in(i)))
    weight_hbm = pl.BlockSpec(memory_space=pl.ANY)
    if inline_scan:
        assert tb == SCAN
        scan_specs = [tok(D_GMLP + D_MLSTM), tok_t(2 * D_MLSTM), tok_t(GR_ROWS)]
    else:
        scan_specs = [tok(D_GMLP), tok_t(D_MLSTM), tok_t(D_MLSTM)]
    in_specs = [
        tok(D_MODEL), *scan_specs, tok(D_MLSTM),
        _const_spec(mod.shape),
        _const_spec((1, D_MLSTM)),
        weight_hbm,
        _const_spec((1, D_MODEL)),
        weight_hbm,
        weight_hbm,
        weight_hbm,
        _const_spec((1, D_MODEL)),
    ]
    scratch_shapes = [
        pltpu.VMEM((2, tb, D_MODEL), F32),
        pltpu.VMEM((2, tb, D_MODEL), BF16),
        pltpu.VMEM((2, tb, d_ff), BF16),
        pltpu.VMEM(p["wout"].shape, BF16),
        pltpu.VMEM(p["w1"].shape, BF16),
        pltpu.VMEM(p["w3"].shape, BF16),
        pltpu.VMEM(p["w2"].shape, BF16),
        pltpu.SemaphoreType.DMA((4,)),
    ]
    out_shape = [jax.ShapeDtypeStruct((n, D_MODEL), F32)]
    out_specs = [pl.BlockSpec((tb2, D_MODEL), lambda i: (blk_out(i), 0))]
    if inline_scan:
        out_shape += [
            jax.ShapeDtypeStruct((n // SCAN, N_UNIT, HD, HD), F32),
            jax.ShapeDtypeStruct((n // SCAN, N_UNIT, HD), F32),
            jax.ShapeDtypeStruct((n // SCAN, N_UNIT, SCAN), F32),
        ]
        out_specs += [
            pl.BlockSpec((2, N_UNIT, HD, HD), lambda i: (blk_in(i), 0, 0, 0)),
            pl.BlockSpec((2, N_UNIT, HD), lambda i: (blk_in(i), 0, 0)),
            pl.BlockSpec((2, N_UNIT, SCAN), lambda i: (blk_in(i), 0, 0)),
        ]
    return pl.pallas_call(
        functools.partial(_outffn_kernel, blocks_per_mod=rows_per_mod // tb2, mod_row0=mod_row0,
                          inline_scan=inline_scan),
        out_shape=out_shape,
        grid=(nb + 1,),
        in_specs=in_specs,
        out_specs=out_specs,
        scratch_shapes=scratch_shapes,
        compiler_params=pltpu.CompilerParams(
            dimension_semantics=("arbitrary",), vmem_limit_bytes=VMEM_LIMIT_BYTES),
        name="outffn",
    )(x, ak, t1, t2, o, mod, p["gh"], p["wout"], p["g2"], p["w1"], p["w3"], p["w2"], p["gf"])


def _layer_params(l, g_norm1, b_gate, w_s, b_s, g_v, conv_w, conv_b, g_h, g_norm2, g_final):
    return {
        "g1": g_norm1[l][None, :],
        "bgt": b_gate[l][:, None],
        "cw": conv_w[l], "cb": conv_b[l][None, :],
        "ws": w_s[l].astype(BF16),
        "bs": jnp.repeat(b_s[l].T, HD, axis=1),
        "gv": g_v[l].reshape(1, D_GMLP),
        "gh": g_h[l].reshape(1, D_MLSTM),
        "g2": g_norm2[l][None, :],
        "gf": g_final[None, :],
    }


_LATE_WEIGHTS = ("wout", "w1", "w3", "w2")


def _trunk_front(x, mod_row0, rows_per_mod, seg, p, late=None):
    cast, ada_tail = ((), None) if late is None else (
        tuple(late[0][name] for name in _LATE_WEIGHTS), late[1])
    ak, qvt, o, gr, *side = _inproj(x, p["mod_head"], mod_row0, rows_per_mod, seg, p, cast,
                                    ada_tail)
    if late is not None:
        p = dict(p, **dict(zip(_LATE_WEIGHTS, side[:-1])))
        p["mod_tail"] = side[-1]
    return (ak, qvt, o), gr, p


def _trunk_back(x, front, gr, mod_row0, rows_per_mod, batch, rounds, p, s0=None, m0=None,
                want_state=False):
    ak, qvt, o = front
    if x.shape[0] // batch == SCAN:
        assert want_state and s0 is None
        y, *state = _outffn(x, ak, qvt, gr, o, p["mod_tail"], mod_row0, rows_per_mod, p, True)
    else:
        assert not want_state
        hft, hbt = _mlstm(ak, qvt, gr, batch, rounds, s0, m0)
        y, = _outffn(x, ak, hft, hbt, o, p["mod_tail"], mod_row0, rows_per_mod, p, False)
        state = []
    return y, state


def kernel(x_prompt, x_sample, state_C, state_n, state_m, c, c_ctx, w_ada, b_ada, g_norm1, w_in,
           b_gate, w_s, b_s, g_v, conv_w, conv_b, g_h, w_out, g_norm2, w1, w3, w2, g_final):
    bp, tp, d = x_prompt.shape
    bs_, ts, _ = x_sample.shape
    depth = w_in.shape[0]
    assert depth == 1, "final norm is fused into the layer's last kernel"
    xp = x_prompt.reshape(bp * tp, d)
    xs = x_sample.reshape(bs_ * ts, d)

    cs = jnp.zeros((8, d), F32).at[0].set(c_ctx).at[1:1 + bs_].set(c)
    new_c, new_n, new_m = [], [], []
    for l in range(depth):
        p = _layer_params(l, g_norm1, b_gate, w_s, b_s, g_v, conv_w, conv_b, g_h, g_norm2, g_final)
        late_f32 = {"wout": w_out[l], "w1": w1[l], "w3": w3[l], "w2": w2[l]}
        p["mod_head"], p["wit"] = _ada(cs, w_ada[l], b_ada[l][None, :], 2 * d, w_in[l].T)
        ada_tail = (cs, w_ada[l], b_ada[l][None, :], 2 * d)

        front_lat, gr_lat, p = _trunk_front(xs, 1, ts, GRID_W, p, (late_f32, ada_tail))
        front_ctx, gr_ctx, _ = _trunk_front(xp, 0, bp * tp, tp, p)

        xp, (c_ctx_out, n_ctx_out, m_ctx_out) = _trunk_back(
            xp, front_ctx, gr_ctx, 0, bp * tp, bp, CTX_SEQS_PER_STEP, p, want_state=True)
        new_c.append(c_ctx_out.reshape(bp, N_DIR, HEADS, HD, HD))
        new_n.append(n_ctx_out.reshape(bp, N_DIR, HEADS, HD))
        new_m.append((m_ctx_out[..., 0] * (1.0 / LOG2E)).reshape(bp, N_DIR, HEADS))

        s0 = jnp.concatenate(
            [jnp.swapaxes(state_C[:, l], -1, -2), state_n[:, l][..., None, :],
             jnp.zeros((bs_, N_DIR, HEADS, ST_ROWS - HD - 1, HD), F32)],
            axis=-2).reshape(bs_, N_UNIT, ST_ROWS, HD)
        m0 = jnp.broadcast_to((state_m[:, l] * LOG2E).reshape(bs_, N_UNIT, 1),
                              (bs_, N_UNIT, SCAN))
        xs, _ = _trunk_back(xs, front_lat, gr_lat, 1, ts, bs_, LAT_CHUNKS_PER_STEP, p, s0=s0, m0=m0)

    return (xp.reshape(bp, tp, d), xs.reshape(bs_, ts, d),
            jnp.stack(new_c, axis=1), jnp.stack(new_n, axis=1), jnp.stack(new_m, axis=1))
```

```python
import functools

import jax
import jax.numpy as jnp
from jax import lax
from jax.experimental import pallas as pl
from jax.experimental.pallas import tpu as pltpu

D_MODEL = 1024
D_GMLP = 512
D_MLSTM = 512
GROUPS = 4
HEADS = 4
HD = 128
CHUNK = 128
SCAN = 256
N_DIR = 2
N_UNIT = N_DIR * HEADS
GRID_W = 64
OFF_V, OFF_Q, OFF_VV, OFF_O, OFF_G = 512, 1024, 2048, 2560, 3072
D_IN = OFF_G + 2 * N_DIR * HEADS
EPS = 1e-6
NEG = -1e30
ST_ROWS = HD + 16
N_GATE = 2 * N_UNIT
GR_ROWS = 6 * N_GATE

TOKEN_BLOCK = 256
CTX_SEQS_PER_STEP = 2
LAT_CHUNKS_PER_STEP = 2
VMEM_LIMIT_BYTES = 56 * 1024 * 1024

F32 = jnp.float32
BF16 = jnp.bfloat16


def _rms(x, g):
    n = x.shape[-1]
    return x * lax.rsqrt(jnp.sum(x * x, axis=-1, keepdims=True) + n * EPS) * (g * n ** 0.5)


LOG2E = 1.4426950408889634
GELU_K = 0.7978845608028654
GELU_C = 0.044715


def _sigmoid(x):
    return 1.0 / (1.0 + jnp.exp2(x * (-LOG2E)))


def _gelu_tanh(x):
    t = (x * x) * (-2.0 * LOG2E * GELU_K * GELU_C) + (-2.0 * LOG2E * GELU_K)
    return x * (1.0 / (1.0 + jnp.exp2(x * t)))


def _log_sigmoid(x):
    return jnp.minimum(x, 0.0) - jnp.log(1.0 + jnp.exp(-jnp.abs(x)))


def _dot(a, b):
    return jnp.dot(a, b, preferred_element_type=F32)


def _dot_nt(a, b):
    return lax.dot_general(a, b, (((1,), (1,)), ((), ())), preferred_element_type=F32)


def _dot_exact(a, b):
    return jnp.dot(a, b, preferred_element_type=F32, precision=lax.Precision.HIGHEST)


def _pipeline_step(tick):
    i = pl.program_id(0)
    last = pl.num_programs(0) - 1

    @pl.when(i == 0)
    def _():
        tick(0, True, False, False)
        tick(1, True, True, False)

    @pl.when(jnp.logical_and(i > 0, i < last))
    def _():
        tick(0, True, True, True)
        tick(1, True, True, True)

    @pl.when(i == last)
    def _():
        tick(0, False, True, True)
        tick(1, False, False, True)


def _const_spec(shape):
    zeros = (0,) * len(shape)
    return pl.BlockSpec(shape, lambda *_: zeros, pipeline_mode=pl.Buffered(1))


def _ada_kernel(c_ref, w_ref, b_ref, o_ref):
    c = c_ref[...]
    s = (c * _sigmoid(c)).astype(BF16)
    o_ref[...] = _dot(s, w_ref[...].astype(BF16)) + b_ref[...]


def _ada_and_round_kernel(c_ref, w_ref, b_ref, wf_ref, o_ref, wb_ref):
    _ada_kernel(c_ref, w_ref, b_ref, o_ref)

    @pl.when(pl.program_id(0) == 0)
    def _():
        wb_ref[...] = wf_ref[...].astype(BF16)


def _ada(cs, w_ada, b_ada, n, w_f32):
    rows, d = cs.shape
    tn = 1024
    return pl.pallas_call(
        _ada_and_round_kernel,
        out_shape=[jax.ShapeDtypeStruct((rows, n), F32), jax.ShapeDtypeStruct(w_f32.shape, BF16)],
        grid=(n // tn,),
        in_specs=[
            pl.BlockSpec((rows, d), lambda j: (0, 0)),
            pl.BlockSpec((d, tn), lambda j: (0, j)),
            pl.BlockSpec((1, tn), lambda j: (0, j)),
            _const_spec(w_f32.shape),
        ],
        out_specs=[pl.BlockSpec((rows, tn), lambda j: (0, j)),
                   pl.BlockSpec(w_f32.shape, lambda j: (0, 0))],
        compiler_params=pltpu.CompilerParams(
            dimension_semantics=("arbitrary",), vmem_limit_bytes=VMEM_LIMIT_BYTES),
        name="ada",
    )(cs, w_ada, b_ada, w_f32)


def _inproj_norm(x, mod, g1, hb_ref):
    sh1 = mod[:, 0:D_MODEL]
    sc1 = mod[:, D_MODEL:2 * D_MODEL]
    hb_ref[...] = (_rms(x, g1 * (1.0 + sc1)) + sh1).astype(BF16)


def _inproj_project(hb_ref, w, z):
    hb = hb_ref[...]
    z["u"][...] = _dot_nt(hb, w["u"][...])
    z["v"][...] = _dot_nt(hb, w["v"][...])
    z["qk"][...] = _dot_nt(hb, w["qk"][...])
    z["o"][...] = _dot_nt(hb, w["o"][...])
    z["vt"][...] = _dot_nt(w["vv"][...], hb).astype(BF16)
    z["gt"][...] = _dot_nt(w["gt"][...], hb)


def _inproj_finish(z, c, out, seg):
    tb = z["u"].shape[0]
    for g in range(GROUPS):
        gs = slice(g * HD, (g + 1) * HD)
        vg = _rms(_gelu_tanh(z["v"][:, gs]), c["gv"][:, gs]).astype(BF16)
        for ch in range(tb // CHUNK):
            cs = slice(ch * CHUNK, (ch + 1) * CHUNK)
            mixed = _dot(c["ws"][g], vg[cs]) + c["bs"][:, gs]
            out["a"][cs, gs] = (_gelu_tanh(z["u"][cs, gs]) * mixed).astype(BF16)

    zqk = z["qk"][...]
    pos = lax.broadcasted_iota(jnp.int32, (tb, 1), 0) % seg
    prev = jnp.where(pos != 0, pltpu.roll(zqk, 1, 0), 0.0)
    nxt = jnp.where(pos != seg - 1, pltpu.roll(zqk, tb - 1, 0), 0.0)
    cw = c["cw"]
    y = c["cb"][...] + prev * cw[0:1, :] + zqk * cw[1:2, :] + nxt * cw[2:3, :]
    y = y * _sigmoid(y)
    out["qt"][...] = jnp.transpose(y[:, :D_MLSTM]).astype(BF16)
    out["k"][...] = (y[:, D_MLSTM:] * (HD ** -0.5)).astype(BF16)
    out["vt"][...] = z["vt"][...]
    out["o"][...] = z["o"][...]

    assert tb == SCAN
    gt = z["gt"][...] + c["bgt"][...]
    gi = pltpu.roll(gt, HEADS, 0)
    lf = _log_sigmoid(gt)
    lane = lax.broadcasted_iota(jnp.int32, (N_GATE, HD), 1)
    tiles = []
    carry = jnp.zeros((N_GATE, HD), F32)
    for t in range(SCAN // HD):
        prefix = lf[:, t * HD:(t + 1) * HD]
        shift = 1
        while shift < HD:
            prefix = prefix + jnp.where(lane >= shift, pltpu.roll(prefix, shift, 1), 0.0)
            shift *= 2
        tiles.append(prefix + carry)
        carry = carry + jnp.broadcast_to(prefix[:, HD - 1:HD], (N_GATE, HD))
    prefix = jnp.concatenate(tiles, axis=1)
    b_last = jnp.concatenate([carry] * (SCAN // HD), axis=1)
    row_bwd = lax.broadcasted_iota(jnp.int32, (N_GATE, SCAN), 0) >= N_GATE // N_DIR
    b_row = jnp.where(row_bwd, b_last - prefix + lf, prefix)
    g_row = b_last - b_row + gi
    g_max = jnp.broadcast_to(jnp.max(g_row, axis=1, keepdims=True), (N_GATE, SCAN))
    d_row = (gi - b_row) * LOG2E
    lane = lax.broadcasted_iota(jnp.int32, (N_GATE, SCAN), 1)
    d_upto, d_from = d_row, d_row
    shift = 1
    while shift < SCAN:
        d_upto = jnp.maximum(d_upto, jnp.where(lane >= shift, pltpu.roll(d_upto, shift, 1), NEG))
        d_from = jnp.maximum(
            d_from, jnp.where(lane < SCAN - shift, pltpu.roll(d_from, SCAN - shift, 1), NEG))
        shift *= 2
    d_max = jnp.where(row_bwd, d_from, d_upto)
    for sec, stat in enumerate((b_row, g_row, b_last, g_max)):
        out["gr"][sec * N_GATE:(sec + 1) * N_GATE, :] = stat * LOG2E
    out["gr"][4 * N_GATE:5 * N_GATE, :] = d_row
    out["gr"][5 * N_GATE:6 * N_GATE, :] = d_max


_INPROJ_Z = ("u", "v", "qk", "o", "vt", "gt")


def _inproj_kernel(*refs, seg, blocks_per_mod, mod_row0, n_cast, ada_tail):
    refs = list(refs)
    take = lambda k: [refs.pop(0) for _ in range(k)]
    (x_ref, mod_ref, g1_ref, wit_ref,
     bgt_ref, cw_ref, cb_ref, ws_ref, bs_ref, gv_ref) = take(10)
    cast_in = take(n_cast)
    ada_in = take(3) if ada_tail else None
    ak_ref, qvt_ref, o_ref, gr_ref = take(4)
    cast_out = take(n_cast)
    ada_out = take(1)[0] if ada_tail else None
    hb_s, *z_s = refs
    tb = x_ref.shape[0] // 2
    mod_in = mod_row0 + jnp.minimum(pl.program_id(0), pl.num_programs(0) - 2) // blocks_per_mod
    w = {"u": wit_ref.at[0:OFF_V, :], "v": wit_ref.at[OFF_V:OFF_Q, :],
         "qk": wit_ref.at[OFF_Q:OFF_VV, :], "vv": wit_ref.at[OFF_VV:OFF_O, :],
         "o": wit_ref.at[OFF_O:OFF_G, :], "gt": wit_ref.at[OFF_G:D_IN, :]}
    c = {"gv": gv_ref, "ws": ws_ref, "bs": bs_ref, "cw": cw_ref, "cb": cb_ref, "bgt": bgt_ref}

    def tick(half, norm, project, finish):
        rows = slice(half * tb, (half + 1) * tb)
        cur, oth = half, 1 - half
        z_cur = {name: ref.at[cur] for name, ref in zip(_INPROJ_Z, z_s)}
        z_oth = {name: ref.at[oth] for name, ref in zip(_INPROJ_Z, z_s)}
        out = {"a": ak_ref.at[rows, 0:D_GMLP], "k": ak_ref.at[rows, D_GMLP:],
               "qt": qvt_ref.at[0:D_MLSTM, rows], "vt": qvt_ref.at[D_MLSTM:, rows],
               "o": o_ref.at[rows, :], "gr": gr_ref.at[:, rows]}
        if project:
            _inproj_project(hb_s.at[oth], w, z_oth)
        if norm:
            _inproj_norm(x_ref[rows, :], mod_ref[pl.ds(mod_in, 1), :], g1_ref[...], hb_s.at[cur])
        if finish:
            _inproj_finish(z_cur, c, out, seg)
        if half == 0:
            for src, dst in zip(cast_in, cast_out):
                dst[...] = src[...].astype(BF16)
            if ada_tail:
                _ada_kernel(*ada_in, ada_out)

    _pipeline_step(tick)


def _inproj(x, mod, mod_row0, rows_per_mod, seg, p, cast=(), ada_tail=None):
    n = x.shape[0]
    tb = TOKEN_BLOCK
    tb2 = 2 * tb
    assert tb % seg == 0 and n % tb2 == 0 and rows_per_mod % tb2 == 0
    nb = n // tb2
    blk_in = lambda i: jnp.minimum(i, nb - 1)
    blk_out = lambda i: jnp.maximum(i - 1, 0)
    tok = lambda width: pl.BlockSpec((tb2, width), lambda i: (blk_out(i), 0))
    tok_t = lambda height: pl.BlockSpec((height, tb2), lambda i: (0, blk_out(i)))
    in_specs = [
        pl.BlockSpec((tb2, D_MODEL), lambda i: (blk_in(i), 0)),
        _const_spec(mod.shape),
        _const_spec((1, D_MODEL)),
        _const_spec((D_IN, D_MODEL)),
        _const_spec((N_GATE, 1)),
        _const_spec((3, 2 * D_MLSTM)),
        _const_spec((1, 2 * D_MLSTM)),
        _const_spec((GROUPS, CHUNK, CHUNK)),
        _const_spec((CHUNK, D_GMLP)),
        _const_spec((1, D_GMLP)),
    ]
    out_shape = [
        jax.ShapeDtypeStruct((n, D_GMLP + D_MLSTM), BF16),
        jax.ShapeDtypeStruct((2 * D_MLSTM, n), BF16),
        jax.ShapeDtypeStruct((n, D_MLSTM), F32),
        jax.ShapeDtypeStruct((GR_ROWS, n), F32),
    ]
    out_specs = [tok(D_GMLP + D_MLSTM), tok_t(2 * D_MLSTM), tok(D_MLSTM), tok_t(GR_ROWS)]
    for wf in cast:
        rows = wf.shape[0] // nb
        assert rows * nb == wf.shape[0] and rows % 16 == 0
        slab = pl.BlockSpec((rows, wf.shape[1]), lambda i: (blk_in(i), 0))
        in_specs.append(slab)
        out_specs.append(slab)
        out_shape.append(jax.ShapeDtypeStruct(wf.shape, BF16))
    side = list(cast)
    if ada_tail is not None:
        cs, w_ada, b_ada, col0 = ada_tail
        width = (w_ada.shape[1] - col0) // nb
        assert col0 % width == 0 and width * nb == w_ada.shape[1] - col0 and width % 128 == 0
        in_specs += [
            _const_spec(cs.shape),
            pl.BlockSpec((w_ada.shape[0], width), lambda i: (0, col0 // width + blk_in(i))),
            pl.BlockSpec((1, width), lambda i: (0, col0 // width + blk_in(i))),
        ]
        out_specs.append(pl.BlockSpec((cs.shape[0], width), lambda i: (0, blk_in(i))))
        out_shape.append(jax.ShapeDtypeStruct((cs.shape[0], w_ada.shape[1] - col0), F32))
        side += [cs, w_ada, b_ada]
    scratch_shapes = [
        pltpu.VMEM((2, tb, D_MODEL), BF16),
        pltpu.VMEM((2, tb, D_GMLP), F32),
        pltpu.VMEM((2, tb, D_GMLP), F32),
        pltpu.VMEM((2, tb, 2 * D_MLSTM), F32),
        pltpu.VMEM((2, tb, D_MLSTM), F32),
        pltpu.VMEM((2, D_MLSTM, tb), BF16),
        pltpu.VMEM((2, N_GATE, tb), F32),
    ]
    return pl.pallas_call(
        functools.partial(_inproj_kernel, seg=seg, blocks_per_mod=rows_per_mod // tb2,
                          mod_row0=mod_row0, n_cast=len(cast), ada_tail=ada_tail is not None),
        out_shape=out_shape,
        grid=(nb + 1,),
        in_specs=in_specs,
        out_specs=out_specs,
        scratch_shapes=scratch_shapes,
        compiler_params=pltpu.CompilerParams(
            dimension_semantics=("arbitrary",), vmem_limit_bytes=VMEM_LIMIT_BYTES),
        name="inproj",
    )(x, mod, p["g1"], p["wit"],
      p["bgt"], p["cw"], p["cb"], p["ws"], p["bs"], p["gv"], *side)


def _lane_broadcast_column(row):
    n = row.shape[1]
    tiles = [jnp.transpose(jnp.broadcast_to(row[:, t * HD:(t + 1) * HD], (HD, HD)))
             for t in range(n // HD)]
    col = jnp.concatenate(tiles, axis=0)
    return jnp.concatenate([col] * (n // HD), axis=1)


def _mlstm_unit(k, vt, kq, qs, d_row, dmax_row, b_row, g_row, bl_row, gmax_row, st, m_row, mask,
                carried):
    logw = jnp.where(mask, _lane_broadcast_column(d_row) + b_row, NEG)
    a = b_row + m_row
    mj = jnp.maximum(a, dmax_row + b_row)
    w = jnp.exp2(logw - mj)
    s = kq * w
    num = _dot(vt, s.astype(BF16))
    den = jnp.sum(s, axis=0, keepdims=True)
    if carried:
        inter = jnp.exp2(a - mj)
        num = num + inter * qs[:HD]
        den = den + inter * qs[HD:HD + 1]
    h = num * (1.0 / jnp.maximum(jnp.abs(den), jnp.exp2(-mj)))
    m_new = jnp.maximum(bl_row + m_row, gmax_row)
    wc = jnp.exp2(g_row - m_new)
    pad_row = lax.broadcasted_iota(jnp.int32, (ST_ROWS - HD, wc.shape[1]), 0)
    v_aug = jnp.concatenate(
        [vt.astype(F32) * wc, jnp.where(pad_row == 0, wc, 0.0)], axis=0).astype(BF16)
    st_new = _dot(v_aug, k)
    if carried:
        decay = jnp.exp2(bl_row + m_row - m_new)
        st_new = decay[:, :HD] * st + st_new
    return h, st_new, m_new


def _mlstm_kernel(*refs, has_init, has_out, rounds, by_sequence):
    if has_init:
        s0_ref, m0_ref = refs[:2]
        refs = refs[2:]
    def split(k_ref, qvt_ref, gr_ref):
        return k_ref, qvt_ref.at[0:D_MLSTM, :], qvt_ref.at[D_MLSTM:, :], gr_ref

    if by_sequence:
        fwd_in = bwd_in = split(*refs[:3])
        refs = refs[3:]
    else:
        fwd_in, bwd_in = split(*refs[:3]), split(*refs[3:6])
        refs = refs[6:]
    hf_ref, hb_ref = refs[:2]
    refs = refs[2:]
    if has_out:
        co_ref, no_ref, mo_ref = refs[:3]
        refs = refs[3:]
    s_ref, m_ref = refs
    c = pl.program_id(1)
    nc = pl.num_programs(1)
    carried = not by_sequence

    if carried:
        @pl.when(c == 0)
        def _():
            if has_init:
                s_ref[...] = s0_ref[0]
                m_ref[...] = m0_ref[0]
            else:
                s_ref[...] = jnp.zeros_like(s_ref)
                m_ref[...] = jnp.zeros_like(m_ref)

    si = lax.broadcasted_iota(jnp.int32, (SCAN, SCAN), 0)
    ji = lax.broadcasted_iota(jnp.int32, (SCAN, SCAN), 1)
    dirs = ((fwd_in, hf_ref, si <= ji), (bwd_in, hb_ref, si >= ji))

    def chunk_cols(d, r):
        pos = rounds - 1 - r if (d == 1 and not by_sequence) else r
        return slice(pos * SCAN, (pos + 1) * SCAN)

    kq = {}
    for r in range(rounds):
        for d, ((k_ref, qt_ref, _, _), _, _) in enumerate(dirs):
            cs = chunk_cols(d, r)
            for hd in range(HEADS):
                hs = slice(hd * HD, (hd + 1) * HD)
                kq[r, d, hd] = _dot(k_ref[cs, hs], qt_ref[hs, cs])

    m_cur = None
    if carried:
        m_all = m_ref[...]
        m_cur = [m_all[u:u + 1] for u in range(N_UNIT)]
    zero_row = jnp.zeros((1, SCAN), F32)
    for r in range(rounds):
        slot = r * N_UNIT if by_sequence else 0
        for d, ((k_ref, qt_ref, vt_ref, gr_ref), h_ref, mask) in enumerate(dirs):
            cs = chunk_cols(d, r)
            for hd in range(HEADS):
                u = d * HEADS + hd
                row = d * 2 * HEADS + HEADS + hd
                hs = slice(hd * HD, (hd + 1) * HD)
                st = s_ref[slot + u] if carried else None
                qs = _dot(st.astype(BF16), qt_ref[hs, cs]) if carried else None
                h, st_new, m_new = _mlstm_unit(
                    k_ref[cs, hs], vt_ref[hs, cs], kq[r, d, hd], qs,
                    *(gr_ref[sec * N_GATE + row:sec * N_GATE + row + 1, cs]
                      for sec in (4, 5, 0, 1, 2, 3)),
                    st, m_cur[u] if carried else zero_row, mask, carried)
                h_ref[hs, cs] = h
                s_ref[slot + u] = st_new
                if carried:
                    m_cur[u] = m_new
                else:
                    m_ref[slot + u:slot + u + 1, :] = m_new
    if carried:
        m_ref[...] = jnp.concatenate(m_cur, axis=0)

    if has_out:
        @pl.when(c == nc - 1)
        def _():
            for j in range(s_ref.shape[0]):
                q, u = divmod(j, N_UNIT)
                co_ref[q, u] = jnp.transpose(s_ref[j, 0:HD, :])
                no_ref[q, u:u + 1, :] = s_ref[j, HD:HD + 1, :]
                mo_ref[q, u:u + 1, :] = m_ref[j:j + 1, :]


def _mlstm(ak, qvt, gr, batch, rounds, s0=None, m0=None, want_state=False):
    n = ak.shape[0]
    nc = n // batch // SCAN
    by_sequence = nc == 1
    width = rounds * SCAN
    has_init = s0 is not None
    if by_sequence:
        assert batch % rounds == 0 and not has_init
        grid = (batch // rounds, 1)
        slots = rounds
        fwd = bwd = lambda b, c: b
    else:
        assert nc % rounds == 0 and not want_state
        steps = nc // rounds
        grid = (batch, steps)
        slots = 1
        fwd = lambda b, c: b * steps + c
        bwd = lambda b, c: b * steps + steps - 1 - c

    def specs(ix):
        return [
            pl.BlockSpec((width, D_MLSTM), lambda b, c: (ix(b, c), 1)),
            pl.BlockSpec((2 * D_MLSTM, width), lambda b, c: (0, ix(b, c))),
            pl.BlockSpec((GR_ROWS, width), lambda b, c: (0, ix(b, c))),
        ]

    in_specs = specs(fwd) if by_sequence else specs(fwd) + specs(bwd)
    args = [ak, qvt, gr] if by_sequence else [ak, qvt, gr, ak, qvt, gr]
    if has_init:
        in_specs = [
            pl.BlockSpec((1, N_UNIT, ST_ROWS, HD), lambda b, c: (b, 0, 0, 0)),
            pl.BlockSpec((1, N_UNIT, SCAN), lambda b, c: (b, 0, 0)),
        ] + in_specs
        args = [s0, m0] + args
    out_shape = [
        jax.ShapeDtypeStruct((D_MLSTM, n), F32),
        jax.ShapeDtypeStruct((D_MLSTM, n), F32),
    ]
    out_specs = [
        pl.BlockSpec((D_MLSTM, width), lambda b, c: (0, fwd(b, c))),
        pl.BlockSpec((D_MLSTM, width), lambda b, c: (0, bwd(b, c))),
    ]
    if want_state:
        out_shape += [
            jax.ShapeDtypeStruct((batch, N_UNIT, HD, HD), F32),
            jax.ShapeDtypeStruct((batch, N_UNIT, HD), F32),
            jax.ShapeDtypeStruct((batch, N_UNIT, SCAN), F32),
        ]
        out_specs += [
            pl.BlockSpec((slots, N_UNIT, HD, HD), lambda b, c: (b, 0, 0, 0)),
            pl.BlockSpec((slots, N_UNIT, HD), lambda b, c: (b, 0, 0)),
            pl.BlockSpec((slots, N_UNIT, SCAN), lambda b, c: (b, 0, 0)),
        ]
    return pl.pallas_call(
        functools.partial(_mlstm_kernel, has_init=has_init, has_out=want_state, rounds=rounds,
                          by_sequence=by_sequence),
        out_shape=out_shape,
        grid=grid,
        in_specs=in_specs,
        out_specs=out_specs,
        scratch_shapes=[pltpu.VMEM((slots * N_UNIT, ST_ROWS, HD), F32),
                        pltpu.VMEM((slots * N_UNIT, SCAN), F32)],
        compiler_params=pltpu.CompilerParams(
            dimension_semantics=("arbitrary", "arbitrary"), vmem_limit_bytes=VMEM_LIMIT_BYTES),
        name="mlstm",
    )(*args)


def _outffn_mix(x, a, hft, hbt, o, mod, c, x1_ref, h2_ref):
    ga1 = mod[:, 0:D_MODEL]
    sh2 = mod[:, D_MODEL:2 * D_MODEL]
    sc2 = mod[:, 2 * D_MODEL:3 * D_MODEL]
    hs = jnp.transpose(hft + hbt)
    sig = _sigmoid(o)
    parts = []
    for hd in range(HEADS):
        sl = slice(hd * HD, (hd + 1) * HD)
        parts.append(_rms(hs[:, sl], c["gh"][:, sl]) * sig[:, sl])
    b_out = jnp.concatenate(parts, axis=-1).astype(BF16)
    mix = _dot(a, c["wout"][0:D_GMLP, :]) + _dot(b_out, c["wout"][D_GMLP:, :])
    x1 = x + ga1 * mix
    x1_ref[...] = x1
    h2_ref[...] = (_rms(x1, c["g2"][...] * (1.0 + sc2)) + sh2).astype(BF16)


def _outffn_up(h2_ref, c, f_ref):
    h2 = h2_ref[...]
    u = _dot(h2, c["w1"][...])
    g = _dot(h2, c["w3"][...])
    f_ref[...] = (u * _sigmoid(u) * g).astype(BF16)


def _outffn_down(f_ref, x1_ref, mod, c, y_ref):
    ga2 = mod[:, 3 * D_MODEL:4 * D_MODEL]
    x2 = x1_ref[...] + ga2 * _dot(f_ref[...], c["w2"][...])
    y_ref[...] = _rms(x2, c["gf"][...])


def _outffn_kernel(x_ref, a_ref, hft_ref, hbt_ref, o_ref, mod_ref, gh_ref, wout_hbm,
                   g2_ref, w1_hbm, w3_hbm, w2_hbm, gf_ref, y_ref,
                   x1_s, h2_s, f_s, wout_s, w1_s, w3_s, w2_s, w_sem,
                   *, blocks_per_mod, mod_row0):
    tb = x_ref.shape[0] // 2
    i = pl.program_id(0)
    mod_in = mod_row0 + jnp.minimum(i, pl.num_programs(0) - 2) // blocks_per_mod
    mod_out = mod_row0 + jnp.maximum(i - 1, 0) // blocks_per_mod
    c = {"gh": gh_ref, "wout": wout_s, "g2": g2_ref, "w1": w1_s, "w3": w3_s, "w2": w2_s,
         "gf": gf_ref}
    weight_copy = {
        name: pltpu.make_async_copy(src, dst, w_sem.at[j])
        for j, (name, src, dst) in enumerate((("wout", wout_hbm, wout_s), ("w1", w1_hbm, w1_s),
                                              ("w3", w3_hbm, w3_s), ("w2", w2_hbm, w2_s)))}

    def tick(half, mix, up, down):
        rows = slice(half * tb, (half + 1) * tb)
        cur, oth = half, 1 - half
        first_tick = mix and not up and not down
        second_tick = mix and up and not down
        if first_tick:
            for name in ("wout", "w1", "w3", "w2"):
                weight_copy[name].start()
            weight_copy["wout"].wait()
        if down:
            _outffn_down(f_s.at[cur], x1_s.at[cur], mod_ref[pl.ds(mod_out, 1), :], c,
                         y_ref.at[rows, :])
        if mix:
            _outffn_mix(x_ref[rows, :], a_ref[rows, :], hft_ref[:, rows], hbt_ref[:, rows],
                        o_ref[rows, :], mod_ref[pl.ds(mod_in, 1), :], c, x1_s.at[cur],
                        h2_s.at[cur])
        if second_tick:
            weight_copy["w1"].wait()
            weight_copy["w3"].wait()
        if up:
            _outffn_up(h2_s.at[oth], c, f_s.at[oth])
        if second_tick:
            weight_copy["w2"].wait()

    _pipeline_step(tick)


def _outffn(x, ak, hft, hbt, o, mod, mod_row0, rows_per_mod, p):
    n = x.shape[0]
    tb = TOKEN_BLOCK
    tb2 = 2 * tb
    assert n % tb2 == 0 and rows_per_mod % tb2 == 0
    nb = n // tb2
    d_ff = p["w1"].shape[1]
    blk_in = lambda i: jnp.minimum(i, nb - 1)
    blk_out = lambda i: jnp.maximum(i - 1, 0)
    tok = lambda w: pl.BlockSpec((tb2, w), lambda i: (blk_in(i), 0))
    tok_t = lambda h: pl.BlockSpec((h, tb2), lambda i: (0, blk_in(i)))
    weight_hbm = pl.BlockSpec(memory_space=pl.ANY)
    in_specs = [
        tok(D_MODEL), tok(D_GMLP), tok_t(D_MLSTM), tok_t(D_MLSTM), tok(D_MLSTM),
        _const_spec(mod.shape),
        _const_spec((1, D_MLSTM)),
        weight_hbm,
        _const_spec((1, D_MODEL)),
        weight_hbm,
        weight_hbm,
        weight_hbm,
        _const_spec((1, D_MODEL)),
    ]
    scratch_shapes = [
        pltpu.VMEM((2, tb, D_MODEL), F32),
        pltpu.VMEM((2, tb, D_MODEL), BF16),
        pltpu.VMEM((2, tb, d_ff), BF16),
        pltpu.VMEM(p["wout"].shape, BF16),
        pltpu.VMEM(p["w1"].shape, BF16),
        pltpu.VMEM(p["w3"].shape, BF16),
        pltpu.VMEM(p["w2"].shape, BF16),
        pltpu.SemaphoreType.DMA((4,)),
    ]
    return pl.pallas_call(
        functools.partial(_outffn_kernel, blocks_per_mod=rows_per_mod // tb2, mod_row0=mod_row0),
        out_shape=jax.ShapeDtypeStruct((n, D_MODEL), F32),
        grid=(nb + 1,),
        in_specs=in_specs,
        out_specs=pl.BlockSpec((tb2, D_MODEL), lambda i: (blk_out(i), 0)),
        scratch_shapes=scratch_shapes,
        compiler_params=pltpu.CompilerParams(
            dimension_semantics=("arbitrary",), vmem_limit_bytes=VMEM_LIMIT_BYTES),
        name="outffn",
    )(x, ak, hft, hbt, o, mod, p["gh"], p["wout"], p["g2"], p["w1"], p["w3"], p["w2"], p["gf"])


def _layer_params(l, g_norm1, b_gate, w_s, b_s, g_v, conv_w, conv_b, g_h, g_norm2, g_final):
    return {
        "g1": g_norm1[l][None, :],
        "bgt": b_gate[l][:, None],
        "cw": conv_w[l], "cb": conv_b[l][None, :],
        "ws": w_s[l].astype(BF16),
        "bs": jnp.repeat(b_s[l].T, HD, axis=1),
        "gv": g_v[l].reshape(1, D_GMLP),
        "gh": g_h[l].reshape(1, D_MLSTM),
        "g2": g_norm2[l][None, :],
        "gf": g_final[None, :],
    }


_LATE_WEIGHTS = ("wout", "w1", "w3", "w2")


def _trunk_front(x, mod_row0, rows_per_mod, seg, p, late=None):
    cast, ada_tail = ((), None) if late is None else (
        tuple(late[0][name] for name in _LATE_WEIGHTS), late[1])
    ak, qvt, o, gr, *side = _inproj(x, p["mod_head"], mod_row0, rows_per_mod, seg, p, cast,
                                    ada_tail)
    if late is not None:
        p = dict(p, **dict(zip(_LATE_WEIGHTS, side[:-1])))
        p["mod_tail"] = side[-1]
    return (ak, qvt, o), gr, p


def _trunk_back(x, front, gr, mod_row0, rows_per_mod, batch, rounds, p, s0=None, m0=None,
                want_state=False):
    ak, qvt, o = front
    hft, hbt, *state = _mlstm(ak, qvt, gr, batch, rounds, s0, m0, want_state)
    y = _outffn(x, ak, hft, hbt, o, p["mod_tail"], mod_row0, rows_per_mod, p)
    return y, state


def kernel(x_prompt, x_sample, state_C, state_n, state_m, c, c_ctx, w_ada, b_ada, g_norm1, w_in,
           b_gate, w_s, b_s, g_v, conv_w, conv_b, g_h, w_out, g_norm2, w1, w3, w2, g_final):
    bp, tp, d = x_prompt.shape
    bs_, ts, _ = x_sample.shape
    depth = w_in.shape[0]
    assert depth == 1, "final norm is fused into the layer's last kernel"
    xp = x_prompt.reshape(bp * tp, d)
    xs = x_sample.reshape(bs_ * ts, d)

    cs = jnp.zeros((8, d), F32).at[0].set(c_ctx).at[1:1 + bs_].set(c)
    new_c, new_n, new_m = [], [], []
    for l in range(depth):
        p = _layer_params(l, g_norm1, b_gate, w_s, b_s, g_v, conv_w, conv_b, g_h, g_norm2, g_final)
        late_f32 = {"wout": w_out[l], "w1": w1[l], "w3": w3[l], "w2": w2[l]}
        p["mod_head"], p["wit"] = _ada(cs, w_ada[l], b_ada[l][None, :], 2 * d, w_in[l].T)
        ada_tail = (cs, w_ada[l], b_ada[l][None, :], 2 * d)

        front_lat, gr_lat, p = _trunk_front(xs, 1, ts, GRID_W, p, (late_f32, ada_tail))
        front_ctx, gr_ctx, _ = _trunk_front(xp, 0, bp * tp, tp, p)

        xp, (c_ctx_out, n_ctx_out, m_ctx_out) = _trunk_back(
            xp, front_ctx, gr_ctx, 0, bp * tp, bp, CTX_SEQS_PER_STEP, p, want_state=True)
        new_c.append(c_ctx_out.reshape(bp, N_DIR, HEADS, HD, HD))
        new_n.append(n_ctx_out.reshape(bp, N_DIR, HEADS, HD))
        new_m.append((m_ctx_out[..., 0] * (1.0 / LOG2E)).reshape(bp, N_DIR, HEADS))

        s0 = jnp.concatenate(
            [jnp.swapaxes(state_C[:, l], -1, -2), state_n[:, l][..., None, :],
             jnp.zeros((bs_, N_DIR, HEADS, ST_ROWS - HD - 1, HD), F32)],
            axis=-2).reshape(bs_, N_UNIT, ST_ROWS, HD)
        m0 = jnp.broadcast_to((state_m[:, l] * LOG2E).reshape(bs_, N_UNIT, 1),
                              (bs_, N_UNIT, SCAN))
        xs, _ = _trunk_back(xs, front_lat, gr_lat, 1, ts, bs_, LAT_CHUNKS_PER_STEP, p, s0=s0, m0=m0)

    return (xp.reshape(bp, tp, d), xs.reshape(bs_, ts, d),
            jnp.stack(new_c, axis=1), jnp.stack(new_n, axis=1), jnp.stack(new_m, axis=1))
```

```python
import functools

import jax
import jax.numpy as jnp
from jax import lax
from jax.experimental import pallas as pl
from jax.experimental.pallas import tpu as pltpu

D_MODEL = 1024
D_GMLP = 512
D_MLSTM = 512
GROUPS = 4
HEADS = 4
HD = 128
CHUNK = 128
SCAN = 256
N_DIR = 2
N_UNIT = N_DIR * HEADS
GRID_W = 64
OFF_V, OFF_Q, OFF_VV, OFF_O, OFF_G = 512, 1024, 2048, 2560, 3072
D_IN = OFF_G + 2 * N_DIR * HEADS
EPS = 1e-6
NEG = -1e30
ST_ROWS = HD + 16
N_GATE = 2 * N_UNIT
GR_ROWS = 6 * N_GATE

TOKEN_BLOCK = 256
CTX_SEQS_PER_STEP = 2
LAT_CHUNKS_PER_STEP = 2
VMEM_LIMIT_BYTES = 56 * 1024 * 1024

F32 = jnp.float32
BF16 = jnp.bfloat16


def _rms(x, g):
    n = x.shape[-1]
    return x * lax.rsqrt(jnp.sum(x * x, axis=-1, keepdims=True) + n * EPS) * (g * n ** 0.5)


LOG2E = 1.4426950408889634
GELU_K = 0.7978845608028654
GELU_C = 0.044715


def _sigmoid(x):
    return 1.0 / (1.0 + jnp.exp2(x * (-LOG2E)))


def _gelu_tanh(x):
    t = (x * x) * (-2.0 * LOG2E * GELU_K * GELU_C) + (-2.0 * LOG2E * GELU_K)
    return x * (1.0 / (1.0 + jnp.exp2(x * t)))


def _log_sigmoid(x):
    return jnp.minimum(x, 0.0) - jnp.log(1.0 + jnp.exp(-jnp.abs(x)))


def _dot(a, b):
    return jnp.dot(a, b, preferred_element_type=F32)


def _dot_nt(a, b):
    return lax.dot_general(a, b, (((1,), (1,)), ((), ())), preferred_element_type=F32)


def _dot_exact(a, b):
    return jnp.dot(a, b, preferred_element_type=F32, precision=lax.Precision.HIGHEST)


def _pipeline_step(tick):
    i = pl.program_id(0)
    last = pl.num_programs(0) - 1

    @pl.when(i == 0)
    def _():
        tick(0, True, False, False)
        tick(1, True, True, False)

    @pl.when(jnp.logical_and(i > 0, i < last))
    def _():
        tick(0, True, True, True)
        tick(1, True, True, True)

    @pl.when(i == last)
    def _():
        tick(0, False, True, True)
        tick(1, False, False, True)


def _const_spec(shape):
    zeros = (0,) * len(shape)
    return pl.BlockSpec(shape, lambda *_: zeros, pipeline_mode=pl.Buffered(1))


def _ada_kernel(c_ref, w_ref, b_ref, o_ref):
    c = c_ref[...]
    s = (c * _sigmoid(c)).astype(BF16)
    o_ref[...] = _dot(s, w_ref[...].astype(BF16)) + b_ref[...]


def _ada_and_round_kernel(c_ref, w_ref, b_ref, wf_ref, o_ref, wb_ref):
    _ada_kernel(c_ref, w_ref, b_ref, o_ref)

    @pl.when(pl.program_id(0) == 0)
    def _():
        wb_ref[...] = wf_ref[...].astype(BF16)


def _ada(cs, w_ada, b_ada, n, w_f32):
    rows, d = cs.shape
    tn = 1024
    return pl.pallas_call(
        _ada_and_round_kernel,
        out_shape=[jax.ShapeDtypeStruct((rows, n), F32), jax.ShapeDtypeStruct(w_f32.shape, BF16)],
        grid=(n // tn,),
        in_specs=[
            pl.BlockSpec((rows, d), lambda j: (0, 0)),
            pl.BlockSpec((d, tn), lambda j: (0, j)),
            pl.BlockSpec((1, tn), lambda j: (0, j)),
            _const_spec(w_f32.shape),
        ],
        out_specs=[pl.BlockSpec((rows, tn), lambda j: (0, j)),
                   pl.BlockSpec(w_f32.shape, lambda j: (0, 0))],
        compiler_params=pltpu.CompilerParams(
            dimension_semantics=("arbitrary",), vmem_limit_bytes=VMEM_LIMIT_BYTES),
        name="ada",
    )(cs, w_ada, b_ada, w_f32)


def _inproj_norm(x, mod, g1, hb_ref):
    sh1 = mod[:, 0:D_MODEL]
    sc1 = mod[:, D_MODEL:2 * D_MODEL]
    hb_ref[...] = (_rms(x, g1 * (1.0 + sc1)) + sh1).astype(BF16)


def _inproj_project(hb_ref, w, z):
    hb = hb_ref[...]
    z["u"][...] = _dot_nt(hb, w["u"][...])
    z["v"][...] = _dot_nt(hb, w["v"][...])
    z["qk"][...] = _dot_nt(hb, w["qk"][...])
    z["o"][...] = _dot_nt(hb, w["o"][...])
    z["vt"][...] = _dot_nt(w["vv"][...], hb).astype(BF16)
    z["gt"][...] = _dot_nt(w["gt"][...], hb)


def _inproj_finish(z, c, out, seg):
    tb = z["u"].shape[0]
    for g in range(GROUPS):
        gs = slice(g * HD, (g + 1) * HD)
        vg = _rms(_gelu_tanh(z["v"][:, gs]), c["gv"][:, gs]).astype(BF16)
        for ch in range(tb // CHUNK):
            cs = slice(ch * CHUNK, (ch + 1) * CHUNK)
            mixed = _dot(c["ws"][g], vg[cs]) + c["bs"][:, gs]
            out["a"][cs, gs] = (_gelu_tanh(z["u"][cs, gs]) * mixed).astype(BF16)

    zqk = z["qk"][...]
    pos = lax.broadcasted_iota(jnp.int32, (tb, 1), 0) % seg
    prev = jnp.where(pos != 0, pltpu.roll(zqk, 1, 0), 0.0)
    nxt = jnp.where(pos != seg - 1, pltpu.roll(zqk, tb - 1, 0), 0.0)
    cw = c["cw"]
    y = c["cb"][...] + prev * cw[0:1, :] + zqk * cw[1:2, :] + nxt * cw[2:3, :]
    y = y * _sigmoid(y)
    out["qt"][...] = jnp.transpose(y[:, :D_MLSTM]).astype(BF16)
    out["k"][...] = (y[:, D_MLSTM:] * (HD ** -0.5)).astype(BF16)
    out["vt"][...] = z["vt"][...]
    out["o"][...] = z["o"][...]

    assert tb == SCAN
    gt = z["gt"][...] + c["bgt"][...]
    gi = pltpu.roll(gt, HEADS, 0)
    lf = _log_sigmoid(gt)
    lane = lax.broadcasted_iota(jnp.int32, (N_GATE, HD), 1)
    tiles = []
    carry = jnp.zeros((N_GATE, HD), F32)
    for t in range(SCAN // HD):
        prefix = lf[:, t * HD:(t + 1) * HD]
        shift = 1
        while shift < HD:
            prefix = prefix + jnp.where(lane >= shift, pltpu.roll(prefix, shift, 1), 0.0)
            shift *= 2
        tiles.append(prefix + carry)
        carry = carry + jnp.broadcast_to(prefix[:, HD - 1:HD], (N_GATE, HD))
    prefix = jnp.concatenate(tiles, axis=1)
    b_last = jnp.concatenate([carry] * (SCAN // HD), axis=1)
    row_bwd = lax.broadcasted_iota(jnp.int32, (N_GATE, SCAN), 0) >= N_GATE // N_DIR
    b_row = jnp.where(row_bwd, b_last - prefix + lf, prefix)
    g_row = b_last - b_row + gi
    g_max = jnp.broadcast_to(jnp.max(g_row, axis=1, keepdims=True), (N_GATE, SCAN))
    d_row = (gi - b_row) * LOG2E
    lane = lax.broadcasted_iota(jnp.int32, (N_GATE, SCAN), 1)
    d_upto, d_from = d_row, d_row
    shift = 1
    while shift < SCAN:
        d_upto = jnp.maximum(d_upto, jnp.where(lane >= shift, pltpu.roll(d_upto, shift, 1), NEG))
        d_from = jnp.maximum(
            d_from, jnp.where(lane < SCAN - shift, pltpu.roll(d_from, SCAN - shift, 1), NEG))
        shift *= 2
    d_max = jnp.where(row_bwd, d_from, d_upto)
    for sec, stat in enumerate((b_row, g_row, b_last, g_max)):
        out["gr"][sec * N_GATE:(sec + 1) * N_GATE, :] = stat * LOG2E
    out["gr"][4 * N_GATE:5 * N_GATE, :] = d_row
    out["gr"][5 * N_GATE:6 * N_GATE, :] = d_max


_INPROJ_Z = ("u", "v", "qk", "o", "vt", "gt")


def _inproj_kernel(*refs, seg, blocks_per_mod, mod_row0, n_cast, ada_tail):
    refs = list(refs)
    take = lambda k: [refs.pop(0) for _ in range(k)]
    (x_ref, mod_ref, g1_ref, wit_ref,
     bgt_ref, cw_ref, cb_ref, ws_ref, bs_ref, gv_ref) = take(10)
    cast_in = take(n_cast)
    ada_in = take(3) if ada_tail else None
    ak_ref, qvt_ref, o_ref, gr_ref = take(4)
    cast_out = take(n_cast)
    ada_out = take(1)[0] if ada_tail else None
    hb_s, *z_s = refs
    tb = x_ref.shape[0] // 2
    mod_in = mod_row0 + jnp.minimum(pl.program_id(0), pl.num_programs(0) - 2) // blocks_per_mod
    w = {"u": wit_ref.at[0:OFF_V, :], "v": wit_ref.at[OFF_V:OFF_Q, :],
         "qk": wit_ref.at[OFF_Q:OFF_VV, :], "vv": wit_ref.at[OFF_VV:OFF_O, :],
         "o": wit_ref.at[OFF_O:OFF_G, :], "gt": wit_ref.at[OFF_G:D_IN, :]}
    c = {"gv": gv_ref, "ws": ws_ref, "bs": bs_ref, "cw": cw_ref, "cb": cb_ref, "bgt": bgt_ref}

    def tick(half, norm, project, finish):
        rows = slice(half * tb, (half + 1) * tb)
        cur, oth = half, 1 - half
        z_cur = {name: ref.at[cur] for name, ref in zip(_INPROJ_Z, z_s)}
        z_oth = {name: ref.at[oth] for name, ref in zip(_INPROJ_Z, z_s)}
        out = {"a": ak_ref.at[rows, 0:D_GMLP], "k": ak_ref.at[rows, D_GMLP:],
               "qt": qvt_ref.at[0:D_MLSTM, rows], "vt": qvt_ref.at[D_MLSTM:, rows],
               "o": o_ref.at[rows, :], "gr": gr_ref.at[:, rows]}
        if project:
            _inproj_project(hb_s.at[oth], w, z_oth)
        if norm:
            _inproj_norm(x_ref[rows, :], mod_ref[pl.ds(mod_in, 1), :], g1_ref[...], hb_s.at[cur])
        if finish:
            _inproj_finish(z_cur, c, out, seg)
        if half == 0:
            for src, dst in zip(cast_in, cast_out):
                dst[...] = src[...].astype(BF16)
            if ada_tail:
                _ada_kernel(*ada_in, ada_out)

    _pipeline_step(tick)


def _inproj(x, mod, mod_row0, rows_per_mod, seg, p, cast=(), ada_tail=None):
    n = x.shape[0]
    tb = TOKEN_BLOCK
    tb2 = 2 * tb
    assert tb % seg == 0 and n % tb2 == 0 and rows_per_mod % tb2 == 0
    nb = n // tb2
    blk_in = lambda i: jnp.minimum(i, nb - 1)
    blk_out = lambda i: jnp.maximum(i - 1, 0)
    tok = lambda width: pl.BlockSpec((tb2, width), lambda i: (blk_out(i), 0))
    tok_t = lambda height: pl.BlockSpec((height, tb2), lambda i: (0, blk_out(i)))
    in_specs = [
        pl.BlockSpec((tb2, D_MODEL), lambda i: (blk_in(i), 0)),
        _const_spec(mod.shape),
        _const_spec((1, D_MODEL)),
        _const_spec((D_IN, D_MODEL)),
        _const_spec((N_GATE, 1)),
        _const_spec((3, 2 * D_MLSTM)),
        _const_spec((1, 2 * D_MLSTM)),
        _const_spec((GROUPS, CHUNK, CHUNK)),
        _const_spec((CHUNK, D_GMLP)),
        _const_spec((1, D_GMLP)),
    ]
    out_shape = [
        jax.ShapeDtypeStruct((n, D_GMLP + D_MLSTM), BF16),
        jax.ShapeDtypeStruct((2 * D_MLSTM, n), BF16),
        jax.ShapeDtypeStruct((n, D_MLSTM), F32),
        jax.ShapeDtypeStruct((GR_ROWS, n), F32),
    ]
    out_specs = [tok(D_GMLP + D_MLSTM), tok_t(2 * D_MLSTM), tok(D_MLSTM), tok_t(GR_ROWS)]
    for wf in cast:
        rows = wf.shape[0] // nb
        assert rows * nb == wf.shape[0] and rows % 16 == 0
        slab = pl.BlockSpec((rows, wf.shape[1]), lambda i: (blk_in(i), 0))
        in_specs.append(slab)
        out_specs.append(slab)
        out_shape.append(jax.ShapeDtypeStruct(wf.shape, BF16))
    side = list(cast)
    if ada_tail is not None:
        cs, w_ada, b_ada, col0 = ada_tail
        width = (w_ada.shape[1] - col0) // nb
        assert col0 % width == 0 and width * nb == w_ada.shape[1] - col0 and width % 128 == 0
        in_specs += [
            _const_spec(cs.shape),
            pl.BlockSpec((w_ada.shape[0], width), lambda i: (0, col0 // width + blk_in(i))),
            pl.BlockSpec((1, width), lambda i: (0, col0 // width + blk_in(i))),
        ]
        out_specs.append(pl.BlockSpec((cs.shape[0], width), lambda i: (0, blk_in(i))))
        out_shape.append(jax.ShapeDtypeStruct((cs.shape[0], w_ada.shape[1] - col0), F32))
        side += [cs, w_ada, b_ada]
    scratch_shapes = [
        pltpu.VMEM((2, tb, D_MODEL), BF16),
        pltpu.VMEM((2, tb, D_GMLP), F32),
        pltpu.VMEM((2, tb, D_GMLP), F32),
        pltpu.VMEM((2, tb, 2 * D_MLSTM), F32),
        pltpu.VMEM((2, tb, D_MLSTM), F32),
        pltpu.VMEM((2, D_MLSTM, tb), BF16),
        pltpu.VMEM((2, N_GATE, tb), F32),
    ]
    return pl.pallas_call(
        functools.partial(_inproj_kernel, seg=seg, blocks_per_mod=rows_per_mod // tb2,
                          mod_row0=mod_row0, n_cast=len(cast), ada_tail=ada_tail is not None),
        out_shape=out_shape,
        grid=(nb + 1,),
        in_specs=in_specs,
        out_specs=out_specs,
        scratch_shapes=scratch_shapes,
        compiler_params=pltpu.CompilerParams(
            dimension_semantics=("arbitrary",), vmem_limit_bytes=VMEM_LIMIT_BYTES),
        name="inproj",
    )(x, mod, p["g1"], p["wit"],
      p["bgt"], p["cw"], p["cb"], p["ws"], p["bs"], p["gv"], *side)


def _lane_broadcast_column(row):
    n = row.shape[1]
    tiles = [jnp.transpose(jnp.broadcast_to(row[:, t * HD:(t + 1) * HD], (HD, HD)))
             for t in range(n // HD)]
    col = jnp.concatenate(tiles, axis=0)
    return jnp.concatenate([col] * (n // HD), axis=1)


def _mlstm_unit(k, vt, kq, qs, d_row, dmax_row, b_row, g_row, bl_row, gmax_row, st, m_row, mask,
                carried):
    a = b_row + m_row
    mj = jnp.maximum(a, dmax_row + b_row)
    w = jnp.exp2(jnp.where(mask, _lane_broadcast_column(d_row) + (b_row - mj), NEG))
    s = kq * w
    num = _dot(vt, s.astype(BF16))
    den = jnp.sum(s, axis=0, keepdims=True)
    if carried:
        inter = jnp.exp2(a - mj)
        num = num + inter * qs[:HD]
        den = den + inter * qs[HD:HD + 1]
    h = num * (1.0 / jnp.maximum(jnp.abs(den), jnp.exp2(-mj)))
    m_new = jnp.maximum(bl_row + m_row, gmax_row)
    wc = jnp.exp2(g_row - m_new)
    pad_row = lax.broadcasted_iota(jnp.int32, (ST_ROWS - HD, wc.shape[1]), 0)
    v_aug = jnp.concatenate(
        [vt.astype(F32) * wc, jnp.where(pad_row == 0, wc, 0.0)], axis=0).astype(BF16)
    st_new = _dot(v_aug, k)
    if carried:
        decay = jnp.exp2(bl_row + m_row - m_new)
        st_new = decay[:, :HD] * st + st_new
    return h, st_new, m_new


def _mlstm_kernel(*refs, has_init, has_out, rounds, by_sequence):
    if has_init:
        s0_ref, m0_ref = refs[:2]
        refs = refs[2:]
    def split(k_ref, qvt_ref, gr_ref):
        return k_ref, qvt_ref.at[0:D_MLSTM, :], qvt_ref.at[D_MLSTM:, :], gr_ref

    if by_sequence:
        fwd_in = bwd_in = split(*refs[:3])
        refs = refs[3:]
    else:
        fwd_in, bwd_in = split(*refs[:3]), split(*refs[3:6])
        refs = refs[6:]
    hf_ref, hb_ref = refs[:2]
    refs = refs[2:]
    if has_out:
        co_ref, no_ref, mo_ref = refs[:3]
        refs = refs[3:]
    s_ref, m_ref = refs
    c = pl.program_id(1)
    nc = pl.num_programs(1)
    carried = not by_sequence

    if carried:
        @pl.when(c == 0)
        def _():
            if has_init:
                s_ref[...] = s0_ref[0]
                m_ref[...] = m0_ref[0]
            else:
                s_ref[...] = jnp.zeros_like(s_ref)
                m_ref[...] = jnp.zeros_like(m_ref)

    si = lax.broadcasted_iota(jnp.int32, (SCAN, SCAN), 0)
    ji = lax.broadcasted_iota(jnp.int32, (SCAN, SCAN), 1)
    dirs = ((fwd_in, hf_ref, si <= ji), (bwd_in, hb_ref, si >= ji))

    def chunk_cols(d, r):
        pos = rounds - 1 - r if (d == 1 and not by_sequence) else r
        return slice(pos * SCAN, (pos + 1) * SCAN)

    kq = {}
    for r in range(rounds):
        for d, ((k_ref, qt_ref, _, _), _, _) in enumerate(dirs):
            cs = chunk_cols(d, r)
            for hd in range(HEADS):
                hs = slice(hd * HD, (hd + 1) * HD)
                kq[r, d, hd] = _dot(k_ref[cs, hs], qt_ref[hs, cs])

    m_cur = None
    if carried:
        m_all = m_ref[...]
        m_cur = [m_all[u:u + 1] for u in range(N_UNIT)]
    zero_row = jnp.zeros((1, SCAN), F32)
    for r in range(rounds):
        slot = r * N_UNIT if by_sequence else 0
        for d, ((k_ref, qt_ref, vt_ref, gr_ref), h_ref, mask) in enumerate(dirs):
            cs = chunk_cols(d, r)
            for hd in range(HEADS):
                u = d * HEADS + hd
                row = d * 2 * HEADS + HEADS + hd
                hs = slice(hd * HD, (hd + 1) * HD)
                st = s_ref[slot + u] if carried else None
                qs = _dot(st.astype(BF16), qt_ref[hs, cs]) if carried else None
                h, st_new, m_new = _mlstm_unit(
                    k_ref[cs, hs], vt_ref[hs, cs], kq[r, d, hd], qs,
                    *(gr_ref[sec * N_GATE + row:sec * N_GATE + row + 1, cs]
                      for sec in (4, 5, 0, 1, 2, 3)),
                    st, m_cur[u] if carried else zero_row, mask, carried)
                h_ref[hs, cs] = h
                s_ref[slot + u] = st_new
                if carried:
                    m_cur[u] = m_new
                else:
                    m_ref[slot + u:slot + u + 1, :] = m_new
    if carried:
        m_ref[...] = jnp.concatenate(m_cur, axis=0)

    if has_out:
        @pl.when(c == nc - 1)
        def _():
            for j in range(s_ref.shape[0]):
                q, u = divmod(j, N_UNIT)
                co_ref[q, u] = jnp.transpose(s_ref[j, 0:HD, :])
                no_ref[q, u:u + 1, :] = s_ref[j, HD:HD + 1, :]
                mo_ref[q, u:u + 1, :] = m_ref[j:j + 1, :]


def _mlstm(ak, qvt, gr, batch, rounds, s0=None, m0=None, want_state=False):
    n = ak.shape[0]
    nc = n // batch // SCAN
    by_sequence = nc == 1
    width = rounds * SCAN
    has_init = s0 is not None
    if by_sequence:
        assert batch % rounds == 0 and not has_init
        grid = (batch // rounds, 1)
        slots = rounds
        fwd = bwd = lambda b, c: b
    else:
        assert nc % rounds == 0 and not want_state
        steps = nc // rounds
        grid = (batch, steps)
        slots = 1
        fwd = lambda b, c: b * steps + c
        bwd = lambda b, c: b * steps + steps - 1 - c

    def specs(ix):
        return [
            pl.BlockSpec((width, D_MLSTM), lambda b, c: (ix(b, c), 1)),
            pl.BlockSpec((2 * D_MLSTM, width), lambda b, c: (0, ix(b, c))),
            pl.BlockSpec((GR_ROWS, width), lambda b, c: (0, ix(b, c))),
        ]

    in_specs = specs(fwd) if by_sequence else specs(fwd) + specs(bwd)
    args = [ak, qvt, gr] if by_sequence else [ak, qvt, gr, ak, qvt, gr]
    if has_init:
        in_specs = [
            pl.BlockSpec((1, N_UNIT, ST_ROWS, HD), lambda b, c: (b, 0, 0, 0)),
            pl.BlockSpec((1, N_UNIT, SCAN), lambda b, c: (b, 0, 0)),
        ] + in_specs
        args = [s0, m0] + args
    out_shape = [
        jax.ShapeDtypeStruct((D_MLSTM, n), F32),
        jax.ShapeDtypeStruct((D_MLSTM, n), F32),
    ]
    out_specs = [
        pl.BlockSpec((D_MLSTM, width), lambda b, c: (0, fwd(b, c))),
        pl.BlockSpec((D_MLSTM, width), lambda b, c: (0, bwd(b, c))),
    ]
    if want_state:
        out_shape += [
            jax.ShapeDtypeStruct((batch, N_UNIT, HD, HD), F32),
            jax.ShapeDtypeStruct((batch, N_UNIT, HD), F32),
            jax.ShapeDtypeStruct((batch, N_UNIT, SCAN), F32),
        ]
        out_specs += [
            pl.BlockSpec((slots, N_UNIT, HD, HD), lambda b, c: (b, 0, 0, 0)),
            pl.BlockSpec((slots, N_UNIT, HD), lambda b, c: (b, 0, 0)),
            pl.BlockSpec((slots, N_UNIT, SCAN), lambda b, c: (b, 0, 0)),
        ]
    return pl.pallas_call(
        functools.partial(_mlstm_kernel, has_init=has_init, has_out=want_state, rounds=rounds,
                          by_sequence=by_sequence),
        out_shape=out_shape,
        grid=grid,
        in_specs=in_specs,
        out_specs=out_specs,
        scratch_shapes=[pltpu.VMEM((slots * N_UNIT, ST_ROWS, HD), F32),
                        pltpu.VMEM((slots * N_UNIT, SCAN), F32)],
        compiler_params=pltpu.CompilerParams(
            dimension_semantics=("arbitrary", "arbitrary"), vmem_limit_bytes=VMEM_LIMIT_BYTES),
        name="mlstm",
    )(*args)


def _outffn_mix(x, a, hft, hbt, o, mod, c, x1_ref, h2_ref):
    ga1 = mod[:, 0:D_MODEL]
    sh2 = mod[:, D_MODEL:2 * D_MODEL]
    sc2 = mod[:, 2 * D_MODEL:3 * D_MODEL]
    hs = jnp.transpose(hft + hbt)
    sig = _sigmoid(o)
    parts = []
    for hd in range(HEADS):
        sl = slice(hd * HD, (hd + 1) * HD)
        parts.append(_rms(hs[:, sl], c["gh"][:, sl]) * sig[:, sl])
    b_out = jnp.concatenate(parts, axis=-1).astype(BF16)
    mix = _dot(a, c["wout"][0:D_GMLP, :]) + _dot(b_out, c["wout"][D_GMLP:, :])
    x1 = x + ga1 * mix
    x1_ref[...] = x1
    h2_ref[...] = (_rms(x1, c["g2"][...] * (1.0 + sc2)) + sh2).astype(BF16)


def _outffn_up(h2_ref, c, f_ref):
    h2 = h2_ref[...]
    u = _dot(h2, c["w1"][...])
    g = _dot(h2, c["w3"][...])
    f_ref[...] = (u * _sigmoid(u) * g).astype(BF16)


def _outffn_down(f_ref, x1_ref, mod, c, y_ref):
    ga2 = mod[:, 3 * D_MODEL:4 * D_MODEL]
    x2 = x1_ref[...] + ga2 * _dot(f_ref[...], c["w2"][...])
    y_ref[...] = _rms(x2, c["gf"][...])


def _outffn_kernel(x_ref, a_ref, hft_ref, hbt_ref, o_ref, mod_ref, gh_ref, wout_hbm,
                   g2_ref, w1_hbm, w3_hbm, w2_hbm, gf_ref, y_ref,
                   x1_s, h2_s, f_s, wout_s, w1_s, w3_s, w2_s, w_sem,
                   *, blocks_per_mod, mod_row0):
    tb = x_ref.shape[0] // 2
    i = pl.program_id(0)
    mod_in = mod_row0 + jnp.minimum(i, pl.num_programs(0) - 2) // blocks_per_mod
    mod_out = mod_row0 + jnp.maximum(i - 1, 0) // blocks_per_mod
    c = {"gh": gh_ref, "wout": wout_s, "g2": g2_ref, "w1": w1_s, "w3": w3_s, "w2": w2_s,
         "gf": gf_ref}
    weight_copy = {
        name: pltpu.make_async_copy(src, dst, w_sem.at[j])
        for j, (name, src, dst) in enumerate((("wout", wout_hbm, wout_s), ("w1", w1_hbm, w1_s),
                                              ("w3", w3_hbm, w3_s), ("w2", w2_hbm, w2_s)))}

    def tick(half, mix, up, down):
        rows = slice(half * tb, (half + 1) * tb)
        cur, oth = half, 1 - half
        first_tick = mix and not up and not down
        second_tick = mix and up and not down
        if first_tick:
            for name in ("wout", "w1", "w3", "w2"):
                weight_copy[name].start()
            weight_copy["wout"].wait()
        if down:
            _outffn_down(f_s.at[cur], x1_s.at[cur], mod_ref[pl.ds(mod_out, 1), :], c,
                         y_ref.at[rows, :])
        if mix:
            _outffn_mix(x_ref[rows, :], a_ref[rows, :], hft_ref[:, rows], hbt_ref[:, rows],
                        o_ref[rows, :], mod_ref[pl.ds(mod_in, 1), :], c, x1_s.at[cur],
                        h2_s.at[cur])
        if second_tick:
            weight_copy["w1"].wait()
            weight_copy["w3"].wait()
        if up:
            _outffn_up(h2_s.at[oth], c, f_s.at[oth])
        if second_tick:
            weight_copy["w2"].wait()

    _pipeline_step(tick)


def _outffn(x, ak, hft, hbt, o, mod, mod_row0, rows_per_mod, p):
    n = x.shape[0]
    tb = TOKEN_BLOCK
    tb2 = 2 * tb
    assert n % tb2 == 0 and rows_per_mod % tb2 == 0
    nb = n // tb2
    d_ff = p["w1"].shape[1]
    blk_in = lambda i: jnp.minimum(i, nb - 1)
    blk_out = lambda i: jnp.maximum(i - 1, 0)
    tok = lambda w: pl.BlockSpec((tb2, w), lambda i: (blk_in(i), 0))
    tok_t = lambda h: pl.BlockSpec((h, tb2), lambda i: (0, blk_in(i)))
    weight_hbm = pl.BlockSpec(memory_space=pl.ANY)
    in_specs = [
        tok(D_MODEL), tok(D_GMLP), tok_t(D_MLSTM), tok_t(D_MLSTM), tok(D_MLSTM),
        _const_spec(mod.shape),
        _const_spec((1, D_MLSTM)),
        weight_hbm,
        _const_spec((1, D_MODEL)),
        weight_hbm,
        weight_hbm,
        weight_hbm,
        _const_spec((1, D_MODEL)),
    ]
    scratch_shapes = [
        pltpu.VMEM((2, tb, D_MODEL), F32),
        pltpu.VMEM((2, tb, D_MODEL), BF16),
        pltpu.VMEM((2, tb, d_ff), BF16),
        pltpu.VMEM(p["wout"].shape, BF16),
        pltpu.VMEM(p["w1"].shape, BF16),
        pltpu.VMEM(p["w3"].shape, BF16),
        pltpu.VMEM(p["w2"].shape, BF16),
        pltpu.SemaphoreType.DMA((4,)),
    ]
    return pl.pallas_call(
        functools.partial(_outffn_kernel, blocks_per_mod=rows_per_mod // tb2, mod_row0=mod_row0),
        out_shape=jax.ShapeDtypeStruct((n, D_MODEL), F32),
        grid=(nb + 1,),
        in_specs=in_specs,
        out_specs=pl.BlockSpec((tb2, D_MODEL), lambda i: (blk_out(i), 0)),
        scratch_shapes=scratch_shapes,
        compiler_params=pltpu.CompilerParams(
            dimension_semantics=("arbitrary",), vmem_limit_bytes=VMEM_LIMIT_BYTES),
        name="outffn",
    )(x, ak, hft, hbt, o, mod, p["gh"], p["wout"], p["g2"], p["w1"], p["w3"], p["w2"], p["gf"])


def _layer_params(l, g_norm1, b_gate, w_s, b_s, g_v, conv_w, conv_b, g_h, g_norm2, g_final):
    return {
        "g1": g_norm1[l][None, :],
        "bgt": b_gate[l][:, None],
        "cw": conv_w[l], "cb": conv_b[l][None, :],
        "ws": w_s[l].astype(BF16),
        "bs": jnp.repeat(b_s[l].T, HD, axis=1),
        "gv": g_v[l].reshape(1, D_GMLP),
        "gh": g_h[l].reshape(1, D_MLSTM),
        "g2": g_norm2[l][None, :],
        "gf": g_final[None, :],
    }


_LATE_WEIGHTS = ("wout", "w1", "w3", "w2")


def _trunk_front(x, mod_row0, rows_per_mod, seg, p, late=None):
    cast, ada_tail = ((), None) if late is None else (
        tuple(late[0][name] for name in _LATE_WEIGHTS), late[1])
    ak, qvt, o, gr, *side = _inproj(x, p["mod_head"], mod_row0, rows_per_mod, seg, p, cast,
                                    ada_tail)
    if late is not None:
        p = dict(p, **dict(zip(_LATE_WEIGHTS, side[:-1])))
        p["mod_tail"] = side[-1]
    return (ak, qvt, o), gr, p


def _trunk_back(x, front, gr, mod_row0, rows_per_mod, batch, rounds, p, s0=None, m0=None,
                want_state=False):
    ak, qvt, o = front
    hft, hbt, *state = _mlstm(ak, qvt, gr, batch, rounds, s0, m0, want_state)
    y = _outffn(x, ak, hft, hbt, o, p["mod_tail"], mod_row0, rows_per_mod, p)
    return y, state


def kernel(x_prompt, x_sample, state_C, state_n, state_m, c, c_ctx, w_ada, b_ada, g_norm1, w_in,
           b_gate, w_s, b_s, g_v, conv_w, conv_b, g_h, w_out, g_norm2, w1, w3, w2, g_final):
    bp, tp, d = x_prompt.shape
    bs_, ts, _ = x_sample.shape
    depth = w_in.shape[0]
    assert depth == 1, "final norm is fused into the layer's last kernel"
    xp = x_prompt.reshape(bp * tp, d)
    xs = x_sample.reshape(bs_ * ts, d)

    cs = jnp.zeros((8, d), F32).at[0].set(c_ctx).at[1:1 + bs_].set(c)
    new_c, new_n, new_m = [], [], []
    for l in range(depth):
        p = _layer_params(l, g_norm1, b_gate, w_s, b_s, g_v, conv_w, conv_b, g_h, g_norm2, g_final)
        late_f32 = {"wout": w_out[l], "w1": w1[l], "w3": w3[l], "w2": w2[l]}
        p["mod_head"], p["wit"] = _ada(cs, w_ada[l], b_ada[l][None, :], 2 * d, w_in[l].T)
        ada_tail = (cs, w_ada[l], b_ada[l][None, :], 2 * d)

        front_lat, gr_lat, p = _trunk_front(xs, 1, ts, GRID_W, p, (late_f32, ada_tail))
        front_ctx, gr_ctx, _ = _trunk_front(xp, 0, bp * tp, tp, p)

        xp, (c_ctx_out, n_ctx_out, m_ctx_out) = _trunk_back(
            xp, front_ctx, gr_ctx, 0, bp * tp, bp, CTX_SEQS_PER_STEP, p, want_state=True)
        new_c.append(c_ctx_out.reshape(bp, N_DIR, HEADS, HD, HD))
        new_n.append(n_ctx_out.reshape(bp, N_DIR, HEADS, HD))
        new_m.append((m_ctx_out[..., 0] * (1.0 / LOG2E)).reshape(bp, N_DIR, HEADS))

        s0 = jnp.concatenate(
            [jnp.swapaxes(state_C[:, l], -1, -2), state_n[:, l][..., None, :],
             jnp.zeros((bs_, N_DIR, HEADS, ST_ROWS - HD - 1, HD), F32)],
            axis=-2).reshape(bs_, N_UNIT, ST_ROWS, HD)
        m0 = jnp.broadcast_to((state_m[:, l] * LOG2E).reshape(bs_, N_UNIT, 1),
                              (bs_, N_UNIT, SCAN))
        xs, _ = _trunk_back(xs, front_lat, gr_lat, 1, ts, bs_, LAT_CHUNKS_PER_STEP, p, s0=s0, m0=m0)

    return (xp.reshape(bp, tp, d), xs.reshape(bs_, ts, d),
            jnp.stack(new_c, axis=1), jnp.stack(new_n, axis=1), jnp.stack(new_m, axis=1))
```

```python
import functools

import jax
import jax.numpy as jnp
from jax import lax
from jax.experimental import pallas as pl
from jax.experimental.pallas import tpu as pltpu

D_MODEL = 1024
D_GMLP = 512
D_MLSTM = 512
GROUPS = 4
HEADS = 4
HD = 128
CHUNK = 128
SCAN = 256
N_DIR = 2
N_UNIT = N_DIR * HEADS
GRID_W = 64
OFF_V, OFF_Q, OFF_VV, OFF_O, OFF_G = 512, 1024, 2048, 2560, 3072
D_IN = OFF_G + 2 * N_DIR * HEADS
EPS = 1e-6
NEG = -1e30
ST_ROWS = HD + 16
N_GATE = 2 * N_UNIT
GR_ROWS = 6 * N_GATE

TOKEN_BLOCK = 256
CTX_SEQS_PER_STEP = 2
LAT_CHUNKS_PER_STEP = 2
VMEM_LIMIT_BYTES = 56 * 1024 * 1024

F32 = jnp.float32
BF16 = jnp.bfloat16


def _rms(x, g):
    n = x.shape[-1]
    return x * lax.rsqrt(jnp.sum(x * x, axis=-1, keepdims=True) + n * EPS) * (g * n ** 0.5)


LOG2E = 1.4426950408889634
GELU_K = 0.7978845608028654
GELU_C = 0.044715


def _sigmoid(x):
    return 1.0 / (1.0 + jnp.exp2(x * (-LOG2E)))


def _gelu_tanh(x):
    t = (x * x) * (-2.0 * LOG2E * GELU_K * GELU_C) + (-2.0 * LOG2E * GELU_K)
    return x * (1.0 / (1.0 + jnp.exp2(x * t)))


def _log_sigmoid(x):
    return jnp.minimum(x, 0.0) - jnp.log(1.0 + jnp.exp(-jnp.abs(x)))


def _dot(a, b):
    return jnp.dot(a, b, preferred_element_type=F32)


def _dot_nt(a, b):
    return lax.dot_general(a, b, (((1,), (1,)), ((), ())), preferred_element_type=F32)


def _dot_exact(a, b):
    return jnp.dot(a, b, preferred_element_type=F32, precision=lax.Precision.HIGHEST)


def _pipeline_step(tick):
    i = pl.program_id(0)
    last = pl.num_programs(0) - 1

    @pl.when(i == 0)
    def _():
        tick(0, True, False, False)
        tick(1, True, True, False)

    @pl.when(jnp.logical_and(i > 0, i < last))
    def _():
        tick(0, True, True, True)
        tick(1, True, True, True)

    @pl.when(i == last)
    def _():
        tick(0, False, True, True)
        tick(1, False, False, True)


def _const_spec(shape):
    zeros = (0,) * len(shape)
    return pl.BlockSpec(shape, lambda *_: zeros, pipeline_mode=pl.Buffered(1))


def _ada_kernel(c_ref, w_ref, b_ref, o_ref):
    c = c_ref[...]
    s = (c * _sigmoid(c)).astype(BF16)
    o_ref[...] = _dot(s, w_ref[...].astype(BF16)) + b_ref[...]


def _ada_and_round_kernel(c_ref, w_ref, b_ref, wf_ref, o_ref, wb_ref):
    _ada_kernel(c_ref, w_ref, b_ref, o_ref)

    @pl.when(pl.program_id(0) == 0)
    def _():
        wb_ref[...] = wf_ref[...].astype(BF16)


def _ada(cs, w_ada, b_ada, n, w_f32):
    rows, d = cs.shape
    tn = 1024
    return pl.pallas_call(
        _ada_and_round_kernel,
        out_shape=[jax.ShapeDtypeStruct((rows, n), F32), jax.ShapeDtypeStruct(w_f32.shape, BF16)],
        grid=(n // tn,),
        in_specs=[
            pl.BlockSpec((rows, d), lambda j: (0, 0)),
            pl.BlockSpec((d, tn), lambda j: (0, j)),
            pl.BlockSpec((1, tn), lambda j: (0, j)),
            _const_spec(w_f32.shape),
        ],
        out_specs=[pl.BlockSpec((rows, tn), lambda j: (0, j)),
                   pl.BlockSpec(w_f32.shape, lambda j: (0, 0))],
        compiler_params=pltpu.CompilerParams(
            dimension_semantics=("arbitrary",), vmem_limit_bytes=VMEM_LIMIT_BYTES),
        name="ada",
    )(cs, w_ada, b_ada, w_f32)


def _inproj_norm(x, mod, g1, hb_ref):
    sh1 = mod[:, 0:D_MODEL]
    sc1 = mod[:, D_MODEL:2 * D_MODEL]
    hb_ref[...] = (_rms(x, g1 * (1.0 + sc1)) + sh1).astype(BF16)


def _inproj_project(hb_ref, w, z):
    hb = hb_ref[...]
    z["u"][...] = _dot_nt(hb, w["u"][...])
    z["v"][...] = _dot_nt(hb, w["v"][...])
    z["qk"][...] = _dot_nt(hb, w["qk"][...])
    z["o"][...] = _dot_nt(hb, w["o"][...])
    z["vt"][...] = _dot_nt(w["vv"][...], hb).astype(BF16)
    z["gt"][...] = _dot_nt(w["gt"][...], hb)


def _inproj_finish(z, c, out, seg):
    tb = z["u"].shape[0]
    for g in range(GROUPS):
        gs = slice(g * HD, (g + 1) * HD)
        vg = _rms(_gelu_tanh(z["v"][:, gs]), c["gv"][:, gs]).astype(BF16)
        for ch in range(tb // CHUNK):
            cs = slice(ch * CHUNK, (ch + 1) * CHUNK)
            mixed = _dot(c["ws"][g], vg[cs]) + c["bs"][:, gs]
            out["a"][cs, gs] = (_gelu_tanh(z["u"][cs, gs]) * mixed).astype(BF16)

    zqk = z["qk"][...]
    pos = lax.broadcasted_iota(jnp.int32, (tb, 1), 0) % seg
    prev = jnp.where(pos != 0, pltpu.roll(zqk, 1, 0), 0.0)
    nxt = jnp.where(pos != seg - 1, pltpu.roll(zqk, tb - 1, 0), 0.0)
    cw = c["cw"]
    y = c["cb"][...] + prev * cw[0:1, :] + zqk * cw[1:2, :] + nxt * cw[2:3, :]
    y = y * _sigmoid(y)
    out["qt"][...] = jnp.transpose(y[:, :D_MLSTM]).astype(BF16)
    out["k"][...] = (y[:, D_MLSTM:] * (HD ** -0.5)).astype(BF16)
    out["vt"][...] = z["vt"][...]
    out["o"][...] = z["o"][...]

    assert tb == SCAN
    gt = z["gt"][...] + c["bgt"][...]
    gi = pltpu.roll(gt, HEADS, 0)
    lf = _log_sigmoid(gt)
    lane = lax.broadcasted_iota(jnp.int32, (N_GATE, HD), 1)
    tiles = []
    carry = jnp.zeros((N_GATE, HD), F32)
    for t in range(SCAN // HD):
        prefix = lf[:, t * HD:(t + 1) * HD]
        shift = 1
        while shift < HD:
            prefix = prefix + jnp.where(lane >= shift, pltpu.roll(prefix, shift, 1), 0.0)
            shift *= 2
        tiles.append(prefix + carry)
        carry = carry + jnp.broadcast_to(prefix[:, HD - 1:HD], (N_GATE, HD))
    prefix = jnp.concatenate(tiles, axis=1)
    b_last = jnp.concatenate([carry] * (SCAN // HD), axis=1)
    row_bwd = lax.broadcasted_iota(jnp.int32, (N_GATE, SCAN), 0) >= N_GATE // N_DIR
    b_row = jnp.where(row_bwd, b_last - prefix + lf, prefix)
    g_row = b_last - b_row + gi
    g_max = jnp.broadcast_to(jnp.max(g_row, axis=1, keepdims=True), (N_GATE, SCAN))
    d_row = (gi - b_row) * LOG2E
    lane = lax.broadcasted_iota(jnp.int32, (N_GATE, SCAN), 1)
    d_upto, d_from = d_row, d_row
    shift = 1
    while shift < SCAN:
        d_upto = jnp.maximum(d_upto, jnp.where(lane >= shift, pltpu.roll(d_upto, shift, 1), NEG))
        d_from = jnp.maximum(
            d_from, jnp.where(lane < SCAN - shift, pltpu.roll(d_from, SCAN - shift, 1), NEG))
        shift *= 2
    d_max = jnp.where(row_bwd, d_from, d_upto)
    for sec, stat in enumerate((b_row, g_row, b_last, g_max)):
        out["gr"][sec * N_GATE:(sec + 1) * N_GATE, :] = stat * LOG2E
    out["gr"][4 * N_GATE:5 * N_GATE, :] = d_row
    out["gr"][5 * N_GATE:6 * N_GATE, :] = d_max


_INPROJ_Z = ("u", "v", "qk", "o", "vt", "gt")


def _inproj_kernel(*refs, seg, blocks_per_mod, mod_row0, n_cast, ada_tail):
    refs = list(refs)
    take = lambda k: [refs.pop(0) for _ in range(k)]
    (x_ref, mod_ref, g1_ref, wit_ref,
     bgt_ref, cw_ref, cb_ref, ws_ref, bs_ref, gv_ref) = take(10)
    cast_in = take(n_cast)
    ada_in = take(3) if ada_tail else None
    ak_ref, qvt_ref, o_ref, gr_ref = take(4)
    cast_out = take(n_cast)
    ada_out = take(1)[0] if ada_tail else None
    hb_s, *z_s = refs
    tb = x_ref.shape[0] // 2
    mod_in = mod_row0 + jnp.minimum(pl.program_id(0), pl.num_programs(0) - 2) // blocks_per_mod
    w = {"u": wit_ref.at[0:OFF_V, :], "v": wit_ref.at[OFF_V:OFF_Q, :],
         "qk": wit_ref.at[OFF_Q:OFF_VV, :], "vv": wit_ref.at[OFF_VV:OFF_O, :],
         "o": wit_ref.at[OFF_O:OFF_G, :], "gt": wit_ref.at[OFF_G:D_IN, :]}
    c = {"gv": gv_ref, "ws": ws_ref, "bs": bs_ref, "cw": cw_ref, "cb": cb_ref, "bgt": bgt_ref}

    def tick(half, norm, project, finish):
        rows = slice(half * tb, (half + 1) * tb)
        cur, oth = half, 1 - half
        z_cur = {name: ref.at[cur] for name, ref in zip(_INPROJ_Z, z_s)}
        z_oth = {name: ref.at[oth] for name, ref in zip(_INPROJ_Z, z_s)}
        out = {"a": ak_ref.at[rows, 0:D_GMLP], "k": ak_ref.at[rows, D_GMLP:],
               "qt": qvt_ref.at[0:D_MLSTM, rows], "vt": qvt_ref.at[D_MLSTM:, rows],
               "o": o_ref.at[rows, :], "gr": gr_ref.at[:, rows]}
        if project:
            _inproj_project(hb_s.at[oth], w, z_oth)
        if norm:
            _inproj_norm(x_ref[rows, :], mod_ref[pl.ds(mod_in, 1), :], g1_ref[...], hb_s.at[cur])
        if finish:
            _inproj_finish(z_cur, c, out, seg)
        if half == 0:
            for src, dst in zip(cast_in, cast_out):
                dst[...] = src[...].astype(BF16)
            if ada_tail:
                _ada_kernel(*ada_in, ada_out)

    _pipeline_step(tick)


def _inproj(x, mod, mod_row0, rows_per_mod, seg, p, cast=(), ada_tail=None):
    n = x.shape[0]
    tb = TOKEN_BLOCK
    tb2 = 2 * tb
    assert tb % seg == 0 and n % tb2 == 0 and rows_per_mod % tb2 == 0
    nb = n // tb2
    blk_in = lambda i: jnp.minimum(i, nb - 1)
    blk_out = lambda i: jnp.maximum(i - 1, 0)
    tok = lambda width: pl.BlockSpec((tb2, width), lambda i: (blk_out(i), 0))
    tok_t = lambda height: pl.BlockSpec((height, tb2), lambda i: (0, blk_out(i)))
    in_specs = [
        pl.BlockSpec((tb2, D_MODEL), lambda i: (blk_in(i), 0)),
        _const_spec(mod.shape),
        _const_spec((1, D_MODEL)),
        _const_spec((D_IN, D_MODEL)),
        _const_spec((N_GATE, 1)),
        _const_spec((3, 2 * D_MLSTM)),
        _const_spec((1, 2 * D_MLSTM)),
        _const_spec((GROUPS, CHUNK, CHUNK)),
        _const_spec((CHUNK, D_GMLP)),
        _const_spec((1, D_GMLP)),
    ]
    out_shape = [
        jax.ShapeDtypeStruct((n, D_GMLP + D_MLSTM), BF16),
        jax.ShapeDtypeStruct((2 * D_MLSTM, n), BF16),
        jax.ShapeDtypeStruct((n, D_MLSTM), F32),
        jax.ShapeDtypeStruct((GR_ROWS, n), F32),
    ]
    out_specs = [tok(D_GMLP + D_MLSTM), tok_t(2 * D_MLSTM), tok(D_MLSTM), tok_t(GR_ROWS)]
    for wf in cast:
        rows = wf.shape[0] // nb
        assert rows * nb == wf.shape[0] and rows % 16 == 0
        slab = pl.BlockSpec((rows, wf.shape[1]), lambda i: (blk_in(i), 0))
        in_specs.append(slab)
        out_specs.append(slab)
        out_shape.append(jax.ShapeDtypeStruct(wf.shape, BF16))
    side = list(cast)
    if ada_tail is not None:
        cs, w_ada, b_ada, col0 = ada_tail
        width = (w_ada.shape[1] - col0) // nb
        assert col0 % width == 0 and width * nb == w_ada.shape[1] - col0 and width % 128 == 0
        in_specs += [
            _const_spec(cs.shape),
            pl.BlockSpec((w_ada.shape[0], width), lambda i: (0, col0 // width + blk_in(i))),
            pl.BlockSpec((1, width), lambda i: (0, col0 // width + blk_in(i))),
        ]
        out_specs.append(pl.BlockSpec((cs.shape[0], width), lambda i: (0, blk_in(i))))
        out_shape.append(jax.ShapeDtypeStruct((cs.shape[0], w_ada.shape[1] - col0), F32))
        side += [cs, w_ada, b_ada]
    scratch_shapes = [
        pltpu.VMEM((2, tb, D_MODEL), BF16),
        pltpu.VMEM((2, tb, D_GMLP), F32),
        pltpu.VMEM((2, tb, D_GMLP), F32),
        pltpu.VMEM((2, tb, 2 * D_MLSTM), F32),
        pltpu.VMEM((2, tb, D_MLSTM), F32),
        pltpu.VMEM((2, D_MLSTM, tb), BF16),
        pltpu.VMEM((2, N_GATE, tb), F32),
    ]
    return pl.pallas_call(
        functools.partial(_inproj_kernel, seg=seg, blocks_per_mod=rows_per_mod // tb2,
                          mod_row0=mod_row0, n_cast=len(cast), ada_tail=ada_tail is not None),
        out_shape=out_shape,
        grid=(nb + 1,),
        in_specs=in_specs,
        out_specs=out_specs,
        scratch_shapes=scratch_shapes,
        compiler_params=pltpu.CompilerParams(
            dimension_semantics=("arbitrary",), vmem_limit_bytes=VMEM_LIMIT_BYTES),
        name="inproj",
    )(x, mod, p["g1"], p["wit"],
      p["bgt"], p["cw"], p["cb"], p["ws"], p["bs"], p["gv"], *side)


def _mlstm_unit(k, vt, kq, qs, d_row, dmax_row, b_row, g_row, bl_row, gmax_row, st, m_row, mask,
                reverse, carried):
    nt = kq.shape[0] // HD
    a = b_row + m_row
    mj = jnp.maximum(a, dmax_row + b_row)
    d_col = [jnp.transpose(jnp.broadcast_to(d_row[:, t * HD:(t + 1) * HD], (HD, HD)))
             for t in range(nt)]
    s_tiles = [[jnp.zeros((HD, HD), BF16)] * nt for _ in range(nt)]
    den_tiles = []
    for jt in range(nt):
        js = slice(jt * HD, (jt + 1) * HD)
        den_j = None
        for t in range(nt):
            if (t > jt) if not reverse else (t < jt):
                continue
            logw = d_col[t] + b_row[:, js]
            if t == jt:
                logw = jnp.where(mask, logw, NEG)
            s = kq[t * HD:(t + 1) * HD, js] * jnp.exp2(logw - mj[:, js])
            s_tiles[t][jt] = s.astype(BF16)
            part = jnp.sum(s, axis=0, keepdims=True)
            den_j = part if den_j is None else den_j + part
        den_tiles.append(den_j)
    num = _dot(vt, jnp.concatenate([jnp.concatenate(row, axis=1) for row in s_tiles], axis=0))
    den = jnp.concatenate(den_tiles, axis=1)
    if carried:
        inter = jnp.exp2(a - mj)
        num = num + inter * qs[:HD]
        den = den + inter * qs[HD:HD + 1]
    h = num * (1.0 / jnp.maximum(jnp.abs(den), jnp.exp2(-mj)))
    m_new = jnp.maximum(bl_row + m_row, gmax_row)
    wc = jnp.exp2(g_row - m_new)
    pad_row = lax.broadcasted_iota(jnp.int32, (ST_ROWS - HD, wc.shape[1]), 0)
    v_aug = jnp.concatenate(
        [vt.astype(F32) * wc, jnp.where(pad_row == 0, wc, 0.0)], axis=0).astype(BF16)
    st_new = _dot(v_aug, k)
    if carried:
        decay = jnp.exp2(bl_row + m_row - m_new)
        st_new = decay[:, :HD] * st + st_new
    return h, st_new, m_new


def _mlstm_kernel(*refs, has_init, has_out, rounds, by_sequence):
    if has_init:
        s0_ref, m0_ref = refs[:2]
        refs = refs[2:]
    def split(k_ref, qvt_ref, gr_ref):
        return k_ref, qvt_ref.at[0:D_MLSTM, :], qvt_ref.at[D_MLSTM:, :], gr_ref

    if by_sequence:
        fwd_in = bwd_in = split(*refs[:3])
        refs = refs[3:]
    else:
        fwd_in, bwd_in = split(*refs[:3]), split(*refs[3:6])
        refs = refs[6:]
    hf_ref, hb_ref = refs[:2]
    refs = refs[2:]
    if has_out:
        co_ref, no_ref, mo_ref = refs[:3]
        refs = refs[3:]
    s_ref, m_ref = refs
    c = pl.program_id(1)
    nc = pl.num_programs(1)
    carried = not by_sequence

    if carried:
        @pl.when(c == 0)
        def _():
            if has_init:
                s_ref[...] = s0_ref[0]
                m_ref[...] = m0_ref[0]
            else:
                s_ref[...] = jnp.zeros_like(s_ref)
                m_ref[...] = jnp.zeros_like(m_ref)

    si = lax.broadcasted_iota(jnp.int32, (HD, HD), 0)
    ji = lax.broadcasted_iota(jnp.int32, (HD, HD), 1)
    dirs = ((fwd_in, hf_ref, si <= ji), (bwd_in, hb_ref, si >= ji))

    def chunk_cols(d, r):
        pos = rounds - 1 - r if (d == 1 and not by_sequence) else r
        return slice(pos * SCAN, (pos + 1) * SCAN)

    kq = {}
    for r in range(rounds):
        for d, ((k_ref, qt_ref, _, _), _, _) in enumerate(dirs):
            cs = chunk_cols(d, r)
            for hd in range(HEADS):
                hs = slice(hd * HD, (hd + 1) * HD)
                kq[r, d, hd] = _dot(k_ref[cs, hs], qt_ref[hs, cs])

    m_cur = None
    if carried:
        m_all = m_ref[...]
        m_cur = [m_all[u:u + 1] for u in range(N_UNIT)]
    zero_row = jnp.zeros((1, SCAN), F32)
    for r in range(rounds):
        slot = r * N_UNIT if by_sequence else 0
        for d, ((k_ref, qt_ref, vt_ref, gr_ref), h_ref, mask) in enumerate(dirs):
            cs = chunk_cols(d, r)
            for hd in range(HEADS):
                u = d * HEADS + hd
                row = d * 2 * HEADS + HEADS + hd
                hs = slice(hd * HD, (hd + 1) * HD)
                st = s_ref[slot + u] if carried else None
                qs = _dot(st.astype(BF16), qt_ref[hs, cs]) if carried else None
                h, st_new, m_new = _mlstm_unit(
                    k_ref[cs, hs], vt_ref[hs, cs], kq[r, d, hd], qs,
                    *(gr_ref[sec * N_GATE + row:sec * N_GATE + row + 1, cs]
                      for sec in (4, 5, 0, 1, 2, 3)),
                    st, m_cur[u] if carried else zero_row, mask, d == 1, carried)
                h_ref[hs, cs] = h
                s_ref[slot + u] = st_new
                if carried:
                    m_cur[u] = m_new
                else:
                    m_ref[slot + u:slot + u + 1, :] = m_new
    if carried:
        m_ref[...] = jnp.concatenate(m_cur, axis=0)

    if has_out:
        @pl.when(c == nc - 1)
        def _():
            for j in range(s_ref.shape[0]):
                q, u = divmod(j, N_UNIT)
                co_ref[q, u] = jnp.transpose(s_ref[j, 0:HD, :])
                no_ref[q, u:u + 1, :] = s_ref[j, HD:HD + 1, :]
                mo_ref[q, u:u + 1, :] = m_ref[j:j + 1, :]


def _mlstm(ak, qvt, gr, batch, rounds, s0=None, m0=None, want_state=False):
    n = ak.shape[0]
    nc = n // batch // SCAN
    by_sequence = nc == 1
    width = rounds * SCAN
    has_init = s0 is not None
    if by_sequence:
        assert batch % rounds == 0 and not has_init
        grid = (batch // rounds, 1)
        slots = rounds
        fwd = bwd = lambda b, c: b
    else:
        assert nc % rounds == 0 and not want_state
        steps = nc // rounds
        grid = (batch, steps)
        slots = 1
        fwd = lambda b, c: b * steps + c
        bwd = lambda b, c: b * steps + steps - 1 - c

    def specs(ix):
        return [
            pl.BlockSpec((width, D_MLSTM), lambda b, c: (ix(b, c), 1)),
            pl.BlockSpec((2 * D_MLSTM, width), lambda b, c: (0, ix(b, c))),
            pl.BlockSpec((GR_ROWS, width), lambda b, c: (0, ix(b, c))),
        ]

    in_specs = specs(fwd) if by_sequence else specs(fwd) + specs(bwd)
    args = [ak, qvt, gr] if by_sequence else [ak, qvt, gr, ak, qvt, gr]
    if has_init:
        in_specs = [
            pl.BlockSpec((1, N_UNIT, ST_ROWS, HD), lambda b, c: (b, 0, 0, 0)),
            pl.BlockSpec((1, N_UNIT, SCAN), lambda b, c: (b, 0, 0)),
        ] + in_specs
        args = [s0, m0] + args
    out_shape = [
        jax.ShapeDtypeStruct((D_MLSTM, n), F32),
        jax.ShapeDtypeStruct((D_MLSTM, n), F32),
    ]
    out_specs = [
        pl.BlockSpec((D_MLSTM, width), lambda b, c: (0, fwd(b, c))),
        pl.BlockSpec((D_MLSTM, width), lambda b, c: (0, bwd(b, c))),
    ]
    if want_state:
        out_shape += [
            jax.ShapeDtypeStruct((batch, N_UNIT, HD, HD), F32),
            jax.ShapeDtypeStruct((batch, N_UNIT, HD), F32),
            jax.ShapeDtypeStruct((batch, N_UNIT, SCAN), F32),
        ]
        out_specs += [
            pl.BlockSpec((slots, N_UNIT, HD, HD), lambda b, c: (b, 0, 0, 0)),
            pl.BlockSpec((slots, N_UNIT, HD), lambda b, c: (b, 0, 0)),
            pl.BlockSpec((slots, N_UNIT, SCAN), lambda b, c: (b, 0, 0)),
        ]
    return pl.pallas_call(
        functools.partial(_mlstm_kernel, has_init=has_init, has_out=want_state, rounds=rounds,
                          by_sequence=by_sequence),
        out_shape=out_shape,
        grid=grid,
        in_specs=in_specs,
        out_specs=out_specs,
        scratch_shapes=[pltpu.VMEM((slots * N_UNIT, ST_ROWS, HD), F32),
                        pltpu.VMEM((slots * N_UNIT, SCAN), F32)],
        compiler_params=pltpu.CompilerParams(
            dimension_semantics=("arbitrary", "arbitrary"), vmem_limit_bytes=VMEM_LIMIT_BYTES),
        name="mlstm",
    )(*args)


def _outffn_mix(x, a, hft, hbt, o, mod, c, x1_ref, h2_ref):
    ga1 = mod[:, 0:D_MODEL]
    sh2 = mod[:, D_MODEL:2 * D_MODEL]
    sc2 = mod[:, 2 * D_MODEL:3 * D_MODEL]
    hs = jnp.transpose(hft + hbt)
    sig = _sigmoid(o)
    parts = []
    for hd in range(HEADS):
        sl = slice(hd * HD, (hd + 1) * HD)
        parts.append(_rms(hs[:, sl], c["gh"][:, sl]) * sig[:, sl])
    b_out = jnp.concatenate(parts, axis=-1).astype(BF16)
    mix = _dot(a, c["wout"][0:D_GMLP, :]) + _dot(b_out, c["wout"][D_GMLP:, :])
    x1 = x + ga1 * mix
    x1_ref[...] = x1
    h2_ref[...] = (_rms(x1, c["g2"][...] * (1.0 + sc2)) + sh2).astype(BF16)


def _outffn_up(h2_ref, c, f_ref):
    h2 = h2_ref[...]
    u = _dot(h2, c["w1"][...])
    g = _dot(h2, c["w3"][...])
    f_ref[...] = (u * _sigmoid(u) * g).astype(BF16)


def _outffn_down(f_ref, x1_ref, mod, c, y_ref):
    ga2 = mod[:, 3 * D_MODEL:4 * D_MODEL]
    x2 = x1_ref[...] + ga2 * _dot(f_ref[...], c["w2"][...])
    y_ref[...] = _rms(x2, c["gf"][...])


def _outffn_kernel(x_ref, a_ref, hft_ref, hbt_ref, o_ref, mod_ref, gh_ref, wout_hbm,
                   g2_ref, w1_hbm, w3_hbm, w2_hbm, gf_ref, y_ref,
                   x1_s, h2_s, f_s, wout_s, w1_s, w3_s, w2_s, w_sem,
                   *, blocks_per_mod, mod_row0):
    tb = x_ref.shape[0] // 2
    i = pl.program_id(0)
    mod_in = mod_row0 + jnp.minimum(i, pl.num_programs(0) - 2) // blocks_per_mod
    mod_out = mod_row0 + jnp.maximum(i - 1, 0) // blocks_per_mod
    c = {"gh": gh_ref, "wout": wout_s, "g2": g2_ref, "w1": w1_s, "w3": w3_s, "w2": w2_s,
         "gf": gf_ref}
    weight_copy = {
        name: pltpu.make_async_copy(src, dst, w_sem.at[j])
        for j, (name, src, dst) in enumerate((("wout", wout_hbm, wout_s), ("w1", w1_hbm, w1_s),
                                              ("w3", w3_hbm, w3_s), ("w2", w2_hbm, w2_s)))}

    def tick(half, mix, up, down):
        rows = slice(half * tb, (half + 1) * tb)
        cur, oth = half, 1 - half
        first_tick = mix and not up and not down
        second_tick = mix and up and not down
        if first_tick:
            for name in ("wout", "w1", "w3", "w2"):
                weight_copy[name].start()
            weight_copy["wout"].wait()
        if down:
            _outffn_down(f_s.at[cur], x1_s.at[cur], mod_ref[pl.ds(mod_out, 1), :], c,
                         y_ref.at[rows, :])
        if mix:
            _outffn_mix(x_ref[rows, :], a_ref[rows, :], hft_ref[:, rows], hbt_ref[:, rows],
                        o_ref[rows, :], mod_ref[pl.ds(mod_in, 1), :], c, x1_s.at[cur],
                        h2_s.at[cur])
        if second_tick:
            weight_copy["w1"].wait()
            weight_copy["w3"].wait()
        if up:
            _outffn_up(h2_s.at[oth], c, f_s.at[oth])
        if second_tick:
            weight_copy["w2"].wait()

    _pipeline_step(tick)


def _outffn(x, ak, hft, hbt, o, mod, mod_row0, rows_per_mod, p):
    n = x.shape[0]
    tb = TOKEN_BLOCK
    tb2 = 2 * tb
    assert n % tb2 == 0 and rows_per_mod % tb2 == 0
    nb = n // tb2
    d_ff = p["w1"].shape[1]
    blk_in = lambda i: jnp.minimum(i, nb - 1)
    blk_out = lambda i: jnp.maximum(i - 1, 0)
    tok = lambda w: pl.BlockSpec((tb2, w), lambda i: (blk_in(i), 0))
    tok_t = lambda h: pl.BlockSpec((h, tb2), lambda i: (0, blk_in(i)))
    weight_hbm = pl.BlockSpec(memory_space=pl.ANY)
    in_specs = [
        tok(D_MODEL), tok(D_GMLP), tok_t(D_MLSTM), tok_t(D_MLSTM), tok(D_MLSTM),
        _const_spec(mod.shape),
        _const_spec((1, D_MLSTM)),
        weight_hbm,
        _const_spec((1, D_MODEL)),
        weight_hbm,
        weight_hbm,
        weight_hbm,
        _const_spec((1, D_MODEL)),
    ]
    scratch_shapes = [
        pltpu.VMEM((2, tb, D_MODEL), F32),
        pltpu.VMEM((2, tb, D_MODEL), BF16),
        pltpu.VMEM((2, tb, d_ff), BF16),
        pltpu.VMEM(p["wout"].shape, BF16),
        pltpu.VMEM(p["w1"].shape, BF16),
        pltpu.VMEM(p["w3"].shape, BF16),
        pltpu.VMEM(p["w2"].shape, BF16),
        pltpu.SemaphoreType.DMA((4,)),
    ]
    return pl.pallas_call(
        functools.partial(_outffn_kernel, blocks_per_mod=rows_per_mod // tb2, mod_row0=mod_row0),
        out_shape=jax.ShapeDtypeStruct((n, D_MODEL), F32),
        grid=(nb + 1,),
        in_specs=in_specs,
        out_specs=pl.BlockSpec((tb2, D_MODEL), lambda i: (blk_out(i), 0)),
        scratch_shapes=scratch_shapes,
        compiler_params=pltpu.CompilerParams(
            dimension_semantics=("arbitrary",), vmem_limit_bytes=VMEM_LIMIT_BYTES),
        name="outffn",
    )(x, ak, hft, hbt, o, mod, p["gh"], p["wout"], p["g2"], p["w1"], p["w3"], p["w2"], p["gf"])


def _layer_params(l, g_norm1, b_gate, w_s, b_s, g_v, conv_w, conv_b, g_h, g_norm2, g_final):
    return {
        "g1": g_norm1[l][None, :],
        "bgt": b_gate[l][:, None],
        "cw": conv_w[l], "cb": conv_b[l][None, :],
        "ws": w_s[l].astype(BF16),
        "bs": jnp.repeat(b_s[l].T, HD, axis=1),
        "gv": g_v[l].reshape(1, D_GMLP),
        "gh": g_h[l].reshape(1, D_MLSTM),
        "g2": g_norm2[l][None, :],
        "gf": g_final[None, :],
    }


_LATE_WEIGHTS = ("wout", "w1", "w3", "w2")


def _trunk_front(x, mod_row0, rows_per_mod, seg, p, late=None):
    cast, ada_tail = ((), None) if late is None else (
        tuple(late[0][name] for name in _LATE_WEIGHTS), late[1])
    ak, qvt, o, gr, *side = _inproj(x, p["mod_head"], mod_row0, rows_per_mod, seg, p, cast,
                                    ada_tail)
    if late is not None:
        p = dict(p, **dict(zip(_LATE_WEIGHTS, side[:-1])))
        p["mod_tail"] = side[-1]
    return (ak, qvt, o), gr, p


def _trunk_back(x, front, gr, mod_row0, rows_per_mod, batch, rounds, p, s0=None, m0=None,
                want_state=False):
    ak, qvt, o = front
    hft, hbt, *state = _mlstm(ak, qvt, gr, batch, rounds, s0, m0, want_state)
    y = _outffn(x, ak, hft, hbt, o, p["mod_tail"], mod_row0, rows_per_mod, p)
    return y, state


def kernel(x_prompt, x_sample, state_C, state_n, state_m, c, c_ctx, w_ada, b_ada, g_norm1, w_in,
           b_gate, w_s, b_s, g_v, conv_w, conv_b, g_h, w_out, g_norm2, w1, w3, w2, g_final):
    bp, tp, d = x_prompt.shape
    bs_, ts, _ = x_sample.shape
    depth = w_in.shape[0]
    assert depth == 1, "final norm is fused into the layer's last kernel"
    xp = x_prompt.reshape(bp * tp, d)
    xs = x_sample.reshape(bs_ * ts, d)

    cs = jnp.zeros((8, d), F32).at[0].set(c_ctx).at[1:1 + bs_].set(c)
    new_c, new_n, new_m = [], [], []
    for l in range(depth):
        p = _layer_params(l, g_norm1, b_gate, w_s, b_s, g_v, conv_w, conv_b, g_h, g_norm2, g_final)
        late_f32 = {"wout": w_out[l], "w1": w1[l], "w3": w3[l], "w2": w2[l]}
        p["mod_head"], p["wit"] = _ada(cs, w_ada[l], b_ada[l][None, :], 2 * d, w_in[l].T)
        ada_tail = (cs, w_ada[l], b_ada[l][None, :], 2 * d)

        front_lat, gr_lat, p = _trunk_front(xs, 1, ts, GRID_W, p, (late_f32, ada_tail))
        front_ctx, gr_ctx, _ = _trunk_front(xp, 0, bp * tp, tp, p)

        xp, (c_ctx_out, n_ctx_out, m_ctx_out) = _trunk_back(
            xp, front_ctx, gr_ctx, 0, bp * tp, bp, CTX_SEQS_PER_STEP, p, want_state=True)
        new_c.append(c_ctx_out.reshape(bp, N_DIR, HEADS, HD, HD))
        new_n.append(n_ctx_out.reshape(bp, N_DIR, HEADS, HD))
        new_m.append((m_ctx_out[..., 0] * (1.0 / LOG2E)).reshape(bp, N_DIR, HEADS))

        s0 = jnp.concatenate(
            [jnp.swapaxes(state_C[:, l], -1, -2), state_n[:, l][..., None, :],
             jnp.zeros((bs_, N_DIR, HEADS, ST_ROWS - HD - 1, HD), F32)],
            axis=-2).reshape(bs_, N_UNIT, ST_ROWS, HD)
        m0 = jnp.broadcast_to((state_m[:, l] * LOG2E).reshape(bs_, N_UNIT, 1),
                              (bs_, N_UNIT, SCAN))
        xs, _ = _trunk_back(xs, front_lat, gr_lat, 1, ts, bs_, LAT_CHUNKS_PER_STEP, p, s0=s0, m0=m0)

    return (xp.reshape(bp, tp, d), xs.reshape(bs_, ts, d),
            jnp.stack(new_c, axis=1), jnp.stack(new_n, axis=1), jnp.stack(new_m, axis=1))
```

```python
import functools

import jax
import jax.numpy as jnp
from jax import lax
from jax.experimental import pallas as pl
from jax.experimental.pallas import tpu as pltpu

D_MODEL = 1024
D_GMLP = 512
D_MLSTM = 512
GROUPS = 4
HEADS = 4
HD = 128
CHUNK = 128
SCAN = 256
N_DIR = 2
N_UNIT = N_DIR * HEADS
GRID_W = 64
OFF_V, OFF_Q, OFF_VV, OFF_O, OFF_G = 512, 1024, 2048, 2560, 3072
D_IN = OFF_G + 2 * N_DIR * HEADS
EPS = 1e-6
NEG = -1e30
ST_ROWS = HD + 16
N_GATE = 2 * N_UNIT
GR_ROWS = 6 * N_GATE

TOKEN_BLOCK = 256
CTX_SEQS_PER_STEP = 2
LAT_CHUNKS_PER_STEP = 2
VMEM_LIMIT_BYTES = 56 * 1024 * 1024

F32 = jnp.float32
BF16 = jnp.bfloat16


def _rms(x, g):
    n = x.shape[-1]
    return x * lax.rsqrt(jnp.sum(x * x, axis=-1, keepdims=True) + n * EPS) * (g * n ** 0.5)


LOG2E = 1.4426950408889634
GELU_K = 0.7978845608028654
GELU_C = 0.044715


def _sigmoid(x):
    return 1.0 / (1.0 + jnp.exp2(x * (-LOG2E)))


def _gelu_tanh(x):
    t = (x * x) * (-2.0 * LOG2E * GELU_K * GELU_C) + (-2.0 * LOG2E * GELU_K)
    return x * (1.0 / (1.0 + jnp.exp2(x * t)))


def _log_sigmoid(x):
    return jnp.minimum(x, 0.0) - jnp.log(1.0 + jnp.exp(-jnp.abs(x)))


def _dot(a, b):
    return jnp.dot(a, b, preferred_element_type=F32)


def _dot_nt(a, b):
    return lax.dot_general(a, b, (((1,), (1,)), ((), ())), preferred_element_type=F32)


def _dot_exact(a, b):
    return jnp.dot(a, b, preferred_element_type=F32, precision=lax.Precision.HIGHEST)


def _pipeline_step(tick):
    i = pl.program_id(0)
    last = pl.num_programs(0) - 1

    @pl.when(i == 0)
    def _():
        tick(0, True, False, False)
        tick(1, True, True, False)

    @pl.when(jnp.logical_and(i > 0, i < last))
    def _():
        tick(0, True, True, True)
        tick(1, True, True, True)

    @pl.when(i == last)
    def _():
        tick(0, False, True, True)
        tick(1, False, False, True)


def _const_spec(shape):
    zeros = (0,) * len(shape)
    return pl.BlockSpec(shape, lambda *_: zeros, pipeline_mode=pl.Buffered(1))


def _ada_kernel(c_ref, w_ref, b_ref, o_ref):
    c = c_ref[...]
    s = (c * _sigmoid(c)).astype(BF16)
    o_ref[...] = _dot(s, w_ref[...].astype(BF16)) + b_ref[...]


def _ada_and_round_kernel(c_ref, w_ref, b_ref, wf_ref, o_ref, wb_ref):
    _ada_kernel(c_ref, w_ref, b_ref, o_ref)

    @pl.when(pl.program_id(0) == 0)
    def _():
        wb_ref[...] = wf_ref[...].astype(BF16)


def _ada(cs, w_ada, b_ada, n, w_f32):
    rows, d = cs.shape
    tn = 1024
    return pl.pallas_call(
        _ada_and_round_kernel,
        out_shape=[jax.ShapeDtypeStruct((rows, n), F32), jax.ShapeDtypeStruct(w_f32.shape, BF16)],
        grid=(n // tn,),
        in_specs=[
            pl.BlockSpec((rows, d), lambda j: (0, 0)),
            pl.BlockSpec((d, tn), lambda j: (0, j)),
            pl.BlockSpec((1, tn), lambda j: (0, j)),
            _const_spec(w_f32.shape),
        ],
        out_specs=[pl.BlockSpec((rows, tn), lambda j: (0, j)),
                   pl.BlockSpec(w_f32.shape, lambda j: (0, 0))],
        compiler_params=pltpu.CompilerParams(
            dimension_semantics=("arbitrary",), vmem_limit_bytes=VMEM_LIMIT_BYTES),
        name="ada",
    )(cs, w_ada, b_ada, w_f32)


def _inproj_norm(x, mod, g1, hb_ref):
    sh1 = mod[:, 0:D_MODEL]
    sc1 = mod[:, D_MODEL:2 * D_MODEL]
    hb_ref[...] = (_rms(x, g1 * (1.0 + sc1)) + sh1).astype(BF16)


def _inproj_project(hb_ref, w, z):
    hb = hb_ref[...]
    z["u"][...] = _dot_nt(hb, w["u"][...])
    z["v"][...] = _dot_nt(hb, w["v"][...])
    z["qk"][...] = _dot_nt(hb, w["qk"][...])
    z["o"][...] = _dot_nt(hb, w["o"][...])
    z["vt"][...] = _dot_nt(w["vv"][...], hb).astype(BF16)
    z["gt"][...] = _dot_nt(w["gt"][...], hb)


def _inproj_finish(z, c, out, seg):
    tb = z["u"].shape[0]
    for g in range(GROUPS):
        gs = slice(g * HD, (g + 1) * HD)
        vg = _rms(_gelu_tanh(z["v"][:, gs]), c["gv"][:, gs]).astype(BF16)
        for ch in range(tb // CHUNK):
            cs = slice(ch * CHUNK, (ch + 1) * CHUNK)
            mixed = _dot(c["ws"][g], vg[cs]) + c["bs"][:, gs]
            out["a"][cs, gs] = (_gelu_tanh(z["u"][cs, gs]) * mixed).astype(BF16)

    zqk = z["qk"][...]
    pos = lax.broadcasted_iota(jnp.int32, (tb, 1), 0) % seg
    prev = jnp.where(pos != 0, pltpu.roll(zqk, 1, 0), 0.0)
    nxt = jnp.where(pos != seg - 1, pltpu.roll(zqk, tb - 1, 0), 0.0)
    cw = c["cw"]
    y = c["cb"][...] + prev * cw[0:1, :] + zqk * cw[1:2, :] + nxt * cw[2:3, :]
    y = y * _sigmoid(y)
    out["qt"][...] = jnp.transpose(y[:, :D_MLSTM]).astype(BF16)
    out["k"][...] = (y[:, D_MLSTM:] * (HD ** -0.5)).astype(BF16)
    out["vt"][...] = z["vt"][...]
    out["o"][...] = z["o"][...]

    assert tb == SCAN
    gt = z["gt"][...] + c["bgt"][...]
    gi = pltpu.roll(gt, HEADS, 0)
    lf = _log_sigmoid(gt)
    lane = lax.broadcasted_iota(jnp.int32, (N_GATE, HD), 1)
    tiles = []
    carry = jnp.zeros((N_GATE, HD), F32)
    for t in range(SCAN // HD):
        prefix = lf[:, t * HD:(t + 1) * HD]
        shift = 1
        while shift < HD:
            prefix = prefix + jnp.where(lane >= shift, pltpu.roll(prefix, shift, 1), 0.0)
            shift *= 2
        tiles.append(prefix + carry)
        carry = carry + jnp.broadcast_to(prefix[:, HD - 1:HD], (N_GATE, HD))
    prefix = jnp.concatenate(tiles, axis=1)
    b_last = jnp.concatenate([carry] * (SCAN // HD), axis=1)
    row_bwd = lax.broadcasted_iota(jnp.int32, (N_GATE, SCAN), 0) >= N_GATE // N_DIR
    b_row = jnp.where(row_bwd, b_last - prefix + lf, prefix)
    g_row = b_last - b_row + gi
    g_max = jnp.broadcast_to(jnp.max(g_row, axis=1, keepdims=True), (N_GATE, SCAN))
    d_row = (gi - b_row) * LOG2E
    lane = lax.broadcasted_iota(jnp.int32, (N_GATE, SCAN), 1)
    d_upto, d_from = d_row, d_row
    shift = 1
    while shift < SCAN:
        d_upto = jnp.maximum(d_upto, jnp.where(lane >= shift, pltpu.roll(d_upto, shift, 1), NEG))
        d_from = jnp.maximum(
            d_from, jnp.where(lane < SCAN - shift, pltpu.roll(d_from, SCAN - shift, 1), NEG))
        shift *= 2
    d_max = jnp.where(row_bwd, d_from, d_upto)
    for sec, stat in enumerate((b_row, g_row, b_last, g_max)):
        out["gr"][sec * N_GATE:(sec + 1) * N_GATE, :] = stat * LOG2E
    out["gr"][4 * N_GATE:5 * N_GATE, :] = d_row
    out["gr"][5 * N_GATE:6 * N_GATE, :] = d_max


_INPROJ_Z = ("u", "v", "qk", "o", "vt", "gt")


def _inproj_kernel(*refs, seg, blocks_per_mod, mod_row0, n_cast, ada_tail):
    refs = list(refs)
    take = lambda k: [refs.pop(0) for _ in range(k)]
    (x_ref, mod_ref, g1_ref, wit_ref,
     bgt_ref, cw_ref, cb_ref, ws_ref, bs_ref, gv_ref) = take(10)
    cast_in = take(n_cast)
    ada_in = take(3) if ada_tail else None
    ak_ref, qvt_ref, o_ref, gr_ref = take(4)
    cast_out = take(n_cast)
    ada_out = take(1)[0] if ada_tail else None
    hb_s, *z_s = refs
    tb = x_ref.shape[0] // 2
    mod_in = mod_row0 + jnp.minimum(pl.program_id(0), pl.num_programs(0) - 2) // blocks_per_mod
    w = {"u": wit_ref.at[0:OFF_V, :], "v": wit_ref.at[OFF_V:OFF_Q, :],
         "qk": wit_ref.at[OFF_Q:OFF_VV, :], "vv": wit_ref.at[OFF_VV:OFF_O, :],
         "o": wit_ref.at[OFF_O:OFF_G, :], "gt": wit_ref.at[OFF_G:D_IN, :]}
    c = {"gv": gv_ref, "ws": ws_ref, "bs": bs_ref, "cw": cw_ref, "cb": cb_ref, "bgt": bgt_ref}

    def tick(half, norm, project, finish):
        rows = slice(half * tb, (half + 1) * tb)
        cur, oth = half, 1 - half
        z_cur = {name: ref.at[cur] for name, ref in zip(_INPROJ_Z, z_s)}
        z_oth = {name: ref.at[oth] for name, ref in zip(_INPROJ_Z, z_s)}
        out = {"a": ak_ref.at[rows, 0:D_GMLP], "k": ak_ref.at[rows, D_GMLP:],
               "qt": qvt_ref.at[0:D_MLSTM, rows], "vt": qvt_ref.at[D_MLSTM:, rows],
               "o": o_ref.at[rows, :], "gr": gr_ref.at[:, rows]}
        if project:
            _inproj_project(hb_s.at[oth], w, z_oth)
        if norm:
            _inproj_norm(x_ref[rows, :], mod_ref[pl.ds(mod_in, 1), :], g1_ref[...], hb_s.at[cur])
        if finish:
            _inproj_finish(z_cur, c, out, seg)
        if half == 0:
            for src, dst in zip(cast_in, cast_out):
                dst[...] = src[...].astype(BF16)
            if ada_tail:
                _ada_kernel(*ada_in, ada_out)

    _pipeline_step(tick)


def _inproj(x, mod, mod_row0, rows_per_mod, seg, p, cast=(), ada_tail=None):
    n = x.shape[0]
    tb = TOKEN_BLOCK
    tb2 = 2 * tb
    assert tb % seg == 0 and n % tb2 == 0 and rows_per_mod % tb2 == 0
    nb = n // tb2
    blk_in = lambda i: jnp.minimum(i, nb - 1)
    blk_out = lambda i: jnp.maximum(i - 1, 0)
    tok = lambda width: pl.BlockSpec((tb2, width), lambda i: (blk_out(i), 0))
    tok_t = lambda height: pl.BlockSpec((height, tb2), lambda i: (0, blk_out(i)))
    in_specs = [
        pl.BlockSpec((tb2, D_MODEL), lambda i: (blk_in(i), 0)),
        _const_spec(mod.shape),
        _const_spec((1, D_MODEL)),
        _const_spec((D_IN, D_MODEL)),
        _const_spec((N_GATE, 1)),
        _const_spec((3, 2 * D_MLSTM)),
        _const_spec((1, 2 * D_MLSTM)),
        _const_spec((GROUPS, CHUNK, CHUNK)),
        _const_spec((CHUNK, D_GMLP)),
        _const_spec((1, D_GMLP)),
    ]
    out_shape = [
        jax.ShapeDtypeStruct((n, D_GMLP + D_MLSTM), BF16),
        jax.ShapeDtypeStruct((2 * D_MLSTM, n), BF16),
        jax.ShapeDtypeStruct((n, D_MLSTM), F32),
        jax.ShapeDtypeStruct((GR_ROWS, n), F32),
    ]
    out_specs = [tok(D_GMLP + D_MLSTM), tok_t(2 * D_MLSTM), tok(D_MLSTM), tok_t(GR_ROWS)]
    for wf in cast:
        rows = wf.shape[0] // nb
        assert rows * nb == wf.shape[0] and rows % 16 == 0
        slab = pl.BlockSpec((rows, wf.shape[1]), lambda i: (blk_in(i), 0))
        in_specs.append(slab)
        out_specs.append(slab)
        out_shape.append(jax.ShapeDtypeStruct(wf.shape, BF16))
    side = list(cast)
    if ada_tail is not None:
        cs, w_ada, b_ada, col0 = ada_tail
        width = (w_ada.shape[1] - col0) // nb
        assert col0 % width == 0 and width * nb == w_ada.shape[1] - col0 and width % 128 == 0
        in_specs += [
            _const_spec(cs.shape),
            pl.BlockSpec((w_ada.shape[0], width), lambda i: (0, col0 // width + blk_in(i))),
            pl.BlockSpec((1, width), lambda i: (0, col0 // width + blk_in(i))),
        ]
        out_specs.append(pl.BlockSpec((cs.shape[0], width), lambda i: (0, blk_in(i))))
        out_shape.append(jax.ShapeDtypeStruct((cs.shape[0], w_ada.shape[1] - col0), F32))
        side += [cs, w_ada, b_ada]
    scratch_shapes = [
        pltpu.VMEM((2, tb, D_MODEL), BF16),
        pltpu.VMEM((2, tb, D_GMLP), F32),
        pltpu.VMEM((2, tb, D_GMLP), F32),
        pltpu.VMEM((2, tb, 2 * D_MLSTM), F32),
        pltpu.VMEM((2, tb, D_MLSTM), F32),
        pltpu.VMEM((2, D_MLSTM, tb), BF16),
        pltpu.VMEM((2, N_GATE, tb), F32),
    ]
    return pl.pallas_call(
        functools.partial(_inproj_kernel, seg=seg, blocks_per_mod=rows_per_mod // tb2,
                          mod_row0=mod_row0, n_cast=len(cast), ada_tail=ada_tail is not None),
        out_shape=out_shape,
        grid=(nb + 1,),
        in_specs=in_specs,
        out_specs=out_specs,
        scratch_shapes=scratch_shapes,
        compiler_params=pltpu.CompilerParams(
            dimension_semantics=("arbitrary",), vmem_limit_bytes=VMEM_LIMIT_BYTES),
        name="inproj",
    )(x, mod, p["g1"], p["wit"],
      p["bgt"], p["cw"], p["cb"], p["ws"], p["bs"], p["gv"], *side)


def _lane_broadcast_column(row):
    n = row.shape[1]
    tiles = [jnp.transpose(jnp.broadcast_to(row[:, t * HD:(t + 1) * HD], (HD, HD)))
             for t in range(n // HD)]
    col = jnp.concatenate(tiles, axis=0)
    return jnp.concatenate([col] * (n // HD), axis=1)


def _mlstm_unit(k, vt, kq, qs, d_row, dmax_row, b_row, g_row, bl_row, gmax_row, st, m_row, mask,
                carried):
    logw = jnp.where(mask, _lane_broadcast_column(d_row) + b_row, NEG)
    a = b_row + m_row
    mj = jnp.maximum(a, dmax_row + b_row)
    w = jnp.exp2(logw - mj)
    s = kq * w
    num = _dot(vt, s.astype(BF16))
    den = jnp.sum(s, axis=0, keepdims=True)
    if carried:
        inter = jnp.exp2(a - mj)
        num = num + inter * qs[:HD]
        den = den + inter * qs[HD:HD + 1]
    h = num * (1.0 / jnp.maximum(jnp.abs(den), jnp.exp2(-mj)))
    m_new = jnp.maximum(bl_row + m_row, gmax_row)
    wc = jnp.exp2(g_row - m_new)
    pad_row = lax.broadcasted_iota(jnp.int32, (ST_ROWS - HD, wc.shape[1]), 0)
    v_aug = jnp.concatenate(
        [vt.astype(F32) * wc, jnp.where(pad_row == 0, wc, 0.0)], axis=0).astype(BF16)
    st_new = _dot(v_aug, k)
    if carried:
        decay = jnp.exp2(bl_row + m_row - m_new)
        st_new = decay[:, :HD] * st + st_new
    return h, st_new, m_new


def _mlstm_kernel(*refs, has_init, has_out, rounds, by_sequence):
    if has_init:
        s0_ref, m0_ref = refs[:2]
        refs = refs[2:]
    def split(k_ref, qvt_ref, gr_ref):
        return k_ref, qvt_ref.at[0:D_MLSTM, :], qvt_ref.at[D_MLSTM:, :], gr_ref

    if by_sequence:
        fwd_in = bwd_in = split(*refs[:3])
        refs = refs[3:]
    else:
        fwd_in, bwd_in = split(*refs[:3]), split(*refs[3:6])
        refs = refs[6:]
    if by_sequence:
        hf_ref = hb_ref = refs[0]
        refs = refs[1:]
    else:
        hf_ref, hb_ref = refs[:2]
        refs = refs[2:]
    if has_out:
        co_ref, no_ref, mo_ref = refs[:3]
        refs = refs[3:]
    s_ref, m_ref = refs
    c = pl.program_id(1)
    nc = pl.num_programs(1)
    carried = not by_sequence

    if carried:
        @pl.when(c == 0)
        def _():
            if has_init:
                s_ref[...] = s0_ref[0]
                m_ref[...] = m0_ref[0]
            else:
                s_ref[...] = jnp.zeros_like(s_ref)
                m_ref[...] = jnp.zeros_like(m_ref)

    si = lax.broadcasted_iota(jnp.int32, (SCAN, SCAN), 0)
    ji = lax.broadcasted_iota(jnp.int32, (SCAN, SCAN), 1)
    dirs = ((fwd_in, hf_ref, si <= ji), (bwd_in, hb_ref, si >= ji))

    def chunk_cols(d, r):
        pos = rounds - 1 - r if (d == 1 and not by_sequence) else r
        return slice(pos * SCAN, (pos + 1) * SCAN)

    kq = {}
    for r in range(rounds):
        for d, ((k_ref, qt_ref, _, _), _, _) in enumerate(dirs):
            cs = chunk_cols(d, r)
            for hd in range(HEADS):
                hs = slice(hd * HD, (hd + 1) * HD)
                kq[r, d, hd] = _dot(k_ref[cs, hs], qt_ref[hs, cs])

    m_cur = None
    if carried:
        m_all = m_ref[...]
        m_cur = [m_all[u:u + 1] for u in range(N_UNIT)]
    zero_row = jnp.zeros((1, SCAN), F32)
    for r in range(rounds):
        slot = r * N_UNIT if by_sequence else 0
        for d, ((k_ref, qt_ref, vt_ref, gr_ref), h_ref, mask) in enumerate(dirs):
            cs = chunk_cols(d, r)
            for hd in range(HEADS):
                u = d * HEADS + hd
                row = d * 2 * HEADS + HEADS + hd
                hs = slice(hd * HD, (hd + 1) * HD)
                st = s_ref[slot + u] if carried else None
                qs = _dot(st.astype(BF16), qt_ref[hs, cs]) if carried else None
                h, st_new, m_new = _mlstm_unit(
                    k_ref[cs, hs], vt_ref[hs, cs], kq[r, d, hd], qs,
                    *(gr_ref[sec * N_GATE + row:sec * N_GATE + row + 1, cs]
                      for sec in (4, 5, 0, 1, 2, 3)),
                    st, m_cur[u] if carried else zero_row, mask, carried)
                h_ref[hs, cs] = h_ref[hs, cs] + h if (by_sequence and d == 1) else h
                s_ref[slot + u] = st_new
                if carried:
                    m_cur[u] = m_new
                else:
                    m_ref[slot + u:slot + u + 1, :] = m_new
    if carried:
        m_ref[...] = jnp.concatenate(m_cur, axis=0)

    if has_out:
        @pl.when(c == nc - 1)
        def _():
            for j in range(s_ref.shape[0]):
                q, u = divmod(j, N_UNIT)
                co_ref[q, u] = jnp.transpose(s_ref[j, 0:HD, :])
                no_ref[q, u:u + 1, :] = s_ref[j, HD:HD + 1, :]
                mo_ref[q, u:u + 1, :] = m_ref[j:j + 1, :]


def _mlstm(ak, qvt, gr, batch, rounds, s0=None, m0=None, want_state=False):
    n = ak.shape[0]
    nc = n // batch // SCAN
    by_sequence = nc == 1
    width = rounds * SCAN
    has_init = s0 is not None
    if by_sequence:
        assert batch % rounds == 0 and not has_init
        grid = (batch // rounds, 1)
        slots = rounds
        fwd = bwd = lambda b, c: b
    else:
        assert nc % rounds == 0 and not want_state
        steps = nc // rounds
        grid = (batch, steps)
        slots = 1
        fwd = lambda b, c: b * steps + c
        bwd = lambda b, c: b * steps + steps - 1 - c

    def specs(ix):
        return [
            pl.BlockSpec((width, D_MLSTM), lambda b, c: (ix(b, c), 1)),
            pl.BlockSpec((2 * D_MLSTM, width), lambda b, c: (0, ix(b, c))),
            pl.BlockSpec((GR_ROWS, width), lambda b, c: (0, ix(b, c))),
        ]

    in_specs = specs(fwd) if by_sequence else specs(fwd) + specs(bwd)
    args = [ak, qvt, gr] if by_sequence else [ak, qvt, gr, ak, qvt, gr]
    if has_init:
        in_specs = [
            pl.BlockSpec((1, N_UNIT, ST_ROWS, HD), lambda b, c: (b, 0, 0, 0)),
            pl.BlockSpec((1, N_UNIT, SCAN), lambda b, c: (b, 0, 0)),
        ] + in_specs
        args = [s0, m0] + args
    n_h = 1 if by_sequence else N_DIR
    out_shape = [jax.ShapeDtypeStruct((D_MLSTM, n), F32)] * n_h
    out_specs = [pl.BlockSpec((D_MLSTM, width), lambda b, c, ix=ix: (0, ix(b, c)))
                 for ix in (fwd, bwd)[:n_h]]
    if want_state:
        out_shape += [
            jax.ShapeDtypeStruct((batch, N_UNIT, HD, HD), F32),
            jax.ShapeDtypeStruct((batch, N_UNIT, HD), F32),
            jax.ShapeDtypeStruct((batch, N_UNIT, SCAN), F32),
        ]
        out_specs += [
            pl.BlockSpec((slots, N_UNIT, HD, HD), lambda b, c: (b, 0, 0, 0)),
            pl.BlockSpec((slots, N_UNIT, HD), lambda b, c: (b, 0, 0)),
            pl.BlockSpec((slots, N_UNIT, SCAN), lambda b, c: (b, 0, 0)),
        ]
    outs = pl.pallas_call(
        functools.partial(_mlstm_kernel, has_init=has_init, has_out=want_state, rounds=rounds,
                          by_sequence=by_sequence),
        out_shape=out_shape,
        grid=grid,
        in_specs=in_specs,
        out_specs=out_specs,
        scratch_shapes=[pltpu.VMEM((slots * N_UNIT, ST_ROWS, HD), F32),
                        pltpu.VMEM((slots * N_UNIT, SCAN), F32)],
        compiler_params=pltpu.CompilerParams(
            dimension_semantics=("arbitrary", "arbitrary"), vmem_limit_bytes=VMEM_LIMIT_BYTES),
        name="mlstm",
    )(*args)
    return outs[:n_h], outs[n_h:]


def _outffn_mix(x, a, hst, o, mod, c, x1_ref, h2_ref):
    ga1 = mod[:, 0:D_MODEL]
    sh2 = mod[:, D_MODEL:2 * D_MODEL]
    sc2 = mod[:, 2 * D_MODEL:3 * D_MODEL]
    hs = jnp.transpose(hst)
    sig = _sigmoid(o)
    parts = []
    for hd in range(HEADS):
        sl = slice(hd * HD, (hd + 1) * HD)
        parts.append(_rms(hs[:, sl], c["gh"][:, sl]) * sig[:, sl])
    b_out = jnp.concatenate(parts, axis=-1).astype(BF16)
    mix = _dot(a, c["wout"][0:D_GMLP, :]) + _dot(b_out, c["wout"][D_GMLP:, :])
    x1 = x + ga1 * mix
    x1_ref[...] = x1
    h2_ref[...] = (_rms(x1, c["g2"][...] * (1.0 + sc2)) + sh2).astype(BF16)


def _outffn_up(h2_ref, c, f_ref):
    h2 = h2_ref[...]
    u = _dot(h2, c["w1"][...])
    g = _dot(h2, c["w3"][...])
    f_ref[...] = (u * _sigmoid(u) * g).astype(BF16)


def _outffn_down(f_ref, x1_ref, mod, c, y_ref):
    ga2 = mod[:, 3 * D_MODEL:4 * D_MODEL]
    x2 = x1_ref[...] + ga2 * _dot(f_ref[...], c["w2"][...])
    y_ref[...] = _rms(x2, c["gf"][...])


def _outffn_kernel(x_ref, a_ref, o_ref, mod_ref, gh_ref, wout_hbm,
                   g2_ref, w1_hbm, w3_hbm, w2_hbm, gf_ref, *refs, n_h, blocks_per_mod, mod_row0):
    ht_refs = refs[:n_h]
    y_ref, x1_s, h2_s, f_s, wout_s, w1_s, w3_s, w2_s, w_sem = refs[n_h:]
    tb = x_ref.shape[0] // 2
    i = pl.program_id(0)
    mod_in = mod_row0 + jnp.minimum(i, pl.num_programs(0) - 2) // blocks_per_mod
    mod_out = mod_row0 + jnp.maximum(i - 1, 0) // blocks_per_mod
    c = {"gh": gh_ref, "wout": wout_s, "g2": g2_ref, "w1": w1_s, "w3": w3_s, "w2": w2_s,
         "gf": gf_ref}
    weight_copy = {
        name: pltpu.make_async_copy(src, dst, w_sem.at[j])
        for j, (name, src, dst) in enumerate((("wout", wout_hbm, wout_s), ("w1", w1_hbm, w1_s),
                                              ("w3", w3_hbm, w3_s), ("w2", w2_hbm, w2_s)))}

    def tick(half, mix, up, down):
        rows = slice(half * tb, (half + 1) * tb)
        cur, oth = half, 1 - half
        first_tick = mix and not up and not down
        second_tick = mix and up and not down
        if first_tick:
            for name in ("wout", "w1", "w3", "w2"):
                weight_copy[name].start()
            weight_copy["wout"].wait()
        if down:
            _outffn_down(f_s.at[cur], x1_s.at[cur], mod_ref[pl.ds(mod_out, 1), :], c,
                         y_ref.at[rows, :])
        if mix:
            hst = functools.reduce(lambda s, t: s + t, [ref[:, rows] for ref in ht_refs])
            _outffn_mix(x_ref[rows, :], a_ref[rows, :], hst, o_ref[rows, :],
                        mod_ref[pl.ds(mod_in, 1), :], c, x1_s.at[cur], h2_s.at[cur])
        if second_tick:
            weight_copy["w1"].wait()
            weight_copy["w3"].wait()
        if up:
            _outffn_up(h2_s.at[oth], c, f_s.at[oth])
        if second_tick:
            weight_copy["w2"].wait()

    _pipeline_step(tick)


def _outffn(x, ak, hts, o, mod, mod_row0, rows_per_mod, p):
    n = x.shape[0]
    tb = TOKEN_BLOCK
    tb2 = 2 * tb
    assert n % tb2 == 0 and rows_per_mod % tb2 == 0
    nb = n // tb2
    d_ff = p["w1"].shape[1]
    blk_in = lambda i: jnp.minimum(i, nb - 1)
    blk_out = lambda i: jnp.maximum(i - 1, 0)
    tok = lambda w: pl.BlockSpec((tb2, w), lambda i: (blk_in(i), 0))
    tok_t = lambda h: pl.BlockSpec((h, tb2), lambda i: (0, blk_in(i)))
    weight_hbm = pl.BlockSpec(memory_space=pl.ANY)
    in_specs = [
        tok(D_MODEL), tok(D_GMLP), tok(D_MLSTM),
        _const_spec(mod.shape),
        _const_spec((1, D_MLSTM)),
        weight_hbm,
        _const_spec((1, D_MODEL)),
        weight_hbm,
        weight_hbm,
        weight_hbm,
        _const_spec((1, D_MODEL)),
    ] + [tok_t(D_MLSTM)] * len(hts)
    scratch_shapes = [
        pltpu.VMEM((2, tb, D_MODEL), F32),
        pltpu.VMEM((2, tb, D_MODEL), BF16),
        pltpu.VMEM((2, tb, d_ff), BF16),
        pltpu.VMEM(p["wout"].shape, BF16),
        pltpu.VMEM(p["w1"].shape, BF16),
        pltpu.VMEM(p["w3"].shape, BF16),
        pltpu.VMEM(p["w2"].shape, BF16),
        pltpu.SemaphoreType.DMA((4,)),
    ]
    return pl.pallas_call(
        functools.partial(_outffn_kernel, n_h=len(hts), blocks_per_mod=rows_per_mod // tb2,
                          mod_row0=mod_row0),
        out_shape=jax.ShapeDtypeStruct((n, D_MODEL), F32),
        grid=(nb + 1,),
        in_specs=in_specs,
        out_specs=pl.BlockSpec((tb2, D_MODEL), lambda i: (blk_out(i), 0)),
        scratch_shapes=scratch_shapes,
        compiler_params=pltpu.CompilerParams(
            dimension_semantics=("arbitrary",), vmem_limit_bytes=VMEM_LIMIT_BYTES),
        name="outffn",
    )(x, ak, o, mod, p["gh"], p["wout"], p["g2"], p["w1"], p["w3"], p["w2"], p["gf"], *hts)


def _layer_params(l, g_norm1, b_gate, w_s, b_s, g_v, conv_w, conv_b, g_h, g_norm2, g_final):
    return {
        "g1": g_norm1[l][None, :],
        "bgt": b_gate[l][:, None],
        "cw": conv_w[l], "cb": conv_b[l][None, :],
        "ws": w_s[l].astype(BF16),
        "bs": jnp.repeat(b_s[l].T, HD, axis=1),
        "gv": g_v[l].reshape(1, D_GMLP),
        "gh": g_h[l].reshape(1, D_MLSTM),
        "g2": g_norm2[l][None, :],
        "gf": g_final[None, :],
    }


_LATE_WEIGHTS = ("wout", "w1", "w3", "w2")


def _trunk_front(x, mod_row0, rows_per_mod, seg, p, late=None):
    cast, ada_tail = ((), None) if late is None else (
        tuple(late[0][name] for name in _LATE_WEIGHTS), late[1])
    ak, qvt, o, gr, *side = _inproj(x, p["mod_head"], mod_row0, rows_per_mod, seg, p, cast,
                                    ada_tail)
    if late is not None:
        p = dict(p, **dict(zip(_LATE_WEIGHTS, side[:-1])))
        p["mod_tail"] = side[-1]
    return (ak, qvt, o), gr, p


def _trunk_back(x, front, gr, mod_row0, rows_per_mod, batch, rounds, p, s0=None, m0=None,
                want_state=False):
    ak, qvt, o = front
    hts, state = _mlstm(ak, qvt, gr, batch, rounds, s0, m0, want_state)
    y = _outffn(x, ak, hts, o, p["mod_tail"], mod_row0, rows_per_mod, p)
    return y, state


def kernel(x_prompt, x_sample, state_C, state_n, state_m, c, c_ctx, w_ada, b_ada, g_norm1, w_in,
           b_gate, w_s, b_s, g_v, conv_w, conv_b, g_h, w_out, g_norm2, w1, w3, w2, g_final):
    bp, tp, d = x_prompt.shape
    bs_, ts, _ = x_sample.shape
    depth = w_in.shape[0]
    assert depth == 1, "final norm is fused into the layer's last kernel"
    xp = x_prompt.reshape(bp * tp, d)
    xs = x_sample.reshape(bs_ * ts, d)

    cs = jnp.zeros((8, d), F32).at[0].set(c_ctx).at[1:1 + bs_].set(c)
    new_c, new_n, new_m = [], [], []
    for l in range(depth):
        p = _layer_params(l, g_norm1, b_gate, w_s, b_s, g_v, conv_w, conv_b, g_h, g_norm2, g_final)
        late_f32 = {"wout": w_out[l], "w1": w1[l], "w3": w3[l], "w2": w2[l]}
        p["mod_head"], p["wit"] = _ada(cs, w_ada[l], b_ada[l][None, :], 2 * d, w_in[l].T)
        ada_tail = (cs, w_ada[l], b_ada[l][None, :], 2 * d)

        front_lat, gr_lat, p = _trunk_front(xs, 1, ts, GRID_W, p, (late_f32, ada_tail))
        front_ctx, gr_ctx, _ = _trunk_front(xp, 0, bp * tp, tp, p)

        xp, (c_ctx_out, n_ctx_out, m_ctx_out) = _trunk_back(
            xp, front_ctx, gr_ctx, 0, bp * tp, bp, CTX_SEQS_PER_STEP, p, want_state=True)
        new_c.append(c_ctx_out.reshape(bp, N_DIR, HEADS, HD, HD))
        new_n.append(n_ctx_out.reshape(bp, N_DIR, HEADS, HD))
        new_m.append((m_ctx_out[..., 0] * (1.0 / LOG2E)).reshape(bp, N_DIR, HEADS))

        s0 = jnp.concatenate(
            [jnp.swapaxes(state_C[:, l], -1, -2), state_n[:, l][..., None, :],
             jnp.zeros((bs_, N_DIR, HEADS, ST_ROWS - HD - 1, HD), F32)],
            axis=-2).reshape(bs_, N_UNIT, ST_ROWS, HD)
        m0 = jnp.broadcast_to((state_m[:, l] * LOG2E).reshape(bs_, N_UNIT, 1),
                              (bs_, N_UNIT, SCAN))
        xs, _ = _trunk_back(xs, front_lat, gr_lat, 1, ts, bs_, LAT_CHUNKS_PER_STEP, p, s0=s0, m0=m0)

    return (xp.reshape(bp, tp, d), xs.reshape(bs_, ts, d),
            jnp.stack(new_c, axis=1), jnp.stack(new_n, axis=1), jnp.stack(new_m, axis=1))
```

```python
import functools

import jax
import jax.numpy as jnp
from jax import lax
from jax.experimental import pallas as pl
from jax.experimental.pallas import tpu as pltpu

D_MODEL = 1024
D_GMLP = 512
D_MLSTM = 512
GROUPS = 4
HEADS = 4
HD = 128
CHUNK = 128
SCAN = 256
N_DIR = 2
N_UNIT = N_DIR * HEADS
GRID_W = 64
OFF_V, OFF_Q, OFF_VV, OFF_O, OFF_G = 512, 1024, 2048, 2560, 3072
D_IN = OFF_G + 2 * N_DIR * HEADS
EPS = 1e-6
NEG = -1e30
ST_ROWS = HD + 16
N_GATE = 2 * N_UNIT
GR_ROWS = 6 * N_GATE

TOKEN_BLOCK = 256
CTX_SEQS_PER_STEP = 2
LAT_CHUNKS_PER_STEP = 2
QVT_RING = 3
VMEM_LIMIT_BYTES = 56 * 1024 * 1024

F32 = jnp.float32
BF16 = jnp.bfloat16


def _rms(x, g):
    n = x.shape[-1]
    return x * lax.rsqrt(jnp.sum(x * x, axis=-1, keepdims=True) + n * EPS) * (g * n ** 0.5)


LOG2E = 1.4426950408889634
GELU_K = 0.7978845608028654
GELU_C = 0.044715


def _sigmoid(x):
    return 1.0 / (1.0 + jnp.exp2(x * (-LOG2E)))


def _gelu_tanh(x):
    t = (x * x) * (-2.0 * LOG2E * GELU_K * GELU_C) + (-2.0 * LOG2E * GELU_K)
    return x * (1.0 / (1.0 + jnp.exp2(x * t)))


def _log_sigmoid(x):
    return jnp.minimum(x, 0.0) - jnp.log(1.0 + jnp.exp(-jnp.abs(x)))


def _dot(a, b):
    return jnp.dot(a, b, preferred_element_type=F32)


def _dot_nt(a, b):
    return lax.dot_general(a, b, (((1,), (1,)), ((), ())), preferred_element_type=F32)


def _pipeline_step(tick):
    i = pl.program_id(0)
    last = pl.num_programs(0) - 1

    @pl.when(i == 0)
    def _():
        tick(0, True, False, False)
        tick(1, True, True, False)

    @pl.when(jnp.logical_and(i > 0, i < last))
    def _():
        tick(0, True, True, True)
        tick(1, True, True, True)

    @pl.when(i == last)
    def _():
        tick(0, False, True, True)
        tick(1, False, False, True)


def _const_spec(shape):
    zeros = (0,) * len(shape)
    return pl.BlockSpec(shape, lambda *_: zeros, pipeline_mode=pl.Buffered(1))


def _ada_kernel(c_ref, w_ref, b_ref, o_ref):
    c = c_ref[...]
    s = (c * _sigmoid(c)).astype(BF16)
    o_ref[...] = _dot(s, w_ref[...].astype(BF16)) + b_ref[...]


def _ada_and_round_kernel(c_ref, w_ref, b_ref, wf_ref, o_ref, wb_ref):
    _ada_kernel(c_ref, w_ref, b_ref, o_ref)

    @pl.when(pl.program_id(0) == 0)
    def _():
        wb_ref[...] = wf_ref[...].astype(BF16)


def _ada(cs, w_ada, b_ada, n, w_f32):
    rows, d = cs.shape
    tn = 1024
    return pl.pallas_call(
        _ada_and_round_kernel,
        out_shape=[jax.ShapeDtypeStruct((rows, n), F32), jax.ShapeDtypeStruct(w_f32.shape, BF16)],
        grid=(n // tn,),
        in_specs=[
            pl.BlockSpec((rows, d), lambda j: (0, 0)),
            pl.BlockSpec((d, tn), lambda j: (0, j)),
            pl.BlockSpec((1, tn), lambda j: (0, j)),
            _const_spec(w_f32.shape),
        ],
        out_specs=[pl.BlockSpec((rows, tn), lambda j: (0, j)),
                   pl.BlockSpec(w_f32.shape, lambda j: (0, 0))],
        compiler_params=pltpu.CompilerParams(
            dimension_semantics=("arbitrary",), vmem_limit_bytes=VMEM_LIMIT_BYTES),
        name="ada",
    )(cs, w_ada, b_ada, w_f32)


def _inproj_norm(x, mod, g1, hb_ref):
    sh1 = mod[:, 0:D_MODEL]
    sc1 = mod[:, D_MODEL:2 * D_MODEL]
    hb_ref[...] = (_rms(x, g1 * (1.0 + sc1)) + sh1).astype(BF16)


def _inproj_project(hb_ref, w, z):
    hb = hb_ref[...]
    z["u"][...] = _dot_nt(hb, w["u"][...])
    z["v"][...] = _dot_nt(hb, w["v"][...])
    z["qk"][...] = _dot_nt(hb, w["qk"][...])
    z["o"][...] = _dot_nt(hb, w["o"][...])
    z["vt"][...] = _dot_nt(w["vv"][...], hb).astype(BF16)
    z["gt"][...] = _dot_nt(w["gt"][...], hb)


def _inproj_finish(z, c, out, seg):
    tb = z["u"].shape[0]
    for g in range(GROUPS):
        gs = slice(g * HD, (g + 1) * HD)
        vg = _rms(_gelu_tanh(z["v"][:, gs]), c["gv"][:, gs]).astype(BF16)
        for ch in range(tb // CHUNK):
            cs = slice(ch * CHUNK, (ch + 1) * CHUNK)
            mixed = _dot(c["ws"][g], vg[cs]) + c["bs"][:, gs]
            out["a"][cs, gs] = (_gelu_tanh(z["u"][cs, gs]) * mixed).astype(BF16)

    zqk = z["qk"][...]
    pos = lax.broadcasted_iota(jnp.int32, (tb, 1), 0) % seg
    prev = jnp.where(pos != 0, pltpu.roll(zqk, 1, 0), 0.0)
    nxt = jnp.where(pos != seg - 1, pltpu.roll(zqk, tb - 1, 0), 0.0)
    cw = c["cw"]
    y = c["cb"][...] + prev * cw[0:1, :] + zqk * cw[1:2, :] + nxt * cw[2:3, :]
    y = y * _sigmoid(y)
    out["qt"][...] = jnp.transpose(y[:, :D_MLSTM]).astype(BF16)
    out["k"][...] = (y[:, D_MLSTM:] * (HD ** -0.5)).astype(BF16)
    out["vt"][...] = z["vt"][...]
    out["o"][...] = z["o"][...]

    assert tb == SCAN
    gt = z["gt"][...] + c["bgt"][...]
    gi = pltpu.roll(gt, HEADS, 0)
    lf = _log_sigmoid(gt)
    lane = lax.broadcasted_iota(jnp.int32, (N_GATE, HD), 1)
    tiles = []
    carry = jnp.zeros((N_GATE, HD), F32)
    for t in range(SCAN // HD):
        prefix = lf[:, t * HD:(t + 1) * HD]
        shift = 1
        while shift < HD:
            prefix = prefix + jnp.where(lane >= shift, pltpu.roll(prefix, shift, 1), 0.0)
            shift *= 2
        tiles.append(prefix + carry)
        carry = carry + jnp.broadcast_to(prefix[:, HD - 1:HD], (N_GATE, HD))
    prefix = jnp.concatenate(tiles, axis=1)
    b_last = jnp.concatenate([carry] * (SCAN // HD), axis=1)
    row_bwd = lax.broadcasted_iota(jnp.int32, (N_GATE, SCAN), 0) >= N_GATE // N_DIR
    b_row = jnp.where(row_bwd, b_last - prefix + lf, prefix)
    g_row = b_last - b_row + gi
    g_max = jnp.broadcast_to(jnp.max(g_row, axis=1, keepdims=True), (N_GATE, SCAN))
    d_row = (gi - b_row) * LOG2E
    lane = lax.broadcasted_iota(jnp.int32, (N_GATE, SCAN), 1)
    d_upto, d_from = d_row, d_row
    shift = 1
    while shift < SCAN:
        d_upto = jnp.maximum(d_upto, jnp.where(lane >= shift, pltpu.roll(d_upto, shift, 1), NEG))
        d_from = jnp.maximum(
            d_from, jnp.where(lane < SCAN - shift, pltpu.roll(d_from, SCAN - shift, 1), NEG))
        shift *= 2
    d_max = jnp.where(row_bwd, d_from, d_upto)
    for sec, stat in enumerate((b_row, g_row, b_last, g_max)):
        out["gr"][sec * N_GATE:(sec + 1) * N_GATE, :] = stat * LOG2E
    out["gr"][4 * N_GATE:5 * N_GATE, :] = d_row
    out["gr"][5 * N_GATE:6 * N_GATE, :] = d_max


_INPROJ_Z = ("u", "v", "qk", "o", "vt", "gt")


def _inproj_kernel(*refs, seg, blocks_per_mod, mod_row0, n_cast, ada_tail):
    refs = list(refs)
    take = lambda k: [refs.pop(0) for _ in range(k)]
    (x_ref, mod_ref, g1_ref, wit_ref,
     bgt_ref, cw_ref, cb_ref, ws_ref, bs_ref, gv_ref) = take(10)
    cast_in = take(n_cast)
    ada_in = take(3) if ada_tail else None
    ak_ref, qvt_ref, o_ref, gr_ref = take(4)
    cast_out = take(n_cast)
    ada_out = take(1)[0] if ada_tail else None
    hb_s, *z_s = refs
    tb = x_ref.shape[0] // 2
    mod_in = mod_row0 + jnp.minimum(pl.program_id(0), pl.num_programs(0) - 2) // blocks_per_mod
    w = {"u": wit_ref.at[0:OFF_V, :], "v": wit_ref.at[OFF_V:OFF_Q, :],
         "qk": wit_ref.at[OFF_Q:OFF_VV, :], "vv": wit_ref.at[OFF_VV:OFF_O, :],
         "o": wit_ref.at[OFF_O:OFF_G, :], "gt": wit_ref.at[OFF_G:D_IN, :]}
    c = {"gv": gv_ref, "ws": ws_ref, "bs": bs_ref, "cw": cw_ref, "cb": cb_ref, "bgt": bgt_ref}

    def tick(half, norm, project, finish):
        rows = slice(half * tb, (half + 1) * tb)
        cur, oth = half, 1 - half
        z_cur = {name: ref.at[cur] for name, ref in zip(_INPROJ_Z, z_s)}
        z_oth = {name: ref.at[oth] for name, ref in zip(_INPROJ_Z, z_s)}
        out = {"a": ak_ref.at[rows, 0:D_GMLP], "k": ak_ref.at[rows, D_GMLP:],
               "qt": qvt_ref.at[0:D_MLSTM, rows], "vt": qvt_ref.at[D_MLSTM:, rows],
               "o": o_ref.at[rows, :], "gr": gr_ref.at[:, rows]}
        if project:
            _inproj_project(hb_s.at[oth], w, z_oth)
        if norm:
            _inproj_norm(x_ref[rows, :], mod_ref[pl.ds(mod_in, 1), :], g1_ref[...], hb_s.at[cur])
        if finish:
            _inproj_finish(z_cur, c, out, seg)
        if half == 0:
            for src, dst in zip(cast_in, cast_out):
                dst[...] = src[...].astype(BF16)
            if ada_tail:
                _ada_kernel(*ada_in, ada_out)

    _pipeline_step(tick)


def _inproj(x, mod, mod_row0, rows_per_mod, seg, p, cast=(), ada_tail=None):
    n = x.shape[0]
    tb = TOKEN_BLOCK
    tb2 = 2 * tb
    assert tb % seg == 0 and n % tb2 == 0 and rows_per_mod % tb2 == 0
    nb = n // tb2
    blk_in = lambda i: jnp.minimum(i, nb - 1)
    blk_out = lambda i: jnp.maximum(i - 1, 0)
    tok = lambda width: pl.BlockSpec((tb2, width), lambda i: (blk_out(i), 0))
    tok_t = lambda height: pl.BlockSpec((height, tb2), lambda i: (0, blk_out(i)))
    in_specs = [
        pl.BlockSpec((tb2, D_MODEL), lambda i: (blk_in(i), 0)),
        _const_spec(mod.shape),
        _const_spec((1, D_MODEL)),
        _const_spec((D_IN, D_MODEL)),
        _const_spec((N_GATE, 1)),
        _const_spec((3, 2 * D_MLSTM)),
        _const_spec((1, 2 * D_MLSTM)),
        _const_spec((GROUPS, CHUNK, CHUNK)),
        _const_spec((CHUNK, D_GMLP)),
        _const_spec((1, D_GMLP)),
    ]
    out_shape = [
        jax.ShapeDtypeStruct((n, D_GMLP + D_MLSTM), BF16),
        jax.ShapeDtypeStruct((2 * D_MLSTM, n), BF16),
        jax.ShapeDtypeStruct((n, D_MLSTM), F32),
        jax.ShapeDtypeStruct((GR_ROWS, n), F32),
    ]
    out_specs = [tok(D_GMLP + D_MLSTM), tok_t(2 * D_MLSTM), tok(D_MLSTM), tok_t(GR_ROWS)]
    for wf in cast:
        rows = wf.shape[0] // nb
        assert rows * nb == wf.shape[0] and rows % 16 == 0
        slab = pl.BlockSpec((rows, wf.shape[1]), lambda i: (blk_in(i), 0))
        in_specs.append(slab)
        out_specs.append(slab)
        out_shape.append(jax.ShapeDtypeStruct(wf.shape, BF16))
    side = list(cast)
    if ada_tail is not None:
        cs, w_ada, b_ada, col0 = ada_tail
        width = (w_ada.shape[1] - col0) // nb
        assert col0 % width == 0 and width * nb == w_ada.shape[1] - col0 and width % 128 == 0
        in_specs += [
            _const_spec(cs.shape),
            pl.BlockSpec((w_ada.shape[0], width), lambda i: (0, col0 // width + blk_in(i))),
            pl.BlockSpec((1, width), lambda i: (0, col0 // width + blk_in(i))),
        ]
        out_specs.append(pl.BlockSpec((cs.shape[0], width), lambda i: (0, blk_in(i))))
        out_shape.append(jax.ShapeDtypeStruct((cs.shape[0], w_ada.shape[1] - col0), F32))
        side += [cs, w_ada, b_ada]
    scratch_shapes = [
        pltpu.VMEM((2, tb, D_MODEL), BF16),
        pltpu.VMEM((2, tb, D_GMLP), F32),
        pltpu.VMEM((2, tb, D_GMLP), F32),
        pltpu.VMEM((2, tb, 2 * D_MLSTM), F32),
        pltpu.VMEM((2, tb, D_MLSTM), F32),
        pltpu.VMEM((2, D_MLSTM, tb), BF16),
        pltpu.VMEM((2, N_GATE, tb), F32),
    ]
    return pl.pallas_call(
        functools.partial(_inproj_kernel, seg=seg, blocks_per_mod=rows_per_mod // tb2,
                          mod_row0=mod_row0, n_cast=len(cast), ada_tail=ada_tail is not None),
        out_shape=out_shape,
        grid=(nb + 1,),
        in_specs=in_specs,
        out_specs=out_specs,
        scratch_shapes=scratch_shapes,
        compiler_params=pltpu.CompilerParams(
            dimension_semantics=("arbitrary",), vmem_limit_bytes=VMEM_LIMIT_BYTES),
        name="inproj",
    )(x, mod, p["g1"], p["wit"],
      p["bgt"], p["cw"], p["cb"], p["ws"], p["bs"], p["gv"], *side)


def _lane_broadcast_column(row):
    n = row.shape[1]
    tiles = [jnp.transpose(jnp.broadcast_to(row[:, t * HD:(t + 1) * HD], (HD, HD)))
             for t in range(n // HD)]
    col = jnp.concatenate(tiles, axis=0)
    return jnp.concatenate([col] * (n // HD), axis=1)


def _mlstm_unit(k, vt, kq, qs, d_row, dmax_row, b_row, g_row, bl_row, gmax_row, st, m_row, mask,
                carried):
    logw = jnp.where(mask, _lane_broadcast_column(d_row) + b_row, NEG)
    a = b_row + m_row
    mj = jnp.maximum(a, dmax_row + b_row)
    w = jnp.exp2(logw - mj)
    s = kq * w
    num = _dot(vt, s.astype(BF16))
    den = jnp.sum(s, axis=0, keepdims=True)
    if carried:
        inter = jnp.exp2(a - mj)
        num = num + inter * qs[:HD]
        den = den + inter * qs[HD:HD + 1]
    h = num * (1.0 / jnp.maximum(jnp.abs(den), jnp.exp2(-mj)))
    m_new = jnp.maximum(bl_row + m_row, gmax_row)
    wc = jnp.exp2(g_row - m_new)
    pad_row = lax.broadcasted_iota(jnp.int32, (ST_ROWS - HD, wc.shape[1]), 0)
    v_aug = jnp.concatenate(
        [vt.astype(F32) * wc, jnp.where(pad_row == 0, wc, 0.0)], axis=0).astype(BF16)
    st_new = _dot(v_aug, k)
    if carried:
        decay = jnp.exp2(bl_row + m_row - m_new)
        st_new = decay[:, :HD] * st + st_new
    return h, st_new, m_new


def _mlstm_kernel(*refs, has_init, has_out, rounds, by_sequence):
    if has_init:
        s0_ref, m0_ref = refs[:2]
        refs = refs[2:]
    n_streams = 1 if by_sequence else N_DIR
    streams = [refs[3 * j:3 * j + 3] for j in range(n_streams)]
    refs = refs[3 * n_streams:]
    if by_sequence:
        hf_ref = hb_ref = refs[0]
        refs = refs[1:]
    else:
        hf_ref, hb_ref = refs[:2]
        refs = refs[2:]
    if has_out:
        co_ref, no_ref, mo_ref = refs[:3]
        refs = refs[3:]
    s_ref, m_ref, ring, ring_sem = refs
    c = pl.program_id(1)
    nc = pl.num_programs(1)
    carried = not by_sequence

    step = pl.program_id(0) * nc + c
    total = pl.num_programs(0) * nc
    width = rounds * SCAN

    def qvt_copy(j, t):
        blk = t if j == 0 else (t // nc) * nc + nc - 1 - t % nc
        return pltpu.make_async_copy(
            streams[j][1].at[:, pl.ds(pl.multiple_of(blk * width, width), width)],
            ring.at[j, t % QVT_RING], ring_sem.at[j, t % QVT_RING])

    @pl.when(step == 0)
    def _():
        for j in range(n_streams):
            for t in range(QVT_RING - 1):
                qvt_copy(j, t).start()

    @pl.when(step + QVT_RING - 1 < total)
    def _():
        for j in range(n_streams):
            qvt_copy(j, step + QVT_RING - 1).start()

    def split(j):
        qvt_copy(j, step).wait()
        qvt = ring.at[j, step % QVT_RING]
        return streams[j][0], qvt.at[0:D_MLSTM, :], qvt.at[D_MLSTM:, :], streams[j][2]

    fwd_in = split(0)
    bwd_in = fwd_in if by_sequence else split(1)

    if carried:
        @pl.when(c == 0)
        def _():
            if has_init:
                s_ref[...] = s0_ref[0]
                m_ref[...] = m0_ref[0]
            else:
                s_ref[...] = jnp.zeros_like(s_ref)
                m_ref[...] = jnp.zeros_like(m_ref)

    si = lax.broadcasted_iota(jnp.int32, (SCAN, SCAN), 0)
    ji = lax.broadcasted_iota(jnp.int32, (SCAN, SCAN), 1)
    dirs = ((fwd_in, hf_ref, si <= ji), (bwd_in, hb_ref, si >= ji))

    def chunk_cols(d, r):
        pos = rounds - 1 - r if (d == 1 and not by_sequence) else r
        return slice(pos * SCAN, (pos + 1) * SCAN)

    kq = {}
    for r in range(rounds):
        for d, ((k_ref, qt_ref, _, _), _, _) in enumerate(dirs):
            cs = chunk_cols(d, r)
            for hd in range(HEADS):
                hs = slice(hd * HD, (hd + 1) * HD)
                kq[r, d, hd] = _dot(k_ref[cs, hs], qt_ref[hs, cs])

    m_cur = None
    if carried:
        m_all = m_ref[...]
        m_cur = [m_all[u:u + 1] for u in range(N_UNIT)]
    zero_row = jnp.zeros((1, SCAN), F32)
    for r in range(rounds):
        slot = r * N_UNIT if by_sequence else 0
        for d, ((k_ref, qt_ref, vt_ref, gr_ref), h_ref, mask) in enumerate(dirs):
            cs = chunk_cols(d, r)
            for hd in range(HEADS):
                u = d * HEADS + hd
                row = d * 2 * HEADS + HEADS + hd
                hs = slice(hd * HD, (hd + 1) * HD)
                st = s_ref[slot + u] if carried else None
                qs = _dot(st.astype(BF16), qt_ref[hs, cs]) if carried else None
                h, st_new, m_new = _mlstm_unit(
                    k_ref[cs, hs], vt_ref[hs, cs], kq[r, d, hd], qs,
                    *(gr_ref[sec * N_GATE + row:sec * N_GATE + row + 1, cs]
                      for sec in (4, 5, 0, 1, 2, 3)),
                    st, m_cur[u] if carried else zero_row, mask, carried)
                h_ref[hs, cs] = h_ref[hs, cs] + h if (by_sequence and d == 1) else h
                s_ref[slot + u] = st_new
                if carried:
                    m_cur[u] = m_new
                else:
                    m_ref[slot + u:slot + u + 1, :] = m_new
    if carried:
        m_ref[...] = jnp.concatenate(m_cur, axis=0)

    if has_out:
        @pl.when(c == nc - 1)
        def _():
            for j in range(s_ref.shape[0]):
                q, u = divmod(j, N_UNIT)
                co_ref[q, u] = jnp.transpose(s_ref[j, 0:HD, :])
                no_ref[q, u:u + 1, :] = s_ref[j, HD:HD + 1, :]
                mo_ref[q, u:u + 1, :] = m_ref[j:j + 1, :]


def _mlstm(ak, qvt, gr, batch, rounds, s0=None, m0=None, want_state=False):
    n = ak.shape[0]
    nc = n // batch // SCAN
    by_sequence = nc == 1
    width = rounds * SCAN
    has_init = s0 is not None
    if by_sequence:
        assert batch % rounds == 0 and not has_init
        grid = (batch // rounds, 1)
        slots = rounds
        fwd = bwd = lambda b, c: b
    else:
        assert nc % rounds == 0 and not want_state
        steps = nc // rounds
        grid = (batch, steps)
        slots = 1
        fwd = lambda b, c: b * steps + c
        bwd = lambda b, c: b * steps + steps - 1 - c

    def specs(ix):
        return [
            pl.BlockSpec((width, D_MLSTM), lambda b, c: (ix(b, c), 1)),
            pl.BlockSpec(memory_space=pl.ANY),
            pl.BlockSpec((GR_ROWS, width), lambda b, c: (0, ix(b, c))),
        ]

    assert grid[0] * grid[1] >= QVT_RING - 1

    in_specs = specs(fwd) if by_sequence else specs(fwd) + specs(bwd)
    args = [ak, qvt, gr] if by_sequence else [ak, qvt, gr, ak, qvt, gr]
    if has_init:
        in_specs = [
            pl.BlockSpec((1, N_UNIT, ST_ROWS, HD), lambda b, c: (b, 0, 0, 0)),
            pl.BlockSpec((1, N_UNIT, SCAN), lambda b, c: (b, 0, 0)),
        ] + in_specs
        args = [s0, m0] + args
    n_h = 1 if by_sequence else N_DIR
    out_shape = [jax.ShapeDtypeStruct((D_MLSTM, n), F32)] * n_h
    out_specs = [pl.BlockSpec((D_MLSTM, width), lambda b, c, ix=ix: (0, ix(b, c)))
                 for ix in (fwd, bwd)[:n_h]]
    if want_state:
        out_shape += [
            jax.ShapeDtypeStruct((batch, N_UNIT, HD, HD), F32),
            jax.ShapeDtypeStruct((batch, N_UNIT, HD), F32),
            jax.ShapeDtypeStruct((batch, N_UNIT, SCAN), F32),
        ]
        out_specs += [
            pl.BlockSpec((slots, N_UNIT, HD, HD), lambda b, c: (b, 0, 0, 0)),
            pl.BlockSpec((slots, N_UNIT, HD), lambda b, c: (b, 0, 0)),
            pl.BlockSpec((slots, N_UNIT, SCAN), lambda b, c: (b, 0, 0)),
        ]
    outs = pl.pallas_call(
        functools.partial(_mlstm_kernel, has_init=has_init, has_out=want_state, rounds=rounds,
                          by_sequence=by_sequence),
        out_shape=out_shape,
        grid=grid,
        in_specs=in_specs,
        out_specs=out_specs,
        scratch_shapes=[pltpu.VMEM((slots * N_UNIT, ST_ROWS, HD), F32),
                        pltpu.VMEM((slots * N_UNIT, SCAN), F32),
                        pltpu.VMEM((n_h, QVT_RING, 2 * D_MLSTM, width), BF16),
                        pltpu.SemaphoreType.DMA((n_h, QVT_RING))],
        compiler_params=pltpu.CompilerParams(
            dimension_semantics=("arbitrary", "arbitrary"), vmem_limit_bytes=VMEM_LIMIT_BYTES),
        name="mlstm",
    )(*args)
    return outs[:n_h], outs[n_h:]


def _outffn_mix(x, a, hst, o, mod, c, x1_ref, h2_ref):
    ga1 = mod[:, 0:D_MODEL]
    sh2 = mod[:, D_MODEL:2 * D_MODEL]
    sc2 = mod[:, 2 * D_MODEL:3 * D_MODEL]
    hs = jnp.transpose(hst)
    sig = _sigmoid(o)
    parts = []
    for hd in range(HEADS):
        sl = slice(hd * HD, (hd + 1) * HD)
        parts.append(_rms(hs[:, sl], c["gh"][:, sl]) * sig[:, sl])
    b_out = jnp.concatenate(parts, axis=-1).astype(BF16)
    mix = _dot(a, c["wout"][0:D_GMLP, :]) + _dot(b_out, c["wout"][D_GMLP:, :])
    x1 = x + ga1 * mix
    x1_ref[...] = x1
    h2_ref[...] = (_rms(x1, c["g2"][...] * (1.0 + sc2)) + sh2).astype(BF16)


def _outffn_up(h2_ref, c, f_ref):
    h2 = h2_ref[...]
    u = _dot(h2, c["w1"][...])
    g = _dot(h2, c["w3"][...])
    f_ref[...] = (u * _sigmoid(u) * g).astype(BF16)


def _outffn_down(f_ref, x1_ref, mod, c, y_ref):
    ga2 = mod[:, 3 * D_MODEL:4 * D_MODEL]
    x2 = x1_ref[...] + ga2 * _dot(f_ref[...], c["w2"][...])
    y_ref[...] = _rms(x2, c["gf"][...])


def _outffn_kernel(x_ref, a_ref, o_ref, mod_ref, gh_ref, wout_hbm,
                   g2_ref, w1_hbm, w3_hbm, w2_hbm, gf_ref, *refs, n_h, blocks_per_mod, mod_row0):
    ht_refs = refs[:n_h]
    y_ref, x1_s, h2_s, f_s, wout_s, w1_s, w3_s, w2_s, w_sem = refs[n_h:]
    tb = x_ref.shape[0] // 2
    i = pl.program_id(0)
    mod_in = mod_row0 + jnp.minimum(i, pl.num_programs(0) - 2) // blocks_per_mod
    mod_out = mod_row0 + jnp.maximum(i - 1, 0) // blocks_per_mod
    c = {"gh": gh_ref, "wout": wout_s, "g2": g2_ref, "w1": w1_s, "w3": w3_s, "w2": w2_s,
         "gf": gf_ref}
    weight_copy = {
        name: pltpu.make_async_copy(src, dst, w_sem.at[j])
        for j, (name, src, dst) in enumerate((("wout", wout_hbm, wout_s), ("w1", w1_hbm, w1_s),
                                              ("w3", w3_hbm, w3_s), ("w2", w2_hbm, w2_s)))}

    def tick(half, mix, up, down):
        rows = slice(half * tb, (half + 1) * tb)
        cur, oth = half, 1 - half
        first_tick = mix and not up and not down
        second_tick = mix and up and not down
        if first_tick:
            for name in ("wout", "w1", "w3", "w2"):
                weight_copy[name].start()
            weight_copy["wout"].wait()
        if down:
            _outffn_down(f_s.at[cur], x1_s.at[cur], mod_ref[pl.ds(mod_out, 1), :], c,
                         y_ref.at[rows, :])
        if mix:
            hst = functools.reduce(lambda s, t: s + t, [ref[:, rows] for ref in ht_refs])
            _outffn_mix(x_ref[rows, :], a_ref[rows, :], hst, o_ref[rows, :],
                        mod_ref[pl.ds(mod_in, 1), :], c, x1_s.at[cur], h2_s.at[cur])
        if second_tick:
            weight_copy["w1"].wait()
            weight_copy["w3"].wait()
        if up:
            _outffn_up(h2_s.at[oth], c, f_s.at[oth])
        if second_tick:
            weight_copy["w2"].wait()

    _pipeline_step(tick)


def _outffn(x, ak, hts, o, mod, mod_row0, rows_per_mod, p):
    n = x.shape[0]
    tb = TOKEN_BLOCK
    tb2 = 2 * tb
    assert n % tb2 == 0 and rows_per_mod % tb2 == 0
    nb = n // tb2
    d_ff = p["w1"].shape[1]
    blk_in = lambda i: jnp.minimum(i, nb - 1)
    blk_out = lambda i: jnp.maximum(i - 1, 0)
    tok = lambda w: pl.BlockSpec((tb2, w), lambda i: (blk_in(i), 0))
    tok_t = lambda h: pl.BlockSpec((h, tb2), lambda i: (0, blk_in(i)))
    weight_hbm = pl.BlockSpec(memory_space=pl.ANY)
    in_specs = [
        tok(D_MODEL), tok(D_GMLP), tok(D_MLSTM),
        _const_spec(mod.shape),
        _const_spec((1, D_MLSTM)),
        weight_hbm,
        _const_spec((1, D_MODEL)),
        weight_hbm,
        weight_hbm,
        weight_hbm,
        _const_spec((1, D_MODEL)),
    ] + [tok_t(D_MLSTM)] * len(hts)
    scratch_shapes = [
        pltpu.VMEM((2, tb, D_MODEL), F32),
        pltpu.VMEM((2, tb, D_MODEL), BF16),
        pltpu.VMEM((2, tb, d_ff), BF16),
        pltpu.VMEM(p["wout"].shape, BF16),
        pltpu.VMEM(p["w1"].shape, BF16),
        pltpu.VMEM(p["w3"].shape, BF16),
        pltpu.VMEM(p["w2"].shape, BF16),
        pltpu.SemaphoreType.DMA((4,)),
    ]
    return pl.pallas_call(
        functools.partial(_outffn_kernel, n_h=len(hts), blocks_per_mod=rows_per_mod // tb2,
                          mod_row0=mod_row0),
        out_shape=jax.ShapeDtypeStruct((n, D_MODEL), F32),
        grid=(nb + 1,),
        in_specs=in_specs,
        out_specs=pl.BlockSpec((tb2, D_MODEL), lambda i: (blk_out(i), 0)),
        scratch_shapes=scratch_shapes,
        compiler_params=pltpu.CompilerParams(
            dimension_semantics=("arbitrary",), vmem_limit_bytes=VMEM_LIMIT_BYTES),
        name="outffn",
    )(x, ak, o, mod, p["gh"], p["wout"], p["g2"], p["w1"], p["w3"], p["w2"], p["gf"], *hts)


def _layer_params(l, g_norm1, b_gate, w_s, b_s, g_v, conv_w, conv_b, g_h, g_norm2, g_final):
    return {
        "g1": g_norm1[l][None, :],
        "bgt": b_gate[l][:, None],
        "cw": conv_w[l], "cb": conv_b[l][None, :],
        "ws": w_s[l].astype(BF16),
        "bs": jnp.repeat(b_s[l].T, HD, axis=1),
        "gv": g_v[l].reshape(1, D_GMLP),
        "gh": g_h[l].reshape(1, D_MLSTM),
        "g2": g_norm2[l][None, :],
        "gf": g_final[None, :],
    }


_LATE_WEIGHTS = ("wout", "w1", "w3", "w2")


def _trunk_front(x, mod_row0, rows_per_mod, seg, p, late=None):
    cast, ada_tail = ((), None) if late is None else (
        tuple(late[0][name] for name in _LATE_WEIGHTS), late[1])
    ak, qvt, o, gr, *side = _inproj(x, p["mod_head"], mod_row0, rows_per_mod, seg, p, cast,
                                    ada_tail)
    if late is not None:
        p = dict(p, **dict(zip(_LATE_WEIGHTS, side[:-1])))
        p["mod_tail"] = side[-1]
    return (ak, qvt, o), gr, p


def _trunk_back(x, front, gr, mod_row0, rows_per_mod, batch, rounds, p, s0=None, m0=None,
                want_state=False):
    ak, qvt, o = front
    hts, state = _mlstm(ak, qvt, gr, batch, rounds, s0, m0, want_state)
    y = _outffn(x, ak, hts, o, p["mod_tail"], mod_row0, rows_per_mod, p)
    return y, state


def kernel(x_prompt, x_sample, state_C, state_n, state_m, c, c_ctx, w_ada, b_ada, g_norm1, w_in,
           b_gate, w_s, b_s, g_v, conv_w, conv_b, g_h, w_out, g_norm2, w1, w3, w2, g_final):
    bp, tp, d = x_prompt.shape
    bs_, ts, _ = x_sample.shape
    depth = w_in.shape[0]
    assert depth == 1, "final norm is fused into the layer's last kernel"
    xp = x_prompt.reshape(bp * tp, d)
    xs = x_sample.reshape(bs_ * ts, d)

    cs = jnp.zeros((8, d), F32).at[0].set(c_ctx).at[1:1 + bs_].set(c)
    new_c, new_n, new_m = [], [], []
    for l in range(depth):
        p = _layer_params(l, g_norm1, b_gate, w_s, b_s, g_v, conv_w, conv_b, g_h, g_norm2, g_final)
        late_f32 = {"wout": w_out[l], "w1": w1[l], "w3": w3[l], "w2": w2[l]}
        p["mod_head"], p["wit"] = _ada(cs, w_ada[l], b_ada[l][None, :], 2 * d, w_in[l].T)
        ada_tail = (cs, w_ada[l], b_ada[l][None, :], 2 * d)

        front_lat, gr_lat, p = _trunk_front(xs, 1, ts, GRID_W, p, (late_f32, ada_tail))
        front_ctx, gr_ctx, _ = _trunk_front(xp, 0, bp * tp, tp, p)

        xp, (c_ctx_out, n_ctx_out, m_ctx_out) = _trunk_back(
            xp, front_ctx, gr_ctx, 0, bp * tp, bp, CTX_SEQS_PER_STEP, p, want_state=True)
        new_c.append(c_ctx_out.reshape(bp, N_DIR, HEADS, HD, HD))
        new_n.append(n_ctx_out.reshape(bp, N_DIR, HEADS, HD))
        new_m.append((m_ctx_out[..., 0] * (1.0 / LOG2E)).reshape(bp, N_DIR, HEADS))

        s0 = jnp.concatenate(
            [jnp.swapaxes(state_C[:, l], -1, -2), state_n[:, l][..., None, :],
             jnp.zeros((bs_, N_DIR, HEADS, ST_ROWS - HD - 1, HD), F32)],
            axis=-2).reshape(bs_, N_UNIT, ST_ROWS, HD)
        m0 = jnp.broadcast_to((state_m[:, l] * LOG2E).reshape(bs_, N_UNIT, 1),
                              (bs_, N_UNIT, SCAN))
        xs, _ = _trunk_back(xs, front_lat, gr_lat, 1, ts, bs_, LAT_CHUNKS_PER_STEP, p, s0=s0, m0=m0)

    return (xp.reshape(bp, tp, d), xs.reshape(bs_, ts, d),
            jnp.stack(new_c, axis=1), jnp.stack(new_n, axis=1), jnp.stack(new_m, axis=1))
```

```python
import functools

import jax
import jax.numpy as jnp
from jax import lax
from jax.experimental import pallas as pl
from jax.experimental.pallas import tpu as pltpu

D_MODEL = 1024
D_GMLP = 512
D_MLSTM = 512
GROUPS = 4
HEADS = 4
HD = 128
CHUNK = 128
SCAN = 256
N_DIR = 2
N_UNIT = N_DIR * HEADS
GRID_W = 64
OFF_V, OFF_Q, OFF_VV, OFF_O, OFF_G = 512, 1024, 2048, 2560, 3072
D_IN = OFF_G + 2 * N_DIR * HEADS
EPS = 1e-6
NEG = -1e30
ST_ROWS = HD + 16
N_GATE = 2 * N_UNIT
GR_ROWS = 6 * N_GATE

TOKEN_BLOCK = 256
CTX_SEQS_PER_STEP = 2
LAT_CHUNKS_PER_STEP = 2
VMEM_LIMIT_BYTES = 56 * 1024 * 1024

F32 = jnp.float32
BF16 = jnp.bfloat16


def _rms(x, g):
    n = x.shape[-1]
    return x * lax.rsqrt(jnp.sum(x * x, axis=-1, keepdims=True) + n * EPS) * (g * n ** 0.5)


LOG2E = 1.4426950408889634
GELU_K = 0.7978845608028654
GELU_C = 0.044715


def _sigmoid(x):
    return 1.0 / (1.0 + jnp.exp2(x * (-LOG2E)))


def _gelu_tanh(x):
    t = (x * x) * (-2.0 * LOG2E * GELU_K * GELU_C) + (-2.0 * LOG2E * GELU_K)
    return x * (1.0 / (1.0 + jnp.exp2(x * t)))


def _log_sigmoid(x):
    return jnp.minimum(x, 0.0) - jnp.log(1.0 + jnp.exp(-jnp.abs(x)))


def _dot(a, b):
    return jnp.dot(a, b, preferred_element_type=F32)


def _dot_nt(a, b):
    return lax.dot_general(a, b, (((1,), (1,)), ((), ())), preferred_element_type=F32)


def _pipeline_step(tick):
    i = pl.program_id(0)
    last = pl.num_programs(0) - 1

    @pl.when(i == 0)
    def _():
        tick(0, True, False, False)
        tick(1, True, True, False)

    @pl.when(jnp.logical_and(i > 0, i < last))
    def _():
        tick(0, True, True, True)
        tick(1, True, True, True)

    @pl.when(i == last)
    def _():
        tick(0, False, True, True)
        tick(1, False, False, True)


def _const_spec(shape):
    zeros = (0,) * len(shape)
    return pl.BlockSpec(shape, lambda *_: zeros, pipeline_mode=pl.Buffered(1))


def _ada_kernel(c_ref, w_ref, b_ref, o_ref):
    c = c_ref[...]
    s = (c * _sigmoid(c)).astype(BF16)
    o_ref[...] = _dot(s, w_ref[...].astype(BF16)) + b_ref[...]


def _ada_and_round_kernel(c_ref, w_ref, b_ref, wf_ref, o_ref, wb_ref):
    _ada_kernel(c_ref, w_ref, b_ref, o_ref)

    @pl.when(pl.program_id(0) == 0)
    def _():
        wb_ref[...] = wf_ref[...].astype(BF16)


def _ada(cs, w_ada, b_ada, n, w_f32):
    rows, d = cs.shape
    tn = 1024
    return pl.pallas_call(
        _ada_and_round_kernel,
        out_shape=[jax.ShapeDtypeStruct((rows, n), F32), jax.ShapeDtypeStruct(w_f32.shape, BF16)],
        grid=(n // tn,),
        in_specs=[
            pl.BlockSpec((rows, d), lambda j: (0, 0)),
            pl.BlockSpec((d, tn), lambda j: (0, j)),
            pl.BlockSpec((1, tn), lambda j: (0, j)),
            _const_spec(w_f32.shape),
        ],
        out_specs=[pl.BlockSpec((rows, tn), lambda j: (0, j)),
                   pl.BlockSpec(w_f32.shape, lambda j: (0, 0))],
        compiler_params=pltpu.CompilerParams(
            dimension_semantics=("arbitrary",), vmem_limit_bytes=VMEM_LIMIT_BYTES),
        name="ada",
    )(cs, w_ada, b_ada, w_f32)


def _inproj_norm(x, mod, g1, hb_ref):
    sh1 = mod[:, 0:D_MODEL]
    sc1 = mod[:, D_MODEL:2 * D_MODEL]
    hb_ref[...] = (_rms(x, g1 * (1.0 + sc1)) + sh1).astype(BF16)


def _inproj_project(hb_ref, w, z):
    hb = hb_ref[...]
    z["u"][...] = _dot_nt(hb, w["u"][...])
    z["v"][...] = _dot_nt(hb, w["v"][...])
    z["qk"][...] = _dot_nt(hb, w["qk"][...])
    z["o"][...] = _dot_nt(hb, w["o"][...])
    z["vt"][...] = _dot_nt(w["vv"][...], hb).astype(BF16)
    z["gt"][...] = _dot_nt(w["gt"][...], hb)


def _inproj_finish(z, c, out, seg):
    tb = z["u"].shape[0]
    for g in range(GROUPS):
        gs = slice(g * HD, (g + 1) * HD)
        vg = _rms(_gelu_tanh(z["v"][:, gs]), c["gv"][:, gs]).astype(BF16)
        for ch in range(tb // CHUNK):
            cs = slice(ch * CHUNK, (ch + 1) * CHUNK)
            mixed = _dot(c["ws"][g], vg[cs]) + c["bs"][:, gs]
            out["a"][cs, gs] = (_gelu_tanh(z["u"][cs, gs]) * mixed).astype(BF16)

    zqk = z["qk"][...]
    pos = lax.broadcasted_iota(jnp.int32, (tb, 1), 0) % seg
    prev = jnp.where(pos != 0, pltpu.roll(zqk, 1, 0), 0.0)
    nxt = jnp.where(pos != seg - 1, pltpu.roll(zqk, tb - 1, 0), 0.0)
    cw = c["cw"]
    y = c["cb"][...] + prev * cw[0:1, :] + zqk * cw[1:2, :] + nxt * cw[2:3, :]
    y = y * _sigmoid(y)
    out["qt"][...] = jnp.transpose(y[:, :D_MLSTM]).astype(BF16)
    out["k"][...] = (y[:, D_MLSTM:] * (HD ** -0.5)).astype(BF16)
    out["vt"][...] = z["vt"][...]
    out["o"][...] = z["o"][...]

    assert tb == SCAN
    gt = z["gt"][...] + c["bgt"][...]
    gi = pltpu.roll(gt, HEADS, 0)
    lf = _log_sigmoid(gt)
    lane = lax.broadcasted_iota(jnp.int32, (N_GATE, HD), 1)
    tiles = []
    carry = jnp.zeros((N_GATE, HD), F32)
    for t in range(SCAN // HD):
        prefix = lf[:, t * HD:(t + 1) * HD]
        shift = 1
        while shift < HD:
            prefix = prefix + jnp.where(lane >= shift, pltpu.roll(prefix, shift, 1), 0.0)
            shift *= 2
        tiles.append(prefix + carry)
        carry = carry + jnp.broadcast_to(prefix[:, HD - 1:HD], (N_GATE, HD))
    prefix = jnp.concatenate(tiles, axis=1)
    b_last = jnp.concatenate([carry] * (SCAN // HD), axis=1)
    row_bwd = lax.broadcasted_iota(jnp.int32, (N_GATE, SCAN), 0) >= N_GATE // N_DIR
    b_row = jnp.where(row_bwd, b_last - prefix + lf, prefix)
    g_row = b_last - b_row + gi
    g_max = jnp.broadcast_to(jnp.max(g_row, axis=1, keepdims=True), (N_GATE, SCAN))
    d_row = (gi - b_row) * LOG2E
    lane = lax.broadcasted_iota(jnp.int32, (N_GATE, SCAN), 1)
    d_upto, d_from = d_row, d_row
    shift = 1
    while shift < SCAN:
        d_upto = jnp.maximum(d_upto, jnp.where(lane >= shift, pltpu.roll(d_upto, shift, 1), NEG))
        d_from = jnp.maximum(
            d_from, jnp.where(lane < SCAN - shift, pltpu.roll(d_from, SCAN - shift, 1), NEG))
        shift *= 2
    d_max = jnp.where(row_bwd, d_from, d_upto)
    for sec, stat in enumerate((b_row, g_row, b_last, g_max)):
        out["gr"][sec * N_GATE:(sec + 1) * N_GATE, :] = stat * LOG2E
    out["gr"][4 * N_GATE:5 * N_GATE, :] = d_row
    out["gr"][5 * N_GATE:6 * N_GATE, :] = d_max


_INPROJ_Z = ("u", "v", "qk", "o", "vt", "gt")


def _inproj_kernel(*refs, seg, blocks_per_mod, mod_row0, n_cast, ada_tail):
    refs = list(refs)
    take = lambda k: [refs.pop(0) for _ in range(k)]
    (x_ref, mod_ref, g1_ref, wit_ref,
     bgt_ref, cw_ref, cb_ref, ws_ref, bs_ref, gv_ref) = take(10)
    cast_in = take(n_cast)
    ada_in = take(3) if ada_tail else None
    ak_ref, qvt_ref, o_ref, gr_ref = take(4)
    cast_out = take(n_cast)
    ada_out = take(1)[0] if ada_tail else None
    hb_s, *z_s = refs
    tb = x_ref.shape[0] // 2
    mod_in = mod_row0 + jnp.minimum(pl.program_id(0), pl.num_programs(0) - 2) // blocks_per_mod
    w = {"u": wit_ref.at[0:OFF_V, :], "v": wit_ref.at[OFF_V:OFF_Q, :],
         "qk": wit_ref.at[OFF_Q:OFF_VV, :], "vv": wit_ref.at[OFF_VV:OFF_O, :],
         "o": wit_ref.at[OFF_O:OFF_G, :], "gt": wit_ref.at[OFF_G:D_IN, :]}
    c = {"gv": gv_ref, "ws": ws_ref, "bs": bs_ref, "cw": cw_ref, "cb": cb_ref, "bgt": bgt_ref}

    def tick(half, norm, project, finish):
        rows = slice(half * tb, (half + 1) * tb)
        cur, oth = half, 1 - half
        z_cur = {name: ref.at[cur] for name, ref in zip(_INPROJ_Z, z_s)}
        z_oth = {name: ref.at[oth] for name, ref in zip(_INPROJ_Z, z_s)}
        out = {"a": ak_ref.at[rows, 0:D_GMLP], "k": ak_ref.at[rows, D_GMLP:],
               "qt": qvt_ref.at[0:D_MLSTM, rows], "vt": qvt_ref.at[D_MLSTM:, rows],
               "o": o_ref.at[rows, :], "gr": gr_ref.at[:, rows]}
        if project:
            _inproj_project(hb_s.at[oth], w, z_oth)
        if norm:
            _inproj_norm(x_ref[rows, :], mod_ref[pl.ds(mod_in, 1), :], g1_ref[...], hb_s.at[cur])
        if finish:
            _inproj_finish(z_cur, c, out, seg)
        if half == 0:
            for src, dst in zip(cast_in, cast_out):
                dst[...] = src[...].astype(BF16)
            if ada_tail:
                _ada_kernel(*ada_in, ada_out)

    _pipeline_step(tick)


def _inproj(x, mod, mod_row0, rows_per_mod, seg, p, cast=(), ada_tail=None):
    n = x.shape[0]
    tb = TOKEN_BLOCK
    tb2 = 2 * tb
    assert tb % seg == 0 and n % tb2 == 0 and rows_per_mod % tb2 == 0
    nb = n // tb2
    blk_in = lambda i: jnp.minimum(i, nb - 1)
    blk_out = lambda i: jnp.maximum(i - 1, 0)
    tok = lambda width: pl.BlockSpec((tb2, width), lambda i: (blk_out(i), 0))
    tok_t = lambda height: pl.BlockSpec((height, tb2), lambda i: (0, blk_out(i)))
    in_specs = [
        pl.BlockSpec((tb2, D_MODEL), lambda i: (blk_in(i), 0)),
        _const_spec(mod.shape),
        _const_spec((1, D_MODEL)),
        _const_spec((D_IN, D_MODEL)),
        _const_spec((N_GATE, 1)),
        _const_spec((3, 2 * D_MLSTM)),
        _const_spec((1, 2 * D_MLSTM)),
        _const_spec((GROUPS, CHUNK, CHUNK)),
        _const_spec((CHUNK, D_GMLP)),
        _const_spec((1, D_GMLP)),
    ]
    out_shape = [
        jax.ShapeDtypeStruct((n, D_GMLP + D_MLSTM), BF16),
        jax.ShapeDtypeStruct((2 * D_MLSTM, n), BF16),
        jax.ShapeDtypeStruct((n, D_MLSTM), F32),
        jax.ShapeDtypeStruct((GR_ROWS, n), F32),
    ]
    out_specs = [tok(D_GMLP + D_MLSTM), tok_t(2 * D_MLSTM), tok(D_MLSTM), tok_t(GR_ROWS)]
    for wf in cast:
        rows = wf.shape[0] // nb
        assert rows * nb == wf.shape[0] and rows % 16 == 0
        slab = pl.BlockSpec((rows, wf.shape[1]), lambda i: (blk_in(i), 0))
        in_specs.append(slab)
        out_specs.append(slab)
        out_shape.append(jax.ShapeDtypeStruct(wf.shape, BF16))
    side = list(cast)
    if ada_tail is not None:
        cs, w_ada, b_ada, col0 = ada_tail
        width = (w_ada.shape[1] - col0) // nb
        assert col0 % width == 0 and width * nb == w_ada.shape[1] - col0 and width % 128 == 0
        in_specs += [
            _const_spec(cs.shape),
            pl.BlockSpec((w_ada.shape[0], width), lambda i: (0, col0 // width + blk_in(i))),
            pl.BlockSpec((1, width), lambda i: (0, col0 // width + blk_in(i))),
        ]
        out_specs.append(pl.BlockSpec((cs.shape[0], width), lambda i: (0, blk_in(i))))
        out_shape.append(jax.ShapeDtypeStruct((cs.shape[0], w_ada.shape[1] - col0), F32))
        side += [cs, w_ada, b_ada]
    scratch_shapes = [
        pltpu.VMEM((2, tb, D_MODEL), BF16),
        pltpu.VMEM((2, tb, D_GMLP), F32),
        pltpu.VMEM((2, tb, D_GMLP), F32),
        pltpu.VMEM((2, tb, 2 * D_MLSTM), F32),
        pltpu.VMEM((2, tb, D_MLSTM), F32),
        pltpu.VMEM((2, D_MLSTM, tb), BF16),
        pltpu.VMEM((2, N_GATE, tb), F32),
    ]
    return pl.pallas_call(
        functools.partial(_inproj_kernel, seg=seg, blocks_per_mod=rows_per_mod // tb2,
                          mod_row0=mod_row0, n_cast=len(cast), ada_tail=ada_tail is not None),
        out_shape=out_shape,
        grid=(nb + 1,),
        in_specs=in_specs,
        out_specs=out_specs,
        scratch_shapes=scratch_shapes,
        compiler_params=pltpu.CompilerParams(
            dimension_semantics=("arbitrary",), vmem_limit_bytes=VMEM_LIMIT_BYTES),
        name="inproj",
    )(x, mod, p["g1"], p["wit"],
      p["bgt"], p["cw"], p["cb"], p["ws"], p["bs"], p["gv"], *side)


def _lane_broadcast_column(row):
    n = row.shape[1]
    tiles = [jnp.transpose(jnp.broadcast_to(row[:, t * HD:(t + 1) * HD], (HD, HD)))
             for t in range(n // HD)]
    col = jnp.concatenate(tiles, axis=0)
    return jnp.concatenate([col] * (n // HD), axis=1)


def _mlstm_unit(k, vt, kq, qs, d_row, dmax_row, b_row, g_row, bl_row, gmax_row, st, m_row, mask,
                carried):
    logw = jnp.where(mask, _lane_broadcast_column(d_row) + b_row, NEG)
    a = b_row + m_row
    mj = jnp.maximum(a, dmax_row + b_row)
    w = jnp.exp2(logw - mj)
    s = kq * w
    num = _dot(vt, s.astype(BF16))
    den = jnp.sum(s, axis=0, keepdims=True)
    if carried:
        inter = jnp.exp2(a - mj)
        num = num + inter * qs[:HD]
        den = den + inter * qs[HD:HD + 1]
    h = num * (1.0 / jnp.maximum(jnp.abs(den), jnp.exp2(-mj)))
    m_new = jnp.maximum(bl_row + m_row, gmax_row)
    wc = jnp.exp2(g_row - m_new)
    pad_row = lax.broadcasted_iota(jnp.int32, (ST_ROWS - HD, wc.shape[1]), 0)
    v_aug = jnp.concatenate(
        [vt.astype(F32) * wc, jnp.where(pad_row == 0, wc, 0.0)], axis=0).astype(BF16)
    st_new = _dot(v_aug, k)
    if carried:
        decay = jnp.exp2(bl_row + m_row - m_new)
        st_new = decay[:, :HD] * st + st_new
    return h, st_new, m_new


def _mlstm_kernel(*refs, has_init, has_out, rounds, by_sequence):
    if has_init:
        s0_ref, m0_ref = refs[:2]
        refs = refs[2:]
    def split(k_ref, qvt_ref, gr_ref):
        return k_ref, qvt_ref.at[0:D_MLSTM, :], qvt_ref.at[D_MLSTM:, :], gr_ref

    if by_sequence:
        fwd_in = bwd_in = split(*refs[:3])
        refs = refs[3:]
    else:
        fwd_in, bwd_in = split(*refs[:3]), split(*refs[3:6])
        refs = refs[6:]
    if by_sequence:
        hf_ref = hb_ref = refs[0]
        refs = refs[1:]
    else:
        hf_ref, hb_ref = refs[:2]
        refs = refs[2:]
    if has_out:
        co_ref, no_ref, mo_ref = refs[:3]
        refs = refs[3:]
    s_ref, m_ref = refs
    c = pl.program_id(1)
    nc = pl.num_programs(1)
    carried = not by_sequence

    if carried:
        @pl.when(c == 0)
        def _():
            if has_init:
                s_ref[...] = s0_ref[0]
                m_ref[...] = m0_ref[0]
            else:
                s_ref[...] = jnp.zeros_like(s_ref)
                m_ref[...] = jnp.zeros_like(m_ref)

    si = lax.broadcasted_iota(jnp.int32, (SCAN, SCAN), 0)
    ji = lax.broadcasted_iota(jnp.int32, (SCAN, SCAN), 1)
    dirs = ((fwd_in, hf_ref, si <= ji), (bwd_in, hb_ref, si >= ji))

    def chunk_cols(d, r):
        pos = rounds - 1 - r if (d == 1 and not by_sequence) else r
        return slice(pos * SCAN, (pos + 1) * SCAN)

    kq = {}
    for r in range(rounds):
        for d, ((k_ref, qt_ref, _, _), _, _) in enumerate(dirs):
            cs = chunk_cols(d, r)
            for hd in range(HEADS):
                hs = slice(hd * HD, (hd + 1) * HD)
                kq[r, d, hd] = _dot(k_ref[cs, hs], qt_ref[hs, cs])

    m_cur = None
    if carried:
        m_all = m_ref[...]
        m_cur = [m_all[u:u + 1] for u in range(N_UNIT)]
    zero_row = jnp.zeros((1, SCAN), F32)
    for r in range(rounds):
        slot = r * N_UNIT if by_sequence else 0
        for d, ((k_ref, qt_ref, vt_ref, gr_ref), h_ref, mask) in enumerate(dirs):
            cs = chunk_cols(d, r)
            for hd in range(HEADS):
                u = d * HEADS + hd
                row = d * 2 * HEADS + HEADS + hd
                hs = slice(hd * HD, (hd + 1) * HD)
                st = s_ref[slot + u] if carried else None
                qs = _dot(st.astype(BF16), qt_ref[hs, cs]) if carried else None
                h, st_new, m_new = _mlstm_unit(
                    k_ref[cs, hs], vt_ref[hs, cs], kq[r, d, hd], qs,
                    *(gr_ref[sec * N_GATE + row:sec * N_GATE + row + 1, cs]
                      for sec in (4, 5, 0, 1, 2, 3)),
                    st, m_cur[u] if carried else zero_row, mask, carried)
                h_ref[hs, cs] = h_ref[hs, cs] + h if (by_sequence and d == 1) else h
                s_ref[slot + u] = st_new
                if carried:
                    m_cur[u] = m_new
                else:
                    m_ref[slot + u:slot + u + 1, :] = m_new
    if carried:
        m_ref[...] = jnp.concatenate(m_cur, axis=0)

    if has_out:
        @pl.when(c == nc - 1)
        def _():
            for j in range(s_ref.shape[0]):
                q, u = divmod(j, N_UNIT)
                co_ref[q, u] = jnp.transpose(s_ref[j, 0:HD, :])
                no_ref[q, u:u + 1, :] = s_ref[j, HD:HD + 1, :]
                mo_ref[q, u:u + 1, :] = m_ref[j:j + 1, :]


def _mlstm(ak, qvt, gr, batch, rounds, s0=None, m0=None, want_state=False):
    n = ak.shape[0]
    nc = n // batch // SCAN
    by_sequence = nc == 1
    width = rounds * SCAN
    has_init = s0 is not None
    if by_sequence:
        assert batch % rounds == 0 and not has_init
        grid = (batch // rounds, 1)
        slots = rounds
        fwd = bwd = lambda b, c: b
    else:
        assert nc % rounds == 0 and not want_state
        steps = nc // rounds
        grid = (batch, steps)
        slots = 1
        fwd = lambda b, c: b * steps + c
        bwd = lambda b, c: b * steps + steps - 1 - c

    def specs(ix):
        return [
            pl.BlockSpec((width, D_MLSTM), lambda b, c: (ix(b, c), 1)),
            pl.BlockSpec((2 * D_MLSTM, width), lambda b, c: (0, ix(b, c))),
            pl.BlockSpec((GR_ROWS, width), lambda b, c: (0, ix(b, c))),
        ]

    in_specs = specs(fwd) if by_sequence else specs(fwd) + specs(bwd)
    args = [ak, qvt, gr] if by_sequence else [ak, qvt, gr, ak, qvt, gr]
    if has_init:
        in_specs = [
            pl.BlockSpec((1, N_UNIT, ST_ROWS, HD), lambda b, c: (b, 0, 0, 0)),
            pl.BlockSpec((1, N_UNIT, SCAN), lambda b, c: (b, 0, 0)),
        ] + in_specs
        args = [s0, m0] + args
    n_h = 1 if by_sequence else N_DIR
    out_shape = [jax.ShapeDtypeStruct((D_MLSTM, n), F32)] * n_h
    out_specs = [pl.BlockSpec((D_MLSTM, width), lambda b, c, ix=ix: (0, ix(b, c)))
                 for ix in (fwd, bwd)[:n_h]]
    if want_state:
        out_shape += [
            jax.ShapeDtypeStruct((batch, N_UNIT, HD, HD), F32),
            jax.ShapeDtypeStruct((batch, N_UNIT, HD), F32),
            jax.ShapeDtypeStruct((batch, N_UNIT, SCAN), F32),
        ]
        out_specs += [
            pl.BlockSpec((slots, N_UNIT, HD, HD), lambda b, c: (b, 0, 0, 0)),
            pl.BlockSpec((slots, N_UNIT, HD), lambda b, c: (b, 0, 0)),
            pl.BlockSpec((slots, N_UNIT, SCAN), lambda b, c: (b, 0, 0)),
        ]
    outs = pl.pallas_call(
        functools.partial(_mlstm_kernel, has_init=has_init, has_out=want_state, rounds=rounds,
                          by_sequence=by_sequence),
        out_shape=out_shape,
        grid=grid,
        in_specs=in_specs,
        out_specs=out_specs,
        scratch_shapes=[pltpu.VMEM((slots * N_UNIT, ST_ROWS, HD), F32),
                        pltpu.VMEM((slots * N_UNIT, SCAN), F32)],
        compiler_params=pltpu.CompilerParams(
            dimension_semantics=("arbitrary", "arbitrary"), vmem_limit_bytes=VMEM_LIMIT_BYTES),
        name="mlstm",
    )(*args)
    return outs[:n_h], outs[n_h:]


def _outffn_mix(x, a, hst, o, mod, c, x1_ref, h2_ref):
    ga1 = mod[:, 0:D_MODEL]
    sh2 = mod[:, D_MODEL:2 * D_MODEL]
    sc2 = mod[:, 2 * D_MODEL:3 * D_MODEL]
    hs = jnp.transpose(hst)
    sig = _sigmoid(o)
    parts = []
    for hd in range(HEADS):
        sl = slice(hd * HD, (hd + 1) * HD)
        parts.append(_rms(hs[:, sl], c["gh"][:, sl]) * sig[:, sl])
    b_out = jnp.concatenate(parts, axis=-1).astype(BF16)
    mix = _dot(a, c["wout"][0:D_GMLP, :]) + _dot(b_out, c["wout"][D_GMLP:, :])
    x1 = x + ga1 * mix
    x1_ref[...] = x1
    h2_ref[...] = (_rms(x1, c["g2"][...] * (1.0 + sc2)) + sh2).astype(BF16)


def _outffn_up(h2_ref, c, f_ref, before_gate_weight=None):
    h2 = h2_ref[...]
    u = _dot(h2, c["w1"][...])
    if before_gate_weight is not None:
        before_gate_weight()
    g = _dot(h2, c["w3"][...])
    f_ref[...] = (u * _sigmoid(u) * g).astype(BF16)


def _outffn_down(f_ref, x1_ref, mod, c, y_ref):
    ga2 = mod[:, 3 * D_MODEL:4 * D_MODEL]
    x2 = x1_ref[...] + ga2 * _dot(f_ref[...], c["w2"][...])
    y_ref[...] = _rms(x2, c["gf"][...])


def _outffn_kernel(x_ref, a_ref, o_ref, mod_ref, gh_ref, wout_hbm,
                   g2_ref, w1_hbm, w3_hbm, w2_hbm, gf_ref, *refs, n_h, blocks_per_mod, mod_row0):
    ht_refs = refs[:n_h]
    y_ref, x1_s, h2_s, f_s, wout_s, w1_s, w3_s, w2_s, w_sem = refs[n_h:]
    tb = x_ref.shape[0] // 2
    i = pl.program_id(0)
    mod_in = mod_row0 + jnp.minimum(i, pl.num_programs(0) - 2) // blocks_per_mod
    mod_out = mod_row0 + jnp.maximum(i - 1, 0) // blocks_per_mod
    c = {"gh": gh_ref, "wout": wout_s, "g2": g2_ref, "w1": w1_s, "w3": w3_s, "w2": w2_s,
         "gf": gf_ref}
    weight_copy = {
        name: pltpu.make_async_copy(src, dst, w_sem.at[j])
        for j, (name, src, dst) in enumerate((("wout", wout_hbm, wout_s), ("w1", w1_hbm, w1_s),
                                              ("w3", w3_hbm, w3_s), ("w2", w2_hbm, w2_s)))}

    def tick(half, mix, up, down):
        rows = slice(half * tb, (half + 1) * tb)
        cur, oth = half, 1 - half
        first_tick = mix and not up and not down
        second_tick = mix and up and not down
        if first_tick:
            for name in ("wout", "w1", "w3", "w2"):
                weight_copy[name].start()
            weight_copy["wout"].wait()
        if down:
            _outffn_down(f_s.at[cur], x1_s.at[cur], mod_ref[pl.ds(mod_out, 1), :], c,
                         y_ref.at[rows, :])
        if mix:
            hst = functools.reduce(lambda s, t: s + t, [ref[:, rows] for ref in ht_refs])
            _outffn_mix(x_ref[rows, :], a_ref[rows, :], hst, o_ref[rows, :],
                        mod_ref[pl.ds(mod_in, 1), :], c, x1_s.at[cur], h2_s.at[cur])
        if second_tick:
            weight_copy["w1"].wait()
        if up:
            _outffn_up(h2_s.at[oth], c, f_s.at[oth],
                       weight_copy["w3"].wait if second_tick else None)
        if second_tick:
            weight_copy["w2"].wait()

    _pipeline_step(tick)


def _outffn(x, ak, hts, o, mod, mod_row0, rows_per_mod, p):
    n = x.shape[0]
    tb = TOKEN_BLOCK
    tb2 = 2 * tb
    assert n % tb2 == 0 and rows_per_mod % tb2 == 0
    nb = n // tb2
    d_ff = p["w1"].shape[1]
    blk_in = lambda i: jnp.minimum(i, nb - 1)
    blk_out = lambda i: jnp.maximum(i - 1, 0)
    tok = lambda w: pl.BlockSpec((tb2, w), lambda i: (blk_in(i), 0))
    tok_t = lambda h: pl.BlockSpec((h, tb2), lambda i: (0, blk_in(i)))
    weight_hbm = pl.BlockSpec(memory_space=pl.ANY)
    in_specs = [
        tok(D_MODEL), tok(D_GMLP), tok(D_MLSTM),
        _const_spec(mod.shape),
        _const_spec((1, D_MLSTM)),
        weight_hbm,
        _const_spec((1, D_MODEL)),
        weight_hbm,
        weight_hbm,
        weight_hbm,
        _const_spec((1, D_MODEL)),
    ] + [tok_t(D_MLSTM)] * len(hts)
    scratch_shapes = [
        pltpu.VMEM((2, tb, D_MODEL), F32),
        pltpu.VMEM((2, tb, D_MODEL), BF16),
        pltpu.VMEM((2, tb, d_ff), BF16),
        pltpu.VMEM(p["wout"].shape, BF16),
        pltpu.VMEM(p["w1"].shape, BF16),
        pltpu.VMEM(p["w3"].shape, BF16),
        pltpu.VMEM(p["w2"].shape, BF16),
        pltpu.SemaphoreType.DMA((4,)),
    ]
    return pl.pallas_call(
        functools.partial(_outffn_kernel, n_h=len(hts), blocks_per_mod=rows_per_mod // tb2,
                          mod_row0=mod_row0),
        out_shape=jax.ShapeDtypeStruct((n, D_MODEL), F32),
        grid=(nb + 1,),
        in_specs=in_specs,
        out_specs=pl.BlockSpec((tb2, D_MODEL), lambda i: (blk_out(i), 0)),
        scratch_shapes=scratch_shapes,
        compiler_params=pltpu.CompilerParams(
            dimension_semantics=("arbitrary",), vmem_limit_bytes=VMEM_LIMIT_BYTES),
        name="outffn",
    )(x, ak, o, mod, p["gh"], p["wout"], p["g2"], p["w1"], p["w3"], p["w2"], p["gf"], *hts)


def _layer_params(l, g_norm1, b_gate, w_s, b_s, g_v, conv_w, conv_b, g_h, g_norm2, g_final):
    return {
        "g1": g_norm1[l][None, :],
        "bgt": b_gate[l][:, None],
        "cw": conv_w[l], "cb": conv_b[l][None, :],
        "ws": w_s[l].astype(BF16),
        "bs": jnp.repeat(b_s[l].T, HD, axis=1),
        "gv": g_v[l].reshape(1, D_GMLP),
        "gh": g_h[l].reshape(1, D_MLSTM),
        "g2": g_norm2[l][None, :],
        "gf": g_final[None, :],
    }


_LATE_WEIGHTS = ("wout", "w1", "w3", "w2")


def _trunk_front(x, mod_row0, rows_per_mod, seg, p, late=None):
    cast, ada_tail = ((), None) if late is None else (
        tuple(late[0][name] for name in _LATE_WEIGHTS), late[1])
    ak, qvt, o, gr, *side = _inproj(x, p["mod_head"], mod_row0, rows_per_mod, seg, p, cast,
                                    ada_tail)
    if late is not None:
        p = dict(p, **dict(zip(_LATE_WEIGHTS, side[:-1])))
        p["mod_tail"] = side[-1]
    return (ak, qvt, o), gr, p


def _trunk_back(x, front, gr, mod_row0, rows_per_mod, batch, rounds, p, s0=None, m0=None,
                want_state=False):
    ak, qvt, o = front
    hts, state = _mlstm(ak, qvt, gr, batch, rounds, s0, m0, want_state)
    y = _outffn(x, ak, hts, o, p["mod_tail"], mod_row0, rows_per_mod, p)
    return y, state


def kernel(x_prompt, x_sample, state_C, state_n, state_m, c, c_ctx, w_ada, b_ada, g_norm1, w_in,
           b_gate, w_s, b_s, g_v, conv_w, conv_b, g_h, w_out, g_norm2, w1, w3, w2, g_final):
    bp, tp, d = x_prompt.shape
    bs_, ts, _ = x_sample.shape
    depth = w_in.shape[0]
    assert depth == 1, "final norm is fused into the layer's last kernel"
    xp = x_prompt.reshape(bp * tp, d)
    xs = x_sample.reshape(bs_ * ts, d)

    cs = jnp.zeros((8, d), F32).at[0].set(c_ctx).at[1:1 + bs_].set(c)
    new_c, new_n, new_m = [], [], []
    for l in range(depth):
        p = _layer_params(l, g_norm1, b_gate, w_s, b_s, g_v, conv_w, conv_b, g_h, g_norm2, g_final)
        late_f32 = {"wout": w_out[l], "w1": w1[l], "w3": w3[l], "w2": w2[l]}
        p["mod_head"], p["wit"] = _ada(cs, w_ada[l], b_ada[l][None, :], 2 * d, w_in[l].T)
        ada_tail = (cs, w_ada[l], b_ada[l][None, :], 2 * d)

        front_lat, gr_lat, p = _trunk_front(xs, 1, ts, GRID_W, p, (late_f32, ada_tail))
        front_ctx, gr_ctx, _ = _trunk_front(xp, 0, bp * tp, tp, p)

        xp, (c_ctx_out, n_ctx_out, m_ctx_out) = _trunk_back(
            xp, front_ctx, gr_ctx, 0, bp * tp, bp, CTX_SEQS_PER_STEP, p, want_state=True)
        new_c.append(c_ctx_out.reshape(bp, N_DIR, HEADS, HD, HD))
        new_n.append(n_ctx_out.reshape(bp, N_DIR, HEADS, HD))
        new_m.append((m_ctx_out[..., 0] * (1.0 / LOG2E)).reshape(bp, N_DIR, HEADS))

        s0 = jnp.concatenate(
            [jnp.swapaxes(state_C[:, l], -1, -2), state_n[:, l][..., None, :],
             jnp.zeros((bs_, N_DIR, HEADS, ST_ROWS - HD - 1, HD), F32)],
            axis=-2).reshape(bs_, N_UNIT, ST_ROWS, HD)
        m0 = jnp.broadcast_to((state_m[:, l] * LOG2E).reshape(bs_, N_UNIT, 1),
                              (bs_, N_UNIT, SCAN))
        xs, _ = _trunk_back(xs, front_lat, gr_lat, 1, ts, bs_, LAT_CHUNKS_PER_STEP, p, s0=s0, m0=m0)

    return (xp.reshape(bp, tp, d), xs.reshape(bs_, ts, d),
            jnp.stack(new_c, axis=1), jnp.stack(new_n, axis=1), jnp.stack(new_m, axis=1))
```

```python
import functools

import jax
import jax.numpy as jnp
from jax import lax
from jax.experimental import pallas as pl
from jax.experimental.pallas import tpu as pltpu

D_MODEL = 1024
D_GMLP = 512
D_MLSTM = 512
GROUPS = 4
HEADS = 4
HD = 128
CHUNK = 128
SCAN = 256
N_DIR = 2
N_UNIT = N_DIR * HEADS
GRID_W = 64
OFF_V, OFF_Q, OFF_VV, OFF_O, OFF_G = 512, 1024, 2048, 2560, 3072
D_IN = OFF_G + 2 * N_DIR * HEADS
EPS = 1e-6
NEG = -1e30
ST_ROWS = HD + 16
N_GATE = 2 * N_UNIT
GR_ROWS = 6 * N_GATE

TOKEN_BLOCK = 256
CTX_SEQS_PER_STEP = 2
LAT_CHUNKS_PER_STEP = 2
VMEM_LIMIT_BYTES = 56 * 1024 * 1024

F32 = jnp.float32
BF16 = jnp.bfloat16


def _rms(x, g):
    n = x.shape[-1]
    return x * lax.rsqrt(jnp.sum(x * x, axis=-1, keepdims=True) + n * EPS) * (g * n ** 0.5)


LOG2E = 1.4426950408889634
GELU_K = 0.7978845608028654
GELU_C = 0.044715


def _sigmoid(x):
    return 1.0 / (1.0 + jnp.exp2(x * (-LOG2E)))


def _gelu_tanh(x):
    t = (x * x) * (-2.0 * LOG2E * GELU_K * GELU_C) + (-2.0 * LOG2E * GELU_K)
    return x * (1.0 / (1.0 + jnp.exp2(x * t)))


def _log_sigmoid(x):
    return jnp.minimum(x, 0.0) - jnp.log(1.0 + jnp.exp(-jnp.abs(x)))


def _dot(a, b):
    return jnp.dot(a, b, preferred_element_type=F32)


def _dot_nt(a, b):
    return lax.dot_general(a, b, (((1,), (1,)), ((), ())), preferred_element_type=F32)


def _pipeline_step(tick):
    i = pl.program_id(0)
    last = pl.num_programs(0) - 1

    @pl.when(i == 0)
    def _():
        tick(0, True, False, False)
        tick(1, True, True, False)

    @pl.when(jnp.logical_and(i > 0, i < last))
    def _():
        tick(0, True, True, True)
        tick(1, True, True, True)

    @pl.when(i == last)
    def _():
        tick(0, False, True, True)
        tick(1, False, False, True)


def _const_spec(shape):
    zeros = (0,) * len(shape)
    return pl.BlockSpec(shape, lambda *_: zeros, pipeline_mode=pl.Buffered(1))


def _ada_kernel(c_ref, w_ref, b_ref, o_ref):
    c = c_ref[...]
    s = (c * _sigmoid(c)).astype(BF16)
    o_ref[...] = _dot(s, w_ref[...].astype(BF16)) + b_ref[...]


def _ada_and_round_kernel(c_ref, w_ref, b_ref, wf_ref, o_ref, wb_ref):
    _ada_kernel(c_ref, w_ref, b_ref, o_ref)

    @pl.when(pl.program_id(0) == 0)
    def _():
        wb_ref[...] = wf_ref[...].astype(BF16)


def _ada(cs, w_ada, b_ada, n, w_f32):
    rows, d = cs.shape
    tn = 1024
    return pl.pallas_call(
        _ada_and_round_kernel,
        out_shape=[jax.ShapeDtypeStruct((rows, n), F32), jax.ShapeDtypeStruct(w_f32.shape, BF16)],
        grid=(n // tn,),
        in_specs=[
            pl.BlockSpec((rows, d), lambda j: (0, 0)),
            pl.BlockSpec((d, tn), lambda j: (0, j)),
            pl.BlockSpec((1, tn), lambda j: (0, j)),
            _const_spec(w_f32.shape),
        ],
        out_specs=[pl.BlockSpec((rows, tn), lambda j: (0, j)),
                   pl.BlockSpec(w_f32.shape, lambda j: (0, 0))],
        compiler_params=pltpu.CompilerParams(
            dimension_semantics=("arbitrary",), vmem_limit_bytes=VMEM_LIMIT_BYTES),
        name="ada",
    )(cs, w_ada, b_ada, w_f32)


def _inproj_norm(x, mod, g1, hb_ref):
    sh1 = mod[:, 0:D_MODEL]
    sc1 = mod[:, D_MODEL:2 * D_MODEL]
    hb_ref[...] = (_rms(x, g1 * (1.0 + sc1)) + sh1).astype(BF16)


def _inproj_project(hb_ref, w, z):
    hb = hb_ref[...]
    z["u"][...] = _dot_nt(hb, w["u"][...])
    z["v"][...] = _dot_nt(hb, w["v"][...])
    z["qk"][...] = _dot_nt(hb, w["qk"][...])
    z["o"][...] = _dot_nt(hb, w["o"][...])
    z["vt"][...] = _dot_nt(w["vv"][...], hb).astype(BF16)
    z["gt"][...] = _dot_nt(w["gt"][...], hb)


def _inproj_finish(z, c, out, seg):
    tb = z["u"].shape[0]
    for g in range(GROUPS):
        gs = slice(g * HD, (g + 1) * HD)
        vg = _rms(_gelu_tanh(z["v"][:, gs]), c["gv"][:, gs]).astype(BF16)
        for ch in range(tb // CHUNK):
            cs = slice(ch * CHUNK, (ch + 1) * CHUNK)
            mixed = _dot(c["ws"][g], vg[cs]) + c["bs"][:, gs]
            out["a"][cs, gs] = (_gelu_tanh(z["u"][cs, gs]) * mixed).astype(BF16)

    zqk = z["qk"][...]
    pos = lax.broadcasted_iota(jnp.int32, (tb, 1), 0) % seg
    prev = jnp.where(pos != 0, pltpu.roll(zqk, 1, 0), 0.0)
    nxt = jnp.where(pos != seg - 1, pltpu.roll(zqk, tb - 1, 0), 0.0)
    cw = c["cw"]
    y = c["cb"][...] + prev * cw[0:1, :] + zqk * cw[1:2, :] + nxt * cw[2:3, :]
    y = y * _sigmoid(y)
    out["qt"][...] = jnp.transpose(y[:, :D_MLSTM]).astype(BF16)
    out["k"][...] = (y[:, D_MLSTM:] * (HD ** -0.5)).astype(BF16)
    out["vt"][...] = z["vt"][...]
    out["o"][...] = z["o"][...]

    assert tb == SCAN
    gt = z["gt"][...] + c["bgt"][...]
    gi = pltpu.roll(gt, HEADS, 0)
    lf = _log_sigmoid(gt)
    lane = lax.broadcasted_iota(jnp.int32, (N_GATE, HD), 1)
    tiles = []
    carry = jnp.zeros((N_GATE, HD), F32)
    for t in range(SCAN // HD):
        prefix = lf[:, t * HD:(t + 1) * HD]
        shift = 1
        while shift < HD:
            prefix = prefix + jnp.where(lane >= shift, pltpu.roll(prefix, shift, 1), 0.0)
            shift *= 2
        tiles.append(prefix + carry)
        carry = carry + jnp.broadcast_to(prefix[:, HD - 1:HD], (N_GATE, HD))
    prefix = jnp.concatenate(tiles, axis=1)
    b_last = jnp.concatenate([carry] * (SCAN // HD), axis=1)
    row_bwd = lax.broadcasted_iota(jnp.int32, (N_GATE, SCAN), 0) >= N_GATE // N_DIR
    b_row = jnp.where(row_bwd, b_last - prefix + lf, prefix)
    g_row = b_last - b_row + gi
    g_max = jnp.broadcast_to(jnp.max(g_row, axis=1, keepdims=True), (N_GATE, SCAN))
    d_row = (gi - b_row) * LOG2E
    lane = lax.broadcasted_iota(jnp.int32, (N_GATE, SCAN), 1)
    d_upto, d_from = d_row, d_row
    shift = 1
    while shift < SCAN:
        d_upto = jnp.maximum(d_upto, jnp.where(lane >= shift, pltpu.roll(d_upto, shift, 1), NEG))
        d_from = jnp.maximum(
            d_from, jnp.where(lane < SCAN - shift, pltpu.roll(d_from, SCAN - shift, 1), NEG))
        shift *= 2
    d_max = jnp.where(row_bwd, d_from, d_upto)
    for sec, stat in enumerate((b_row, g_row, b_last, g_max)):
        out["gr"][sec * N_GATE:(sec + 1) * N_GATE, :] = stat * LOG2E
    out["gr"][4 * N_GATE:5 * N_GATE, :] = d_row
    out["gr"][5 * N_GATE:6 * N_GATE, :] = d_max


_INPROJ_Z = ("u", "v", "qk", "o", "vt", "gt")


def _inproj_kernel(*refs, seg, blocks_per_mod, mod_row0, n_cast, ada_tail):
    refs = list(refs)
    take = lambda k: [refs.pop(0) for _ in range(k)]
    (x_ref, mod_ref, g1_ref, wit_ref,
     bgt_ref, cw_ref, cb_ref, ws_ref, bs_ref, gv_ref) = take(10)
    cast_in = take(n_cast)
    ada_in = take(3) if ada_tail else None
    ak_ref, qvt_ref, o_ref, gr_ref = take(4)
    cast_out = take(n_cast)
    ada_out = take(1)[0] if ada_tail else None
    hb_s, *z_s = refs
    tb = x_ref.shape[0] // 2
    mod_in = mod_row0 + jnp.minimum(pl.program_id(0), pl.num_programs(0) - 2) // blocks_per_mod
    w = {"u": wit_ref.at[0:OFF_V, :], "v": wit_ref.at[OFF_V:OFF_Q, :],
         "qk": wit_ref.at[OFF_Q:OFF_VV, :], "vv": wit_ref.at[OFF_VV:OFF_O, :],
         "o": wit_ref.at[OFF_O:OFF_G, :], "gt": wit_ref.at[OFF_G:D_IN, :]}
    c = {"gv": gv_ref, "ws": ws_ref, "bs": bs_ref, "cw": cw_ref, "cb": cb_ref, "bgt": bgt_ref}

    def tick(half, norm, project, finish):
        rows = slice(half * tb, (half + 1) * tb)
        cur, oth = half, 1 - half
        z_cur = {name: ref.at[cur] for name, ref in zip(_INPROJ_Z, z_s)}
        z_oth = {name: ref.at[oth] for name, ref in zip(_INPROJ_Z, z_s)}
        out = {"a": ak_ref.at[rows, 0:D_GMLP], "k": ak_ref.at[rows, D_GMLP:],
               "qt": qvt_ref.at[0:D_MLSTM, rows], "vt": qvt_ref.at[D_MLSTM:, rows],
               "o": o_ref.at[rows, :], "gr": gr_ref.at[:, rows]}
        if project:
            _inproj_project(hb_s.at[oth], w, z_oth)
        if norm:
            _inproj_norm(x_ref[rows, :], mod_ref[pl.ds(mod_in, 1), :], g1_ref[...], hb_s.at[cur])
        if finish:
            _inproj_finish(z_cur, c, out, seg)
        if half == 0:
            for src, dst in zip(cast_in, cast_out):
                dst[...] = src[...].astype(BF16)
            if ada_tail:
                _ada_kernel(*ada_in, ada_out)

    _pipeline_step(tick)


def _inproj(x, mod, mod_row0, rows_per_mod, seg, p, cast=(), ada_tail=None):
    n = x.shape[0]
    tb = TOKEN_BLOCK
    tb2 = 2 * tb
    assert tb % seg == 0 and n % tb2 == 0 and rows_per_mod % tb2 == 0
    nb = n // tb2
    blk_in = lambda i: jnp.minimum(i, nb - 1)
    blk_out = lambda i: jnp.maximum(i - 1, 0)
    tok = lambda width: pl.BlockSpec((tb2, width), lambda i: (blk_out(i), 0))
    tok_t = lambda height: pl.BlockSpec((height, tb2), lambda i: (0, blk_out(i)))
    in_specs = [
        pl.BlockSpec((tb2, D_MODEL), lambda i: (blk_in(i), 0)),
        _const_spec(mod.shape),
        _const_spec((1, D_MODEL)),
        _const_spec((D_IN, D_MODEL)),
        _const_spec((N_GATE, 1)),
        _const_spec((3, 2 * D_MLSTM)),
        _const_spec((1, 2 * D_MLSTM)),
        _const_spec((GROUPS, CHUNK, CHUNK)),
        _const_spec((CHUNK, D_GMLP)),
        _const_spec((1, D_GMLP)),
    ]
    out_shape = [
        jax.ShapeDtypeStruct((n, D_GMLP + D_MLSTM), BF16),
        jax.ShapeDtypeStruct((2 * D_MLSTM, n), BF16),
        jax.ShapeDtypeStruct((n, D_MLSTM), F32),
        jax.ShapeDtypeStruct((GR_ROWS, n), F32),
    ]
    out_specs = [tok(D_GMLP + D_MLSTM), tok_t(2 * D_MLSTM), tok(D_MLSTM), tok_t(GR_ROWS)]
    for wf in cast:
        rows = wf.shape[0] // nb
        assert rows * nb == wf.shape[0] and rows % 16 == 0
        slab = pl.BlockSpec((rows, wf.shape[1]), lambda i: (blk_in(i), 0))
        in_specs.append(slab)
        out_specs.append(slab)
        out_shape.append(jax.ShapeDtypeStruct(wf.shape, BF16))
    side = list(cast)
    if ada_tail is not None:
        cs, w_ada, b_ada, col0 = ada_tail
        width = (w_ada.shape[1] - col0) // nb
        assert col0 % width == 0 and width * nb == w_ada.shape[1] - col0 and width % 128 == 0
        in_specs += [
            _const_spec(cs.shape),
            pl.BlockSpec((w_ada.shape[0], width), lambda i: (0, col0 // width + blk_in(i))),
            pl.BlockSpec((1, width), lambda i: (0, col0 // width + blk_in(i))),
        ]
        out_specs.append(pl.BlockSpec((cs.shape[0], width), lambda i: (0, blk_in(i))))
        out_shape.append(jax.ShapeDtypeStruct((cs.shape[0], w_ada.shape[1] - col0), F32))
        side += [cs, w_ada, b_ada]
    scratch_shapes = [
        pltpu.VMEM((2, tb, D_MODEL), BF16),
        pltpu.VMEM((2, tb, D_GMLP), F32),
        pltpu.VMEM((2, tb, D_GMLP), F32),
        pltpu.VMEM((2, tb, 2 * D_MLSTM), F32),
        pltpu.VMEM((2, tb, D_MLSTM), F32),
        pltpu.VMEM((2, D_MLSTM, tb), BF16),
        pltpu.VMEM((2, N_GATE, tb), F32),
    ]
    return pl.pallas_call(
        functools.partial(_inproj_kernel, seg=seg, blocks_per_mod=rows_per_mod // tb2,
                          mod_row0=mod_row0, n_cast=len(cast), ada_tail=ada_tail is not None),
        out_shape=out_shape,
        grid=(nb + 1,),
        in_specs=in_specs,
        out_specs=out_specs,
        scratch_shapes=scratch_shapes,
        compiler_params=pltpu.CompilerParams(
            dimension_semantics=("arbitrary",), vmem_limit_bytes=VMEM_LIMIT_BYTES),
        name="inproj",
    )(x, mod, p["g1"], p["wit"],
      p["bgt"], p["cw"], p["cb"], p["ws"], p["bs"], p["gv"], *side)


def _lane_broadcast_column(row):
    n = row.shape[1]
    tiles = [jnp.transpose(jnp.broadcast_to(row[:, t * HD:(t + 1) * HD], (HD, HD)))
             for t in range(n // HD)]
    col = jnp.concatenate(tiles, axis=0)
    return jnp.concatenate([col] * (n // HD), axis=1)


def _mlstm_unit(k, vt, kq, qs, d_row, dmax_row, b_row, g_row, bl_row, gmax_row, st, m_row, mask,
                carried):
    logw = jnp.where(mask, _lane_broadcast_column(d_row) + b_row, NEG)
    a = b_row + m_row
    mj = jnp.maximum(a, dmax_row + b_row)
    w = jnp.exp2(logw - mj)
    s = kq * w
    num = _dot(vt, s.astype(BF16))
    den = jnp.sum(s, axis=0, keepdims=True)
    if carried:
        inter = jnp.exp2(a - mj)
        num = num + inter * qs[:HD]
        den = den + inter * qs[HD:HD + 1]
    h = num * (1.0 / jnp.maximum(jnp.abs(den), jnp.exp2(-mj)))
    m_new = jnp.maximum(bl_row + m_row, gmax_row)
    wc = jnp.exp2(g_row - m_new)
    pad_row = lax.broadcasted_iota(jnp.int32, (ST_ROWS - HD, wc.shape[1]), 0)
    v_aug = jnp.concatenate(
        [vt.astype(F32) * wc, jnp.where(pad_row == 0, wc, 0.0)], axis=0).astype(BF16)
    st_new = _dot(v_aug, k)
    if carried:
        decay = jnp.exp2(bl_row + m_row - m_new)
        st_new = decay[:, :HD] * st + st_new
    return h, st_new, m_new


def _mlstm_kernel(*refs, has_init, has_out, rounds, by_sequence):
    if has_init:
        s0_ref, m0_ref = refs[:2]
        refs = refs[2:]
    def split(k_ref, qvt_ref, gr_ref):
        return k_ref, qvt_ref.at[0:D_MLSTM, :], qvt_ref.at[D_MLSTM:, :], gr_ref

    if by_sequence:
        fwd_in = bwd_in = split(*refs[:3])
        refs = refs[3:]
    else:
        fwd_in, bwd_in = split(*refs[:3]), split(*refs[3:6])
        refs = refs[6:]
    if by_sequence:
        hf_ref = hb_ref = refs[0]
        refs = refs[1:]
    else:
        hf_ref, hb_ref = refs[:2]
        refs = refs[2:]
    if has_out:
        co_ref, no_ref, mo_ref = refs[:3]
        refs = refs[3:]
    s_ref, m_ref = refs
    c = pl.program_id(1)
    nc = pl.num_programs(1)
    carried = not by_sequence

    if carried:
        @pl.when(c == 0)
        def _():
            if has_init:
                s_ref[...] = s0_ref[0]
                m_ref[...] = m0_ref[0]
            else:
                s_ref[...] = jnp.zeros_like(s_ref)
                m_ref[...] = jnp.zeros_like(m_ref)

    si = lax.broadcasted_iota(jnp.int32, (SCAN, SCAN), 0)
    ji = lax.broadcasted_iota(jnp.int32, (SCAN, SCAN), 1)
    dirs = ((fwd_in, hf_ref, si <= ji), (bwd_in, hb_ref, si >= ji))

    def chunk_cols(d, r):
        pos = rounds - 1 - r if (d == 1 and not by_sequence) else r
        return slice(pos * SCAN, (pos + 1) * SCAN)

    kq = {}
    for r in range(rounds):
        for d, ((k_ref, qt_ref, _, _), _, _) in enumerate(dirs):
            cs = chunk_cols(d, r)
            for hd in range(HEADS):
                hs = slice(hd * HD, (hd + 1) * HD)
                kq[r, d, hd] = _dot(k_ref[cs, hs], qt_ref[hs, cs])

    m_cur = None
    if carried:
        m_all = m_ref[...]
        m_cur = [m_all[u:u + 1] for u in range(N_UNIT)]
    zero_row = jnp.zeros((1, SCAN), F32)
    for r in range(rounds):
        slot = r * N_UNIT if by_sequence else 0
        for d, ((k_ref, qt_ref, vt_ref, gr_ref), h_ref, mask) in enumerate(dirs):
            cs = chunk_cols(d, r)
            for hd in range(HEADS):
                u = d * HEADS + hd
                row = d * 2 * HEADS + HEADS + hd
                hs = slice(hd * HD, (hd + 1) * HD)
                st = s_ref[slot + u] if carried else None
                qs = _dot(st.astype(BF16), qt_ref[hs, cs]) if carried else None
                h, st_new, m_new = _mlstm_unit(
                    k_ref[cs, hs], vt_ref[hs, cs], kq[r, d, hd], qs,
                    *(gr_ref[sec * N_GATE + row:sec * N_GATE + row + 1, cs]
                      for sec in (4, 5, 0, 1, 2, 3)),
                    st, m_cur[u] if carried else zero_row, mask, carried)
                h_ref[hs, cs] = h_ref[hs, cs] + h if (by_sequence and d == 1) else h
                s_ref[slot + u] = st_new
                if carried:
                    m_cur[u] = m_new
                else:
                    m_ref[slot + u:slot + u + 1, :] = m_new
    if carried:
        m_ref[...] = jnp.concatenate(m_cur, axis=0)

    if has_out:
        @pl.when(c == nc - 1)
        def _():
            for j in range(s_ref.shape[0]):
                q, u = divmod(j, N_UNIT)
                co_ref[q, u] = jnp.transpose(s_ref[j, 0:HD, :])
                no_ref[q, u:u + 1, :] = s_ref[j, HD:HD + 1, :]
                mo_ref[q, u:u + 1, :] = m_ref[j:j + 1, :]


def _mlstm(ak, qvt, gr, batch, rounds, s0=None, m0=None, want_state=False):
    n = ak.shape[0]
    nc = n // batch // SCAN
    by_sequence = nc == 1
    width = rounds * SCAN
    has_init = s0 is not None
    if by_sequence:
        assert batch % rounds == 0 and not has_init
        grid = (batch // rounds, 1)
        slots = rounds
        fwd = bwd = lambda b, c: b
    else:
        assert nc % rounds == 0 and not want_state
        steps = nc // rounds
        grid = (batch, steps)
        slots = 1
        fwd = lambda b, c: b * steps + c
        bwd = lambda b, c: b * steps + steps - 1 - c

    def specs(ix):
        return [
            pl.BlockSpec((width, D_MLSTM), lambda b, c: (ix(b, c), 1)),
            pl.BlockSpec((2 * D_MLSTM, width), lambda b, c: (0, ix(b, c))),
            pl.BlockSpec((GR_ROWS, width), lambda b, c: (0, ix(b, c))),
        ]

    in_specs = specs(fwd) if by_sequence else specs(fwd) + specs(bwd)
    args = [ak, qvt, gr] if by_sequence else [ak, qvt, gr, ak, qvt, gr]
    if has_init:
        in_specs = [
            pl.BlockSpec((1, N_UNIT, ST_ROWS, HD), lambda b, c: (b, 0, 0, 0)),
            pl.BlockSpec((1, N_UNIT, SCAN), lambda b, c: (b, 0, 0)),
        ] + in_specs
        args = [s0, m0] + args
    n_h = 1 if by_sequence else N_DIR
    out_shape = [jax.ShapeDtypeStruct((D_MLSTM, n), F32)] * n_h
    out_specs = [pl.BlockSpec((D_MLSTM, width), lambda b, c, ix=ix: (0, ix(b, c)))
                 for ix in (fwd, bwd)[:n_h]]
    if want_state:
        out_shape += [
            jax.ShapeDtypeStruct((batch, N_UNIT, HD, HD), F32),
            jax.ShapeDtypeStruct((batch, N_UNIT, HD), F32),
            jax.ShapeDtypeStruct((batch, N_UNIT, SCAN), F32),
        ]
        out_specs += [
            pl.BlockSpec((slots, N_UNIT, HD, HD), lambda b, c: (b, 0, 0, 0)),
            pl.BlockSpec((slots, N_UNIT, HD), lambda b, c: (b, 0, 0)),
            pl.BlockSpec((slots, N_UNIT, SCAN), lambda b, c: (b, 0, 0)),
        ]
    outs = pl.pallas_call(
        functools.partial(_mlstm_kernel, has_init=has_init, has_out=want_state, rounds=rounds,
                          by_sequence=by_sequence),
        out_shape=out_shape,
        grid=grid,
        in_specs=in_specs,
        out_specs=out_specs,
        scratch_shapes=[pltpu.VMEM((slots * N_UNIT, ST_ROWS, HD), F32),
                        pltpu.VMEM((slots * N_UNIT, SCAN), F32)],
        compiler_params=pltpu.CompilerParams(
            dimension_semantics=("arbitrary", "arbitrary"), vmem_limit_bytes=VMEM_LIMIT_BYTES),
        name="mlstm",
    )(*args)
    return outs[:n_h], outs[n_h:]


def _outffn_mix(x, a, hst, o, mod, c, x1_ref, h2_ref, before_out_weight=None):
    ga1 = mod[:, 0:D_MODEL]
    sh2 = mod[:, D_MODEL:2 * D_MODEL]
    sc2 = mod[:, 2 * D_MODEL:3 * D_MODEL]
    hs = jnp.transpose(hst)
    sig = _sigmoid(o)
    parts = []
    for hd in range(HEADS):
        sl = slice(hd * HD, (hd + 1) * HD)
        parts.append(_rms(hs[:, sl], c["gh"][:, sl]) * sig[:, sl])
    b_out = jnp.concatenate(parts, axis=-1).astype(BF16)
    if before_out_weight is not None:
        before_out_weight()
    mix = _dot(a, c["wout"][0:D_GMLP, :]) + _dot(b_out, c["wout"][D_GMLP:, :])
    x1 = x + ga1 * mix
    x1_ref[...] = x1
    h2_ref[...] = (_rms(x1, c["g2"][...] * (1.0 + sc2)) + sh2).astype(BF16)


def _outffn_up(h2_ref, c, f_ref, before_gate_weight=None):
    h2 = h2_ref[...]
    u = _dot(h2, c["w1"][...])
    if before_gate_weight is not None:
        before_gate_weight()
    g = _dot(h2, c["w3"][...])
    f_ref[...] = (u * _sigmoid(u) * g).astype(BF16)


def _outffn_down(f_ref, x1_ref, mod, c, y_ref):
    ga2 = mod[:, 3 * D_MODEL:4 * D_MODEL]
    x2 = x1_ref[...] + ga2 * _dot(f_ref[...], c["w2"][...])
    y_ref[...] = _rms(x2, c["gf"][...])


def _outffn_kernel(x_ref, a_ref, o_ref, mod_ref, gh_ref, wout_hbm,
                   g2_ref, w1_hbm, w3_hbm, w2_hbm, gf_ref, *refs, n_h, blocks_per_mod, mod_row0):
    ht_refs = refs[:n_h]
    y_ref, x1_s, h2_s, f_s, wout_s, w1_s, w3_s, w2_s, w_sem = refs[n_h:]
    tb = x_ref.shape[0] // 2
    i = pl.program_id(0)
    mod_in = mod_row0 + jnp.minimum(i, pl.num_programs(0) - 2) // blocks_per_mod
    mod_out = mod_row0 + jnp.maximum(i - 1, 0) // blocks_per_mod
    c = {"gh": gh_ref, "wout": wout_s, "g2": g2_ref, "w1": w1_s, "w3": w3_s, "w2": w2_s,
         "gf": gf_ref}
    weight_copy = {
        name: pltpu.make_async_copy(src, dst, w_sem.at[j])
        for j, (name, src, dst) in enumerate((("wout", wout_hbm, wout_s), ("w1", w1_hbm, w1_s),
                                              ("w3", w3_hbm, w3_s), ("w2", w2_hbm, w2_s)))}

    def tick(half, mix, up, down):
        rows = slice(half * tb, (half + 1) * tb)
        cur, oth = half, 1 - half
        first_tick = mix and not up and not down
        second_tick = mix and up and not down
        if first_tick:
            for name in ("wout", "w1", "w3", "w2"):
                weight_copy[name].start()
        if down:
            _outffn_down(f_s.at[cur], x1_s.at[cur], mod_ref[pl.ds(mod_out, 1), :], c,
                         y_ref.at[rows, :])
        if mix:
            hst = functools.reduce(lambda s, t: s + t, [ref[:, rows] for ref in ht_refs])
            _outffn_mix(x_ref[rows, :], a_ref[rows, :], hst, o_ref[rows, :],
                        mod_ref[pl.ds(mod_in, 1), :], c, x1_s.at[cur], h2_s.at[cur],
                        weight_copy["wout"].wait if first_tick else None)
        if second_tick:
            weight_copy["w1"].wait()
        if up:
            _outffn_up(h2_s.at[oth], c, f_s.at[oth],
                       weight_copy["w3"].wait if second_tick else None)
        if second_tick:
            weight_copy["w2"].wait()

    _pipeline_step(tick)


def _outffn(x, ak, hts, o, mod, mod_row0, rows_per_mod, p):
    n = x.shape[0]
    tb = TOKEN_BLOCK
    tb2 = 2 * tb
    assert n % tb2 == 0 and rows_per_mod % tb2 == 0
    nb = n // tb2
    d_ff = p["w1"].shape[1]
    blk_in = lambda i: jnp.minimum(i, nb - 1)
    blk_out = lambda i: jnp.maximum(i - 1, 0)
    tok = lambda w: pl.BlockSpec((tb2, w), lambda i: (blk_in(i), 0))
    tok_t = lambda h: pl.BlockSpec((h, tb2), lambda i: (0, blk_in(i)))
    weight_hbm = pl.BlockSpec(memory_space=pl.ANY)
    in_specs = [
        tok(D_MODEL), tok(D_GMLP), tok(D_MLSTM),
        _const_spec(mod.shape),
        _const_spec((1, D_MLSTM)),
        weight_hbm,
        _const_spec((1, D_MODEL)),
        weight_hbm,
        weight_hbm,
        weight_hbm,
        _const_spec((1, D_MODEL)),
    ] + [tok_t(D_MLSTM)] * len(hts)
    scratch_shapes = [
        pltpu.VMEM((2, tb, D_MODEL), F32),
        pltpu.VMEM((2, tb, D_MODEL), BF16),
        pltpu.VMEM((2, tb, d_ff), BF16),
        pltpu.VMEM(p["wout"].shape, BF16),
        pltpu.VMEM(p["w1"].shape, BF16),
        pltpu.VMEM(p["w3"].shape, BF16),
        pltpu.VMEM(p["w2"].shape, BF16),
        pltpu.SemaphoreType.DMA((4,)),
    ]
    return pl.pallas_call(
        functools.partial(_outffn_kernel, n_h=len(hts), blocks_per_mod=rows_per_mod // tb2,
                          mod_row0=mod_row0),
        out_shape=jax.ShapeDtypeStruct((n, D_MODEL), F32),
        grid=(nb + 1,),
        in_specs=in_specs,
        out_specs=pl.BlockSpec((tb2, D_MODEL), lambda i: (blk_out(i), 0)),
        scratch_shapes=scratch_shapes,
        compiler_params=pltpu.CompilerParams(
            dimension_semantics=("arbitrary",), vmem_limit_bytes=VMEM_LIMIT_BYTES),
        name="outffn",
    )(x, ak, o, mod, p["gh"], p["wout"], p["g2"], p["w1"], p["w3"], p["w2"], p["gf"], *hts)


def _layer_params(l, g_norm1, b_gate, w_s, b_s, g_v, conv_w, conv_b, g_h, g_norm2, g_final):
    return {
        "g1": g_norm1[l][None, :],
        "bgt": b_gate[l][:, None],
        "cw": conv_w[l], "cb": conv_b[l][None, :],
        "ws": w_s[l].astype(BF16),
        "bs": jnp.repeat(b_s[l].T, HD, axis=1),
        "gv": g_v[l].reshape(1, D_GMLP),
        "gh": g_h[l].reshape(1, D_MLSTM),
        "g2": g_norm2[l][None, :],
        "gf": g_final[None, :],
    }


_LATE_WEIGHTS = ("wout", "w1", "w3", "w2")


def _trunk_front(x, mod_row0, rows_per_mod, seg, p, late=None):
    cast, ada_tail = ((), None) if late is None else (
        tuple(late[0][name] for name in _LATE_WEIGHTS), late[1])
    ak, qvt, o, gr, *side = _inproj(x, p["mod_head"], mod_row0, rows_per_mod, seg, p, cast,
                                    ada_tail)
    if late is not None:
        p = dict(p, **dict(zip(_LATE_WEIGHTS, side[:-1])))
        p["mod_tail"] = side[-1]
    return (ak, qvt, o), gr, p


def _trunk_back(x, front, gr, mod_row0, rows_per_mod, batch, rounds, p, s0=None, m0=None,
                want_state=False):
    ak, qvt, o = front
    hts, state = _mlstm(ak, qvt, gr, batch, rounds, s0, m0, want_state)
    y = _outffn(x, ak, hts, o, p["mod_tail"], mod_row0, rows_per_mod, p)
    return y, state


def kernel(x_prompt, x_sample, state_C, state_n, state_m, c, c_ctx, w_ada, b_ada, g_norm1, w_in,
           b_gate, w_s, b_s, g_v, conv_w, conv_b, g_h, w_out, g_norm2, w1, w3, w2, g_final):
    bp, tp, d = x_prompt.shape
    bs_, ts, _ = x_sample.shape
    depth = w_in.shape[0]
    assert depth == 1, "final norm is fused into the layer's last kernel"
    xp = x_prompt.reshape(bp * tp, d)
    xs = x_sample.reshape(bs_ * ts, d)

    cs = jnp.zeros((8, d), F32).at[0].set(c_ctx).at[1:1 + bs_].set(c)
    new_c, new_n, new_m = [], [], []
    for l in range(depth):
        p = _layer_params(l, g_norm1, b_gate, w_s, b_s, g_v, conv_w, conv_b, g_h, g_norm2, g_final)
        late_f32 = {"wout": w_out[l], "w1": w1[l], "w3": w3[l], "w2": w2[l]}
        p["mod_head"], p["wit"] = _ada(cs, w_ada[l], b_ada[l][None, :], 2 * d, w_in[l].T)
        ada_tail = (cs, w_ada[l], b_ada[l][None, :], 2 * d)

        front_lat, gr_lat, p = _trunk_front(xs, 1, ts, GRID_W, p, (late_f32, ada_tail))
        front_ctx, gr_ctx, _ = _trunk_front(xp, 0, bp * tp, tp, p)

        xp, (c_ctx_out, n_ctx_out, m_ctx_out) = _trunk_back(
            xp, front_ctx, gr_ctx, 0, bp * tp, bp, CTX_SEQS_PER_STEP, p, want_state=True)
        new_c.append(c_ctx_out.reshape(bp, N_DIR, HEADS, HD, HD))
        new_n.append(n_ctx_out.reshape(bp, N_DIR, HEADS, HD))
        new_m.append((m_ctx_out[..., 0] * (1.0 / LOG2E)).reshape(bp, N_DIR, HEADS))

        s0 = jnp.concatenate(
            [jnp.swapaxes(state_C[:, l], -1, -2), state_n[:, l][..., None, :],
             jnp.zeros((bs_, N_DIR, HEADS, ST_ROWS - HD - 1, HD), F32)],
            axis=-2).reshape(bs_, N_UNIT, ST_ROWS, HD)
        m0 = jnp.broadcast_to((state_m[:, l] * LOG2E).reshape(bs_, N_UNIT, 1),
                              (bs_, N_UNIT, SCAN))
        xs, _ = _trunk_back(xs, front_lat, gr_lat, 1, ts, bs_, LAT_CHUNKS_PER_STEP, p, s0=s0, m0=m0)

    return (xp.reshape(bp, tp, d), xs.reshape(bs_, ts, d),
            jnp.stack(new_c, axis=1), jnp.stack(new_n, axis=1), jnp.stack(new_m, axis=1))
```
